```python
import math
import jax, jax.numpy as jnp
from jax import lax
import numpy as np

D_MODEL = 1024
BATCH = 16
SEQ = 2048
DEPTH = 4

GRID_W = 64
CTX_LEN = 256
HEAD_DIM = 64
ROPE_BASE = 10000.0
EPS = 1e-6
Q_BLOCK = 128
N_MOD = 6
D_FF = 4 * D_MODEL
N_EVEN = (DEPTH + 1) // 2
N_ODD = DEPTH // 2

GQA_Q_HEADS = 12
GQA_KV_HEADS = 4
GQA_GROUP = GQA_Q_HEADS // GQA_KV_HEADS
GQA_Q_W = GQA_Q_HEADS * HEAD_DIM
GQA_KV_W = GQA_KV_HEADS * HEAD_DIM
SSM_WIDTH = 256
SSM_GROUP = 16
SSM_GROUPS = SSM_WIDTH // SSM_GROUP
SSM_STATE = 64
SSM_DT_MIN = 0.001
SSM_DT_MAX = 0.1
EVEN_IN_W = GQA_Q_W + 2 * GQA_KV_W + SSM_WIDTH
EVEN_MIX_W = GQA_Q_W + SSM_WIDTH

MLA_HEADS = 8
MLA_Q_RANK = 512
MLA_KV_RANK = 256
MLA_NOPE = 64
MLA_ROPE = 32
MLA_QK = MLA_NOPE + MLA_ROPE
MLA_V = 64
NA_HEADS = 8
NA_W = NA_HEADS * HEAD_DIM
NA_WIN_R = 8
NA_WIN_C = 16
ODD_IN_W = MLA_Q_RANK + MLA_KV_RANK + MLA_ROPE + 3 * NA_W
ODD_MIX_W = MLA_HEADS * MLA_V + NA_W

kernel_name = 'hybrid_diffusion_gqa_s5_mla_natten'


def rms_norm(x, g):
    xf = x.astype(jnp.float32)
    y = xf * lax.rsqrt(jnp.mean(jnp.square(xf), axis=-1, keepdims=True) + EPS)
    return (y * g.astype(jnp.float32)).astype(x.dtype)


def modulate(x, g, shift, scale):
    return rms_norm(x, g) * (1 + scale) + shift


def axial_rope_tables(n_tokens, rot_dim):
    t = jnp.arange(n_tokens)
    rows = (t // GRID_W).astype(jnp.float32)
    cols = (t % GRID_W).astype(jnp.float32)
    axis_dim = rot_dim // 2
    freqs = ROPE_BASE ** (-jnp.arange(0, axis_dim, 2, dtype=jnp.float32) / axis_dim)
    ang_r = rows[:, None] * freqs
    ang_c = cols[:, None] * freqs
    ang = jnp.concatenate([ang_r, ang_r, ang_c, ang_c], axis=-1)
    return jnp.cos(ang), jnp.sin(ang)


def apply_axial_rope(x, cos, sin):
    xf = x.astype(jnp.float32)
    x1, x2, x3, x4 = jnp.split(xf, 4, axis=-1)
    rot = jnp.concatenate([-x2, x1, -x4, x3], axis=-1)
    return (xf * cos[:, None, :] + rot * sin[:, None, :]).astype(x.dtype)


def blocked_attention(q, k, v, scale):
    b, n = q.shape[:2]
    nb = n // Q_BLOCK
    qb = jnp.swapaxes(q.reshape((b, nb, Q_BLOCK) + q.shape[2:]), 0, 1)

    def one_block(qblk):
        s = jnp.einsum('bqkgd,bskd->bkgqs', qblk, k, preferred_element_type=jnp.float32) * scale
        p = jax.nn.softmax(s, axis=-1).astype(v.dtype)
        return jnp.einsum('bkgqs,bskd->bqkgd', p, v)

    out = lax.map(one_block, qb)
    return jnp.swapaxes(out, 0, 1).reshape((b, n) + out.shape[3:])


def s5_discretize(lam_re, lam_im, log_dt, b_re, b_im):
    f32 = jnp.float32
    lam_re, lam_im = lam_re.astype(f32), lam_im.astype(f32)
    dt = jnp.exp(log_dt.astype(f32))[:, None]
    mag = jnp.exp(lam_re * dt)
    a_re = mag * jnp.cos(lam_im * dt)
    a_im = mag * jnp.sin(lam_im * dt)
    den = jnp.square(lam_re) + jnp.square(lam_im)
    f_re = ((a_re - 1.0) * lam_re + a_im * lam_im) / den
    f_im = (a_im * lam_re - (a_re - 1.0) * lam_im) / den
    b_re, b_im = b_re.astype(f32), b_im.astype(f32)
    bb_re = f_re[..., None] * b_re - f_im[..., None] * b_im
    bb_im = f_re[..., None] * b_im + f_im[..., None] * b_re
    return a_re, a_im, bb_re, bb_im


def _complex_affine_combine(e1, e2):
    a1r, a1i, b1r, b1i = e1
    a2r, a2i, b2r, b2i = e2
    return (a2r * a1r - a2i * a1i, a2r * a1i + a2i * a1r,
            a2r * b1r - a2i * b1i + b2r, a2r * b1i + a2i * b1r + b2i)


def s5_scan(u, a_re, a_im, bb_re, bb_im, h0=None):
    n = u.shape[1]
    bu_re = jnp.einsum('bngp,gsp->bngs', u, bb_re)
    bu_im = jnp.einsum('bngp,gsp->bngs', u, bb_im)
    shape = (1, n) + a_re.shape
    ar = jnp.broadcast_to(a_re, shape)
    ai = jnp.broadcast_to(a_im, shape)
    p_re, p_im, h_re, h_im = lax.associative_scan(_complex_affine_combine, (ar, ai, bu_re, bu_im), axis=1)
    if h0 is not None:
        h0_re, h0_im = h0[0][:, None], h0[1][:, None]
        h_re = h_re + p_re * h0_re - p_im * h0_im
        h_im = h_im + p_re * h0_im + p_im * h0_re
    return h_re, h_im


def s5_readout(h_re, h_im, c_re, c_im):
    return jnp.einsum('bngs,gps->bngp', h_re, c_re) - jnp.einsum('bngs,gps->bngp', h_im, c_im)


def maybe_flip(t, d):
    return t[:, ::-1] if d == 1 else t


def s5_mixer(u_lat, u_ctx, lam_re, lam_im, log_dt, b_re, b_im, c_re, c_im, d_skip, w_glu, b_glu, need_ctx):
    f32 = jnp.float32
    out_dtype = u_lat.dtype

    def grouped(u):
        return u.astype(f32).reshape(u.shape[:2] + (SSM_GROUPS, SSM_GROUP))

    ul, uc = grouped(u_lat), grouped(u_ctx)
    d_g = d_skip.astype(f32).reshape(SSM_GROUPS, SSM_GROUP)
    y_lat = d_g * ul
    y_ctx = d_g * uc if need_ctx else None
    for d in range(2):
        a_re, a_im, bb_re, bb_im = s5_discretize(lam_re[d], lam_im[d], log_dt[d], b_re[d], b_im[d])
        cr, ci = c_re[d].astype(f32), c_im[d].astype(f32)
        hc_re, hc_im = s5_scan(maybe_flip(uc, d), a_re, a_im, bb_re, bb_im)
        hl_re, hl_im = s5_scan(maybe_flip(ul, d), a_re, a_im, bb_re, bb_im, h0=(hc_re[:, -1], hc_im[:, -1]))
        y_lat = y_lat + maybe_flip(s5_readout(hl_re, hl_im, cr, ci), d)
        if need_ctx:
            y_ctx = y_ctx + maybe_flip(s5_readout(hc_re, hc_im, cr, ci), d)
    wg, bg = w_glu.astype(f32), b_glu.astype(f32)

    def glu(y):
        y = jax.nn.gelu(y.reshape(y.shape[:2] + (SSM_WIDTH,)))
        return (y * jax.nn.sigmoid(y @ wg + bg)).astype(out_dtype)

    return glu(y_lat), (glu(y_ctx) if need_ctx else None)


def even_mixer(h_lat, h_ctx, w_in, w_out, g_q, g_k, lam_re, lam_im, log_dt, b_re, b_im, c_re, c_im,
               d_skip, w_glu, b_glu, need_ctx):
    b, n, _ = h_lat.shape
    n_ctx = h_ctx.shape[1]
    cos, sin = axial_rope_tables(n, HEAD_DIM)
    cuts = [GQA_Q_W, GQA_Q_W + GQA_KV_W, GQA_Q_W + 2 * GQA_KV_W]

    def project(h):
        t = h.shape[:2]
        q, k, v, u = jnp.split(h @ w_in, cuts, axis=-1)
        q = rms_norm(q.reshape(t + (GQA_Q_HEADS, HEAD_DIM)), g_q)
        k = rms_norm(k.reshape(t + (GQA_KV_HEADS, HEAD_DIM)), g_k)
        v = v.reshape(t + (GQA_KV_HEADS, HEAD_DIM))
        return q, k, v, u

    def grouped_q(q):
        return q.reshape(q.shape[:2] + (GQA_KV_HEADS, GQA_GROUP, HEAD_DIM))

    q_l, k_l, v_l, u_l = project(h_lat)
    q_c, k_c, v_c, u_c = project(h_ctx)
    q_l = apply_axial_rope(q_l, cos, sin)
    k_l = apply_axial_rope(k_l, cos, sin)
    scale = HEAD_DIM ** -0.5
    att_l = blocked_attention(grouped_q(q_l), jnp.concatenate([k_c, k_l], axis=1),
                              jnp.concatenate([v_c, v_l], axis=1), scale).reshape(b, n, GQA_Q_W)
    ssm_l, ssm_c = s5_mixer(u_l, u_c, lam_re, lam_im, log_dt, b_re, b_im, c_re, c_im, d_skip, w_glu, b_glu, need_ctx)
    out_l = jnp.concatenate([att_l, ssm_l], axis=-1) @ w_out
    out_c = None
    if need_ctx:
        att_c = blocked_attention(grouped_q(q_c), k_c, v_c, scale).reshape(b, n_ctx, GQA_Q_W)
        out_c = jnp.concatenate([att_c, ssm_c], axis=-1) @ w_out
    return out_l, out_c


def neighbourhood_attention(q, k, v, k_ctx, v_ctx, rpb, rows):
    b, n, h, dh = q.shape
    wr = min(NA_WIN_R, rows)
    n_loc = wr * NA_WIN_C
    scale = dh ** -0.5
    kg = k.reshape(b, rows, GRID_W, h, dh)
    vg = v.reshape(b, rows, GRID_W, h, dh)
    qg = jnp.swapaxes(q.reshape(b, rows, GRID_W, h, dh), 0, 1)
    row_start = jnp.clip(jnp.arange(rows) - wr // 2, 0, rows - wr)
    q_col = jnp.arange(GRID_W)
    col_idx = jnp.clip(q_col - NA_WIN_C // 2, 0, GRID_W - NA_WIN_C)[:, None] + jnp.arange(NA_WIN_C)
    col_bias = rpb[:, :, col_idx - q_col[:, None] + (NA_WIN_C - 1)]

    def one_row(args):
        r, q_row = args
        rs = row_start[r]
        kw = lax.dynamic_slice_in_dim(kg, rs, wr, axis=1)[:, :, col_idx]
        vw = lax.dynamic_slice_in_dim(vg, rs, wr, axis=1)[:, :, col_idx]
        bias = jnp.transpose(col_bias[:, rs + jnp.arange(wr) - r + (NA_WIN_R - 1)], (0, 2, 1, 3))
        s_loc = jnp.einsum('bqhd,bjqchd->bhqjc', q_row, kw, preferred_element_type=jnp.float32) * scale + bias
        s_ctx = jnp.einsum('bqhd,bshd->bhqs', q_row, k_ctx, preferred_element_type=jnp.float32) * scale
        s = jnp.concatenate([s_loc.reshape(b, h, GRID_W, n_loc), s_ctx], axis=-1)
        p = jax.nn.softmax(s, axis=-1).astype(v.dtype)
        p_loc = p[..., :n_loc].reshape(b, h, GRID_W, wr, NA_WIN_C)
        return (jnp.einsum('bhqjc,bjqchd->bqhd', p_loc, vw)
                + jnp.einsum('bhqs,bshd->bqhd', p[..., n_loc:], v_ctx))

    out = lax.map(one_row, (jnp.arange(rows), qg))
    return jnp.swapaxes(out, 0, 1).reshape(b, n, h * dh)


def odd_mixer(h_lat, h_ctx, w_in, w_out, g_cq, g_ckv, w_uq, w_ukv, g_mq, g_mk, g_nq, g_nk, rpb, need_ctx):
    b, n, _ = h_lat.shape
    n_ctx = h_ctx.shape[1]
    rows = n // GRID_W
    cos, sin = axial_rope_tables(n, MLA_ROPE)
    c1 = MLA_Q_RANK
    c2 = c1 + MLA_KV_RANK
    c3 = c2 + MLA_ROPE
    cuts = [c1, c2, c3, c3 + NA_W, c3 + 2 * NA_W]

    def project(h):
        t = h.shape[:2]
        cq, ckv, kr, nq, nk, nv = jnp.split(h @ w_in, cuts, axis=-1)
        q = (rms_norm(cq, g_cq) @ w_uq).reshape(t + (MLA_HEADS, MLA_QK))
        kv = (rms_norm(ckv, g_ckv) @ w_ukv).reshape(t + (MLA_HEADS, MLA_NOPE + MLA_V))
        k = jnp.concatenate([kv[..., :MLA_NOPE],
                             jnp.broadcast_to(kr[:, :, None, :], t + (MLA_HEADS, MLA_ROPE))], axis=-1)
        mla = (rms_norm(q, g_mq), rms_norm(k, g_mk), kv[..., MLA_NOPE:])
        na = (rms_norm(nq.reshape(t + (NA_HEADS, HEAD_DIM)), g_nq),
              rms_norm(nk.reshape(t + (NA_HEADS, HEAD_DIM)), g_nk),
              nv.reshape(t + (NA_HEADS, HEAD_DIM)))
        return mla, na

    def rope_tail(t):
        return jnp.concatenate([t[..., :MLA_NOPE], apply_axial_rope(t[..., MLA_NOPE:], cos, sin)], axis=-1)

    (mq_l, mk_l, mv_l), (nq_l, nk_l, nv_l) = project(h_lat)
    (mq_c, mk_c, mv_c), (nq_c, nk_c, nv_c) = project(h_ctx)
    mq_l, mk_l = rope_tail(mq_l), rope_tail(mk_l)
    mla_scale = MLA_QK ** -0.5
    mla_l = blocked_attention(mq_l[:, :, :, None], jnp.concatenate([mk_c, mk_l], axis=1),
                              jnp.concatenate([mv_c, mv_l], axis=1), mla_scale).reshape(b, n, MLA_HEADS * MLA_V)
    na_l = neighbourhood_attention(nq_l, nk_l, nv_l, nk_c, nv_c, rpb, rows)
    out_l = jnp.concatenate([mla_l, na_l], axis=-1) @ w_out
    out_c = None
    if need_ctx:
        mla_c = blocked_attention(mq_c[:, :, :, None], mk_c, mv_c, mla_scale).reshape(b, n_ctx, MLA_HEADS * MLA_V)
        na_c = blocked_attention(nq_c[:, :, :, None], nk_c, nv_c, HEAD_DIM ** -0.5).reshape(b, n_ctx, NA_W)
        out_c = jnp.concatenate([mla_c, na_c], axis=-1) @ w_out
    return out_l, out_c


def sq_relu_mlp(h, w1, w2):
    return jnp.square(jax.nn.relu(h @ w1)) @ w2


def _fwd_setup_inputs(seed: int = 0) -> dict:
    key = jax.random.key(seed)
    keys = iter(jax.random.split(key, 40))
    f32 = jnp.float32

    def normal(shape, scale):
        return scale * jax.random.normal(next(keys), shape, f32)

    def gain(shape):
        return 1.0 + 0.01 * jax.random.normal(next(keys), shape, f32)

    ne, no = N_EVEN, N_ODD
    G, N, P = SSM_GROUPS, SSM_STATE, SSM_GROUP
    return {
        'x': normal((BATCH, SEQ, D_MODEL), 1.0),
        'c': normal((BATCH, D_MODEL), 1.0),
        'ctx': normal((BATCH, CTX_LEN, D_MODEL), 1.0),
        'c_ctx': normal((D_MODEL,), 1.0),
        'w_mod': normal((DEPTH, D_MODEL, N_MOD * D_MODEL), D_MODEL ** -0.5),
        'b_mod': normal((DEPTH, N_MOD * D_MODEL), 0.01),
        'g_norm1': gain((DEPTH, D_MODEL)),
        'g_norm2': gain((DEPTH, D_MODEL)),
        'w_ff1': normal((DEPTH, D_MODEL, D_FF), D_MODEL ** -0.5),
        'w_ff2': normal((DEPTH, D_FF, D_MODEL), D_FF ** -0.5),
        'e_w_in': normal((ne, D_MODEL, EVEN_IN_W), D_MODEL ** -0.5),
        'e_w_out': normal((ne, EVEN_MIX_W, D_MODEL), EVEN_MIX_W ** -0.5),
        'e_g_q': gain((ne, HEAD_DIM)),
        'e_g_k': gain((ne, HEAD_DIM)),
        'ssm_lam_re': -0.5 + normal((ne, 2, G, N), 0.01),
        'ssm_lam_im': jnp.pi * jnp.arange(N, dtype=f32) + normal((ne, 2, G, N), 0.01),
        'ssm_log_dt': jax.random.uniform(next(keys), (ne, 2, G), f32, math.log(SSM_DT_MIN), math.log(SSM_DT_MAX)),
        'ssm_b_re': normal((ne, 2, G, N, P), (2 * P) ** -0.5),
        'ssm_b_im': normal((ne, 2, G, N, P), (2 * P) ** -0.5),
        'ssm_c_re': normal((ne, 2, G, P, N), 0.5),
        'ssm_c_im': normal((ne, 2, G, P, N), 0.5),
        'ssm_d': normal((ne, SSM_WIDTH), 0.5),
        'ssm_w_glu': normal((ne, SSM_WIDTH, SSM_WIDTH), SSM_WIDTH ** -0.5),
        'ssm_b_glu': normal((ne, SSM_WIDTH), 0.01),
        'o_w_in': normal((no, D_MODEL, ODD_IN_W), D_MODEL ** -0.5),
        'o_w_out': normal((no, ODD_MIX_W, D_MODEL), ODD_MIX_W ** -0.5),
        'mla_g_cq': gain((no, MLA_Q_RANK)),
        'mla_g_ckv': gain((no, MLA_KV_RANK)),
        'mla_w_uq': normal((no, MLA_Q_RANK, MLA_HEADS * MLA_QK), MLA_Q_RANK ** -0.5),
        'mla_w_ukv': normal((no, MLA_KV_RANK, MLA_HEADS * (MLA_NOPE + MLA_V)), MLA_KV_RANK ** -0.5),
        'mla_g_q': gain((no, MLA_QK)),
        'mla_g_k': gain((no, MLA_QK)),
        'na_g_q': gain((no, HEAD_DIM)),
        'na_g_k': gain((no, HEAD_DIM)),
        'na_rpb': normal((no, NA_HEADS, 2 * NA_WIN_R - 1, 2 * NA_WIN_C - 1), 0.1),
    }


def _fwd_reference(x, c, ctx, c_ctx, w_mod, b_mod, g_norm1, g_norm2, w_ff1, w_ff2,
              e_w_in, e_w_out, e_g_q, e_g_k, ssm_lam_re, ssm_lam_im, ssm_log_dt, ssm_b_re, ssm_b_im,
              ssm_c_re, ssm_c_im, ssm_d, ssm_w_glu, ssm_b_glu,
              o_w_in, o_w_out, mla_g_cq, mla_g_ckv, mla_w_uq, mla_w_ukv, mla_g_q, mla_g_k,
              na_g_q, na_g_k, na_rpb):
    cond_lat = jax.nn.silu(c)[:, None, :]
    cond_ctx = jax.nn.silu(c_ctx)[None, None, :]
    xc = ctx
    for i in range(DEPTH):
        need_ctx = i < DEPTH - 1
        j = i // 2
        m_lat = jnp.split(cond_lat @ w_mod[i] + b_mod[i], N_MOD, axis=-1)
        m_ctx = jnp.split(cond_ctx @ w_mod[i] + b_mod[i], N_MOD, axis=-1)
        a_lat = modulate(x, g_norm1[i], m_lat[0], m_lat[1])
        a_ctx = modulate(xc, g_norm1[i], m_ctx[0], m_ctx[1])
        if i % 2 == 0:
            o_lat, o_ctx = even_mixer(a_lat, a_ctx, e_w_in[j], e_w_out[j], e_g_q[j], e_g_k[j],
                                      ssm_lam_re[j], ssm_lam_im[j], ssm_log_dt[j], ssm_b_re[j], ssm_b_im[j],
                                      ssm_c_re[j], ssm_c_im[j], ssm_d[j], ssm_w_glu[j], ssm_b_glu[j], need_ctx)
        else:
            o_lat, o_ctx = odd_mixer(a_lat, a_ctx, o_w_in[j], o_w_out[j], mla_g_cq[j], mla_g_ckv[j],
                                     mla_w_uq[j], mla_w_ukv[j], mla_g_q[j], mla_g_k[j],
                                     na_g_q[j], na_g_k[j], na_rpb[j], need_ctx)
        x = x + m_lat[2] * o_lat
        x = x + m_lat[5] * sq_relu_mlp(modulate(x, g_norm2[i], m_lat[3], m_lat[4]), w_ff1[i], w_ff2[i])
        if need_ctx:
            xc = xc + m_ctx[2] * o_ctx
            xc = xc + m_ctx[5] * sq_relu_mlp(modulate(xc, g_norm2[i], m_ctx[3], m_ctx[4]), w_ff1[i], w_ff2[i])
    return x


import jax as _jax
import jax.numpy as _jnp

TWIN_FORMAT = 'train_step'
FWD_PARAMS = ['x', 'c', 'ctx', 'c_ctx', 'w_mod', 'b_mod', 'g_norm1', 'g_norm2', 'w_ff1', 'w_ff2', 'e_w_in', 'e_w_out', 'e_g_q', 'e_g_k', 'ssm_lam_re', 'ssm_lam_im', 'ssm_log_dt', 'ssm_b_re', 'ssm_b_im', 'ssm_c_re', 'ssm_c_im', 'ssm_d', 'ssm_w_glu', 'ssm_b_glu', 'o_w_in', 'o_w_out', 'mla_g_cq', 'mla_g_ckv', 'mla_w_uq', 'mla_w_ukv', 'mla_g_q', 'mla_g_k', 'na_g_q', 'na_g_k', 'na_rpb']
TWIN_WEIGHTS = ['c_ctx', 'w_mod', 'b_mod', 'g_norm1', 'g_norm2', 'w_ff1', 'w_ff2', 'e_w_in', 'e_w_out', 'e_g_q', 'e_g_k', 'ssm_lam_re', 'ssm_lam_im', 'ssm_log_dt', 'ssm_b_re', 'ssm_b_im', 'ssm_c_re', 'ssm_c_im', 'ssm_d', 'ssm_w_glu', 'ssm_b_glu', 'o_w_in', 'o_w_out', 'mla_g_cq', 'mla_g_ckv', 'mla_w_uq', 'mla_w_ukv', 'mla_g_q', 'mla_g_k', 'na_g_q', 'na_g_k', 'na_rpb']
TWIN_DIFF_INPUT = 'x'
TWIN_INPUTS = ['x', 'c', 'ctx', 'c_ctx', 'w_mod', 'b_mod', 'g_norm1', 'g_norm2', 'w_ff1', 'w_ff2', 'e_w_in', 'e_w_out', 'e_g_q', 'e_g_k', 'ssm_lam_re', 'ssm_lam_im', 'ssm_log_dt', 'ssm_b_re', 'ssm_b_im', 'ssm_c_re', 'ssm_c_im', 'ssm_d', 'ssm_w_glu', 'ssm_b_glu', 'o_w_in', 'o_w_out', 'mla_g_cq', 'mla_g_ckv', 'mla_w_uq', 'mla_w_ukv', 'mla_g_q', 'mla_g_k', 'na_g_q', 'na_g_k', 'na_rpb', 'loss_target', 'm_c_ctx', 'm_w_mod', 'm_b_mod', 'm_g_norm1', 'm_g_norm2', 'm_w_ff1', 'm_w_ff2', 'm_e_w_in', 'm_e_w_out', 'm_e_g_q', 'm_e_g_k', 'm_ssm_lam_re', 'm_ssm_lam_im', 'm_ssm_log_dt', 'm_ssm_b_re', 'm_ssm_b_im', 'm_ssm_c_re', 'm_ssm_c_im', 'm_ssm_d', 'm_ssm_w_glu', 'm_ssm_b_glu', 'm_o_w_in', 'm_o_w_out', 'm_mla_g_cq', 'm_mla_g_ckv', 'm_mla_w_uq', 'm_mla_w_ukv', 'm_mla_g_q', 'm_mla_g_k', 'm_na_g_q', 'm_na_g_k', 'm_na_rpb', 'v_c_ctx', 'v_w_mod', 'v_b_mod', 'v_g_norm1', 'v_g_norm2', 'v_w_ff1', 'v_w_ff2', 'v_e_w_in', 'v_e_w_out', 'v_e_g_q', 'v_e_g_k', 'v_ssm_lam_re', 'v_ssm_lam_im', 'v_ssm_log_dt', 'v_ssm_b_re', 'v_ssm_b_im', 'v_ssm_c_re', 'v_ssm_c_im', 'v_ssm_d', 'v_ssm_w_glu', 'v_ssm_b_glu', 'v_o_w_in', 'v_o_w_out', 'v_mla_g_cq', 'v_mla_g_ckv', 'v_mla_w_uq', 'v_mla_w_ukv', 'v_mla_g_q', 'v_mla_g_k', 'v_na_g_q', 'v_na_g_k', 'v_na_rpb']
TWIN_OUTPUTS = ['loss', 'grad_x', 'grad_c_ctx', 'grad_w_mod', 'grad_b_mod', 'grad_g_norm1', 'grad_g_norm2', 'grad_w_ff1', 'grad_w_ff2', 'grad_e_w_in', 'grad_e_w_out', 'grad_e_g_q', 'grad_e_g_k', 'grad_ssm_lam_re', 'grad_ssm_lam_im', 'grad_ssm_log_dt', 'grad_ssm_b_re', 'grad_ssm_b_im', 'grad_ssm_c_re', 'grad_ssm_c_im', 'grad_ssm_d', 'grad_ssm_w_glu', 'grad_ssm_b_glu', 'grad_o_w_in', 'grad_o_w_out', 'grad_mla_g_cq', 'grad_mla_g_ckv', 'grad_mla_w_uq', 'grad_mla_w_ukv', 'grad_mla_g_q', 'grad_mla_g_k', 'grad_na_g_q', 'grad_na_g_k', 'grad_na_rpb', 'delta_c_ctx', 'delta_w_mod', 'delta_b_mod', 'delta_g_norm1', 'delta_g_norm2', 'delta_w_ff1', 'delta_w_ff2', 'delta_e_w_in', 'delta_e_w_out', 'delta_e_g_q', 'delta_e_g_k', 'delta_ssm_lam_re', 'delta_ssm_lam_im', 'delta_ssm_log_dt', 'delta_ssm_b_re', 'delta_ssm_b_im', 'delta_ssm_c_re', 'delta_ssm_c_im', 'delta_ssm_d', 'delta_ssm_w_glu', 'delta_ssm_b_glu', 'delta_o_w_in', 'delta_o_w_out', 'delta_mla_g_cq', 'delta_mla_g_ckv', 'delta_mla_w_uq', 'delta_mla_w_ukv', 'delta_mla_g_q', 'delta_mla_g_k', 'delta_na_g_q', 'delta_na_g_k', 'delta_na_rpb', 'new_m_c_ctx', 'new_m_w_mod', 'new_m_b_mod', 'new_m_g_norm1', 'new_m_g_norm2', 'new_m_w_ff1', 'new_m_w_ff2', 'new_m_e_w_in', 'new_m_e_w_out', 'new_m_e_g_q', 'new_m_e_g_k', 'new_m_ssm_lam_re', 'new_m_ssm_lam_im', 'new_m_ssm_log_dt', 'new_m_ssm_b_re', 'new_m_ssm_b_im', 'new_m_ssm_c_re', 'new_m_ssm_c_im', 'new_m_ssm_d', 'new_m_ssm_w_glu', 'new_m_ssm_b_glu', 'new_m_o_w_in', 'new_m_o_w_out', 'new_m_mla_g_cq', 'new_m_mla_g_ckv', 'new_m_mla_w_uq', 'new_m_mla_w_ukv', 'new_m_mla_g_q', 'new_m_mla_g_k', 'new_m_na_g_q', 'new_m_na_g_k', 'new_m_na_rpb', 'new_v_c_ctx', 'new_v_w_mod', 'new_v_b_mod', 'new_v_g_norm1', 'new_v_g_norm2', 'new_v_w_ff1', 'new_v_w_ff2', 'new_v_e_w_in', 'new_v_e_w_out', 'new_v_e_g_q', 'new_v_e_g_k', 'new_v_ssm_lam_re', 'new_v_ssm_lam_im', 'new_v_ssm_log_dt', 'new_v_ssm_b_re', 'new_v_ssm_b_im', 'new_v_ssm_c_re', 'new_v_ssm_c_im', 'new_v_ssm_d', 'new_v_ssm_w_glu', 'new_v_ssm_b_glu', 'new_v_o_w_in', 'new_v_o_w_out', 'new_v_mla_g_cq', 'new_v_mla_g_ckv', 'new_v_mla_w_uq', 'new_v_mla_w_ukv', 'new_v_mla_g_q', 'new_v_mla_g_k', 'new_v_na_g_q', 'new_v_na_g_k', 'new_v_na_rpb']
TWIN_LEAF_KINDS = {'loss': 'loss', 'grad_x': 'grad_x', 'grad_c_ctx': 'grad_w', 'grad_w_mod': 'grad_w', 'grad_b_mod': 'grad_w', 'grad_g_norm1': 'grad_w', 'grad_g_norm2': 'grad_w', 'grad_w_ff1': 'grad_w', 'grad_w_ff2': 'grad_w', 'grad_e_w_in': 'grad_w', 'grad_e_w_out': 'grad_w', 'grad_e_g_q': 'grad_w', 'grad_e_g_k': 'grad_w', 'grad_ssm_lam_re': 'grad_w', 'grad_ssm_lam_im': 'grad_w', 'grad_ssm_log_dt': 'grad_w', 'grad_ssm_b_re': 'grad_w', 'grad_ssm_b_im': 'grad_w', 'grad_ssm_c_re': 'grad_w', 'grad_ssm_c_im': 'grad_w', 'grad_ssm_d': 'grad_w', 'grad_ssm_w_glu': 'grad_w', 'grad_ssm_b_glu': 'grad_w', 'grad_o_w_in': 'grad_w', 'grad_o_w_out': 'grad_w', 'grad_mla_g_cq': 'grad_w', 'grad_mla_g_ckv': 'grad_w', 'grad_mla_w_uq': 'grad_w', 'grad_mla_w_ukv': 'grad_w', 'grad_mla_g_q': 'grad_w', 'grad_mla_g_k': 'grad_w', 'grad_na_g_q': 'grad_w', 'grad_na_g_k': 'grad_w', 'grad_na_rpb': 'grad_w', 'delta_c_ctx': 'delta_w', 'delta_w_mod': 'delta_w', 'delta_b_mod': 'delta_w', 'delta_g_norm1': 'delta_w', 'delta_g_norm2': 'delta_w', 'delta_w_ff1': 'delta_w', 'delta_w_ff2': 'delta_w', 'delta_e_w_in': 'delta_w', 'delta_e_w_out': 'delta_w', 'delta_e_g_q': 'delta_w', 'delta_e_g_k': 'delta_w', 'delta_ssm_lam_re': 'delta_w', 'delta_ssm_lam_im': 'delta_w', 'delta_ssm_log_dt': 'delta_w', 'delta_ssm_b_re': 'delta_w', 'delta_ssm_b_im': 'delta_w', 'delta_ssm_c_re': 'delta_w', 'delta_ssm_c_im': 'delta_w', 'delta_ssm_d': 'delta_w', 'delta_ssm_w_glu': 'delta_w', 'delta_ssm_b_glu': 'delta_w', 'delta_o_w_in': 'delta_w', 'delta_o_w_out': 'delta_w', 'delta_mla_g_cq': 'delta_w', 'delta_mla_g_ckv': 'delta_w', 'delta_mla_w_uq': 'delta_w', 'delta_mla_w_ukv': 'delta_w', 'delta_mla_g_q': 'delta_w', 'delta_mla_g_k': 'delta_w', 'delta_na_g_q': 'delta_w', 'delta_na_g_k': 'delta_w', 'delta_na_rpb': 'delta_w', 'new_m_c_ctx': 'new_m', 'new_m_w_mod': 'new_m', 'new_m_b_mod': 'new_m', 'new_m_g_norm1': 'new_m', 'new_m_g_norm2': 'new_m', 'new_m_w_ff1': 'new_m', 'new_m_w_ff2': 'new_m', 'new_m_e_w_in': 'new_m', 'new_m_e_w_out': 'new_m', 'new_m_e_g_q': 'new_m', 'new_m_e_g_k': 'new_m', 'new_m_ssm_lam_re': 'new_m', 'new_m_ssm_lam_im': 'new_m', 'new_m_ssm_log_dt': 'new_m', 'new_m_ssm_b_re': 'new_m', 'new_m_ssm_b_im': 'new_m', 'new_m_ssm_c_re': 'new_m', 'new_m_ssm_c_im': 'new_m', 'new_m_ssm_d': 'new_m', 'new_m_ssm_w_glu': 'new_m', 'new_m_ssm_b_glu': 'new_m', 'new_m_o_w_in': 'new_m', 'new_m_o_w_out': 'new_m', 'new_m_mla_g_cq': 'new_m', 'new_m_mla_g_ckv': 'new_m', 'new_m_mla_w_uq': 'new_m', 'new_m_mla_w_ukv': 'new_m', 'new_m_mla_g_q': 'new_m', 'new_m_mla_g_k': 'new_m', 'new_m_na_g_q': 'new_m', 'new_m_na_g_k': 'new_m', 'new_m_na_rpb': 'new_m', 'new_v_c_ctx': 'new_v', 'new_v_w_mod': 'new_v', 'new_v_b_mod': 'new_v', 'new_v_g_norm1': 'new_v', 'new_v_g_norm2': 'new_v', 'new_v_w_ff1': 'new_v', 'new_v_w_ff2': 'new_v', 'new_v_e_w_in': 'new_v', 'new_v_e_w_out': 'new_v', 'new_v_e_g_q': 'new_v', 'new_v_e_g_k': 'new_v', 'new_v_ssm_lam_re': 'new_v', 'new_v_ssm_lam_im': 'new_v', 'new_v_ssm_log_dt': 'new_v', 'new_v_ssm_b_re': 'new_v', 'new_v_ssm_b_im': 'new_v', 'new_v_ssm_c_re': 'new_v', 'new_v_ssm_c_im': 'new_v', 'new_v_ssm_d': 'new_v', 'new_v_ssm_w_glu': 'new_v', 'new_v_ssm_b_glu': 'new_v', 'new_v_o_w_in': 'new_v', 'new_v_o_w_out': 'new_v', 'new_v_mla_g_cq': 'new_v', 'new_v_mla_g_ckv': 'new_v', 'new_v_mla_w_uq': 'new_v', 'new_v_mla_w_ukv': 'new_v', 'new_v_mla_g_q': 'new_v', 'new_v_mla_g_k': 'new_v', 'new_v_na_g_q': 'new_v', 'new_v_na_g_k': 'new_v', 'new_v_na_rpb': 'new_v'}


def _forward(args):
    return _fwd_reference(*[args[k] for k in FWD_PARAMS])


def _output_shape():
    out = _jax.eval_shape(lambda: _forward(_fwd_setup_inputs(0)))
    return out.shape, out.dtype

N_MICROBATCH = 1
ADAM_LR = 0.001
ADAM_B1 = 0.9
ADAM_B2 = 0.999
ADAM_EPS = 1e-08
ADAM_WD = 0.01
ADAM_STEP = 10
PER_EXAMPLE_BATCH_AXIS = {'x': 0, 'c': 0, 'ctx': 0, 'loss_target': 0}
SHARED_INPUTS = []
_WEIGHT_DTYPES = {'c_ctx': _jnp.float32, 'w_mod': _jnp.float32, 'b_mod': _jnp.float32, 'g_norm1': _jnp.float32, 'g_norm2': _jnp.float32, 'w_ff1': _jnp.float32, 'w_ff2': _jnp.float32, 'e_w_in': _jnp.float32, 'e_w_out': _jnp.float32, 'e_g_q': _jnp.float32, 'e_g_k': _jnp.float32, 'ssm_lam_re': _jnp.float32, 'ssm_lam_im': _jnp.float32, 'ssm_log_dt': _jnp.float32, 'ssm_b_re': _jnp.float32, 'ssm_b_im': _jnp.float32, 'ssm_c_re': _jnp.float32, 'ssm_c_im': _jnp.float32, 'ssm_d': _jnp.float32, 'ssm_w_glu': _jnp.float32, 'ssm_b_glu': _jnp.float32, 'o_w_in': _jnp.float32, 'o_w_out': _jnp.float32, 'mla_g_cq': _jnp.float32, 'mla_g_ckv': _jnp.float32, 'mla_w_uq': _jnp.float32, 'mla_w_ukv': _jnp.float32, 'mla_g_q': _jnp.float32, 'mla_g_k': _jnp.float32, 'na_g_q': _jnp.float32, 'na_g_k': _jnp.float32, 'na_rpb': _jnp.float32}
MOMENT_SCALE = {'c_ctx': 1.414371e+01, 'w_mod': 3.124323e+01, 'b_mod': 5.835544e+01, 'g_norm1': 1.268426e+01, 'g_norm2': 1.103945e+02, 'w_ff1': 2.006566e+01, 'w_ff2': 4.282019e+01, 'e_w_in': 1.988164e+01, 'e_w_out': 2.245283e+01, 'e_g_q': 4.381644e+00, 'e_g_k': 4.543328e+00, 'ssm_lam_re': 8.256228e+00, 'ssm_lam_im': 5.851678e+00, 'ssm_log_dt': 6.749905e+01, 'ssm_b_re': 3.338147e+00, 'ssm_b_im': 3.804732e+00, 'ssm_c_re': 1.305256e+00, 'ssm_c_im': 1.498311e+00, 'ssm_d': 1.211512e+01, 'ssm_w_glu': 4.006399e+00, 'ssm_b_glu': 6.489610e+00, 'o_w_in': 1.224737e+01, 'o_w_out': 1.838467e+01, 'mla_g_cq': 1.640806e+00, 'mla_g_ckv': 2.440757e+01, 'mla_w_uq': 1.337600e+00, 'mla_w_ukv': 1.101339e+01, 'mla_g_q': 3.197201e+00, 'mla_g_k': 3.149509e+00, 'na_g_q': 7.859380e+00, 'na_g_k': 7.872866e+00, 'na_rpb': 1.598115e-01}


def _to_microbatches(a, axis):
    t = _jnp.moveaxis(a, axis, 0)
    t = t.reshape((N_MICROBATCH, t.shape[0] // N_MICROBATCH) + t.shape[1:])
    return _jnp.moveaxis(t, 1, axis + 1)


def setup_inputs(seed: int = 0) -> dict:
    inp = _fwd_setup_inputs(seed)
    key = _jax.random.fold_in(_jax.random.key(seed), 7919)
    shape, _ = _output_shape()
    out = dict(inp)
    out["loss_target"] = _jax.random.normal(_jax.random.fold_in(key, 0), shape, _jnp.float32)
    for i, name in enumerate(TWIN_WEIGHTS):
        w = inp[name].astype(_jnp.float32)
        if MOMENT_SCALE is None:
            s = _jnp.sqrt(_jnp.mean(_jnp.square(w)) + 1e-30)
        else:
            s = MOMENT_SCALE[name]
        km, kv = _jax.random.split(_jax.random.fold_in(key, i + 1))
        out[name] = w
        out["m_" + name] = s * _jax.random.normal(km, w.shape, _jnp.float32)
        out["v_" + name] = (s * s) * _jax.random.uniform(kv, w.shape, _jnp.float32, 0.5, 1.5)
    if N_MICROBATCH > 1:
        for name, axis in PER_EXAMPLE_BATCH_AXIS.items():
            out[name] = _to_microbatches(out[name], axis)
    return {'x': out['x'], 'c': out['c'], 'ctx': out['ctx'], 'c_ctx': out['c_ctx'], 'w_mod': out['w_mod'], 'b_mod': out['b_mod'], 'g_norm1': out['g_norm1'], 'g_norm2': out['g_norm2'], 'w_ff1': out['w_ff1'], 'w_ff2': out['w_ff2'], 'e_w_in': out['e_w_in'], 'e_w_out': out['e_w_out'], 'e_g_q': out['e_g_q'], 'e_g_k': out['e_g_k'], 'ssm_lam_re': out['ssm_lam_re'], 'ssm_lam_im': out['ssm_lam_im'], 'ssm_log_dt': out['ssm_log_dt'], 'ssm_b_re': out['ssm_b_re'], 'ssm_b_im': out['ssm_b_im'], 'ssm_c_re': out['ssm_c_re'], 'ssm_c_im': out['ssm_c_im'], 'ssm_d': out['ssm_d'], 'ssm_w_glu': out['ssm_w_glu'], 'ssm_b_glu': out['ssm_b_glu'], 'o_w_in': out['o_w_in'], 'o_w_out': out['o_w_out'], 'mla_g_cq': out['mla_g_cq'], 'mla_g_ckv': out['mla_g_ckv'], 'mla_w_uq': out['mla_w_uq'], 'mla_w_ukv': out['mla_w_ukv'], 'mla_g_q': out['mla_g_q'], 'mla_g_k': out['mla_g_k'], 'na_g_q': out['na_g_q'], 'na_g_k': out['na_g_k'], 'na_rpb': out['na_rpb'], 'loss_target': out['loss_target'], 'm_c_ctx': out['m_c_ctx'], 'm_w_mod': out['m_w_mod'], 'm_b_mod': out['m_b_mod'], 'm_g_norm1': out['m_g_norm1'], 'm_g_norm2': out['m_g_norm2'], 'm_w_ff1': out['m_w_ff1'], 'm_w_ff2': out['m_w_ff2'], 'm_e_w_in': out['m_e_w_in'], 'm_e_w_out': out['m_e_w_out'], 'm_e_g_q': out['m_e_g_q'], 'm_e_g_k': out['m_e_g_k'], 'm_ssm_lam_re': out['m_ssm_lam_re'], 'm_ssm_lam_im': out['m_ssm_lam_im'], 'm_ssm_log_dt': out['m_ssm_log_dt'], 'm_ssm_b_re': out['m_ssm_b_re'], 'm_ssm_b_im': out['m_ssm_b_im'], 'm_ssm_c_re': out['m_ssm_c_re'], 'm_ssm_c_im': out['m_ssm_c_im'], 'm_ssm_d': out['m_ssm_d'], 'm_ssm_w_glu': out['m_ssm_w_glu'], 'm_ssm_b_glu': out['m_ssm_b_glu'], 'm_o_w_in': out['m_o_w_in'], 'm_o_w_out': out['m_o_w_out'], 'm_mla_g_cq': out['m_mla_g_cq'], 'm_mla_g_ckv': out['m_mla_g_ckv'], 'm_mla_w_uq': out['m_mla_w_uq'], 'm_mla_w_ukv': out['m_mla_w_ukv'], 'm_mla_g_q': out['m_mla_g_q'], 'm_mla_g_k': out['m_mla_g_k'], 'm_na_g_q': out['m_na_g_q'], 'm_na_g_k': out['m_na_g_k'], 'm_na_rpb': out['m_na_rpb'], 'v_c_ctx': out['v_c_ctx'], 'v_w_mod': out['v_w_mod'], 'v_b_mod': out['v_b_mod'], 'v_g_norm1': out['v_g_norm1'], 'v_g_norm2': out['v_g_norm2'], 'v_w_ff1': out['v_w_ff1'], 'v_w_ff2': out['v_w_ff2'], 'v_e_w_in': out['v_e_w_in'], 'v_e_w_out': out['v_e_w_out'], 'v_e_g_q': out['v_e_g_q'], 'v_e_g_k': out['v_e_g_k'], 'v_ssm_lam_re': out['v_ssm_lam_re'], 'v_ssm_lam_im': out['v_ssm_lam_im'], 'v_ssm_log_dt': out['v_ssm_log_dt'], 'v_ssm_b_re': out['v_ssm_b_re'], 'v_ssm_b_im': out['v_ssm_b_im'], 'v_ssm_c_re': out['v_ssm_c_re'], 'v_ssm_c_im': out['v_ssm_c_im'], 'v_ssm_d': out['v_ssm_d'], 'v_ssm_w_glu': out['v_ssm_w_glu'], 'v_ssm_b_glu': out['v_ssm_b_glu'], 'v_o_w_in': out['v_o_w_in'], 'v_o_w_out': out['v_o_w_out'], 'v_mla_g_cq': out['v_mla_g_cq'], 'v_mla_g_ckv': out['v_mla_g_ckv'], 'v_mla_w_uq': out['v_mla_w_uq'], 'v_mla_w_ukv': out['v_mla_w_ukv'], 'v_mla_g_q': out['v_mla_g_q'], 'v_mla_g_k': out['v_mla_g_k'], 'v_na_g_q': out['v_na_g_q'], 'v_na_g_k': out['v_na_g_k'], 'v_na_rpb': out['v_na_rpb']}


def _loss(weights, diff, rest, loss_target):
    with _jax.named_scope("forward"):
        args = {**rest, TWIN_DIFF_INPUT: diff, **{k: w.astype(_WEIGHT_DTYPES[k]) for k, w in weights.items()}}
        y = _forward(args)
    with _jax.named_scope("loss_head"):
        err = _jnp.square(y.astype(_jnp.float32) - loss_target)
        return 0.5 * _jnp.sum(_jnp.mean(err, axis=-1)) if err.ndim else 0.5 * err


def _adamw(w, g, m, v):
    m = ADAM_B1 * m + (1.0 - ADAM_B1) * g
    v = ADAM_B2 * v + (1.0 - ADAM_B2) * _jnp.square(g)
    m_hat = m / (1.0 - ADAM_B1 ** ADAM_STEP)
    v_hat = v / (1.0 - ADAM_B2 ** ADAM_STEP)
    delta = -ADAM_LR * (m_hat / (_jnp.sqrt(v_hat) + ADAM_EPS) + ADAM_WD * w)
    return delta, m, v


def reference(x, c, ctx, c_ctx, w_mod, b_mod, g_norm1, g_norm2, w_ff1, w_ff2, e_w_in, e_w_out, e_g_q, e_g_k, ssm_lam_re, ssm_lam_im, ssm_log_dt, ssm_b_re, ssm_b_im, ssm_c_re, ssm_c_im, ssm_d, ssm_w_glu, ssm_b_glu, o_w_in, o_w_out, mla_g_cq, mla_g_ckv, mla_w_uq, mla_w_ukv, mla_g_q, mla_g_k, na_g_q, na_g_k, na_rpb, loss_target, m_c_ctx, m_w_mod, m_b_mod, m_g_norm1, m_g_norm2, m_w_ff1, m_w_ff2, m_e_w_in, m_e_w_out, m_e_g_q, m_e_g_k, m_ssm_lam_re, m_ssm_lam_im, m_ssm_log_dt, m_ssm_b_re, m_ssm_b_im, m_ssm_c_re, m_ssm_c_im, m_ssm_d, m_ssm_w_glu, m_ssm_b_glu, m_o_w_in, m_o_w_out, m_mla_g_cq, m_mla_g_ckv, m_mla_w_uq, m_mla_w_ukv, m_mla_g_q, m_mla_g_k, m_na_g_q, m_na_g_k, m_na_rpb, v_c_ctx, v_w_mod, v_b_mod, v_g_norm1, v_g_norm2, v_w_ff1, v_w_ff2, v_e_w_in, v_e_w_out, v_e_g_q, v_e_g_k, v_ssm_lam_re, v_ssm_lam_im, v_ssm_log_dt, v_ssm_b_re, v_ssm_b_im, v_ssm_c_re, v_ssm_c_im, v_ssm_d, v_ssm_w_glu, v_ssm_b_glu, v_o_w_in, v_o_w_out, v_mla_g_cq, v_mla_g_ckv, v_mla_w_uq, v_mla_w_ukv, v_mla_g_q, v_mla_g_k, v_na_g_q, v_na_g_k, v_na_rpb):
    given = dict(x=x, c=c, ctx=ctx, c_ctx=c_ctx, w_mod=w_mod, b_mod=b_mod, g_norm1=g_norm1, g_norm2=g_norm2, w_ff1=w_ff1, w_ff2=w_ff2, e_w_in=e_w_in, e_w_out=e_w_out, e_g_q=e_g_q, e_g_k=e_g_k, ssm_lam_re=ssm_lam_re, ssm_lam_im=ssm_lam_im, ssm_log_dt=ssm_log_dt, ssm_b_re=ssm_b_re, ssm_b_im=ssm_b_im, ssm_c_re=ssm_c_re, ssm_c_im=ssm_c_im, ssm_d=ssm_d, ssm_w_glu=ssm_w_glu, ssm_b_glu=ssm_b_glu, o_w_in=o_w_in, o_w_out=o_w_out, mla_g_cq=mla_g_cq, mla_g_ckv=mla_g_ckv, mla_w_uq=mla_w_uq, mla_w_ukv=mla_w_ukv, mla_g_q=mla_g_q, mla_g_k=mla_g_k, na_g_q=na_g_q, na_g_k=na_g_k, na_rpb=na_rpb, loss_target=loss_target, m_c_ctx=m_c_ctx, m_w_mod=m_w_mod, m_b_mod=m_b_mod, m_g_norm1=m_g_norm1, m_g_norm2=m_g_norm2, m_w_ff1=m_w_ff1, m_w_ff2=m_w_ff2, m_e_w_in=m_e_w_in, m_e_w_out=m_e_w_out, m_e_g_q=m_e_g_q, m_e_g_k=m_e_g_k, m_ssm_lam_re=m_ssm_lam_re, m_ssm_lam_im=m_ssm_lam_im, m_ssm_log_dt=m_ssm_log_dt, m_ssm_b_re=m_ssm_b_re, m_ssm_b_im=m_ssm_b_im, m_ssm_c_re=m_ssm_c_re, m_ssm_c_im=m_ssm_c_im, m_ssm_d=m_ssm_d, m_ssm_w_glu=m_ssm_w_glu, m_ssm_b_glu=m_ssm_b_glu, m_o_w_in=m_o_w_in, m_o_w_out=m_o_w_out, m_mla_g_cq=m_mla_g_cq, m_mla_g_ckv=m_mla_g_ckv, m_mla_w_uq=m_mla_w_uq, m_mla_w_ukv=m_mla_w_ukv, m_mla_g_q=m_mla_g_q, m_mla_g_k=m_mla_g_k, m_na_g_q=m_na_g_q, m_na_g_k=m_na_g_k, m_na_rpb=m_na_rpb, v_c_ctx=v_c_ctx, v_w_mod=v_w_mod, v_b_mod=v_b_mod, v_g_norm1=v_g_norm1, v_g_norm2=v_g_norm2, v_w_ff1=v_w_ff1, v_w_ff2=v_w_ff2, v_e_w_in=v_e_w_in, v_e_w_out=v_e_w_out, v_e_g_q=v_e_g_q, v_e_g_k=v_e_g_k, v_ssm_lam_re=v_ssm_lam_re, v_ssm_lam_im=v_ssm_lam_im, v_ssm_log_dt=v_ssm_log_dt, v_ssm_b_re=v_ssm_b_re, v_ssm_b_im=v_ssm_b_im, v_ssm_c_re=v_ssm_c_re, v_ssm_c_im=v_ssm_c_im, v_ssm_d=v_ssm_d, v_ssm_w_glu=v_ssm_w_glu, v_ssm_b_glu=v_ssm_b_glu, v_o_w_in=v_o_w_in, v_o_w_out=v_o_w_out, v_mla_g_cq=v_mla_g_cq, v_mla_g_ckv=v_mla_g_ckv, v_mla_w_uq=v_mla_w_uq, v_mla_w_ukv=v_mla_w_ukv, v_mla_g_q=v_mla_g_q, v_mla_g_k=v_mla_g_k, v_na_g_q=v_na_g_q, v_na_g_k=v_na_g_k, v_na_rpb=v_na_rpb)
    weights = {n: given[n] for n in TWIN_WEIGHTS}
    shared = {n: given[n] for n in SHARED_INPUTS}
    per_example = {n: given[n] for n in ['x', 'c', 'ctx']}
    grad_fn = _jax.value_and_grad(_loss, argnums=(0, 1))

    def one_microbatch(ex, loss_target):
        ex = dict(ex)
        diff = ex.pop(TWIN_DIFF_INPUT)
        return grad_fn(weights, diff, {**shared, **ex}, loss_target)

    if N_MICROBATCH == 1:
        loss, (grad_w, grad_x) = one_microbatch(per_example, given["loss_target"])
    else:
        def body(carry, xs):
            loss_sum, grad_sum = carry
            l_k, (gw_k, gx_k) = one_microbatch(xs[0], xs[1])
            with _jax.named_scope("update"):
                return (loss_sum + l_k, _jax.tree.map(_jnp.add, grad_sum, gw_k)), gx_k

        init = (_jnp.zeros((), _jnp.float32), _jax.tree.map(_jnp.zeros_like, weights))
        (loss, grad_w), grad_x = _jax.lax.scan(body, init, (per_example, given["loss_target"]))
    with _jax.named_scope("update"):
        delta_w, new_m, new_v = {}, {}, {}
        for n in TWIN_WEIGHTS:
            delta_w[n], new_m[n], new_v[n] = _adamw(weights[n], grad_w[n], given["m_" + n], given["v_" + n])
    return (loss, grad_x, *[grad_w[n] for n in TWIN_WEIGHTS], *[delta_w[n] for n in TWIN_WEIGHTS],
            *[new_m[n] for n in TWIN_WEIGHTS], *[new_v[n] for n in TWIN_WEIGHTS])
```

```python
import functools
import math

import numpy as np
import jax
import jax.numpy as jnp
from jax import lax
from jax.experimental import pallas as pl
from jax.experimental.pallas import tpu as pltpu

F32 = jnp.float32
BF16 = jnp.bfloat16
HI = lax.Precision.HIGHEST
MESH = pl.DeviceIdType.MESH
ANY = pl.BlockSpec(memory_space=pl.ANY)
VMEM_SPEC = pl.BlockSpec(memory_space=pltpu.VMEM)

GRID_W = 64
HEAD_DIM = 64
ROPE_BASE = 10000.0
EPS = 1e-6
N_MOD = 6
GQA_Q_HEADS, GQA_KV_HEADS = 12, 4
GQA_Q_W, GQA_KV_W = GQA_Q_HEADS * HEAD_DIM, GQA_KV_HEADS * HEAD_DIM
SSM_WIDTH, SSM_GROUP, SSM_STATE = 256, 16, 64
SSM_GROUPS = SSM_WIDTH // SSM_GROUP
SSM_LANES = SSM_GROUPS * SSM_STATE
MLA_HEADS, MLA_Q_RANK, MLA_KV_RANK, MLA_NOPE, MLA_ROPE, MLA_V = 8, 512, 256, 64, 32, 64
MLA_QK = MLA_NOPE + MLA_ROPE
NA_HEADS, NA_WIN_R, NA_WIN_C = 8, 8, 16
NA_W = NA_HEADS * HEAD_DIM
NA_BAND = NA_WIN_R * GRID_W
ODD_IN_W = MLA_Q_RANK + MLA_KV_RANK + MLA_ROPE + 3 * NA_W
ODD_IN_PAD = 2560
ADAM_LR, ADAM_B1, ADAM_B2, ADAM_EPS, ADAM_WD, ADAM_STEP = 0.001, 0.9, 0.999, 1e-08, 0.01, 10
NEG = -1e30
VMEM_LIMIT = 56 * 1024 * 1024
LANE = 128
N_PLANE = 4
N_DEV = 8


def _pick(n, cands):
    for c in cands:
        if n % c == 0:
            return c
    return n


def _params(**kw):
    return pltpu.CompilerParams(vmem_limit_bytes=VMEM_LIMIT, **kw)


def _mm(a, b, *, ta=False, tb=False, a_act=None, epi=None, e=None, exact=False):
    m, kd = (a.shape[1], a.shape[0]) if ta else a.shape
    n = b.shape[0] if tb else b.shape[1]
    tm = _pick(m, (512, 256, 128))
    tn = _pick(n, (512, 256, 128))
    tk = _pick(kd, (1024, 512, 256, 128))
    nk = kd // tk
    dn = (((0 if ta else 1,), (1 if tb else 0,)), ((), ()))

    def body(*refs):
        if epi is None:
            a_ref, b_ref, o_ref = refs
        else:
            a_ref, b_ref, e_ref, o_ref = refs
        k = pl.program_id(2)
        av = a_ref[...]
        if a_act == "relu2":
            av = jnp.square(jnp.maximum(av, 0.0))
        elif a_act == "silu":
            av = av * jax.nn.sigmoid(av)
        bv = b_ref[...]
        if exact:
            p = lax.dot_general(av, bv, dn, precision=HI, preferred_element_type=F32)
        else:
            p = lax.dot_general(av.astype(BF16), bv.astype(BF16), dn, preferred_element_type=F32)

        @pl.when(k == 0)
        def _():
            o_ref[...] = p

        @pl.when(k > 0)
        def _():
            o_ref[...] += p

        if epi == "drelu2":
            @pl.when(k == nk - 1)
            def _():
                o_ref[...] = o_ref[...] * (2.0 * jnp.maximum(e_ref[...], 0.0))

    a_spec = pl.BlockSpec((tk, tm), lambda i, j, k: (k, i)) if ta else pl.BlockSpec((tm, tk), lambda i, j, k: (i, k))
    b_spec = pl.BlockSpec((tn, tk), lambda i, j, k: (j, k)) if tb else pl.BlockSpec((tk, tn), lambda i, j, k: (k, j))
    o_spec = pl.BlockSpec((tm, tn), lambda i, j, k: (i, j))
    ins, specs = [a, b], [a_spec, b_spec]
    if epi is not None:
        ins.append(e)
        specs.append(o_spec)
    name = f"mm_{m}x{kd}x{n}_{int(ta)}{int(tb)}_{a_act}_{epi}_{int(exact)}"
    return pl.pallas_call(
        body, out_shape=jax.ShapeDtypeStruct((m, n), F32), grid=(m // tm, n // tn, nk),
        in_specs=specs, out_specs=o_spec, name=name, compiler_params=_params(),
    )(*ins)


@functools.partial(jax.custom_vjp, nondiff_argnums=(2,))
def _linear(a, w, exact):
    return _mm(a, w, exact=exact)


def _linear_fwd(a, w, exact):
    return _mm(a, w, exact=exact), (a, w)


def _linear_bwd(exact, res, g):
    a, w = res
    return _mm(g, w, tb=True, exact=exact), _mm(a, g, ta=True, exact=exact)


_linear.defvjp(_linear_fwd, _linear_bwd)


def linear(a, w, exact=False):
    return _linear(a, w, exact)


@jax.custom_vjp
def ffn(a, w1, w2):
    return _mm(_mm(a, w1), w2, a_act="relu2")


def _ffn_fwd(a, w1, w2):
    h1 = _mm(a, w1)
    return _mm(h1, w2, a_act="relu2"), (a, w1, w2, h1)


def _ffn_bwd(res, g):
    a, w1, w2, h1 = res
    dh1 = _mm(g, w2, tb=True, epi="drelu2", e=h1)
    dw2 = _mm(h1, g, ta=True, a_act="relu2")
    return _mm(dh1, w1, tb=True), _mm(a, dh1, ta=True), dw2


ffn.defvjp(_ffn_fwd, _ffn_bwd)


def make_rowwise(fn, name, kinds, out_dims, nctx_rows=0, whole_seq=False):
    n_in = len(kinds)
    n_out = len(out_dims)
    diff = [i for i, kd in enumerate(kinds) if kd in ("row", "glob", "seg")]

    def layout(args):
        row0 = args[kinds.index("row")]
        g, s = row0.shape[0], row0.shape[1]
        ts = s if whole_seq else (min(256, nctx_rows) if nctx_rows else _pick(s, (256, 128, 64)))
        nctx = nctx_rows // ts
        return g, s, ts, nctx

    def spec_of(kind, arr, ts, nctx):
        if kind == "row":
            return pl.BlockSpec((None, ts, arr.shape[2]), lambda g, i: (g, i, 0))
        if kind == "tab":
            return pl.BlockSpec((ts, arr.shape[1]), lambda g, i: (i, 0))
        if kind in ("const", "glob"):
            return pl.BlockSpec(arr.shape, lambda g, i: (0, 0))
        return pl.BlockSpec((None, None) + arr.shape[2:], lambda g, i: (g, (i >= nctx).astype(jnp.int32), 0, 0))

    def fwd_call(*args):
        g, s, ts, nctx = layout(args)

        def body(*refs):
            vals = [r[...] for r in refs[:n_in]]
            outs = fn(*vals)
            for o_ref, o in zip(refs[n_in:], outs):
                o_ref[...] = o

        return pl.pallas_call(
            body, out_shape=[jax.ShapeDtypeStruct((g, s, d), F32) for d in out_dims], grid=(g, s // ts),
            in_specs=[spec_of(kd, a, ts, nctx) for kd, a in zip(kinds, args)],
            out_specs=[pl.BlockSpec((None, ts, d), lambda g_, i: (g_, i, 0)) for d in out_dims],
            name=f"{name}_f_{g}x{s}", compiler_params=_params(),
        )(*args)

    def bwd_call(args, cts):
        g, s, ts, nctx = layout(args)

        def body(*refs):
            in_refs, ct_refs, out_refs = refs[:n_in], refs[n_in:n_in + n_out], refs[n_in + n_out:]
            gi, i = pl.program_id(0), pl.program_id(1)
            vals = [r[...] for r in in_refs]

            def f(*dv):
                full = list(vals)
                for idx, v in zip(diff, dv):
                    full[idx] = v
                return tuple(fn(*full))

            _, vjp = jax.vjp(f, *[vals[idx] for idx in diff])
            grads = vjp(tuple(r[...] for r in ct_refs))
            for idx, o_ref, gr in zip(diff, out_refs, grads):
                if kinds[idx] == "row":
                    o_ref[...] = gr
                    continue
                if kinds[idx] == "glob":
                    first = jnp.logical_and(gi == 0, i == 0)
                else:
                    first = jnp.logical_or(i == 0, i == nctx)

                @pl.when(first)
                def _(o_ref=o_ref, gr=gr):
                    o_ref[...] = gr

                @pl.when(jnp.logical_not(first))
                def _(o_ref=o_ref, gr=gr):
                    o_ref[...] += gr

        in_specs = [spec_of(kd, a, ts, nctx) for kd, a in zip(kinds, args)]
        in_specs += [pl.BlockSpec((None, ts, d), lambda g_, i: (g_, i, 0)) for d in out_dims]
        return pl.pallas_call(
            body, out_shape=[jax.ShapeDtypeStruct(args[idx].shape, F32) for idx in diff], grid=(g, s // ts),
            in_specs=in_specs, out_specs=[spec_of(kinds[idx], args[idx], ts, nctx) for idx in diff],
            name=f"{name}_b_{g}x{s}", compiler_params=_params(),
        )(*args, *cts)

    @jax.custom_vjp
    def op(*args):
        return tuple(fwd_call(*args))

    def op_fwd(*args):
        return tuple(fwd_call(*args)), args

    def op_bwd(args, cts):
        grads = bwd_call(args, cts)
        full = [None] * n_in
        for idx, gr in zip(diff, grads):
            full[idx] = gr
        return tuple(jnp.zeros_like(a) if gfull is None else gfull for a, gfull in zip(args, full))

    op.defvjp(op_fwd, op_bwd)
    return op


def _rms(x):
    return lax.rsqrt(jnp.mean(x * x, axis=-1, keepdims=True) + EPS)


def _fn_modulate(x, g, shift, scale):
    return ((x * _rms(x) * g) * (1.0 + scale) + shift,)


def _fn_gated_add(x, o, gate):
    return (x + gate * o,)


def _fn_norm(x, g):
    return (x * _rms(x) * g,)


def _fn_norm_rope(x, cos, sin, rot, g):
    y = x * _rms(x) * g
    r = jnp.dot(y, rot, precision=HI, preferred_element_type=F32)
    return (y * cos + r * sin,)


def _fn_glu_pre(u, y0, y1, d):
    return (jax.nn.gelu(d * u + y0 + y1),)


def _fn_glu_post(z, t, bg):
    return (z * jax.nn.sigmoid(t + bg),)


def _rope_matrix(dh, start, rot_dim):
    r = np.zeros((dh, dh), np.float32)
    q = rot_dim // 4
    for j in range(rot_dim):
        if (j // q) % 2 == 0:
            r[start + j + q, start + j] = -1.0
        else:
            r[start + j - q, start + j] = 1.0
    return r


def _rope_tables(n_ctx, n_lat, dh, start, rot_dim):
    t = jnp.arange(n_lat)
    rows = (t // GRID_W).astype(F32)
    cols = (t % GRID_W).astype(F32)
    axis_dim = rot_dim // 2
    freqs = ROPE_BASE ** (-jnp.arange(0, axis_dim, 2, dtype=F32) / axis_dim)
    ang_r = rows[:, None] * freqs
    ang_c = cols[:, None] * freqs
    ang = jnp.concatenate([ang_r, ang_r, ang_c, ang_c], axis=-1)
    cos = jnp.concatenate([jnp.ones((n_lat, start), F32), jnp.cos(ang)], axis=-1)
    sin = jnp.concatenate([jnp.zeros((n_lat, start), F32), jnp.sin(ang)], axis=-1)
    cos = jnp.concatenate([jnp.ones((n_ctx, dh), F32), cos], axis=0)
    sin = jnp.concatenate([jnp.zeros((n_ctx, dh), F32), sin], axis=0)
    return cos, sin


def _attn_scores(qv, kv, scale, mask):
    s = lax.dot_general(qv, kv, (((1,), (1,)), ((), ())), preferred_element_type=F32) * scale
    return jnp.where(mask, NEG, s)


def _attn_fwd(q, k, v, group, n_ctx, scale):
    b, h, s, dq = q.shape
    dv = v.shape[-1]
    tq = min(256, n_ctx)
    nc = n_ctx // tq

    def body(q_ref, k_ref, v_ref, o_ref, lse_ref):
        i = pl.program_id(2)
        col = lax.broadcasted_iota(jnp.int32, (tq, s), 1)
        mask = jnp.logical_and(col >= n_ctx, i < nc)
        sc = _attn_scores(q_ref[...].astype(BF16), k_ref[...].astype(BF16), scale, mask)
        m = jnp.max(sc, axis=-1, keepdims=True)
        p = jnp.exp(sc - m)
        l = jnp.sum(p, axis=-1, keepdims=True)
        o = jnp.dot(p.astype(BF16), v_ref[...].astype(BF16), preferred_element_type=F32)
        o_ref[...] = o / l
        lse_ref[...] = m + jnp.log(l)

    return pl.pallas_call(
        body, out_shape=[jax.ShapeDtypeStruct((b, h, s, dv), F32), jax.ShapeDtypeStruct((b, h, s, 1), F32)],
        grid=(b, h, s // tq),
        in_specs=[pl.BlockSpec((None, None, tq, dq), lambda bi, hi, i: (bi, hi, i, 0)),
                  pl.BlockSpec((None, None, s, dq), lambda bi, hi, i: (bi, lax.div(hi, group), 0, 0)),
                  pl.BlockSpec((None, None, s, dv), lambda bi, hi, i: (bi, lax.div(hi, group), 0, 0))],
        out_specs=[pl.BlockSpec((None, None, tq, dv), lambda bi, hi, i: (bi, hi, i, 0)),
                   pl.BlockSpec((None, None, tq, 1), lambda bi, hi, i: (bi, hi, i, 0))],
        name=f"attn_f_{h}x{s}x{dq}", compiler_params=_params(),
    )(q, k, v)


def _attn_dq(q, k, v, o, lse, do, group, n_ctx, scale):
    b, h, s, dq = q.shape
    dv = v.shape[-1]
    tq = min(256, n_ctx)
    nc = n_ctx // tq

    def body(q_ref, k_ref, v_ref, o_ref, lse_ref, do_ref, dq_ref, delta_ref):
        i = pl.program_id(2)
        col = lax.broadcasted_iota(jnp.int32, (tq, s), 1)
        mask = jnp.logical_and(col >= n_ctx, i < nc)
        kv = k_ref[...].astype(BF16)
        sc = _attn_scores(q_ref[...].astype(BF16), kv, scale, mask)
        p = jnp.exp(sc - lse_ref[...])
        dov = do_ref[...]
        delta = jnp.sum(dov * o_ref[...], axis=-1, keepdims=True)
        dp = lax.dot_general(dov.astype(BF16), v_ref[...].astype(BF16), (((1,), (1,)), ((), ())),
                             preferred_element_type=F32)
        ds = p * (dp - delta) * scale
        dq_ref[...] = jnp.dot(ds.astype(BF16), kv, preferred_element_type=F32)
        delta_ref[...] = delta

    qs = lambda d: pl.BlockSpec((None, None, tq, d), lambda bi, hi, i: (bi, hi, i, 0))
    ks = lambda d: pl.BlockSpec((None, None, s, d), lambda bi, hi, i: (bi, lax.div(hi, group), 0, 0))
    return pl.pallas_call(
        body, out_shape=[jax.ShapeDtypeStruct((b, h, s, dq), F32), jax.ShapeDtypeStruct((b, h, s, 1), F32)],
        grid=(b, h, s // tq),
        in_specs=[qs(dq), ks(dq), ks(dv), qs(dv), qs(1), qs(dv)], out_specs=[qs(dq), qs(1)],
        name=f"attn_dq_{h}x{s}x{dq}", compiler_params=_params(),
    )(q, k, v, o, lse, do)


def _attn_dkv(q, k, v, lse, delta, do, group, n_ctx, scale):
    b, h, s, dq = q.shape
    hk = k.shape[1]
    dv = v.shape[-1]
    tk = min(256, n_ctx)
    nc = n_ctx // tk

    def body(q_ref, k_ref, v_ref, lse_ref, delta_ref, do_ref, dk_ref, dv_ref):
        j = pl.program_id(2)
        row = lax.broadcasted_iota(jnp.int32, (s, tk), 0)
        mask = jnp.logical_and(row < n_ctx, j >= nc)
        kv = k_ref[...].astype(BF16)
        vv = v_ref[...].astype(BF16)
        dk = jnp.zeros((tk, dq), F32)
        dvv = jnp.zeros((tk, dv), F32)
        for g in range(group):
            qg = q_ref[g].astype(BF16)
            dog = do_ref[g].astype(BF16)
            sc = _attn_scores(qg, kv, scale, mask)
            p = jnp.exp(sc - lse_ref[g])
            dvv = dvv + lax.dot_general(p.astype(BF16), dog, (((0,), (0,)), ((), ())), preferred_element_type=F32)
            dp = lax.dot_general(dog, vv, (((1,), (1,)), ((), ())), preferred_element_type=F32)
            ds = p * (dp - delta_ref[g]) * scale
            dk = dk + lax.dot_general(ds.astype(BF16), qg, (((0,), (0,)), ((), ())), preferred_element_type=F32)
        dk_ref[...] = dk
        dv_ref[...] = dvv

    gs = lambda d: pl.BlockSpec((None, group, s, d), lambda bi, hi, j: (bi, hi, 0, 0))
    ks = lambda d: pl.BlockSpec((None, None, tk, d), lambda bi, hi, j: (bi, hi, j, 0))
    return pl.pallas_call(
        body, out_shape=[jax.ShapeDtypeStruct((b, hk, s, dq), F32), jax.ShapeDtypeStruct((b, hk, s, dv), F32)],
        grid=(b, hk, s // tk),
        in_specs=[gs(dq), ks(dq), ks(dv), gs(1), gs(1), gs(dv)], out_specs=[ks(dq), ks(dv)],
        name=f"attn_dkv_{h}x{s}x{dq}", compiler_params=_params(),
    )(q, k, v, lse, delta, do)


@functools.partial(jax.custom_vjp, nondiff_argnums=(3, 4, 5))
def attention(q, k, v, group, n_ctx, scale):
    return _attn_fwd(q, k, v, group, n_ctx, scale)[0]


def _attention_fwd(q, k, v, group, n_ctx, scale):
    o, lse = _attn_fwd(q, k, v, group, n_ctx, scale)
    return o, (q, k, v, o, lse)


def _attention_bwd(group, n_ctx, scale, res, do):
    q, k, v, o, lse = res
    dq, delta = _attn_dq(q, k, v, o, lse, do, group, n_ctx, scale)
    dk, dv = _attn_dkv(q, k, v, lse, delta, do, group, n_ctx, scale)
    return dq, dk, dv


attention.defvjp(_attention_fwd, _attention_bwd)


def _na_geometry(i, nc, rows):
    r = i - nc
    rs = jnp.clip(r - NA_WIN_R // 2, 0, rows - NA_WIN_R)
    is_ctx = i < nc
    cls = jnp.where(is_ctx, NA_WIN_R, r - rs)
    return jnp.where(is_ctx, 0, rs), cls


def _na_scores(q_ref, k_ref, bias_ref, n_ctx, start, scale):
    qv = q_ref[...].astype(BF16)
    kc = k_ref[0:n_ctx, :].astype(BF16)
    kb = k_ref[pl.ds(start, NA_BAND), :].astype(BF16)
    dn = (((1,), (1,)), ((), ()))
    s_c = lax.dot_general(qv, kc, dn, preferred_element_type=F32) * scale
    s_l = lax.dot_general(qv, kb, dn, preferred_element_type=F32) * scale + bias_ref[...]
    return qv, kc, kb, s_c, s_l


def _na_fwd(q, k, v, bias, n_ctx):
    b, h, s, dh = q.shape
    nc = n_ctx // GRID_W
    rows = (s - n_ctx) // GRID_W
    scale = dh ** -0.5

    def body(q_ref, k_ref, v_ref, bias_ref, o_ref, lse_ref):
        rs, _ = _na_geometry(pl.program_id(2), nc, rows)
        start = pl.multiple_of(n_ctx + rs * GRID_W, GRID_W)
        _, _, _, s_c, s_l = _na_scores(q_ref, k_ref, bias_ref, n_ctx, start, scale)
        m = jnp.maximum(jnp.max(s_c, axis=-1, keepdims=True), jnp.max(s_l, axis=-1, keepdims=True))
        p_c = jnp.exp(s_c - m)
        p_l = jnp.exp(s_l - m)
        l = jnp.sum(p_c, axis=-1, keepdims=True) + jnp.sum(p_l, axis=-1, keepdims=True)
        o = jnp.dot(p_c.astype(BF16), v_ref[0:n_ctx, :].astype(BF16), preferred_element_type=F32)
        o = o + jnp.dot(p_l.astype(BF16), v_ref[pl.ds(start, NA_BAND), :].astype(BF16), preferred_element_type=F32)
        o_ref[...] = o / l
        lse_ref[...] = m + jnp.log(l)

    qs = lambda d: pl.BlockSpec((None, None, GRID_W, d), lambda bi, hi, i: (bi, hi, i, 0))
    ks = pl.BlockSpec((None, None, s, dh), lambda bi, hi, i: (bi, hi, 0, 0))
    bs = pl.BlockSpec((None, None, GRID_W, NA_BAND), lambda bi, hi, i: (hi, _na_geometry(i, nc, rows)[1], 0, 0))
    return pl.pallas_call(
        body, out_shape=[jax.ShapeDtypeStruct((b, h, s, dh), F32), jax.ShapeDtypeStruct((b, h, s, 1), F32)],
        grid=(b, h, s // GRID_W), in_specs=[qs(dh), ks, ks, bs], out_specs=[qs(dh), qs(1)],
        name=f"na_f_{s}", compiler_params=_params(),
    )(q, k, v, bias)


def _na_bwd(q, k, v, bias, o, lse, do, n_ctx):
    b, h, s, dh = q.shape
    nc = n_ctx // GRID_W
    rows = (s - n_ctx) // GRID_W
    scale = dh ** -0.5
    n_cls = NA_WIN_R + 1

    def body(q_ref, k_ref, v_ref, bias_ref, o_ref, lse_ref, do_ref, dq_ref, dk_ref, dv_ref, db_ref):
        i = pl.program_id(2)
        rs, cls = _na_geometry(i, nc, rows)
        _, cls_prev = _na_geometry(i - 1, nc, rows)
        start = pl.multiple_of(n_ctx + rs * GRID_W, GRID_W)

        @pl.when(i == 0)
        def _():
            dk_ref[...] = jnp.zeros_like(dk_ref)
            dv_ref[...] = jnp.zeros_like(dv_ref)

        qv, kc, kb, s_c, s_l = _na_scores(q_ref, k_ref, bias_ref, n_ctx, start, scale)
        lse_v = lse_ref[...]
        p_c = jnp.exp(s_c - lse_v)
        p_l = jnp.exp(s_l - lse_v)
        dov = do_ref[...]
        dob = dov.astype(BF16)
        delta = jnp.sum(dov * o_ref[...], axis=-1, keepdims=True)
        dn = (((1,), (1,)), ((), ()))
        dt = (((0,), (0,)), ((), ()))
        vc = v_ref[0:n_ctx, :].astype(BF16)
        vb = v_ref[pl.ds(start, NA_BAND), :].astype(BF16)
        ds_c = p_c * (lax.dot_general(dob, vc, dn, preferred_element_type=F32) - delta)
        ds_l = p_l * (lax.dot_general(dob, vb, dn, preferred_element_type=F32) - delta)
        dsc_b = ds_c.astype(BF16)
        dsl_b = ds_l.astype(BF16)
        dq_ref[...] = (jnp.dot(dsc_b, kc, preferred_element_type=F32)
                       + jnp.dot(dsl_b, kb, preferred_element_type=F32)) * scale
        dk_ref[0:n_ctx, :] += lax.dot_general(dsc_b, qv, dt, preferred_element_type=F32) * scale
        dk_ref[pl.ds(start, NA_BAND), :] += lax.dot_general(dsl_b, qv, dt, preferred_element_type=F32) * scale
        dv_ref[0:n_ctx, :] += lax.dot_general(p_c.astype(BF16), dob, dt, preferred_element_type=F32)
        dv_ref[pl.ds(start, NA_BAND), :] += lax.dot_general(p_l.astype(BF16), dob, dt, preferred_element_type=F32)
        first = jnp.logical_or(i == 0, cls != cls_prev)

        @pl.when(first)
        def _():
            db_ref[...] = ds_l

        @pl.when(jnp.logical_not(first))
        def _():
            db_ref[...] += ds_l

    qs = lambda d: pl.BlockSpec((None, None, GRID_W, d), lambda bi, hi, i: (bi, hi, i, 0))
    ks = pl.BlockSpec((None, None, s, dh), lambda bi, hi, i: (bi, hi, 0, 0))
    bs = pl.BlockSpec((None, None, GRID_W, NA_BAND), lambda bi, hi, i: (hi, _na_geometry(i, nc, rows)[1], 0, 0))
    dbs = pl.BlockSpec((None, None, None, GRID_W, NA_BAND),
                       lambda bi, hi, i: (bi, hi, _na_geometry(i, nc, rows)[1], 0, 0))
    return pl.pallas_call(
        body,
        out_shape=[jax.ShapeDtypeStruct((b, h, s, dh), F32), jax.ShapeDtypeStruct((b, h, s, dh), F32),
                   jax.ShapeDtypeStruct((b, h, s, dh), F32), jax.ShapeDtypeStruct((b, h, n_cls, GRID_W, NA_BAND), F32)],
        grid=(b, h, s // GRID_W), in_specs=[qs(dh), ks, ks, bs, qs(dh), qs(1), qs(dh)],
        out_specs=[qs(dh), ks, ks, dbs], name=f"na_b_{s}", compiler_params=_params(),
    )(q, k, v, bias, o, lse, do)


@functools.partial(jax.custom_vjp, nondiff_argnums=(4,))
def na_attention(q, k, v, bias, n_ctx):
    return _na_fwd(q, k, v, bias, n_ctx)[0]


def _na_attention_fwd(q, k, v, bias, n_ctx):
    o, lse = _na_fwd(q, k, v, bias, n_ctx)
    return o, (q, k, v, bias, o, lse)


def _na_attention_bwd(n_ctx, res, do):
    q, k, v, bias, o, lse = res
    dq, dk, dv, db = _na_bwd(q, k, v, bias, o, lse, do, n_ctx)
    return dq, dk, dv, jnp.sum(db, axis=0)


na_attention.defvjp(_na_attention_fwd, _na_attention_bwd)


def _na_onehots():
    q = np.arange(GRID_W)[:, None]
    col = np.arange(GRID_W)[None, :]
    cs = np.clip(q - NA_WIN_C // 2, 0, GRID_W - NA_WIN_C)
    valid = (col >= cs) & (col < cs + NA_WIN_C)
    cidx = col - q + (NA_WIN_C - 1)
    n_b = 2 * NA_WIN_C - 1
    col_hot = np.zeros((LANE, GRID_W * GRID_W), np.float32)
    for qq in range(GRID_W):
        for cc in range(GRID_W):
            if valid[qq, cc]:
                col_hot[cidx[qq, cc], qq * GRID_W + cc] = 1.0
    row_hot = np.zeros((NA_WIN_R, NA_WIN_R, 2 * NA_WIN_R - 1), np.float32)
    for c in range(NA_WIN_R):
        for j in range(NA_WIN_R):
            row_hot[c, j, j - c + NA_WIN_R - 1] = 1.0
    mask = np.where(valid, 0.0, NEG).astype(np.float32)
    return col_hot, row_hot, mask, n_b


def na_bias_table(rpb):
    h = rpb.shape[0]
    col_hot, row_hot, mask, n_b = _na_onehots()
    t1 = jnp.einsum("cja,hab->hcjb", jnp.asarray(row_hot), rpb)
    t1 = jnp.pad(t1.reshape(h * NA_WIN_R * NA_WIN_R, n_b), ((0, 0), (0, LANE - n_b)))
    t2 = linear(t1, jnp.asarray(col_hot), True)
    t2 = t2.reshape(h, NA_WIN_R, NA_WIN_R, GRID_W, GRID_W) + jnp.asarray(mask)
    tab = jnp.transpose(t2, (0, 1, 3, 2, 4)).reshape(h, NA_WIN_R, GRID_W, NA_BAND)
    return jnp.concatenate([tab, jnp.full((h, 1, GRID_W, NA_BAND), NEG, F32)], axis=1)


def _cmul(ar, ai, br, bi):
    return ar * br - ai * bi, ar * bi + ai * br


def _s5_chunk(n_ctx):
    return min(256, n_ctx)


def _s5_tables(a_re, a_im, t_len, rev):
    a_re, a_im = lax.stop_gradient(a_re), lax.stop_gradient(a_im)
    mag = jnp.sqrt(a_re * a_re + a_im * a_im)
    th = jnp.arctan2(a_im, a_re)
    t = jnp.arange(t_len + 1, dtype=F32)[:, None]
    pm = jnp.where(t == 0, 1.0, jnp.exp(t * jnp.log(jnp.maximum(mag, 1e-37))) * (mag > 0))
    pw = jnp.stack([pm * jnp.cos(t * th), pm * jnp.sin(t * th)])
    steps = pw[:, np.minimum(2 ** np.arange(8), t_len)]
    tile = pw[:, 1:9]
    a8k = pw[:, 0:t_len:8]
    if rev:
        tile, a8k = tile[:, ::-1], a8k[:, ::-1]
    misc = jnp.concatenate([pw[:, t_len:t_len + 1], jnp.zeros((2, 7, pw.shape[-1]), F32)], axis=1)
    return jnp.concatenate([steps, tile, misc, a8k], axis=1)


def _scan_chunk(x_re, x_im, tab_ref, hin_re, hin_im, rev, t_len, xs_ref, es_ref):
    outs = [_scan_slab(x_re[:, k:k + LANE], x_im[:, k:k + LANE], tab_ref, hin_re[:, k:k + LANE], hin_im[:, k:k + LANE],
                       rev, t_len, xs_ref, es_ref, k) for k in range(0, x_re.shape[-1], LANE)]
    return tuple(jnp.concatenate([o[t] for o in outs], axis=-1) for t in range(4))


def _scan_slab(x_re, x_im, tab_ref, hin_re, hin_im, rev, t_len, xs_ref, es_ref, k0):
    lanes = LANE
    n2 = t_len // 8
    tab_ref = tab_ref.at[:, :, k0:k0 + LANE]
    rin = lax.broadcasted_iota(jnp.int32, (t_len, lanes), 0) & 7
    for li, sh in enumerate((1, 2, 4)):
        m_re, m_im = tab_ref[0, li:li + 1, :], tab_ref[1, li:li + 1, :]
        amt = sh if not rev else t_len - sh
        c_re, c_im = _cmul(m_re, m_im, pltpu.roll(x_re, amt, 0), pltpu.roll(x_im, amt, 0))
        ok = (rin >= sh) if not rev else (rin < 8 - sh)
        x_re = x_re + jnp.where(ok, c_re, 0.0)
        x_im = x_im + jnp.where(ok, c_im, 0.0)
    xr_ref, xi_ref = xs_ref
    xr_ref[...] = x_re
    xi_ref[...] = x_im
    off = 0 if rev else 7
    e_re = xr_ref[pl.ds(off, n2, stride=8), :]
    e_im = xi_ref[pl.ds(off, n2, stride=8), :]
    row2 = lax.broadcasted_iota(jnp.int32, (n2, lanes), 0)
    sh, li = 1, 3
    while sh < n2:
        m_re, m_im = tab_ref[0, li:li + 1, :], tab_ref[1, li:li + 1, :]
        amt = sh if not rev else n2 - sh
        c_re, c_im = _cmul(m_re, m_im, pltpu.roll(e_re, amt, 0), pltpu.roll(e_im, amt, 0))
        ok = (row2 >= sh) if not rev else (row2 < n2 - sh)
        e_re = e_re + jnp.where(ok, c_re, 0.0)
        e_im = e_im + jnp.where(ok, c_im, 0.0)
        sh, li = sh * 2, li + 1
    es_ref[0] = e_re
    es_ref[1] = e_im
    last = 0 if rev else n2 - 1
    t_re, t_im = _cmul(tab_ref[0, 16:17, :], tab_ref[1, 16:17, :], hin_re, hin_im)
    hout_re = es_ref[0, last:last + 1, :] + t_re
    hout_im = es_ref[1, last:last + 1, :] + t_im
    amt = 1 if not rev else n2 - 1
    ok = (row2 >= 1) if not rev else (row2 < n2 - 1)
    k_re, k_im = _cmul(tab_ref[0, 24:24 + n2, :], tab_ref[1, 24:24 + n2, :], hin_re, hin_im)
    c_re = jnp.where(ok, pltpu.roll(e_re, amt, 0), 0.0) + k_re
    c_im = jnp.where(ok, pltpu.roll(e_im, amt, 0), 0.0) + k_im
    tp_re, tp_im = tab_ref[0, 8:16, :][None], tab_ref[1, 8:16, :][None]
    add_re, add_im = _cmul(tp_re, tp_im, c_re[:, None, :], c_im[:, None, :])
    h_re = xr_ref[...] + add_re.reshape(t_len, lanes)
    h_im = xi_ref[...] + add_im.reshape(t_len, lanes)
    return h_re, h_im, hout_re, hout_im


def _s5_order(j, n_chunks, nc, rev):
    if not rev:
        return j
    return jnp.where(j < nc, nc - 1 - j, n_chunks - 1 - (j - nc))


def _s5_fwd(u, tab, b_bd, c_bd, n_ctx, rev):
    b, s, w = u.shape
    lanes = b_bd.shape[-1]
    t_len = _s5_chunk(n_ctx)
    n_chunks, nc = s // t_len, n_ctx // t_len

    def body(u_ref, tab_ref, b_ref, c_ref, y_ref, h_ref, hin_ref, carry_ref, xr_ref, xi_ref, es_ref):
        xs_ref = (xr_ref, xi_ref)

        @pl.when(pl.program_id(1) == 0)
        def _():
            carry_ref[...] = jnp.zeros_like(carry_ref)

        ub = u_ref[...].astype(BF16)
        x_re = jnp.dot(ub, b_ref[0].astype(BF16), preferred_element_type=F32)
        x_im = jnp.dot(ub, b_ref[1].astype(BF16), preferred_element_type=F32)
        hin_re, hin_im = carry_ref[0, 0:1, :], carry_ref[1, 0:1, :]
        hin_ref[...] = carry_ref[...]
        h_re, h_im, ho_re, ho_im = _scan_chunk(x_re, x_im, tab_ref, hin_re, hin_im, rev, t_len, xs_ref, es_ref)
        carry_ref[0] = jnp.broadcast_to(ho_re, (8, lanes))
        carry_ref[1] = jnp.broadcast_to(ho_im, (8, lanes))
        h_ref[0] = h_re
        h_ref[1] = h_im
        y_ref[...] = (jnp.dot(h_re.astype(BF16), c_ref[0].astype(BF16), preferred_element_type=F32)
                      - jnp.dot(h_im.astype(BF16), c_ref[1].astype(BF16), preferred_element_type=F32))

    order = lambda j: _s5_order(j, n_chunks, nc, rev)
    whole = lambda arr: pl.BlockSpec(arr.shape, lambda bi, j: (0,) * arr.ndim)
    return pl.pallas_call(
        body,
        out_shape=[jax.ShapeDtypeStruct((b, s, w), F32), jax.ShapeDtypeStruct((2, b, s, lanes), F32),
                   jax.ShapeDtypeStruct((2, b, n_chunks, 8, lanes), F32)],
        grid=(b, n_chunks),
        in_specs=[pl.BlockSpec((None, t_len, w), lambda bi, j: (bi, order(j), 0)), whole(tab), whole(b_bd), whole(c_bd)],
        out_specs=[pl.BlockSpec((None, t_len, w), lambda bi, j: (bi, order(j), 0)),
                   pl.BlockSpec((2, None, t_len, lanes), lambda bi, j: (0, bi, order(j), 0)),
                   pl.BlockSpec((2, None, None, 8, lanes), lambda bi, j: (0, bi, order(j), 0, 0))],
        scratch_shapes=[pltpu.VMEM((2, 8, lanes), F32), pltpu.VMEM((t_len, LANE), F32), pltpu.VMEM((t_len, LANE), F32),
                        pltpu.VMEM((2, t_len // 8, LANE), F32)],
        name=f"s5_f_{s}_{int(rev)}", compiler_params=_params(),
    )(u, tab, b_bd, c_bd)


def _s5_bwd(u, tab_adj, b_bd, c_bd, h, hin, dy, n_ctx, rev):
    b, s, w = u.shape
    lanes = b_bd.shape[-1]
    t_len = _s5_chunk(n_ctx)
    n_chunks, nc = s // t_len, n_ctx // t_len
    arev = not rev

    def body(u_ref, tab_ref, b_ref, c_ref, h_ref, hin_ref, dy_ref, du_ref, db_ref, dc_ref, da_ref,
             carry_ref, xr_ref, xi_ref, es_ref):
        xs_ref = (xr_ref, xi_ref)
        first = jnp.logical_and(pl.program_id(0) == 0, pl.program_id(1) == 0)

        @pl.when(pl.program_id(1) == 0)
        def _():
            carry_ref[...] = jnp.zeros_like(carry_ref)

        dyv = dy_ref[...]
        dyb = dyv.astype(BF16)
        dn = (((1,), (1,)), ((), ()))
        dt = (((0,), (0,)), ((), ()))
        x_re = lax.dot_general(dyb, c_ref[0].astype(BF16), dn, preferred_element_type=F32)
        x_im = -lax.dot_general(dyb, c_ref[1].astype(BF16), dn, preferred_element_type=F32)
        g_re, g_im, go_re, go_im = _scan_chunk(x_re, x_im, tab_ref, carry_ref[0, 0:1, :], carry_ref[1, 0:1, :],
                                               arev, t_len, xs_ref, es_ref)
        carry_ref[0] = jnp.broadcast_to(go_re, (8, lanes))
        carry_ref[1] = jnp.broadcast_to(go_im, (8, lanes))
        h_re, h_im = h_ref[0], h_ref[1]
        gb_re, gb_im = g_re.astype(BF16), g_im.astype(BF16)
        du_ref[...] = (lax.dot_general(gb_re, b_ref[0].astype(BF16), dn, preferred_element_type=F32)
                       + lax.dot_general(gb_im, b_ref[1].astype(BF16), dn, preferred_element_type=F32))
        ub = u_ref[...].astype(BF16)
        db_re = lax.dot_general(ub, gb_re, dt, preferred_element_type=F32)
        db_im = lax.dot_general(ub, gb_im, dt, preferred_element_type=F32)
        dc_re = lax.dot_general(h_re.astype(BF16), dyb, dt, preferred_element_type=F32)
        dc_im = -lax.dot_general(h_im.astype(BF16), dyb, dt, preferred_element_type=F32)
        row = lax.broadcasted_iota(jnp.int32, (t_len, lanes), 0)
        amt = 1 if not rev else t_len - 1
        edge = (row == 0) if not rev else (row == t_len - 1)
        hp_re = jnp.where(edge, hin_ref[0, 0:1, :], pltpu.roll(h_re, amt, 0))
        hp_im = jnp.where(edge, hin_ref[1, 0:1, :], pltpu.roll(h_im, amt, 0))
        da_re = jnp.sum(g_re * hp_re + g_im * hp_im, axis=0, keepdims=True)
        da_im = jnp.sum(g_im * hp_re - g_re * hp_im, axis=0, keepdims=True)

        @pl.when(first)
        def _():
            db_ref[0], db_ref[1] = db_re, db_im
            dc_ref[0], dc_ref[1] = dc_re, dc_im
            da_ref[0] = jnp.broadcast_to(da_re, (8, lanes))
            da_ref[1] = jnp.broadcast_to(da_im, (8, lanes))

        @pl.when(jnp.logical_not(first))
        def _():
            db_ref[0] += db_re
            db_ref[1] += db_im
            dc_ref[0] += dc_re
            dc_ref[1] += dc_im
            da_ref[0] += jnp.broadcast_to(da_re, (8, lanes))
            da_ref[1] += jnp.broadcast_to(da_im, (8, lanes))

    order = lambda j: _s5_order(n_chunks - 1 - j, n_chunks, nc, rev)
    whole = lambda arr: pl.BlockSpec(arr.shape, lambda bi, j: (0,) * arr.ndim)
    us = pl.BlockSpec((None, t_len, w), lambda bi, j: (bi, order(j), 0))
    return pl.pallas_call(
        body,
        out_shape=[jax.ShapeDtypeStruct((b, s, w), F32), jax.ShapeDtypeStruct(b_bd.shape, F32),
                   jax.ShapeDtypeStruct(c_bd.shape, F32), jax.ShapeDtypeStruct((2, 8, lanes), F32)],
        grid=(b, n_chunks),
        in_specs=[us, whole(tab_adj), whole(b_bd), whole(c_bd),
                  pl.BlockSpec((2, None, t_len, lanes), lambda bi, j: (0, bi, order(j), 0)),
                  pl.BlockSpec((2, None, None, 8, lanes), lambda bi, j: (0, bi, order(j), 0, 0)), us],
        out_specs=[us, whole(b_bd), whole(c_bd), pl.BlockSpec((2, 8, lanes), lambda bi, j: (0, 0, 0))],
        scratch_shapes=[pltpu.VMEM((2, 8, lanes), F32), pltpu.VMEM((t_len, LANE), F32), pltpu.VMEM((t_len, LANE), F32),
                        pltpu.VMEM((2, t_len // 8, LANE), F32)],
        name=f"s5_b_{s}_{int(rev)}", compiler_params=_params(),
    )(u, tab_adj, b_bd, c_bd, h, hin, dy)


@functools.partial(jax.custom_vjp, nondiff_argnums=(4, 5))
def s5_direction(u, a, b_bd, c_bd, n_ctx, rev):
    tab = _s5_tables(a[0], a[1], _s5_chunk(n_ctx), rev)
    return _s5_fwd(u, tab, b_bd, c_bd, n_ctx, rev)[0]


def _s5_direction_fwd(u, a, b_bd, c_bd, n_ctx, rev):
    tab = _s5_tables(a[0], a[1], _s5_chunk(n_ctx), rev)
    y, h, hin = _s5_fwd(u, tab, b_bd, c_bd, n_ctx, rev)
    return y, (u, a, b_bd, c_bd, h, hin)


def _s5_direction_bwd(n_ctx, rev, res, dy):
    u, a, b_bd, c_bd, h, hin = res
    tab_adj = _s5_tables(a[0], -a[1], _s5_chunk(n_ctx), not rev)
    du, db, dc, da = _s5_bwd(u, tab_adj, b_bd, c_bd, h, hin, dy, n_ctx, rev)
    return du, da[:, 0, :], db, dc


s5_direction.defvjp(_s5_direction_fwd, _s5_direction_bwd)


def _s5_discretize(lam_re, lam_im, log_dt, b_re, b_im):
    dt = jnp.exp(log_dt)[:, None]
    mag = jnp.exp(lam_re * dt)
    a_re = mag * jnp.cos(lam_im * dt)
    a_im = mag * jnp.sin(lam_im * dt)
    den = jnp.square(lam_re) + jnp.square(lam_im)
    f_re = ((a_re - 1.0) * lam_re + a_im * lam_im) / den
    f_im = (a_im * lam_re - (a_re - 1.0) * lam_im) / den
    bb_re = f_re[..., None] * b_re - f_im[..., None] * b_im
    bb_im = f_re[..., None] * b_im + f_im[..., None] * b_re
    return a_re, a_im, bb_re, bb_im


def _block_diag(t):
    g, r, c = t.shape
    return (jnp.eye(g, dtype=F32)[:, None, :, None] * t[:, :, None, :]).reshape(g * r, g * c)


def _loss_head(y, target):
    b, n, d = y.shape
    ts = _pick(n, (256, 128, 64))

    def body(y_ref, t_ref, loss_ref, dy_ref):
        first = jnp.logical_and(pl.program_id(0) == 0, pl.program_id(1) == 0)
        err = y_ref[...] - t_ref[...]
        dy_ref[...] = err * (1.0 / d)
        part = 0.5 * jnp.sum(jnp.sum(err * err, axis=-1, keepdims=True) * (1.0 / d), axis=0, keepdims=True)
        part = jnp.broadcast_to(part, (8, LANE))

        @pl.when(first)
        def _():
            loss_ref[...] = part

        @pl.when(jnp.logical_not(first))
        def _():
            loss_ref[...] += part

    blk = pl.BlockSpec((None, ts, d), lambda bi, i: (bi, i, 0))
    return pl.pallas_call(
        body, out_shape=[jax.ShapeDtypeStruct((8, LANE), F32), jax.ShapeDtypeStruct((b, n, d), F32)],
        grid=(b, n // ts), in_specs=[blk, blk], out_specs=[pl.BlockSpec((8, LANE), lambda bi, i: (0, 0)), blk],
        name="loss_head", compiler_params=_params(),
    )(y, target)


def _adamw(w, g, m, v):
    shape = w.shape
    n = int(np.prod(shape))
    cols = 1024 if n % 1024 == 0 else shape[-1]
    r = n // cols
    tr = _pick(r, (512, 256, 128, 64, 32, 16, 8))
    c1 = 1.0 / (1.0 - ADAM_B1 ** ADAM_STEP)
    c2 = 1.0 / (1.0 - ADAM_B2 ** ADAM_STEP)

    def body(w_ref, g_ref, m_ref, v_ref, d_ref, mo_ref, vo_ref):
        gv = g_ref[...]
        m2 = ADAM_B1 * m_ref[...] + (1.0 - ADAM_B1) * gv
        v2 = ADAM_B2 * v_ref[...] + (1.0 - ADAM_B2) * (gv * gv)
        d_ref[...] = -ADAM_LR * ((m2 * c1) / (jnp.sqrt(v2 * c2) + ADAM_EPS) + ADAM_WD * w_ref[...])
        mo_ref[...] = m2
        vo_ref[...] = v2

    blk = pl.BlockSpec((tr, cols), lambda i: (i, 0))
    outs = pl.pallas_call(
        body, out_shape=[jax.ShapeDtypeStruct((r, cols), F32)] * 3, grid=(r // tr,),
        in_specs=[blk] * 4, out_specs=[blk] * 3, name=f"adamw_{r}x{cols}", compiler_params=_params(),
    )(*[t.reshape(r, cols) for t in (w, g, m, v)])
    return tuple(o.reshape(shape) for o in outs)


def _sum_rows(x, n):
    _, r, c = x.shape
    tr = _pick(r, (512, 256, 128, 64, 32, 16, 8))

    def body(x_ref, o_ref):
        acc = x_ref[0]
        for j in range(1, n):
            acc = acc + x_ref[j]
        o_ref[...] = acc

    return pl.pallas_call(
        body, out_shape=jax.ShapeDtypeStruct((r, c), F32), grid=(r // tr,),
        in_specs=[pl.BlockSpec((n, tr, c), lambda i: (0, i, 0))], out_specs=pl.BlockSpec((tr, c), lambda i: (i, 0)),
        name=f"sum{n}_{r}x{c}", compiler_params=_params(),
    )(x)


def _add2(x, y):
    shape = x.shape
    c = shape[-1]
    r = int(np.prod(shape)) // c
    tr = _pick(r, (512, 256, 128, 64, 32, 16, 8))

    def body(x_ref, y_ref, o_ref):
        o_ref[...] = x_ref[...] + y_ref[...]

    blk = pl.BlockSpec((tr, c), lambda i: (i, 0))
    return pl.pallas_call(
        body, out_shape=jax.ShapeDtypeStruct((r, c), F32), grid=(r // tr,), in_specs=[blk, blk], out_specs=blk,
        name=f"add2_{r}x{c}", compiler_params=_params(),
    )(x.reshape(r, c), y.reshape(r, c)).reshape(shape)


_FLIPS = ((1, 0), (0, 1), (1, 1))


def _me():
    return lax.axis_index("x"), lax.axis_index("y"), lax.axis_index("c")


def allgather8(v):
    m_per, n = v.shape

    def body(x_ref, out_ref, send_sems, recv_sems, local_sem):
        x, y, c = _me()
        me, sibling = (x, y, c), (x, y, 1 - c)
        chips = [(1 - x, y), (x, 1 - y), (1 - x, 1 - y)]

        def rows(px, py, pc):
            return out_ref.at[pl.ds((4 * px + 2 * py + pc) * m_per, m_per), :]

        def copy(k, block, to, src=None):
            return pltpu.make_async_remote_copy(
                src_ref=rows(*block) if src is None else src, dst_ref=rows(*block),
                send_sem=send_sems.at[k], recv_sem=recv_sems.at[k], device_id=to, device_id_type=MESH)

        mine = pltpu.make_async_copy(x_ref, rows(*me), local_sem)
        mine.start()
        first = [copy(0, me, sibling, src=x_ref)]
        first += [copy(1 + j, me, (*chip, c), src=x_ref) for j, chip in enumerate(chips)]
        for cp in first:
            cp.start()
        passed = [copy(4 + j, (*chip, c), sibling) for j, chip in enumerate(chips)]
        for j, chip in enumerate(chips):
            copy(1 + j, (*chip, c), me).wait_recv()
            passed[j].start()
        copy(0, sibling, me).wait_recv()
        for j, chip in enumerate(chips):
            copy(4 + j, (*chip, 1 - c), me).wait_recv()
        for cp in first + passed:
            cp.wait_send()
        mine.wait()

    return pl.pallas_call(
        body, out_shape=jax.ShapeDtypeStruct((N_DEV * m_per, n), v.dtype), in_specs=[VMEM_SPEC], out_specs=VMEM_SPEC,
        scratch_shapes=[pltpu.SemaphoreType.DMA((7,)), pltpu.SemaphoreType.DMA((7,)), pltpu.SemaphoreType.DMA],
        name=f"allgather8_{m_per}x{n}", compiler_params=_params(),
    )(v)


def plane_allgather(big, small):
    def body(big_ref, small_ref, obig_ref, osmall_ref, send_sems, recv_sems, local_sems):
        x, y, c = _me()
        me = 2 * x + y
        l_big = pltpu.make_async_copy(big_ref, obig_ref.at[me], local_sems.at[0])
        l_small = pltpu.make_async_copy(small_ref, osmall_ref.at[me], local_sems.at[1])
        l_big.start()
        l_small.start()
        sends = []
        for j, (fx, fy) in enumerate(_FLIPS):
            peer = ((x + fx) & 1, (y + fy) & 1, c)
            for t, (src, dst) in enumerate(((big_ref, obig_ref), (small_ref, osmall_ref))):
                cp = pltpu.make_async_remote_copy(src_ref=src, dst_ref=dst.at[me], send_sem=send_sems.at[2 * j + t],
                                                  recv_sem=recv_sems.at[2 * j + t], device_id=peer, device_id_type=MESH)
                cp.start()
                sends.append(cp)
        for j, (fx, fy) in enumerate(_FLIPS):
            px, py = (x + fx) & 1, (y + fy) & 1
            for t, (src, dst) in enumerate(((big_ref, obig_ref), (small_ref, osmall_ref))):
                pltpu.make_async_remote_copy(src_ref=src, dst_ref=dst.at[2 * px + py], send_sem=send_sems.at[2 * j + t],
                                             recv_sem=recv_sems.at[2 * j + t], device_id=(px, py, c),
                                             device_id_type=MESH).wait_recv()
        for cp in sends:
            cp.wait_send()
        l_big.wait()
        l_small.wait()

    return pl.pallas_call(
        body, out_shape=[jax.ShapeDtypeStruct((N_PLANE,) + big.shape, big.dtype),
                         jax.ShapeDtypeStruct((N_PLANE,) + small.shape, small.dtype)],
        in_specs=[ANY, ANY], out_specs=[ANY, ANY],
        scratch_shapes=[pltpu.SemaphoreType.DMA((6,)), pltpu.SemaphoreType.DMA((6,)), pltpu.SemaphoreType.DMA((2,))],
        name="plane_allgather", compiler_params=_params(),
    )(big, small)


def sibling_split(buf):
    def body(buf_ref, mine_ref, got_ref, send_sem, recv_sem, local_sem):
        x, y, c = _me()
        loc = pltpu.make_async_copy(buf_ref.at[c], mine_ref, local_sem)
        loc.start()
        cp = pltpu.make_async_remote_copy(src_ref=buf_ref.at[1 - c], dst_ref=got_ref, send_sem=send_sem,
                                          recv_sem=recv_sem, device_id=(x, y, 1 - c), device_id_type=MESH)
        cp.start()
        cp.wait()
        loc.wait()

    half = jax.ShapeDtypeStruct(buf.shape[1:], buf.dtype)
    return pl.pallas_call(
        body, out_shape=[half, half], in_specs=[ANY], out_specs=[ANY, ANY],
        scratch_shapes=[pltpu.SemaphoreType.DMA, pltpu.SemaphoreType.DMA, pltpu.SemaphoreType.DMA],
        name="sibling_split", compiler_params=_params(),
    )(buf)


def plane_scatter(p):
    def body(p_ref, out_ref, send_sems, recv_sems, local_sem):
        x, y, c = _me()
        me = 2 * x + y
        loc = pltpu.make_async_copy(p_ref.at[me], out_ref.at[me], local_sem)
        loc.start()
        sends = []
        for j, (fx, fy) in enumerate(_FLIPS):
            px, py = (x + fx) & 1, (y + fy) & 1
            cp = pltpu.make_async_remote_copy(src_ref=p_ref.at[2 * px + py], dst_ref=out_ref.at[me],
                                              send_sem=send_sems.at[j], recv_sem=recv_sems.at[j],
                                              device_id=(px, py, c), device_id_type=MESH)
            cp.start()
            sends.append(cp)
        for j, (fx, fy) in enumerate(_FLIPS):
            px, py = (x + fx) & 1, (y + fy) & 1
            pltpu.make_async_remote_copy(src_ref=p_ref.at[me], dst_ref=out_ref.at[2 * px + py],
                                         send_sem=send_sems.at[j], recv_sem=recv_sems.at[j],
                                         device_id=(px, py, c), device_id_type=MESH).wait_recv()
        for cp in sends:
            cp.wait_send()
        loc.wait()

    return pl.pallas_call(
        body, out_shape=jax.ShapeDtypeStruct(p.shape, p.dtype), in_specs=[ANY], out_specs=ANY,
        scratch_shapes=[pltpu.SemaphoreType.DMA((3,)), pltpu.SemaphoreType.DMA((3,)), pltpu.SemaphoreType.DMA],
        name="plane_scatter", compiler_params=_params(),
    )(p)


def sibling_join(s):
    def body(s_ref, out_ref, send_sem, recv_sem, local_sem):
        x, y, c = _me()
        loc = pltpu.make_async_copy(s_ref, out_ref.at[c], local_sem)
        loc.start()
        cp = pltpu.make_async_remote_copy(src_ref=s_ref, dst_ref=out_ref.at[c], send_sem=send_sem, recv_sem=recv_sem,
                                          device_id=(x, y, 1 - c), device_id_type=MESH)
        cp.start()
        cp.wait_send()
        pltpu.make_async_remote_copy(src_ref=s_ref, dst_ref=out_ref.at[1 - c], send_sem=send_sem, recv_sem=recv_sem,
                                     device_id=(x, y, 1 - c), device_id_type=MESH).wait_recv()
        loc.wait()

    return pl.pallas_call(
        body, out_shape=jax.ShapeDtypeStruct((2,) + s.shape, s.dtype), in_specs=[ANY], out_specs=ANY,
        scratch_shapes=[pltpu.SemaphoreType.DMA, pltpu.SemaphoreType.DMA, pltpu.SemaphoreType.DMA],
        name="sibling_join", compiler_params=_params(),
    )(s)


def _heads(t, n_heads):
    b, s, w = t.shape
    return jnp.transpose(t.reshape(b, s, n_heads, w // n_heads), (0, 2, 1, 3)).reshape(b * n_heads, s, w // n_heads)


def _unheads(t, b):
    bh, s, d = t.shape
    return jnp.transpose(t.reshape(b, bh // b, s, d), (0, 2, 1, 3)).reshape(b, s, (bh // b) * d)


def _op(cache, fn, name, kinds, out_dims, **kw):
    key = (name, tuple(out_dims), tuple(sorted(kw.items())))
    if key not in cache:
        cache[key] = make_rowwise(fn, name, kinds, out_dims, **kw)
    return cache[key]


def _even_mixer(ops, a, w, n_ctx):
    b, s, d = a.shape
    proj = linear(a.reshape(b * s, d), w["e_w_in"]).reshape(b, s, -1)
    q, k, v, u = jnp.split(proj, [GQA_Q_W, GQA_Q_W + GQA_KV_W, GQA_Q_W + 2 * GQA_KV_W], axis=-1)
    cos, sin = _rope_tables(n_ctx, s - n_ctx, HEAD_DIM, 0, HEAD_DIM)
    rot = jnp.asarray(_rope_matrix(HEAD_DIM, 0, HEAD_DIM))
    nr = _op(ops, _fn_norm_rope, "norm_rope", ("row", "tab", "tab", "const", "glob"), (HEAD_DIM,), whole_seq=True)
    qh = nr(_heads(q, GQA_Q_HEADS), cos, sin, rot, w["e_g_q"][None])[0]
    kh = nr(_heads(k, GQA_KV_HEADS), cos, sin, rot, w["e_g_k"][None])[0]
    vh = _heads(v, GQA_KV_HEADS)
    att = attention(qh.reshape(b, GQA_Q_HEADS, s, HEAD_DIM), kh.reshape(b, GQA_KV_HEADS, s, HEAD_DIM),
                    vh.reshape(b, GQA_KV_HEADS, s, HEAD_DIM), GQA_Q_HEADS // GQA_KV_HEADS, n_ctx, HEAD_DIM ** -0.5)
    att = _unheads(att.reshape(b * GQA_Q_HEADS, s, HEAD_DIM), b)
    ys = []
    for dr in range(2):
        a_re, a_im, bb_re, bb_im = _s5_discretize(w["ssm_lam_re"][dr], w["ssm_lam_im"][dr], w["ssm_log_dt"][dr],
                                                  w["ssm_b_re"][dr], w["ssm_b_im"][dr])
        a_flat = jnp.stack([a_re.reshape(-1), a_im.reshape(-1)])
        b_bd = jnp.stack([_block_diag(jnp.swapaxes(bb_re, 1, 2)), _block_diag(jnp.swapaxes(bb_im, 1, 2))])
        c_bd = jnp.stack([_block_diag(jnp.swapaxes(w["ssm_c_re"][dr], 1, 2)),
                          _block_diag(jnp.swapaxes(w["ssm_c_im"][dr], 1, 2))])
        ys.append(s5_direction(u, a_flat, b_bd, c_bd, n_ctx, dr == 1))
    pre = _op(ops, _fn_glu_pre, "glu_pre", ("row", "row", "row", "glob"), (SSM_WIDTH,))
    post = _op(ops, _fn_glu_post, "glu_post", ("row", "row", "glob"), (SSM_WIDTH,))
    z = pre(u, ys[0], ys[1], w["ssm_d"][None])[0]
    t = linear(z.reshape(b * s, SSM_WIDTH), w["ssm_w_glu"]).reshape(b, s, SSM_WIDTH)
    ssm = post(z, t, w["ssm_b_glu"][None])[0]
    mix = jnp.concatenate([att, ssm], axis=-1)
    return linear(mix.reshape(b * s, -1), w["e_w_out"]).reshape(b, s, d)


def _odd_mixer(ops, a, w, n_ctx):
    b, s, d = a.shape
    w_in = jnp.pad(w["o_w_in"], ((0, 0), (0, ODD_IN_PAD - ODD_IN_W)))
    proj = linear(a.reshape(b * s, d), w_in).reshape(b, s, -1)
    c1 = MLA_Q_RANK
    c2 = c1 + MLA_KV_RANK
    c3 = c2 + MLA_ROPE
    cq, ckv, kr, nq, nk, nv, _ = jnp.split(proj, [c1, c2, c3, c3 + NA_W, c3 + 2 * NA_W, ODD_IN_W], axis=-1)
    nrm = lambda wd: _op(ops, _fn_norm, f"norm{wd}", ("row", "glob"), (wd,))
    cqn = nrm(MLA_Q_RANK)(cq, w["mla_g_cq"][None])[0]
    ckvn = nrm(MLA_KV_RANK)(ckv, w["mla_g_ckv"][None])[0]
    q = linear(cqn.reshape(b * s, -1), w["mla_w_uq"]).reshape(b, s, -1)
    kv = linear(ckvn.reshape(b * s, -1), w["mla_w_ukv"]).reshape(b, s, MLA_HEADS, MLA_NOPE + MLA_V)
    k_nope = kv[..., :MLA_NOPE].reshape(b, s, MLA_HEADS * MLA_NOPE)
    mv = kv[..., MLA_NOPE:].reshape(b, s, MLA_HEADS * MLA_V)
    kh = jnp.concatenate([_heads(k_nope, MLA_HEADS),
                          jnp.broadcast_to(kr[:, None], (b, MLA_HEADS, s, MLA_ROPE)).reshape(b * MLA_HEADS, s, MLA_ROPE)],
                         axis=-1)
    cos, sin = _rope_tables(n_ctx, s - n_ctx, MLA_QK, MLA_NOPE, MLA_ROPE)
    rot = jnp.asarray(_rope_matrix(MLA_QK, MLA_NOPE, MLA_ROPE))
    nr = _op(ops, _fn_norm_rope, "norm_rope", ("row", "tab", "tab", "const", "glob"), (MLA_QK,), whole_seq=True)
    mq = nr(_heads(q, MLA_HEADS), cos, sin, rot, w["mla_g_q"][None])[0]
    mk = nr(kh, cos, sin, rot, w["mla_g_k"][None])[0]
    mla = attention(mq.reshape(b, MLA_HEADS, s, MLA_QK), mk.reshape(b, MLA_HEADS, s, MLA_QK),
                    _heads(mv, MLA_HEADS).reshape(b, MLA_HEADS, s, MLA_V), 1, n_ctx, MLA_QK ** -0.5)
    mla = _unheads(mla.reshape(b * MLA_HEADS, s, MLA_V), b)
    nh = _op(ops, _fn_norm, "normh", ("row", "glob"), (HEAD_DIM,), whole_seq=True)
    nqh = nh(_heads(nq, NA_HEADS), w["na_g_q"][None])[0]
    nkh = nh(_heads(nk, NA_HEADS), w["na_g_k"][None])[0]
    r4 = lambda t: t.reshape(b, NA_HEADS, s, HEAD_DIM)
    na = na_attention(r4(nqh), r4(nkh), r4(_heads(nv, NA_HEADS)), na_bias_table(w["na_rpb"]), n_ctx)
    na = _unheads(na.reshape(b * NA_HEADS, s, HEAD_DIM), b)
    mix = jnp.concatenate([mla, na], axis=-1)
    return linear(mix.reshape(b * s, -1), w["o_w_out"]).reshape(b, s, d)


_EVEN_KEYS = ("e_w_in", "e_w_out", "e_g_q", "e_g_k", "ssm_lam_re", "ssm_lam_im", "ssm_log_dt", "ssm_b_re", "ssm_b_im",
              "ssm_c_re", "ssm_c_im", "ssm_d", "ssm_w_glu", "ssm_b_glu")
_ODD_KEYS = ("o_w_in", "o_w_out", "mla_g_cq", "mla_g_ckv", "mla_w_uq", "mla_w_ukv", "mla_g_q", "mla_g_k", "na_g_q",
             "na_g_k", "na_rpb")


def _trunk(x_all, mods, w, n_ctx):
    ops = {}
    depth = mods.shape[0]
    b, s, d = x_all.shape
    modulate = _op(ops, _fn_modulate, "modulate", ("row", "glob", "seg", "seg"), (d,), nctx_rows=n_ctx)
    gated = _op(ops, _fn_gated_add, "gated", ("row", "row", "seg"), (d,), nctx_rows=n_ctx)
    x = x_all
    for i in range(depth):
        j = i // 2
        m = [mods[i][:, :, r:r + 1, :] for r in range(N_MOD)]
        a = modulate(x, w["g_norm1"][i][None], m[0], m[1])[0]
        if i % 2 == 0:
            o = _even_mixer(ops, a, {k: w[k][j] for k in _EVEN_KEYS}, n_ctx)
        else:
            o = _odd_mixer(ops, a, {k: w[k][j] for k in _ODD_KEYS}, n_ctx)
        x = gated(x, o, m[2])[0]
        a2 = modulate(x, w["g_norm2"][i][None], m[3], m[4])[0]
        f = ffn(a2.reshape(b * s, d), w["w_ff1"][i], w["w_ff2"][i]).reshape(b, s, d)
        x = gated(x, f, m[5])[0]
    return x[:, n_ctx:]


def local_step(x, ctx, mods, w, loss_target):
    n_ctx = ctx.shape[1]
    x_all = jnp.concatenate([ctx, x], axis=1)
    y, vjp = jax.vjp(lambda xa, md, ww: _trunk(xa, md, ww, n_ctx), x_all, mods, w)
    loss_tile, dy = _loss_head(y, loss_target)
    dx_all, dmods, dw = vjp(dy)
    return loss_tile[0, 0], dx_all[:, n_ctx:], dmods, dw


_SHARDED = (("w_ff1", 2), ("w_ff2", 1), ("e_w_in", 2), ("e_w_out", 1), ("o_w_in", 2), ("o_w_out", 1),
            ("mla_w_uq", 2), ("mla_w_ukv", 2), ("ssm_w_glu", 1))
_SHARDED_SMALL = (("mla_g_cq", 1), ("mla_g_ckv", 1))
_REPLICATED = ("g_norm1", "g_norm2", "e_g_q", "e_g_k", "ssm_lam_re", "ssm_lam_im", "ssm_log_dt", "ssm_b_re", "ssm_b_im",
               "ssm_c_re", "ssm_c_im", "ssm_d", "ssm_b_glu", "mla_g_q", "mla_g_k", "na_g_q", "na_g_k", "na_rpb")
_WEIGHTS = ("c_ctx", "w_mod", "b_mod", "g_norm1", "g_norm2", "w_ff1", "w_ff2", "e_w_in", "e_w_out", "e_g_q", "e_g_k",
            "ssm_lam_re", "ssm_lam_im", "ssm_log_dt", "ssm_b_re", "ssm_b_im", "ssm_c_re", "ssm_c_im", "ssm_d",
            "ssm_w_glu", "ssm_b_glu", "o_w_in", "o_w_out", "mla_g_cq", "mla_g_ckv", "mla_w_uq", "mla_w_ukv", "mla_g_q",
            "mla_g_k", "na_g_q", "na_g_k", "na_rpb")
_PACK_ROWS = 64


def _pack(arrs, dtype, cols=1024, row_mult=_PACK_ROWS):
    flat = jnp.concatenate([a.reshape(-1).astype(dtype) for a in arrs])
    unit = cols * row_mult
    pad = (-flat.shape[0]) % unit
    return jnp.pad(flat, (0, pad)).reshape(-1, cols)


def _unpack(flat, shapes):
    flat = flat.reshape(-1)
    out, off = [], 0
    for sh in shapes:
        n = int(np.prod(sh))
        out.append(flat[off:off + n].reshape(sh))
        off += n
    return out


def _silu(t):
    return t * jax.nn.sigmoid(t)


def kernel(x, c, ctx, c_ctx, w_mod, b_mod, g_norm1, g_norm2, w_ff1, w_ff2, e_w_in, e_w_out, e_g_q, e_g_k, ssm_lam_re, ssm_lam_im, ssm_log_dt, ssm_b_re, ssm_b_im, ssm_c_re, ssm_c_im, ssm_d, ssm_w_glu, ssm_b_glu, o_w_in, o_w_out, mla_g_cq, mla_g_ckv, mla_w_uq, mla_w_ukv, mla_g_q, mla_g_k, na_g_q, na_g_k, na_rpb, loss_target, m_c_ctx, m_w_mod, m_b_mod, m_g_norm1, m_g_norm2, m_w_ff1, m_w_ff2, m_e_w_in, m_e_w_out, m_e_g_q, m_e_g_k, m_ssm_lam_re, m_ssm_lam_im, m_ssm_log_dt, m_ssm_b_re, m_ssm_b_im, m_ssm_c_re, m_ssm_c_im, m_ssm_d, m_ssm_w_glu, m_ssm_b_glu, m_o_w_in, m_o_w_out, m_mla_g_cq, m_mla_g_ckv, m_mla_w_uq, m_mla_w_ukv, m_mla_g_q, m_mla_g_k, m_na_g_q, m_na_g_k, m_na_rpb, v_c_ctx, v_w_mod, v_b_mod, v_g_norm1, v_g_norm2, v_w_ff1, v_w_ff2, v_e_w_in, v_e_w_out, v_e_g_q, v_e_g_k, v_ssm_lam_re, v_ssm_lam_im, v_ssm_log_dt, v_ssm_b_re, v_ssm_b_im, v_ssm_c_re, v_ssm_c_im, v_ssm_d, v_ssm_w_glu, v_ssm_b_glu, v_o_w_in, v_o_w_out, v_mla_g_cq, v_mla_g_ckv, v_mla_w_uq, v_mla_w_ukv, v_mla_g_q, v_mla_g_k, v_na_g_q, v_na_g_k, v_na_rpb):
    env = dict(locals())
    weights = {n: env[n] for n in _WEIGHTS}
    mom_m = {n: env["m_" + n] for n in _WEIGHTS}
    mom_v = {n: env["v_" + n] for n in _WEIGHTS}
    ax, ay, ac = _me()
    plane = 2 * ax + ay
    dev = 4 * ax + 2 * ay + ac
    b_loc, d = c.shape
    depth = w_mod.shape[0]
    n_all = N_DEV * b_loc
    mod_cols = w_mod.shape[2]

    big = _pack([weights[n] for n, _ in _SHARDED], BF16)
    small = _pack([weights[n] for n, _ in _SHARDED_SMALL], F32, cols=LANE, row_mult=8)
    g_big, g_small = plane_allgather(big, small)
    full = {n: weights[n] for n in _REPLICATED}
    parts = [_unpack(g_big[j], [weights[n].shape for n, _ in _SHARDED]) for j in range(N_PLANE)]
    for t, (n, axis) in enumerate(_SHARDED):
        full[n] = jnp.concatenate([parts[j][t] for j in range(N_PLANE)], axis=axis).astype(F32)
    parts_s = [_unpack(g_small[j], [weights[n].shape for n, _ in _SHARDED_SMALL]) for j in range(N_PLANE)]
    for t, (n, axis) in enumerate(_SHARDED_SMALL):
        full[n] = jnp.concatenate([parts_s[j][t] for j in range(N_PLANE)], axis=axis)

    rows_pad = 8 * ((n_all + 1 + 7) // 8)
    c_all = allgather8(jnp.pad(c, ((0, 8 - b_loc), (0, 0)))).reshape(N_DEV, 8, d)[:, :b_loc].reshape(n_all, d)
    cond_raw = jnp.concatenate([c_all, c_ctx[None], jnp.zeros((rows_pad - n_all - 1, d), F32)], axis=0)
    b_cols = lax.dynamic_slice_in_dim(b_mod, plane * mod_cols, mod_cols, axis=1)
    mod_loc = jnp.stack([_mm(cond_raw, w_mod[i], a_act="silu") + b_cols[i][None] for i in range(depth)])
    mod_g = allgather8(mod_loc.reshape(depth * rows_pad, mod_cols)).reshape(N_PLANE, 2, depth, rows_pad, mod_cols)
    mod_all = jnp.concatenate([mod_g[j, 0] for j in range(N_PLANE)], axis=-1)
    m_lat = lax.dynamic_slice_in_dim(mod_all, dev * b_loc, b_loc, axis=1)
    m_ctx = jnp.broadcast_to(mod_all[:, n_all][:, None], m_lat.shape)
    mods = jnp.stack([m_ctx, m_lat], axis=2).reshape(depth, b_loc, 2, N_MOD, d)

    loss_part, grad_x, dmods, dw = local_step(x, ctx, mods, full, loss_target)
    loss = lax.psum(loss_part, ("x", "y", "c"))

    dm = dmods.reshape(depth, b_loc, 2, N_MOD * d)
    dm_rows = jnp.concatenate([dm[:, :, 1], jnp.sum(dm[:, :, 0], axis=1, keepdims=True)], axis=1)
    rep_shapes = [weights[n].shape for n in _REPLICATED]
    small_pack = _pack([dm_rows] + [dw[n] for n in _REPLICATED], F32, cols=1024, row_mult=8)
    sp_rows = small_pack.shape[0]
    gathered = allgather8(small_pack).reshape(N_DEV, sp_rows, 1024)
    n_dm = depth * (b_loc + 1) * N_MOD * d
    dm_all = gathered.reshape(N_DEV, -1)[:, :n_dm].reshape(N_DEV, depth, b_loc + 1, N_MOD * d)
    rep_sum = _sum_rows(gathered, N_DEV).reshape(-1)
    rep_grads = dict(zip(_REPLICATED, _unpack(rep_sum[n_dm:], rep_shapes)))
    d_ctx_row = rep_sum[:n_dm].reshape(depth, b_loc + 1, N_MOD * d)[:, b_loc]
    d_lat_rows = jnp.transpose(dm_all[:, :, :b_loc], (1, 0, 2, 3)).reshape(depth, n_all, N_MOD * d)
    d_mod_all = jnp.concatenate([d_lat_rows, d_ctx_row[:, None],
                                 jnp.zeros((depth, rows_pad - n_all - 1, N_MOD * d), F32)], axis=1)
    grads = dict(rep_grads)
    grads["b_mod"] = jnp.sum(d_mod_all, axis=1)
    d_cols = lax.dynamic_slice_in_dim(d_mod_all, plane * mod_cols, mod_cols, axis=2)
    grads["w_mod"] = jnp.stack([_mm(cond_raw, d_cols[i], ta=True, a_act="silu") for i in range(depth)])
    d_cond = _mm(d_cols[0], w_mod[0], tb=True)
    for i in range(1, depth):
        d_cond = _add2(d_cond, _mm(d_cols[i], w_mod[i], tb=True))
    d_cond_g = allgather8(d_cond[n_all:n_all + 8] if rows_pad - n_all >= 8 else
                          jnp.pad(d_cond[n_all:], ((0, 8 - (rows_pad - n_all)), (0, 0)))).reshape(N_PLANE, 2, 8, d)
    d_silu = _sum_rows(d_cond_g[:, 0], N_PLANE)[0]
    sg = jax.nn.sigmoid(c_ctx)
    grads["c_ctx"] = d_silu * (sg * (1.0 + c_ctx * (1.0 - sg)))

    def shard_of(g, axis, j):
        n = g.shape[axis] // N_PLANE
        return lax.slice_in_dim(g, j * n, (j + 1) * n, axis=axis)

    send = jnp.stack([_pack([shard_of(dw[n], axis, j) for n, axis in _SHARDED]
                            + [shard_of(dw[n], axis, j) for n, axis in _SHARDED_SMALL], F32) for j in range(N_PLANE)])
    rows = send.shape[1]
    send = jnp.transpose(send.reshape(N_PLANE, 2, rows // 2, 1024), (1, 0, 2, 3))
    mine, got = sibling_split(send)
    chip_sum = _add2(mine, got)
    arrived = plane_scatter(chip_sum)
    total = _sum_rows(arrived.reshape(N_PLANE, rows // 2, 1024), N_PLANE)
    flat = sibling_join(total).reshape(-1)
    shard_shapes = [weights[n].shape for n, _ in _SHARDED] + [weights[n].shape for n, _ in _SHARDED_SMALL]
    for (n, _), g in zip(_SHARDED + _SHARDED_SMALL, _unpack(flat, shard_shapes)):
        grads[n] = g

    big_names = ("w_mod",) + tuple(n for n, _ in _SHARDED)
    small_names = tuple(n for n in _WEIGHTS if n not in big_names)
    delta, new_m, new_v = {}, {}, {}
    for n in big_names:
        delta[n], new_m[n], new_v[n] = _adamw(weights[n], grads[n], mom_m[n], mom_v[n])
    sm_shapes = [weights[n].shape for n in small_names]
    packed = [_pack([src[n] for n in small_names], F32, cols=1024, row_mult=8)
              for src in (weights, grads, mom_m, mom_v)]
    for dst, res in zip((delta, new_m, new_v), _adamw(*packed)):
        dst.update(dict(zip(small_names, _unpack(res, sm_shapes))))

    return (loss, grad_x, *[grads[n] for n in _WEIGHTS], *[delta[n] for n in _WEIGHTS],
            *[new_m[n] for n in _WEIGHTS], *[new_v[n] for n in _WEIGHTS])
```

```python
import functools
import math

import numpy as np
import jax
import jax.numpy as jnp
from jax import lax
from jax.experimental import pallas as pl
from jax.experimental.pallas import tpu as pltpu

F32 = jnp.float32
BF16 = jnp.bfloat16
HI = lax.Precision.HIGHEST
MESH = pl.DeviceIdType.MESH
ANY = pl.BlockSpec(memory_space=pl.ANY)
VMEM_SPEC = pl.BlockSpec(memory_space=pltpu.VMEM)

GRID_W = 64
HEAD_DIM = 64
ROPE_BASE = 10000.0
EPS = 1e-6
N_MOD = 6
GQA_Q_HEADS, GQA_KV_HEADS = 12, 4
GQA_Q_W, GQA_KV_W = GQA_Q_HEADS * HEAD_DIM, GQA_KV_HEADS * HEAD_DIM
SSM_WIDTH, SSM_GROUP, SSM_STATE = 256, 16, 64
SSM_GROUPS = SSM_WIDTH // SSM_GROUP
SSM_LANES = SSM_GROUPS * SSM_STATE
MLA_HEADS, MLA_Q_RANK, MLA_KV_RANK, MLA_NOPE, MLA_ROPE, MLA_V = 8, 512, 256, 64, 32, 64
MLA_QK = MLA_NOPE + MLA_ROPE
NA_HEADS, NA_WIN_R, NA_WIN_C = 8, 8, 16
NA_W = NA_HEADS * HEAD_DIM
NA_BAND = NA_WIN_R * GRID_W
ODD_IN_W = MLA_Q_RANK + MLA_KV_RANK + MLA_ROPE + 3 * NA_W
ODD_IN_PAD = 2560
ADAM_LR, ADAM_B1, ADAM_B2, ADAM_EPS, ADAM_WD, ADAM_STEP = 0.001, 0.9, 0.999, 1e-08, 0.01, 10
NEG = -1e30
VMEM_LIMIT = 56 * 1024 * 1024
LANE = 128
MM_TILE_M = (1152, 1024, 768, 512, 256, 128)
MM_TILE_N = (1280, 1024, 768, 512, 256, 128)
MM_TILE_K = (1152, 1024, 768, 512, 256, 128)
N_PLANE = 4
N_DEV = 8


def _pick(n, cands):
    for c in cands:
        if n % c == 0:
            return c
    return n


def _params(**kw):
    return pltpu.CompilerParams(vmem_limit_bytes=VMEM_LIMIT, **kw)


def _mm(a, b, *, ta=False, tb=False, a_act=None, epi=None, e=None, exact=False):
    m, kd = (a.shape[1], a.shape[0]) if ta else a.shape
    n = b.shape[0] if tb else b.shape[1]
    tm = _pick(m, MM_TILE_M)
    tn = _pick(n, MM_TILE_N)
    tk = _pick(kd, MM_TILE_K)
    nk = kd // tk
    dn = (((0 if ta else 1,), (1 if tb else 0,)), ((), ()))

    def body(*refs):
        if epi is None:
            a_ref, b_ref, o_ref = refs
        else:
            a_ref, b_ref, e_ref, o_ref = refs
        k = pl.program_id(2)
        av = a_ref[...]
        if a_act == "relu2":
            av = jnp.square(jnp.maximum(av, 0.0))
        elif a_act == "silu":
            av = av * jax.nn.sigmoid(av)
        bv = b_ref[...]
        if exact:
            p = lax.dot_general(av, bv, dn, precision=HI, preferred_element_type=F32)
        else:
            p = lax.dot_general(av.astype(BF16), bv.astype(BF16), dn, preferred_element_type=F32)

        @pl.when(k == 0)
        def _():
            o_ref[...] = p

        @pl.when(k > 0)
        def _():
            o_ref[...] += p

        if epi == "drelu2":
            @pl.when(k == nk - 1)
            def _():
                o_ref[...] = o_ref[...] * (2.0 * jnp.maximum(e_ref[...], 0.0))

    a_spec = pl.BlockSpec((tk, tm), lambda i, j, k: (k, i)) if ta else pl.BlockSpec((tm, tk), lambda i, j, k: (i, k))
    b_spec = pl.BlockSpec((tn, tk), lambda i, j, k: (j, k)) if tb else pl.BlockSpec((tk, tn), lambda i, j, k: (k, j))
    o_spec = pl.BlockSpec((tm, tn), lambda i, j, k: (i, j))
    ins, specs = [a, b], [a_spec, b_spec]
    if epi is not None:
        ins.append(e)
        specs.append(o_spec)
    name = f"mm_{m}x{kd}x{n}_{int(ta)}{int(tb)}_{a_act}_{epi}_{int(exact)}"
    return pl.pallas_call(
        body, out_shape=jax.ShapeDtypeStruct((m, n), F32), grid=(m // tm, n // tn, nk),
        in_specs=specs, out_specs=o_spec, name=name, compiler_params=_params(),
    )(*ins)


@functools.partial(jax.custom_vjp, nondiff_argnums=(2,))
def _linear(a, w, exact):
    return _mm(a, w, exact=exact)


def _linear_fwd(a, w, exact):
    return _mm(a, w, exact=exact), (a, w)


def _linear_bwd(exact, res, g):
    a, w = res
    return _mm(g, w, tb=True, exact=exact), _mm(a, g, ta=True, exact=exact)


_linear.defvjp(_linear_fwd, _linear_bwd)


def linear(a, w, exact=False):
    return _linear(a, w, exact)


@jax.custom_vjp
def ffn(a, w1, w2):
    return _mm(_mm(a, w1), w2, a_act="relu2")


def _ffn_fwd(a, w1, w2):
    h1 = _mm(a, w1)
    return _mm(h1, w2, a_act="relu2"), (a, w1, w2, h1)


def _ffn_bwd(res, g):
    a, w1, w2, h1 = res
    dh1 = _mm(g, w2, tb=True, epi="drelu2", e=h1)
    dw2 = _mm(h1, g, ta=True, a_act="relu2")
    return _mm(dh1, w1, tb=True), _mm(a, dh1, ta=True), dw2


ffn.defvjp(_ffn_fwd, _ffn_bwd)


def make_rowwise(fn, name, kinds, out_dims, nctx_rows=0, whole_seq=False):
    n_in = len(kinds)
    n_out = len(out_dims)
    diff = [i for i, kd in enumerate(kinds) if kd in ("row", "glob", "seg")]

    def layout(args):
        row0 = args[kinds.index("row")]
        g, s = row0.shape[0], row0.shape[1]
        ts = s if whole_seq else (min(256, nctx_rows) if nctx_rows else _pick(s, (256, 128, 64)))
        nctx = nctx_rows // ts
        return g, s, ts, nctx

    def spec_of(kind, arr, ts, nctx):
        if kind == "row":
            return pl.BlockSpec((None, ts, arr.shape[2]), lambda g, i: (g, i, 0))
        if kind == "tab":
            return pl.BlockSpec((ts, arr.shape[1]), lambda g, i: (i, 0))
        if kind in ("const", "glob"):
            return pl.BlockSpec(arr.shape, lambda g, i: (0, 0))
        return pl.BlockSpec((None, None) + arr.shape[2:], lambda g, i: (g, (i >= nctx).astype(jnp.int32), 0, 0))

    def fwd_call(*args):
        g, s, ts, nctx = layout(args)

        def body(*refs):
            vals = [r[...] for r in refs[:n_in]]
            outs = fn(*vals)
            for o_ref, o in zip(refs[n_in:], outs):
                o_ref[...] = o

        return pl.pallas_call(
            body, out_shape=[jax.ShapeDtypeStruct((g, s, d), F32) for d in out_dims], grid=(g, s // ts),
            in_specs=[spec_of(kd, a, ts, nctx) for kd, a in zip(kinds, args)],
            out_specs=[pl.BlockSpec((None, ts, d), lambda g_, i: (g_, i, 0)) for d in out_dims],
            name=f"{name}_f_{g}x{s}", compiler_params=_params(),
        )(*args)

    def bwd_call(args, cts):
        g, s, ts, nctx = layout(args)

        def body(*refs):
            in_refs, ct_refs, out_refs = refs[:n_in], refs[n_in:n_in + n_out], refs[n_in + n_out:]
            gi, i = pl.program_id(0), pl.program_id(1)
            vals = [r[...] for r in in_refs]

            def f(*dv):
                full = list(vals)
                for idx, v in zip(diff, dv):
                    full[idx] = v
                return tuple(fn(*full))

            _, vjp = jax.vjp(f, *[vals[idx] for idx in diff])
            grads = vjp(tuple(r[...] for r in ct_refs))
            for idx, o_ref, gr in zip(diff, out_refs, grads):
                if kinds[idx] == "row":
                    o_ref[...] = gr
                    continue
                if kinds[idx] == "glob":
                    first = jnp.logical_and(gi == 0, i == 0)
                else:
                    first = jnp.logical_or(i == 0, i == nctx)

                @pl.when(first)
                def _(o_ref=o_ref, gr=gr):
                    o_ref[...] = gr

                @pl.when(jnp.logical_not(first))
                def _(o_ref=o_ref, gr=gr):
                    o_ref[...] += gr

        in_specs = [spec_of(kd, a, ts, nctx) for kd, a in zip(kinds, args)]
        in_specs += [pl.BlockSpec((None, ts, d), lambda g_, i: (g_, i, 0)) for d in out_dims]
        return pl.pallas_call(
            body, out_shape=[jax.ShapeDtypeStruct(args[idx].shape, F32) for idx in diff], grid=(g, s // ts),
            in_specs=in_specs, out_specs=[spec_of(kinds[idx], args[idx], ts, nctx) for idx in diff],
            name=f"{name}_b_{g}x{s}", compiler_params=_params(),
        )(*args, *cts)

    @jax.custom_vjp
    def op(*args):
        return tuple(fwd_call(*args))

    def op_fwd(*args):
        return tuple(fwd_call(*args)), args

    def op_bwd(args, cts):
        grads = bwd_call(args, cts)
        full = [None] * n_in
        for idx, gr in zip(diff, grads):
            full[idx] = gr
        return tuple(jnp.zeros_like(a) if gfull is None else gfull for a, gfull in zip(args, full))

    op.defvjp(op_fwd, op_bwd)
    return op


def _rms(x):
    return lax.rsqrt(jnp.mean(x * x, axis=-1, keepdims=True) + EPS)


def _fn_modulate(x, g, shift, scale):
    return ((x * _rms(x) * g) * (1.0 + scale) + shift,)


def _fn_gated_add(x, o, gate):
    return (x + gate * o,)


def _fn_norm(x, g):
    return (x * _rms(x) * g,)


def _fn_norm_rope(x, cos, sin, rot, g):
    y = x * _rms(x) * g
    r = jnp.dot(y, rot, precision=HI, preferred_element_type=F32)
    return (y * cos + r * sin,)


def _fn_glu_pre(u, y0, y1, d):
    return (jax.nn.gelu(d * u + y0 + y1),)


def _fn_glu_post(z, t, bg):
    return (z * jax.nn.sigmoid(t + bg),)


def _rope_matrix(dh, start, rot_dim):
    r = np.zeros((dh, dh), np.float32)
    q = rot_dim // 4
    for j in range(rot_dim):
        if (j // q) % 2 == 0:
            r[start + j + q, start + j] = -1.0
        else:
            r[start + j - q, start + j] = 1.0
    return r


def _rope_tables(n_ctx, n_lat, dh, start, rot_dim):
    t = jnp.arange(n_lat)
    rows = (t // GRID_W).astype(F32)
    cols = (t % GRID_W).astype(F32)
    axis_dim = rot_dim // 2
    freqs = ROPE_BASE ** (-jnp.arange(0, axis_dim, 2, dtype=F32) / axis_dim)
    ang_r = rows[:, None] * freqs
    ang_c = cols[:, None] * freqs
    ang = jnp.concatenate([ang_r, ang_r, ang_c, ang_c], axis=-1)
    cos = jnp.concatenate([jnp.ones((n_lat, start), F32), jnp.cos(ang)], axis=-1)
    sin = jnp.concatenate([jnp.zeros((n_lat, start), F32), jnp.sin(ang)], axis=-1)
    cos = jnp.concatenate([jnp.ones((n_ctx, dh), F32), cos], axis=0)
    sin = jnp.concatenate([jnp.zeros((n_ctx, dh), F32), sin], axis=0)
    return cos, sin


def _attn_scores(qv, kv, scale, mask):
    s = lax.dot_general(qv, kv, (((1,), (1,)), ((), ())), preferred_element_type=F32) * scale
    return jnp.where(mask, NEG, s)


def _attn_fwd(q, k, v, group, n_ctx, scale):
    b, h, s, dq = q.shape
    dv = v.shape[-1]
    tq = min(256, n_ctx)
    nc = n_ctx // tq

    def body(q_ref, k_ref, v_ref, o_ref, lse_ref):
        i = pl.program_id(2)
        col = lax.broadcasted_iota(jnp.int32, (tq, s), 1)
        mask = jnp.logical_and(col >= n_ctx, i < nc)
        sc = _attn_scores(q_ref[...].astype(BF16), k_ref[...].astype(BF16), scale, mask)
        m = jnp.max(sc, axis=-1, keepdims=True)
        p = jnp.exp(sc - m)
        l = jnp.sum(p, axis=-1, keepdims=True)
        o = jnp.dot(p.astype(BF16), v_ref[...].astype(BF16), preferred_element_type=F32)
        o_ref[...] = o / l
        lse_ref[...] = m + jnp.log(l)

    return pl.pallas_call(
        body, out_shape=[jax.ShapeDtypeStruct((b, h, s, dv), F32), jax.ShapeDtypeStruct((b, h, s, 1), F32)],
        grid=(b, h, s // tq),
        in_specs=[pl.BlockSpec((None, None, tq, dq), lambda bi, hi, i: (bi, hi, i, 0)),
                  pl.BlockSpec((None, None, s, dq), lambda bi, hi, i: (bi, lax.div(hi, group), 0, 0)),
                  pl.BlockSpec((None, None, s, dv), lambda bi, hi, i: (bi, lax.div(hi, group), 0, 0))],
        out_specs=[pl.BlockSpec((None, None, tq, dv), lambda bi, hi, i: (bi, hi, i, 0)),
                   pl.BlockSpec((None, None, tq, 1), lambda bi, hi, i: (bi, hi, i, 0))],
        name=f"attn_f_{h}x{s}x{dq}", compiler_params=_params(),
    )(q, k, v)


def _attn_dq(q, k, v, o, lse, do, group, n_ctx, scale):
    b, h, s, dq = q.shape
    dv = v.shape[-1]
    tq = min(256, n_ctx)
    nc = n_ctx // tq

    def body(q_ref, k_ref, v_ref, o_ref, lse_ref, do_ref, dq_ref, delta_ref):
        i = pl.program_id(2)
        col = lax.broadcasted_iota(jnp.int32, (tq, s), 1)
        mask = jnp.logical_and(col >= n_ctx, i < nc)
        kv = k_ref[...].astype(BF16)
        sc = _attn_scores(q_ref[...].astype(BF16), kv, scale, mask)
        p = jnp.exp(sc - lse_ref[...])
        dov = do_ref[...]
        delta = jnp.sum(dov * o_ref[...], axis=-1, keepdims=True)
        dp = lax.dot_general(dov.astype(BF16), v_ref[...].astype(BF16), (((1,), (1,)), ((), ())),
                             preferred_element_type=F32)
        ds = p * (dp - delta) * scale
        dq_ref[...] = jnp.dot(ds.astype(BF16), kv, preferred_element_type=F32)
        delta_ref[...] = delta

    qs = lambda d: pl.BlockSpec((None, None, tq, d), lambda bi, hi, i: (bi, hi, i, 0))
    ks = lambda d: pl.BlockSpec((None, None, s, d), lambda bi, hi, i: (bi, lax.div(hi, group), 0, 0))
    return pl.pallas_call(
        body, out_shape=[jax.ShapeDtypeStruct((b, h, s, dq), F32), jax.ShapeDtypeStruct((b, h, s, 1), F32)],
        grid=(b, h, s // tq),
        in_specs=[qs(dq), ks(dq), ks(dv), qs(dv), qs(1), qs(dv)], out_specs=[qs(dq), qs(1)],
        name=f"attn_dq_{h}x{s}x{dq}", compiler_params=_params(),
    )(q, k, v, o, lse, do)


def _attn_dkv(q, k, v, lse, delta, do, group, n_ctx, scale):
    b, h, s, dq = q.shape
    hk = k.shape[1]
    dv = v.shape[-1]
    tk = min(256, n_ctx)
    nc = n_ctx // tk

    def body(q_ref, k_ref, v_ref, lse_ref, delta_ref, do_ref, dk_ref, dv_ref):
        j = pl.program_id(2)
        row = lax.broadcasted_iota(jnp.int32, (s, tk), 0)
        mask = jnp.logical_and(row < n_ctx, j >= nc)
        kv = k_ref[...].astype(BF16)
        vv = v_ref[...].astype(BF16)
        dk = jnp.zeros((tk, dq), F32)
        dvv = jnp.zeros((tk, dv), F32)
        for g in range(group):
            qg = q_ref[g].astype(BF16)
            dog = do_ref[g].astype(BF16)
            sc = _attn_scores(qg, kv, scale, mask)
            p = jnp.exp(sc - lse_ref[g])
            dvv = dvv + lax.dot_general(p.astype(BF16), dog, (((0,), (0,)), ((), ())), preferred_element_type=F32)
            dp = lax.dot_general(dog, vv, (((1,), (1,)), ((), ())), preferred_element_type=F32)
            ds = p * (dp - delta_ref[g]) * scale
            dk = dk + lax.dot_general(ds.astype(BF16), qg, (((0,), (0,)), ((), ())), preferred_element_type=F32)
        dk_ref[...] = dk
        dv_ref[...] = dvv

    gs = lambda d: pl.BlockSpec((None, group, s, d), lambda bi, hi, j: (bi, hi, 0, 0))
    ks = lambda d: pl.BlockSpec((None, None, tk, d), lambda bi, hi, j: (bi, hi, j, 0))
    return pl.pallas_call(
        body, out_shape=[jax.ShapeDtypeStruct((b, hk, s, dq), F32), jax.ShapeDtypeStruct((b, hk, s, dv), F32)],
        grid=(b, hk, s // tk),
        in_specs=[gs(dq), ks(dq), ks(dv), gs(1), gs(1), gs(dv)], out_specs=[ks(dq), ks(dv)],
        name=f"attn_dkv_{h}x{s}x{dq}", compiler_params=_params(),
    )(q, k, v, lse, delta, do)


@functools.partial(jax.custom_vjp, nondiff_argnums=(3, 4, 5))
def attention(q, k, v, group, n_ctx, scale):
    return _attn_fwd(q, k, v, group, n_ctx, scale)[0]


def _attention_fwd(q, k, v, group, n_ctx, scale):
    o, lse = _attn_fwd(q, k, v, group, n_ctx, scale)
    return o, (q, k, v, o, lse)


def _attention_bwd(group, n_ctx, scale, res, do):
    q, k, v, o, lse = res
    dq, delta = _attn_dq(q, k, v, o, lse, do, group, n_ctx, scale)
    dk, dv = _attn_dkv(q, k, v, lse, delta, do, group, n_ctx, scale)
    return dq, dk, dv


attention.defvjp(_attention_fwd, _attention_bwd)


def _na_geometry(i, nc, rows):
    r = i - nc
    rs = jnp.clip(r - NA_WIN_R // 2, 0, rows - NA_WIN_R)
    is_ctx = i < nc
    cls = jnp.where(is_ctx, NA_WIN_R, r - rs)
    return jnp.where(is_ctx, 0, rs), cls


def _na_scores(q_ref, k_ref, bias_ref, n_ctx, start, scale):
    qv = q_ref[...].astype(BF16)
    kc = k_ref[0:n_ctx, :].astype(BF16)
    kb = k_ref[pl.ds(start, NA_BAND), :].astype(BF16)
    dn = (((1,), (1,)), ((), ()))
    s_c = lax.dot_general(qv, kc, dn, preferred_element_type=F32) * scale
    s_l = lax.dot_general(qv, kb, dn, preferred_element_type=F32) * scale + bias_ref[...]
    return qv, kc, kb, s_c, s_l


def _na_fwd(q, k, v, bias, n_ctx):
    b, h, s, dh = q.shape
    nc = n_ctx // GRID_W
    rows = (s - n_ctx) // GRID_W
    scale = dh ** -0.5

    def body(q_ref, k_ref, v_ref, bias_ref, o_ref, lse_ref):
        rs, _ = _na_geometry(pl.program_id(2), nc, rows)
        start = pl.multiple_of(n_ctx + rs * GRID_W, GRID_W)
        _, _, _, s_c, s_l = _na_scores(q_ref, k_ref, bias_ref, n_ctx, start, scale)
        m = jnp.maximum(jnp.max(s_c, axis=-1, keepdims=True), jnp.max(s_l, axis=-1, keepdims=True))
        p_c = jnp.exp(s_c - m)
        p_l = jnp.exp(s_l - m)
        l = jnp.sum(p_c, axis=-1, keepdims=True) + jnp.sum(p_l, axis=-1, keepdims=True)
        o = jnp.dot(p_c.astype(BF16), v_ref[0:n_ctx, :].astype(BF16), preferred_element_type=F32)
        o = o + jnp.dot(p_l.astype(BF16), v_ref[pl.ds(start, NA_BAND), :].astype(BF16), preferred_element_type=F32)
        o_ref[...] = o / l
        lse_ref[...] = m + jnp.log(l)

    qs = lambda d: pl.BlockSpec((None, None, GRID_W, d), lambda bi, hi, i: (bi, hi, i, 0))
    ks = pl.BlockSpec((None, None, s, dh), lambda bi, hi, i: (bi, hi, 0, 0))
    bs = pl.BlockSpec((None, None, GRID_W, NA_BAND), lambda bi, hi, i: (hi, _na_geometry(i, nc, rows)[1], 0, 0))
    return pl.pallas_call(
        body, out_shape=[jax.ShapeDtypeStruct((b, h, s, dh), F32), jax.ShapeDtypeStruct((b, h, s, 1), F32)],
        grid=(b, h, s // GRID_W), in_specs=[qs(dh), ks, ks, bs], out_specs=[qs(dh), qs(1)],
        name=f"na_f_{s}", compiler_params=_params(),
    )(q, k, v, bias)


def _na_bwd(q, k, v, bias, o, lse, do, n_ctx):
    b, h, s, dh = q.shape
    nc = n_ctx // GRID_W
    rows = (s - n_ctx) // GRID_W
    scale = dh ** -0.5
    n_cls = NA_WIN_R + 1

    def body(q_ref, k_ref, v_ref, bias_ref, o_ref, lse_ref, do_ref, dq_ref, dk_ref, dv_ref, db_ref):
        i = pl.program_id(2)
        rs, cls = _na_geometry(i, nc, rows)
        _, cls_prev = _na_geometry(i - 1, nc, rows)
        start = pl.multiple_of(n_ctx + rs * GRID_W, GRID_W)

        @pl.when(i == 0)
        def _():
            dk_ref[...] = jnp.zeros_like(dk_ref)
            dv_ref[...] = jnp.zeros_like(dv_ref)

        qv, kc, kb, s_c, s_l = _na_scores(q_ref, k_ref, bias_ref, n_ctx, start, scale)
        lse_v = lse_ref[...]
        p_c = jnp.exp(s_c - lse_v)
        p_l = jnp.exp(s_l - lse_v)
        dov = do_ref[...]
        dob = dov.astype(BF16)
        delta = jnp.sum(dov * o_ref[...], axis=-1, keepdims=True)
        dn = (((1,), (1,)), ((), ()))
        dt = (((0,), (0,)), ((), ()))
        vc = v_ref[0:n_ctx, :].astype(BF16)
        vb = v_ref[pl.ds(start, NA_BAND), :].astype(BF16)
        ds_c = p_c * (lax.dot_general(dob, vc, dn, preferred_element_type=F32) - delta)
        ds_l = p_l * (lax.dot_general(dob, vb, dn, preferred_element_type=F32) - delta)
        dsc_b = ds_c.astype(BF16)
        dsl_b = ds_l.astype(BF16)
        dq_ref[...] = (jnp.dot(dsc_b, kc, preferred_element_type=F32)
                       + jnp.dot(dsl_b, kb, preferred_element_type=F32)) * scale
        dk_ref[0:n_ctx, :] += lax.dot_general(dsc_b, qv, dt, preferred_element_type=F32) * scale
        dk_ref[pl.ds(start, NA_BAND), :] += lax.dot_general(dsl_b, qv, dt, preferred_element_type=F32) * scale
        dv_ref[0:n_ctx, :] += lax.dot_general(p_c.astype(BF16), dob, dt, preferred_element_type=F32)
        dv_ref[pl.ds(start, NA_BAND), :] += lax.dot_general(p_l.astype(BF16), dob, dt, preferred_element_type=F32)
        first = jnp.logical_or(i == 0, cls != cls_prev)

        @pl.when(first)
        def _():
            db_ref[...] = ds_l

        @pl.when(jnp.logical_not(first))
        def _():
            db_ref[...] += ds_l

    qs = lambda d: pl.BlockSpec((None, None, GRID_W, d), lambda bi, hi, i: (bi, hi, i, 0))
    ks = pl.BlockSpec((None, None, s, dh), lambda bi, hi, i: (bi, hi, 0, 0))
    bs = pl.BlockSpec((None, None, GRID_W, NA_BAND), lambda bi, hi, i: (hi, _na_geometry(i, nc, rows)[1], 0, 0))
    dbs = pl.BlockSpec((None, None, None, GRID_W, NA_BAND),
                       lambda bi, hi, i: (bi, hi, _na_geometry(i, nc, rows)[1], 0, 0))
    return pl.pallas_call(
        body,
        out_shape=[jax.ShapeDtypeStruct((b, h, s, dh), F32), jax.ShapeDtypeStruct((b, h, s, dh), F32),
                   jax.ShapeDtypeStruct((b, h, s, dh), F32), jax.ShapeDtypeStruct((b, h, n_cls, GRID_W, NA_BAND), F32)],
        grid=(b, h, s // GRID_W), in_specs=[qs(dh), ks, ks, bs, qs(dh), qs(1), qs(dh)],
        out_specs=[qs(dh), ks, ks, dbs], name=f"na_b_{s}", compiler_params=_params(),
    )(q, k, v, bias, o, lse, do)


@functools.partial(jax.custom_vjp, nondiff_argnums=(4,))
def na_attention(q, k, v, bias, n_ctx):
    return _na_fwd(q, k, v, bias, n_ctx)[0]


def _na_attention_fwd(q, k, v, bias, n_ctx):
    o, lse = _na_fwd(q, k, v, bias, n_ctx)
    return o, (q, k, v, bias, o, lse)


def _na_attention_bwd(n_ctx, res, do):
    q, k, v, bias, o, lse = res
    dq, dk, dv, db = _na_bwd(q, k, v, bias, o, lse, do, n_ctx)
    return dq, dk, dv, jnp.sum(db, axis=0)


na_attention.defvjp(_na_attention_fwd, _na_attention_bwd)


def _na_onehots():
    q = np.arange(GRID_W)[:, None]
    col = np.arange(GRID_W)[None, :]
    cs = np.clip(q - NA_WIN_C // 2, 0, GRID_W - NA_WIN_C)
    valid = (col >= cs) & (col < cs + NA_WIN_C)
    cidx = col - q + (NA_WIN_C - 1)
    n_b = 2 * NA_WIN_C - 1
    col_hot = np.zeros((LANE, GRID_W * GRID_W), np.float32)
    for qq in range(GRID_W):
        for cc in range(GRID_W):
            if valid[qq, cc]:
                col_hot[cidx[qq, cc], qq * GRID_W + cc] = 1.0
    row_hot = np.zeros((NA_WIN_R, NA_WIN_R, 2 * NA_WIN_R - 1), np.float32)
    for c in range(NA_WIN_R):
        for j in range(NA_WIN_R):
            row_hot[c, j, j - c + NA_WIN_R - 1] = 1.0
    mask = np.where(valid, 0.0, NEG).astype(np.float32)
    return col_hot, row_hot, mask, n_b


def na_bias_table(rpb):
    h = rpb.shape[0]
    col_hot, row_hot, mask, n_b = _na_onehots()
    t1 = jnp.einsum("cja,hab->hcjb", jnp.asarray(row_hot), rpb)
    t1 = jnp.pad(t1.reshape(h * NA_WIN_R * NA_WIN_R, n_b), ((0, 0), (0, LANE - n_b)))
    t2 = linear(t1, jnp.asarray(col_hot), True)
    t2 = t2.reshape(h, NA_WIN_R, NA_WIN_R, GRID_W, GRID_W) + jnp.asarray(mask)
    tab = jnp.transpose(t2, (0, 1, 3, 2, 4)).reshape(h, NA_WIN_R, GRID_W, NA_BAND)
    return jnp.concatenate([tab, jnp.full((h, 1, GRID_W, NA_BAND), NEG, F32)], axis=1)


def _cmul(ar, ai, br, bi):
    return ar * br - ai * bi, ar * bi + ai * br


def _s5_chunk(n_ctx):
    return min(256, n_ctx)


def _s5_tables(a_re, a_im, t_len, rev):
    a_re, a_im = lax.stop_gradient(a_re), lax.stop_gradient(a_im)
    mag = jnp.sqrt(a_re * a_re + a_im * a_im)
    th = jnp.arctan2(a_im, a_re)
    t = jnp.arange(t_len + 1, dtype=F32)[:, None]
    pm = jnp.where(t == 0, 1.0, jnp.exp(t * jnp.log(jnp.maximum(mag, 1e-37))) * (mag > 0))
    pw = jnp.stack([pm * jnp.cos(t * th), pm * jnp.sin(t * th)])
    steps = jnp.concatenate([pw[:, min(2 ** i, t_len)][:, None] for i in range(8)], axis=1)
    tile = pw[:, 1:9]
    a8k = pw[:, 0:t_len:8]
    if rev:
        tile, a8k = tile[:, ::-1], a8k[:, ::-1]
    misc = jnp.concatenate([pw[:, t_len:t_len + 1], jnp.zeros((2, 7, pw.shape[-1]), F32)], axis=1)
    return jnp.concatenate([steps, tile, misc, a8k], axis=1)


def _scan_chunk(x_re, x_im, tab_ref, hin_re, hin_im, rev, t_len, xs_ref, es_ref):
    outs = [_scan_slab(x_re[:, k:k + LANE], x_im[:, k:k + LANE], tab_ref, hin_re[:, k:k + LANE], hin_im[:, k:k + LANE],
                       rev, t_len, xs_ref, es_ref, k) for k in range(0, x_re.shape[-1], LANE)]
    return tuple(jnp.concatenate([o[t] for o in outs], axis=-1) for t in range(4))


def _scan_slab(x_re, x_im, tab_ref, hin_re, hin_im, rev, t_len, xs_ref, es_ref, k0):
    lanes = LANE
    n2 = t_len // 8
    tab_ref = tab_ref.at[:, :, k0:k0 + LANE]
    rin = lax.broadcasted_iota(jnp.int32, (t_len, lanes), 0) & 7
    for li, sh in enumerate((1, 2, 4)):
        m_re, m_im = tab_ref[0, li:li + 1, :], tab_ref[1, li:li + 1, :]
        amt = sh if not rev else t_len - sh
        c_re, c_im = _cmul(m_re, m_im, pltpu.roll(x_re, amt, 0), pltpu.roll(x_im, amt, 0))
        ok = (rin >= sh) if not rev else (rin < 8 - sh)
        x_re = x_re + jnp.where(ok, c_re, 0.0)
        x_im = x_im + jnp.where(ok, c_im, 0.0)
    xr_ref, xi_ref = xs_ref
    xr_ref[...] = x_re
    xi_ref[...] = x_im
    off = 0 if rev else 7
    e_re = xr_ref[pl.ds(off, n2, stride=8), :]
    e_im = xi_ref[pl.ds(off, n2, stride=8), :]
    row2 = lax.broadcasted_iota(jnp.int32, (n2, lanes), 0)
    sh, li = 1, 3
    while sh < n2:
        m_re, m_im = tab_ref[0, li:li + 1, :], tab_ref[1, li:li + 1, :]
        amt = sh if not rev else n2 - sh
        c_re, c_im = _cmul(m_re, m_im, pltpu.roll(e_re, amt, 0), pltpu.roll(e_im, amt, 0))
        ok = (row2 >= sh) if not rev else (row2 < n2 - sh)
        e_re = e_re + jnp.where(ok, c_re, 0.0)
        e_im = e_im + jnp.where(ok, c_im, 0.0)
        sh, li = sh * 2, li + 1
    es_ref[0] = e_re
    es_ref[1] = e_im
    last = 0 if rev else n2 - 1
    t_re, t_im = _cmul(tab_ref[0, 16:17, :], tab_ref[1, 16:17, :], hin_re, hin_im)
    hout_re = es_ref[0, last:last + 1, :] + t_re
    hout_im = es_ref[1, last:last + 1, :] + t_im
    amt = 1 if not rev else n2 - 1
    ok = (row2 >= 1) if not rev else (row2 < n2 - 1)
    k_re, k_im = _cmul(tab_ref[0, 24:24 + n2, :], tab_ref[1, 24:24 + n2, :], hin_re, hin_im)
    c_re = jnp.where(ok, pltpu.roll(e_re, amt, 0), 0.0) + k_re
    c_im = jnp.where(ok, pltpu.roll(e_im, amt, 0), 0.0) + k_im
    tp_re, tp_im = tab_ref[0, 8:16, :][None], tab_ref[1, 8:16, :][None]
    add_re, add_im = _cmul(tp_re, tp_im, c_re[:, None, :], c_im[:, None, :])
    h_re = xr_ref[...] + add_re.reshape(t_len, lanes)
    h_im = xi_ref[...] + add_im.reshape(t_len, lanes)
    return h_re, h_im, hout_re, hout_im


def _s5_order(j, n_chunks, nc, rev):
    if not rev:
        return j
    return jnp.where(j < nc, nc - 1 - j, n_chunks - 1 - (j - nc))


def _s5_fwd(u, tab, b_bd, c_bd, n_ctx, rev):
    b, s, w = u.shape
    lanes = b_bd.shape[-1]
    t_len = _s5_chunk(n_ctx)
    n_chunks, nc = s // t_len, n_ctx // t_len

    def body(u_ref, tab_ref, b_ref, c_ref, y_ref, h_ref, hin_ref, carry_ref, xr_ref, xi_ref, es_ref):
        xs_ref = (xr_ref, xi_ref)

        @pl.when(pl.program_id(1) == 0)
        def _():
            carry_ref[...] = jnp.zeros_like(carry_ref)

        ub = u_ref[...].astype(BF16)
        x_re = jnp.dot(ub, b_ref[0].astype(BF16), preferred_element_type=F32)
        x_im = jnp.dot(ub, b_ref[1].astype(BF16), preferred_element_type=F32)
        hin_re, hin_im = carry_ref[0, 0:1, :], carry_ref[1, 0:1, :]
        hin_ref[...] = carry_ref[...]
        h_re, h_im, ho_re, ho_im = _scan_chunk(x_re, x_im, tab_ref, hin_re, hin_im, rev, t_len, xs_ref, es_ref)
        carry_ref[0] = jnp.broadcast_to(ho_re, (8, lanes))
        carry_ref[1] = jnp.broadcast_to(ho_im, (8, lanes))
        h_ref[0] = h_re
        h_ref[1] = h_im
        y_ref[...] = (jnp.dot(h_re.astype(BF16), c_ref[0].astype(BF16), preferred_element_type=F32)
                      - jnp.dot(h_im.astype(BF16), c_ref[1].astype(BF16), preferred_element_type=F32))

    order = lambda j: _s5_order(j, n_chunks, nc, rev)
    whole = lambda arr: pl.BlockSpec(arr.shape, lambda bi, j: (0,) * arr.ndim)
    return pl.pallas_call(
        body,
        out_shape=[jax.ShapeDtypeStruct((b, s, w), F32), jax.ShapeDtypeStruct((2, b, s, lanes), F32),
                   jax.ShapeDtypeStruct((2, b, n_chunks, 8, lanes), F32)],
        grid=(b, n_chunks),
        in_specs=[pl.BlockSpec((None, t_len, w), lambda bi, j: (bi, order(j), 0)), whole(tab), whole(b_bd), whole(c_bd)],
        out_specs=[pl.BlockSpec((None, t_len, w), lambda bi, j: (bi, order(j), 0)),
                   pl.BlockSpec((2, None, t_len, lanes), lambda bi, j: (0, bi, order(j), 0)),
                   pl.BlockSpec((2, None, None, 8, lanes), lambda bi, j: (0, bi, order(j), 0, 0))],
        scratch_shapes=[pltpu.VMEM((2, 8, lanes), F32), pltpu.VMEM((t_len, LANE), F32), pltpu.VMEM((t_len, LANE), F32),
                        pltpu.VMEM((2, t_len // 8, LANE), F32)],
        name=f"s5_f_{s}_{int(rev)}", compiler_params=_params(),
    )(u, tab, b_bd, c_bd)


def _s5_bwd(u, tab_adj, b_bd, c_bd, h, hin, dy, n_ctx, rev):
    b, s, w = u.shape
    lanes = b_bd.shape[-1]
    t_len = _s5_chunk(n_ctx)
    n_chunks, nc = s // t_len, n_ctx // t_len
    arev = not rev

    def body(u_ref, tab_ref, b_ref, c_ref, h_ref, hin_ref, dy_ref, du_ref, db_ref, dc_ref, da_ref,
             carry_ref, xr_ref, xi_ref, es_ref):
        xs_ref = (xr_ref, xi_ref)
        first = jnp.logical_and(pl.program_id(0) == 0, pl.program_id(1) == 0)

        @pl.when(pl.program_id(1) == 0)
        def _():
            carry_ref[...] = jnp.zeros_like(carry_ref)

        dyv = dy_ref[...]
        dyb = dyv.astype(BF16)
        dn = (((1,), (1,)), ((), ()))
        dt = (((0,), (0,)), ((), ()))
        x_re = lax.dot_general(dyb, c_ref[0].astype(BF16), dn, preferred_element_type=F32)
        x_im = -lax.dot_general(dyb, c_ref[1].astype(BF16), dn, preferred_element_type=F32)
        g_re, g_im, go_re, go_im = _scan_chunk(x_re, x_im, tab_ref, carry_ref[0, 0:1, :], carry_ref[1, 0:1, :],
                                               arev, t_len, xs_ref, es_ref)
        carry_ref[0] = jnp.broadcast_to(go_re, (8, lanes))
        carry_ref[1] = jnp.broadcast_to(go_im, (8, lanes))
        h_re, h_im = h_ref[0], h_ref[1]
        gb_re, gb_im = g_re.astype(BF16), g_im.astype(BF16)
        du_ref[...] = (lax.dot_general(gb_re, b_ref[0].astype(BF16), dn, preferred_element_type=F32)
                       + lax.dot_general(gb_im, b_ref[1].astype(BF16), dn, preferred_element_type=F32))
        ub = u_ref[...].astype(BF16)
        db_re = lax.dot_general(ub, gb_re, dt, preferred_element_type=F32)
        db_im = lax.dot_general(ub, gb_im, dt, preferred_element_type=F32)
        dc_re = lax.dot_general(h_re.astype(BF16), dyb, dt, preferred_element_type=F32)
        dc_im = -lax.dot_general(h_im.astype(BF16), dyb, dt, preferred_element_type=F32)
        row = lax.broadcasted_iota(jnp.int32, (t_len, lanes), 0)
        amt = 1 if not rev else t_len - 1
        edge = (row == 0) if not rev else (row == t_len - 1)
        hp_re = jnp.where(edge, hin_ref[0, 0:1, :], pltpu.roll(h_re, amt, 0))
        hp_im = jnp.where(edge, hin_ref[1, 0:1, :], pltpu.roll(h_im, amt, 0))
        da_re = jnp.sum(g_re * hp_re + g_im * hp_im, axis=0, keepdims=True)
        da_im = jnp.sum(g_im * hp_re - g_re * hp_im, axis=0, keepdims=True)

        @pl.when(first)
        def _():
            db_ref[0], db_ref[1] = db_re, db_im
            dc_ref[0], dc_ref[1] = dc_re, dc_im
            da_ref[0] = jnp.broadcast_to(da_re, (8, lanes))
            da_ref[1] = jnp.broadcast_to(da_im, (8, lanes))

        @pl.when(jnp.logical_not(first))
        def _():
            db_ref[0] += db_re
            db_ref[1] += db_im
            dc_ref[0] += dc_re
            dc_ref[1] += dc_im
            da_ref[0] += jnp.broadcast_to(da_re, (8, lanes))
            da_ref[1] += jnp.broadcast_to(da_im, (8, lanes))

    order = lambda j: _s5_order(n_chunks - 1 - j, n_chunks, nc, rev)
    whole = lambda arr: pl.BlockSpec(arr.shape, lambda bi, j: (0,) * arr.ndim)
    us = pl.BlockSpec((None, t_len, w), lambda bi, j: (bi, order(j), 0))
    return pl.pallas_call(
        body,
        out_shape=[jax.ShapeDtypeStruct((b, s, w), F32), jax.ShapeDtypeStruct(b_bd.shape, F32),
                   jax.ShapeDtypeStruct(c_bd.shape, F32), jax.ShapeDtypeStruct((2, 8, lanes), F32)],
        grid=(b, n_chunks),
        in_specs=[us, whole(tab_adj), whole(b_bd), whole(c_bd),
                  pl.BlockSpec((2, None, t_len, lanes), lambda bi, j: (0, bi, order(j), 0)),
                  pl.BlockSpec((2, None, None, 8, lanes), lambda bi, j: (0, bi, order(j), 0, 0)), us],
        out_specs=[us, whole(b_bd), whole(c_bd), pl.BlockSpec((2, 8, lanes), lambda bi, j: (0, 0, 0))],
        scratch_shapes=[pltpu.VMEM((2, 8, lanes), F32), pltpu.VMEM((t_len, LANE), F32), pltpu.VMEM((t_len, LANE), F32),
                        pltpu.VMEM((2, t_len // 8, LANE), F32)],
        name=f"s5_b_{s}_{int(rev)}", compiler_params=_params(),
    )(u, tab_adj, b_bd, c_bd, h, hin, dy)


@functools.partial(jax.custom_vjp, nondiff_argnums=(4, 5))
def s5_direction(u, a, b_bd, c_bd, n_ctx, rev):
    tab = _s5_tables(a[0], a[1], _s5_chunk(n_ctx), rev)
    return _s5_fwd(u, tab, b_bd, c_bd, n_ctx, rev)[0]


def _s5_direction_fwd(u, a, b_bd, c_bd, n_ctx, rev):
    tab = _s5_tables(a[0], a[1], _s5_chunk(n_ctx), rev)
    y, h, hin = _s5_fwd(u, tab, b_bd, c_bd, n_ctx, rev)
    return y, (u, a, b_bd, c_bd, h, hin)


def _s5_direction_bwd(n_ctx, rev, res, dy):
    u, a, b_bd, c_bd, h, hin = res
    tab_adj = _s5_tables(a[0], -a[1], _s5_chunk(n_ctx), not rev)
    du, db, dc, da = _s5_bwd(u, tab_adj, b_bd, c_bd, h, hin, dy, n_ctx, rev)
    return du, da[:, 0, :], db, dc


s5_direction.defvjp(_s5_direction_fwd, _s5_direction_bwd)


def _s5_discretize(lam_re, lam_im, log_dt, b_re, b_im):
    dt = jnp.exp(log_dt)[:, None]
    mag = jnp.exp(lam_re * dt)
    a_re = mag * jnp.cos(lam_im * dt)
    a_im = mag * jnp.sin(lam_im * dt)
    den = jnp.square(lam_re) + jnp.square(lam_im)
    f_re = ((a_re - 1.0) * lam_re + a_im * lam_im) / den
    f_im = (a_im * lam_re - (a_re - 1.0) * lam_im) / den
    bb_re = f_re[..., None] * b_re - f_im[..., None] * b_im
    bb_im = f_re[..., None] * b_im + f_im[..., None] * b_re
    return a_re, a_im, bb_re, bb_im


def _block_diag(t):
    g, r, c = t.shape
    return (jnp.eye(g, dtype=F32)[:, None, :, None] * t[:, :, None, :]).reshape(g * r, g * c)


def _loss_head(y, target):
    b, n, d = y.shape
    ts = _pick(n, (256, 128, 64))

    def body(y_ref, t_ref, loss_ref, dy_ref):
        first = jnp.logical_and(pl.program_id(0) == 0, pl.program_id(1) == 0)
        err = y_ref[...] - t_ref[...]
        dy_ref[...] = err * (1.0 / d)
        part = 0.5 * jnp.sum(jnp.sum(err * err, axis=-1, keepdims=True) * (1.0 / d), axis=0, keepdims=True)
        part = jnp.broadcast_to(part, (8, LANE))

        @pl.when(first)
        def _():
            loss_ref[...] = part

        @pl.when(jnp.logical_not(first))
        def _():
            loss_ref[...] += part

    blk = pl.BlockSpec((None, ts, d), lambda bi, i: (bi, i, 0))
    return pl.pallas_call(
        body, out_shape=[jax.ShapeDtypeStruct((8, LANE), F32), jax.ShapeDtypeStruct((b, n, d), F32)],
        grid=(b, n // ts), in_specs=[blk, blk], out_specs=[pl.BlockSpec((8, LANE), lambda bi, i: (0, 0)), blk],
        name="loss_head", compiler_params=_params(),
    )(y, target)


def _adamw(w, g, m, v):
    shape = w.shape
    n = int(np.prod(shape))
    cols = shape[-1]
    r = n // cols
    tr = _pick(r, (512, 256, 128, 64, 32, 16, 8))
    c1 = 1.0 / (1.0 - ADAM_B1 ** ADAM_STEP)
    c2 = 1.0 / (1.0 - ADAM_B2 ** ADAM_STEP)

    def body(w_ref, g_ref, m_ref, v_ref, d_ref, mo_ref, vo_ref):
        gv = g_ref[...]
        m2 = ADAM_B1 * m_ref[...] + (1.0 - ADAM_B1) * gv
        v2 = ADAM_B2 * v_ref[...] + (1.0 - ADAM_B2) * (gv * gv)
        d_ref[...] = -ADAM_LR * ((m2 * c1) / (jnp.sqrt(v2 * c2) + ADAM_EPS) + ADAM_WD * w_ref[...])
        mo_ref[...] = m2
        vo_ref[...] = v2

    blk = pl.BlockSpec((tr, cols), lambda i: (i, 0))
    outs = pl.pallas_call(
        body, out_shape=[jax.ShapeDtypeStruct((r, cols), F32)] * 3, grid=(r // tr,),
        in_specs=[blk] * 4, out_specs=[blk] * 3, name=f"adamw_{r}x{cols}", compiler_params=_params(),
    )(*[t.reshape(r, cols) for t in (w, g, m, v)])
    return tuple(o.reshape(shape) for o in outs)


def _sum_rows(x, n):
    _, r, c = x.shape
    tr = _pick(r, (512, 256, 128, 64, 32, 16, 8))

    def body(x_ref, o_ref):
        acc = x_ref[0]
        for j in range(1, n):
            acc = acc + x_ref[j]
        o_ref[...] = acc

    return pl.pallas_call(
        body, out_shape=jax.ShapeDtypeStruct((r, c), F32), grid=(r // tr,),
        in_specs=[pl.BlockSpec((n, tr, c), lambda i: (0, i, 0))], out_specs=pl.BlockSpec((tr, c), lambda i: (i, 0)),
        name=f"sum{n}_{r}x{c}", compiler_params=_params(),
    )(x)


def _add2(x, y):
    shape = x.shape
    c = shape[-1]
    r = int(np.prod(shape)) // c
    tr = _pick(r, (512, 256, 128, 64, 32, 16, 8))

    def body(x_ref, y_ref, o_ref):
        o_ref[...] = x_ref[...] + y_ref[...]

    blk = pl.BlockSpec((tr, c), lambda i: (i, 0))
    return pl.pallas_call(
        body, out_shape=jax.ShapeDtypeStruct((r, c), F32), grid=(r // tr,), in_specs=[blk, blk], out_specs=blk,
        name=f"add2_{r}x{c}", compiler_params=_params(),
    )(x.reshape(r, c), y.reshape(r, c)).reshape(shape)


_FLIPS = ((1, 0), (0, 1), (1, 1))


def _me():
    return lax.axis_index("x"), lax.axis_index("y"), lax.axis_index("c")


def allgather8(v):
    m_per, n = v.shape

    def body(x_ref, out_ref, send_sems, recv_sems, local_sem):
        x, y, c = _me()
        me, sibling = (x, y, c), (x, y, 1 - c)
        chips = [(1 - x, y), (x, 1 - y), (1 - x, 1 - y)]

        def rows(px, py, pc):
            return out_ref.at[pl.ds((4 * px + 2 * py + pc) * m_per, m_per), :]

        def copy(k, block, to, src=None):
            return pltpu.make_async_remote_copy(
                src_ref=rows(*block) if src is None else src, dst_ref=rows(*block),
                send_sem=send_sems.at[k], recv_sem=recv_sems.at[k], device_id=to, device_id_type=MESH)

        mine = pltpu.make_async_copy(x_ref, rows(*me), local_sem)
        mine.start()
        first = [copy(0, me, sibling, src=x_ref)]
        first += [copy(1 + j, me, (*chip, c), src=x_ref) for j, chip in enumerate(chips)]
        for cp in first:
            cp.start()
        passed = [copy(4 + j, (*chip, c), sibling) for j, chip in enumerate(chips)]
        for j, chip in enumerate(chips):
            copy(1 + j, (*chip, c), me).wait_recv()
            passed[j].start()
        copy(0, sibling, me).wait_recv()
        for j, chip in enumerate(chips):
            copy(4 + j, (*chip, 1 - c), me).wait_recv()
        for cp in first + passed:
            cp.wait_send()
        mine.wait()

    return pl.pallas_call(
        body, out_shape=jax.ShapeDtypeStruct((N_DEV * m_per, n), v.dtype), in_specs=[VMEM_SPEC], out_specs=VMEM_SPEC,
        scratch_shapes=[pltpu.SemaphoreType.DMA((7,)), pltpu.SemaphoreType.DMA((7,)), pltpu.SemaphoreType.DMA],
        name=f"allgather8_{m_per}x{n}", compiler_params=_params(),
    )(v)


def _row_chunks(rows, tile_rows, want):
    n = want
    while n > 1 and rows % (n * tile_rows):
        n //= 2
    return [(i * (rows // n), rows // n) for i in range(n)]


def _remote(src, dst, send_sem, recv_sem, to):
    return pltpu.make_async_remote_copy(src_ref=src, dst_ref=dst, send_sem=send_sem, recv_sem=recv_sem, device_id=to,
                                        device_id_type=MESH)


def plane_allgather(big, small):
    rows = big.shape[0]
    rh = rows // 2
    tile = 16 if big.dtype == BF16 else 8
    ch_full = _row_chunks(rows, tile, 8)
    ch_half = _row_chunks(rh, tile, 4)

    def body(big_ref, small_ref, obig_ref, osmall_ref, send_sems, recv_sems, fwd_send, fwd_recv, local_sems):
        x, y, c = _me()
        me = 2 * x + y
        sibling = (x, y, 1 - c)
        mine = pl.ds(c * rh, rh)
        other = pl.ds((1 - c) * rh, rh)
        peers = [((x + fx) & 1, (y + fy) & 1) for fx, fy in _FLIPS]
        for st, sz in ch_full:
            pltpu.make_async_copy(big_ref.at[pl.ds(st, sz)], obig_ref.at[me, pl.ds(st, sz)], local_sems.at[0]).start()
        l_small = pltpu.make_async_copy(small_ref, osmall_ref.at[me], local_sems.at[1])
        l_small.start()
        for j, (px, py) in enumerate(peers):
            for st, sz in ch_half:
                sl = pl.ds(c * rh + st, sz)
                _remote(big_ref.at[sl], obig_ref.at[me, sl], send_sems.at[j], recv_sems.at[j], (px, py, c)).start()
            _remote(small_ref, osmall_ref.at[me], send_sems.at[3 + j], recv_sems.at[3 + j], (px, py, c)).start()
        for j, (px, py) in enumerate(peers):
            pidx = 2 * px + py
            _remote(big_ref.at[mine], obig_ref.at[pidx, mine], send_sems.at[j], recv_sems.at[j], (px, py, c)).wait_recv()
            for st, sz in ch_half:
                sl = pl.ds(c * rh + st, sz)
                _remote(obig_ref.at[pidx, sl], obig_ref.at[pidx, sl], fwd_send.at[j], fwd_recv.at[j], sibling).start()
            _remote(small_ref, osmall_ref.at[pidx], send_sems.at[3 + j], recv_sems.at[3 + j], (px, py, c)).wait_recv()
        for j, (px, py) in enumerate(peers):
            pidx = 2 * px + py
            _remote(obig_ref.at[pidx, other], obig_ref.at[pidx, other], fwd_send.at[j], fwd_recv.at[j], sibling).wait_recv()
        for j, (px, py) in enumerate(peers):
            pidx = 2 * px + py
            _remote(big_ref.at[mine], obig_ref.at[me, mine], send_sems.at[j], recv_sems.at[j], (px, py, c)).wait_send()
            _remote(small_ref, osmall_ref.at[me], send_sems.at[3 + j], recv_sems.at[3 + j], (px, py, c)).wait_send()
            _remote(obig_ref.at[pidx, mine], obig_ref.at[pidx, mine], fwd_send.at[j], fwd_recv.at[j], sibling).wait_send()
        pltpu.make_async_copy(big_ref, obig_ref.at[me], local_sems.at[0]).wait()
        l_small.wait()

    return pl.pallas_call(
        body, out_shape=[jax.ShapeDtypeStruct((N_PLANE,) + big.shape, big.dtype),
                         jax.ShapeDtypeStruct((N_PLANE,) + small.shape, small.dtype)],
        in_specs=[ANY, ANY], out_specs=[ANY, ANY],
        scratch_shapes=[pltpu.SemaphoreType.DMA((6,)), pltpu.SemaphoreType.DMA((6,)), pltpu.SemaphoreType.DMA((3,)),
                        pltpu.SemaphoreType.DMA((3,)), pltpu.SemaphoreType.DMA((2,))],
        name="plane_allgather", compiler_params=_params(),
    )(big, small)


def sibling_split(buf):
    n_blk, rows = buf.shape[1], buf.shape[2]
    chunks = _row_chunks(rows, 8, 4)

    def body(buf_ref, mine_ref, got_ref, send_sem, recv_sem, local_sem):
        x, y, c = _me()
        sibling = (x, y, 1 - c)
        for j in range(n_blk):
            for st, sz in chunks:
                sl = pl.ds(st, sz)
                pltpu.make_async_copy(buf_ref.at[c, j, sl], mine_ref.at[j, sl], local_sem).start()
                _remote(buf_ref.at[1 - c, j, sl], got_ref.at[j, sl], send_sem, recv_sem, sibling).start()
        _remote(buf_ref.at[1 - c], got_ref, send_sem, recv_sem, sibling).wait()
        pltpu.make_async_copy(buf_ref.at[c], mine_ref, local_sem).wait()

    half = jax.ShapeDtypeStruct(buf.shape[1:], buf.dtype)
    return pl.pallas_call(
        body, out_shape=[half, half], in_specs=[ANY], out_specs=[ANY, ANY],
        scratch_shapes=[pltpu.SemaphoreType.DMA, pltpu.SemaphoreType.DMA, pltpu.SemaphoreType.DMA],
        name="sibling_split", compiler_params=_params(),
    )(buf)


def plane_scatter(p):
    chunks = _row_chunks(p.shape[1], 8, 4)

    def body(p_ref, out_ref, send_sems, recv_sems, local_sem):
        x, y, c = _me()
        me = 2 * x + y
        peers = [((x + fx) & 1, (y + fy) & 1) for fx, fy in _FLIPS]
        loc = pltpu.make_async_copy(p_ref.at[me], out_ref.at[me], local_sem)
        loc.start()
        for j, (px, py) in enumerate(peers):
            for st, sz in chunks:
                sl = pl.ds(st, sz)
                _remote(p_ref.at[2 * px + py, sl], out_ref.at[me, sl], send_sems.at[j], recv_sems.at[j], (px, py, c)).start()
        for j, (px, py) in enumerate(peers):
            _remote(p_ref.at[me], out_ref.at[2 * px + py], send_sems.at[j], recv_sems.at[j], (px, py, c)).wait_recv()
        for j, (px, py) in enumerate(peers):
            _remote(p_ref.at[2 * px + py], out_ref.at[me], send_sems.at[j], recv_sems.at[j], (px, py, c)).wait_send()
        loc.wait()

    return pl.pallas_call(
        body, out_shape=jax.ShapeDtypeStruct(p.shape, p.dtype), in_specs=[ANY], out_specs=ANY,
        scratch_shapes=[pltpu.SemaphoreType.DMA((3,)), pltpu.SemaphoreType.DMA((3,)), pltpu.SemaphoreType.DMA],
        name="plane_scatter", compiler_params=_params(),
    )(p)


def sibling_join(s):
    chunks = _row_chunks(s.shape[0], 8, 16)

    def body(s_ref, out_ref, send_sem, recv_sem, local_sem):
        x, y, c = _me()
        sibling = (x, y, 1 - c)
        for st, sz in chunks:
            sl = pl.ds(st, sz)
            pltpu.make_async_copy(s_ref.at[sl], out_ref.at[c, sl], local_sem).start()
            _remote(s_ref.at[sl], out_ref.at[c, sl], send_sem, recv_sem, sibling).start()
        _remote(s_ref, out_ref.at[c], send_sem, recv_sem, sibling).wait_send()
        _remote(s_ref, out_ref.at[1 - c], send_sem, recv_sem, sibling).wait_recv()
        pltpu.make_async_copy(s_ref, out_ref.at[c], local_sem).wait()

    return pl.pallas_call(
        body, out_shape=jax.ShapeDtypeStruct((2,) + s.shape, s.dtype), in_specs=[ANY], out_specs=ANY,
        scratch_shapes=[pltpu.SemaphoreType.DMA, pltpu.SemaphoreType.DMA, pltpu.SemaphoreType.DMA],
        name="sibling_join", compiler_params=_params(),
    )(s)


def _heads(t, n_heads):
    b, s, w = t.shape
    return jnp.transpose(t.reshape(b, s, n_heads, w // n_heads), (0, 2, 1, 3)).reshape(b * n_heads, s, w // n_heads)


def _unheads(t, b):
    bh, s, d = t.shape
    return jnp.transpose(t.reshape(b, bh // b, s, d), (0, 2, 1, 3)).reshape(b, s, (bh // b) * d)


def _op(cache, fn, name, kinds, out_dims, **kw):
    key = (name, tuple(out_dims), tuple(sorted(kw.items())))
    if key not in cache:
        cache[key] = make_rowwise(fn, name, kinds, out_dims, **kw)
    return cache[key]


def _even_mixer(ops, a, w, n_ctx):
    b, s, d = a.shape
    proj = linear(a.reshape(b * s, d), w["e_w_in"]).reshape(b, s, -1)
    q, k, v, u = jnp.split(proj, [GQA_Q_W, GQA_Q_W + GQA_KV_W, GQA_Q_W + 2 * GQA_KV_W], axis=-1)
    cos, sin = _rope_tables(n_ctx, s - n_ctx, HEAD_DIM, 0, HEAD_DIM)
    rot = jnp.asarray(_rope_matrix(HEAD_DIM, 0, HEAD_DIM))
    nr = _op(ops, _fn_norm_rope, "norm_rope", ("row", "tab", "tab", "const", "glob"), (HEAD_DIM,), whole_seq=True)
    qh = nr(_heads(q, GQA_Q_HEADS), cos, sin, rot, w["e_g_q"][None])[0]
    kh = nr(_heads(k, GQA_KV_HEADS), cos, sin, rot, w["e_g_k"][None])[0]
    vh = _heads(v, GQA_KV_HEADS)
    att = attention(qh.reshape(b, GQA_Q_HEADS, s, HEAD_DIM), kh.reshape(b, GQA_KV_HEADS, s, HEAD_DIM),
                    vh.reshape(b, GQA_KV_HEADS, s, HEAD_DIM), GQA_Q_HEADS // GQA_KV_HEADS, n_ctx, HEAD_DIM ** -0.5)
    att = _unheads(att.reshape(b * GQA_Q_HEADS, s, HEAD_DIM), b)
    ys = []
    for dr in range(2):
        a_re, a_im, bb_re, bb_im = _s5_discretize(w["ssm_lam_re"][dr], w["ssm_lam_im"][dr], w["ssm_log_dt"][dr],
                                                  w["ssm_b_re"][dr], w["ssm_b_im"][dr])
        a_flat = jnp.stack([a_re.reshape(-1), a_im.reshape(-1)])
        b_bd = jnp.stack([_block_diag(jnp.swapaxes(bb_re, 1, 2)), _block_diag(jnp.swapaxes(bb_im, 1, 2))])
        c_bd = jnp.stack([_block_diag(jnp.swapaxes(w["ssm_c_re"][dr], 1, 2)),
                          _block_diag(jnp.swapaxes(w["ssm_c_im"][dr], 1, 2))])
        ys.append(s5_direction(u, a_flat, b_bd, c_bd, n_ctx, dr == 1))
    pre = _op(ops, _fn_glu_pre, "glu_pre", ("row", "row", "row", "glob"), (SSM_WIDTH,))
    post = _op(ops, _fn_glu_post, "glu_post", ("row", "row", "glob"), (SSM_WIDTH,))
    z = pre(u, ys[0], ys[1], w["ssm_d"][None])[0]
    t = linear(z.reshape(b * s, SSM_WIDTH), w["ssm_w_glu"]).reshape(b, s, SSM_WIDTH)
    ssm = post(z, t, w["ssm_b_glu"][None])[0]
    mix = jnp.concatenate([att, ssm], axis=-1)
    return linear(mix.reshape(b * s, -1), w["e_w_out"]).reshape(b, s, d)


def _odd_mixer(ops, a, w, n_ctx):
    b, s, d = a.shape
    w_in = jnp.pad(w["o_w_in"], ((0, 0), (0, ODD_IN_PAD - ODD_IN_W)))
    proj = linear(a.reshape(b * s, d), w_in).reshape(b, s, -1)
    c1 = MLA_Q_RANK
    c2 = c1 + MLA_KV_RANK
    c3 = c2 + MLA_ROPE
    cq, ckv, kr, nq, nk, nv, _ = jnp.split(proj, [c1, c2, c3, c3 + NA_W, c3 + 2 * NA_W, ODD_IN_W], axis=-1)
    nrm = lambda wd: _op(ops, _fn_norm, f"norm{wd}", ("row", "glob"), (wd,))
    cqn = nrm(MLA_Q_RANK)(cq, w["mla_g_cq"][None])[0]
    ckvn = nrm(MLA_KV_RANK)(ckv, w["mla_g_ckv"][None])[0]
    q = linear(cqn.reshape(b * s, -1), w["mla_w_uq"]).reshape(b, s, -1)
    kv = linear(ckvn.reshape(b * s, -1), w["mla_w_ukv"]).reshape(b, s, MLA_HEADS, MLA_NOPE + MLA_V)
    k_nope = kv[..., :MLA_NOPE].reshape(b, s, MLA_HEADS * MLA_NOPE)
    mv = kv[..., MLA_NOPE:].reshape(b, s, MLA_HEADS * MLA_V)
    kh = jnp.concatenate([_heads(k_nope, MLA_HEADS),
                          jnp.broadcast_to(kr[:, None], (b, MLA_HEADS, s, MLA_ROPE)).reshape(b * MLA_HEADS, s, MLA_ROPE)],
                         axis=-1)
    cos, sin = _rope_tables(n_ctx, s - n_ctx, MLA_QK, MLA_NOPE, MLA_ROPE)
    rot = jnp.asarray(_rope_matrix(MLA_QK, MLA_NOPE, MLA_ROPE))
    nr = _op(ops, _fn_norm_rope, "norm_rope", ("row", "tab", "tab", "const", "glob"), (MLA_QK,), whole_seq=True)
    mq = nr(_heads(q, MLA_HEADS), cos, sin, rot, w["mla_g_q"][None])[0]
    mk = nr(kh, cos, sin, rot, w["mla_g_k"][None])[0]
    mla = attention(mq.reshape(b, MLA_HEADS, s, MLA_QK), mk.reshape(b, MLA_HEADS, s, MLA_QK),
                    _heads(mv, MLA_HEADS).reshape(b, MLA_HEADS, s, MLA_V), 1, n_ctx, MLA_QK ** -0.5)
    mla = _unheads(mla.reshape(b * MLA_HEADS, s, MLA_V), b)
    nh = _op(ops, _fn_norm, "normh", ("row", "glob"), (HEAD_DIM,), whole_seq=True)
    nqh = nh(_heads(nq, NA_HEADS), w["na_g_q"][None])[0]
    nkh = nh(_heads(nk, NA_HEADS), w["na_g_k"][None])[0]
    r4 = lambda t: t.reshape(b, NA_HEADS, s, HEAD_DIM)
    na = na_attention(r4(nqh), r4(nkh), r4(_heads(nv, NA_HEADS)), na_bias_table(w["na_rpb"]), n_ctx)
    na = _unheads(na.reshape(b * NA_HEADS, s, HEAD_DIM), b)
    mix = jnp.concatenate([mla, na], axis=-1)
    return linear(mix.reshape(b * s, -1), w["o_w_out"]).reshape(b, s, d)


_EVEN_KEYS = ("e_w_in", "e_w_out", "e_g_q", "e_g_k", "ssm_lam_re", "ssm_lam_im", "ssm_log_dt", "ssm_b_re", "ssm_b_im",
              "ssm_c_re", "ssm_c_im", "ssm_d", "ssm_w_glu", "ssm_b_glu")
_ODD_KEYS = ("o_w_in", "o_w_out", "mla_g_cq", "mla_g_ckv", "mla_w_uq", "mla_w_ukv", "mla_g_q", "mla_g_k", "na_g_q",
             "na_g_k", "na_rpb")


def _trunk(x_all, mods, w, n_ctx):
    ops = {}
    depth = mods.shape[0]
    b, s, d = x_all.shape
    modulate = _op(ops, _fn_modulate, "modulate", ("row", "glob", "seg", "seg"), (d,), nctx_rows=n_ctx)
    gated = _op(ops, _fn_gated_add, "gated", ("row", "row", "seg"), (d,), nctx_rows=n_ctx)
    x = x_all
    for i in range(depth):
        j = i // 2
        m = [mods[i][:, :, r:r + 1, :] for r in range(N_MOD)]
        a = modulate(x, w["g_norm1"][i][None], m[0], m[1])[0]
        if i % 2 == 0:
            o = _even_mixer(ops, a, {k: w[k][j] for k in _EVEN_KEYS}, n_ctx)
        else:
            o = _odd_mixer(ops, a, {k: w[k][j] for k in _ODD_KEYS}, n_ctx)
        x = gated(x, o, m[2])[0]
        a2 = modulate(x, w["g_norm2"][i][None], m[3], m[4])[0]
        f = ffn(a2.reshape(b * s, d), w["w_ff1"][i], w["w_ff2"][i]).reshape(b, s, d)
        x = gated(x, f, m[5])[0]
    return x[:, n_ctx:]


def local_step(x, ctx, mods, w, loss_target):
    n_ctx = ctx.shape[1]
    x_all = jnp.concatenate([ctx, x], axis=1)
    y, vjp = jax.vjp(lambda xa, md, ww: _trunk(xa, md, ww, n_ctx), x_all, mods, w)
    loss_tile, dy = _loss_head(y, loss_target)
    dx_all, dmods, dw = vjp(dy)
    return loss_tile[0, 0], dx_all[:, n_ctx:], dmods, dw


_SHARDED = (("w_ff1", 2), ("w_ff2", 1), ("e_w_in", 2), ("e_w_out", 1), ("o_w_in", 2), ("o_w_out", 1),
            ("mla_w_uq", 2), ("mla_w_ukv", 2), ("ssm_w_glu", 1))
_SHARDED_SMALL = (("mla_g_cq", 1), ("mla_g_ckv", 1))
_REPLICATED = ("g_norm1", "g_norm2", "e_g_q", "e_g_k", "ssm_lam_re", "ssm_lam_im", "ssm_log_dt", "ssm_b_re", "ssm_b_im",
               "ssm_c_re", "ssm_c_im", "ssm_d", "ssm_b_glu", "mla_g_q", "mla_g_k", "na_g_q", "na_g_k", "na_rpb")
_WEIGHTS = ("c_ctx", "w_mod", "b_mod", "g_norm1", "g_norm2", "w_ff1", "w_ff2", "e_w_in", "e_w_out", "e_g_q", "e_g_k",
            "ssm_lam_re", "ssm_lam_im", "ssm_log_dt", "ssm_b_re", "ssm_b_im", "ssm_c_re", "ssm_c_im", "ssm_d",
            "ssm_w_glu", "ssm_b_glu", "o_w_in", "o_w_out", "mla_g_cq", "mla_g_ckv", "mla_w_uq", "mla_w_ukv", "mla_g_q",
            "mla_g_k", "na_g_q", "na_g_k", "na_rpb")
_PACK_ROWS = 64


def _pack(arrs, dtype, cols=1024, row_mult=_PACK_ROWS):
    flat = jnp.concatenate([a.reshape(-1).astype(dtype) for a in arrs])
    unit = cols * row_mult
    pad = (-flat.shape[0]) % unit
    return jnp.pad(flat, (0, pad)).reshape(-1, cols)


def _unpack(flat, shapes):
    flat = flat.reshape(-1)
    out, off = [], 0
    for sh in shapes:
        n = int(np.prod(sh))
        out.append(flat[off:off + n].reshape(sh))
        off += n
    return out


def _silu(t):
    return t * jax.nn.sigmoid(t)


def kernel(x, c, ctx, c_ctx, w_mod, b_mod, g_norm1, g_norm2, w_ff1, w_ff2, e_w_in, e_w_out, e_g_q, e_g_k, ssm_lam_re, ssm_lam_im, ssm_log_dt, ssm_b_re, ssm_b_im, ssm_c_re, ssm_c_im, ssm_d, ssm_w_glu, ssm_b_glu, o_w_in, o_w_out, mla_g_cq, mla_g_ckv, mla_w_uq, mla_w_ukv, mla_g_q, mla_g_k, na_g_q, na_g_k, na_rpb, loss_target, m_c_ctx, m_w_mod, m_b_mod, m_g_norm1, m_g_norm2, m_w_ff1, m_w_ff2, m_e_w_in, m_e_w_out, m_e_g_q, m_e_g_k, m_ssm_lam_re, m_ssm_lam_im, m_ssm_log_dt, m_ssm_b_re, m_ssm_b_im, m_ssm_c_re, m_ssm_c_im, m_ssm_d, m_ssm_w_glu, m_ssm_b_glu, m_o_w_in, m_o_w_out, m_mla_g_cq, m_mla_g_ckv, m_mla_w_uq, m_mla_w_ukv, m_mla_g_q, m_mla_g_k, m_na_g_q, m_na_g_k, m_na_rpb, v_c_ctx, v_w_mod, v_b_mod, v_g_norm1, v_g_norm2, v_w_ff1, v_w_ff2, v_e_w_in, v_e_w_out, v_e_g_q, v_e_g_k, v_ssm_lam_re, v_ssm_lam_im, v_ssm_log_dt, v_ssm_b_re, v_ssm_b_im, v_ssm_c_re, v_ssm_c_im, v_ssm_d, v_ssm_w_glu, v_ssm_b_glu, v_o_w_in, v_o_w_out, v_mla_g_cq, v_mla_g_ckv, v_mla_w_uq, v_mla_w_ukv, v_mla_g_q, v_mla_g_k, v_na_g_q, v_na_g_k, v_na_rpb):
    env = dict(locals())
    weights = {n: env[n] for n in _WEIGHTS}
    mom_m = {n: env["m_" + n] for n in _WEIGHTS}
    mom_v = {n: env["v_" + n] for n in _WEIGHTS}
    ax, ay, ac = _me()
    plane = 2 * ax + ay
    dev = 4 * ax + 2 * ay + ac
    b_loc, d = c.shape
    depth = w_mod.shape[0]
    n_all = N_DEV * b_loc
    mod_cols = w_mod.shape[2]

    big = _pack([weights[n] for n, _ in _SHARDED], BF16)
    small = _pack([weights[n] for n, _ in _SHARDED_SMALL], F32, cols=LANE, row_mult=8)
    g_big, g_small = plane_allgather(big, small)
    full = {n: weights[n] for n in _REPLICATED}
    parts = [_unpack(g_big[j], [weights[n].shape for n, _ in _SHARDED]) for j in range(N_PLANE)]
    for t, (n, axis) in enumerate(_SHARDED):
        full[n] = jnp.concatenate([parts[j][t] for j in range(N_PLANE)], axis=axis).astype(F32)
    parts_s = [_unpack(g_small[j], [weights[n].shape for n, _ in _SHARDED_SMALL]) for j in range(N_PLANE)]
    for t, (n, axis) in enumerate(_SHARDED_SMALL):
        full[n] = jnp.concatenate([parts_s[j][t] for j in range(N_PLANE)], axis=axis)

    rows_pad = 8 * ((n_all + 1 + 7) // 8)
    c_all = allgather8(jnp.pad(c, ((0, 8 - b_loc), (0, 0)))).reshape(N_DEV, 8, d)[:, :b_loc].reshape(n_all, d)
    cond_raw = jnp.concatenate([c_all, c_ctx[None], jnp.zeros((rows_pad - n_all - 1, d), F32)], axis=0)
    b_cols = lax.dynamic_slice_in_dim(b_mod, plane * mod_cols, mod_cols, axis=1)
    mod_loc = jnp.stack([_mm(cond_raw, w_mod[i], a_act="silu") + b_cols[i][None] for i in range(depth)])
    mod_g = allgather8(mod_loc.reshape(depth * rows_pad, mod_cols)).reshape(N_PLANE, 2, depth, rows_pad, mod_cols)
    mod_all = jnp.concatenate([mod_g[j, 0] for j in range(N_PLANE)], axis=-1)
    m_lat = lax.dynamic_slice_in_dim(mod_all, dev * b_loc, b_loc, axis=1)
    m_ctx = jnp.broadcast_to(mod_all[:, n_all][:, None], m_lat.shape)
    mods = jnp.stack([m_ctx, m_lat], axis=2).reshape(depth, b_loc, 2, N_MOD, d)

    loss_part, grad_x, dmods, dw = local_step(x, ctx, mods, full, loss_target)
    loss = lax.psum(loss_part, ("x", "y", "c"))

    dm = dmods.reshape(depth, b_loc, 2, N_MOD * d)
    dm_rows = jnp.concatenate([dm[:, :, 1], jnp.sum(dm[:, :, 0], axis=1, keepdims=True)], axis=1)
    rep_shapes = [weights[n].shape for n in _REPLICATED]
    small_pack = _pack([dm_rows] + [dw[n] for n in _REPLICATED], F32, cols=1024, row_mult=8)
    sp_rows = small_pack.shape[0]
    gathered = allgather8(small_pack).reshape(N_DEV, sp_rows, 1024)
    n_dm = depth * (b_loc + 1) * N_MOD * d
    dm_all = gathered.reshape(N_DEV, -1)[:, :n_dm].reshape(N_DEV, depth, b_loc + 1, N_MOD * d)
    rep_sum = _sum_rows(gathered, N_DEV).reshape(-1)
    rep_grads = dict(zip(_REPLICATED, _unpack(rep_sum[n_dm:], rep_shapes)))
    d_ctx_row = rep_sum[:n_dm].reshape(depth, b_loc + 1, N_MOD * d)[:, b_loc]
    d_lat_rows = jnp.transpose(dm_all[:, :, :b_loc], (1, 0, 2, 3)).reshape(depth, n_all, N_MOD * d)
    d_mod_all = jnp.concatenate([d_lat_rows, d_ctx_row[:, None],
                                 jnp.zeros((depth, rows_pad - n_all - 1, N_MOD * d), F32)], axis=1)
    grads = dict(rep_grads)
    grads["b_mod"] = jnp.sum(d_mod_all, axis=1)
    d_cols = lax.dynamic_slice_in_dim(d_mod_all, plane * mod_cols, mod_cols, axis=2)
    grads["w_mod"] = jnp.stack([_mm(cond_raw, d_cols[i], ta=True, a_act="silu") for i in range(depth)])
    d_cond = _mm(d_cols[0], w_mod[0], tb=True)
    for i in range(1, depth):
        d_cond = _add2(d_cond, _mm(d_cols[i], w_mod[i], tb=True))
    d_cond_g = allgather8(d_cond[n_all:n_all + 8] if rows_pad - n_all >= 8 else
                          jnp.pad(d_cond[n_all:], ((0, 8 - (rows_pad - n_all)), (0, 0)))).reshape(N_PLANE, 2, 8, d)
    d_silu = _sum_rows(d_cond_g[:, 0], N_PLANE)[0]
    sg = jax.nn.sigmoid(c_ctx)
    grads["c_ctx"] = d_silu * (sg * (1.0 + c_ctx * (1.0 - sg)))

    def shard_of(g, axis, j):
        n = g.shape[axis] // N_PLANE
        return lax.slice_in_dim(g, j * n, (j + 1) * n, axis=axis)

    send = jnp.stack([_pack([shard_of(dw[n], axis, j) for n, axis in _SHARDED]
                            + [shard_of(dw[n], axis, j) for n, axis in _SHARDED_SMALL], F32) for j in range(N_PLANE)])
    rows = send.shape[1]
    send = jnp.transpose(send.reshape(N_PLANE, 2, rows // 2, 1024), (1, 0, 2, 3))
    mine, got = sibling_split(send)
    chip_sum = _add2(mine, got)
    arrived = plane_scatter(chip_sum)
    total = _sum_rows(arrived.reshape(N_PLANE, rows // 2, 1024), N_PLANE)
    flat = sibling_join(total).reshape(-1)
    shard_shapes = [weights[n].shape for n, _ in _SHARDED] + [weights[n].shape for n, _ in _SHARDED_SMALL]
    for (n, _), g in zip(_SHARDED + _SHARDED_SMALL, _unpack(flat, shard_shapes)):
        grads[n] = g

    big_names = ("w_mod",) + tuple(n for n, _ in _SHARDED)
    small_names = tuple(n for n in _WEIGHTS if n not in big_names)
    delta, new_m, new_v = {}, {}, {}
    for n in big_names:
        delta[n], new_m[n], new_v[n] = _adamw(weights[n], grads[n], mom_m[n], mom_v[n])
    sm_shapes = [weights[n].shape for n in small_names]
    packed = [_pack([src[n] for n in small_names], F32, cols=1024, row_mult=8)
              for src in (weights, grads, mom_m, mom_v)]
    for dst, res in zip((delta, new_m, new_v), _adamw(*packed)):
        dst.update(dict(zip(small_names, _unpack(res, sm_shapes))))

    return (loss, grad_x, *[grads[n] for n in _WEIGHTS], *[delta[n] for n in _WEIGHTS],
            *[new_m[n] for n in _WEIGHTS], *[new_v[n] for n in _WEIGHTS])
```

```python
import functools
import math

import numpy as np
import jax
import jax.numpy as jnp
from jax import lax
from jax.experimental import pallas as pl
from jax.experimental.pallas import tpu as pltpu

F32 = jnp.float32
BF16 = jnp.bfloat16
HI = lax.Precision.HIGHEST
MESH = pl.DeviceIdType.MESH
ANY = pl.BlockSpec(memory_space=pl.ANY)
VMEM_SPEC = pl.BlockSpec(memory_space=pltpu.VMEM)

GRID_W = 64
HEAD_DIM = 64
ROPE_BASE = 10000.0
EPS = 1e-6
N_MOD = 6
GQA_Q_HEADS, GQA_KV_HEADS = 12, 4
GQA_Q_W, GQA_KV_W = GQA_Q_HEADS * HEAD_DIM, GQA_KV_HEADS * HEAD_DIM
SSM_WIDTH, SSM_GROUP, SSM_STATE = 256, 16, 64
SSM_GROUPS = SSM_WIDTH // SSM_GROUP
SSM_LANES = SSM_GROUPS * SSM_STATE
MLA_HEADS, MLA_Q_RANK, MLA_KV_RANK, MLA_NOPE, MLA_ROPE, MLA_V = 8, 512, 256, 64, 32, 64
MLA_QK = MLA_NOPE + MLA_ROPE
NA_HEADS, NA_WIN_R, NA_WIN_C = 8, 8, 16
NA_W = NA_HEADS * HEAD_DIM
NA_BAND = NA_WIN_R * GRID_W
ODD_IN_W = MLA_Q_RANK + MLA_KV_RANK + MLA_ROPE + 3 * NA_W
ODD_IN_PAD = 2560
ADAM_LR, ADAM_B1, ADAM_B2, ADAM_EPS, ADAM_WD, ADAM_STEP = 0.001, 0.9, 0.999, 1e-08, 0.01, 10
NEG = -1e30
VMEM_LIMIT = 56 * 1024 * 1024
LANE = 128
MM_TILE_M = (1152, 1024, 768, 512, 256, 128)
MM_TILE_N = (1280, 1024, 768, 512, 256, 128)
MM_TILE_K = (1152, 1024, 768, 512, 256, 128)
N_PLANE = 4
N_DEV = 8


def _pick(n, cands):
    for c in cands:
        if n % c == 0:
            return c
    return n


def _params(**kw):
    return pltpu.CompilerParams(vmem_limit_bytes=VMEM_LIMIT, **kw)


def _mm(a, b, *, ta=False, tb=False, a_act=None, epi=None, e=None, exact=False):
    m, kd = (a.shape[1], a.shape[0]) if ta else a.shape
    n = b.shape[0] if tb else b.shape[1]
    tm = _pick(m, MM_TILE_M)
    tn = _pick(n, MM_TILE_N)
    tk = _pick(kd, MM_TILE_K)
    nk = kd // tk
    dn = (((0 if ta else 1,), (1 if tb else 0,)), ((), ()))

    def body(*refs):
        if epi is None:
            a_ref, b_ref, o_ref = refs
        else:
            a_ref, b_ref, e_ref, o_ref = refs
        k = pl.program_id(2)
        av = a_ref[...]
        if a_act == "relu2":
            av = jnp.square(jnp.maximum(av, 0.0))
        elif a_act == "silu":
            av = av * jax.nn.sigmoid(av)
        bv = b_ref[...]
        if exact:
            p = lax.dot_general(av, bv, dn, precision=HI, preferred_element_type=F32)
        else:
            p = lax.dot_general(av.astype(BF16), bv.astype(BF16), dn, preferred_element_type=F32)

        @pl.when(k == 0)
        def _():
            o_ref[...] = p

        @pl.when(k > 0)
        def _():
            o_ref[...] += p

        if epi == "drelu2":
            @pl.when(k == nk - 1)
            def _():
                o_ref[...] = o_ref[...] * (2.0 * jnp.maximum(e_ref[...], 0.0))

    a_spec = pl.BlockSpec((tk, tm), lambda i, j, k: (k, i)) if ta else pl.BlockSpec((tm, tk), lambda i, j, k: (i, k))
    b_spec = pl.BlockSpec((tn, tk), lambda i, j, k: (j, k)) if tb else pl.BlockSpec((tk, tn), lambda i, j, k: (k, j))
    o_spec = pl.BlockSpec((tm, tn), lambda i, j, k: (i, j))
    ins, specs = [a, b], [a_spec, b_spec]
    if epi is not None:
        ins.append(e)
        specs.append(o_spec)
    name = f"mm_{m}x{kd}x{n}_{int(ta)}{int(tb)}_{a_act}_{epi}_{int(exact)}"
    return pl.pallas_call(
        body, out_shape=jax.ShapeDtypeStruct((m, n), F32), grid=(m // tm, n // tn, nk),
        in_specs=specs, out_specs=o_spec, name=name, compiler_params=_params(),
    )(*ins)


@functools.partial(jax.custom_vjp, nondiff_argnums=(2,))
def _linear(a, w, exact):
    return _mm(a, w, exact=exact)


def _linear_fwd(a, w, exact):
    return _mm(a, w, exact=exact), (a, w)


def _linear_bwd(exact, res, g):
    a, w = res
    return _mm(g, w, tb=True, exact=exact), _mm(a, g, ta=True, exact=exact)


_linear.defvjp(_linear_fwd, _linear_bwd)


def linear(a, w, exact=False):
    return _linear(a, w, exact)


@jax.custom_vjp
def ffn(a, w1, w2):
    return _mm(_mm(a, w1), w2, a_act="relu2")


def _ffn_fwd(a, w1, w2):
    h1 = _mm(a, w1)
    return _mm(h1, w2, a_act="relu2"), (a, w1, w2, h1)


def _ffn_bwd(res, g):
    a, w1, w2, h1 = res
    dh1 = _mm(g, w2, tb=True, epi="drelu2", e=h1)
    dw2 = _mm(h1, g, ta=True, a_act="relu2")
    return _mm(dh1, w1, tb=True), _mm(a, dh1, ta=True), dw2


ffn.defvjp(_ffn_fwd, _ffn_bwd)


def make_rowwise(fn, name, kinds, out_dims, nctx_rows=0, whole_seq=False):
    n_in = len(kinds)
    n_out = len(out_dims)
    diff = [i for i, kd in enumerate(kinds) if kd in ("row", "glob", "seg")]

    def layout(args):
        row0 = args[kinds.index("row")]
        g, s = row0.shape[0], row0.shape[1]
        ts = s if whole_seq else (min(256, nctx_rows) if nctx_rows else _pick(s, (256, 128, 64)))
        nctx = nctx_rows // ts
        return g, s, ts, nctx

    def spec_of(kind, arr, ts, nctx):
        if kind == "row":
            return pl.BlockSpec((None, ts, arr.shape[2]), lambda g, i: (g, i, 0))
        if kind == "tab":
            return pl.BlockSpec((ts, arr.shape[1]), lambda g, i: (i, 0))
        if kind in ("const", "glob"):
            return pl.BlockSpec(arr.shape, lambda g, i: (0, 0))
        return pl.BlockSpec((None, None) + arr.shape[2:], lambda g, i: (g, (i >= nctx).astype(jnp.int32), 0, 0))

    def fwd_call(*args):
        g, s, ts, nctx = layout(args)

        def body(*refs):
            vals = [r[...] for r in refs[:n_in]]
            outs = fn(*vals)
            for o_ref, o in zip(refs[n_in:], outs):
                o_ref[...] = o

        return pl.pallas_call(
            body, out_shape=[jax.ShapeDtypeStruct((g, s, d), F32) for d in out_dims], grid=(g, s // ts),
            in_specs=[spec_of(kd, a, ts, nctx) for kd, a in zip(kinds, args)],
            out_specs=[pl.BlockSpec((None, ts, d), lambda g_, i: (g_, i, 0)) for d in out_dims],
            name=f"{name}_f_{g}x{s}", compiler_params=_params(),
        )(*args)

    def bwd_call(args, cts):
        g, s, ts, nctx = layout(args)

        def body(*refs):
            in_refs, ct_refs, out_refs = refs[:n_in], refs[n_in:n_in + n_out], refs[n_in + n_out:]
            gi, i = pl.program_id(0), pl.program_id(1)
            vals = [r[...] for r in in_refs]

            def f(*dv):
                full = list(vals)
                for idx, v in zip(diff, dv):
                    full[idx] = v
                return tuple(fn(*full))

            _, vjp = jax.vjp(f, *[vals[idx] for idx in diff])
            grads = vjp(tuple(r[...] for r in ct_refs))
            for idx, o_ref, gr in zip(diff, out_refs, grads):
                if kinds[idx] == "row":
                    o_ref[...] = gr
                    continue
                if kinds[idx] == "glob":
                    first = jnp.logical_and(gi == 0, i == 0)
                else:
                    first = jnp.logical_or(i == 0, i == nctx)

                @pl.when(first)
                def _(o_ref=o_ref, gr=gr):
                    o_ref[...] = gr

                @pl.when(jnp.logical_not(first))
                def _(o_ref=o_ref, gr=gr):
                    o_ref[...] += gr

        in_specs = [spec_of(kd, a, ts, nctx) for kd, a in zip(kinds, args)]
        in_specs += [pl.BlockSpec((None, ts, d), lambda g_, i: (g_, i, 0)) for d in out_dims]
        return pl.pallas_call(
            body, out_shape=[jax.ShapeDtypeStruct(args[idx].shape, F32) for idx in diff], grid=(g, s // ts),
            in_specs=in_specs, out_specs=[spec_of(kinds[idx], args[idx], ts, nctx) for idx in diff],
            name=f"{name}_b_{g}x{s}", compiler_params=_params(),
        )(*args, *cts)

    @jax.custom_vjp
    def op(*args):
        return tuple(fwd_call(*args))

    def op_fwd(*args):
        return tuple(fwd_call(*args)), args

    def op_bwd(args, cts):
        grads = bwd_call(args, cts)
        full = [None] * n_in
        for idx, gr in zip(diff, grads):
            full[idx] = gr
        return tuple(jnp.zeros_like(a) if gfull is None else gfull for a, gfull in zip(args, full))

    op.defvjp(op_fwd, op_bwd)
    return op


def _rms(x):
    return lax.rsqrt(jnp.mean(x * x, axis=-1, keepdims=True) + EPS)


def _fn_modulate(x, g, shift, scale):
    return ((x * _rms(x) * g) * (1.0 + scale) + shift,)


def _fn_gated_add(x, o, gate):
    return (x + gate * o,)


def _fn_norm(x, g):
    return (x * _rms(x) * g,)


def _fn_norm_rope(x, cos, sin, rot, g):
    y = x * _rms(x) * g
    r = jnp.dot(y, rot, precision=HI, preferred_element_type=F32)
    return (y * cos + r * sin,)


def _fn_glu_pre(u, y0, y1, d):
    return (jax.nn.gelu(d * u + y0 + y1),)


def _fn_glu_post(z, t, bg):
    return (z * jax.nn.sigmoid(t + bg),)


def _rope_matrix(dh, start, rot_dim):
    r = np.zeros((dh, dh), np.float32)
    q = rot_dim // 4
    for j in range(rot_dim):
        if (j // q) % 2 == 0:
            r[start + j + q, start + j] = -1.0
        else:
            r[start + j - q, start + j] = 1.0
    return r


def _rope_tables(n_ctx, n_lat, dh, start, rot_dim):
    t = jnp.arange(n_lat)
    rows = (t // GRID_W).astype(F32)
    cols = (t % GRID_W).astype(F32)
    axis_dim = rot_dim // 2
    freqs = ROPE_BASE ** (-jnp.arange(0, axis_dim, 2, dtype=F32) / axis_dim)
    ang_r = rows[:, None] * freqs
    ang_c = cols[:, None] * freqs
    ang = jnp.concatenate([ang_r, ang_r, ang_c, ang_c], axis=-1)
    cos = jnp.concatenate([jnp.ones((n_lat, start), F32), jnp.cos(ang)], axis=-1)
    sin = jnp.concatenate([jnp.zeros((n_lat, start), F32), jnp.sin(ang)], axis=-1)
    cos = jnp.concatenate([jnp.ones((n_ctx, dh), F32), cos], axis=0)
    sin = jnp.concatenate([jnp.zeros((n_ctx, dh), F32), sin], axis=0)
    return cos, sin


_NT = (((1,), (1,)), ((), ()))
_TN = (((0,), (0,)), ((), ()))


def _attn_fwd(q, k, v, group, n_ctx, scale):
    b, h, s, dq = q.shape
    dv = v.shape[-1]
    tq = min(256, n_ctx)
    nc = n_ctx // tq

    def body(q_ref, k_ref, v_ref, o_ref, lse_ref):
        qv = (q_ref[...] * scale).astype(BF16)

        def run(n_keys):
            sc = lax.dot_general(qv, k_ref[0:n_keys, :].astype(BF16), _NT, preferred_element_type=F32)
            m = jnp.max(sc, axis=-1, keepdims=True)
            p = jnp.exp(sc - m)
            l = jnp.sum(p, axis=-1, keepdims=True)
            o = jnp.dot(p.astype(BF16), v_ref[0:n_keys, :].astype(BF16), preferred_element_type=F32)
            o_ref[...] = o / l
            lse_ref[...] = m + jnp.log(l)

        pl.when(pl.program_id(2) < nc)(lambda: run(n_ctx))
        pl.when(pl.program_id(2) >= nc)(lambda: run(s))

    return pl.pallas_call(
        body, out_shape=[jax.ShapeDtypeStruct((b, h, s, dv), F32), jax.ShapeDtypeStruct((b, h, s, 1), F32)],
        grid=(b, h, s // tq),
        in_specs=[pl.BlockSpec((None, None, tq, dq), lambda bi, hi, i: (bi, hi, i, 0)),
                  pl.BlockSpec((None, None, s, dq), lambda bi, hi, i: (bi, lax.div(hi, group), 0, 0)),
                  pl.BlockSpec((None, None, s, dv), lambda bi, hi, i: (bi, lax.div(hi, group), 0, 0))],
        out_specs=[pl.BlockSpec((None, None, tq, dv), lambda bi, hi, i: (bi, hi, i, 0)),
                   pl.BlockSpec((None, None, tq, 1), lambda bi, hi, i: (bi, hi, i, 0))],
        name=f"attn_f_{h}x{s}x{dq}", compiler_params=_params(),
    )(q, k, v)


def _attn_dq(q, k, v, o, lse, do, group, n_ctx, scale):
    b, h, s, dq = q.shape
    dv = v.shape[-1]
    tq = min(256, n_ctx)
    nc = n_ctx // tq

    def body(q_ref, k_ref, v_ref, o_ref, lse_ref, do_ref, dq_ref, delta_ref):
        qv = (q_ref[...] * scale).astype(BF16)
        dov = do_ref[...]
        delta = jnp.sum(dov * o_ref[...], axis=-1, keepdims=True)
        delta_ref[...] = delta

        def run(n_keys):
            kv = k_ref[0:n_keys, :].astype(BF16)
            sc = lax.dot_general(qv, kv, _NT, preferred_element_type=F32)
            p = jnp.exp(sc - lse_ref[...])
            dp = lax.dot_general(dov.astype(BF16), v_ref[0:n_keys, :].astype(BF16), _NT, preferred_element_type=F32)
            ds = p * (dp - delta)
            dq_ref[...] = jnp.dot(ds.astype(BF16), kv, preferred_element_type=F32) * scale

        pl.when(pl.program_id(2) < nc)(lambda: run(n_ctx))
        pl.when(pl.program_id(2) >= nc)(lambda: run(s))

    qs = lambda d: pl.BlockSpec((None, None, tq, d), lambda bi, hi, i: (bi, hi, i, 0))
    ks = lambda d: pl.BlockSpec((None, None, s, d), lambda bi, hi, i: (bi, lax.div(hi, group), 0, 0))
    return pl.pallas_call(
        body, out_shape=[jax.ShapeDtypeStruct((b, h, s, dq), F32), jax.ShapeDtypeStruct((b, h, s, 1), F32)],
        grid=(b, h, s // tq),
        in_specs=[qs(dq), ks(dq), ks(dv), qs(dv), qs(1), qs(dv)], out_specs=[qs(dq), qs(1)],
        name=f"attn_dq_{h}x{s}x{dq}", compiler_params=_params(),
    )(q, k, v, o, lse, do)


def _attn_dkv(q, k, v, lse, delta, do, group, n_ctx, scale):
    b, h, s, dq = q.shape
    hk = k.shape[1]
    dv = v.shape[-1]
    tk = min(256, n_ctx)
    nc = n_ctx // tk

    def body(q_ref, k_ref, v_ref, lse_ref, delta_ref, do_ref, dk_ref, dv_ref):
        kv = k_ref[...].astype(BF16)
        vv = v_ref[...].astype(BF16)

        def run(r0):
            dk = jnp.zeros((tk, dq), F32)
            dvv = jnp.zeros((tk, dv), F32)
            for g in range(group):
                qg = (q_ref[g, r0:s, :] * scale).astype(BF16)
                dog = do_ref[g, r0:s, :].astype(BF16)
                sc = lax.dot_general(qg, kv, _NT, preferred_element_type=F32)
                p = jnp.exp(sc - lse_ref[g, r0:s, :])
                dvv = dvv + lax.dot_general(p.astype(BF16), dog, _TN, preferred_element_type=F32)
                dp = lax.dot_general(dog, vv, _NT, preferred_element_type=F32)
                ds = p * (dp - delta_ref[g, r0:s, :])
                dk = dk + lax.dot_general(ds.astype(BF16), qg, _TN, preferred_element_type=F32)
            dk_ref[...] = dk
            dv_ref[...] = dvv

        pl.when(pl.program_id(2) < nc)(lambda: run(0))
        pl.when(pl.program_id(2) >= nc)(lambda: run(n_ctx))

    gs = lambda d: pl.BlockSpec((None, group, s, d), lambda bi, hi, j: (bi, hi, 0, 0))
    ks = lambda d: pl.BlockSpec((None, None, tk, d), lambda bi, hi, j: (bi, hi, j, 0))
    return pl.pallas_call(
        body, out_shape=[jax.ShapeDtypeStruct((b, hk, s, dq), F32), jax.ShapeDtypeStruct((b, hk, s, dv), F32)],
        grid=(b, hk, s // tk),
        in_specs=[gs(dq), ks(dq), ks(dv), gs(1), gs(1), gs(dv)], out_specs=[ks(dq), ks(dv)],
        name=f"attn_dkv_{h}x{s}x{dq}", compiler_params=_params(),
    )(q, k, v, lse, delta, do)


@functools.partial(jax.custom_vjp, nondiff_argnums=(3, 4, 5))
def attention(q, k, v, group, n_ctx, scale):
    return _attn_fwd(q, k, v, group, n_ctx, scale)[0]


def _attention_fwd(q, k, v, group, n_ctx, scale):
    o, lse = _attn_fwd(q, k, v, group, n_ctx, scale)
    return o, (q, k, v, o, lse)


def _attention_bwd(group, n_ctx, scale, res, do):
    q, k, v, o, lse = res
    dq, delta = _attn_dq(q, k, v, o, lse, do, group, n_ctx, scale)
    dk, dv = _attn_dkv(q, k, v, lse, delta, do, group, n_ctx, scale)
    return dq, dk, dv


attention.defvjp(_attention_fwd, _attention_bwd)


def _na_geometry(i, nc, rows):
    r = i - nc
    rs = jnp.clip(r - NA_WIN_R // 2, 0, rows - NA_WIN_R)
    is_ctx = i < nc
    cls = jnp.where(is_ctx, NA_WIN_R, r - rs)
    return jnp.where(is_ctx, 0, rs), cls


def _na_scores(q_ref, k_ref, bias_ref, hd, n_ctx, start, scale):
    qv = (q_ref[hd] * scale).astype(BF16)
    kc = k_ref[hd, 0:n_ctx, :].astype(BF16)
    kb = k_ref[hd, pl.ds(start, NA_BAND), :].astype(BF16)
    s_c = lax.dot_general(qv, kc, _NT, preferred_element_type=F32)
    s_l = lax.dot_general(qv, kb, _NT, preferred_element_type=F32) + bias_ref[hd]
    return qv, kc, kb, s_c, s_l


NA_HEADS_FWD = 4
NA_HEADS_BWD = 2


def _na_specs(hp, s, dh, nc, rows):
    qs = lambda d: pl.BlockSpec((None, hp, GRID_W, d), lambda bi, hg, i: (bi, hg, i, 0))
    ks = pl.BlockSpec((None, hp, s, dh), lambda bi, hg, i: (bi, hg, 0, 0))
    bs = pl.BlockSpec((hp, None, GRID_W, NA_BAND), lambda bi, hg, i: (hg, _na_geometry(i, nc, rows)[1], 0, 0))
    return qs, ks, bs


def _na_fwd(q, k, v, bias, n_ctx):
    b, h, s, dh = q.shape
    nc = n_ctx // GRID_W
    rows = (s - n_ctx) // GRID_W
    scale = dh ** -0.5
    hp = math.gcd(h, NA_HEADS_FWD)

    def body(q_ref, k_ref, v_ref, bias_ref, o_ref, lse_ref):
        rs, _ = _na_geometry(pl.program_id(2), nc, rows)
        start = pl.multiple_of(n_ctx + rs * GRID_W, GRID_W)
        for hd in range(hp):
            _, _, _, s_c, s_l = _na_scores(q_ref, k_ref, bias_ref, hd, n_ctx, start, scale)
            m = jnp.maximum(jnp.max(s_c, axis=-1, keepdims=True), jnp.max(s_l, axis=-1, keepdims=True))
            p_c = jnp.exp(s_c - m)
            p_l = jnp.exp(s_l - m)
            l = jnp.sum(p_c, axis=-1, keepdims=True) + jnp.sum(p_l, axis=-1, keepdims=True)
            o = jnp.dot(p_c.astype(BF16), v_ref[hd, 0:n_ctx, :].astype(BF16), preferred_element_type=F32)
            o = o + jnp.dot(p_l.astype(BF16), v_ref[hd, pl.ds(start, NA_BAND), :].astype(BF16),
                            preferred_element_type=F32)
            o_ref[hd] = o / l
            lse_ref[hd] = m + jnp.log(l)

    qs, ks, bs = _na_specs(hp, s, dh, nc, rows)
    return pl.pallas_call(
        body, out_shape=[jax.ShapeDtypeStruct((b, h, s, dh), F32), jax.ShapeDtypeStruct((b, h, s, 1), F32)],
        grid=(b, h // hp, s // GRID_W), in_specs=[qs(dh), ks, ks, bs], out_specs=[qs(dh), qs(1)],
        name=f"na_f_{s}", compiler_params=_params(),
    )(q, k, v, bias)


def _na_bwd(q, k, v, bias, o, lse, do, n_ctx):
    b, h, s, dh = q.shape
    nc = n_ctx // GRID_W
    rows = (s - n_ctx) // GRID_W
    scale = dh ** -0.5
    n_cls = NA_WIN_R + 1
    hp = math.gcd(h, NA_HEADS_BWD)

    def body(q_ref, k_ref, v_ref, bias_ref, o_ref, lse_ref, do_ref, dq_ref, dk_ref, dv_ref, db_ref):
        i = pl.program_id(2)
        rs, cls = _na_geometry(i, nc, rows)
        _, cls_prev = _na_geometry(i - 1, nc, rows)
        start = pl.multiple_of(n_ctx + rs * GRID_W, GRID_W)
        first = jnp.logical_or(i == 0, cls != cls_prev)

        @pl.when(i == 0)
        def _():
            dk_ref[...] = jnp.zeros_like(dk_ref)
            dv_ref[...] = jnp.zeros_like(dv_ref)

        for hd in range(hp):
            qv, kc, kb, s_c, s_l = _na_scores(q_ref, k_ref, bias_ref, hd, n_ctx, start, scale)
            lse_v = lse_ref[hd]
            p_c = jnp.exp(s_c - lse_v)
            p_l = jnp.exp(s_l - lse_v)
            dov = do_ref[hd]
            dob = dov.astype(BF16)
            delta = jnp.sum(dov * o_ref[hd], axis=-1, keepdims=True)
            vc = v_ref[hd, 0:n_ctx, :].astype(BF16)
            vb = v_ref[hd, pl.ds(start, NA_BAND), :].astype(BF16)
            ds_c = p_c * (lax.dot_general(dob, vc, _NT, preferred_element_type=F32) - delta)
            ds_l = p_l * (lax.dot_general(dob, vb, _NT, preferred_element_type=F32) - delta)
            dsc_b = ds_c.astype(BF16)
            dsl_b = ds_l.astype(BF16)
            dq_ref[hd] = (jnp.dot(dsc_b, kc, preferred_element_type=F32)
                          + jnp.dot(dsl_b, kb, preferred_element_type=F32)) * scale
            dk_ref[hd, 0:n_ctx, :] += lax.dot_general(dsc_b, qv, _TN, preferred_element_type=F32)
            dk_ref[hd, pl.ds(start, NA_BAND), :] += lax.dot_general(dsl_b, qv, _TN, preferred_element_type=F32)
            dv_ref[hd, 0:n_ctx, :] += lax.dot_general(p_c.astype(BF16), dob, _TN, preferred_element_type=F32)
            dv_ref[hd, pl.ds(start, NA_BAND), :] += lax.dot_general(p_l.astype(BF16), dob, _TN, preferred_element_type=F32)

            @pl.when(first)
            def _(hd=hd, ds_l=ds_l):
                db_ref[hd] = ds_l

            @pl.when(jnp.logical_not(first))
            def _(hd=hd, ds_l=ds_l):
                db_ref[hd] += ds_l

    qs, ks, bs = _na_specs(hp, s, dh, nc, rows)
    dbs = pl.BlockSpec((None, hp, None, GRID_W, NA_BAND),
                       lambda bi, hg, i: (bi, hg, _na_geometry(i, nc, rows)[1], 0, 0))
    return pl.pallas_call(
        body,
        out_shape=[jax.ShapeDtypeStruct((b, h, s, dh), F32), jax.ShapeDtypeStruct((b, h, s, dh), F32),
                   jax.ShapeDtypeStruct((b, h, s, dh), F32), jax.ShapeDtypeStruct((b, h, n_cls, GRID_W, NA_BAND), F32)],
        grid=(b, h // hp, s // GRID_W), in_specs=[qs(dh), ks, ks, bs, qs(dh), qs(1), qs(dh)],
        out_specs=[qs(dh), ks, ks, dbs], name=f"na_b_{s}", compiler_params=_params(),
    )(q, k, v, bias, o, lse, do)


@functools.partial(jax.custom_vjp, nondiff_argnums=(4,))
def na_attention(q, k, v, bias, n_ctx):
    return _na_fwd(q, k, v, bias, n_ctx)[0]


def _na_attention_fwd(q, k, v, bias, n_ctx):
    o, lse = _na_fwd(q, k, v, bias, n_ctx)
    return o, (q, k, v, bias, o, lse)


def _na_attention_bwd(n_ctx, res, do):
    q, k, v, bias, o, lse = res
    dq, dk, dv, db = _na_bwd(q, k, v, bias, o, lse, do, n_ctx)
    return dq, dk, dv, jnp.sum(db, axis=0)


na_attention.defvjp(_na_attention_fwd, _na_attention_bwd)


def _na_onehots():
    q = np.arange(GRID_W)[:, None]
    col = np.arange(GRID_W)[None, :]
    cs = np.clip(q - NA_WIN_C // 2, 0, GRID_W - NA_WIN_C)
    valid = (col >= cs) & (col < cs + NA_WIN_C)
    cidx = col - q + (NA_WIN_C - 1)
    n_b = 2 * NA_WIN_C - 1
    col_hot = np.zeros((LANE, GRID_W * GRID_W), np.float32)
    for qq in range(GRID_W):
        for cc in range(GRID_W):
            if valid[qq, cc]:
                col_hot[cidx[qq, cc], qq * GRID_W + cc] = 1.0
    row_hot = np.zeros((NA_WIN_R, NA_WIN_R, 2 * NA_WIN_R - 1), np.float32)
    for c in range(NA_WIN_R):
        for j in range(NA_WIN_R):
            row_hot[c, j, j - c + NA_WIN_R - 1] = 1.0
    mask = np.where(valid, 0.0, NEG).astype(np.float32)
    return col_hot, row_hot, mask, n_b


def na_bias_table(rpb):
    h = rpb.shape[0]
    col_hot, row_hot, mask, n_b = _na_onehots()
    t1 = jnp.einsum("cja,hab->hcjb", jnp.asarray(row_hot), rpb)
    t1 = jnp.pad(t1.reshape(h * NA_WIN_R * NA_WIN_R, n_b), ((0, 0), (0, LANE - n_b)))
    t2 = linear(t1, jnp.asarray(col_hot), True)
    t2 = t2.reshape(h, NA_WIN_R, NA_WIN_R, GRID_W, GRID_W) + jnp.asarray(mask)
    tab = jnp.transpose(t2, (0, 1, 3, 2, 4)).reshape(h, NA_WIN_R, GRID_W, NA_BAND)
    return jnp.concatenate([tab, jnp.full((h, 1, GRID_W, NA_BAND), NEG, F32)], axis=1)


def _cmul(ar, ai, br, bi):
    return ar * br - ai * bi, ar * bi + ai * br


def _s5_chunk(n_ctx):
    return min(256, n_ctx)


def _s5_tables(a_re, a_im, t_len, rev):
    a_re, a_im = lax.stop_gradient(a_re), lax.stop_gradient(a_im)
    mag = jnp.sqrt(a_re * a_re + a_im * a_im)
    th = jnp.arctan2(a_im, a_re)
    t = jnp.arange(t_len + 1, dtype=F32)[:, None]
    pm = jnp.where(t == 0, 1.0, jnp.exp(t * jnp.log(jnp.maximum(mag, 1e-37))) * (mag > 0))
    pw = jnp.stack([pm * jnp.cos(t * th), pm * jnp.sin(t * th)])
    steps = jnp.concatenate([pw[:, min(2 ** i, t_len)][:, None] for i in range(8)], axis=1)
    tile = pw[:, 1:9]
    a8k = pw[:, 0:t_len:8]
    if rev:
        tile, a8k = tile[:, ::-1], a8k[:, ::-1]
    misc = jnp.concatenate([pw[:, t_len:t_len + 1], jnp.zeros((2, 7, pw.shape[-1]), F32)], axis=1)
    return jnp.concatenate([steps, tile, misc, a8k], axis=1)


def _scan_chunk(x_re, x_im, tab_ref, hin_re, hin_im, rev, t_len, xs_ref, es_ref):
    outs = [_scan_slab(x_re[:, k:k + LANE], x_im[:, k:k + LANE], tab_ref, hin_re[:, k:k + LANE], hin_im[:, k:k + LANE],
                       rev, t_len, xs_ref, es_ref, k) for k in range(0, x_re.shape[-1], LANE)]
    return tuple(jnp.concatenate([o[t] for o in outs], axis=-1) for t in range(4))


def _scan_slab(x_re, x_im, tab_ref, hin_re, hin_im, rev, t_len, xs_ref, es_ref, k0):
    lanes = LANE
    n2 = t_len // 8
    tab_ref = tab_ref.at[:, :, k0:k0 + LANE]
    rin = lax.broadcasted_iota(jnp.int32, (t_len, lanes), 0) & 7
    for li, sh in enumerate((1, 2, 4)):
        m_re, m_im = tab_ref[0, li:li + 1, :], tab_ref[1, li:li + 1, :]
        amt = sh if not rev else t_len - sh
        c_re, c_im = _cmul(m_re, m_im, pltpu.roll(x_re, amt, 0), pltpu.roll(x_im, amt, 0))
        ok = (rin >= sh) if not rev else (rin < 8 - sh)
        x_re = x_re + jnp.where(ok, c_re, 0.0)
        x_im = x_im + jnp.where(ok, c_im, 0.0)
    xr_ref, xi_ref = xs_ref
    xr_ref[...] = x_re
    xi_ref[...] = x_im
    off = 0 if rev else 7
    e_re = xr_ref[pl.ds(off, n2, stride=8), :]
    e_im = xi_ref[pl.ds(off, n2, stride=8), :]
    row2 = lax.broadcasted_iota(jnp.int32, (n2, lanes), 0)
    sh, li = 1, 3
    while sh < n2:
        m_re, m_im = tab_ref[0, li:li + 1, :], tab_ref[1, li:li + 1, :]
        amt = sh if not rev else n2 - sh
        c_re, c_im = _cmul(m_re, m_im, pltpu.roll(e_re, amt, 0), pltpu.roll(e_im, amt, 0))
        ok = (row2 >= sh) if not rev else (row2 < n2 - sh)
        e_re = e_re + jnp.where(ok, c_re, 0.0)
        e_im = e_im + jnp.where(ok, c_im, 0.0)
        sh, li = sh * 2, li + 1
    es_ref[0] = e_re
    es_ref[1] = e_im
    last = 0 if rev else n2 - 1
    t_re, t_im = _cmul(tab_ref[0, 16:17, :], tab_ref[1, 16:17, :], hin_re, hin_im)
    hout_re = es_ref[0, last:last + 1, :] + t_re
    hout_im = es_ref[1, last:last + 1, :] + t_im
    amt = 1 if not rev else n2 - 1
    ok = (row2 >= 1) if not rev else (row2 < n2 - 1)
    k_re, k_im = _cmul(tab_ref[0, 24:24 + n2, :], tab_ref[1, 24:24 + n2, :], hin_re, hin_im)
    c_re = jnp.where(ok, pltpu.roll(e_re, amt, 0), 0.0) + k_re
    c_im = jnp.where(ok, pltpu.roll(e_im, amt, 0), 0.0) + k_im
    tp_re, tp_im = tab_ref[0, 8:16, :][None], tab_ref[1, 8:16, :][None]
    add_re, add_im = _cmul(tp_re, tp_im, c_re[:, None, :], c_im[:, None, :])
    h_re = xr_ref[...] + add_re.reshape(t_len, lanes)
    h_im = xi_ref[...] + add_im.reshape(t_len, lanes)
    return h_re, h_im, hout_re, hout_im


def _s5_order(j, n_chunks, nc, rev):
    if not rev:
        return j
    return jnp.where(j < nc, nc - 1 - j, n_chunks - 1 - (j - nc))


def _s5_fwd(u, tab, b_bd, c_bd, n_ctx, rev):
    b, s, w = u.shape
    lanes = b_bd.shape[-1]
    t_len = _s5_chunk(n_ctx)
    n_chunks, nc = s // t_len, n_ctx // t_len

    def body(u_ref, tab_ref, b_ref, c_ref, y_ref, h_ref, hin_ref, carry_ref, xr_ref, xi_ref, es_ref):
        xs_ref = (xr_ref, xi_ref)

        @pl.when(pl.program_id(1) == 0)
        def _():
            carry_ref[...] = jnp.zeros_like(carry_ref)

        ub = u_ref[...].astype(BF16)
        x_re = jnp.dot(ub, b_ref[0].astype(BF16), preferred_element_type=F32)
        x_im = jnp.dot(ub, b_ref[1].astype(BF16), preferred_element_type=F32)
        hin_re, hin_im = carry_ref[0, 0:1, :], carry_ref[1, 0:1, :]
        hin_ref[...] = carry_ref[...]
        h_re, h_im, ho_re, ho_im = _scan_chunk(x_re, x_im, tab_ref, hin_re, hin_im, rev, t_len, xs_ref, es_ref)
        carry_ref[0] = jnp.broadcast_to(ho_re, (8, lanes))
        carry_ref[1] = jnp.broadcast_to(ho_im, (8, lanes))
        h_ref[0] = h_re
        h_ref[1] = h_im
        y_ref[...] = (jnp.dot(h_re.astype(BF16), c_ref[0].astype(BF16), preferred_element_type=F32)
                      - jnp.dot(h_im.astype(BF16), c_ref[1].astype(BF16), preferred_element_type=F32))

    order = lambda j: _s5_order(j, n_chunks, nc, rev)
    whole = lambda arr: pl.BlockSpec(arr.shape, lambda bi, j: (0,) * arr.ndim)
    return pl.pallas_call(
        body,
        out_shape=[jax.ShapeDtypeStruct((b, s, w), F32), jax.ShapeDtypeStruct((2, b, s, lanes), F32),
                   jax.ShapeDtypeStruct((2, b, n_chunks, 8, lanes), F32)],
        grid=(b, n_chunks),
        in_specs=[pl.BlockSpec((None, t_len, w), lambda bi, j: (bi, order(j), 0)), whole(tab), whole(b_bd), whole(c_bd)],
        out_specs=[pl.BlockSpec((None, t_len, w), lambda bi, j: (bi, order(j), 0)),
                   pl.BlockSpec((2, None, t_len, lanes), lambda bi, j: (0, bi, order(j), 0)),
                   pl.BlockSpec((2, None, None, 8, lanes), lambda bi, j: (0, bi, order(j), 0, 0))],
        scratch_shapes=[pltpu.VMEM((2, 8, lanes), F32), pltpu.VMEM((t_len, LANE), F32), pltpu.VMEM((t_len, LANE), F32),
                        pltpu.VMEM((2, t_len // 8, LANE), F32)],
        name=f"s5_f_{s}_{int(rev)}", compiler_params=_params(),
    )(u, tab, b_bd, c_bd)


def _s5_bwd(u, tab_adj, b_bd, c_bd, h, hin, dy, n_ctx, rev):
    b, s, w = u.shape
    lanes = b_bd.shape[-1]
    t_len = _s5_chunk(n_ctx)
    n_chunks, nc = s // t_len, n_ctx // t_len
    arev = not rev

    def body(u_ref, tab_ref, b_ref, c_ref, h_ref, hin_ref, dy_ref, du_ref, db_ref, dc_ref, da_ref,
             carry_ref, xr_ref, xi_ref, es_ref):
        xs_ref = (xr_ref, xi_ref)
        first = jnp.logical_and(pl.program_id(0) == 0, pl.program_id(1) == 0)

        @pl.when(pl.program_id(1) == 0)
        def _():
            carry_ref[...] = jnp.zeros_like(carry_ref)

        dyv = dy_ref[...]
        dyb = dyv.astype(BF16)
        dn = (((1,), (1,)), ((), ()))
        dt = (((0,), (0,)), ((), ()))
        x_re = lax.dot_general(dyb, c_ref[0].astype(BF16), dn, preferred_element_type=F32)
        x_im = -lax.dot_general(dyb, c_ref[1].astype(BF16), dn, preferred_element_type=F32)
        g_re, g_im, go_re, go_im = _scan_chunk(x_re, x_im, tab_ref, carry_ref[0, 0:1, :], carry_ref[1, 0:1, :],
                                               arev, t_len, xs_ref, es_ref)
        carry_ref[0] = jnp.broadcast_to(go_re, (8, lanes))
        carry_ref[1] = jnp.broadcast_to(go_im, (8, lanes))
        h_re, h_im = h_ref[0], h_ref[1]
        gb_re, gb_im = g_re.astype(BF16), g_im.astype(BF16)
        du_ref[...] = (lax.dot_general(gb_re, b_ref[0].astype(BF16), dn, preferred_element_type=F32)
                       + lax.dot_general(gb_im, b_ref[1].astype(BF16), dn, preferred_element_type=F32))
        ub = u_ref[...].astype(BF16)
        db_re = lax.dot_general(ub, gb_re, dt, preferred_element_type=F32)
        db_im = lax.dot_general(ub, gb_im, dt, preferred_element_type=F32)
        dc_re = lax.dot_general(h_re.astype(BF16), dyb, dt, preferred_element_type=F32)
        dc_im = -lax.dot_general(h_im.astype(BF16), dyb, dt, preferred_element_type=F32)
        row = lax.broadcasted_iota(jnp.int32, (t_len, lanes), 0)
        amt = 1 if not rev else t_len - 1
        edge = (row == 0) if not rev else (row == t_len - 1)
        hp_re = jnp.where(edge, hin_ref[0, 0:1, :], pltpu.roll(h_re, amt, 0))
        hp_im = jnp.where(edge, hin_ref[1, 0:1, :], pltpu.roll(h_im, amt, 0))
        da_re = jnp.sum(g_re * hp_re + g_im * hp_im, axis=0, keepdims=True)
        da_im = jnp.sum(g_im * hp_re - g_re * hp_im, axis=0, keepdims=True)

        @pl.when(first)
        def _():
            db_ref[0], db_ref[1] = db_re, db_im
            dc_ref[0], dc_ref[1] = dc_re, dc_im
            da_ref[0] = jnp.broadcast_to(da_re, (8, lanes))
            da_ref[1] = jnp.broadcast_to(da_im, (8, lanes))

        @pl.when(jnp.logical_not(first))
        def _():
            db_ref[0] += db_re
            db_ref[1] += db_im
            dc_ref[0] += dc_re
            dc_ref[1] += dc_im
            da_ref[0] += jnp.broadcast_to(da_re, (8, lanes))
            da_ref[1] += jnp.broadcast_to(da_im, (8, lanes))

    order = lambda j: _s5_order(n_chunks - 1 - j, n_chunks, nc, rev)
    whole = lambda arr: pl.BlockSpec(arr.shape, lambda bi, j: (0,) * arr.ndim)
    us = pl.BlockSpec((None, t_len, w), lambda bi, j: (bi, order(j), 0))
    return pl.pallas_call(
        body,
        out_shape=[jax.ShapeDtypeStruct((b, s, w), F32), jax.ShapeDtypeStruct(b_bd.shape, F32),
                   jax.ShapeDtypeStruct(c_bd.shape, F32), jax.ShapeDtypeStruct((2, 8, lanes), F32)],
        grid=(b, n_chunks),
        in_specs=[us, whole(tab_adj), whole(b_bd), whole(c_bd),
                  pl.BlockSpec((2, None, t_len, lanes), lambda bi, j: (0, bi, order(j), 0)),
                  pl.BlockSpec((2, None, None, 8, lanes), lambda bi, j: (0, bi, order(j), 0, 0)), us],
        out_specs=[us, whole(b_bd), whole(c_bd), pl.BlockSpec((2, 8, lanes), lambda bi, j: (0, 0, 0))],
        scratch_shapes=[pltpu.VMEM((2, 8, lanes), F32), pltpu.VMEM((t_len, LANE), F32), pltpu.VMEM((t_len, LANE), F32),
                        pltpu.VMEM((2, t_len // 8, LANE), F32)],
        name=f"s5_b_{s}_{int(rev)}", compiler_params=_params(),
    )(u, tab_adj, b_bd, c_bd, h, hin, dy)


@functools.partial(jax.custom_vjp, nondiff_argnums=(4, 5))
def s5_direction(u, a, b_bd, c_bd, n_ctx, rev):
    tab = _s5_tables(a[0], a[1], _s5_chunk(n_ctx), rev)
    return _s5_fwd(u, tab, b_bd, c_bd, n_ctx, rev)[0]


def _s5_direction_fwd(u, a, b_bd, c_bd, n_ctx, rev):
    tab = _s5_tables(a[0], a[1], _s5_chunk(n_ctx), rev)
    y, h, hin = _s5_fwd(u, tab, b_bd, c_bd, n_ctx, rev)
    return y, (u, a, b_bd, c_bd, h, hin)


def _s5_direction_bwd(n_ctx, rev, res, dy):
    u, a, b_bd, c_bd, h, hin = res
    tab_adj = _s5_tables(a[0], -a[1], _s5_chunk(n_ctx), not rev)
    du, db, dc, da = _s5_bwd(u, tab_adj, b_bd, c_bd, h, hin, dy, n_ctx, rev)
    return du, da[:, 0, :], db, dc


s5_direction.defvjp(_s5_direction_fwd, _s5_direction_bwd)


def _s5_discretize(lam_re, lam_im, log_dt, b_re, b_im):
    dt = jnp.exp(log_dt)[:, None]
    mag = jnp.exp(lam_re * dt)
    a_re = mag * jnp.cos(lam_im * dt)
    a_im = mag * jnp.sin(lam_im * dt)
    den = jnp.square(lam_re) + jnp.square(lam_im)
    f_re = ((a_re - 1.0) * lam_re + a_im * lam_im) / den
    f_im = (a_im * lam_re - (a_re - 1.0) * lam_im) / den
    bb_re = f_re[..., None] * b_re - f_im[..., None] * b_im
    bb_im = f_re[..., None] * b_im + f_im[..., None] * b_re
    return a_re, a_im, bb_re, bb_im


def _block_diag(t):
    g, r, c = t.shape
    return (jnp.eye(g, dtype=F32)[:, None, :, None] * t[:, :, None, :]).reshape(g * r, g * c)


def _loss_head(y, target):
    b, n, d = y.shape
    ts = _pick(n, (256, 128, 64))

    def body(y_ref, t_ref, loss_ref, dy_ref):
        first = jnp.logical_and(pl.program_id(0) == 0, pl.program_id(1) == 0)
        err = y_ref[...] - t_ref[...]
        dy_ref[...] = err * (1.0 / d)
        part = 0.5 * jnp.sum(jnp.sum(err * err, axis=-1, keepdims=True) * (1.0 / d), axis=0, keepdims=True)
        part = jnp.broadcast_to(part, (8, LANE))

        @pl.when(first)
        def _():
            loss_ref[...] = part

        @pl.when(jnp.logical_not(first))
        def _():
            loss_ref[...] += part

    blk = pl.BlockSpec((None, ts, d), lambda bi, i: (bi, i, 0))
    return pl.pallas_call(
        body, out_shape=[jax.ShapeDtypeStruct((8, LANE), F32), jax.ShapeDtypeStruct((b, n, d), F32)],
        grid=(b, n // ts), in_specs=[blk, blk], out_specs=[pl.BlockSpec((8, LANE), lambda bi, i: (0, 0)), blk],
        name="loss_head", compiler_params=_params(),
    )(y, target)


def _adamw(w, g, m, v):
    shape = w.shape
    n = int(np.prod(shape))
    cols = shape[-1]
    r = n // cols
    tr = _pick(r, (512, 256, 128, 64, 32, 16, 8))
    c1 = 1.0 / (1.0 - ADAM_B1 ** ADAM_STEP)
    c2 = 1.0 / (1.0 - ADAM_B2 ** ADAM_STEP)

    def body(w_ref, g_ref, m_ref, v_ref, d_ref, mo_ref, vo_ref):
        gv = g_ref[...]
        m2 = ADAM_B1 * m_ref[...] + (1.0 - ADAM_B1) * gv
        v2 = ADAM_B2 * v_ref[...] + (1.0 - ADAM_B2) * (gv * gv)
        d_ref[...] = -ADAM_LR * ((m2 * c1) / (jnp.sqrt(v2 * c2) + ADAM_EPS) + ADAM_WD * w_ref[...])
        mo_ref[...] = m2
        vo_ref[...] = v2

    blk = pl.BlockSpec((tr, cols), lambda i: (i, 0))
    outs = pl.pallas_call(
        body, out_shape=[jax.ShapeDtypeStruct((r, cols), F32)] * 3, grid=(r // tr,),
        in_specs=[blk] * 4, out_specs=[blk] * 3, name=f"adamw_{r}x{cols}", compiler_params=_params(),
    )(*[t.reshape(r, cols) for t in (w, g, m, v)])
    return tuple(o.reshape(shape) for o in outs)


def _sum_rows(x, n):
    _, r, c = x.shape
    tr = _pick(r, (512, 256, 128, 64, 32, 16, 8))

    def body(x_ref, o_ref):
        acc = x_ref[0]
        for j in range(1, n):
            acc = acc + x_ref[j]
        o_ref[...] = acc

    return pl.pallas_call(
        body, out_shape=jax.ShapeDtypeStruct((r, c), F32), grid=(r // tr,),
        in_specs=[pl.BlockSpec((n, tr, c), lambda i: (0, i, 0))], out_specs=pl.BlockSpec((tr, c), lambda i: (i, 0)),
        name=f"sum{n}_{r}x{c}", compiler_params=_params(),
    )(x)


def _accumulate(parts, out_dtype):
    r, c = parts[0].shape[-2:]
    tr = _pick(r, (512, 256, 128, 64, 32, 16))

    def body(*refs):
        acc = None
        for ref in refs[:-1]:
            terms = [ref[j] for j in range(ref.shape[0])] if len(ref.shape) == 3 else [ref[...]]
            for t in terms:
                acc = t.astype(F32) if acc is None else acc + t.astype(F32)
        refs[-1][...] = acc.astype(out_dtype)

    specs = [pl.BlockSpec((p.shape[0], tr, c), lambda i: (0, i, 0)) if p.ndim == 3 else pl.BlockSpec((tr, c), lambda i: (i, 0))
             for p in parts]
    tag = "_".join(str(p.shape[0]) if p.ndim == 3 else "1" for p in parts)
    return pl.pallas_call(
        body, out_shape=jax.ShapeDtypeStruct((r, c), out_dtype), grid=(r // tr,), in_specs=specs,
        out_specs=pl.BlockSpec((tr, c), lambda i: (i, 0)), name=f"accumulate_{tag}_{r}x{c}_{jnp.dtype(out_dtype).name}",
        compiler_params=_params(),
    )(*parts)


def _add2(x, y):
    shape = x.shape
    c = shape[-1]
    r = int(np.prod(shape)) // c
    tr = _pick(r, (512, 256, 128, 64, 32, 16, 8))

    def body(x_ref, y_ref, o_ref):
        o_ref[...] = x_ref[...] + y_ref[...]

    blk = pl.BlockSpec((tr, c), lambda i: (i, 0))
    return pl.pallas_call(
        body, out_shape=jax.ShapeDtypeStruct((r, c), F32), grid=(r // tr,), in_specs=[blk, blk], out_specs=blk,
        name=f"add2_{r}x{c}", compiler_params=_params(),
    )(x.reshape(r, c), y.reshape(r, c)).reshape(shape)


_FLIPS = ((1, 0), (0, 1), (1, 1))


def _me():
    return lax.axis_index("x"), lax.axis_index("y"), lax.axis_index("c")


def allgather8(v):
    m_per, n = v.shape

    def body(x_ref, out_ref, send_sems, recv_sems, local_sem):
        x, y, c = _me()
        me, sibling = (x, y, c), (x, y, 1 - c)
        chips = [(1 - x, y), (x, 1 - y), (1 - x, 1 - y)]

        def rows(px, py, pc):
            return out_ref.at[pl.ds((4 * px + 2 * py + pc) * m_per, m_per), :]

        def copy(k, block, to, src=None):
            return pltpu.make_async_remote_copy(
                src_ref=rows(*block) if src is None else src, dst_ref=rows(*block),
                send_sem=send_sems.at[k], recv_sem=recv_sems.at[k], device_id=to, device_id_type=MESH)

        mine = pltpu.make_async_copy(x_ref, rows(*me), local_sem)
        mine.start()
        first = [copy(0, me, sibling, src=x_ref)]
        first += [copy(1 + j, me, (*chip, c), src=x_ref) for j, chip in enumerate(chips)]
        for cp in first:
            cp.start()
        passed = [copy(4 + j, (*chip, c), sibling) for j, chip in enumerate(chips)]
        for j, chip in enumerate(chips):
            copy(1 + j, (*chip, c), me).wait_recv()
            passed[j].start()
        copy(0, sibling, me).wait_recv()
        for j, chip in enumerate(chips):
            copy(4 + j, (*chip, 1 - c), me).wait_recv()
        for cp in first + passed:
            cp.wait_send()
        mine.wait()

    return pl.pallas_call(
        body, out_shape=jax.ShapeDtypeStruct((N_DEV * m_per, n), v.dtype), in_specs=[VMEM_SPEC], out_specs=VMEM_SPEC,
        scratch_shapes=[pltpu.SemaphoreType.DMA((7,)), pltpu.SemaphoreType.DMA((7,)), pltpu.SemaphoreType.DMA],
        name=f"allgather8_{m_per}x{n}", compiler_params=_params(),
    )(v)


def _row_chunks(rows, tile_rows, want):
    n = want
    while n > 1 and rows % (n * tile_rows):
        n //= 2
    return [(i * (rows // n), rows // n) for i in range(n)]


def _remote(src, dst, send_sem, recv_sem, to):
    return pltpu.make_async_remote_copy(src_ref=src, dst_ref=dst, send_sem=send_sem, recv_sem=recv_sem, device_id=to,
                                        device_id_type=MESH)


def plane_allgather(big, small):
    rows = big.shape[0]
    rh = rows // 2
    tile = 16 if big.dtype == BF16 else 8
    ch_full = _row_chunks(rows, tile, 8)
    ch_half = _row_chunks(rh, tile, 4)

    def body(big_ref, small_ref, obig_ref, osmall_ref, send_sems, recv_sems, fwd_send, fwd_recv, own_send, own_recv):
        x, y, c = _me()
        me = 2 * x + y
        sibling = (x, y, 1 - c)
        mine = pl.ds(c * rh, rh)
        other = pl.ds((1 - c) * rh, rh)
        peers = [((x + fx) & 1, (y + fy) & 1) for fx, fy in _FLIPS]
        for st, sz in ch_full:
            sl = pl.ds(st, sz)
            _remote(big_ref.at[sl], obig_ref.at[me, sl], own_send.at[0], own_recv.at[0], sibling).start()
        _remote(small_ref, osmall_ref.at[me], own_send.at[1], own_recv.at[1], sibling).start()
        for j, (px, py) in enumerate(peers):
            for st, sz in ch_half:
                sl = pl.ds(c * rh + st, sz)
                _remote(big_ref.at[sl], obig_ref.at[me, sl], send_sems.at[j], recv_sems.at[j], (px, py, c)).start()
            _remote(small_ref, osmall_ref.at[me], send_sems.at[3 + j], recv_sems.at[3 + j], (px, py, c)).start()
        for j, (px, py) in enumerate(peers):
            pidx = 2 * px + py
            _remote(big_ref.at[mine], obig_ref.at[pidx, mine], send_sems.at[j], recv_sems.at[j], (px, py, c)).wait_recv()
            for st, sz in ch_half:
                sl = pl.ds(c * rh + st, sz)
                _remote(obig_ref.at[pidx, sl], obig_ref.at[pidx, sl], fwd_send.at[j], fwd_recv.at[j], sibling).start()
            _remote(small_ref, osmall_ref.at[pidx], send_sems.at[3 + j], recv_sems.at[3 + j], (px, py, c)).wait_recv()
        for j, (px, py) in enumerate(peers):
            pidx = 2 * px + py
            _remote(obig_ref.at[pidx, other], obig_ref.at[pidx, other], fwd_send.at[j], fwd_recv.at[j], sibling).wait_recv()
        for j, (px, py) in enumerate(peers):
            pidx = 2 * px + py
            _remote(big_ref.at[mine], obig_ref.at[me, mine], send_sems.at[j], recv_sems.at[j], (px, py, c)).wait_send()
            _remote(small_ref, osmall_ref.at[me], send_sems.at[3 + j], recv_sems.at[3 + j], (px, py, c)).wait_send()
            _remote(obig_ref.at[pidx, mine], obig_ref.at[pidx, mine], fwd_send.at[j], fwd_recv.at[j], sibling).wait_send()
        _remote(big_ref, obig_ref.at[me], own_send.at[0], own_recv.at[0], sibling).wait()
        _remote(small_ref, osmall_ref.at[me], own_send.at[1], own_recv.at[1], sibling).wait()

    return pl.pallas_call(
        body, out_shape=[jax.ShapeDtypeStruct((N_PLANE,) + big.shape, big.dtype),
                         jax.ShapeDtypeStruct((N_PLANE,) + small.shape, small.dtype)],
        in_specs=[ANY, ANY], out_specs=[ANY, ANY],
        scratch_shapes=[pltpu.SemaphoreType.DMA((6,)), pltpu.SemaphoreType.DMA((6,)), pltpu.SemaphoreType.DMA((3,)),
                        pltpu.SemaphoreType.DMA((3,)), pltpu.SemaphoreType.DMA((2,)), pltpu.SemaphoreType.DMA((2,))],
        name="plane_allgather", compiler_params=_params(),
    )(big, small)


def plane_scatter(p):
    tile = 16 if p.dtype == BF16 else 8
    chunks = _row_chunks(p.shape[1], tile, 4)

    def body(p_ref, out_ref, send_sems, recv_sems):
        x, y, c = _me()
        peers = [((x + fx) & 1, (y + fy) & 1) for fx, fy in _FLIPS]
        for j, (px, py) in enumerate(peers):
            for st, sz in chunks:
                sl = pl.ds(st, sz)
                _remote(p_ref.at[2 * px + py, sl], out_ref.at[j, sl], send_sems.at[j], recv_sems.at[j], (px, py, c)).start()
        for j, (px, py) in enumerate(peers):
            _remote(p_ref.at[0], out_ref.at[j], send_sems.at[j], recv_sems.at[j], (px, py, c)).wait_recv()
        for j, (px, py) in enumerate(peers):
            _remote(p_ref.at[0], out_ref.at[j], send_sems.at[j], recv_sems.at[j], (px, py, c)).wait_send()

    return pl.pallas_call(
        body, out_shape=jax.ShapeDtypeStruct((len(_FLIPS),) + p.shape[1:], p.dtype), in_specs=[ANY], out_specs=ANY,
        scratch_shapes=[pltpu.SemaphoreType.DMA((3,)), pltpu.SemaphoreType.DMA((3,))],
        name="plane_scatter", compiler_params=_params(),
    )(p)


def sibling_swap(s):
    tile = 16 if s.dtype == BF16 else 8
    chunks = _row_chunks(s.shape[0], tile, 8)

    def body(s_ref, got_ref, send_sem, recv_sem):
        x, y, c = _me()
        for st, sz in chunks:
            sl = pl.ds(st, sz)
            _remote(s_ref.at[sl], got_ref.at[sl], send_sem, recv_sem, (x, y, 1 - c)).start()
        _remote(s_ref, got_ref, send_sem, recv_sem, (x, y, 1 - c)).wait()

    return pl.pallas_call(
        body, out_shape=jax.ShapeDtypeStruct(s.shape, s.dtype), in_specs=[ANY], out_specs=ANY,
        scratch_shapes=[pltpu.SemaphoreType.DMA, pltpu.SemaphoreType.DMA],
        name="sibling_swap", compiler_params=_params(),
    )(s)


def _heads(t, n_heads):
    b, s, w = t.shape
    return jnp.transpose(t.reshape(b, s, n_heads, w // n_heads), (0, 2, 1, 3)).reshape(b * n_heads, s, w // n_heads)


def _unheads(t, b):
    bh, s, d = t.shape
    return jnp.transpose(t.reshape(b, bh // b, s, d), (0, 2, 1, 3)).reshape(b, s, (bh // b) * d)


def _op(cache, fn, name, kinds, out_dims, **kw):
    key = (name, tuple(out_dims), tuple(sorted(kw.items())))
    if key not in cache:
        cache[key] = make_rowwise(fn, name, kinds, out_dims, **kw)
    return cache[key]


def _even_mixer(ops, a, w, n_ctx):
    b, s, d = a.shape
    proj = linear(a.reshape(b * s, d), w["e_w_in"]).reshape(b, s, -1)
    q, k, v, u = jnp.split(proj, [GQA_Q_W, GQA_Q_W + GQA_KV_W, GQA_Q_W + 2 * GQA_KV_W], axis=-1)
    cos, sin = _rope_tables(n_ctx, s - n_ctx, HEAD_DIM, 0, HEAD_DIM)
    rot = jnp.asarray(_rope_matrix(HEAD_DIM, 0, HEAD_DIM))
    nr = _op(ops, _fn_norm_rope, "norm_rope", ("row", "tab", "tab", "const", "glob"), (HEAD_DIM,), whole_seq=True)
    qh = nr(_heads(q, GQA_Q_HEADS), cos, sin, rot, w["e_g_q"][None])[0]
    kh = nr(_heads(k, GQA_KV_HEADS), cos, sin, rot, w["e_g_k"][None])[0]
    vh = _heads(v, GQA_KV_HEADS)
    att = attention(qh.reshape(b, GQA_Q_HEADS, s, HEAD_DIM), kh.reshape(b, GQA_KV_HEADS, s, HEAD_DIM),
                    vh.reshape(b, GQA_KV_HEADS, s, HEAD_DIM), GQA_Q_HEADS // GQA_KV_HEADS, n_ctx, HEAD_DIM ** -0.5)
    att = _unheads(att.reshape(b * GQA_Q_HEADS, s, HEAD_DIM), b)
    ys = []
    for dr in range(2):
        a_re, a_im, bb_re, bb_im = _s5_discretize(w["ssm_lam_re"][dr], w["ssm_lam_im"][dr], w["ssm_log_dt"][dr],
                                                  w["ssm_b_re"][dr], w["ssm_b_im"][dr])
        a_flat = jnp.stack([a_re.reshape(-1), a_im.reshape(-1)])
        b_bd = jnp.stack([_block_diag(jnp.swapaxes(bb_re, 1, 2)), _block_diag(jnp.swapaxes(bb_im, 1, 2))])
        c_bd = jnp.stack([_block_diag(jnp.swapaxes(w["ssm_c_re"][dr], 1, 2)),
                          _block_diag(jnp.swapaxes(w["ssm_c_im"][dr], 1, 2))])
        ys.append(s5_direction(u, a_flat, b_bd, c_bd, n_ctx, dr == 1))
    pre = _op(ops, _fn_glu_pre, "glu_pre", ("row", "row", "row", "glob"), (SSM_WIDTH,))
    post = _op(ops, _fn_glu_post, "glu_post", ("row", "row", "glob"), (SSM_WIDTH,))
    z = pre(u, ys[0], ys[1], w["ssm_d"][None])[0]
    t = linear(z.reshape(b * s, SSM_WIDTH), w["ssm_w_glu"]).reshape(b, s, SSM_WIDTH)
    ssm = post(z, t, w["ssm_b_glu"][None])[0]
    mix = jnp.concatenate([att, ssm], axis=-1)
    return linear(mix.reshape(b * s, -1), w["e_w_out"]).reshape(b, s, d)


def _odd_mixer(ops, a, w, n_ctx):
    b, s, d = a.shape
    w_in = jnp.pad(w["o_w_in"], ((0, 0), (0, ODD_IN_PAD - ODD_IN_W)))
    proj = linear(a.reshape(b * s, d), w_in).reshape(b, s, -1)
    c1 = MLA_Q_RANK
    c2 = c1 + MLA_KV_RANK
    c3 = c2 + MLA_ROPE
    cq, ckv, kr, nq, nk, nv, _ = jnp.split(proj, [c1, c2, c3, c3 + NA_W, c3 + 2 * NA_W, ODD_IN_W], axis=-1)
    nrm = lambda wd: _op(ops, _fn_norm, f"norm{wd}", ("row", "glob"), (wd,))
    cqn = nrm(MLA_Q_RANK)(cq, w["mla_g_cq"][None])[0]
    ckvn = nrm(MLA_KV_RANK)(ckv, w["mla_g_ckv"][None])[0]
    q = linear(cqn.reshape(b * s, -1), w["mla_w_uq"]).reshape(b, s, -1)
    kv = linear(ckvn.reshape(b * s, -1), w["mla_w_ukv"]).reshape(b, s, MLA_HEADS, MLA_NOPE + MLA_V)
    k_nope = kv[..., :MLA_NOPE].reshape(b, s, MLA_HEADS * MLA_NOPE)
    mv = kv[..., MLA_NOPE:].reshape(b, s, MLA_HEADS * MLA_V)
    kh = jnp.concatenate([_heads(k_nope, MLA_HEADS),
                          jnp.broadcast_to(kr[:, None], (b, MLA_HEADS, s, MLA_ROPE)).reshape(b * MLA_HEADS, s, MLA_ROPE)],
                         axis=-1)
    cos, sin = _rope_tables(n_ctx, s - n_ctx, MLA_QK, MLA_NOPE, MLA_ROPE)
    rot = jnp.asarray(_rope_matrix(MLA_QK, MLA_NOPE, MLA_ROPE))
    nr = _op(ops, _fn_norm_rope, "norm_rope", ("row", "tab", "tab", "const", "glob"), (MLA_QK,), whole_seq=True)
    mq = nr(_heads(q, MLA_HEADS), cos, sin, rot, w["mla_g_q"][None])[0]
    mk = nr(kh, cos, sin, rot, w["mla_g_k"][None])[0]
    mla = attention(mq.reshape(b, MLA_HEADS, s, MLA_QK), mk.reshape(b, MLA_HEADS, s, MLA_QK),
                    _heads(mv, MLA_HEADS).reshape(b, MLA_HEADS, s, MLA_V), 1, n_ctx, MLA_QK ** -0.5)
    mla = _unheads(mla.reshape(b * MLA_HEADS, s, MLA_V), b)
    nh = _op(ops, _fn_norm, "normh", ("row", "glob"), (HEAD_DIM,), whole_seq=True)
    nqh = nh(_heads(nq, NA_HEADS), w["na_g_q"][None])[0]
    nkh = nh(_heads(nk, NA_HEADS), w["na_g_k"][None])[0]
    r4 = lambda t: t.reshape(b, NA_HEADS, s, HEAD_DIM)
    na = na_attention(r4(nqh), r4(nkh), r4(_heads(nv, NA_HEADS)), na_bias_table(w["na_rpb"]), n_ctx)
    na = _unheads(na.reshape(b * NA_HEADS, s, HEAD_DIM), b)
    mix = jnp.concatenate([mla, na], axis=-1)
    return linear(mix.reshape(b * s, -1), w["o_w_out"]).reshape(b, s, d)


_EVEN_KEYS = ("e_w_in", "e_w_out", "e_g_q", "e_g_k", "ssm_lam_re", "ssm_lam_im", "ssm_log_dt", "ssm_b_re", "ssm_b_im",
              "ssm_c_re", "ssm_c_im", "ssm_d", "ssm_w_glu", "ssm_b_glu")
_ODD_KEYS = ("o_w_in", "o_w_out", "mla_g_cq", "mla_g_ckv", "mla_w_uq", "mla_w_ukv", "mla_g_q", "mla_g_k", "na_g_q",
             "na_g_k", "na_rpb")


def _trunk(x_all, mods, w, n_ctx):
    ops = {}
    depth = mods.shape[0]
    b, s, d = x_all.shape
    modulate = _op(ops, _fn_modulate, "modulate", ("row", "glob", "seg", "seg"), (d,), nctx_rows=n_ctx)
    gated = _op(ops, _fn_gated_add, "gated", ("row", "row", "seg"), (d,), nctx_rows=n_ctx)
    x = x_all
    for i in range(depth):
        j = i // 2
        m = [mods[i][:, :, r:r + 1, :] for r in range(N_MOD)]
        a = modulate(x, w["g_norm1"][i][None], m[0], m[1])[0]
        if i % 2 == 0:
            o = _even_mixer(ops, a, {k: w[k][j] for k in _EVEN_KEYS}, n_ctx)
        else:
            o = _odd_mixer(ops, a, {k: w[k][j] for k in _ODD_KEYS}, n_ctx)
        x = gated(x, o, m[2])[0]
        a2 = modulate(x, w["g_norm2"][i][None], m[3], m[4])[0]
        f = ffn(a2.reshape(b * s, d), w["w_ff1"][i], w["w_ff2"][i]).reshape(b, s, d)
        x = gated(x, f, m[5])[0]
    return x[:, n_ctx:]


def local_step(x, ctx, mods, w, loss_target):
    n_ctx = ctx.shape[1]
    x_all = jnp.concatenate([ctx, x], axis=1)
    y, vjp = jax.vjp(lambda xa, md, ww: _trunk(xa, md, ww, n_ctx), x_all, mods, w)
    loss_tile, dy = _loss_head(y, loss_target)
    dx_all, dmods, dw = vjp(dy)
    return loss_tile[0, 0], dx_all[:, n_ctx:], dmods, dw


_SHARDED = (("w_ff1", 2), ("w_ff2", 1), ("e_w_in", 2), ("e_w_out", 1), ("o_w_in", 2), ("o_w_out", 1),
            ("mla_w_uq", 2), ("mla_w_ukv", 2), ("ssm_w_glu", 1))
_SHARDED_SMALL = (("mla_g_cq", 1), ("mla_g_ckv", 1))
_REPLICATED = ("g_norm1", "g_norm2", "e_g_q", "e_g_k", "ssm_lam_re", "ssm_lam_im", "ssm_log_dt", "ssm_b_re", "ssm_b_im",
               "ssm_c_re", "ssm_c_im", "ssm_d", "ssm_b_glu", "mla_g_q", "mla_g_k", "na_g_q", "na_g_k", "na_rpb")
_WEIGHTS = ("c_ctx", "w_mod", "b_mod", "g_norm1", "g_norm2", "w_ff1", "w_ff2", "e_w_in", "e_w_out", "e_g_q", "e_g_k",
            "ssm_lam_re", "ssm_lam_im", "ssm_log_dt", "ssm_b_re", "ssm_b_im", "ssm_c_re", "ssm_c_im", "ssm_d",
            "ssm_w_glu", "ssm_b_glu", "o_w_in", "o_w_out", "mla_g_cq", "mla_g_ckv", "mla_w_uq", "mla_w_ukv", "mla_g_q",
            "mla_g_k", "na_g_q", "na_g_k", "na_rpb")
_PACK_ROWS = 64


def _pack(arrs, dtype, cols=1024, row_mult=_PACK_ROWS):
    flat = jnp.concatenate([a.reshape(-1).astype(dtype) for a in arrs])
    unit = cols * row_mult
    pad = (-flat.shape[0]) % unit
    return jnp.pad(flat, (0, pad)).reshape(-1, cols)


def _unpack(flat, shapes):
    flat = flat.reshape(-1)
    out, off = [], 0
    for sh in shapes:
        n = int(np.prod(sh))
        out.append(flat[off:off + n].reshape(sh))
        off += n
    return out


def _silu(t):
    return t * jax.nn.sigmoid(t)


def kernel(x, c, ctx, c_ctx, w_mod, b_mod, g_norm1, g_norm2, w_ff1, w_ff2, e_w_in, e_w_out, e_g_q, e_g_k, ssm_lam_re, ssm_lam_im, ssm_log_dt, ssm_b_re, ssm_b_im, ssm_c_re, ssm_c_im, ssm_d, ssm_w_glu, ssm_b_glu, o_w_in, o_w_out, mla_g_cq, mla_g_ckv, mla_w_uq, mla_w_ukv, mla_g_q, mla_g_k, na_g_q, na_g_k, na_rpb, loss_target, m_c_ctx, m_w_mod, m_b_mod, m_g_norm1, m_g_norm2, m_w_ff1, m_w_ff2, m_e_w_in, m_e_w_out, m_e_g_q, m_e_g_k, m_ssm_lam_re, m_ssm_lam_im, m_ssm_log_dt, m_ssm_b_re, m_ssm_b_im, m_ssm_c_re, m_ssm_c_im, m_ssm_d, m_ssm_w_glu, m_ssm_b_glu, m_o_w_in, m_o_w_out, m_mla_g_cq, m_mla_g_ckv, m_mla_w_uq, m_mla_w_ukv, m_mla_g_q, m_mla_g_k, m_na_g_q, m_na_g_k, m_na_rpb, v_c_ctx, v_w_mod, v_b_mod, v_g_norm1, v_g_norm2, v_w_ff1, v_w_ff2, v_e_w_in, v_e_w_out, v_e_g_q, v_e_g_k, v_ssm_lam_re, v_ssm_lam_im, v_ssm_log_dt, v_ssm_b_re, v_ssm_b_im, v_ssm_c_re, v_ssm_c_im, v_ssm_d, v_ssm_w_glu, v_ssm_b_glu, v_o_w_in, v_o_w_out, v_mla_g_cq, v_mla_g_ckv, v_mla_w_uq, v_mla_w_ukv, v_mla_g_q, v_mla_g_k, v_na_g_q, v_na_g_k, v_na_rpb):
    env = dict(locals())
    weights = {n: env[n] for n in _WEIGHTS}
    mom_m = {n: env["m_" + n] for n in _WEIGHTS}
    mom_v = {n: env["v_" + n] for n in _WEIGHTS}
    ax, ay, ac = _me()
    plane = 2 * ax + ay
    dev = 4 * ax + 2 * ay + ac
    b_loc, d = c.shape
    depth = w_mod.shape[0]
    n_all = N_DEV * b_loc
    mod_cols = w_mod.shape[2]

    big = _pack([weights[n] for n, _ in _SHARDED], BF16)
    small = _pack([weights[n] for n, _ in _SHARDED_SMALL], F32, cols=LANE, row_mult=8)
    g_big, g_small = plane_allgather(big, small)
    full = {n: weights[n] for n in _REPLICATED}
    parts = [_unpack(g_big[j], [weights[n].shape for n, _ in _SHARDED]) for j in range(N_PLANE)]
    for t, (n, axis) in enumerate(_SHARDED):
        full[n] = jnp.concatenate([parts[j][t] for j in range(N_PLANE)], axis=axis).astype(F32)
    parts_s = [_unpack(g_small[j], [weights[n].shape for n, _ in _SHARDED_SMALL]) for j in range(N_PLANE)]
    for t, (n, axis) in enumerate(_SHARDED_SMALL):
        full[n] = jnp.concatenate([parts_s[j][t] for j in range(N_PLANE)], axis=axis)

    rows_pad = 8 * ((n_all + 1 + 7) // 8)
    c_all = allgather8(jnp.pad(c, ((0, 8 - b_loc), (0, 0)))).reshape(N_DEV, 8, d)[:, :b_loc].reshape(n_all, d)
    cond_raw = jnp.concatenate([c_all, c_ctx[None], jnp.zeros((rows_pad - n_all - 1, d), F32)], axis=0)
    b_cols = lax.dynamic_slice_in_dim(b_mod, plane * mod_cols, mod_cols, axis=1)
    mod_loc = jnp.stack([_mm(cond_raw, w_mod[i], a_act="silu") + b_cols[i][None] for i in range(depth)])
    mod_g = allgather8(mod_loc.reshape(depth * rows_pad, mod_cols)).reshape(N_PLANE, 2, depth, rows_pad, mod_cols)
    mod_all = jnp.concatenate([mod_g[j, 0] for j in range(N_PLANE)], axis=-1)
    m_lat = lax.dynamic_slice_in_dim(mod_all, dev * b_loc, b_loc, axis=1)
    m_ctx = jnp.broadcast_to(mod_all[:, n_all][:, None], m_lat.shape)
    mods = jnp.stack([m_ctx, m_lat], axis=2).reshape(depth, b_loc, 2, N_MOD, d)

    loss_part, grad_x, dmods, dw = local_step(x, ctx, mods, full, loss_target)
    loss = lax.psum(loss_part, ("x", "y", "c"))

    dm = dmods.reshape(depth, b_loc, 2, N_MOD * d)
    dm_rows = jnp.concatenate([dm[:, :, 1], jnp.sum(dm[:, :, 0], axis=1, keepdims=True)], axis=1)
    rep_shapes = [weights[n].shape for n in _REPLICATED]
    small_pack = _pack([dm_rows] + [dw[n] for n in _REPLICATED], F32, cols=1024, row_mult=8)
    sp_rows = small_pack.shape[0]
    gathered = allgather8(small_pack).reshape(N_DEV, sp_rows, 1024)
    n_dm = depth * (b_loc + 1) * N_MOD * d
    dm_all = gathered.reshape(N_DEV, -1)[:, :n_dm].reshape(N_DEV, depth, b_loc + 1, N_MOD * d)
    rep_sum = _sum_rows(gathered, N_DEV).reshape(-1)
    rep_grads = dict(zip(_REPLICATED, _unpack(rep_sum[n_dm:], rep_shapes)))
    d_ctx_row = rep_sum[:n_dm].reshape(depth, b_loc + 1, N_MOD * d)[:, b_loc]
    d_lat_rows = jnp.transpose(dm_all[:, :, :b_loc], (1, 0, 2, 3)).reshape(depth, n_all, N_MOD * d)
    d_mod_all = jnp.concatenate([d_lat_rows, d_ctx_row[:, None],
                                 jnp.zeros((depth, rows_pad - n_all - 1, N_MOD * d), F32)], axis=1)
    grads = dict(rep_grads)
    grads["b_mod"] = jnp.sum(d_mod_all, axis=1)
    d_cols = lax.dynamic_slice_in_dim(d_mod_all, plane * mod_cols, mod_cols, axis=2)
    grads["w_mod"] = jnp.stack([_mm(cond_raw, d_cols[i], ta=True, a_act="silu") for i in range(depth)])
    d_cond = _mm(d_cols[0], w_mod[0], tb=True)
    for i in range(1, depth):
        d_cond = _add2(d_cond, _mm(d_cols[i], w_mod[i], tb=True))
    d_cond_g = allgather8(d_cond[n_all:n_all + 8] if rows_pad - n_all >= 8 else
                          jnp.pad(d_cond[n_all:], ((0, 8 - (rows_pad - n_all)), (0, 0)))).reshape(N_PLANE, 2, 8, d)
    d_silu = _sum_rows(d_cond_g[:, 0], N_PLANE)[0]
    sg = jax.nn.sigmoid(c_ctx)
    grads["c_ctx"] = d_silu * (sg * (1.0 + c_ctx * (1.0 - sg)))

    def shard_of(g, axis, j):
        n = g.shape[axis] // N_PLANE
        return lax.slice_in_dim(g, j * n, (j + 1) * n, axis=axis)

    send = jnp.stack([_pack([shard_of(dw[n], axis, j) for n, axis in _SHARDED]
                            + [shard_of(dw[n], axis, j) for n, axis in _SHARDED_SMALL], BF16) for j in range(N_PLANE)])
    own = lax.dynamic_index_in_dim(send, plane, 0, keepdims=False)
    plane_sum = _accumulate([own, plane_scatter(send)], BF16)
    flat = _accumulate([plane_sum, sibling_swap(plane_sum)], F32).reshape(-1)
    shard_shapes = [weights[n].shape for n, _ in _SHARDED] + [weights[n].shape for n, _ in _SHARDED_SMALL]
    for (n, _), g in zip(_SHARDED + _SHARDED_SMALL, _unpack(flat, shard_shapes)):
        grads[n] = g

    big_names = ("w_mod",) + tuple(n for n, _ in _SHARDED)
    small_names = tuple(n for n in _WEIGHTS if n not in big_names)
    delta, new_m, new_v = {}, {}, {}
    for n in big_names:
        delta[n], new_m[n], new_v[n] = _adamw(weights[n], grads[n], mom_m[n], mom_v[n])
    sm_shapes = [weights[n].shape for n in small_names]
    packed = [_pack([src[n] for n in small_names], F32, cols=1024, row_mult=8)
              for src in (weights, grads, mom_m, mom_v)]
    for dst, res in zip((delta, new_m, new_v), _adamw(*packed)):
        dst.update(dict(zip(small_names, _unpack(res, sm_shapes))))

    return (loss, grad_x, *[grads[n] for n in _WEIGHTS], *[delta[n] for n in _WEIGHTS],
            *[new_m[n] for n in _WEIGHTS], *[new_v[n] for n in _WEIGHTS])
```

```python
import functools
import math

import numpy as np
import jax
import jax.numpy as jnp
from jax import lax
from jax.experimental import pallas as pl
from jax.experimental.pallas import tpu as pltpu

F32 = jnp.float32
BF16 = jnp.bfloat16
HI = lax.Precision.HIGHEST
MESH = pl.DeviceIdType.MESH
ANY = pl.BlockSpec(memory_space=pl.ANY)
VMEM_SPEC = pl.BlockSpec(memory_space=pltpu.VMEM)

GRID_W = 64
HEAD_DIM = 64
ROPE_BASE = 10000.0
EPS = 1e-6
N_MOD = 6
GQA_Q_HEADS, GQA_KV_HEADS = 12, 4
GQA_Q_W, GQA_KV_W = GQA_Q_HEADS * HEAD_DIM, GQA_KV_HEADS * HEAD_DIM
SSM_WIDTH, SSM_GROUP, SSM_STATE = 256, 16, 64
SSM_GROUPS = SSM_WIDTH // SSM_GROUP
SSM_LANES = SSM_GROUPS * SSM_STATE
MLA_HEADS, MLA_Q_RANK, MLA_KV_RANK, MLA_NOPE, MLA_ROPE, MLA_V = 8, 512, 256, 64, 32, 64
MLA_QK = MLA_NOPE + MLA_ROPE
NA_HEADS, NA_WIN_R, NA_WIN_C = 8, 8, 16
NA_W = NA_HEADS * HEAD_DIM
NA_BAND = NA_WIN_R * GRID_W
ODD_IN_W = MLA_Q_RANK + MLA_KV_RANK + MLA_ROPE + 3 * NA_W
ODD_IN_PAD = 2560
ADAM_LR, ADAM_B1, ADAM_B2, ADAM_EPS, ADAM_WD, ADAM_STEP = 0.001, 0.9, 0.999, 1e-08, 0.01, 10
NEG = -1e30
VMEM_LIMIT = 56 * 1024 * 1024
LANE = 128
MM_TILE_M = (1152, 1024, 768, 512, 256, 128)
MM_TILE_N = (1280, 1024, 768, 512, 256, 128)
MM_TILE_K = (1152, 1024, 768, 512, 256, 128)
N_PLANE = 4
N_DEV = 8


def _pick(n, cands):
    for c in cands:
        if n % c == 0:
            return c
    return n


def _params(**kw):
    return pltpu.CompilerParams(vmem_limit_bytes=VMEM_LIMIT, **kw)


def _mm(a, b, *, ta=False, tb=False, a_act=None, epi=None, e=None, exact=False):
    m, kd = (a.shape[1], a.shape[0]) if ta else a.shape
    n = b.shape[0] if tb else b.shape[1]
    tm = _pick(m, MM_TILE_M)
    tn = _pick(n, MM_TILE_N)
    tk = _pick(kd, MM_TILE_K)
    nk = kd // tk
    dn = (((0 if ta else 1,), (1 if tb else 0,)), ((), ()))

    def body(*refs):
        if epi is None:
            a_ref, b_ref, o_ref = refs
        else:
            a_ref, b_ref, e_ref, o_ref = refs
        k = pl.program_id(2)
        av = a_ref[...]
        if a_act == "relu2":
            av = jnp.square(jnp.maximum(av, 0.0))
        elif a_act == "silu":
            av = av * jax.nn.sigmoid(av)
        bv = b_ref[...]
        if exact:
            p = lax.dot_general(av, bv, dn, precision=HI, preferred_element_type=F32)
        else:
            p = lax.dot_general(av.astype(BF16), bv.astype(BF16), dn, preferred_element_type=F32)

        @pl.when(k == 0)
        def _():
            o_ref[...] = p

        @pl.when(k > 0)
        def _():
            o_ref[...] += p

        if epi == "drelu2":
            @pl.when(k == nk - 1)
            def _():
                o_ref[...] = o_ref[...] * (2.0 * jnp.maximum(e_ref[...], 0.0))

    a_spec = pl.BlockSpec((tk, tm), lambda i, j, k: (k, i)) if ta else pl.BlockSpec((tm, tk), lambda i, j, k: (i, k))
    b_spec = pl.BlockSpec((tn, tk), lambda i, j, k: (j, k)) if tb else pl.BlockSpec((tk, tn), lambda i, j, k: (k, j))
    o_spec = pl.BlockSpec((tm, tn), lambda i, j, k: (i, j))
    ins, specs = [a, b], [a_spec, b_spec]
    if epi is not None:
        ins.append(e)
        specs.append(o_spec)
    name = f"mm_{m}x{kd}x{n}_{int(ta)}{int(tb)}_{a_act}_{epi}_{int(exact)}"
    return pl.pallas_call(
        body, out_shape=jax.ShapeDtypeStruct((m, n), F32), grid=(m // tm, n // tn, nk),
        in_specs=specs, out_specs=o_spec, name=name, compiler_params=_params(),
    )(*ins)


@functools.partial(jax.custom_vjp, nondiff_argnums=(2,))
def _linear(a, w, exact):
    return _mm(a, w, exact=exact)


def _linear_fwd(a, w, exact):
    return _mm(a, w, exact=exact), (a, w)


def _linear_bwd(exact, res, g):
    a, w = res
    return _mm(g, w, tb=True, exact=exact), _mm(a, g, ta=True, exact=exact)


_linear.defvjp(_linear_fwd, _linear_bwd)


def linear(a, w, exact=False):
    return _linear(a, w, exact)


@jax.custom_vjp
def ffn(a, w1, w2):
    return _mm(_mm(a, w1), w2, a_act="relu2")


def _ffn_fwd(a, w1, w2):
    h1 = _mm(a, w1)
    return _mm(h1, w2, a_act="relu2"), (a, w1, w2, h1)


def _ffn_bwd(res, g):
    a, w1, w2, h1 = res
    dh1 = _mm(g, w2, tb=True, epi="drelu2", e=h1)
    dw2 = _mm(h1, g, ta=True, a_act="relu2")
    return _mm(dh1, w1, tb=True), _mm(a, dh1, ta=True), dw2


ffn.defvjp(_ffn_fwd, _ffn_bwd)


def make_rowwise(fn, name, kinds, out_dims, nctx_rows=0, whole_seq=False):
    n_in = len(kinds)
    n_out = len(out_dims)
    diff = [i for i, kd in enumerate(kinds) if kd in ("row", "glob", "seg")]

    def layout(args):
        row0 = args[kinds.index("row")]
        g, s = row0.shape[0], row0.shape[1]
        ts = s if whole_seq else (min(256, nctx_rows) if nctx_rows else _pick(s, (256, 128, 64)))
        nctx = nctx_rows // ts
        return g, s, ts, nctx

    def spec_of(kind, arr, ts, nctx):
        if kind == "row":
            return pl.BlockSpec((None, ts, arr.shape[2]), lambda g, i: (g, i, 0))
        if kind == "tab":
            return pl.BlockSpec((ts, arr.shape[1]), lambda g, i: (i, 0))
        if kind in ("const", "glob"):
            return pl.BlockSpec(arr.shape, lambda g, i: (0, 0))
        return pl.BlockSpec((None, None) + arr.shape[2:], lambda g, i: (g, (i >= nctx).astype(jnp.int32), 0, 0))

    def fwd_call(*args):
        g, s, ts, nctx = layout(args)

        def body(*refs):
            vals = [r[...] for r in refs[:n_in]]
            outs = fn(*vals)
            for o_ref, o in zip(refs[n_in:], outs):
                o_ref[...] = o

        return pl.pallas_call(
            body, out_shape=[jax.ShapeDtypeStruct((g, s, d), F32) for d in out_dims], grid=(g, s // ts),
            in_specs=[spec_of(kd, a, ts, nctx) for kd, a in zip(kinds, args)],
            out_specs=[pl.BlockSpec((None, ts, d), lambda g_, i: (g_, i, 0)) for d in out_dims],
            name=f"{name}_f_{g}x{s}", compiler_params=_params(),
        )(*args)

    def bwd_call(args, cts):
        g, s, ts, nctx = layout(args)

        def body(*refs):
            in_refs, ct_refs, out_refs = refs[:n_in], refs[n_in:n_in + n_out], refs[n_in + n_out:]
            gi, i = pl.program_id(0), pl.program_id(1)
            vals = [r[...] for r in in_refs]

            def f(*dv):
                full = list(vals)
                for idx, v in zip(diff, dv):
                    full[idx] = v
                return tuple(fn(*full))

            _, vjp = jax.vjp(f, *[vals[idx] for idx in diff])
            grads = vjp(tuple(r[...] for r in ct_refs))
            for idx, o_ref, gr in zip(diff, out_refs, grads):
                if kinds[idx] == "row":
                    o_ref[...] = gr
                    continue
                if kinds[idx] == "glob":
                    first = jnp.logical_and(gi == 0, i == 0)
                else:
                    first = jnp.logical_or(i == 0, i == nctx)

                @pl.when(first)
                def _(o_ref=o_ref, gr=gr):
                    o_ref[...] = gr

                @pl.when(jnp.logical_not(first))
                def _(o_ref=o_ref, gr=gr):
                    o_ref[...] += gr

        in_specs = [spec_of(kd, a, ts, nctx) for kd, a in zip(kinds, args)]
        in_specs += [pl.BlockSpec((None, ts, d), lambda g_, i: (g_, i, 0)) for d in out_dims]
        return pl.pallas_call(
            body, out_shape=[jax.ShapeDtypeStruct(args[idx].shape, F32) for idx in diff], grid=(g, s // ts),
            in_specs=in_specs, out_specs=[spec_of(kinds[idx], args[idx], ts, nctx) for idx in diff],
            name=f"{name}_b_{g}x{s}", compiler_params=_params(),
        )(*args, *cts)

    @jax.custom_vjp
    def op(*args):
        return tuple(fwd_call(*args))

    def op_fwd(*args):
        return tuple(fwd_call(*args)), args

    def op_bwd(args, cts):
        grads = bwd_call(args, cts)
        full = [None] * n_in
        for idx, gr in zip(diff, grads):
            full[idx] = gr
        return tuple(jnp.zeros_like(a) if gfull is None else gfull for a, gfull in zip(args, full))

    op.defvjp(op_fwd, op_bwd)
    return op


def _rms(x):
    return lax.rsqrt(jnp.mean(x * x, axis=-1, keepdims=True) + EPS)


def _fn_modulate(x, g, shift, scale):
    return ((x * _rms(x) * g) * (1.0 + scale) + shift,)


def _fn_gated_add(x, o, gate):
    return (x + gate * o,)


def _fn_norm(x, g):
    return (x * _rms(x) * g,)


def _fn_norm_rope(x, cos, sin, rot, g):
    y = x * _rms(x) * g
    r = jnp.dot(y, rot, precision=HI, preferred_element_type=F32)
    return (y * cos + r * sin,)


def _fn_glu_pre(u, y0, y1, d):
    return (jax.nn.gelu(d * u + y0 + y1),)


def _fn_glu_post(z, t, bg):
    return (z * jax.nn.sigmoid(t + bg),)


def _rope_matrix(dh, start, rot_dim):
    r = np.zeros((dh, dh), np.float32)
    q = rot_dim // 4
    for j in range(rot_dim):
        if (j // q) % 2 == 0:
            r[start + j + q, start + j] = -1.0
        else:
            r[start + j - q, start + j] = 1.0
    return r


def _rope_tables(n_ctx, n_lat, dh, start, rot_dim):
    t = jnp.arange(n_lat)
    rows = (t // GRID_W).astype(F32)
    cols = (t % GRID_W).astype(F32)
    axis_dim = rot_dim // 2
    freqs = ROPE_BASE ** (-jnp.arange(0, axis_dim, 2, dtype=F32) / axis_dim)
    ang_r = rows[:, None] * freqs
    ang_c = cols[:, None] * freqs
    ang = jnp.concatenate([ang_r, ang_r, ang_c, ang_c], axis=-1)
    cos = jnp.concatenate([jnp.ones((n_lat, start), F32), jnp.cos(ang)], axis=-1)
    sin = jnp.concatenate([jnp.zeros((n_lat, start), F32), jnp.sin(ang)], axis=-1)
    cos = jnp.concatenate([jnp.ones((n_ctx, dh), F32), cos], axis=0)
    sin = jnp.concatenate([jnp.zeros((n_ctx, dh), F32), sin], axis=0)
    return cos, sin


_NT = (((1,), (1,)), ((), ()))
_TN = (((0,), (0,)), ((), ()))


def _attn_fwd(q, k, v, group, n_ctx, scale):
    b, h, s, dq = q.shape
    dv = v.shape[-1]
    tq = min(256, n_ctx)
    nc = n_ctx // tq

    def body(q_ref, k_ref, v_ref, o_ref, lse_ref):
        qv = (q_ref[...] * scale).astype(BF16)

        def run(n_keys):
            sc = lax.dot_general(qv, k_ref[0:n_keys, :].astype(BF16), _NT, preferred_element_type=F32)
            m = jnp.max(sc, axis=-1, keepdims=True)
            p = jnp.exp(sc - m)
            l = jnp.sum(p, axis=-1, keepdims=True)
            o = jnp.dot(p.astype(BF16), v_ref[0:n_keys, :].astype(BF16), preferred_element_type=F32)
            o_ref[...] = o / l
            lse_ref[...] = m + jnp.log(l)

        pl.when(pl.program_id(2) < nc)(lambda: run(n_ctx))
        pl.when(pl.program_id(2) >= nc)(lambda: run(s))

    return pl.pallas_call(
        body, out_shape=[jax.ShapeDtypeStruct((b, h, s, dv), F32), jax.ShapeDtypeStruct((b, h, s, 1), F32)],
        grid=(b, h, s // tq),
        in_specs=[pl.BlockSpec((None, None, tq, dq), lambda bi, hi, i: (bi, hi, i, 0)),
                  pl.BlockSpec((None, None, s, dq), lambda bi, hi, i: (bi, lax.div(hi, group), 0, 0)),
                  pl.BlockSpec((None, None, s, dv), lambda bi, hi, i: (bi, lax.div(hi, group), 0, 0))],
        out_specs=[pl.BlockSpec((None, None, tq, dv), lambda bi, hi, i: (bi, hi, i, 0)),
                   pl.BlockSpec((None, None, tq, 1), lambda bi, hi, i: (bi, hi, i, 0))],
        name=f"attn_f_{h}x{s}x{dq}", compiler_params=_params(),
    )(q, k, v)


def _attn_dq(q, k, v, o, lse, do, group, n_ctx, scale):
    b, h, s, dq = q.shape
    dv = v.shape[-1]
    tq = min(256, n_ctx)
    nc = n_ctx // tq

    def body(q_ref, k_ref, v_ref, o_ref, lse_ref, do_ref, dq_ref, delta_ref):
        qv = (q_ref[...] * scale).astype(BF16)
        dov = do_ref[...]
        delta = jnp.sum(dov * o_ref[...], axis=-1, keepdims=True)
        delta_ref[...] = delta

        def run(n_keys):
            kv = k_ref[0:n_keys, :].astype(BF16)
            sc = lax.dot_general(qv, kv, _NT, preferred_element_type=F32)
            p = jnp.exp(sc - lse_ref[...])
            dp = lax.dot_general(dov.astype(BF16), v_ref[0:n_keys, :].astype(BF16), _NT, preferred_element_type=F32)
            ds = p * (dp - delta)
            dq_ref[...] = jnp.dot(ds.astype(BF16), kv, preferred_element_type=F32) * scale

        pl.when(pl.program_id(2) < nc)(lambda: run(n_ctx))
        pl.when(pl.program_id(2) >= nc)(lambda: run(s))

    qs = lambda d: pl.BlockSpec((None, None, tq, d), lambda bi, hi, i: (bi, hi, i, 0))
    ks = lambda d: pl.BlockSpec((None, None, s, d), lambda bi, hi, i: (bi, lax.div(hi, group), 0, 0))
    return pl.pallas_call(
        body, out_shape=[jax.ShapeDtypeStruct((b, h, s, dq), F32), jax.ShapeDtypeStruct((b, h, s, 1), F32)],
        grid=(b, h, s // tq),
        in_specs=[qs(dq), ks(dq), ks(dv), qs(dv), qs(1), qs(dv)], out_specs=[qs(dq), qs(1)],
        name=f"attn_dq_{h}x{s}x{dq}", compiler_params=_params(),
    )(q, k, v, o, lse, do)


def _attn_dkv(q, k, v, lse, delta, do, group, n_ctx, scale):
    b, h, s, dq = q.shape
    hk = k.shape[1]
    dv = v.shape[-1]
    tk = min(256, n_ctx)
    nc = n_ctx // tk

    def body(q_ref, k_ref, v_ref, lse_ref, delta_ref, do_ref, dk_ref, dv_ref):
        kv = k_ref[...].astype(BF16)
        vv = v_ref[...].astype(BF16)

        def run(r0):
            dk = jnp.zeros((tk, dq), F32)
            dvv = jnp.zeros((tk, dv), F32)
            for g in range(group):
                qg = (q_ref[g, r0:s, :] * scale).astype(BF16)
                dog = do_ref[g, r0:s, :].astype(BF16)
                sc = lax.dot_general(qg, kv, _NT, preferred_element_type=F32)
                p = jnp.exp(sc - lse_ref[g, r0:s, :])
                dvv = dvv + lax.dot_general(p.astype(BF16), dog, _TN, preferred_element_type=F32)
                dp = lax.dot_general(dog, vv, _NT, preferred_element_type=F32)
                ds = p * (dp - delta_ref[g, r0:s, :])
                dk = dk + lax.dot_general(ds.astype(BF16), qg, _TN, preferred_element_type=F32)
            dk_ref[...] = dk
            dv_ref[...] = dvv

        pl.when(pl.program_id(2) < nc)(lambda: run(0))
        pl.when(pl.program_id(2) >= nc)(lambda: run(n_ctx))

    gs = lambda d: pl.BlockSpec((None, group, s, d), lambda bi, hi, j: (bi, hi, 0, 0))
    ks = lambda d: pl.BlockSpec((None, None, tk, d), lambda bi, hi, j: (bi, hi, j, 0))
    return pl.pallas_call(
        body, out_shape=[jax.ShapeDtypeStruct((b, hk, s, dq), F32), jax.ShapeDtypeStruct((b, hk, s, dv), F32)],
        grid=(b, hk, s // tk),
        in_specs=[gs(dq), ks(dq), ks(dv), gs(1), gs(1), gs(dv)], out_specs=[ks(dq), ks(dv)],
        name=f"attn_dkv_{h}x{s}x{dq}", compiler_params=_params(),
    )(q, k, v, lse, delta, do)


@functools.partial(jax.custom_vjp, nondiff_argnums=(3, 4, 5))
def attention(q, k, v, group, n_ctx, scale):
    return _attn_fwd(q, k, v, group, n_ctx, scale)[0]


def _attention_fwd(q, k, v, group, n_ctx, scale):
    o, lse = _attn_fwd(q, k, v, group, n_ctx, scale)
    return o, (q, k, v, o, lse)


def _attention_bwd(group, n_ctx, scale, res, do):
    q, k, v, o, lse = res
    dq, delta = _attn_dq(q, k, v, o, lse, do, group, n_ctx, scale)
    dk, dv = _attn_dkv(q, k, v, lse, delta, do, group, n_ctx, scale)
    return dq, dk, dv


attention.defvjp(_attention_fwd, _attention_bwd)


def _na_geometry(i, nc, rows):
    r = i - nc
    rs = jnp.clip(r - NA_WIN_R // 2, 0, rows - NA_WIN_R)
    is_ctx = i < nc
    cls = jnp.where(is_ctx, NA_WIN_R, r - rs)
    return jnp.where(is_ctx, 0, rs), cls


def _na_scores(q_ref, k_ref, bias_ref, hd, n_ctx, start, scale):
    qv = (q_ref[hd] * scale).astype(BF16)
    kc = k_ref[hd, 0:n_ctx, :].astype(BF16)
    kb = k_ref[hd, pl.ds(start, NA_BAND), :].astype(BF16)
    s_c = lax.dot_general(qv, kc, _NT, preferred_element_type=F32)
    s_l = lax.dot_general(qv, kb, _NT, preferred_element_type=F32) + bias_ref[hd]
    return qv, kc, kb, s_c, s_l


NA_HEADS_FWD = 4
NA_HEADS_BWD = 2


def _na_specs(hp, s, dh, nc, rows):
    qs = lambda d: pl.BlockSpec((None, hp, GRID_W, d), lambda bi, hg, i: (bi, hg, i, 0))
    ks = pl.BlockSpec((None, hp, s, dh), lambda bi, hg, i: (bi, hg, 0, 0))
    bs = pl.BlockSpec((hp, None, GRID_W, NA_BAND), lambda bi, hg, i: (hg, _na_geometry(i, nc, rows)[1], 0, 0))
    return qs, ks, bs


def _na_fwd(q, k, v, bias, n_ctx):
    b, h, s, dh = q.shape
    nc = n_ctx // GRID_W
    rows = (s - n_ctx) // GRID_W
    scale = dh ** -0.5
    hp = math.gcd(h, NA_HEADS_FWD)

    def body(q_ref, k_ref, v_ref, bias_ref, o_ref, lse_ref):
        rs, _ = _na_geometry(pl.program_id(2), nc, rows)
        start = pl.multiple_of(n_ctx + rs * GRID_W, GRID_W)
        for hd in range(hp):
            _, _, _, s_c, s_l = _na_scores(q_ref, k_ref, bias_ref, hd, n_ctx, start, scale)
            m = jnp.maximum(jnp.max(s_c, axis=-1, keepdims=True), jnp.max(s_l, axis=-1, keepdims=True))
            p_c = jnp.exp(s_c - m)
            p_l = jnp.exp(s_l - m)
            l = jnp.sum(p_c, axis=-1, keepdims=True) + jnp.sum(p_l, axis=-1, keepdims=True)
            o = jnp.dot(p_c.astype(BF16), v_ref[hd, 0:n_ctx, :].astype(BF16), preferred_element_type=F32)
            o = o + jnp.dot(p_l.astype(BF16), v_ref[hd, pl.ds(start, NA_BAND), :].astype(BF16),
                            preferred_element_type=F32)
            o_ref[hd] = o / l
            lse_ref[hd] = m + jnp.log(l)

    qs, ks, bs = _na_specs(hp, s, dh, nc, rows)
    return pl.pallas_call(
        body, out_shape=[jax.ShapeDtypeStruct((b, h, s, dh), F32), jax.ShapeDtypeStruct((b, h, s, 1), F32)],
        grid=(b, h // hp, s // GRID_W), in_specs=[qs(dh), ks, ks, bs], out_specs=[qs(dh), qs(1)],
        name=f"na_f_{s}", compiler_params=_params(),
    )(q, k, v, bias)


def _na_bwd(q, k, v, bias, o, lse, do, n_ctx):
    b, h, s, dh = q.shape
    nc = n_ctx // GRID_W
    rows = (s - n_ctx) // GRID_W
    scale = dh ** -0.5
    n_cls = NA_WIN_R + 1
    hp = math.gcd(h, NA_HEADS_BWD)

    def body(q_ref, k_ref, v_ref, bias_ref, o_ref, lse_ref, do_ref, dq_ref, dk_ref, dv_ref, db_ref):
        i = pl.program_id(2)
        rs, cls = _na_geometry(i, nc, rows)
        _, cls_prev = _na_geometry(i - 1, nc, rows)
        start = pl.multiple_of(n_ctx + rs * GRID_W, GRID_W)
        first = jnp.logical_or(i == 0, cls != cls_prev)

        @pl.when(i == 0)
        def _():
            dk_ref[...] = jnp.zeros_like(dk_ref)
            dv_ref[...] = jnp.zeros_like(dv_ref)

        for hd in range(hp):
            qv, kc, kb, s_c, s_l = _na_scores(q_ref, k_ref, bias_ref, hd, n_ctx, start, scale)
            lse_v = lse_ref[hd]
            p_c = jnp.exp(s_c - lse_v)
            p_l = jnp.exp(s_l - lse_v)
            dov = do_ref[hd]
            dob = dov.astype(BF16)
            delta = jnp.sum(dov * o_ref[hd], axis=-1, keepdims=True)
            vc = v_ref[hd, 0:n_ctx, :].astype(BF16)
            vb = v_ref[hd, pl.ds(start, NA_BAND), :].astype(BF16)
            ds_c = p_c * (lax.dot_general(dob, vc, _NT, preferred_element_type=F32) - delta)
            ds_l = p_l * (lax.dot_general(dob, vb, _NT, preferred_element_type=F32) - delta)
            dsc_b = ds_c.astype(BF16)
            dsl_b = ds_l.astype(BF16)
            dq_ref[hd] = (jnp.dot(dsc_b, kc, preferred_element_type=F32)
                          + jnp.dot(dsl_b, kb, preferred_element_type=F32)) * scale
            dk_ref[hd, 0:n_ctx, :] += lax.dot_general(dsc_b, qv, _TN, preferred_element_type=F32)
            dk_ref[hd, pl.ds(start, NA_BAND), :] += lax.dot_general(dsl_b, qv, _TN, preferred_element_type=F32)
            dv_ref[hd, 0:n_ctx, :] += lax.dot_general(p_c.astype(BF16), dob, _TN, preferred_element_type=F32)
            dv_ref[hd, pl.ds(start, NA_BAND), :] += lax.dot_general(p_l.astype(BF16), dob, _TN, preferred_element_type=F32)

            @pl.when(first)
            def _(hd=hd, ds_l=ds_l):
                db_ref[hd] = ds_l

            @pl.when(jnp.logical_not(first))
            def _(hd=hd, ds_l=ds_l):
                db_ref[hd] += ds_l

    qs, ks, bs = _na_specs(hp, s, dh, nc, rows)
    dbs = pl.BlockSpec((None, hp, None, GRID_W, NA_BAND),
                       lambda bi, hg, i: (bi, hg, _na_geometry(i, nc, rows)[1], 0, 0))
    return pl.pallas_call(
        body,
        out_shape=[jax.ShapeDtypeStruct((b, h, s, dh), F32), jax.ShapeDtypeStruct((b, h, s, dh), F32),
                   jax.ShapeDtypeStruct((b, h, s, dh), F32), jax.ShapeDtypeStruct((b, h, n_cls, GRID_W, NA_BAND), F32)],
        grid=(b, h // hp, s // GRID_W), in_specs=[qs(dh), ks, ks, bs, qs(dh), qs(1), qs(dh)],
        out_specs=[qs(dh), ks, ks, dbs], name=f"na_b_{s}", compiler_params=_params(),
    )(q, k, v, bias, o, lse, do)


@functools.partial(jax.custom_vjp, nondiff_argnums=(4,))
def na_attention(q, k, v, bias, n_ctx):
    return _na_fwd(q, k, v, bias, n_ctx)[0]


def _na_attention_fwd(q, k, v, bias, n_ctx):
    o, lse = _na_fwd(q, k, v, bias, n_ctx)
    return o, (q, k, v, bias, o, lse)


def _na_attention_bwd(n_ctx, res, do):
    q, k, v, bias, o, lse = res
    dq, dk, dv, db = _na_bwd(q, k, v, bias, o, lse, do, n_ctx)
    return dq, dk, dv, jnp.sum(db, axis=0)


na_attention.defvjp(_na_attention_fwd, _na_attention_bwd)


def _na_onehots():
    q = np.arange(GRID_W)[:, None]
    col = np.arange(GRID_W)[None, :]
    cs = np.clip(q - NA_WIN_C // 2, 0, GRID_W - NA_WIN_C)
    valid = (col >= cs) & (col < cs + NA_WIN_C)
    cidx = col - q + (NA_WIN_C - 1)
    n_b = 2 * NA_WIN_C - 1
    col_hot = np.zeros((LANE, GRID_W * GRID_W), np.float32)
    for qq in range(GRID_W):
        for cc in range(GRID_W):
            if valid[qq, cc]:
                col_hot[cidx[qq, cc], qq * GRID_W + cc] = 1.0
    row_hot = np.zeros((NA_WIN_R, NA_WIN_R, 2 * NA_WIN_R - 1), np.float32)
    for c in range(NA_WIN_R):
        for j in range(NA_WIN_R):
            row_hot[c, j, j - c + NA_WIN_R - 1] = 1.0
    mask = np.where(valid, 0.0, NEG).astype(np.float32)
    return col_hot, row_hot, mask, n_b


def na_bias_table(rpb):
    h = rpb.shape[0]
    col_hot, row_hot, mask, n_b = _na_onehots()
    t1 = jnp.einsum("cja,hab->hcjb", jnp.asarray(row_hot), rpb)
    t1 = jnp.pad(t1.reshape(h * NA_WIN_R * NA_WIN_R, n_b), ((0, 0), (0, LANE - n_b)))
    t2 = linear(t1, jnp.asarray(col_hot), True)
    t2 = t2.reshape(h, NA_WIN_R, NA_WIN_R, GRID_W, GRID_W) + jnp.asarray(mask)
    tab = jnp.transpose(t2, (0, 1, 3, 2, 4)).reshape(h, NA_WIN_R, GRID_W, NA_BAND)
    return jnp.concatenate([tab, jnp.full((h, 1, GRID_W, NA_BAND), NEG, F32)], axis=1)


def _first_step():
    return jnp.logical_and(pl.program_id(0) == 0, pl.program_id(1) == 0)


def _accum_out(ref, val, first):
    @pl.when(first)
    def _():
        ref[...] = val

    @pl.when(jnp.logical_not(first))
    def _():
        ref[...] += val


def _norm_rope_head(xh, g, cos, sin, rot):
    r = _rms(xh)
    yn = xh * r
    y = yn * g
    if cos is not None:
        y = y * cos + jnp.dot(y, rot, precision=HI, preferred_element_type=F32) * sin
    return y, yn, r


def _norm_rope_head_bwd(dy, yn, r, g, cos, sin, rot):
    if cos is not None:
        dy = dy * cos + lax.dot_general(dy * sin, rot, _NT, precision=HI, preferred_element_type=F32)
    dg = jnp.sum(dy * yn, axis=0, keepdims=True)
    dyn = dy * g
    return r * (dyn - yn * jnp.mean(dyn * yn, axis=-1, keepdims=True)), dg


def _hnr_call(x, g, cos, sin, rot, n_heads, dy=None):
    b, s, w = x.shape
    dh = w // n_heads
    ts = _pick(s, (256, 128, 64))
    rope = cos is not None

    def body(*refs):
        refs = list(refs)
        x_ref, g_ref = refs[0], refs[1]
        cos_v = sin_v = rot_v = None
        k = 2
        if rope:
            cos_v, sin_v, rot_v = refs[2][...], refs[3][...], refs[4][...]
            k = 5
        gv = g_ref[...]
        if dy is None:
            o_ref = refs[k]
            for h in range(n_heads):
                sl = slice(h * dh, (h + 1) * dh)
                o_ref[:, sl] = _norm_rope_head(x_ref[:, sl], gv, cos_v, sin_v, rot_v)[0]
            return
        dy_ref, dx_ref, dg_ref = refs[k], refs[k + 1], refs[k + 2]
        dg = jnp.zeros((1, dh), F32)
        for h in range(n_heads):
            sl = slice(h * dh, (h + 1) * dh)
            _, yn, r = _norm_rope_head(x_ref[:, sl], gv, None, None, None)
            dxh, dgh = _norm_rope_head_bwd(dy_ref[:, sl], yn, r, gv, cos_v, sin_v, rot_v)
            dx_ref[:, sl] = dxh
            dg = dg + dgh
        _accum_out(dg_ref, dg, _first_step())

    row = pl.BlockSpec((None, ts, w), lambda bi, i: (bi, i, 0))
    whole = lambda a: pl.BlockSpec(a.shape, lambda bi, i: (0, 0))
    ins, specs = [x, g], [row, whole(g)]
    if rope:
        ins += [cos, sin, rot]
        specs += [pl.BlockSpec((ts, dh), lambda bi, i: (i, 0)), pl.BlockSpec((ts, dh), lambda bi, i: (i, 0)), whole(rot)]
    if dy is None:
        out_shape, out_specs = jax.ShapeDtypeStruct(x.shape, F32), row
    else:
        ins.append(dy)
        specs.append(row)
        out_shape = [jax.ShapeDtypeStruct(x.shape, F32), jax.ShapeDtypeStruct(g.shape, F32)]
        out_specs = [row, whole(g)]
    return pl.pallas_call(
        body, out_shape=out_shape, grid=(b, s // ts), in_specs=specs, out_specs=out_specs,
        name=f"hnr_{'b' if dy is not None else 'f'}_{n_heads}x{dh}_{int(rope)}", compiler_params=_params(),
    )(*ins)


@functools.partial(jax.custom_vjp, nondiff_argnums=(5,))
def head_norm_rope(x, g, cos, sin, rot, n_heads):
    return _hnr_call(x, g, cos, sin, rot, n_heads)


def _head_norm_rope_fwd(x, g, cos, sin, rot, n_heads):
    return _hnr_call(x, g, cos, sin, rot, n_heads), (x, g, cos, sin, rot)


def _head_norm_rope_bwd(n_heads, res, dy):
    x, g, cos, sin, rot = res
    dx, dg = _hnr_call(x, g, cos, sin, rot, n_heads, dy=dy)
    zero = lambda t: None if t is None else jnp.zeros_like(t)
    return dx, dg, zero(cos), zero(sin), zero(rot)


head_norm_rope.defvjp(_head_norm_rope_fwd, _head_norm_rope_bwd)


def _mla_k_call(kv, kr, g, cos, sin, rot, dkn=None):
    b, s, _ = kv.shape
    ts = _pick(s, (256, 128, 64))
    hw = MLA_NOPE + MLA_V

    def body(kv_ref, kr_ref, g_ref, cos_ref, sin_ref, rot_ref, *rest):
        gv, cos_v, sin_v, rot_v = g_ref[...], cos_ref[...], sin_ref[...], rot_ref[...]
        krv = kr_ref[...]
        if dkn is None:
            (o_ref,) = rest
            for h in range(MLA_HEADS):
                kh = jnp.concatenate([kv_ref[:, h * hw:h * hw + MLA_NOPE], krv], axis=-1)
                o_ref[:, h * MLA_QK:(h + 1) * MLA_QK] = _norm_rope_head(kh, gv, cos_v, sin_v, rot_v)[0]
            return
        dkn_ref, dkv_ref, dkr_ref, dg_ref = rest
        dg = jnp.zeros((1, MLA_QK), F32)
        dkr = jnp.zeros((ts, MLA_ROPE), F32)
        for h in range(MLA_HEADS):
            kh = jnp.concatenate([kv_ref[:, h * hw:h * hw + MLA_NOPE], krv], axis=-1)
            _, yn, r = _norm_rope_head(kh, gv, None, None, None)
            dxh, dgh = _norm_rope_head_bwd(dkn_ref[:, h * MLA_QK:(h + 1) * MLA_QK], yn, r, gv, cos_v, sin_v, rot_v)
            dkv_ref[:, h * hw:h * hw + MLA_NOPE] = dxh[:, :MLA_NOPE]
            dkv_ref[:, h * hw + MLA_NOPE:(h + 1) * hw] = jnp.zeros((ts, MLA_V), F32)
            dkr = dkr + dxh[:, MLA_NOPE:]
            dg = dg + dgh
        dkr_ref[...] = dkr
        _accum_out(dg_ref, dg, _first_step())

    row = lambda w: pl.BlockSpec((None, ts, w), lambda bi, i: (bi, i, 0))
    tab = pl.BlockSpec((ts, MLA_QK), lambda bi, i: (i, 0))
    whole = lambda a: pl.BlockSpec(a.shape, lambda bi, i: (0, 0))
    ins = [kv, kr, g, cos, sin, rot]
    specs = [row(kv.shape[2]), row(MLA_ROPE), whole(g), tab, tab, whole(rot)]
    kn_w = MLA_HEADS * MLA_QK
    if dkn is None:
        out_shape, out_specs = jax.ShapeDtypeStruct((b, s, kn_w), F32), row(kn_w)
    else:
        ins.append(dkn)
        specs.append(row(kn_w))
        out_shape = [jax.ShapeDtypeStruct(kv.shape, F32), jax.ShapeDtypeStruct(kr.shape, F32),
                     jax.ShapeDtypeStruct(g.shape, F32)]
        out_specs = [row(kv.shape[2]), row(MLA_ROPE), whole(g)]
    return pl.pallas_call(
        body, out_shape=out_shape, grid=(b, s // ts), in_specs=specs, out_specs=out_specs,
        name=f"mla_k_{'b' if dkn is not None else 'f'}", compiler_params=_params(),
    )(*ins)


@jax.custom_vjp
def mla_k_prep(kv, kr, g, cos, sin, rot):
    return _mla_k_call(kv, kr, g, cos, sin, rot)


def _mla_k_prep_fwd(kv, kr, g, cos, sin, rot):
    return _mla_k_call(kv, kr, g, cos, sin, rot), (kv, kr, g, cos, sin, rot)


def _mla_k_prep_bwd(res, dkn):
    kv, kr, g, cos, sin, rot = res
    dkv, dkr, dg = _mla_k_call(kv, kr, g, cos, sin, rot, dkn=dkn)
    return dkv, dkr, dg, jnp.zeros_like(cos), jnp.zeros_like(sin), jnp.zeros_like(rot)


mla_k_prep.defvjp(_mla_k_prep_fwd, _mla_k_prep_bwd)


class _HeadLayout:
    def __init__(self, groups, dq, dv, q_off, k_off, v_off, o_off, wq, wk, wv, wo, scale):
        self.groups, self.dq, self.dv, self.scale = groups, dq, dv, scale
        self.q_off, self.k_off, self.v_off, self.o_off = q_off, k_off, v_off, o_off
        self.wq, self.wk, self.wv, self.wo = wq, wk, wv, wo
        self.n_h = len(q_off)


def _gqa_layout():
    rep = GQA_Q_HEADS // GQA_KV_HEADS
    n_h = GQA_Q_HEADS // 2
    return _HeadLayout(2, HEAD_DIM, HEAD_DIM, [h * HEAD_DIM for h in range(n_h)], [(h // rep) * HEAD_DIM for h in range(n_h)],
                       [(h // rep) * HEAD_DIM for h in range(n_h)], [h * HEAD_DIM for h in range(n_h)],
                       n_h * HEAD_DIM, (n_h // rep) * HEAD_DIM, (n_h // rep) * HEAD_DIM, n_h * HEAD_DIM, HEAD_DIM ** -0.5)


def _mla_layout():
    n_h = MLA_HEADS // 2
    hw = MLA_NOPE + MLA_V
    return _HeadLayout(2, MLA_QK, MLA_V, [h * MLA_QK for h in range(n_h)], [h * MLA_QK for h in range(n_h)],
                       [h * hw + MLA_NOPE for h in range(n_h)], [h * MLA_V for h in range(n_h)],
                       n_h * MLA_QK, n_h * MLA_QK, n_h * hw, n_h * MLA_V, MLA_QK ** -0.5)


def _attn_tm_fwd(q, k, v, lay, n_ctx):
    b, s, _ = q.shape
    tq = min(256, n_ctx)
    nc = n_ctx // tq

    def body(q_ref, k_ref, v_ref, o_ref, lse_ref):
        def run(n_keys):
            for h in range(lay.n_h):
                qo, ko, vo, oo = lay.q_off[h], lay.k_off[h], lay.v_off[h], lay.o_off[h]
                qv = (q_ref[:, qo:qo + lay.dq] * lay.scale).astype(BF16)
                sc = lax.dot_general(qv, k_ref[0:n_keys, ko:ko + lay.dq].astype(BF16), _NT, preferred_element_type=F32)
                m = jnp.max(sc, axis=-1, keepdims=True)
                p = jnp.exp(sc - m)
                l = jnp.sum(p, axis=-1, keepdims=True)
                o = jnp.dot(p.astype(BF16), v_ref[0:n_keys, vo:vo + lay.dv].astype(BF16), preferred_element_type=F32)
                o_ref[:, oo:oo + lay.dv] = o / l
                lse_ref[:, h:h + 1] = m + jnp.log(l)

        pl.when(pl.program_id(2) < nc)(lambda: run(n_ctx))
        pl.when(pl.program_id(2) >= nc)(lambda: run(s))

    return pl.pallas_call(
        body, out_shape=[jax.ShapeDtypeStruct((b, s, lay.groups * lay.wo), F32),
                         jax.ShapeDtypeStruct((b, lay.groups, s, lay.n_h), F32)],
        grid=(b, lay.groups, s // tq),
        in_specs=[pl.BlockSpec((None, tq, lay.wq), lambda bi, g, i: (bi, i, g)),
                  pl.BlockSpec((None, s, lay.wk), lambda bi, g, i: (bi, 0, g)),
                  pl.BlockSpec((None, s, lay.wv), lambda bi, g, i: (bi, 0, g))],
        out_specs=[pl.BlockSpec((None, tq, lay.wo), lambda bi, g, i: (bi, i, g)),
                   pl.BlockSpec((None, None, tq, lay.n_h), lambda bi, g, i: (bi, g, i, 0))],
        name=f"attn_tm_f_{lay.dq}", compiler_params=_params(),
    )(q, k, v)


def _attn_tm_delta(o, do, lay):
    b, s, _ = o.shape
    ts = _pick(s, (256, 128, 64))

    def body(o_ref, do_ref, d_ref):
        for h in range(lay.n_h):
            oo = lay.o_off[h]
            d_ref[:, h:h + 1] = jnp.sum(o_ref[:, oo:oo + lay.dv] * do_ref[:, oo:oo + lay.dv], axis=-1, keepdims=True)

    blk = pl.BlockSpec((None, ts, lay.wo), lambda bi, g, i: (bi, i, g))
    return pl.pallas_call(
        body, out_shape=jax.ShapeDtypeStruct((b, lay.groups, s, lay.n_h), F32), grid=(b, lay.groups, s // ts),
        in_specs=[blk, blk], out_specs=pl.BlockSpec((None, None, ts, lay.n_h), lambda bi, g, i: (bi, g, i, 0)),
        name=f"attn_tm_delta_{lay.dq}", compiler_params=_params(),
    )(o, do)


def _attn_tm_bwd(q, k, v, lse, delta, do, lay, n_ctx):
    b, s, _ = q.shape
    tk = min(256, n_ctx)
    nc = n_ctx // tk

    def body(q_ref, k_ref, v_ref, lse_ref, delta_ref, do_ref, dq_ref, dk_ref, dv_ref):
        @pl.when(pl.program_id(2) == 0)
        def _():
            dq_ref[...] = jnp.zeros_like(dq_ref)

        def run(r0):
            dk_acc, dv_acc = {}, {}
            for h in range(lay.n_h):
                qo, ko, vo, oo = lay.q_off[h], lay.k_off[h], lay.v_off[h], lay.o_off[h]
                kh = k_ref[:, ko:ko + lay.dq].astype(BF16)
                vh = v_ref[:, vo:vo + lay.dv].astype(BF16)
                qv = (q_ref[r0:s, qo:qo + lay.dq] * lay.scale).astype(BF16)
                dob = do_ref[r0:s, oo:oo + lay.dv].astype(BF16)
                sc = lax.dot_general(qv, kh, _NT, preferred_element_type=F32)
                p = jnp.exp(sc - lse_ref[r0:s, h:h + 1])
                dvh = lax.dot_general(p.astype(BF16), dob, _TN, preferred_element_type=F32)
                dp = lax.dot_general(dob, vh, _NT, preferred_element_type=F32)
                dsb = (p * (dp - delta_ref[r0:s, h:h + 1])).astype(BF16)
                dkh = lax.dot_general(dsb, qv, _TN, preferred_element_type=F32)
                dq_ref[r0:s, qo:qo + lay.dq] += jnp.dot(dsb, kh, preferred_element_type=F32) * lay.scale
                dk_acc[ko] = dkh if ko not in dk_acc else dk_acc[ko] + dkh
                dv_acc[vo] = dvh if vo not in dv_acc else dv_acc[vo] + dvh
            if len(dv_acc) * lay.dv != lay.wv:
                dv_ref[...] = jnp.zeros_like(dv_ref)
            for ko, val in dk_acc.items():
                dk_ref[:, ko:ko + lay.dq] = val
            for vo, val in dv_acc.items():
                dv_ref[:, vo:vo + lay.dv] = val

        pl.when(pl.program_id(2) < nc)(lambda: run(0))
        pl.when(pl.program_id(2) >= nc)(lambda: run(n_ctx))

    full = lambda w: pl.BlockSpec((None, s, w), lambda bi, g, j: (bi, 0, g))
    blk = lambda w: pl.BlockSpec((None, tk, w), lambda bi, g, j: (bi, j, g))
    stat = pl.BlockSpec((None, None, s, lay.n_h), lambda bi, g, j: (bi, g, 0, 0))
    return pl.pallas_call(
        body, out_shape=[jax.ShapeDtypeStruct(q.shape, F32), jax.ShapeDtypeStruct(k.shape, F32),
                         jax.ShapeDtypeStruct(v.shape, F32)],
        grid=(b, lay.groups, s // tk),
        in_specs=[full(lay.wq), blk(lay.wk), blk(lay.wv), stat, stat, full(lay.wo)],
        out_specs=[full(lay.wq), blk(lay.wk), blk(lay.wv)],
        name=f"attn_tm_b_{lay.dq}", compiler_params=_params(),
    )(q, k, v, lse, delta, do)


def _make_attention_tm(lay):
    @functools.partial(jax.custom_vjp, nondiff_argnums=(3,))
    def op(q, k, v, n_ctx):
        return _attn_tm_fwd(q, k, v, lay, n_ctx)[0]

    def fwd(q, k, v, n_ctx):
        o, lse = _attn_tm_fwd(q, k, v, lay, n_ctx)
        return o, (q, k, v, o, lse)

    def bwd(n_ctx, res, do):
        q, k, v, o, lse = res
        return _attn_tm_bwd(q, k, v, lse, _attn_tm_delta(o, do, lay), do, lay, n_ctx)

    op.defvjp(fwd, bwd)
    return op


gqa_attention = _make_attention_tm(_gqa_layout())
mla_attention = _make_attention_tm(_mla_layout())

NA_GROUPS = 2


def _na_tm_specs(s, nc, rows):
    hg = NA_HEADS // NA_GROUPS
    w = hg * HEAD_DIM
    qs = pl.BlockSpec((None, GRID_W, w), lambda bi, g, i: (bi, i, g))
    ks = pl.BlockSpec((None, s, w), lambda bi, g, i: (bi, 0, g))
    bs = pl.BlockSpec((hg, None, GRID_W, NA_BAND), lambda bi, g, i: (g, _na_geometry(i, nc, rows)[1], 0, 0))
    ls = pl.BlockSpec((None, None, GRID_W, hg), lambda bi, g, i: (bi, g, i, 0))
    return hg, w, qs, ks, bs, ls


def _na_tm_scores(q_ref, k_ref, bias_ref, hd, n_ctx, start, scale):
    sl = slice(hd * HEAD_DIM, (hd + 1) * HEAD_DIM)
    qv = (q_ref[:, sl] * scale).astype(BF16)
    kc = k_ref[0:n_ctx, sl].astype(BF16)
    kb = k_ref[pl.ds(start, NA_BAND), sl].astype(BF16)
    s_c = lax.dot_general(qv, kc, _NT, preferred_element_type=F32)
    s_l = lax.dot_general(qv, kb, _NT, preferred_element_type=F32) + bias_ref[hd]
    return sl, qv, kc, kb, s_c, s_l


def _na_tm_fwd(q, k, v, bias, n_ctx):
    b, s, _ = q.shape
    nc = n_ctx // GRID_W
    rows = (s - n_ctx) // GRID_W
    scale = HEAD_DIM ** -0.5
    hg, w, qs, ks, bs, ls = _na_tm_specs(s, nc, rows)

    def body(q_ref, k_ref, v_ref, bias_ref, o_ref, lse_ref):
        rs, _ = _na_geometry(pl.program_id(2), nc, rows)
        start = pl.multiple_of(n_ctx + rs * GRID_W, GRID_W)
        for hd in range(hg):
            sl, _, _, _, s_c, s_l = _na_tm_scores(q_ref, k_ref, bias_ref, hd, n_ctx, start, scale)
            m = jnp.maximum(jnp.max(s_c, axis=-1, keepdims=True), jnp.max(s_l, axis=-1, keepdims=True))
            p_c = jnp.exp(s_c - m)
            p_l = jnp.exp(s_l - m)
            l = jnp.sum(p_c, axis=-1, keepdims=True) + jnp.sum(p_l, axis=-1, keepdims=True)
            o = jnp.dot(p_c.astype(BF16), v_ref[0:n_ctx, sl].astype(BF16), preferred_element_type=F32)
            o = o + jnp.dot(p_l.astype(BF16), v_ref[pl.ds(start, NA_BAND), sl].astype(BF16), preferred_element_type=F32)
            o_ref[:, sl] = o / l
            lse_ref[:, hd:hd + 1] = m + jnp.log(l)

    return pl.pallas_call(
        body, out_shape=[jax.ShapeDtypeStruct(q.shape, F32), jax.ShapeDtypeStruct((b, NA_GROUPS, s, hg), F32)],
        grid=(b, NA_GROUPS, s // GRID_W), in_specs=[qs, ks, ks, bs], out_specs=[qs, ls],
        name=f"na_tm_f_{s}", compiler_params=_params(),
    )(q, k, v, bias)


def _na_tm_bwd(q, k, v, bias, o, lse, do, n_ctx):
    b, s, _ = q.shape
    nc = n_ctx // GRID_W
    rows = (s - n_ctx) // GRID_W
    scale = HEAD_DIM ** -0.5
    n_cls = NA_WIN_R + 1
    hg, w, qs, ks, bs, ls = _na_tm_specs(s, nc, rows)

    def body(q_ref, k_ref, v_ref, bias_ref, o_ref, lse_ref, do_ref, dq_ref, dk_ref, dv_ref, db_ref):
        i = pl.program_id(2)
        rs, cls = _na_geometry(i, nc, rows)
        _, cls_prev = _na_geometry(i - 1, nc, rows)
        start = pl.multiple_of(n_ctx + rs * GRID_W, GRID_W)
        first = jnp.logical_or(i == 0, cls != cls_prev)

        @pl.when(i == 0)
        def _():
            dk_ref[...] = jnp.zeros_like(dk_ref)
            dv_ref[...] = jnp.zeros_like(dv_ref)

        for hd in range(hg):
            sl, qv, kc, kb, s_c, s_l = _na_tm_scores(q_ref, k_ref, bias_ref, hd, n_ctx, start, scale)
            lse_v = lse_ref[:, hd:hd + 1]
            p_c = jnp.exp(s_c - lse_v)
            p_l = jnp.exp(s_l - lse_v)
            dov = do_ref[:, sl]
            dob = dov.astype(BF16)
            delta = jnp.sum(dov * o_ref[:, sl], axis=-1, keepdims=True)
            vc = v_ref[0:n_ctx, sl].astype(BF16)
            vb = v_ref[pl.ds(start, NA_BAND), sl].astype(BF16)
            ds_c = p_c * (lax.dot_general(dob, vc, _NT, preferred_element_type=F32) - delta)
            ds_l = p_l * (lax.dot_general(dob, vb, _NT, preferred_element_type=F32) - delta)
            dsc_b = ds_c.astype(BF16)
            dsl_b = ds_l.astype(BF16)
            dq_ref[:, sl] = (jnp.dot(dsc_b, kc, preferred_element_type=F32)
                             + jnp.dot(dsl_b, kb, preferred_element_type=F32)) * scale
            dk_ref[0:n_ctx, sl] += lax.dot_general(dsc_b, qv, _TN, preferred_element_type=F32)
            dk_ref[pl.ds(start, NA_BAND), sl] += lax.dot_general(dsl_b, qv, _TN, preferred_element_type=F32)
            dv_ref[0:n_ctx, sl] += lax.dot_general(p_c.astype(BF16), dob, _TN, preferred_element_type=F32)
            dv_ref[pl.ds(start, NA_BAND), sl] += lax.dot_general(p_l.astype(BF16), dob, _TN, preferred_element_type=F32)

            @pl.when(first)
            def _(hd=hd, ds_l=ds_l):
                db_ref[hd] = ds_l

            @pl.when(jnp.logical_not(first))
            def _(hd=hd, ds_l=ds_l):
                db_ref[hd] += ds_l

    dbs = pl.BlockSpec((None, hg, None, GRID_W, NA_BAND), lambda bi, g, i: (bi, g, _na_geometry(i, nc, rows)[1], 0, 0))
    return pl.pallas_call(
        body,
        out_shape=[jax.ShapeDtypeStruct(q.shape, F32), jax.ShapeDtypeStruct(q.shape, F32), jax.ShapeDtypeStruct(q.shape, F32),
                   jax.ShapeDtypeStruct((b, NA_HEADS, n_cls, GRID_W, NA_BAND), F32)],
        grid=(b, NA_GROUPS, s // GRID_W), in_specs=[qs, ks, ks, bs, qs, ls, qs], out_specs=[qs, ks, ks, dbs],
        name=f"na_tm_b_{s}", compiler_params=_params(),
    )(q, k, v, bias, o, lse, do)


@functools.partial(jax.custom_vjp, nondiff_argnums=(4,))
def na_attention_tm(q, k, v, bias, n_ctx):
    return _na_tm_fwd(q, k, v, bias, n_ctx)[0]


def _na_attention_tm_fwd(q, k, v, bias, n_ctx):
    o, lse = _na_tm_fwd(q, k, v, bias, n_ctx)
    return o, (q, k, v, bias, o, lse)


def _na_attention_tm_bwd(n_ctx, res, do):
    q, k, v, bias, o, lse = res
    dq, dk, dv, db = _na_tm_bwd(q, k, v, bias, o, lse, do, n_ctx)
    return dq, dk, dv, _sum_rows(db.reshape(db.shape[0], -1, NA_BAND), db.shape[0]).reshape(db.shape[1:])


na_attention_tm.defvjp(_na_attention_tm_fwd, _na_attention_tm_bwd)


def _cmul(ar, ai, br, bi):
    return ar * br - ai * bi, ar * bi + ai * br


def _s5_chunk(n_ctx):
    return min(256, n_ctx)


def _s5_tables(a_re, a_im, t_len, rev):
    a_re, a_im = lax.stop_gradient(a_re), lax.stop_gradient(a_im)
    mag = jnp.sqrt(a_re * a_re + a_im * a_im)
    th = jnp.arctan2(a_im, a_re)
    t = jnp.arange(t_len + 1, dtype=F32)[:, None]
    pm = jnp.where(t == 0, 1.0, jnp.exp(t * jnp.log(jnp.maximum(mag, 1e-37))) * (mag > 0))
    pw = jnp.stack([pm * jnp.cos(t * th), pm * jnp.sin(t * th)])
    steps = jnp.concatenate([pw[:, min(2 ** i, t_len)][:, None] for i in range(8)], axis=1)
    tile = pw[:, 1:9]
    a8k = pw[:, 0:t_len:8]
    if rev:
        tile, a8k = tile[:, ::-1], a8k[:, ::-1]
    misc = jnp.concatenate([pw[:, t_len:t_len + 1], jnp.zeros((2, 7, pw.shape[-1]), F32)], axis=1)
    return jnp.concatenate([steps, tile, misc, a8k], axis=1)


def _scan_chunk(x_re, x_im, tab_ref, hin_re, hin_im, rev, t_len, xs_ref, es_ref):
    outs = [_scan_slab(x_re[:, k:k + LANE], x_im[:, k:k + LANE], tab_ref, hin_re[:, k:k + LANE], hin_im[:, k:k + LANE],
                       rev, t_len, xs_ref, es_ref, k) for k in range(0, x_re.shape[-1], LANE)]
    return tuple(jnp.concatenate([o[t] for o in outs], axis=-1) for t in range(4))


def _scan_slab(x_re, x_im, tab_ref, hin_re, hin_im, rev, t_len, xs_ref, es_ref, k0):
    lanes = LANE
    n2 = t_len // 8
    tab_ref = tab_ref.at[:, :, k0:k0 + LANE]
    rin = lax.broadcasted_iota(jnp.int32, (t_len, lanes), 0) & 7
    for li, sh in enumerate((1, 2, 4)):
        m_re, m_im = tab_ref[0, li:li + 1, :], tab_ref[1, li:li + 1, :]
        amt = sh if not rev else t_len - sh
        c_re, c_im = _cmul(m_re, m_im, pltpu.roll(x_re, amt, 0), pltpu.roll(x_im, amt, 0))
        ok = (rin >= sh) if not rev else (rin < 8 - sh)
        x_re = x_re + jnp.where(ok, c_re, 0.0)
        x_im = x_im + jnp.where(ok, c_im, 0.0)
    xr_ref, xi_ref = xs_ref
    xr_ref[...] = x_re
    xi_ref[...] = x_im
    off = 0 if rev else 7
    e_re = xr_ref[pl.ds(off, n2, stride=8), :]
    e_im = xi_ref[pl.ds(off, n2, stride=8), :]
    row2 = lax.broadcasted_iota(jnp.int32, (n2, lanes), 0)
    sh, li = 1, 3
    while sh < n2:
        m_re, m_im = tab_ref[0, li:li + 1, :], tab_ref[1, li:li + 1, :]
        amt = sh if not rev else n2 - sh
        c_re, c_im = _cmul(m_re, m_im, pltpu.roll(e_re, amt, 0), pltpu.roll(e_im, amt, 0))
        ok = (row2 >= sh) if not rev else (row2 < n2 - sh)
        e_re = e_re + jnp.where(ok, c_re, 0.0)
        e_im = e_im + jnp.where(ok, c_im, 0.0)
        sh, li = sh * 2, li + 1
    es_ref[0] = e_re
    es_ref[1] = e_im
    last = 0 if rev else n2 - 1
    t_re, t_im = _cmul(tab_ref[0, 16:17, :], tab_ref[1, 16:17, :], hin_re, hin_im)
    hout_re = es_ref[0, last:last + 1, :] + t_re
    hout_im = es_ref[1, last:last + 1, :] + t_im
    amt = 1 if not rev else n2 - 1
    ok = (row2 >= 1) if not rev else (row2 < n2 - 1)
    k_re, k_im = _cmul(tab_ref[0, 24:24 + n2, :], tab_ref[1, 24:24 + n2, :], hin_re, hin_im)
    c_re = jnp.where(ok, pltpu.roll(e_re, amt, 0), 0.0) + k_re
    c_im = jnp.where(ok, pltpu.roll(e_im, amt, 0), 0.0) + k_im
    tp_re, tp_im = tab_ref[0, 8:16, :][None], tab_ref[1, 8:16, :][None]
    add_re, add_im = _cmul(tp_re, tp_im, c_re[:, None, :], c_im[:, None, :])
    h_re = xr_ref[...] + add_re.reshape(t_len, lanes)
    h_im = xi_ref[...] + add_im.reshape(t_len, lanes)
    return h_re, h_im, hout_re, hout_im


def _s5_order(j, n_chunks, nc, rev):
    if not rev:
        return j
    return jnp.where(j < nc, nc - 1 - j, n_chunks - 1 - (j - nc))


def _s5_fwd(u, tab, b_bd, c_bd, n_ctx, rev):
    b, s, w = u.shape
    lanes = b_bd.shape[-1]
    t_len = _s5_chunk(n_ctx)
    n_chunks, nc = s // t_len, n_ctx // t_len

    def body(u_ref, tab_ref, b_ref, c_ref, y_ref, h_ref, hin_ref, carry_ref, xr_ref, xi_ref, es_ref):
        xs_ref = (xr_ref, xi_ref)

        @pl.when(pl.program_id(1) == 0)
        def _():
            carry_ref[...] = jnp.zeros_like(carry_ref)

        ub = u_ref[...].astype(BF16)
        x_re = jnp.dot(ub, b_ref[0].astype(BF16), preferred_element_type=F32)
        x_im = jnp.dot(ub, b_ref[1].astype(BF16), preferred_element_type=F32)
        hin_re, hin_im = carry_ref[0, 0:1, :], carry_ref[1, 0:1, :]
        hin_ref[...] = carry_ref[...]
        h_re, h_im, ho_re, ho_im = _scan_chunk(x_re, x_im, tab_ref, hin_re, hin_im, rev, t_len, xs_ref, es_ref)
        carry_ref[0] = jnp.broadcast_to(ho_re, (8, lanes))
        carry_ref[1] = jnp.broadcast_to(ho_im, (8, lanes))
        h_ref[0] = h_re
        h_ref[1] = h_im
        y_ref[...] = (jnp.dot(h_re.astype(BF16), c_ref[0].astype(BF16), preferred_element_type=F32)
                      - jnp.dot(h_im.astype(BF16), c_ref[1].astype(BF16), preferred_element_type=F32))

    order = lambda j: _s5_order(j, n_chunks, nc, rev)
    whole = lambda arr: pl.BlockSpec(arr.shape, lambda bi, j: (0,) * arr.ndim)
    return pl.pallas_call(
        body,
        out_shape=[jax.ShapeDtypeStruct((b, s, w), F32), jax.ShapeDtypeStruct((2, b, s, lanes), F32),
                   jax.ShapeDtypeStruct((2, b, n_chunks, 8, lanes), F32)],
        grid=(b, n_chunks),
        in_specs=[pl.BlockSpec((None, t_len, w), lambda bi, j: (bi, order(j), 0)), whole(tab), whole(b_bd), whole(c_bd)],
        out_specs=[pl.BlockSpec((None, t_len, w), lambda bi, j: (bi, order(j), 0)),
                   pl.BlockSpec((2, None, t_len, lanes), lambda bi, j: (0, bi, order(j), 0)),
                   pl.BlockSpec((2, None, None, 8, lanes), lambda bi, j: (0, bi, order(j), 0, 0))],
        scratch_shapes=[pltpu.VMEM((2, 8, lanes), F32), pltpu.VMEM((t_len, LANE), F32), pltpu.VMEM((t_len, LANE), F32),
                        pltpu.VMEM((2, t_len // 8, LANE), F32)],
        name=f"s5_f_{s}_{int(rev)}", compiler_params=_params(),
    )(u, tab, b_bd, c_bd)


def _s5_bwd(u, tab_adj, b_bd, c_bd, h, hin, dy, n_ctx, rev):
    b, s, w = u.shape
    lanes = b_bd.shape[-1]
    t_len = _s5_chunk(n_ctx)
    n_chunks, nc = s // t_len, n_ctx // t_len
    arev = not rev

    def body(u_ref, tab_ref, b_ref, c_ref, h_ref, hin_ref, dy_ref, du_ref, db_ref, dc_ref, da_ref,
             carry_ref, xr_ref, xi_ref, es_ref):
        xs_ref = (xr_ref, xi_ref)
        first = jnp.logical_and(pl.program_id(0) == 0, pl.program_id(1) == 0)

        @pl.when(pl.program_id(1) == 0)
        def _():
            carry_ref[...] = jnp.zeros_like(carry_ref)

        dyv = dy_ref[...]
        dyb = dyv.astype(BF16)
        dn = (((1,), (1,)), ((), ()))
        dt = (((0,), (0,)), ((), ()))
        x_re = lax.dot_general(dyb, c_ref[0].astype(BF16), dn, preferred_element_type=F32)
        x_im = -lax.dot_general(dyb, c_ref[1].astype(BF16), dn, preferred_element_type=F32)
        g_re, g_im, go_re, go_im = _scan_chunk(x_re, x_im, tab_ref, carry_ref[0, 0:1, :], carry_ref[1, 0:1, :],
                                               arev, t_len, xs_ref, es_ref)
        carry_ref[0] = jnp.broadcast_to(go_re, (8, lanes))
        carry_ref[1] = jnp.broadcast_to(go_im, (8, lanes))
        h_re, h_im = h_ref[0], h_ref[1]
        gb_re, gb_im = g_re.astype(BF16), g_im.astype(BF16)
        du_ref[...] = (lax.dot_general(gb_re, b_ref[0].astype(BF16), dn, preferred_element_type=F32)
                       + lax.dot_general(gb_im, b_ref[1].astype(BF16), dn, preferred_element_type=F32))
        ub = u_ref[...].astype(BF16)
        db_re = lax.dot_general(ub, gb_re, dt, preferred_element_type=F32)
        db_im = lax.dot_general(ub, gb_im, dt, preferred_element_type=F32)
        dc_re = lax.dot_general(h_re.astype(BF16), dyb, dt, preferred_element_type=F32)
        dc_im = -lax.dot_general(h_im.astype(BF16), dyb, dt, preferred_element_type=F32)
        row = lax.broadcasted_iota(jnp.int32, (t_len, lanes), 0)
        amt = 1 if not rev else t_len - 1
        edge = (row == 0) if not rev else (row == t_len - 1)
        hp_re = jnp.where(edge, hin_ref[0, 0:1, :], pltpu.roll(h_re, amt, 0))
        hp_im = jnp.where(edge, hin_ref[1, 0:1, :], pltpu.roll(h_im, amt, 0))
        da_re = jnp.sum(g_re * hp_re + g_im * hp_im, axis=0, keepdims=True)
        da_im = jnp.sum(g_im * hp_re - g_re * hp_im, axis=0, keepdims=True)

        @pl.when(first)
        def _():
            db_ref[0], db_ref[1] = db_re, db_im
            dc_ref[0], dc_ref[1] = dc_re, dc_im
            da_ref[0] = jnp.broadcast_to(da_re, (8, lanes))
            da_ref[1] = jnp.broadcast_to(da_im, (8, lanes))

        @pl.when(jnp.logical_not(first))
        def _():
            db_ref[0] += db_re
            db_ref[1] += db_im
            dc_ref[0] += dc_re
            dc_ref[1] += dc_im
            da_ref[0] += jnp.broadcast_to(da_re, (8, lanes))
            da_ref[1] += jnp.broadcast_to(da_im, (8, lanes))

    order = lambda j: _s5_order(n_chunks - 1 - j, n_chunks, nc, rev)
    whole = lambda arr: pl.BlockSpec(arr.shape, lambda bi, j: (0,) * arr.ndim)
    us = pl.BlockSpec((None, t_len, w), lambda bi, j: (bi, order(j), 0))
    return pl.pallas_call(
        body,
        out_shape=[jax.ShapeDtypeStruct((b, s, w), F32), jax.ShapeDtypeStruct(b_bd.shape, F32),
                   jax.ShapeDtypeStruct(c_bd.shape, F32), jax.ShapeDtypeStruct((2, 8, lanes), F32)],
        grid=(b, n_chunks),
        in_specs=[us, whole(tab_adj), whole(b_bd), whole(c_bd),
                  pl.BlockSpec((2, None, t_len, lanes), lambda bi, j: (0, bi, order(j), 0)),
                  pl.BlockSpec((2, None, None, 8, lanes), lambda bi, j: (0, bi, order(j), 0, 0)), us],
        out_specs=[us, whole(b_bd), whole(c_bd), pl.BlockSpec((2, 8, lanes), lambda bi, j: (0, 0, 0))],
        scratch_shapes=[pltpu.VMEM((2, 8, lanes), F32), pltpu.VMEM((t_len, LANE), F32), pltpu.VMEM((t_len, LANE), F32),
                        pltpu.VMEM((2, t_len // 8, LANE), F32)],
        name=f"s5_b_{s}_{int(rev)}", compiler_params=_params(),
    )(u, tab_adj, b_bd, c_bd, h, hin, dy)


@functools.partial(jax.custom_vjp, nondiff_argnums=(4, 5))
def s5_direction(u, a, b_bd, c_bd, n_ctx, rev):
    tab = _s5_tables(a[0], a[1], _s5_chunk(n_ctx), rev)
    return _s5_fwd(u, tab, b_bd, c_bd, n_ctx, rev)[0]


def _s5_direction_fwd(u, a, b_bd, c_bd, n_ctx, rev):
    tab = _s5_tables(a[0], a[1], _s5_chunk(n_ctx), rev)
    y, h, hin = _s5_fwd(u, tab, b_bd, c_bd, n_ctx, rev)
    return y, (u, a, b_bd, c_bd, h, hin)


def _s5_direction_bwd(n_ctx, rev, res, dy):
    u, a, b_bd, c_bd, h, hin = res
    tab_adj = _s5_tables(a[0], -a[1], _s5_chunk(n_ctx), not rev)
    du, db, dc, da = _s5_bwd(u, tab_adj, b_bd, c_bd, h, hin, dy, n_ctx, rev)
    return du, da[:, 0, :], db, dc


s5_direction.defvjp(_s5_direction_fwd, _s5_direction_bwd)


def _s5_discretize(lam_re, lam_im, log_dt, b_re, b_im):
    dt = jnp.exp(log_dt)[:, None]
    mag = jnp.exp(lam_re * dt)
    a_re = mag * jnp.cos(lam_im * dt)
    a_im = mag * jnp.sin(lam_im * dt)
    den = jnp.square(lam_re) + jnp.square(lam_im)
    f_re = ((a_re - 1.0) * lam_re + a_im * lam_im) / den
    f_im = (a_im * lam_re - (a_re - 1.0) * lam_im) / den
    bb_re = f_re[..., None] * b_re - f_im[..., None] * b_im
    bb_im = f_re[..., None] * b_im + f_im[..., None] * b_re
    return a_re, a_im, bb_re, bb_im


def _block_diag(t):
    g, r, c = t.shape
    return (jnp.eye(g, dtype=F32)[:, None, :, None] * t[:, :, None, :]).reshape(g * r, g * c)


def _loss_head(y, target):
    b, n, d = y.shape
    ts = _pick(n, (256, 128, 64))

    def body(y_ref, t_ref, loss_ref, dy_ref):
        first = jnp.logical_and(pl.program_id(0) == 0, pl.program_id(1) == 0)
        err = y_ref[...] - t_ref[...]
        dy_ref[...] = err * (1.0 / d)
        part = 0.5 * jnp.sum(jnp.sum(err * err, axis=-1, keepdims=True) * (1.0 / d), axis=0, keepdims=True)
        part = jnp.broadcast_to(part, (8, LANE))

        @pl.when(first)
        def _():
            loss_ref[...] = part

        @pl.when(jnp.logical_not(first))
        def _():
            loss_ref[...] += part

    blk = pl.BlockSpec((None, ts, d), lambda bi, i: (bi, i, 0))
    return pl.pallas_call(
        body, out_shape=[jax.ShapeDtypeStruct((8, LANE), F32), jax.ShapeDtypeStruct((b, n, d), F32)],
        grid=(b, n // ts), in_specs=[blk, blk], out_specs=[pl.BlockSpec((8, LANE), lambda bi, i: (0, 0)), blk],
        name="loss_head", compiler_params=_params(),
    )(y, target)


def _adamw(w, g, m, v):
    shape = w.shape
    n = int(np.prod(shape))
    cols = shape[-1]
    r = n // cols
    tr = _pick(r, (512, 256, 128, 64, 32, 16, 8))
    c1 = 1.0 / (1.0 - ADAM_B1 ** ADAM_STEP)
    c2 = 1.0 / (1.0 - ADAM_B2 ** ADAM_STEP)

    def body(w_ref, g_ref, m_ref, v_ref, d_ref, mo_ref, vo_ref):
        gv = g_ref[...]
        m2 = ADAM_B1 * m_ref[...] + (1.0 - ADAM_B1) * gv
        v2 = ADAM_B2 * v_ref[...] + (1.0 - ADAM_B2) * (gv * gv)
        d_ref[...] = -ADAM_LR * ((m2 * c1) / (jnp.sqrt(v2 * c2) + ADAM_EPS) + ADAM_WD * w_ref[...])
        mo_ref[...] = m2
        vo_ref[...] = v2

    blk = pl.BlockSpec((tr, cols), lambda i: (i, 0))
    outs = pl.pallas_call(
        body, out_shape=[jax.ShapeDtypeStruct((r, cols), F32)] * 3, grid=(r // tr,),
        in_specs=[blk] * 4, out_specs=[blk] * 3, name=f"adamw_{r}x{cols}", compiler_params=_params(),
    )(*[t.reshape(r, cols) for t in (w, g, m, v)])
    return tuple(o.reshape(shape) for o in outs)


def _sum_rows(x, n):
    _, r, c = x.shape
    tr = _pick(r, (512, 256, 128, 64, 32, 16, 8))

    def body(x_ref, o_ref):
        acc = x_ref[0]
        for j in range(1, n):
            acc = acc + x_ref[j]
        o_ref[...] = acc

    return pl.pallas_call(
        body, out_shape=jax.ShapeDtypeStruct((r, c), F32), grid=(r // tr,),
        in_specs=[pl.BlockSpec((n, tr, c), lambda i: (0, i, 0))], out_specs=pl.BlockSpec((tr, c), lambda i: (i, 0)),
        name=f"sum{n}_{r}x{c}", compiler_params=_params(),
    )(x)


def _accumulate(parts, out_dtype):
    r, c = parts[0].shape[-2:]
    tr = _pick(r, (512, 256, 128, 64, 32, 16))

    def body(*refs):
        acc = None
        for ref in refs[:-1]:
            terms = [ref[j] for j in range(ref.shape[0])] if len(ref.shape) == 3 else [ref[...]]
            for t in terms:
                acc = t.astype(F32) if acc is None else acc + t.astype(F32)
        refs[-1][...] = acc.astype(out_dtype)

    specs = [pl.BlockSpec((p.shape[0], tr, c), lambda i: (0, i, 0)) if p.ndim == 3 else pl.BlockSpec((tr, c), lambda i: (i, 0))
             for p in parts]
    tag = "_".join(str(p.shape[0]) if p.ndim == 3 else "1" for p in parts)
    return pl.pallas_call(
        body, out_shape=jax.ShapeDtypeStruct((r, c), out_dtype), grid=(r // tr,), in_specs=specs,
        out_specs=pl.BlockSpec((tr, c), lambda i: (i, 0)), name=f"accumulate_{tag}_{r}x{c}_{jnp.dtype(out_dtype).name}",
        compiler_params=_params(),
    )(*parts)


def _add2(x, y):
    shape = x.shape
    c = shape[-1]
    r = int(np.prod(shape)) // c
    tr = _pick(r, (512, 256, 128, 64, 32, 16, 8))

    def body(x_ref, y_ref, o_ref):
        o_ref[...] = x_ref[...] + y_ref[...]

    blk = pl.BlockSpec((tr, c), lambda i: (i, 0))
    return pl.pallas_call(
        body, out_shape=jax.ShapeDtypeStruct((r, c), F32), grid=(r // tr,), in_specs=[blk, blk], out_specs=blk,
        name=f"add2_{r}x{c}", compiler_params=_params(),
    )(x.reshape(r, c), y.reshape(r, c)).reshape(shape)


_FLIPS = ((1, 0), (0, 1), (1, 1))


def _me():
    return lax.axis_index("x"), lax.axis_index("y"), lax.axis_index("c")


def allgather8(v):
    m_per, n = v.shape

    def body(x_ref, out_ref, send_sems, recv_sems, local_sem):
        x, y, c = _me()
        me, sibling = (x, y, c), (x, y, 1 - c)
        chips = [(1 - x, y), (x, 1 - y), (1 - x, 1 - y)]

        def rows(px, py, pc):
            return out_ref.at[pl.ds((4 * px + 2 * py + pc) * m_per, m_per), :]

        def copy(k, block, to, src=None):
            return pltpu.make_async_remote_copy(
                src_ref=rows(*block) if src is None else src, dst_ref=rows(*block),
                send_sem=send_sems.at[k], recv_sem=recv_sems.at[k], device_id=to, device_id_type=MESH)

        mine = pltpu.make_async_copy(x_ref, rows(*me), local_sem)
        mine.start()
        first = [copy(0, me, sibling, src=x_ref)]
        first += [copy(1 + j, me, (*chip, c), src=x_ref) for j, chip in enumerate(chips)]
        for cp in first:
            cp.start()
        passed = [copy(4 + j, (*chip, c), sibling) for j, chip in enumerate(chips)]
        for j, chip in enumerate(chips):
            copy(1 + j, (*chip, c), me).wait_recv()
            passed[j].start()
        copy(0, sibling, me).wait_recv()
        for j, chip in enumerate(chips):
            copy(4 + j, (*chip, 1 - c), me).wait_recv()
        for cp in first + passed:
            cp.wait_send()
        mine.wait()

    return pl.pallas_call(
        body, out_shape=jax.ShapeDtypeStruct((N_DEV * m_per, n), v.dtype), in_specs=[VMEM_SPEC], out_specs=VMEM_SPEC,
        scratch_shapes=[pltpu.SemaphoreType.DMA((7,)), pltpu.SemaphoreType.DMA((7,)), pltpu.SemaphoreType.DMA],
        name=f"allgather8_{m_per}x{n}", compiler_params=_params(),
    )(v)


def _row_chunks(rows, tile_rows, want):
    n = want
    while n > 1 and rows % (n * tile_rows):
        n //= 2
    return [(i * (rows // n), rows // n) for i in range(n)]


def _remote(src, dst, send_sem, recv_sem, to):
    return pltpu.make_async_remote_copy(src_ref=src, dst_ref=dst, send_sem=send_sem, recv_sem=recv_sem, device_id=to,
                                        device_id_type=MESH)


def plane_allgather(big, small):
    rows = big.shape[0]
    rh = rows // 2
    tile = 16 if big.dtype == BF16 else 8
    ch_full = _row_chunks(rows, tile, 8)
    ch_half = _row_chunks(rh, tile, 4)

    def body(big_ref, small_ref, obig_ref, osmall_ref, send_sems, recv_sems, fwd_send, fwd_recv, own_send, own_recv):
        x, y, c = _me()
        me = 2 * x + y
        sibling = (x, y, 1 - c)
        mine = pl.ds(c * rh, rh)
        other = pl.ds((1 - c) * rh, rh)
        peers = [((x + fx) & 1, (y + fy) & 1) for fx, fy in _FLIPS]
        for st, sz in ch_full:
            sl = pl.ds(st, sz)
            _remote(big_ref.at[sl], obig_ref.at[me, sl], own_send.at[0], own_recv.at[0], sibling).start()
        _remote(small_ref, osmall_ref.at[me], own_send.at[1], own_recv.at[1], sibling).start()
        for j, (px, py) in enumerate(peers):
            for st, sz in ch_half:
                sl = pl.ds(c * rh + st, sz)
                _remote(big_ref.at[sl], obig_ref.at[me, sl], send_sems.at[j], recv_sems.at[j], (px, py, c)).start()
            _remote(small_ref, osmall_ref.at[me], send_sems.at[3 + j], recv_sems.at[3 + j], (px, py, c)).start()
        for j, (px, py) in enumerate(peers):
            pidx = 2 * px + py
            _remote(big_ref.at[mine], obig_ref.at[pidx, mine], send_sems.at[j], recv_sems.at[j], (px, py, c)).wait_recv()
            for st, sz in ch_half:
                sl = pl.ds(c * rh + st, sz)
                _remote(obig_ref.at[pidx, sl], obig_ref.at[pidx, sl], fwd_send.at[j], fwd_recv.at[j], sibling).start()
            _remote(small_ref, osmall_ref.at[pidx], send_sems.at[3 + j], recv_sems.at[3 + j], (px, py, c)).wait_recv()
        for j, (px, py) in enumerate(peers):
            pidx = 2 * px + py
            _remote(obig_ref.at[pidx, other], obig_ref.at[pidx, other], fwd_send.at[j], fwd_recv.at[j], sibling).wait_recv()
        for j, (px, py) in enumerate(peers):
            pidx = 2 * px + py
            _remote(big_ref.at[mine], obig_ref.at[me, mine], send_sems.at[j], recv_sems.at[j], (px, py, c)).wait_send()
            _remote(small_ref, osmall_ref.at[me], send_sems.at[3 + j], recv_sems.at[3 + j], (px, py, c)).wait_send()
            _remote(obig_ref.at[pidx, mine], obig_ref.at[pidx, mine], fwd_send.at[j], fwd_recv.at[j], sibling).wait_send()
        _remote(big_ref, obig_ref.at[me], own_send.at[0], own_recv.at[0], sibling).wait()
        _remote(small_ref, osmall_ref.at[me], own_send.at[1], own_recv.at[1], sibling).wait()

    return pl.pallas_call(
        body, out_shape=[jax.ShapeDtypeStruct((N_PLANE,) + big.shape, big.dtype),
                         jax.ShapeDtypeStruct((N_PLANE,) + small.shape, small.dtype)],
        in_specs=[ANY, ANY], out_specs=[ANY, ANY],
        scratch_shapes=[pltpu.SemaphoreType.DMA((6,)), pltpu.SemaphoreType.DMA((6,)), pltpu.SemaphoreType.DMA((3,)),
                        pltpu.SemaphoreType.DMA((3,)), pltpu.SemaphoreType.DMA((2,)), pltpu.SemaphoreType.DMA((2,))],
        name="plane_allgather", compiler_params=_params(),
    )(big, small)


def plane_scatter(p):
    tile = 16 if p.dtype == BF16 else 8
    chunks = _row_chunks(p.shape[1], tile, 4)

    def body(p_ref, out_ref, send_sems, recv_sems):
        x, y, c = _me()
        peers = [((x + fx) & 1, (y + fy) & 1) for fx, fy in _FLIPS]
        for j, (px, py) in enumerate(peers):
            for st, sz in chunks:
                sl = pl.ds(st, sz)
                _remote(p_ref.at[2 * px + py, sl], out_ref.at[j, sl], send_sems.at[j], recv_sems.at[j], (px, py, c)).start()
        for j, (px, py) in enumerate(peers):
            _remote(p_ref.at[0], out_ref.at[j], send_sems.at[j], recv_sems.at[j], (px, py, c)).wait_recv()
        for j, (px, py) in enumerate(peers):
            _remote(p_ref.at[0], out_ref.at[j], send_sems.at[j], recv_sems.at[j], (px, py, c)).wait_send()

    return pl.pallas_call(
        body, out_shape=jax.ShapeDtypeStruct((len(_FLIPS),) + p.shape[1:], p.dtype), in_specs=[ANY], out_specs=ANY,
        scratch_shapes=[pltpu.SemaphoreType.DMA((3,)), pltpu.SemaphoreType.DMA((3,))],
        name="plane_scatter", compiler_params=_params(),
    )(p)


def sibling_swap(s):
    tile = 16 if s.dtype == BF16 else 8
    chunks = _row_chunks(s.shape[0], tile, 8)

    def body(s_ref, got_ref, send_sem, recv_sem):
        x, y, c = _me()
        for st, sz in chunks:
            sl = pl.ds(st, sz)
            _remote(s_ref.at[sl], got_ref.at[sl], send_sem, recv_sem, (x, y, 1 - c)).start()
        _remote(s_ref, got_ref, send_sem, recv_sem, (x, y, 1 - c)).wait()

    return pl.pallas_call(
        body, out_shape=jax.ShapeDtypeStruct(s.shape, s.dtype), in_specs=[ANY], out_specs=ANY,
        scratch_shapes=[pltpu.SemaphoreType.DMA, pltpu.SemaphoreType.DMA],
        name="sibling_swap", compiler_params=_params(),
    )(s)


def _heads(t, n_heads):
    b, s, w = t.shape
    return jnp.transpose(t.reshape(b, s, n_heads, w // n_heads), (0, 2, 1, 3)).reshape(b * n_heads, s, w // n_heads)


def _unheads(t, b):
    bh, s, d = t.shape
    return jnp.transpose(t.reshape(b, bh // b, s, d), (0, 2, 1, 3)).reshape(b, s, (bh // b) * d)


def _op(cache, fn, name, kinds, out_dims, **kw):
    key = (name, tuple(out_dims), tuple(sorted(kw.items())))
    if key not in cache:
        cache[key] = make_rowwise(fn, name, kinds, out_dims, **kw)
    return cache[key]


def _even_mixer(ops, a, w, n_ctx):
    b, s, d = a.shape
    proj = linear(a.reshape(b * s, d), w["e_w_in"]).reshape(b, s, -1)
    q, k, v, u = jnp.split(proj, [GQA_Q_W, GQA_Q_W + GQA_KV_W, GQA_Q_W + 2 * GQA_KV_W], axis=-1)
    cos, sin = _rope_tables(n_ctx, s - n_ctx, HEAD_DIM, 0, HEAD_DIM)
    rot = jnp.asarray(_rope_matrix(HEAD_DIM, 0, HEAD_DIM))
    qn = head_norm_rope(q, w["e_g_q"][None], cos, sin, rot, GQA_Q_HEADS)
    kn = head_norm_rope(k, w["e_g_k"][None], cos, sin, rot, GQA_KV_HEADS)
    att = gqa_attention(qn, kn, v, n_ctx)
    ys = []
    for dr in range(2):
        a_re, a_im, bb_re, bb_im = _s5_discretize(w["ssm_lam_re"][dr], w["ssm_lam_im"][dr], w["ssm_log_dt"][dr],
                                                  w["ssm_b_re"][dr], w["ssm_b_im"][dr])
        a_flat = jnp.stack([a_re.reshape(-1), a_im.reshape(-1)])
        b_bd = jnp.stack([_block_diag(jnp.swapaxes(bb_re, 1, 2)), _block_diag(jnp.swapaxes(bb_im, 1, 2))])
        c_bd = jnp.stack([_block_diag(jnp.swapaxes(w["ssm_c_re"][dr], 1, 2)),
                          _block_diag(jnp.swapaxes(w["ssm_c_im"][dr], 1, 2))])
        ys.append(s5_direction(u, a_flat, b_bd, c_bd, n_ctx, dr == 1))
    pre = _op(ops, _fn_glu_pre, "glu_pre", ("row", "row", "row", "glob"), (SSM_WIDTH,))
    post = _op(ops, _fn_glu_post, "glu_post", ("row", "row", "glob"), (SSM_WIDTH,))
    z = pre(u, ys[0], ys[1], w["ssm_d"][None])[0]
    t = linear(z.reshape(b * s, SSM_WIDTH), w["ssm_w_glu"]).reshape(b, s, SSM_WIDTH)
    ssm = post(z, t, w["ssm_b_glu"][None])[0]
    mix = jnp.concatenate([att, ssm], axis=-1)
    return linear(mix.reshape(b * s, -1), w["e_w_out"]).reshape(b, s, d)


def _odd_mixer(ops, a, w, n_ctx):
    b, s, d = a.shape
    w_in = jnp.pad(w["o_w_in"], ((0, 0), (0, ODD_IN_PAD - ODD_IN_W)))
    proj = linear(a.reshape(b * s, d), w_in).reshape(b, s, -1)
    c1 = MLA_Q_RANK
    c2 = c1 + MLA_KV_RANK
    c3 = c2 + MLA_ROPE
    cq, ckv, kr, nq, nk, nv, _ = jnp.split(proj, [c1, c2, c3, c3 + NA_W, c3 + 2 * NA_W, ODD_IN_W], axis=-1)
    nrm = lambda wd: _op(ops, _fn_norm, f"norm{wd}", ("row", "glob"), (wd,))
    cqn = nrm(MLA_Q_RANK)(cq, w["mla_g_cq"][None])[0]
    ckvn = nrm(MLA_KV_RANK)(ckv, w["mla_g_ckv"][None])[0]
    q = linear(cqn.reshape(b * s, -1), w["mla_w_uq"]).reshape(b, s, -1)
    kv = linear(ckvn.reshape(b * s, -1), w["mla_w_ukv"]).reshape(b, s, -1)
    cos, sin = _rope_tables(n_ctx, s - n_ctx, MLA_QK, MLA_NOPE, MLA_ROPE)
    rot = jnp.asarray(_rope_matrix(MLA_QK, MLA_NOPE, MLA_ROPE))
    mq = head_norm_rope(q, w["mla_g_q"][None], cos, sin, rot, MLA_HEADS)
    mk = mla_k_prep(kv, kr, w["mla_g_k"][None], cos, sin, rot)
    mla = mla_attention(mq, mk, kv, n_ctx)
    nqn = head_norm_rope(nq, w["na_g_q"][None], None, None, None, NA_HEADS)
    nkn = head_norm_rope(nk, w["na_g_k"][None], None, None, None, NA_HEADS)
    na = na_attention_tm(nqn, nkn, nv, na_bias_table(w["na_rpb"]), n_ctx)
    mix = jnp.concatenate([mla, na], axis=-1)
    return linear(mix.reshape(b * s, -1), w["o_w_out"]).reshape(b, s, d)


_EVEN_KEYS = ("e_w_in", "e_w_out", "e_g_q", "e_g_k", "ssm_lam_re", "ssm_lam_im", "ssm_log_dt", "ssm_b_re", "ssm_b_im",
              "ssm_c_re", "ssm_c_im", "ssm_d", "ssm_w_glu", "ssm_b_glu")
_ODD_KEYS = ("o_w_in", "o_w_out", "mla_g_cq", "mla_g_ckv", "mla_w_uq", "mla_w_ukv", "mla_g_q", "mla_g_k", "na_g_q",
             "na_g_k", "na_rpb")


def _trunk(x_all, mods, w, n_ctx):
    ops = {}
    depth = mods.shape[0]
    b, s, d = x_all.shape
    modulate = _op(ops, _fn_modulate, "modulate", ("row", "glob", "seg", "seg"), (d,), nctx_rows=n_ctx)
    gated = _op(ops, _fn_gated_add, "gated", ("row", "row", "seg"), (d,), nctx_rows=n_ctx)
    x = x_all
    for i in range(depth):
        j = i // 2
        m = [mods[i][:, :, r:r + 1, :] for r in range(N_MOD)]
        a = modulate(x, w["g_norm1"][i][None], m[0], m[1])[0]
        if i % 2 == 0:
            o = _even_mixer(ops, a, {k: w[k][j] for k in _EVEN_KEYS}, n_ctx)
        else:
            o = _odd_mixer(ops, a, {k: w[k][j] for k in _ODD_KEYS}, n_ctx)
        x = gated(x, o, m[2])[0]
        a2 = modulate(x, w["g_norm2"][i][None], m[3], m[4])[0]
        f = ffn(a2.reshape(b * s, d), w["w_ff1"][i], w["w_ff2"][i]).reshape(b, s, d)
        x = gated(x, f, m[5])[0]
    return x[:, n_ctx:]


def local_step(x, ctx, mods, w, loss_target):
    n_ctx = ctx.shape[1]
    x_all = jnp.concatenate([ctx, x], axis=1)
    y, vjp = jax.vjp(lambda xa, md, ww: _trunk(xa, md, ww, n_ctx), x_all, mods, w)
    loss_tile, dy = _loss_head(y, loss_target)
    dx_all, dmods, dw = vjp(dy)
    return loss_tile[0, 0], dx_all[:, n_ctx:], dmods, dw


_SHARDED = (("w_ff1", 2), ("w_ff2", 1), ("e_w_in", 2), ("e_w_out", 1), ("o_w_in", 2), ("o_w_out", 1),
            ("mla_w_uq", 2), ("mla_w_ukv", 2), ("ssm_w_glu", 1))
_SHARDED_SMALL = (("mla_g_cq", 1), ("mla_g_ckv", 1))
_REPLICATED = ("g_norm1", "g_norm2", "e_g_q", "e_g_k", "ssm_lam_re", "ssm_lam_im", "ssm_log_dt", "ssm_b_re", "ssm_b_im",
               "ssm_c_re", "ssm_c_im", "ssm_d", "ssm_b_glu", "mla_g_q", "mla_g_k", "na_g_q", "na_g_k", "na_rpb")
_WEIGHTS = ("c_ctx", "w_mod", "b_mod", "g_norm1", "g_norm2", "w_ff1", "w_ff2", "e_w_in", "e_w_out", "e_g_q", "e_g_k",
            "ssm_lam_re", "ssm_lam_im", "ssm_log_dt", "ssm_b_re", "ssm_b_im", "ssm_c_re", "ssm_c_im", "ssm_d",
            "ssm_w_glu", "ssm_b_glu", "o_w_in", "o_w_out", "mla_g_cq", "mla_g_ckv", "mla_w_uq", "mla_w_ukv", "mla_g_q",
            "mla_g_k", "na_g_q", "na_g_k", "na_rpb")
_PACK_ROWS = 64


def _pack(arrs, dtype, cols=1024, row_mult=_PACK_ROWS):
    flat = jnp.concatenate([a.reshape(-1).astype(dtype) for a in arrs])
    unit = cols * row_mult
    pad = (-flat.shape[0]) % unit
    return jnp.pad(flat, (0, pad)).reshape(-1, cols)


def _unpack(flat, shapes):
    flat = flat.reshape(-1)
    out, off = [], 0
    for sh in shapes:
        n = int(np.prod(sh))
        out.append(flat[off:off + n].reshape(sh))
        off += n
    return out


def _silu(t):
    return t * jax.nn.sigmoid(t)


def kernel(x, c, ctx, c_ctx, w_mod, b_mod, g_norm1, g_norm2, w_ff1, w_ff2, e_w_in, e_w_out, e_g_q, e_g_k, ssm_lam_re, ssm_lam_im, ssm_log_dt, ssm_b_re, ssm_b_im, ssm_c_re, ssm_c_im, ssm_d, ssm_w_glu, ssm_b_glu, o_w_in, o_w_out, mla_g_cq, mla_g_ckv, mla_w_uq, mla_w_ukv, mla_g_q, mla_g_k, na_g_q, na_g_k, na_rpb, loss_target, m_c_ctx, m_w_mod, m_b_mod, m_g_norm1, m_g_norm2, m_w_ff1, m_w_ff2, m_e_w_in, m_e_w_out, m_e_g_q, m_e_g_k, m_ssm_lam_re, m_ssm_lam_im, m_ssm_log_dt, m_ssm_b_re, m_ssm_b_im, m_ssm_c_re, m_ssm_c_im, m_ssm_d, m_ssm_w_glu, m_ssm_b_glu, m_o_w_in, m_o_w_out, m_mla_g_cq, m_mla_g_ckv, m_mla_w_uq, m_mla_w_ukv, m_mla_g_q, m_mla_g_k, m_na_g_q, m_na_g_k, m_na_rpb, v_c_ctx, v_w_mod, v_b_mod, v_g_norm1, v_g_norm2, v_w_ff1, v_w_ff2, v_e_w_in, v_e_w_out, v_e_g_q, v_e_g_k, v_ssm_lam_re, v_ssm_lam_im, v_ssm_log_dt, v_ssm_b_re, v_ssm_b_im, v_ssm_c_re, v_ssm_c_im, v_ssm_d, v_ssm_w_glu, v_ssm_b_glu, v_o_w_in, v_o_w_out, v_mla_g_cq, v_mla_g_ckv, v_mla_w_uq, v_mla_w_ukv, v_mla_g_q, v_mla_g_k, v_na_g_q, v_na_g_k, v_na_rpb):
    env = dict(locals())
    weights = {n: env[n] for n in _WEIGHTS}
    mom_m = {n: env["m_" + n] for n in _WEIGHTS}
    mom_v = {n: env["v_" + n] for n in _WEIGHTS}
    ax, ay, ac = _me()
    plane = 2 * ax + ay
    dev = 4 * ax + 2 * ay + ac
    b_loc, d = c.shape
    depth = w_mod.shape[0]
    n_all = N_DEV * b_loc
    mod_cols = w_mod.shape[2]

    big = _pack([weights[n] for n, _ in _SHARDED], BF16)
    small = _pack([weights[n] for n, _ in _SHARDED_SMALL], F32, cols=LANE, row_mult=8)
    g_big, g_small = plane_allgather(big, small)
    full = {n: weights[n] for n in _REPLICATED}
    parts = [_unpack(g_big[j], [weights[n].shape for n, _ in _SHARDED]) for j in range(N_PLANE)]
    for t, (n, axis) in enumerate(_SHARDED):
        full[n] = jnp.concatenate([parts[j][t] for j in range(N_PLANE)], axis=axis).astype(F32)
    parts_s = [_unpack(g_small[j], [weights[n].shape for n, _ in _SHARDED_SMALL]) for j in range(N_PLANE)]
    for t, (n, axis) in enumerate(_SHARDED_SMALL):
        full[n] = jnp.concatenate([parts_s[j][t] for j in range(N_PLANE)], axis=axis)

    rows_pad = 8 * ((n_all + 1 + 7) // 8)
    c_all = allgather8(jnp.pad(c, ((0, 8 - b_loc), (0, 0)))).reshape(N_DEV, 8, d)[:, :b_loc].reshape(n_all, d)
    cond_raw = jnp.concatenate([c_all, c_ctx[None], jnp.zeros((rows_pad - n_all - 1, d), F32)], axis=0)
    b_cols = lax.dynamic_slice_in_dim(b_mod, plane * mod_cols, mod_cols, axis=1)
    mod_loc = jnp.stack([_mm(cond_raw, w_mod[i], a_act="silu") + b_cols[i][None] for i in range(depth)])
    mod_g = allgather8(mod_loc.reshape(depth * rows_pad, mod_cols)).reshape(N_PLANE, 2, depth, rows_pad, mod_cols)
    mod_all = jnp.concatenate([mod_g[j, 0] for j in range(N_PLANE)], axis=-1)
    m_lat = lax.dynamic_slice_in_dim(mod_all, dev * b_loc, b_loc, axis=1)
    m_ctx = jnp.broadcast_to(mod_all[:, n_all][:, None], m_lat.shape)
    mods = jnp.stack([m_ctx, m_lat], axis=2).reshape(depth, b_loc, 2, N_MOD, d)

    loss_part, grad_x, dmods, dw = local_step(x, ctx, mods, full, loss_target)
    loss = lax.psum(loss_part, ("x", "y", "c"))

    dm = dmods.reshape(depth, b_loc, 2, N_MOD * d)
    dm_rows = jnp.concatenate([dm[:, :, 1], jnp.sum(dm[:, :, 0], axis=1, keepdims=True)], axis=1)
    rep_shapes = [weights[n].shape for n in _REPLICATED]
    small_pack = _pack([dm_rows] + [dw[n] for n in _REPLICATED], F32, cols=1024, row_mult=8)
    sp_rows = small_pack.shape[0]
    gathered = allgather8(small_pack).reshape(N_DEV, sp_rows, 1024)
    n_dm = depth * (b_loc + 1) * N_MOD * d
    dm_all = gathered.reshape(N_DEV, -1)[:, :n_dm].reshape(N_DEV, depth, b_loc + 1, N_MOD * d)
    rep_sum = _sum_rows(gathered, N_DEV).reshape(-1)
    rep_grads = dict(zip(_REPLICATED, _unpack(rep_sum[n_dm:], rep_shapes)))
    d_ctx_row = rep_sum[:n_dm].reshape(depth, b_loc + 1, N_MOD * d)[:, b_loc]
    d_lat_rows = jnp.transpose(dm_all[:, :, :b_loc], (1, 0, 2, 3)).reshape(depth, n_all, N_MOD * d)
    d_mod_all = jnp.concatenate([d_lat_rows, d_ctx_row[:, None],
                                 jnp.zeros((depth, rows_pad - n_all - 1, N_MOD * d), F32)], axis=1)
    grads = dict(rep_grads)
    grads["b_mod"] = jnp.sum(d_mod_all, axis=1)
    d_cols = lax.dynamic_slice_in_dim(d_mod_all, plane * mod_cols, mod_cols, axis=2)
    grads["w_mod"] = jnp.stack([_mm(cond_raw, d_cols[i], ta=True, a_act="silu") for i in range(depth)])
    d_cond = _mm(d_cols[0], w_mod[0], tb=True)
    for i in range(1, depth):
        d_cond = _add2(d_cond, _mm(d_cols[i], w_mod[i], tb=True))
    d_cond_g = allgather8(d_cond[n_all:n_all + 8] if rows_pad - n_all >= 8 else
                          jnp.pad(d_cond[n_all:], ((0, 8 - (rows_pad - n_all)), (0, 0)))).reshape(N_PLANE, 2, 8, d)
    d_silu = _sum_rows(d_cond_g[:, 0], N_PLANE)[0]
    sg = jax.nn.sigmoid(c_ctx)
    grads["c_ctx"] = d_silu * (sg * (1.0 + c_ctx * (1.0 - sg)))

    def shard_of(g, axis, j):
        n = g.shape[axis] // N_PLANE
        return lax.slice_in_dim(g, j * n, (j + 1) * n, axis=axis)

    send = jnp.stack([_pack([shard_of(dw[n], axis, j) for n, axis in _SHARDED]
                            + [shard_of(dw[n], axis, j) for n, axis in _SHARDED_SMALL], BF16) for j in range(N_PLANE)])
    own = lax.dynamic_index_in_dim(send, plane, 0, keepdims=False)
    plane_sum = _accumulate([own, plane_scatter(send)], BF16)
    flat = _accumulate([plane_sum, sibling_swap(plane_sum)], F32).reshape(-1)
    shard_shapes = [weights[n].shape for n, _ in _SHARDED] + [weights[n].shape for n, _ in _SHARDED_SMALL]
    for (n, _), g in zip(_SHARDED + _SHARDED_SMALL, _unpack(flat, shard_shapes)):
        grads[n] = g

    big_names = ("w_mod",) + tuple(n for n, _ in _SHARDED)
    small_names = tuple(n for n in _WEIGHTS if n not in big_names)
    delta, new_m, new_v = {}, {}, {}
    for n in big_names:
        delta[n], new_m[n], new_v[n] = _adamw(weights[n], grads[n], mom_m[n], mom_v[n])
    sm_shapes = [weights[n].shape for n in small_names]
    packed = [_pack([src[n] for n in small_names], F32, cols=1024, row_mult=8)
              for src in (weights, grads, mom_m, mom_v)]
    for dst, res in zip((delta, new_m, new_v), _adamw(*packed)):
        dst.update(dict(zip(small_names, _unpack(res, sm_shapes))))

    return (loss, grad_x, *[grads[n] for n in _WEIGHTS], *[delta[n] for n in _WEIGHTS],
            *[new_m[n] for n in _WEIGHTS], *[new_v[n] for n in _WEIGHTS])
```

```python
import functools
import math

import numpy as np
import jax
import jax.numpy as jnp
from jax import lax
from jax.experimental import pallas as pl
from jax.experimental.pallas import tpu as pltpu

F32 = jnp.float32
BF16 = jnp.bfloat16
HI = lax.Precision.HIGHEST
MESH = pl.DeviceIdType.MESH
ANY = pl.BlockSpec(memory_space=pl.ANY)
VMEM_SPEC = pl.BlockSpec(memory_space=pltpu.VMEM)

GRID_W = 64
HEAD_DIM = 64
ROPE_BASE = 10000.0
EPS = 1e-6
N_MOD = 6
GQA_Q_HEADS, GQA_KV_HEADS = 12, 4
GQA_Q_W, GQA_KV_W = GQA_Q_HEADS * HEAD_DIM, GQA_KV_HEADS * HEAD_DIM
SSM_WIDTH, SSM_GROUP, SSM_STATE = 256, 16, 64
SSM_GROUPS = SSM_WIDTH // SSM_GROUP
SSM_LANES = SSM_GROUPS * SSM_STATE
MLA_HEADS, MLA_Q_RANK, MLA_KV_RANK, MLA_NOPE, MLA_ROPE, MLA_V = 8, 512, 256, 64, 32, 64
MLA_QK = MLA_NOPE + MLA_ROPE
NA_HEADS, NA_WIN_R, NA_WIN_C = 8, 8, 16
NA_W = NA_HEADS * HEAD_DIM
NA_BAND = NA_WIN_R * GRID_W
ODD_IN_W = MLA_Q_RANK + MLA_KV_RANK + MLA_ROPE + 3 * NA_W
ODD_IN_PAD = 2560
ADAM_LR, ADAM_B1, ADAM_B2, ADAM_EPS, ADAM_WD, ADAM_STEP = 0.001, 0.9, 0.999, 1e-08, 0.01, 10
NEG = -1e30
VMEM_LIMIT = 56 * 1024 * 1024
LANE = 128
MM_TILE_M = (1152, 1024, 768, 512, 256, 128)
MM_TILE_N = (1280, 1024, 768, 512, 256, 128)
MM_TILE_K = (1152, 1024, 768, 512, 256, 128)
N_PLANE = 4
N_DEV = 8


def _pick(n, cands):
    for c in cands:
        if n % c == 0:
            return c
    return n


def _params(**kw):
    return pltpu.CompilerParams(vmem_limit_bytes=VMEM_LIMIT, **kw)


def _mm(a, b, *, ta=False, tb=False, a_act=None, epi=None, e=None, exact=False):
    m, kd = (a.shape[1], a.shape[0]) if ta else a.shape
    n = b.shape[0] if tb else b.shape[1]
    tm = _pick(m, MM_TILE_M)
    tn = _pick(n, MM_TILE_N)
    tk = _pick(kd, MM_TILE_K)
    nk = kd // tk
    dn = (((0 if ta else 1,), (1 if tb else 0,)), ((), ()))

    def body(*refs):
        if epi is None:
            a_ref, b_ref, o_ref = refs
        else:
            a_ref, b_ref, e_ref, o_ref = refs
        k = pl.program_id(2)
        av = a_ref[...]
        if a_act == "relu2":
            av = jnp.square(jnp.maximum(av, 0.0))
        elif a_act == "silu":
            av = av * jax.nn.sigmoid(av)
        bv = b_ref[...]
        if exact:
            p = lax.dot_general(av, bv, dn, precision=HI, preferred_element_type=F32)
        else:
            p = lax.dot_general(av.astype(BF16), bv.astype(BF16), dn, preferred_element_type=F32)

        @pl.when(k == 0)
        def _():
            o_ref[...] = p

        @pl.when(k > 0)
        def _():
            o_ref[...] += p

        if epi == "drelu2":
            @pl.when(k == nk - 1)
            def _():
                o_ref[...] = o_ref[...] * (2.0 * jnp.maximum(e_ref[...], 0.0))

    a_spec = pl.BlockSpec((tk, tm), lambda i, j, k: (k, i)) if ta else pl.BlockSpec((tm, tk), lambda i, j, k: (i, k))
    b_spec = pl.BlockSpec((tn, tk), lambda i, j, k: (j, k)) if tb else pl.BlockSpec((tk, tn), lambda i, j, k: (k, j))
    o_spec = pl.BlockSpec((tm, tn), lambda i, j, k: (i, j))
    ins, specs = [a, b], [a_spec, b_spec]
    if epi is not None:
        ins.append(e)
        specs.append(o_spec)
    name = f"mm_{m}x{kd}x{n}_{int(ta)}{int(tb)}_{a_act}_{epi}_{int(exact)}"
    return pl.pallas_call(
        body, out_shape=jax.ShapeDtypeStruct((m, n), F32), grid=(m // tm, n // tn, nk),
        in_specs=specs, out_specs=o_spec, name=name, compiler_params=_params(),
    )(*ins)


@functools.partial(jax.custom_vjp, nondiff_argnums=(2,))
def _linear(a, w, exact):
    return _mm(a, w, exact=exact)


def _linear_fwd(a, w, exact):
    return _mm(a, w, exact=exact), (a, w)


def _linear_bwd(exact, res, g):
    a, w = res
    return _mm(g, w, tb=True, exact=exact), _mm(a, g, ta=True, exact=exact)


_linear.defvjp(_linear_fwd, _linear_bwd)


def linear(a, w, exact=False):
    return _linear(a, w, exact)


@jax.custom_vjp
def ffn(a, w1, w2):
    return _mm(_mm(a, w1), w2, a_act="relu2")


def _ffn_fwd(a, w1, w2):
    h1 = _mm(a, w1)
    return _mm(h1, w2, a_act="relu2"), (a, w1, w2, h1)


def _ffn_bwd(res, g):
    a, w1, w2, h1 = res
    dh1 = _mm(g, w2, tb=True, epi="drelu2", e=h1)
    dw2 = _mm(h1, g, ta=True, a_act="relu2")
    return _mm(dh1, w1, tb=True), _mm(a, dh1, ta=True), dw2


ffn.defvjp(_ffn_fwd, _ffn_bwd)


def make_rowwise(fn, name, kinds, out_dims, nctx_rows=0, whole_seq=False):
    n_in = len(kinds)
    n_out = len(out_dims)
    diff = [i for i, kd in enumerate(kinds) if kd in ("row", "glob", "seg")]

    def layout(args):
        row0 = args[kinds.index("row")]
        g, s = row0.shape[0], row0.shape[1]
        ts = s if whole_seq else (min(256, nctx_rows) if nctx_rows else _pick(s, (256, 128, 64)))
        nctx = nctx_rows // ts
        return g, s, ts, nctx

    def spec_of(kind, arr, ts, nctx):
        if kind == "row":
            return pl.BlockSpec((None, ts, arr.shape[2]), lambda g, i: (g, i, 0))
        if kind == "tab":
            return pl.BlockSpec((ts, arr.shape[1]), lambda g, i: (i, 0))
        if kind in ("const", "glob"):
            return pl.BlockSpec(arr.shape, lambda g, i: (0, 0))
        return pl.BlockSpec((None, None) + arr.shape[2:], lambda g, i: (g, (i >= nctx).astype(jnp.int32), 0, 0))

    def fwd_call(*args):
        g, s, ts, nctx = layout(args)

        def body(*refs):
            vals = [r[...] for r in refs[:n_in]]
            outs = fn(*vals)
            for o_ref, o in zip(refs[n_in:], outs):
                o_ref[...] = o

        return pl.pallas_call(
            body, out_shape=[jax.ShapeDtypeStruct((g, s, d), F32) for d in out_dims], grid=(g, s // ts),
            in_specs=[spec_of(kd, a, ts, nctx) for kd, a in zip(kinds, args)],
            out_specs=[pl.BlockSpec((None, ts, d), lambda g_, i: (g_, i, 0)) for d in out_dims],
            name=f"{name}_f_{g}x{s}", compiler_params=_params(),
        )(*args)

    def bwd_call(args, cts):
        g, s, ts, nctx = layout(args)

        def body(*refs):
            in_refs, ct_refs, out_refs = refs[:n_in], refs[n_in:n_in + n_out], refs[n_in + n_out:]
            gi, i = pl.program_id(0), pl.program_id(1)
            vals = [r[...] for r in in_refs]

            def f(*dv):
                full = list(vals)
                for idx, v in zip(diff, dv):
                    full[idx] = v
                return tuple(fn(*full))

            _, vjp = jax.vjp(f, *[vals[idx] for idx in diff])
            grads = vjp(tuple(r[...] for r in ct_refs))
            for idx, o_ref, gr in zip(diff, out_refs, grads):
                if kinds[idx] == "row":
                    o_ref[...] = gr
                    continue
                if kinds[idx] == "glob":
                    first = jnp.logical_and(gi == 0, i == 0)
                else:
                    first = jnp.logical_or(i == 0, i == nctx)

                @pl.when(first)
                def _(o_ref=o_ref, gr=gr):
                    o_ref[...] = gr

                @pl.when(jnp.logical_not(first))
                def _(o_ref=o_ref, gr=gr):
                    o_ref[...] += gr

        in_specs = [spec_of(kd, a, ts, nctx) for kd, a in zip(kinds, args)]
        in_specs += [pl.BlockSpec((None, ts, d), lambda g_, i: (g_, i, 0)) for d in out_dims]
        return pl.pallas_call(
            body, out_shape=[jax.ShapeDtypeStruct(args[idx].shape, F32) for idx in diff], grid=(g, s // ts),
            in_specs=in_specs, out_specs=[spec_of(kinds[idx], args[idx], ts, nctx) for idx in diff],
            name=f"{name}_b_{g}x{s}", compiler_params=_params(),
        )(*args, *cts)

    @jax.custom_vjp
    def op(*args):
        return tuple(fwd_call(*args))

    def op_fwd(*args):
        return tuple(fwd_call(*args)), args

    def op_bwd(args, cts):
        grads = bwd_call(args, cts)
        full = [None] * n_in
        for idx, gr in zip(diff, grads):
            full[idx] = gr
        return tuple(jnp.zeros_like(a) if gfull is None else gfull for a, gfull in zip(args, full))

    op.defvjp(op_fwd, op_bwd)
    op.fwd_call, op.bwd_call = fwd_call, bwd_call
    return op


def make_gated_add(d, n_ctx):
    add = make_rowwise(_fn_gated_add, "gated", ("row", "row", "seg"), (d,), nctx_rows=n_ctx)
    mul = make_rowwise(_fn_gate_mul, "gate_mul", ("row", "seg"), (d,), nctx_rows=n_ctx)

    @jax.custom_vjp
    def op(x, o, gate):
        return add.fwd_call(x, o, gate)[0]

    def fwd(x, o, gate):
        return add.fwd_call(x, o, gate)[0], (o, gate)

    def bwd(res, ct):
        do, dgate = mul.bwd_call(res, (ct,))
        return ct, do, dgate

    op.defvjp(fwd, bwd)
    return op


def _rms(x):
    return lax.rsqrt(jnp.mean(x * x, axis=-1, keepdims=True) + EPS)


def _fn_modulate(x, g, shift, scale):
    return ((x * _rms(x) * g) * (1.0 + scale) + shift,)


def _fn_gated_add(x, o, gate):
    return (x + gate * o,)


def _fn_gate_mul(o, gate):
    return (gate * o,)


def _fn_norm(x, g):
    return (x * _rms(x) * g,)


def _fn_norm_rope(x, cos, sin, rot, g):
    y = x * _rms(x) * g
    r = jnp.dot(y, rot, precision=HI, preferred_element_type=F32)
    return (y * cos + r * sin,)


def _fn_glu_pre(u, y0, y1, d):
    return (jax.nn.gelu(d * u + y0 + y1),)


def _fn_glu_post(z, t, bg):
    return (z * jax.nn.sigmoid(t + bg),)


def _rope_matrix(dh, start, rot_dim):
    r = np.zeros((dh, dh), np.float32)
    q = rot_dim // 4
    for j in range(rot_dim):
        if (j // q) % 2 == 0:
            r[start + j + q, start + j] = -1.0
        else:
            r[start + j - q, start + j] = 1.0
    return r


def _rope_tables(n_ctx, n_lat, dh, start, rot_dim):
    t = jnp.arange(n_lat)
    rows = (t // GRID_W).astype(F32)
    cols = (t % GRID_W).astype(F32)
    axis_dim = rot_dim // 2
    freqs = ROPE_BASE ** (-jnp.arange(0, axis_dim, 2, dtype=F32) / axis_dim)
    ang_r = rows[:, None] * freqs
    ang_c = cols[:, None] * freqs
    ang = jnp.concatenate([ang_r, ang_r, ang_c, ang_c], axis=-1)
    cos = jnp.concatenate([jnp.ones((n_lat, start), F32), jnp.cos(ang)], axis=-1)
    sin = jnp.concatenate([jnp.zeros((n_lat, start), F32), jnp.sin(ang)], axis=-1)
    cos = jnp.concatenate([jnp.ones((n_ctx, dh), F32), cos], axis=0)
    sin = jnp.concatenate([jnp.zeros((n_ctx, dh), F32), sin], axis=0)
    return cos, sin


_NT = (((1,), (1,)), ((), ()))
_TN = (((0,), (0,)), ((), ()))


def _attn_fwd(q, k, v, group, n_ctx, scale):
    b, h, s, dq = q.shape
    dv = v.shape[-1]
    tq = min(256, n_ctx)
    nc = n_ctx // tq

    def body(q_ref, k_ref, v_ref, o_ref, lse_ref):
        qv = (q_ref[...] * scale).astype(BF16)

        def run(n_keys):
            sc = lax.dot_general(qv, k_ref[0:n_keys, :].astype(BF16), _NT, preferred_element_type=F32)
            m = jnp.max(sc, axis=-1, keepdims=True)
            p = jnp.exp(sc - m)
            l = jnp.sum(p, axis=-1, keepdims=True)
            o = jnp.dot(p.astype(BF16), v_ref[0:n_keys, :].astype(BF16), preferred_element_type=F32)
            o_ref[...] = o / l
            lse_ref[...] = m + jnp.log(l)

        pl.when(pl.program_id(2) < nc)(lambda: run(n_ctx))
        pl.when(pl.program_id(2) >= nc)(lambda: run(s))

    return pl.pallas_call(
        body, out_shape=[jax.ShapeDtypeStruct((b, h, s, dv), F32), jax.ShapeDtypeStruct((b, h, s, 1), F32)],
        grid=(b, h, s // tq),
        in_specs=[pl.BlockSpec((None, None, tq, dq), lambda bi, hi, i: (bi, hi, i, 0)),
                  pl.BlockSpec((None, None, s, dq), lambda bi, hi, i: (bi, lax.div(hi, group), 0, 0)),
                  pl.BlockSpec((None, None, s, dv), lambda bi, hi, i: (bi, lax.div(hi, group), 0, 0))],
        out_specs=[pl.BlockSpec((None, None, tq, dv), lambda bi, hi, i: (bi, hi, i, 0)),
                   pl.BlockSpec((None, None, tq, 1), lambda bi, hi, i: (bi, hi, i, 0))],
        name=f"attn_f_{h}x{s}x{dq}", compiler_params=_params(),
    )(q, k, v)


def _attn_dq(q, k, v, o, lse, do, group, n_ctx, scale):
    b, h, s, dq = q.shape
    dv = v.shape[-1]
    tq = min(256, n_ctx)
    nc = n_ctx // tq

    def body(q_ref, k_ref, v_ref, o_ref, lse_ref, do_ref, dq_ref, delta_ref):
        qv = (q_ref[...] * scale).astype(BF16)
        dov = do_ref[...]
        delta = jnp.sum(dov * o_ref[...], axis=-1, keepdims=True)
        delta_ref[...] = delta

        def run(n_keys):
            kv = k_ref[0:n_keys, :].astype(BF16)
            sc = lax.dot_general(qv, kv, _NT, preferred_element_type=F32)
            p = jnp.exp(sc - lse_ref[...])
            dp = lax.dot_general(dov.astype(BF16), v_ref[0:n_keys, :].astype(BF16), _NT, preferred_element_type=F32)
            ds = p * (dp - delta)
            dq_ref[...] = jnp.dot(ds.astype(BF16), kv, preferred_element_type=F32) * scale

        pl.when(pl.program_id(2) < nc)(lambda: run(n_ctx))
        pl.when(pl.program_id(2) >= nc)(lambda: run(s))

    qs = lambda d: pl.BlockSpec((None, None, tq, d), lambda bi, hi, i: (bi, hi, i, 0))
    ks = lambda d: pl.BlockSpec((None, None, s, d), lambda bi, hi, i: (bi, lax.div(hi, group), 0, 0))
    return pl.pallas_call(
        body, out_shape=[jax.ShapeDtypeStruct((b, h, s, dq), F32), jax.ShapeDtypeStruct((b, h, s, 1), F32)],
        grid=(b, h, s // tq),
        in_specs=[qs(dq), ks(dq), ks(dv), qs(dv), qs(1), qs(dv)], out_specs=[qs(dq), qs(1)],
        name=f"attn_dq_{h}x{s}x{dq}", compiler_params=_params(),
    )(q, k, v, o, lse, do)


def _attn_dkv(q, k, v, lse, delta, do, group, n_ctx, scale):
    b, h, s, dq = q.shape
    hk = k.shape[1]
    dv = v.shape[-1]
    tk = min(256, n_ctx)
    nc = n_ctx // tk

    def body(q_ref, k_ref, v_ref, lse_ref, delta_ref, do_ref, dk_ref, dv_ref):
        kv = k_ref[...].astype(BF16)
        vv = v_ref[...].astype(BF16)

        def run(r0):
            dk = jnp.zeros((tk, dq), F32)
            dvv = jnp.zeros((tk, dv), F32)
            for g in range(group):
                qg = (q_ref[g, r0:s, :] * scale).astype(BF16)
                dog = do_ref[g, r0:s, :].astype(BF16)
                sc = lax.dot_general(qg, kv, _NT, preferred_element_type=F32)
                p = jnp.exp(sc - lse_ref[g, r0:s, :])
                dvv = dvv + lax.dot_general(p.astype(BF16), dog, _TN, preferred_element_type=F32)
                dp = lax.dot_general(dog, vv, _NT, preferred_element_type=F32)
                ds = p * (dp - delta_ref[g, r0:s, :])
                dk = dk + lax.dot_general(ds.astype(BF16), qg, _TN, preferred_element_type=F32)
            dk_ref[...] = dk
            dv_ref[...] = dvv

        pl.when(pl.program_id(2) < nc)(lambda: run(0))
        pl.when(pl.program_id(2) >= nc)(lambda: run(n_ctx))

    gs = lambda d: pl.BlockSpec((None, group, s, d), lambda bi, hi, j: (bi, hi, 0, 0))
    ks = lambda d: pl.BlockSpec((None, None, tk, d), lambda bi, hi, j: (bi, hi, j, 0))
    return pl.pallas_call(
        body, out_shape=[jax.ShapeDtypeStruct((b, hk, s, dq), F32), jax.ShapeDtypeStruct((b, hk, s, dv), F32)],
        grid=(b, hk, s // tk),
        in_specs=[gs(dq), ks(dq), ks(dv), gs(1), gs(1), gs(dv)], out_specs=[ks(dq), ks(dv)],
        name=f"attn_dkv_{h}x{s}x{dq}", compiler_params=_params(),
    )(q, k, v, lse, delta, do)


@functools.partial(jax.custom_vjp, nondiff_argnums=(3, 4, 5))
def attention(q, k, v, group, n_ctx, scale):
    return _attn_fwd(q, k, v, group, n_ctx, scale)[0]


def _attention_fwd(q, k, v, group, n_ctx, scale):
    o, lse = _attn_fwd(q, k, v, group, n_ctx, scale)
    return o, (q, k, v, o, lse)


def _attention_bwd(group, n_ctx, scale, res, do):
    q, k, v, o, lse = res
    dq, delta = _attn_dq(q, k, v, o, lse, do, group, n_ctx, scale)
    dk, dv = _attn_dkv(q, k, v, lse, delta, do, group, n_ctx, scale)
    return dq, dk, dv


attention.defvjp(_attention_fwd, _attention_bwd)


def _na_geometry(i, nc, rows):
    r = i - nc
    rs = jnp.clip(r - NA_WIN_R // 2, 0, rows - NA_WIN_R)
    is_ctx = i < nc
    cls = jnp.where(is_ctx, NA_WIN_R, r - rs)
    return jnp.where(is_ctx, 0, rs), cls


def _na_scores(q_ref, k_ref, bias_ref, hd, n_ctx, start, scale):
    qv = (q_ref[hd] * scale).astype(BF16)
    kc = k_ref[hd, 0:n_ctx, :].astype(BF16)
    kb = k_ref[hd, pl.ds(start, NA_BAND), :].astype(BF16)
    s_c = lax.dot_general(qv, kc, _NT, preferred_element_type=F32)
    s_l = lax.dot_general(qv, kb, _NT, preferred_element_type=F32) + bias_ref[hd]
    return qv, kc, kb, s_c, s_l


NA_HEADS_FWD = 4
NA_HEADS_BWD = 2


def _na_specs(hp, s, dh, nc, rows):
    qs = lambda d: pl.BlockSpec((None, hp, GRID_W, d), lambda bi, hg, i: (bi, hg, i, 0))
    ks = pl.BlockSpec((None, hp, s, dh), lambda bi, hg, i: (bi, hg, 0, 0))
    bs = pl.BlockSpec((hp, None, GRID_W, NA_BAND), lambda bi, hg, i: (hg, _na_geometry(i, nc, rows)[1], 0, 0))
    return qs, ks, bs


def _na_fwd(q, k, v, bias, n_ctx):
    b, h, s, dh = q.shape
    nc = n_ctx // GRID_W
    rows = (s - n_ctx) // GRID_W
    scale = dh ** -0.5
    hp = math.gcd(h, NA_HEADS_FWD)

    def body(q_ref, k_ref, v_ref, bias_ref, o_ref, lse_ref):
        rs, _ = _na_geometry(pl.program_id(2), nc, rows)
        start = pl.multiple_of(n_ctx + rs * GRID_W, GRID_W)
        for hd in range(hp):
            _, _, _, s_c, s_l = _na_scores(q_ref, k_ref, bias_ref, hd, n_ctx, start, scale)
            m = jnp.maximum(jnp.max(s_c, axis=-1, keepdims=True), jnp.max(s_l, axis=-1, keepdims=True))
            p_c = jnp.exp(s_c - m)
            p_l = jnp.exp(s_l - m)
            l = jnp.sum(p_c, axis=-1, keepdims=True) + jnp.sum(p_l, axis=-1, keepdims=True)
            o = jnp.dot(p_c.astype(BF16), v_ref[hd, 0:n_ctx, :].astype(BF16), preferred_element_type=F32)
            o = o + jnp.dot(p_l.astype(BF16), v_ref[hd, pl.ds(start, NA_BAND), :].astype(BF16),
                            preferred_element_type=F32)
            o_ref[hd] = o / l
            lse_ref[hd] = m + jnp.log(l)

    qs, ks, bs = _na_specs(hp, s, dh, nc, rows)
    return pl.pallas_call(
        body, out_shape=[jax.ShapeDtypeStruct((b, h, s, dh), F32), jax.ShapeDtypeStruct((b, h, s, 1), F32)],
        grid=(b, h // hp, s // GRID_W), in_specs=[qs(dh), ks, ks, bs], out_specs=[qs(dh), qs(1)],
        name=f"na_f_{s}", compiler_params=_params(),
    )(q, k, v, bias)


def _na_bwd(q, k, v, bias, o, lse, do, n_ctx):
    b, h, s, dh = q.shape
    nc = n_ctx // GRID_W
    rows = (s - n_ctx) // GRID_W
    scale = dh ** -0.5
    n_cls = NA_WIN_R + 1
    hp = math.gcd(h, NA_HEADS_BWD)

    def body(q_ref, k_ref, v_ref, bias_ref, o_ref, lse_ref, do_ref, dq_ref, dk_ref, dv_ref, db_ref):
        i = pl.program_id(2)
        rs, cls = _na_geometry(i, nc, rows)
        _, cls_prev = _na_geometry(i - 1, nc, rows)
        start = pl.multiple_of(n_ctx + rs * GRID_W, GRID_W)
        first = jnp.logical_or(i == 0, cls != cls_prev)

        @pl.when(i == 0)
        def _():
            dk_ref[...] = jnp.zeros_like(dk_ref)
            dv_ref[...] = jnp.zeros_like(dv_ref)

        for hd in range(hp):
            qv, kc, kb, s_c, s_l = _na_scores(q_ref, k_ref, bias_ref, hd, n_ctx, start, scale)
            lse_v = lse_ref[hd]
            p_c = jnp.exp(s_c - lse_v)
            p_l = jnp.exp(s_l - lse_v)
            dov = do_ref[hd]
            dob = dov.astype(BF16)
            delta = jnp.sum(dov * o_ref[hd], axis=-1, keepdims=True)
            vc = v_ref[hd, 0:n_ctx, :].astype(BF16)
            vb = v_ref[hd, pl.ds(start, NA_BAND), :].astype(BF16)
            ds_c = p_c * (lax.dot_general(dob, vc, _NT, preferred_element_type=F32) - delta)
            ds_l = p_l * (lax.dot_general(dob, vb, _NT, preferred_element_type=F32) - delta)
            dsc_b = ds_c.astype(BF16)
            dsl_b = ds_l.astype(BF16)
            dq_ref[hd] = (jnp.dot(dsc_b, kc, preferred_element_type=F32)
                          + jnp.dot(dsl_b, kb, preferred_element_type=F32)) * scale
            dk_ref[hd, 0:n_ctx, :] += lax.dot_general(dsc_b, qv, _TN, preferred_element_type=F32)
            dk_ref[hd, pl.ds(start, NA_BAND), :] += lax.dot_general(dsl_b, qv, _TN, preferred_element_type=F32)
            dv_ref[hd, 0:n_ctx, :] += lax.dot_general(p_c.astype(BF16), dob, _TN, preferred_element_type=F32)
            dv_ref[hd, pl.ds(start, NA_BAND), :] += lax.dot_general(p_l.astype(BF16), dob, _TN, preferred_element_type=F32)

            @pl.when(first)
            def _(hd=hd, ds_l=ds_l):
                db_ref[hd] = ds_l

            @pl.when(jnp.logical_not(first))
            def _(hd=hd, ds_l=ds_l):
                db_ref[hd] += ds_l

    qs, ks, bs = _na_specs(hp, s, dh, nc, rows)
    dbs = pl.BlockSpec((None, hp, None, GRID_W, NA_BAND),
                       lambda bi, hg, i: (bi, hg, _na_geometry(i, nc, rows)[1], 0, 0))
    return pl.pallas_call(
        body,
        out_shape=[jax.ShapeDtypeStruct((b, h, s, dh), F32), jax.ShapeDtypeStruct((b, h, s, dh), F32),
                   jax.ShapeDtypeStruct((b, h, s, dh), F32), jax.ShapeDtypeStruct((b, h, n_cls, GRID_W, NA_BAND), F32)],
        grid=(b, h // hp, s // GRID_W), in_specs=[qs(dh), ks, ks, bs, qs(dh), qs(1), qs(dh)],
        out_specs=[qs(dh), ks, ks, dbs], name=f"na_b_{s}", compiler_params=_params(),
    )(q, k, v, bias, o, lse, do)


@functools.partial(jax.custom_vjp, nondiff_argnums=(4,))
def na_attention(q, k, v, bias, n_ctx):
    return _na_fwd(q, k, v, bias, n_ctx)[0]


def _na_attention_fwd(q, k, v, bias, n_ctx):
    o, lse = _na_fwd(q, k, v, bias, n_ctx)
    return o, (q, k, v, bias, o, lse)


def _na_attention_bwd(n_ctx, res, do):
    q, k, v, bias, o, lse = res
    dq, dk, dv, db = _na_bwd(q, k, v, bias, o, lse, do, n_ctx)
    return dq, dk, dv, jnp.sum(db, axis=0)


na_attention.defvjp(_na_attention_fwd, _na_attention_bwd)


def _na_onehots():
    q = np.arange(GRID_W)[:, None]
    col = np.arange(GRID_W)[None, :]
    cs = np.clip(q - NA_WIN_C // 2, 0, GRID_W - NA_WIN_C)
    valid = (col >= cs) & (col < cs + NA_WIN_C)
    cidx = col - q + (NA_WIN_C - 1)
    n_b = 2 * NA_WIN_C - 1
    col_hot = np.zeros((LANE, GRID_W * GRID_W), np.float32)
    for qq in range(GRID_W):
        for cc in range(GRID_W):
            if valid[qq, cc]:
                col_hot[cidx[qq, cc], qq * GRID_W + cc] = 1.0
    row_hot = np.zeros((NA_WIN_R, NA_WIN_R, 2 * NA_WIN_R - 1), np.float32)
    for c in range(NA_WIN_R):
        for j in range(NA_WIN_R):
            row_hot[c, j, j - c + NA_WIN_R - 1] = 1.0
    mask = np.where(valid, 0.0, NEG).astype(np.float32)
    return col_hot, row_hot, mask, n_b


def na_bias_table(rpb):
    h = rpb.shape[0]
    col_hot, row_hot, mask, n_b = _na_onehots()
    t1 = jnp.einsum("cja,hab->hcjb", jnp.asarray(row_hot), rpb)
    t1 = jnp.pad(t1.reshape(h * NA_WIN_R * NA_WIN_R, n_b), ((0, 0), (0, LANE - n_b)))
    t2 = linear(t1, jnp.asarray(col_hot), True)
    t2 = t2.reshape(h, NA_WIN_R, NA_WIN_R, GRID_W, GRID_W) + jnp.asarray(mask)
    tab = jnp.transpose(t2, (0, 1, 3, 2, 4)).reshape(h, NA_WIN_R, GRID_W, NA_BAND)
    return jnp.concatenate([tab, jnp.full((h, 1, GRID_W, NA_BAND), NEG, F32)], axis=1)


def _first_step():
    return jnp.logical_and(pl.program_id(0) == 0, pl.program_id(1) == 0)


def _accum_out(ref, val, first):
    @pl.when(first)
    def _():
        ref[...] = val

    @pl.when(jnp.logical_not(first))
    def _():
        ref[...] += val


def _norm_rope_head(xh, g, cos, sin, rot):
    r = _rms(xh)
    yn = xh * r
    y = yn * g
    if cos is not None:
        y = y * cos + jnp.dot(y, rot, precision=HI, preferred_element_type=F32) * sin
    return y, yn, r


def _norm_rope_head_bwd(dy, yn, r, g, cos, sin, rot):
    if cos is not None:
        dy = dy * cos + lax.dot_general(dy * sin, rot, _NT, precision=HI, preferred_element_type=F32)
    dg = jnp.sum(dy * yn, axis=0, keepdims=True)
    dyn = dy * g
    return r * (dyn - yn * jnp.mean(dyn * yn, axis=-1, keepdims=True)), dg


def _hnr_call(x, g, cos, sin, rot, n_heads, dy=None):
    b, s, w = x.shape
    dh = w // n_heads
    ts = _pick(s, (256, 128, 64))
    rope = cos is not None

    def body(*refs):
        refs = list(refs)
        x_ref, g_ref = refs[0], refs[1]
        cos_v = sin_v = rot_v = None
        k = 2
        if rope:
            cos_v, sin_v, rot_v = refs[2][...], refs[3][...], refs[4][...]
            k = 5
        gv = g_ref[...]
        if dy is None:
            o_ref = refs[k]
            for h in range(n_heads):
                sl = slice(h * dh, (h + 1) * dh)
                o_ref[:, sl] = _norm_rope_head(x_ref[:, sl], gv, cos_v, sin_v, rot_v)[0]
            return
        dy_ref, dx_ref, dg_ref = refs[k], refs[k + 1], refs[k + 2]
        dg = jnp.zeros((1, dh), F32)
        for h in range(n_heads):
            sl = slice(h * dh, (h + 1) * dh)
            _, yn, r = _norm_rope_head(x_ref[:, sl], gv, None, None, None)
            dxh, dgh = _norm_rope_head_bwd(dy_ref[:, sl], yn, r, gv, cos_v, sin_v, rot_v)
            dx_ref[:, sl] = dxh
            dg = dg + dgh
        _accum_out(dg_ref, dg, _first_step())

    row = pl.BlockSpec((None, ts, w), lambda bi, i: (bi, i, 0))
    whole = lambda a: pl.BlockSpec(a.shape, lambda bi, i: (0, 0))
    ins, specs = [x, g], [row, whole(g)]
    if rope:
        ins += [cos, sin, rot]
        specs += [pl.BlockSpec((ts, dh), lambda bi, i: (i, 0)), pl.BlockSpec((ts, dh), lambda bi, i: (i, 0)), whole(rot)]
    if dy is None:
        out_shape, out_specs = jax.ShapeDtypeStruct(x.shape, F32), row
    else:
        ins.append(dy)
        specs.append(row)
        out_shape = [jax.ShapeDtypeStruct(x.shape, F32), jax.ShapeDtypeStruct(g.shape, F32)]
        out_specs = [row, whole(g)]
    return pl.pallas_call(
        body, out_shape=out_shape, grid=(b, s // ts), in_specs=specs, out_specs=out_specs,
        name=f"hnr_{'b' if dy is not None else 'f'}_{n_heads}x{dh}_{int(rope)}", compiler_params=_params(),
    )(*ins)


@functools.partial(jax.custom_vjp, nondiff_argnums=(5,))
def head_norm_rope(x, g, cos, sin, rot, n_heads):
    return _hnr_call(x, g, cos, sin, rot, n_heads)


def _head_norm_rope_fwd(x, g, cos, sin, rot, n_heads):
    return _hnr_call(x, g, cos, sin, rot, n_heads), (x, g, cos, sin, rot)


def _head_norm_rope_bwd(n_heads, res, dy):
    x, g, cos, sin, rot = res
    dx, dg = _hnr_call(x, g, cos, sin, rot, n_heads, dy=dy)
    zero = lambda t: None if t is None else jnp.zeros_like(t)
    return dx, dg, zero(cos), zero(sin), zero(rot)


head_norm_rope.defvjp(_head_norm_rope_fwd, _head_norm_rope_bwd)


def _mla_k_call(kv, kr, g, cos, sin, rot, dkn=None):
    b, s, _ = kv.shape
    ts = _pick(s, (256, 128, 64))
    hw = MLA_NOPE + MLA_V

    def body(kv_ref, kr_ref, g_ref, cos_ref, sin_ref, rot_ref, *rest):
        gv, cos_v, sin_v, rot_v = g_ref[...], cos_ref[...], sin_ref[...], rot_ref[...]
        krv = kr_ref[...]
        if dkn is None:
            (o_ref,) = rest
            for h in range(MLA_HEADS):
                kh = jnp.concatenate([kv_ref[:, h * hw:h * hw + MLA_NOPE], krv], axis=-1)
                o_ref[:, h * MLA_QK:(h + 1) * MLA_QK] = _norm_rope_head(kh, gv, cos_v, sin_v, rot_v)[0]
            return
        dkn_ref, dkv_ref, dkr_ref, dg_ref = rest
        dg = jnp.zeros((1, MLA_QK), F32)
        dkr = jnp.zeros((ts, MLA_ROPE), F32)
        for h in range(MLA_HEADS):
            kh = jnp.concatenate([kv_ref[:, h * hw:h * hw + MLA_NOPE], krv], axis=-1)
            _, yn, r = _norm_rope_head(kh, gv, None, None, None)
            dxh, dgh = _norm_rope_head_bwd(dkn_ref[:, h * MLA_QK:(h + 1) * MLA_QK], yn, r, gv, cos_v, sin_v, rot_v)
            dkv_ref[:, h * hw:h * hw + MLA_NOPE] = dxh[:, :MLA_NOPE]
            dkv_ref[:, h * hw + MLA_NOPE:(h + 1) * hw] = jnp.zeros((ts, MLA_V), F32)
            dkr = dkr + dxh[:, MLA_NOPE:]
            dg = dg + dgh
        dkr_ref[...] = dkr
        _accum_out(dg_ref, dg, _first_step())

    row = lambda w: pl.BlockSpec((None, ts, w), lambda bi, i: (bi, i, 0))
    tab = pl.BlockSpec((ts, MLA_QK), lambda bi, i: (i, 0))
    whole = lambda a: pl.BlockSpec(a.shape, lambda bi, i: (0, 0))
    ins = [kv, kr, g, cos, sin, rot]
    specs = [row(kv.shape[2]), row(MLA_ROPE), whole(g), tab, tab, whole(rot)]
    kn_w = MLA_HEADS * MLA_QK
    if dkn is None:
        out_shape, out_specs = jax.ShapeDtypeStruct((b, s, kn_w), F32), row(kn_w)
    else:
        ins.append(dkn)
        specs.append(row(kn_w))
        out_shape = [jax.ShapeDtypeStruct(kv.shape, F32), jax.ShapeDtypeStruct(kr.shape, F32),
                     jax.ShapeDtypeStruct(g.shape, F32)]
        out_specs = [row(kv.shape[2]), row(MLA_ROPE), whole(g)]
    return pl.pallas_call(
        body, out_shape=out_shape, grid=(b, s // ts), in_specs=specs, out_specs=out_specs,
        name=f"mla_k_{'b' if dkn is not None else 'f'}", compiler_params=_params(),
    )(*ins)


@jax.custom_vjp
def mla_k_prep(kv, kr, g, cos, sin, rot):
    return _mla_k_call(kv, kr, g, cos, sin, rot)


def _mla_k_prep_fwd(kv, kr, g, cos, sin, rot):
    return _mla_k_call(kv, kr, g, cos, sin, rot), (kv, kr, g, cos, sin, rot)


def _mla_k_prep_bwd(res, dkn):
    kv, kr, g, cos, sin, rot = res
    dkv, dkr, dg = _mla_k_call(kv, kr, g, cos, sin, rot, dkn=dkn)
    return dkv, dkr, dg, jnp.zeros_like(cos), jnp.zeros_like(sin), jnp.zeros_like(rot)


mla_k_prep.defvjp(_mla_k_prep_fwd, _mla_k_prep_bwd)


class _HeadLayout:
    def __init__(self, groups, dq, dv, q_off, k_off, v_off, o_off, wq, wk, wv, wo, scale):
        self.groups, self.dq, self.dv, self.scale = groups, dq, dv, scale
        self.q_off, self.k_off, self.v_off, self.o_off = q_off, k_off, v_off, o_off
        self.wq, self.wk, self.wv, self.wo = wq, wk, wv, wo
        self.n_h = len(q_off)


def _gqa_layout():
    rep = GQA_Q_HEADS // GQA_KV_HEADS
    n_h = GQA_Q_HEADS // 2
    return _HeadLayout(2, HEAD_DIM, HEAD_DIM, [h * HEAD_DIM for h in range(n_h)], [(h // rep) * HEAD_DIM for h in range(n_h)],
                       [(h // rep) * HEAD_DIM for h in range(n_h)], [h * HEAD_DIM for h in range(n_h)],
                       n_h * HEAD_DIM, (n_h // rep) * HEAD_DIM, (n_h // rep) * HEAD_DIM, n_h * HEAD_DIM, HEAD_DIM ** -0.5)


def _mla_layout():
    n_h = MLA_HEADS // 2
    hw = MLA_NOPE + MLA_V
    return _HeadLayout(2, MLA_QK, MLA_V, [h * MLA_QK for h in range(n_h)], [h * MLA_QK for h in range(n_h)],
                       [h * hw + MLA_NOPE for h in range(n_h)], [h * MLA_V for h in range(n_h)],
                       n_h * MLA_QK, n_h * MLA_QK, n_h * hw, n_h * MLA_V, MLA_QK ** -0.5)


def _attn_tm_fwd(q, k, v, lay, n_ctx):
    b, s, _ = q.shape
    tq = min(256, n_ctx)
    nc = n_ctx // tq

    def body(q_ref, k_ref, v_ref, o_ref, lse_ref):
        def run(n_keys):
            for h in range(lay.n_h):
                qo, ko, vo, oo = lay.q_off[h], lay.k_off[h], lay.v_off[h], lay.o_off[h]
                qv = (q_ref[:, qo:qo + lay.dq] * lay.scale).astype(BF16)
                sc = lax.dot_general(qv, k_ref[0:n_keys, ko:ko + lay.dq].astype(BF16), _NT, preferred_element_type=F32)
                m = jnp.max(sc, axis=-1, keepdims=True)
                p = jnp.exp(sc - m)
                l = jnp.sum(p, axis=-1, keepdims=True)
                o = jnp.dot(p.astype(BF16), v_ref[0:n_keys, vo:vo + lay.dv].astype(BF16), preferred_element_type=F32)
                o_ref[:, oo:oo + lay.dv] = o / l
                lse_ref[:, h:h + 1] = m + jnp.log(l)

        pl.when(pl.program_id(2) < nc)(lambda: run(n_ctx))
        pl.when(pl.program_id(2) >= nc)(lambda: run(s))

    return pl.pallas_call(
        body, out_shape=[jax.ShapeDtypeStruct((b, s, lay.groups * lay.wo), F32),
                         jax.ShapeDtypeStruct((b, lay.groups, s, lay.n_h), F32)],
        grid=(b, lay.groups, s // tq),
        in_specs=[pl.BlockSpec((None, tq, lay.wq), lambda bi, g, i: (bi, i, g)),
                  pl.BlockSpec((None, s, lay.wk), lambda bi, g, i: (bi, 0, g)),
                  pl.BlockSpec((None, s, lay.wv), lambda bi, g, i: (bi, 0, g))],
        out_specs=[pl.BlockSpec((None, tq, lay.wo), lambda bi, g, i: (bi, i, g)),
                   pl.BlockSpec((None, None, tq, lay.n_h), lambda bi, g, i: (bi, g, i, 0))],
        name=f"attn_tm_f_{lay.dq}", compiler_params=_params(),
    )(q, k, v)


def _attn_tm_delta(o, do, lay):
    b, s, _ = o.shape
    ts = _pick(s, (256, 128, 64))

    def body(o_ref, do_ref, d_ref):
        for h in range(lay.n_h):
            oo = lay.o_off[h]
            d_ref[:, h:h + 1] = jnp.sum(o_ref[:, oo:oo + lay.dv] * do_ref[:, oo:oo + lay.dv], axis=-1, keepdims=True)

    blk = pl.BlockSpec((None, ts, lay.wo), lambda bi, g, i: (bi, i, g))
    return pl.pallas_call(
        body, out_shape=jax.ShapeDtypeStruct((b, lay.groups, s, lay.n_h), F32), grid=(b, lay.groups, s // ts),
        in_specs=[blk, blk], out_specs=pl.BlockSpec((None, None, ts, lay.n_h), lambda bi, g, i: (bi, g, i, 0)),
        name=f"attn_tm_delta_{lay.dq}", compiler_params=_params(),
    )(o, do)


def _attn_tm_bwd(q, k, v, lse, delta, do, lay, n_ctx):
    b, s, _ = q.shape
    tk = min(256, n_ctx)
    nc = n_ctx // tk

    def body(q_ref, k_ref, v_ref, lse_ref, delta_ref, do_ref, dq_ref, dk_ref, dv_ref):
        @pl.when(pl.program_id(2) == 0)
        def _():
            dq_ref[...] = jnp.zeros_like(dq_ref)

        def run(r0):
            dk_acc, dv_acc = {}, {}
            for h in range(lay.n_h):
                qo, ko, vo, oo = lay.q_off[h], lay.k_off[h], lay.v_off[h], lay.o_off[h]
                kh = k_ref[:, ko:ko + lay.dq].astype(BF16)
                vh = v_ref[:, vo:vo + lay.dv].astype(BF16)
                qv = (q_ref[r0:s, qo:qo + lay.dq] * lay.scale).astype(BF16)
                dob = do_ref[r0:s, oo:oo + lay.dv].astype(BF16)
                sc = lax.dot_general(qv, kh, _NT, preferred_element_type=F32)
                p = jnp.exp(sc - lse_ref[r0:s, h:h + 1])
                dvh = lax.dot_general(p.astype(BF16), dob, _TN, preferred_element_type=F32)
                dp = lax.dot_general(dob, vh, _NT, preferred_element_type=F32)
                dsb = (p * (dp - delta_ref[r0:s, h:h + 1])).astype(BF16)
                dkh = lax.dot_general(dsb, qv, _TN, preferred_element_type=F32)
                dq_ref[r0:s, qo:qo + lay.dq] += jnp.dot(dsb, kh, preferred_element_type=F32) * lay.scale
                dk_acc[ko] = dkh if ko not in dk_acc else dk_acc[ko] + dkh
                dv_acc[vo] = dvh if vo not in dv_acc else dv_acc[vo] + dvh
            if len(dv_acc) * lay.dv != lay.wv:
                dv_ref[...] = jnp.zeros_like(dv_ref)
            for ko, val in dk_acc.items():
                dk_ref[:, ko:ko + lay.dq] = val
            for vo, val in dv_acc.items():
                dv_ref[:, vo:vo + lay.dv] = val

        pl.when(pl.program_id(2) < nc)(lambda: run(0))
        pl.when(pl.program_id(2) >= nc)(lambda: run(n_ctx))

    full = lambda w: pl.BlockSpec((None, s, w), lambda bi, g, j: (bi, 0, g))
    blk = lambda w: pl.BlockSpec((None, tk, w), lambda bi, g, j: (bi, j, g))
    stat = pl.BlockSpec((None, None, s, lay.n_h), lambda bi, g, j: (bi, g, 0, 0))
    return pl.pallas_call(
        body, out_shape=[jax.ShapeDtypeStruct(q.shape, F32), jax.ShapeDtypeStruct(k.shape, F32),
                         jax.ShapeDtypeStruct(v.shape, F32)],
        grid=(b, lay.groups, s // tk),
        in_specs=[full(lay.wq), blk(lay.wk), blk(lay.wv), stat, stat, full(lay.wo)],
        out_specs=[full(lay.wq), blk(lay.wk), blk(lay.wv)],
        name=f"attn_tm_b_{lay.dq}", compiler_params=_params(),
    )(q, k, v, lse, delta, do)


def _make_attention_tm(lay):
    @functools.partial(jax.custom_vjp, nondiff_argnums=(3,))
    def op(q, k, v, n_ctx):
        return _attn_tm_fwd(q, k, v, lay, n_ctx)[0]

    def fwd(q, k, v, n_ctx):
        o, lse = _attn_tm_fwd(q, k, v, lay, n_ctx)
        return o, (q, k, v, o, lse)

    def bwd(n_ctx, res, do):
        q, k, v, o, lse = res
        return _attn_tm_bwd(q, k, v, lse, _attn_tm_delta(o, do, lay), do, lay, n_ctx)

    op.defvjp(fwd, bwd)
    return op


gqa_attention = _make_attention_tm(_gqa_layout())
mla_attention = _make_attention_tm(_mla_layout())

NA_GROUPS = 2


def _na_tm_specs(s, nc, rows):
    hg = NA_HEADS // NA_GROUPS
    w = hg * HEAD_DIM
    qs = pl.BlockSpec((None, GRID_W, w), lambda bi, g, i: (bi, i, g))
    ks = pl.BlockSpec((None, s, w), lambda bi, g, i: (bi, 0, g))
    bs = pl.BlockSpec((hg, None, GRID_W, NA_BAND), lambda bi, g, i: (g, _na_geometry(i, nc, rows)[1], 0, 0))
    ls = pl.BlockSpec((None, None, GRID_W, hg), lambda bi, g, i: (bi, g, i, 0))
    return hg, w, qs, ks, bs, ls


def _na_tm_scores(q_ref, k_ref, bias_ref, hd, n_ctx, start, scale):
    sl = slice(hd * HEAD_DIM, (hd + 1) * HEAD_DIM)
    qv = (q_ref[:, sl] * scale).astype(BF16)
    kc = k_ref[0:n_ctx, sl].astype(BF16)
    kb = k_ref[pl.ds(start, NA_BAND), sl].astype(BF16)
    s_c = lax.dot_general(qv, kc, _NT, preferred_element_type=F32)
    s_l = lax.dot_general(qv, kb, _NT, preferred_element_type=F32) + bias_ref[hd]
    return sl, qv, kc, kb, s_c, s_l


def _na_tm_fwd(q, k, v, bias, n_ctx):
    b, s, _ = q.shape
    nc = n_ctx // GRID_W
    rows = (s - n_ctx) // GRID_W
    scale = HEAD_DIM ** -0.5
    hg, w, qs, ks, bs, ls = _na_tm_specs(s, nc, rows)

    def body(q_ref, k_ref, v_ref, bias_ref, o_ref, lse_ref):
        rs, _ = _na_geometry(pl.program_id(2), nc, rows)
        start = pl.multiple_of(n_ctx + rs * GRID_W, GRID_W)
        for hd in range(hg):
            sl, _, _, _, s_c, s_l = _na_tm_scores(q_ref, k_ref, bias_ref, hd, n_ctx, start, scale)
            m = jnp.maximum(jnp.max(s_c, axis=-1, keepdims=True), jnp.max(s_l, axis=-1, keepdims=True))
            p_c = jnp.exp(s_c - m)
            p_l = jnp.exp(s_l - m)
            l = jnp.sum(p_c, axis=-1, keepdims=True) + jnp.sum(p_l, axis=-1, keepdims=True)
            o = jnp.dot(p_c.astype(BF16), v_ref[0:n_ctx, sl].astype(BF16), preferred_element_type=F32)
            o = o + jnp.dot(p_l.astype(BF16), v_ref[pl.ds(start, NA_BAND), sl].astype(BF16), preferred_element_type=F32)
            o_ref[:, sl] = o / l
            lse_ref[:, hd:hd + 1] = m + jnp.log(l)

    return pl.pallas_call(
        body, out_shape=[jax.ShapeDtypeStruct(q.shape, F32), jax.ShapeDtypeStruct((b, NA_GROUPS, s, hg), F32)],
        grid=(b, NA_GROUPS, s // GRID_W), in_specs=[qs, ks, ks, bs], out_specs=[qs, ls],
        name=f"na_tm_f_{s}", compiler_params=_params(),
    )(q, k, v, bias)


def _na_tm_bwd(q, k, v, bias, o, lse, do, n_ctx):
    b, s, _ = q.shape
    nc = n_ctx // GRID_W
    rows = (s - n_ctx) // GRID_W
    scale = HEAD_DIM ** -0.5
    n_cls = NA_WIN_R + 1
    hg, w, qs, ks, bs, ls = _na_tm_specs(s, nc, rows)

    def body(q_ref, k_ref, v_ref, bias_ref, o_ref, lse_ref, do_ref, dq_ref, dk_ref, dv_ref, db_ref):
        i = pl.program_id(2)
        rs, cls = _na_geometry(i, nc, rows)
        _, cls_prev = _na_geometry(i - 1, nc, rows)
        start = pl.multiple_of(n_ctx + rs * GRID_W, GRID_W)
        first = jnp.logical_or(i == 0, cls != cls_prev)

        @pl.when(i == 0)
        def _():
            dk_ref[...] = jnp.zeros_like(dk_ref)
            dv_ref[...] = jnp.zeros_like(dv_ref)

        for hd in range(hg):
            sl, qv, kc, kb, s_c, s_l = _na_tm_scores(q_ref, k_ref, bias_ref, hd, n_ctx, start, scale)
            lse_v = lse_ref[:, hd:hd + 1]
            p_c = jnp.exp(s_c - lse_v)
            p_l = jnp.exp(s_l - lse_v)
            dov = do_ref[:, sl]
            dob = dov.astype(BF16)
            delta = jnp.sum(dov * o_ref[:, sl], axis=-1, keepdims=True)
            vc = v_ref[0:n_ctx, sl].astype(BF16)
            vb = v_ref[pl.ds(start, NA_BAND), sl].astype(BF16)
            ds_c = p_c * (lax.dot_general(dob, vc, _NT, preferred_element_type=F32) - delta)
            ds_l = p_l * (lax.dot_general(dob, vb, _NT, preferred_element_type=F32) - delta)
            dsc_b = ds_c.astype(BF16)
            dsl_b = ds_l.astype(BF16)
            dq_ref[:, sl] = (jnp.dot(dsc_b, kc, preferred_element_type=F32)
                             + jnp.dot(dsl_b, kb, preferred_element_type=F32)) * scale
            dk_ref[0:n_ctx, sl] += lax.dot_general(dsc_b, qv, _TN, preferred_element_type=F32)
            dk_ref[pl.ds(start, NA_BAND), sl] += lax.dot_general(dsl_b, qv, _TN, preferred_element_type=F32)
            dv_ref[0:n_ctx, sl] += lax.dot_general(p_c.astype(BF16), dob, _TN, preferred_element_type=F32)
            dv_ref[pl.ds(start, NA_BAND), sl] += lax.dot_general(p_l.astype(BF16), dob, _TN, preferred_element_type=F32)

            @pl.when(first)
            def _(hd=hd, ds_l=ds_l):
                db_ref[hd] = ds_l

            @pl.when(jnp.logical_not(first))
            def _(hd=hd, ds_l=ds_l):
                db_ref[hd] += ds_l

    dbs = pl.BlockSpec((None, hg, None, GRID_W, NA_BAND), lambda bi, g, i: (bi, g, _na_geometry(i, nc, rows)[1], 0, 0))
    return pl.pallas_call(
        body,
        out_shape=[jax.ShapeDtypeStruct(q.shape, F32), jax.ShapeDtypeStruct(q.shape, F32), jax.ShapeDtypeStruct(q.shape, F32),
                   jax.ShapeDtypeStruct((b, NA_HEADS, n_cls, GRID_W, NA_BAND), F32)],
        grid=(b, NA_GROUPS, s // GRID_W), in_specs=[qs, ks, ks, bs, qs, ls, qs], out_specs=[qs, ks, ks, dbs],
        name=f"na_tm_b_{s}", compiler_params=_params(),
    )(q, k, v, bias, o, lse, do)


@functools.partial(jax.custom_vjp, nondiff_argnums=(4,))
def na_attention_tm(q, k, v, bias, n_ctx):
    return _na_tm_fwd(q, k, v, bias, n_ctx)[0]


def _na_attention_tm_fwd(q, k, v, bias, n_ctx):
    o, lse = _na_tm_fwd(q, k, v, bias, n_ctx)
    return o, (q, k, v, bias, o, lse)


def _na_attention_tm_bwd(n_ctx, res, do):
    q, k, v, bias, o, lse = res
    dq, dk, dv, db = _na_tm_bwd(q, k, v, bias, o, lse, do, n_ctx)
    return dq, dk, dv, _sum_rows(db.reshape(db.shape[0], -1, NA_BAND), db.shape[0]).reshape(db.shape[1:])


na_attention_tm.defvjp(_na_attention_tm_fwd, _na_attention_tm_bwd)


def _cmul(ar, ai, br, bi):
    return ar * br - ai * bi, ar * bi + ai * br


def _s5_chunk(n_ctx):
    return min(256, n_ctx)


def _s5_tables(a_re, a_im, t_len, rev):
    a_re, a_im = lax.stop_gradient(a_re), lax.stop_gradient(a_im)
    mag = jnp.sqrt(a_re * a_re + a_im * a_im)
    th = jnp.arctan2(a_im, a_re)
    t = jnp.arange(t_len + 1, dtype=F32)[:, None]
    pm = jnp.where(t == 0, 1.0, jnp.exp(t * jnp.log(jnp.maximum(mag, 1e-37))) * (mag > 0))
    pw = jnp.stack([pm * jnp.cos(t * th), pm * jnp.sin(t * th)])
    steps = jnp.concatenate([pw[:, min(2 ** i, t_len)][:, None] for i in range(8)], axis=1)
    tile = pw[:, 1:9]
    a8k = pw[:, 0:t_len:8]
    if rev:
        tile, a8k = tile[:, ::-1], a8k[:, ::-1]
    misc = jnp.concatenate([pw[:, t_len:t_len + 1], jnp.zeros((2, 7, pw.shape[-1]), F32)], axis=1)
    return jnp.concatenate([steps, tile, misc, a8k], axis=1)


def _scan_chunk(x_re, x_im, tab_ref, hin_re, hin_im, rev, t_len, xs_ref, es_ref):
    outs = [_scan_slab(x_re[:, k:k + LANE], x_im[:, k:k + LANE], tab_ref, hin_re[:, k:k + LANE], hin_im[:, k:k + LANE],
                       rev, t_len, xs_ref, es_ref, k) for k in range(0, x_re.shape[-1], LANE)]
    return tuple(jnp.concatenate([o[t] for o in outs], axis=-1) for t in range(4))


def _scan_slab(x_re, x_im, tab_ref, hin_re, hin_im, rev, t_len, xs_ref, es_ref, k0):
    lanes = LANE
    n2 = t_len // 8
    tab_ref = tab_ref.at[:, :, k0:k0 + LANE]
    rin = lax.broadcasted_iota(jnp.int32, (t_len, lanes), 0) & 7
    for li, sh in enumerate((1, 2, 4)):
        m_re, m_im = tab_ref[0, li:li + 1, :], tab_ref[1, li:li + 1, :]
        amt = sh if not rev else t_len - sh
        c_re, c_im = _cmul(m_re, m_im, pltpu.roll(x_re, amt, 0), pltpu.roll(x_im, amt, 0))
        ok = (rin >= sh) if not rev else (rin < 8 - sh)
        x_re = x_re + jnp.where(ok, c_re, 0.0)
        x_im = x_im + jnp.where(ok, c_im, 0.0)
    xr_ref, xi_ref = xs_ref
    xr_ref[...] = x_re
    xi_ref[...] = x_im
    off = 0 if rev else 7
    e_re = xr_ref[pl.ds(off, n2, stride=8), :]
    e_im = xi_ref[pl.ds(off, n2, stride=8), :]
    row2 = lax.broadcasted_iota(jnp.int32, (n2, lanes), 0)
    sh, li = 1, 3
    while sh < n2:
        m_re, m_im = tab_ref[0, li:li + 1, :], tab_ref[1, li:li + 1, :]
        amt = sh if not rev else n2 - sh
        c_re, c_im = _cmul(m_re, m_im, pltpu.roll(e_re, amt, 0), pltpu.roll(e_im, amt, 0))
        ok = (row2 >= sh) if not rev else (row2 < n2 - sh)
        e_re = e_re + jnp.where(ok, c_re, 0.0)
        e_im = e_im + jnp.where(ok, c_im, 0.0)
        sh, li = sh * 2, li + 1
    es_ref[0] = e_re
    es_ref[1] = e_im
    last = 0 if rev else n2 - 1
    t_re, t_im = _cmul(tab_ref[0, 16:17, :], tab_ref[1, 16:17, :], hin_re, hin_im)
    hout_re = es_ref[0, last:last + 1, :] + t_re
    hout_im = es_ref[1, last:last + 1, :] + t_im
    amt = 1 if not rev else n2 - 1
    ok = (row2 >= 1) if not rev else (row2 < n2 - 1)
    k_re, k_im = _cmul(tab_ref[0, 24:24 + n2, :], tab_ref[1, 24:24 + n2, :], hin_re, hin_im)
    c_re = jnp.where(ok, pltpu.roll(e_re, amt, 0), 0.0) + k_re
    c_im = jnp.where(ok, pltpu.roll(e_im, amt, 0), 0.0) + k_im
    tp_re, tp_im = tab_ref[0, 8:16, :][None], tab_ref[1, 8:16, :][None]
    add_re, add_im = _cmul(tp_re, tp_im, c_re[:, None, :], c_im[:, None, :])
    h_re = xr_ref[...] + add_re.reshape(t_len, lanes)
    h_im = xi_ref[...] + add_im.reshape(t_len, lanes)
    return h_re, h_im, hout_re, hout_im


def _s5_order(j, n_chunks, nc, rev):
    if not rev:
        return j
    return jnp.where(j < nc, nc - 1 - j, n_chunks - 1 - (j - nc))


def _s5_fwd(u, tab, b_bd, c_bd, n_ctx, rev):
    b, s, w = u.shape
    lanes = b_bd.shape[-1]
    t_len = _s5_chunk(n_ctx)
    n_chunks, nc = s // t_len, n_ctx // t_len

    def body(u_ref, tab_ref, b_ref, c_ref, y_ref, h_ref, hin_ref, carry_ref, xr_ref, xi_ref, es_ref):
        xs_ref = (xr_ref, xi_ref)

        @pl.when(pl.program_id(1) == 0)
        def _():
            carry_ref[...] = jnp.zeros_like(carry_ref)

        ub = u_ref[...].astype(BF16)
        x_re = jnp.dot(ub, b_ref[0].astype(BF16), preferred_element_type=F32)
        x_im = jnp.dot(ub, b_ref[1].astype(BF16), preferred_element_type=F32)
        hin_re, hin_im = carry_ref[0, 0:1, :], carry_ref[1, 0:1, :]
        hin_ref[...] = carry_ref[...]
        h_re, h_im, ho_re, ho_im = _scan_chunk(x_re, x_im, tab_ref, hin_re, hin_im, rev, t_len, xs_ref, es_ref)
        carry_ref[0] = jnp.broadcast_to(ho_re, (8, lanes))
        carry_ref[1] = jnp.broadcast_to(ho_im, (8, lanes))
        h_ref[0] = h_re
        h_ref[1] = h_im
        y_ref[...] = (jnp.dot(h_re.astype(BF16), c_ref[0].astype(BF16), preferred_element_type=F32)
                      - jnp.dot(h_im.astype(BF16), c_ref[1].astype(BF16), preferred_element_type=F32))

    order = lambda j: _s5_order(j, n_chunks, nc, rev)
    whole = lambda arr: pl.BlockSpec(arr.shape, lambda bi, j: (0,) * arr.ndim)
    return pl.pallas_call(
        body,
        out_shape=[jax.ShapeDtypeStruct((b, s, w), F32), jax.ShapeDtypeStruct((2, b, s, lanes), F32),
                   jax.ShapeDtypeStruct((2, b, n_chunks, 8, lanes), F32)],
        grid=(b, n_chunks),
        in_specs=[pl.BlockSpec((None, t_len, w), lambda bi, j: (bi, order(j), 0)), whole(tab), whole(b_bd), whole(c_bd)],
        out_specs=[pl.BlockSpec((None, t_len, w), lambda bi, j: (bi, order(j), 0)),
                   pl.BlockSpec((2, None, t_len, lanes), lambda bi, j: (0, bi, order(j), 0)),
                   pl.BlockSpec((2, None, None, 8, lanes), lambda bi, j: (0, bi, order(j), 0, 0))],
        scratch_shapes=[pltpu.VMEM((2, 8, lanes), F32), pltpu.VMEM((t_len, LANE), F32), pltpu.VMEM((t_len, LANE), F32),
                        pltpu.VMEM((2, t_len // 8, LANE), F32)],
        name=f"s5_f_{s}_{int(rev)}", compiler_params=_params(),
    )(u, tab, b_bd, c_bd)


def _s5_bwd(u, tab_adj, b_bd, c_bd, h, hin, dy, n_ctx, rev):
    b, s, w = u.shape
    lanes = b_bd.shape[-1]
    t_len = _s5_chunk(n_ctx)
    n_chunks, nc = s // t_len, n_ctx // t_len
    arev = not rev

    def body(u_ref, tab_ref, b_ref, c_ref, h_ref, hin_ref, dy_ref, du_ref, db_ref, dc_ref, da_ref,
             carry_ref, xr_ref, xi_ref, es_ref):
        xs_ref = (xr_ref, xi_ref)
        first = jnp.logical_and(pl.program_id(0) == 0, pl.program_id(1) == 0)

        @pl.when(pl.program_id(1) == 0)
        def _():
            carry_ref[...] = jnp.zeros_like(carry_ref)

        dyv = dy_ref[...]
        dyb = dyv.astype(BF16)
        dn = (((1,), (1,)), ((), ()))
        dt = (((0,), (0,)), ((), ()))
        x_re = lax.dot_general(dyb, c_ref[0].astype(BF16), dn, preferred_element_type=F32)
        x_im = -lax.dot_general(dyb, c_ref[1].astype(BF16), dn, preferred_element_type=F32)
        g_re, g_im, go_re, go_im = _scan_chunk(x_re, x_im, tab_ref, carry_ref[0, 0:1, :], carry_ref[1, 0:1, :],
                                               arev, t_len, xs_ref, es_ref)
        carry_ref[0] = jnp.broadcast_to(go_re, (8, lanes))
        carry_ref[1] = jnp.broadcast_to(go_im, (8, lanes))
        h_re, h_im = h_ref[0], h_ref[1]
        gb_re, gb_im = g_re.astype(BF16), g_im.astype(BF16)
        du_ref[...] = (lax.dot_general(gb_re, b_ref[0].astype(BF16), dn, preferred_element_type=F32)
                       + lax.dot_general(gb_im, b_ref[1].astype(BF16), dn, preferred_element_type=F32))
        ub = u_ref[...].astype(BF16)
        db_re = lax.dot_general(ub, gb_re, dt, preferred_element_type=F32)
        db_im = lax.dot_general(ub, gb_im, dt, preferred_element_type=F32)
        dc_re = lax.dot_general(h_re.astype(BF16), dyb, dt, preferred_element_type=F32)
        dc_im = -lax.dot_general(h_im.astype(BF16), dyb, dt, preferred_element_type=F32)
        row = lax.broadcasted_iota(jnp.int32, (t_len, lanes), 0)
        amt = 1 if not rev else t_len - 1
        edge = (row == 0) if not rev else (row == t_len - 1)
        hp_re = jnp.where(edge, hin_ref[0, 0:1, :], pltpu.roll(h_re, amt, 0))
        hp_im = jnp.where(edge, hin_ref[1, 0:1, :], pltpu.roll(h_im, amt, 0))
        da_re = jnp.sum(g_re * hp_re + g_im * hp_im, axis=0, keepdims=True)
        da_im = jnp.sum(g_im * hp_re - g_re * hp_im, axis=0, keepdims=True)

        @pl.when(first)
        def _():
            db_ref[0], db_ref[1] = db_re, db_im
            dc_ref[0], dc_ref[1] = dc_re, dc_im
            da_ref[0] = jnp.broadcast_to(da_re, (8, lanes))
            da_ref[1] = jnp.broadcast_to(da_im, (8, lanes))

        @pl.when(jnp.logical_not(first))
        def _():
            db_ref[0] += db_re
            db_ref[1] += db_im
            dc_ref[0] += dc_re
            dc_ref[1] += dc_im
            da_ref[0] += jnp.broadcast_to(da_re, (8, lanes))
            da_ref[1] += jnp.broadcast_to(da_im, (8, lanes))

    order = lambda j: _s5_order(n_chunks - 1 - j, n_chunks, nc, rev)
    whole = lambda arr: pl.BlockSpec(arr.shape, lambda bi, j: (0,) * arr.ndim)
    us = pl.BlockSpec((None, t_len, w), lambda bi, j: (bi, order(j), 0))
    return pl.pallas_call(
        body,
        out_shape=[jax.ShapeDtypeStruct((b, s, w), F32), jax.ShapeDtypeStruct(b_bd.shape, F32),
                   jax.ShapeDtypeStruct(c_bd.shape, F32), jax.ShapeDtypeStruct((2, 8, lanes), F32)],
        grid=(b, n_chunks),
        in_specs=[us, whole(tab_adj), whole(b_bd), whole(c_bd),
                  pl.BlockSpec((2, None, t_len, lanes), lambda bi, j: (0, bi, order(j), 0)),
                  pl.BlockSpec((2, None, None, 8, lanes), lambda bi, j: (0, bi, order(j), 0, 0)), us],
        out_specs=[us, whole(b_bd), whole(c_bd), pl.BlockSpec((2, 8, lanes), lambda bi, j: (0, 0, 0))],
        scratch_shapes=[pltpu.VMEM((2, 8, lanes), F32), pltpu.VMEM((t_len, LANE), F32), pltpu.VMEM((t_len, LANE), F32),
                        pltpu.VMEM((2, t_len // 8, LANE), F32)],
        name=f"s5_b_{s}_{int(rev)}", compiler_params=_params(),
    )(u, tab_adj, b_bd, c_bd, h, hin, dy)


@functools.partial(jax.custom_vjp, nondiff_argnums=(4, 5))
def s5_direction(u, a, b_bd, c_bd, n_ctx, rev):
    tab = _s5_tables(a[0], a[1], _s5_chunk(n_ctx), rev)
    return _s5_fwd(u, tab, b_bd, c_bd, n_ctx, rev)[0]


def _s5_direction_fwd(u, a, b_bd, c_bd, n_ctx, rev):
    tab = _s5_tables(a[0], a[1], _s5_chunk(n_ctx), rev)
    y, h, hin = _s5_fwd(u, tab, b_bd, c_bd, n_ctx, rev)
    return y, (u, a, b_bd, c_bd, h, hin)


def _s5_direction_bwd(n_ctx, rev, res, dy):
    u, a, b_bd, c_bd, h, hin = res
    tab_adj = _s5_tables(a[0], -a[1], _s5_chunk(n_ctx), not rev)
    du, db, dc, da = _s5_bwd(u, tab_adj, b_bd, c_bd, h, hin, dy, n_ctx, rev)
    return du, da[:, 0, :], db, dc


s5_direction.defvjp(_s5_direction_fwd, _s5_direction_bwd)


def _s5_discretize(lam_re, lam_im, log_dt, b_re, b_im):
    dt = jnp.exp(log_dt)[:, None]
    mag = jnp.exp(lam_re * dt)
    a_re = mag * jnp.cos(lam_im * dt)
    a_im = mag * jnp.sin(lam_im * dt)
    den = jnp.square(lam_re) + jnp.square(lam_im)
    f_re = ((a_re - 1.0) * lam_re + a_im * lam_im) / den
    f_im = (a_im * lam_re - (a_re - 1.0) * lam_im) / den
    bb_re = f_re[..., None] * b_re - f_im[..., None] * b_im
    bb_im = f_re[..., None] * b_im + f_im[..., None] * b_re
    return a_re, a_im, bb_re, bb_im


def _block_diag(t):
    g, r, c = t.shape
    return (jnp.eye(g, dtype=F32)[:, None, :, None] * t[:, :, None, :]).reshape(g * r, g * c)


def _loss_head(y, target):
    b, n, d = y.shape
    ts = _pick(n, (256, 128, 64))

    def body(y_ref, t_ref, loss_ref, dy_ref):
        first = jnp.logical_and(pl.program_id(0) == 0, pl.program_id(1) == 0)
        err = y_ref[...] - t_ref[...]
        dy_ref[...] = err * (1.0 / d)
        part = 0.5 * jnp.sum(jnp.sum(err * err, axis=-1, keepdims=True) * (1.0 / d), axis=0, keepdims=True)
        part = jnp.broadcast_to(part, (8, LANE))

        @pl.when(first)
        def _():
            loss_ref[...] = part

        @pl.when(jnp.logical_not(first))
        def _():
            loss_ref[...] += part

    blk = pl.BlockSpec((None, ts, d), lambda bi, i: (bi, i, 0))
    return pl.pallas_call(
        body, out_shape=[jax.ShapeDtypeStruct((8, LANE), F32), jax.ShapeDtypeStruct((b, n, d), F32)],
        grid=(b, n // ts), in_specs=[blk, blk], out_specs=[pl.BlockSpec((8, LANE), lambda bi, i: (0, 0)), blk],
        name="loss_head", compiler_params=_params(),
    )(y, target)


def _adamw(w, g, m, v):
    shape = w.shape
    n = int(np.prod(shape))
    cols = shape[-1]
    r = n // cols
    tr = _pick(r, (512, 256, 128, 64, 32, 16, 8))
    c1 = 1.0 / (1.0 - ADAM_B1 ** ADAM_STEP)
    c2 = 1.0 / (1.0 - ADAM_B2 ** ADAM_STEP)

    def body(w_ref, g_ref, m_ref, v_ref, d_ref, mo_ref, vo_ref):
        gv = g_ref[...]
        m2 = ADAM_B1 * m_ref[...] + (1.0 - ADAM_B1) * gv
        v2 = ADAM_B2 * v_ref[...] + (1.0 - ADAM_B2) * (gv * gv)
        d_ref[...] = -ADAM_LR * ((m2 * c1) / (jnp.sqrt(v2 * c2) + ADAM_EPS) + ADAM_WD * w_ref[...])
        mo_ref[...] = m2
        vo_ref[...] = v2

    blk = pl.BlockSpec((tr, cols), lambda i: (i, 0))
    outs = pl.pallas_call(
        body, out_shape=[jax.ShapeDtypeStruct((r, cols), F32)] * 3, grid=(r // tr,),
        in_specs=[blk] * 4, out_specs=[blk] * 3, name=f"adamw_{r}x{cols}", compiler_params=_params(),
    )(*[t.reshape(r, cols) for t in (w, g, m, v)])
    return tuple(o.reshape(shape) for o in outs)


def _sum_rows(x, n):
    _, r, c = x.shape
    tr = _pick(r, (512, 256, 128, 64, 32, 16, 8))

    def body(x_ref, o_ref):
        acc = x_ref[0]
        for j in range(1, n):
            acc = acc + x_ref[j]
        o_ref[...] = acc

    return pl.pallas_call(
        body, out_shape=jax.ShapeDtypeStruct((r, c), F32), grid=(r // tr,),
        in_specs=[pl.BlockSpec((n, tr, c), lambda i: (0, i, 0))], out_specs=pl.BlockSpec((tr, c), lambda i: (i, 0)),
        name=f"sum{n}_{r}x{c}", compiler_params=_params(),
    )(x)


def _accumulate(parts, out_dtype):
    r, c = parts[0].shape[-2:]
    tr = _pick(r, (512, 256, 128, 64, 32, 16))

    def body(*refs):
        acc = None
        for ref in refs[:-1]:
            terms = [ref[j] for j in range(ref.shape[0])] if len(ref.shape) == 3 else [ref[...]]
            for t in terms:
                acc = t.astype(F32) if acc is None else acc + t.astype(F32)
        refs[-1][...] = acc.astype(out_dtype)

    specs = [pl.BlockSpec((p.shape[0], tr, c), lambda i: (0, i, 0)) if p.ndim == 3 else pl.BlockSpec((tr, c), lambda i: (i, 0))
             for p in parts]
    tag = "_".join(str(p.shape[0]) if p.ndim == 3 else "1" for p in parts)
    return pl.pallas_call(
        body, out_shape=jax.ShapeDtypeStruct((r, c), out_dtype), grid=(r // tr,), in_specs=specs,
        out_specs=pl.BlockSpec((tr, c), lambda i: (i, 0)), name=f"accumulate_{tag}_{r}x{c}_{jnp.dtype(out_dtype).name}",
        compiler_params=_params(),
    )(*parts)


def _add2(x, y):
    shape = x.shape
    c = shape[-1]
    r = int(np.prod(shape)) // c
    tr = _pick(r, (512, 256, 128, 64, 32, 16, 8))

    def body(x_ref, y_ref, o_ref):
        o_ref[...] = x_ref[...] + y_ref[...]

    blk = pl.BlockSpec((tr, c), lambda i: (i, 0))
    return pl.pallas_call(
        body, out_shape=jax.ShapeDtypeStruct((r, c), F32), grid=(r // tr,), in_specs=[blk, blk], out_specs=blk,
        name=f"add2_{r}x{c}", compiler_params=_params(),
    )(x.reshape(r, c), y.reshape(r, c)).reshape(shape)


_FLIPS = ((1, 0), (0, 1), (1, 1))


def _me():
    return lax.axis_index("x"), lax.axis_index("y"), lax.axis_index("c")


def allgather8(v):
    m_per, n = v.shape

    def body(x_ref, out_ref, send_sems, recv_sems, local_sem):
        x, y, c = _me()
        me, sibling = (x, y, c), (x, y, 1 - c)
        chips = [(1 - x, y), (x, 1 - y), (1 - x, 1 - y)]

        def rows(px, py, pc):
            return out_ref.at[pl.ds((4 * px + 2 * py + pc) * m_per, m_per), :]

        def copy(k, block, to, src=None):
            return pltpu.make_async_remote_copy(
                src_ref=rows(*block) if src is None else src, dst_ref=rows(*block),
                send_sem=send_sems.at[k], recv_sem=recv_sems.at[k], device_id=to, device_id_type=MESH)

        mine = pltpu.make_async_copy(x_ref, rows(*me), local_sem)
        mine.start()
        first = [copy(0, me, sibling, src=x_ref)]
        first += [copy(1 + j, me, (*chip, c), src=x_ref) for j, chip in enumerate(chips)]
        for cp in first:
            cp.start()
        passed = [copy(4 + j, (*chip, c), sibling) for j, chip in enumerate(chips)]
        for j, chip in enumerate(chips):
            copy(1 + j, (*chip, c), me).wait_recv()
            passed[j].start()
        copy(0, sibling, me).wait_recv()
        for j, chip in enumerate(chips):
            copy(4 + j, (*chip, 1 - c), me).wait_recv()
        for cp in first + passed:
            cp.wait_send()
        mine.wait()

    return pl.pallas_call(
        body, out_shape=jax.ShapeDtypeStruct((N_DEV * m_per, n), v.dtype), in_specs=[VMEM_SPEC], out_specs=VMEM_SPEC,
        scratch_shapes=[pltpu.SemaphoreType.DMA((7,)), pltpu.SemaphoreType.DMA((7,)), pltpu.SemaphoreType.DMA],
        name=f"allgather8_{m_per}x{n}", compiler_params=_params(),
    )(v)


def _row_chunks(rows, tile_rows, want):
    n = want
    while n > 1 and rows % (n * tile_rows):
        n //= 2
    return [(i * (rows // n), rows // n) for i in range(n)]


def _remote(src, dst, send_sem, recv_sem, to):
    return pltpu.make_async_remote_copy(src_ref=src, dst_ref=dst, send_sem=send_sem, recv_sem=recv_sem, device_id=to,
                                        device_id_type=MESH)


def plane_allgather(big, small):
    rows = big.shape[0]
    rh = rows // 2
    tile = 16 if big.dtype == BF16 else 8
    ch_full = _row_chunks(rows, tile, 8)
    ch_half = _row_chunks(rh, tile, 4)

    def body(big_ref, small_ref, obig_ref, osmall_ref, send_sems, recv_sems, fwd_send, fwd_recv, own_send, own_recv):
        x, y, c = _me()
        me = 2 * x + y
        sibling = (x, y, 1 - c)
        mine = pl.ds(c * rh, rh)
        other = pl.ds((1 - c) * rh, rh)
        peers = [((x + fx) & 1, (y + fy) & 1) for fx, fy in _FLIPS]
        for st, sz in ch_full:
            sl = pl.ds(st, sz)
            _remote(big_ref.at[sl], obig_ref.at[me, sl], own_send.at[0], own_recv.at[0], sibling).start()
        _remote(small_ref, osmall_ref.at[me], own_send.at[1], own_recv.at[1], sibling).start()
        for j, (px, py) in enumerate(peers):
            for st, sz in ch_half:
                sl = pl.ds(c * rh + st, sz)
                _remote(big_ref.at[sl], obig_ref.at[me, sl], send_sems.at[j], recv_sems.at[j], (px, py, c)).start()
            _remote(small_ref, osmall_ref.at[me], send_sems.at[3 + j], recv_sems.at[3 + j], (px, py, c)).start()
        for j, (px, py) in enumerate(peers):
            pidx = 2 * px + py
            _remote(big_ref.at[mine], obig_ref.at[pidx, mine], send_sems.at[j], recv_sems.at[j], (px, py, c)).wait_recv()
            for st, sz in ch_half:
                sl = pl.ds(c * rh + st, sz)
                _remote(obig_ref.at[pidx, sl], obig_ref.at[pidx, sl], fwd_send.at[j], fwd_recv.at[j], sibling).start()
            _remote(small_ref, osmall_ref.at[pidx], send_sems.at[3 + j], recv_sems.at[3 + j], (px, py, c)).wait_recv()
        for j, (px, py) in enumerate(peers):
            pidx = 2 * px + py
            _remote(obig_ref.at[pidx, other], obig_ref.at[pidx, other], fwd_send.at[j], fwd_recv.at[j], sibling).wait_recv()
        for j, (px, py) in enumerate(peers):
            pidx = 2 * px + py
            _remote(big_ref.at[mine], obig_ref.at[me, mine], send_sems.at[j], recv_sems.at[j], (px, py, c)).wait_send()
            _remote(small_ref, osmall_ref.at[me], send_sems.at[3 + j], recv_sems.at[3 + j], (px, py, c)).wait_send()
            _remote(obig_ref.at[pidx, mine], obig_ref.at[pidx, mine], fwd_send.at[j], fwd_recv.at[j], sibling).wait_send()
        _remote(big_ref, obig_ref.at[me], own_send.at[0], own_recv.at[0], sibling).wait()
        _remote(small_ref, osmall_ref.at[me], own_send.at[1], own_recv.at[1], sibling).wait()

    return pl.pallas_call(
        body, out_shape=[jax.ShapeDtypeStruct((N_PLANE,) + big.shape, big.dtype),
                         jax.ShapeDtypeStruct((N_PLANE,) + small.shape, small.dtype)],
        in_specs=[ANY, ANY], out_specs=[ANY, ANY],
        scratch_shapes=[pltpu.SemaphoreType.DMA((6,)), pltpu.SemaphoreType.DMA((6,)), pltpu.SemaphoreType.DMA((3,)),
                        pltpu.SemaphoreType.DMA((3,)), pltpu.SemaphoreType.DMA((2,)), pltpu.SemaphoreType.DMA((2,))],
        name="plane_allgather", compiler_params=_params(),
    )(big, small)


def plane_scatter(p):
    tile = 16 if p.dtype == BF16 else 8
    chunks = _row_chunks(p.shape[1], tile, 4)

    def body(p_ref, out_ref, send_sems, recv_sems):
        x, y, c = _me()
        peers = [((x + fx) & 1, (y + fy) & 1) for fx, fy in _FLIPS]
        for j, (px, py) in enumerate(peers):
            for st, sz in chunks:
                sl = pl.ds(st, sz)
                _remote(p_ref.at[2 * px + py, sl], out_ref.at[j, sl], send_sems.at[j], recv_sems.at[j], (px, py, c)).start()
        for j, (px, py) in enumerate(peers):
            _remote(p_ref.at[0], out_ref.at[j], send_sems.at[j], recv_sems.at[j], (px, py, c)).wait_recv()
        for j, (px, py) in enumerate(peers):
            _remote(p_ref.at[0], out_ref.at[j], send_sems.at[j], recv_sems.at[j], (px, py, c)).wait_send()

    return pl.pallas_call(
        body, out_shape=jax.ShapeDtypeStruct((len(_FLIPS),) + p.shape[1:], p.dtype), in_specs=[ANY], out_specs=ANY,
        scratch_shapes=[pltpu.SemaphoreType.DMA((3,)), pltpu.SemaphoreType.DMA((3,))],
        name="plane_scatter", compiler_params=_params(),
    )(p)


def sibling_halves(buf):
    n_blk, _, rows, cols = buf.shape
    tile = 16 if buf.dtype == BF16 else 8
    chunks = _row_chunks(rows, tile, 2)

    def body(buf_ref, got_ref, send_sem, recv_sem):
        x, y, c = _me()
        for j in range(n_blk):
            for st, sz in chunks:
                sl = pl.ds(st, sz)
                _remote(buf_ref.at[j, 1 - c, sl], got_ref.at[j, sl], send_sem, recv_sem, (x, y, 1 - c)).start()
        _remote(got_ref, got_ref, send_sem, recv_sem, (x, y, 1 - c)).wait()

    return pl.pallas_call(
        body, out_shape=jax.ShapeDtypeStruct((n_blk, rows, cols), buf.dtype), in_specs=[ANY], out_specs=ANY,
        scratch_shapes=[pltpu.SemaphoreType.DMA, pltpu.SemaphoreType.DMA],
        name="sibling_halves", compiler_params=_params(),
    )(buf)


def sibling_swap(s):
    tile = 16 if s.dtype == BF16 else 8
    chunks = _row_chunks(s.shape[0], tile, 8)

    def body(s_ref, got_ref, send_sem, recv_sem):
        x, y, c = _me()
        for st, sz in chunks:
            sl = pl.ds(st, sz)
            _remote(s_ref.at[sl], got_ref.at[sl], send_sem, recv_sem, (x, y, 1 - c)).start()
        _remote(s_ref, got_ref, send_sem, recv_sem, (x, y, 1 - c)).wait()

    return pl.pallas_call(
        body, out_shape=jax.ShapeDtypeStruct(s.shape, s.dtype), in_specs=[ANY], out_specs=ANY,
        scratch_shapes=[pltpu.SemaphoreType.DMA, pltpu.SemaphoreType.DMA],
        name="sibling_swap", compiler_params=_params(),
    )(s)


def _heads(t, n_heads):
    b, s, w = t.shape
    return jnp.transpose(t.reshape(b, s, n_heads, w // n_heads), (0, 2, 1, 3)).reshape(b * n_heads, s, w // n_heads)


def _unheads(t, b):
    bh, s, d = t.shape
    return jnp.transpose(t.reshape(b, bh // b, s, d), (0, 2, 1, 3)).reshape(b, s, (bh // b) * d)


def _op(cache, fn, name, kinds, out_dims, **kw):
    key = (name, tuple(out_dims), tuple(sorted(kw.items())))
    if key not in cache:
        cache[key] = make_rowwise(fn, name, kinds, out_dims, **kw)
    return cache[key]


def _even_mixer(ops, a, w, n_ctx):
    b, s, d = a.shape
    proj = linear(a.reshape(b * s, d), w["e_w_in"]).reshape(b, s, -1)
    q, k, v, u = jnp.split(proj, [GQA_Q_W, GQA_Q_W + GQA_KV_W, GQA_Q_W + 2 * GQA_KV_W], axis=-1)
    cos, sin = _rope_tables(n_ctx, s - n_ctx, HEAD_DIM, 0, HEAD_DIM)
    rot = jnp.asarray(_rope_matrix(HEAD_DIM, 0, HEAD_DIM))
    qn = head_norm_rope(q, w["e_g_q"][None], cos, sin, rot, GQA_Q_HEADS)
    kn = head_norm_rope(k, w["e_g_k"][None], cos, sin, rot, GQA_KV_HEADS)
    att = gqa_attention(qn, kn, v, n_ctx)
    ys = []
    for dr in range(2):
        a_re, a_im, bb_re, bb_im = _s5_discretize(w["ssm_lam_re"][dr], w["ssm_lam_im"][dr], w["ssm_log_dt"][dr],
                                                  w["ssm_b_re"][dr], w["ssm_b_im"][dr])
        a_flat = jnp.stack([a_re.reshape(-1), a_im.reshape(-1)])
        b_bd = jnp.stack([_block_diag(jnp.swapaxes(bb_re, 1, 2)), _block_diag(jnp.swapaxes(bb_im, 1, 2))])
        c_bd = jnp.stack([_block_diag(jnp.swapaxes(w["ssm_c_re"][dr], 1, 2)),
                          _block_diag(jnp.swapaxes(w["ssm_c_im"][dr], 1, 2))])
        ys.append(s5_direction(u, a_flat, b_bd, c_bd, n_ctx, dr == 1))
    pre = _op(ops, _fn_glu_pre, "glu_pre", ("row", "row", "row", "glob"), (SSM_WIDTH,))
    post = _op(ops, _fn_glu_post, "glu_post", ("row", "row", "glob"), (SSM_WIDTH,))
    z = pre(u, ys[0], ys[1], w["ssm_d"][None])[0]
    t = linear(z.reshape(b * s, SSM_WIDTH), w["ssm_w_glu"]).reshape(b, s, SSM_WIDTH)
    ssm = post(z, t, w["ssm_b_glu"][None])[0]
    mix = jnp.concatenate([att, ssm], axis=-1)
    return linear(mix.reshape(b * s, -1), w["e_w_out"]).reshape(b, s, d)


def _odd_mixer(ops, a, w, n_ctx):
    b, s, d = a.shape
    w_in = jnp.pad(w["o_w_in"], ((0, 0), (0, ODD_IN_PAD - ODD_IN_W)))
    proj = linear(a.reshape(b * s, d), w_in).reshape(b, s, -1)
    c1 = MLA_Q_RANK
    c2 = c1 + MLA_KV_RANK
    c3 = c2 + MLA_ROPE
    cq, ckv, kr, nq, nk, nv, _ = jnp.split(proj, [c1, c2, c3, c3 + NA_W, c3 + 2 * NA_W, ODD_IN_W], axis=-1)
    nrm = lambda wd: _op(ops, _fn_norm, f"norm{wd}", ("row", "glob"), (wd,))
    cqn = nrm(MLA_Q_RANK)(cq, w["mla_g_cq"][None])[0]
    ckvn = nrm(MLA_KV_RANK)(ckv, w["mla_g_ckv"][None])[0]
    q = linear(cqn.reshape(b * s, -1), w["mla_w_uq"]).reshape(b, s, -1)
    kv = linear(ckvn.reshape(b * s, -1), w["mla_w_ukv"]).reshape(b, s, -1)
    cos, sin = _rope_tables(n_ctx, s - n_ctx, MLA_QK, MLA_NOPE, MLA_ROPE)
    rot = jnp.asarray(_rope_matrix(MLA_QK, MLA_NOPE, MLA_ROPE))
    mq = head_norm_rope(q, w["mla_g_q"][None], cos, sin, rot, MLA_HEADS)
    mk = mla_k_prep(kv, kr, w["mla_g_k"][None], cos, sin, rot)
    mla = mla_attention(mq, mk, kv, n_ctx)
    nqn = head_norm_rope(nq, w["na_g_q"][None], None, None, None, NA_HEADS)
    nkn = head_norm_rope(nk, w["na_g_k"][None], None, None, None, NA_HEADS)
    na = na_attention_tm(nqn, nkn, nv, na_bias_table(w["na_rpb"]), n_ctx)
    mix = jnp.concatenate([mla, na], axis=-1)
    return linear(mix.reshape(b * s, -1), w["o_w_out"]).reshape(b, s, d)


_EVEN_KEYS = ("e_w_in", "e_w_out", "e_g_q", "e_g_k", "ssm_lam_re", "ssm_lam_im", "ssm_log_dt", "ssm_b_re", "ssm_b_im",
              "ssm_c_re", "ssm_c_im", "ssm_d", "ssm_w_glu", "ssm_b_glu")
_ODD_KEYS = ("o_w_in", "o_w_out", "mla_g_cq", "mla_g_ckv", "mla_w_uq", "mla_w_ukv", "mla_g_q", "mla_g_k", "na_g_q",
             "na_g_k", "na_rpb")


def _trunk(x_all, mods, w, n_ctx):
    ops = {}
    depth = mods.shape[0]
    b, s, d = x_all.shape
    modulate = _op(ops, _fn_modulate, "modulate", ("row", "glob", "seg", "seg"), (d,), nctx_rows=n_ctx)
    gated = make_gated_add(d, n_ctx)
    x = x_all
    for i in range(depth):
        j = i // 2
        m = [mods[i][:, :, r:r + 1, :] for r in range(N_MOD)]
        a = modulate(x, w["g_norm1"][i][None], m[0], m[1])[0]
        if i % 2 == 0:
            o = _even_mixer(ops, a, {k: w[k][j] for k in _EVEN_KEYS}, n_ctx)
        else:
            o = _odd_mixer(ops, a, {k: w[k][j] for k in _ODD_KEYS}, n_ctx)
        x = gated(x, o, m[2])
        a2 = modulate(x, w["g_norm2"][i][None], m[3], m[4])[0]
        f = ffn(a2.reshape(b * s, d), w["w_ff1"][i], w["w_ff2"][i]).reshape(b, s, d)
        x = gated(x, f, m[5])
    return x[:, n_ctx:]


def local_step(x, ctx, mods, w, loss_target):
    n_ctx = ctx.shape[1]
    x_all = jnp.concatenate([ctx, x], axis=1)
    y, vjp = jax.vjp(lambda xa, md, ww: _trunk(xa, md, ww, n_ctx), x_all, mods, w)
    loss_tile, dy = _loss_head(y, loss_target)
    dx_all, dmods, dw = vjp(dy)
    return loss_tile[0, 0], dx_all[:, n_ctx:], dmods, dw


_SHARDED = (("w_ff1", 2), ("w_ff2", 1), ("e_w_in", 2), ("e_w_out", 1), ("o_w_in", 2), ("o_w_out", 1),
            ("mla_w_uq", 2), ("mla_w_ukv", 2), ("ssm_w_glu", 1))
_SHARDED_SMALL = (("mla_g_cq", 1), ("mla_g_ckv", 1))
_REPLICATED = ("g_norm1", "g_norm2", "e_g_q", "e_g_k", "ssm_lam_re", "ssm_lam_im", "ssm_log_dt", "ssm_b_re", "ssm_b_im",
               "ssm_c_re", "ssm_c_im", "ssm_d", "ssm_b_glu", "mla_g_q", "mla_g_k", "na_g_q", "na_g_k", "na_rpb")
_WEIGHTS = ("c_ctx", "w_mod", "b_mod", "g_norm1", "g_norm2", "w_ff1", "w_ff2", "e_w_in", "e_w_out", "e_g_q", "e_g_k",
            "ssm_lam_re", "ssm_lam_im", "ssm_log_dt", "ssm_b_re", "ssm_b_im", "ssm_c_re", "ssm_c_im", "ssm_d",
            "ssm_w_glu", "ssm_b_glu", "o_w_in", "o_w_out", "mla_g_cq", "mla_g_ckv", "mla_w_uq", "mla_w_ukv", "mla_g_q",
            "mla_g_k", "na_g_q", "na_g_k", "na_rpb")
_PACK_ROWS = 64


def _pack(arrs, dtype, cols=1024, row_mult=_PACK_ROWS):
    flat = jnp.concatenate([a.reshape(-1).astype(dtype) for a in arrs])
    unit = cols * row_mult
    pad = (-flat.shape[0]) % unit
    return jnp.pad(flat, (0, pad)).reshape(-1, cols)


def _unpack(flat, shapes):
    flat = flat.reshape(-1)
    out, off = [], 0
    for sh in shapes:
        n = int(np.prod(sh))
        out.append(flat[off:off + n].reshape(sh))
        off += n
    return out


def _silu(t):
    return t * jax.nn.sigmoid(t)


def kernel(x, c, ctx, c_ctx, w_mod, b_mod, g_norm1, g_norm2, w_ff1, w_ff2, e_w_in, e_w_out, e_g_q, e_g_k, ssm_lam_re, ssm_lam_im, ssm_log_dt, ssm_b_re, ssm_b_im, ssm_c_re, ssm_c_im, ssm_d, ssm_w_glu, ssm_b_glu, o_w_in, o_w_out, mla_g_cq, mla_g_ckv, mla_w_uq, mla_w_ukv, mla_g_q, mla_g_k, na_g_q, na_g_k, na_rpb, loss_target, m_c_ctx, m_w_mod, m_b_mod, m_g_norm1, m_g_norm2, m_w_ff1, m_w_ff2, m_e_w_in, m_e_w_out, m_e_g_q, m_e_g_k, m_ssm_lam_re, m_ssm_lam_im, m_ssm_log_dt, m_ssm_b_re, m_ssm_b_im, m_ssm_c_re, m_ssm_c_im, m_ssm_d, m_ssm_w_glu, m_ssm_b_glu, m_o_w_in, m_o_w_out, m_mla_g_cq, m_mla_g_ckv, m_mla_w_uq, m_mla_w_ukv, m_mla_g_q, m_mla_g_k, m_na_g_q, m_na_g_k, m_na_rpb, v_c_ctx, v_w_mod, v_b_mod, v_g_norm1, v_g_norm2, v_w_ff1, v_w_ff2, v_e_w_in, v_e_w_out, v_e_g_q, v_e_g_k, v_ssm_lam_re, v_ssm_lam_im, v_ssm_log_dt, v_ssm_b_re, v_ssm_b_im, v_ssm_c_re, v_ssm_c_im, v_ssm_d, v_ssm_w_glu, v_ssm_b_glu, v_o_w_in, v_o_w_out, v_mla_g_cq, v_mla_g_ckv, v_mla_w_uq, v_mla_w_ukv, v_mla_g_q, v_mla_g_k, v_na_g_q, v_na_g_k, v_na_rpb):
    env = dict(locals())
    weights = {n: env[n] for n in _WEIGHTS}
    mom_m = {n: env["m_" + n] for n in _WEIGHTS}
    mom_v = {n: env["v_" + n] for n in _WEIGHTS}
    ax, ay, ac = _me()
    plane = 2 * ax + ay
    dev = 4 * ax + 2 * ay + ac
    b_loc, d = c.shape
    depth = w_mod.shape[0]
    n_all = N_DEV * b_loc
    mod_cols = w_mod.shape[2]

    big = _pack([weights[n] for n, _ in _SHARDED], BF16)
    small = _pack([weights[n] for n, _ in _SHARDED_SMALL], F32, cols=LANE, row_mult=8)
    g_big, g_small = plane_allgather(big, small)
    full = {n: weights[n] for n in _REPLICATED}
    parts = [_unpack(g_big[j], [weights[n].shape for n, _ in _SHARDED]) for j in range(N_PLANE)]
    for t, (n, axis) in enumerate(_SHARDED):
        full[n] = [jnp.concatenate([parts[j][t][l] for j in range(N_PLANE)], axis=axis - 1).astype(F32)
                   for l in range(weights[n].shape[0])]
    parts_s = [_unpack(g_small[j], [weights[n].shape for n, _ in _SHARDED_SMALL]) for j in range(N_PLANE)]
    for t, (n, axis) in enumerate(_SHARDED_SMALL):
        full[n] = jnp.concatenate([parts_s[j][t] for j in range(N_PLANE)], axis=axis)

    rows_pad = 8 * ((n_all + 1 + 7) // 8)
    c_all = allgather8(jnp.pad(c, ((0, 8 - b_loc), (0, 0)))).reshape(N_DEV, 8, d)[:, :b_loc].reshape(n_all, d)
    cond_raw = jnp.concatenate([c_all, c_ctx[None], jnp.zeros((rows_pad - n_all - 1, d), F32)], axis=0)
    b_cols = lax.dynamic_slice_in_dim(b_mod, plane * mod_cols, mod_cols, axis=1)
    mod_loc = jnp.stack([_mm(cond_raw, w_mod[i], a_act="silu") + b_cols[i][None] for i in range(depth)])
    mod_g = allgather8(mod_loc.reshape(depth * rows_pad, mod_cols)).reshape(N_PLANE, 2, depth, rows_pad, mod_cols)
    mod_all = jnp.concatenate([mod_g[j, 0] for j in range(N_PLANE)], axis=-1)
    m_lat = lax.dynamic_slice_in_dim(mod_all, dev * b_loc, b_loc, axis=1)
    m_ctx = jnp.broadcast_to(mod_all[:, n_all][:, None], m_lat.shape)
    mods = jnp.stack([m_ctx, m_lat], axis=2).reshape(depth, b_loc, 2, N_MOD, d)

    loss_part, grad_x, dmods, dw = local_step(x, ctx, mods, full, loss_target)
    loss = lax.psum(loss_part, ("x", "y", "c"))

    dm = dmods.reshape(depth, b_loc, 2, N_MOD * d)
    dm_rows = jnp.concatenate([dm[:, :, 1], jnp.sum(dm[:, :, 0], axis=1, keepdims=True)], axis=1)
    rep_shapes = [weights[n].shape for n in _REPLICATED]
    small_pack = _pack([dm_rows] + [dw[n] for n in _REPLICATED], F32, cols=1024, row_mult=8)
    sp_rows = small_pack.shape[0]
    gathered = allgather8(small_pack).reshape(N_DEV, sp_rows, 1024)
    n_dm = depth * (b_loc + 1) * N_MOD * d
    dm_all = gathered.reshape(N_DEV, -1)[:, :n_dm].reshape(N_DEV, depth, b_loc + 1, N_MOD * d)
    rep_sum = _sum_rows(gathered, N_DEV).reshape(-1)
    rep_grads = dict(zip(_REPLICATED, _unpack(rep_sum[n_dm:], rep_shapes)))
    d_ctx_row = rep_sum[:n_dm].reshape(depth, b_loc + 1, N_MOD * d)[:, b_loc]
    d_lat_rows = jnp.transpose(dm_all[:, :, :b_loc], (1, 0, 2, 3)).reshape(depth, n_all, N_MOD * d)
    d_mod_all = jnp.concatenate([d_lat_rows, d_ctx_row[:, None],
                                 jnp.zeros((depth, rows_pad - n_all - 1, N_MOD * d), F32)], axis=1)
    grads = dict(rep_grads)
    grads["b_mod"] = jnp.sum(d_mod_all, axis=1)
    d_cols = lax.dynamic_slice_in_dim(d_mod_all, plane * mod_cols, mod_cols, axis=2)
    grads["w_mod"] = jnp.stack([_mm(cond_raw, d_cols[i], ta=True, a_act="silu") for i in range(depth)])
    d_cond = _mm(d_cols[0], w_mod[0], tb=True)
    for i in range(1, depth):
        d_cond = _add2(d_cond, _mm(d_cols[i], w_mod[i], tb=True))
    d_cond_g = allgather8(d_cond[n_all:n_all + 8] if rows_pad - n_all >= 8 else
                          jnp.pad(d_cond[n_all:], ((0, 8 - (rows_pad - n_all)), (0, 0)))).reshape(N_PLANE, 2, 8, d)
    d_silu = _sum_rows(d_cond_g[:, 0], N_PLANE)[0]
    sg = jax.nn.sigmoid(c_ctx)
    grads["c_ctx"] = d_silu * (sg * (1.0 + c_ctx * (1.0 - sg)))

    def shards_of(g, axis, j):
        layers = g if isinstance(g, (list, tuple)) else [g]
        ax = axis - 1 if isinstance(g, (list, tuple)) else axis
        n = layers[0].shape[ax] // N_PLANE
        return [lax.slice_in_dim(t, j * n, (j + 1) * n, axis=ax) for t in layers]

    send = jnp.stack([_pack([t for n, axis in _SHARDED + _SHARDED_SMALL for t in shards_of(dw[n], axis, j)], BF16)
                      for j in range(N_PLANE)])
    rows_h = send.shape[1] // 2
    send = send.reshape(N_PLANE, 2, rows_h, 1024)
    mine = lax.dynamic_index_in_dim(send, ac, 1, keepdims=False).reshape(N_PLANE * rows_h, 1024)
    theirs = sibling_halves(send).reshape(N_PLANE * rows_h, 1024)
    chip_sum = _accumulate([mine, theirs], BF16).reshape(N_PLANE, rows_h, 1024)
    own = lax.dynamic_index_in_dim(chip_sum, plane, 0, keepdims=False)
    done = _accumulate([own, plane_scatter(chip_sum)], BF16)
    both = jnp.stack([done, sibling_swap(done)])
    flat = jnp.where(ac == 0, both, both[::-1]).astype(F32).reshape(-1)
    shard_shapes = [weights[n].shape for n, _ in _SHARDED] + [weights[n].shape for n, _ in _SHARDED_SMALL]
    for (n, _), g in zip(_SHARDED + _SHARDED_SMALL, _unpack(flat, shard_shapes)):
        grads[n] = g

    big_names = ("w_mod",) + tuple(n for n, _ in _SHARDED)
    small_names = tuple(n for n in _WEIGHTS if n not in big_names)
    delta, new_m, new_v = {}, {}, {}
    for n in big_names:
        delta[n], new_m[n], new_v[n] = _adamw(weights[n], grads[n], mom_m[n], mom_v[n])
    sm_shapes = [weights[n].shape for n in small_names]
    packed = [_pack([src[n] for n in small_names], F32, cols=1024, row_mult=8)
              for src in (weights, grads, mom_m, mom_v)]
    for dst, res in zip((delta, new_m, new_v), _adamw(*packed)):
        dst.update(dict(zip(small_names, _unpack(res, sm_shapes))))

    return (loss, grad_x, *[grads[n] for n in _WEIGHTS], *[delta[n] for n in _WEIGHTS],
            *[new_m[n] for n in _WEIGHTS], *[new_v[n] for n in _WEIGHTS])
```

```python
import functools
import math

import numpy as np
import jax
import jax.numpy as jnp
from jax import lax
from jax.experimental import pallas as pl
from jax.experimental.pallas import tpu as pltpu

F32 = jnp.float32
BF16 = jnp.bfloat16
HI = lax.Precision.HIGHEST
MESH = pl.DeviceIdType.MESH
ANY = pl.BlockSpec(memory_space=pl.ANY)
VMEM_SPEC = pl.BlockSpec(memory_space=pltpu.VMEM)

GRID_W = 64
HEAD_DIM = 64
ROPE_BASE = 10000.0
EPS = 1e-6
N_MOD = 6
GQA_Q_HEADS, GQA_KV_HEADS = 12, 4
GQA_Q_W, GQA_KV_W = GQA_Q_HEADS * HEAD_DIM, GQA_KV_HEADS * HEAD_DIM
SSM_WIDTH, SSM_GROUP, SSM_STATE = 256, 16, 64
SSM_GROUPS = SSM_WIDTH // SSM_GROUP
SSM_LANES = SSM_GROUPS * SSM_STATE
MLA_HEADS, MLA_Q_RANK, MLA_KV_RANK, MLA_NOPE, MLA_ROPE, MLA_V = 8, 512, 256, 64, 32, 64
MLA_QK = MLA_NOPE + MLA_ROPE
NA_HEADS, NA_WIN_R, NA_WIN_C = 8, 8, 16
NA_W = NA_HEADS * HEAD_DIM
NA_BAND = NA_WIN_R * GRID_W
ODD_IN_W = MLA_Q_RANK + MLA_KV_RANK + MLA_ROPE + 3 * NA_W
ODD_IN_PAD = 2560
ADAM_LR, ADAM_B1, ADAM_B2, ADAM_EPS, ADAM_WD, ADAM_STEP = 0.001, 0.9, 0.999, 1e-08, 0.01, 10
NEG = -1e30
VMEM_LIMIT = 56 * 1024 * 1024
LANE = 128
MM_TILE_M = (1152, 1024, 768, 512, 256, 128)
MM_TILE_N = (1280, 1024, 768, 512, 256, 128)
MM_TILE_K = (1152, 1024, 768, 512, 256, 128)
N_PLANE = 4
N_DEV = 8


def _pick(n, cands):
    for c in cands:
        if n % c == 0:
            return c
    return n


def _params(**kw):
    return pltpu.CompilerParams(vmem_limit_bytes=VMEM_LIMIT, **kw)


def _mm(a, b, *, ta=False, tb=False, a_act=None, epi=None, e=None, exact=False):
    m, kd = (a.shape[1], a.shape[0]) if ta else a.shape
    n = b.shape[0] if tb else b.shape[1]
    tm = _pick(m, MM_TILE_M)
    tn = _pick(n, MM_TILE_N)
    tk = _pick(kd, MM_TILE_K)
    nk = kd // tk
    dn = (((0 if ta else 1,), (1 if tb else 0,)), ((), ()))

    def body(*refs):
        if epi is None:
            a_ref, b_ref, o_ref = refs
        else:
            a_ref, b_ref, e_ref, o_ref = refs
        k = pl.program_id(2)
        av = a_ref[...]
        if a_act == "relu2":
            av = jnp.square(jnp.maximum(av, 0.0))
        elif a_act == "silu":
            av = av * jax.nn.sigmoid(av)
        bv = b_ref[...]
        if exact:
            p = lax.dot_general(av, bv, dn, precision=HI, preferred_element_type=F32)
        else:
            p = lax.dot_general(av.astype(BF16), bv.astype(BF16), dn, preferred_element_type=F32)

        @pl.when(k == 0)
        def _():
            o_ref[...] = p

        @pl.when(k > 0)
        def _():
            o_ref[...] += p

        if epi == "drelu2":
            @pl.when(k == nk - 1)
            def _():
                o_ref[...] = o_ref[...] * (2.0 * jnp.maximum(e_ref[...], 0.0))

    a_spec = pl.BlockSpec((tk, tm), lambda i, j, k: (k, i)) if ta else pl.BlockSpec((tm, tk), lambda i, j, k: (i, k))
    b_spec = pl.BlockSpec((tn, tk), lambda i, j, k: (j, k)) if tb else pl.BlockSpec((tk, tn), lambda i, j, k: (k, j))
    o_spec = pl.BlockSpec((tm, tn), lambda i, j, k: (i, j))
    ins, specs = [a, b], [a_spec, b_spec]
    if epi is not None:
        ins.append(e)
        specs.append(o_spec)
    name = f"mm_{m}x{kd}x{n}_{int(ta)}{int(tb)}_{a_act}_{epi}_{int(exact)}"
    return pl.pallas_call(
        body, out_shape=jax.ShapeDtypeStruct((m, n), F32), grid=(m // tm, n // tn, nk),
        in_specs=specs, out_specs=o_spec, name=name, compiler_params=_params(),
    )(*ins)


@functools.partial(jax.custom_vjp, nondiff_argnums=(2,))
def _linear(a, w, exact):
    return _mm(a, w, exact=exact)


def _linear_fwd(a, w, exact):
    return _mm(a, w, exact=exact), (a, w)


def _linear_bwd(exact, res, g):
    a, w = res
    return _mm(g, w, tb=True, exact=exact), _mm(a, g, ta=True, exact=exact)


_linear.defvjp(_linear_fwd, _linear_bwd)


def linear(a, w, exact=False):
    return _linear(a, w, exact)


@jax.custom_vjp
def ffn(a, w1, w2):
    return _mm(_mm(a, w1), w2, a_act="relu2")


def _ffn_fwd(a, w1, w2):
    h1 = _mm(a, w1)
    return _mm(h1, w2, a_act="relu2"), (a, w1, w2, h1)


def _ffn_bwd(res, g):
    a, w1, w2, h1 = res
    dh1 = _mm(g, w2, tb=True, epi="drelu2", e=h1)
    dw2 = _mm(h1, g, ta=True, a_act="relu2")
    return _mm(dh1, w1, tb=True), _mm(a, dh1, ta=True), dw2


ffn.defvjp(_ffn_fwd, _ffn_bwd)


def make_rowwise(fn, name, kinds, out_dims, nctx_rows=0, whole_seq=False):
    n_in = len(kinds)
    n_out = len(out_dims)
    diff = [i for i, kd in enumerate(kinds) if kd in ("row", "glob", "seg")]

    def layout(args):
        row0 = args[kinds.index("row")]
        g, s = row0.shape[0], row0.shape[1]
        ts = s if whole_seq else (min(256, nctx_rows) if nctx_rows else _pick(s, (256, 128, 64)))
        nctx = nctx_rows // ts
        return g, s, ts, nctx

    def spec_of(kind, arr, ts, nctx):
        if kind == "row":
            return pl.BlockSpec((None, ts, arr.shape[2]), lambda g, i: (g, i, 0))
        if kind == "tab":
            return pl.BlockSpec((ts, arr.shape[1]), lambda g, i: (i, 0))
        if kind in ("const", "glob"):
            return pl.BlockSpec(arr.shape, lambda g, i: (0, 0))
        return pl.BlockSpec((None, None) + arr.shape[2:], lambda g, i: (g, (i >= nctx).astype(jnp.int32), 0, 0))

    def fwd_call(*args):
        g, s, ts, nctx = layout(args)

        def body(*refs):
            vals = [r[...] for r in refs[:n_in]]
            outs = fn(*vals)
            for o_ref, o in zip(refs[n_in:], outs):
                o_ref[...] = o

        return pl.pallas_call(
            body, out_shape=[jax.ShapeDtypeStruct((g, s, d), F32) for d in out_dims], grid=(g, s // ts),
            in_specs=[spec_of(kd, a, ts, nctx) for kd, a in zip(kinds, args)],
            out_specs=[pl.BlockSpec((None, ts, d), lambda g_, i: (g_, i, 0)) for d in out_dims],
            name=f"{name}_f_{g}x{s}", compiler_params=_params(),
        )(*args)

    def bwd_call(args, cts):
        g, s, ts, nctx = layout(args)

        def body(*refs):
            in_refs, ct_refs, out_refs = refs[:n_in], refs[n_in:n_in + n_out], refs[n_in + n_out:]
            gi, i = pl.program_id(0), pl.program_id(1)
            vals = [r[...] for r in in_refs]

            def f(*dv):
                full = list(vals)
                for idx, v in zip(diff, dv):
                    full[idx] = v
                return tuple(fn(*full))

            _, vjp = jax.vjp(f, *[vals[idx] for idx in diff])
            grads = vjp(tuple(r[...] for r in ct_refs))
            for idx, o_ref, gr in zip(diff, out_refs, grads):
                if kinds[idx] == "row":
                    o_ref[...] = gr
                    continue
                if kinds[idx] == "glob":
                    first = jnp.logical_and(gi == 0, i == 0)
                else:
                    first = jnp.logical_or(i == 0, i == nctx)

                @pl.when(first)
                def _(o_ref=o_ref, gr=gr):
                    o_ref[...] = gr

                @pl.when(jnp.logical_not(first))
                def _(o_ref=o_ref, gr=gr):
                    o_ref[...] += gr

        in_specs = [spec_of(kd, a, ts, nctx) for kd, a in zip(kinds, args)]
        in_specs += [pl.BlockSpec((None, ts, d), lambda g_, i: (g_, i, 0)) for d in out_dims]
        return pl.pallas_call(
            body, out_shape=[jax.ShapeDtypeStruct(args[idx].shape, F32) for idx in diff], grid=(g, s // ts),
            in_specs=in_specs, out_specs=[spec_of(kinds[idx], args[idx], ts, nctx) for idx in diff],
            name=f"{name}_b_{g}x{s}", compiler_params=_params(),
        )(*args, *cts)

    @jax.custom_vjp
    def op(*args):
        return tuple(fwd_call(*args))

    def op_fwd(*args):
        return tuple(fwd_call(*args)), args

    def op_bwd(args, cts):
        grads = bwd_call(args, cts)
        full = [None] * n_in
        for idx, gr in zip(diff, grads):
            full[idx] = gr
        return tuple(jnp.zeros_like(a) if gfull is None else gfull for a, gfull in zip(args, full))

    op.defvjp(op_fwd, op_bwd)
    op.fwd_call, op.bwd_call = fwd_call, bwd_call
    return op


def make_gated_add(d, n_ctx):
    add = make_rowwise(_fn_gated_add, "gated", ("row", "row", "seg"), (d,), nctx_rows=n_ctx)
    mul = make_rowwise(_fn_gate_mul, "gate_mul", ("row", "seg"), (d,), nctx_rows=n_ctx)

    @jax.custom_vjp
    def op(x, o, gate):
        return add.fwd_call(x, o, gate)[0]

    def fwd(x, o, gate):
        return add.fwd_call(x, o, gate)[0], (o, gate)

    def bwd(res, ct):
        do, dgate = mul.bwd_call(res, (ct,))
        return ct, do, dgate

    op.defvjp(fwd, bwd)
    return op


def _rms(x):
    return lax.rsqrt(jnp.mean(x * x, axis=-1, keepdims=True) + EPS)


def _fn_modulate(x, g, shift, scale):
    return ((x * _rms(x) * g) * (1.0 + scale) + shift,)


def _fn_gated_add(x, o, gate):
    return (x + gate * o,)


def _fn_gate_mul(o, gate):
    return (gate * o,)


def _fn_norm(x, g):
    return (x * _rms(x) * g,)


def _fn_norm_rope(x, cos, sin, rot, g):
    y = x * _rms(x) * g
    r = jnp.dot(y, rot, precision=HI, preferred_element_type=F32)
    return (y * cos + r * sin,)


def _fn_glu_pre(u, y0, y1, d):
    return (jax.nn.gelu(d * u + y0 + y1),)


def _fn_glu_post(z, t, bg):
    return (z * jax.nn.sigmoid(t + bg),)


def _rope_matrix(dh, start, rot_dim):
    r = np.zeros((dh, dh), np.float32)
    q = rot_dim // 4
    for j in range(rot_dim):
        if (j // q) % 2 == 0:
            r[start + j + q, start + j] = -1.0
        else:
            r[start + j - q, start + j] = 1.0
    return r


def _rope_tables(n_ctx, n_lat, dh, start, rot_dim):
    t = jnp.arange(n_lat)
    rows = (t // GRID_W).astype(F32)
    cols = (t % GRID_W).astype(F32)
    axis_dim = rot_dim // 2
    freqs = ROPE_BASE ** (-jnp.arange(0, axis_dim, 2, dtype=F32) / axis_dim)
    ang_r = rows[:, None] * freqs
    ang_c = cols[:, None] * freqs
    ang = jnp.concatenate([ang_r, ang_r, ang_c, ang_c], axis=-1)
    cos = jnp.concatenate([jnp.ones((n_lat, start), F32), jnp.cos(ang)], axis=-1)
    sin = jnp.concatenate([jnp.zeros((n_lat, start), F32), jnp.sin(ang)], axis=-1)
    cos = jnp.concatenate([jnp.ones((n_ctx, dh), F32), cos], axis=0)
    sin = jnp.concatenate([jnp.zeros((n_ctx, dh), F32), sin], axis=0)
    return cos, sin


_NT = (((1,), (1,)), ((), ()))
_TN = (((0,), (0,)), ((), ()))


def _attn_fwd(q, k, v, group, n_ctx, scale):
    b, h, s, dq = q.shape
    dv = v.shape[-1]
    tq = min(256, n_ctx)
    nc = n_ctx // tq

    def body(q_ref, k_ref, v_ref, o_ref, lse_ref):
        qv = (q_ref[...] * scale).astype(BF16)

        def run(n_keys):
            sc = lax.dot_general(qv, k_ref[0:n_keys, :].astype(BF16), _NT, preferred_element_type=F32)
            m = jnp.max(sc, axis=-1, keepdims=True)
            p = jnp.exp(sc - m)
            l = jnp.sum(p, axis=-1, keepdims=True)
            o = jnp.dot(p.astype(BF16), v_ref[0:n_keys, :].astype(BF16), preferred_element_type=F32)
            o_ref[...] = o / l
            lse_ref[...] = m + jnp.log(l)

        pl.when(pl.program_id(2) < nc)(lambda: run(n_ctx))
        pl.when(pl.program_id(2) >= nc)(lambda: run(s))

    return pl.pallas_call(
        body, out_shape=[jax.ShapeDtypeStruct((b, h, s, dv), F32), jax.ShapeDtypeStruct((b, h, s, 1), F32)],
        grid=(b, h, s // tq),
        in_specs=[pl.BlockSpec((None, None, tq, dq), lambda bi, hi, i: (bi, hi, i, 0)),
                  pl.BlockSpec((None, None, s, dq), lambda bi, hi, i: (bi, lax.div(hi, group), 0, 0)),
                  pl.BlockSpec((None, None, s, dv), lambda bi, hi, i: (bi, lax.div(hi, group), 0, 0))],
        out_specs=[pl.BlockSpec((None, None, tq, dv), lambda bi, hi, i: (bi, hi, i, 0)),
                   pl.BlockSpec((None, None, tq, 1), lambda bi, hi, i: (bi, hi, i, 0))],
        name=f"attn_f_{h}x{s}x{dq}", compiler_params=_params(),
    )(q, k, v)


def _attn_dq(q, k, v, o, lse, do, group, n_ctx, scale):
    b, h, s, dq = q.shape
    dv = v.shape[-1]
    tq = min(256, n_ctx)
    nc = n_ctx // tq

    def body(q_ref, k_ref, v_ref, o_ref, lse_ref, do_ref, dq_ref, delta_ref):
        qv = (q_ref[...] * scale).astype(BF16)
        dov = do_ref[...]
        delta = jnp.sum(dov * o_ref[...], axis=-1, keepdims=True)
        delta_ref[...] = delta

        def run(n_keys):
            kv = k_ref[0:n_keys, :].astype(BF16)
            sc = lax.dot_general(qv, kv, _NT, preferred_element_type=F32)
            p = jnp.exp(sc - lse_ref[...])
            dp = lax.dot_general(dov.astype(BF16), v_ref[0:n_keys, :].astype(BF16), _NT, preferred_element_type=F32)
            ds = p * (dp - delta)
            dq_ref[...] = jnp.dot(ds.astype(BF16), kv, preferred_element_type=F32) * scale

        pl.when(pl.program_id(2) < nc)(lambda: run(n_ctx))
        pl.when(pl.program_id(2) >= nc)(lambda: run(s))

    qs = lambda d: pl.BlockSpec((None, None, tq, d), lambda bi, hi, i: (bi, hi, i, 0))
    ks = lambda d: pl.BlockSpec((None, None, s, d), lambda bi, hi, i: (bi, lax.div(hi, group), 0, 0))
    return pl.pallas_call(
        body, out_shape=[jax.ShapeDtypeStruct((b, h, s, dq), F32), jax.ShapeDtypeStruct((b, h, s, 1), F32)],
        grid=(b, h, s // tq),
        in_specs=[qs(dq), ks(dq), ks(dv), qs(dv), qs(1), qs(dv)], out_specs=[qs(dq), qs(1)],
        name=f"attn_dq_{h}x{s}x{dq}", compiler_params=_params(),
    )(q, k, v, o, lse, do)


def _attn_dkv(q, k, v, lse, delta, do, group, n_ctx, scale):
    b, h, s, dq = q.shape
    hk = k.shape[1]
    dv = v.shape[-1]
    tk = min(256, n_ctx)
    nc = n_ctx // tk

    def body(q_ref, k_ref, v_ref, lse_ref, delta_ref, do_ref, dk_ref, dv_ref):
        kv = k_ref[...].astype(BF16)
        vv = v_ref[...].astype(BF16)

        def run(r0):
            dk = jnp.zeros((tk, dq), F32)
            dvv = jnp.zeros((tk, dv), F32)
            for g in range(group):
                qg = (q_ref[g, r0:s, :] * scale).astype(BF16)
                dog = do_ref[g, r0:s, :].astype(BF16)
                sc = lax.dot_general(qg, kv, _NT, preferred_element_type=F32)
                p = jnp.exp(sc - lse_ref[g, r0:s, :])
                dvv = dvv + lax.dot_general(p.astype(BF16), dog, _TN, preferred_element_type=F32)
                dp = lax.dot_general(dog, vv, _NT, preferred_element_type=F32)
                ds = p * (dp - delta_ref[g, r0:s, :])
                dk = dk + lax.dot_general(ds.astype(BF16), qg, _TN, preferred_element_type=F32)
            dk_ref[...] = dk
            dv_ref[...] = dvv

        pl.when(pl.program_id(2) < nc)(lambda: run(0))
        pl.when(pl.program_id(2) >= nc)(lambda: run(n_ctx))

    gs = lambda d: pl.BlockSpec((None, group, s, d), lambda bi, hi, j: (bi, hi, 0, 0))
    ks = lambda d: pl.BlockSpec((None, None, tk, d), lambda bi, hi, j: (bi, hi, j, 0))
    return pl.pallas_call(
        body, out_shape=[jax.ShapeDtypeStruct((b, hk, s, dq), F32), jax.ShapeDtypeStruct((b, hk, s, dv), F32)],
        grid=(b, hk, s // tk),
        in_specs=[gs(dq), ks(dq), ks(dv), gs(1), gs(1), gs(dv)], out_specs=[ks(dq), ks(dv)],
        name=f"attn_dkv_{h}x{s}x{dq}", compiler_params=_params(),
    )(q, k, v, lse, delta, do)


@functools.partial(jax.custom_vjp, nondiff_argnums=(3, 4, 5))
def attention(q, k, v, group, n_ctx, scale):
    return _attn_fwd(q, k, v, group, n_ctx, scale)[0]


def _attention_fwd(q, k, v, group, n_ctx, scale):
    o, lse = _attn_fwd(q, k, v, group, n_ctx, scale)
    return o, (q, k, v, o, lse)


def _attention_bwd(group, n_ctx, scale, res, do):
    q, k, v, o, lse = res
    dq, delta = _attn_dq(q, k, v, o, lse, do, group, n_ctx, scale)
    dk, dv = _attn_dkv(q, k, v, lse, delta, do, group, n_ctx, scale)
    return dq, dk, dv


attention.defvjp(_attention_fwd, _attention_bwd)


def _na_geometry(i, nc, rows):
    r = i - nc
    rs = jnp.clip(r - NA_WIN_R // 2, 0, rows - NA_WIN_R)
    is_ctx = i < nc
    cls = jnp.where(is_ctx, NA_WIN_R, r - rs)
    return jnp.where(is_ctx, 0, rs), cls


def _na_scores(q_ref, k_ref, bias_ref, hd, n_ctx, start, scale):
    qv = (q_ref[hd] * scale).astype(BF16)
    kc = k_ref[hd, 0:n_ctx, :].astype(BF16)
    kb = k_ref[hd, pl.ds(start, NA_BAND), :].astype(BF16)
    s_c = lax.dot_general(qv, kc, _NT, preferred_element_type=F32)
    s_l = lax.dot_general(qv, kb, _NT, preferred_element_type=F32) + bias_ref[hd]
    return qv, kc, kb, s_c, s_l


NA_HEADS_FWD = 4
NA_HEADS_BWD = 2


def _na_specs(hp, s, dh, nc, rows):
    qs = lambda d: pl.BlockSpec((None, hp, GRID_W, d), lambda bi, hg, i: (bi, hg, i, 0))
    ks = pl.BlockSpec((None, hp, s, dh), lambda bi, hg, i: (bi, hg, 0, 0))
    bs = pl.BlockSpec((hp, None, GRID_W, NA_BAND), lambda bi, hg, i: (hg, _na_geometry(i, nc, rows)[1], 0, 0))
    return qs, ks, bs


def _na_fwd(q, k, v, bias, n_ctx):
    b, h, s, dh = q.shape
    nc = n_ctx // GRID_W
    rows = (s - n_ctx) // GRID_W
    scale = dh ** -0.5
    hp = math.gcd(h, NA_HEADS_FWD)

    def body(q_ref, k_ref, v_ref, bias_ref, o_ref, lse_ref):
        rs, _ = _na_geometry(pl.program_id(2), nc, rows)
        start = pl.multiple_of(n_ctx + rs * GRID_W, GRID_W)
        for hd in range(hp):
            _, _, _, s_c, s_l = _na_scores(q_ref, k_ref, bias_ref, hd, n_ctx, start, scale)
            m = jnp.maximum(jnp.max(s_c, axis=-1, keepdims=True), jnp.max(s_l, axis=-1, keepdims=True))
            p_c = jnp.exp(s_c - m)
            p_l = jnp.exp(s_l - m)
            l = jnp.sum(p_c, axis=-1, keepdims=True) + jnp.sum(p_l, axis=-1, keepdims=True)
            o = jnp.dot(p_c.astype(BF16), v_ref[hd, 0:n_ctx, :].astype(BF16), preferred_element_type=F32)
            o = o + jnp.dot(p_l.astype(BF16), v_ref[hd, pl.ds(start, NA_BAND), :].astype(BF16),
                            preferred_element_type=F32)
            o_ref[hd] = o / l
            lse_ref[hd] = m + jnp.log(l)

    qs, ks, bs = _na_specs(hp, s, dh, nc, rows)
    return pl.pallas_call(
        body, out_shape=[jax.ShapeDtypeStruct((b, h, s, dh), F32), jax.ShapeDtypeStruct((b, h, s, 1), F32)],
        grid=(b, h // hp, s // GRID_W), in_specs=[qs(dh), ks, ks, bs], out_specs=[qs(dh), qs(1)],
        name=f"na_f_{s}", compiler_params=_params(),
    )(q, k, v, bias)


def _na_bwd(q, k, v, bias, o, lse, do, n_ctx):
    b, h, s, dh = q.shape
    nc = n_ctx // GRID_W
    rows = (s - n_ctx) // GRID_W
    scale = dh ** -0.5
    n_cls = NA_WIN_R + 1
    hp = math.gcd(h, NA_HEADS_BWD)

    def body(q_ref, k_ref, v_ref, bias_ref, o_ref, lse_ref, do_ref, dq_ref, dk_ref, dv_ref, db_ref):
        i = pl.program_id(2)
        rs, cls = _na_geometry(i, nc, rows)
        _, cls_prev = _na_geometry(i - 1, nc, rows)
        start = pl.multiple_of(n_ctx + rs * GRID_W, GRID_W)
        first = jnp.logical_or(i == 0, cls != cls_prev)

        @pl.when(i == 0)
        def _():
            dk_ref[...] = jnp.zeros_like(dk_ref)
            dv_ref[...] = jnp.zeros_like(dv_ref)

        for hd in range(hp):
            qv, kc, kb, s_c, s_l = _na_scores(q_ref, k_ref, bias_ref, hd, n_ctx, start, scale)
            lse_v = lse_ref[hd]
            p_c = jnp.exp(s_c - lse_v)
            p_l = jnp.exp(s_l - lse_v)
            dov = do_ref[hd]
            dob = dov.astype(BF16)
            delta = jnp.sum(dov * o_ref[hd], axis=-1, keepdims=True)
            vc = v_ref[hd, 0:n_ctx, :].astype(BF16)
            vb = v_ref[hd, pl.ds(start, NA_BAND), :].astype(BF16)
            ds_c = p_c * (lax.dot_general(dob, vc, _NT, preferred_element_type=F32) - delta)
            ds_l = p_l * (lax.dot_general(dob, vb, _NT, preferred_element_type=F32) - delta)
            dsc_b = ds_c.astype(BF16)
            dsl_b = ds_l.astype(BF16)
            dq_ref[hd] = (jnp.dot(dsc_b, kc, preferred_element_type=F32)
                          + jnp.dot(dsl_b, kb, preferred_element_type=F32)) * scale
            dk_ref[hd, 0:n_ctx, :] += lax.dot_general(dsc_b, qv, _TN, preferred_element_type=F32)
            dk_ref[hd, pl.ds(start, NA_BAND), :] += lax.dot_general(dsl_b, qv, _TN, preferred_element_type=F32)
            dv_ref[hd, 0:n_ctx, :] += lax.dot_general(p_c.astype(BF16), dob, _TN, preferred_element_type=F32)
            dv_ref[hd, pl.ds(start, NA_BAND), :] += lax.dot_general(p_l.astype(BF16), dob, _TN, preferred_element_type=F32)

            @pl.when(first)
            def _(hd=hd, ds_l=ds_l):
                db_ref[hd] = ds_l

            @pl.when(jnp.logical_not(first))
            def _(hd=hd, ds_l=ds_l):
                db_ref[hd] += ds_l

    qs, ks, bs = _na_specs(hp, s, dh, nc, rows)
    dbs = pl.BlockSpec((None, hp, None, GRID_W, NA_BAND),
                       lambda bi, hg, i: (bi, hg, _na_geometry(i, nc, rows)[1], 0, 0))
    return pl.pallas_call(
        body,
        out_shape=[jax.ShapeDtypeStruct((b, h, s, dh), F32), jax.ShapeDtypeStruct((b, h, s, dh), F32),
                   jax.ShapeDtypeStruct((b, h, s, dh), F32), jax.ShapeDtypeStruct((b, h, n_cls, GRID_W, NA_BAND), F32)],
        grid=(b, h // hp, s // GRID_W), in_specs=[qs(dh), ks, ks, bs, qs(dh), qs(1), qs(dh)],
        out_specs=[qs(dh), ks, ks, dbs], name=f"na_b_{s}", compiler_params=_params(),
    )(q, k, v, bias, o, lse, do)


@functools.partial(jax.custom_vjp, nondiff_argnums=(4,))
def na_attention(q, k, v, bias, n_ctx):
    return _na_fwd(q, k, v, bias, n_ctx)[0]


def _na_attention_fwd(q, k, v, bias, n_ctx):
    o, lse = _na_fwd(q, k, v, bias, n_ctx)
    return o, (q, k, v, bias, o, lse)


def _na_attention_bwd(n_ctx, res, do):
    q, k, v, bias, o, lse = res
    dq, dk, dv, db = _na_bwd(q, k, v, bias, o, lse, do, n_ctx)
    return dq, dk, dv, jnp.sum(db, axis=0)


na_attention.defvjp(_na_attention_fwd, _na_attention_bwd)


def _na_onehots():
    q = np.arange(GRID_W)[:, None]
    col = np.arange(GRID_W)[None, :]
    cs = np.clip(q - NA_WIN_C // 2, 0, GRID_W - NA_WIN_C)
    valid = (col >= cs) & (col < cs + NA_WIN_C)
    cidx = col - q + (NA_WIN_C - 1)
    n_b = 2 * NA_WIN_C - 1
    col_hot = np.zeros((LANE, GRID_W * GRID_W), np.float32)
    for qq in range(GRID_W):
        for cc in range(GRID_W):
            if valid[qq, cc]:
                col_hot[cidx[qq, cc], qq * GRID_W + cc] = 1.0
    row_hot = np.zeros((NA_WIN_R, NA_WIN_R, 2 * NA_WIN_R - 1), np.float32)
    for c in range(NA_WIN_R):
        for j in range(NA_WIN_R):
            row_hot[c, j, j - c + NA_WIN_R - 1] = 1.0
    mask = np.where(valid, 0.0, NEG).astype(np.float32)
    return col_hot, row_hot, mask, n_b


def na_bias_table(rpb):
    h = rpb.shape[0]
    col_hot, row_hot, mask, n_b = _na_onehots()
    t1 = jnp.einsum("cja,hab->hcjb", jnp.asarray(row_hot), rpb)
    t1 = jnp.pad(t1.reshape(h * NA_WIN_R * NA_WIN_R, n_b), ((0, 0), (0, LANE - n_b)))
    t2 = linear(t1, jnp.asarray(col_hot), True)
    t2 = t2.reshape(h, NA_WIN_R, NA_WIN_R, GRID_W, GRID_W) + jnp.asarray(mask)
    tab = jnp.transpose(t2, (0, 1, 3, 2, 4)).reshape(h, NA_WIN_R, GRID_W, NA_BAND)
    return jnp.concatenate([tab, jnp.full((h, 1, GRID_W, NA_BAND), NEG, F32)], axis=1)


def _first_step():
    return jnp.logical_and(pl.program_id(0) == 0, pl.program_id(1) == 0)


def _accum_out(ref, val, first):
    @pl.when(first)
    def _():
        ref[...] = val

    @pl.when(jnp.logical_not(first))
    def _():
        ref[...] += val


def _norm_head(xh, g):
    r = _rms(xh)
    yn = xh * r
    return yn * g, yn, r


def _norm_head_bwd(dy, yn, r, g):
    dg = jnp.sum(dy * yn, axis=0, keepdims=True)
    dyn = dy * g
    return r * (dyn - yn * jnp.mean(dyn * yn, axis=-1, keepdims=True)), dg


def _rope_signs(dh, start, rot_dim, n_heads):
    q = rot_dim // 4
    pos = np.arange(dh)
    quarter = (pos - start) // q
    inr = pos >= start
    sg = np.zeros((8, n_heads * dh), np.float32)
    sg[0] = np.tile(np.where(inr & (quarter % 2 == 0), -1.0, 0.0), n_heads)
    sg[1] = np.tile(np.where(inr & (quarter % 2 == 1), 1.0, 0.0), n_heads)
    return sg


def _rope_full(y, cos, sin, sg, q):
    w = y.shape[-1]
    rot = sg[0:1] * pltpu.roll(y, w - q, 1) + sg[1:2] * pltpu.roll(y, q, 1)
    return y * cos + rot * sin


def _rope_full_t(dy, cos, sin, sg, q):
    w = dy.shape[-1]
    z = dy * sin
    return dy * cos - sg[1:2] * pltpu.roll(z, q, 1) - sg[0:1] * pltpu.roll(z, w - q, 1)


def _hnr_call(x, g, cos, sin, sg, n_heads, q, dy=None):
    b, s, w = x.shape
    dh = w // n_heads
    ts = _pick(s, (256, 128, 64))
    rope = cos is not None

    def body(*refs):
        refs = list(refs)
        x_ref, g_ref = refs[0], refs[1]
        k = 2
        if rope:
            cos_ref, sin_ref, sg_ref = refs[2], refs[3], refs[4]
            k = 5
        gv = g_ref[...]
        if dy is None:
            o_ref = refs[k]
            for h in range(n_heads):
                sl = slice(h * dh, (h + 1) * dh)
                o_ref[:, sl] = _norm_head(x_ref[:, sl], gv)[0]
            if rope:
                o_ref[...] = _rope_full(o_ref[...], cos_ref[...], sin_ref[...], sg_ref[...], q)
            return
        dy_ref, dx_ref, dg_ref = refs[k], refs[k + 1], refs[k + 2]
        src = dy_ref
        if rope:
            dx_ref[...] = _rope_full_t(dy_ref[...], cos_ref[...], sin_ref[...], sg_ref[...], q)
            src = dx_ref
        dg = jnp.zeros((1, dh), F32)
        for h in range(n_heads):
            sl = slice(h * dh, (h + 1) * dh)
            _, yn, r = _norm_head(x_ref[:, sl], gv)
            dxh, dgh = _norm_head_bwd(src[:, sl], yn, r, gv)
            dx_ref[:, sl] = dxh
            dg = dg + dgh
        _accum_out(dg_ref, dg, _first_step())

    row = pl.BlockSpec((None, ts, w), lambda bi, i: (bi, i, 0))
    whole = lambda a: pl.BlockSpec(a.shape, lambda bi, i: (0, 0))
    ins, specs = [x, g], [row, whole(g)]
    if rope:
        ins += [cos, sin, sg]
        specs += [pl.BlockSpec((ts, w), lambda bi, i: (i, 0)), pl.BlockSpec((ts, w), lambda bi, i: (i, 0)), whole(sg)]
    if dy is None:
        out_shape, out_specs = jax.ShapeDtypeStruct(x.shape, F32), row
    else:
        ins.append(dy)
        specs.append(row)
        out_shape = [jax.ShapeDtypeStruct(x.shape, F32), jax.ShapeDtypeStruct(g.shape, F32)]
        out_specs = [row, whole(g)]
    return pl.pallas_call(
        body, out_shape=out_shape, grid=(b, s // ts), in_specs=specs, out_specs=out_specs,
        name=f"hnr_{'b' if dy is not None else 'f'}_{n_heads}x{dh}_{int(rope)}", compiler_params=_params(),
    )(*ins)


@functools.partial(jax.custom_vjp, nondiff_argnums=(5, 6))
def head_norm_rope(x, g, cos, sin, sg, n_heads, q):
    return _hnr_call(x, g, cos, sin, sg, n_heads, q)


def _head_norm_rope_fwd(x, g, cos, sin, sg, n_heads, q):
    return _hnr_call(x, g, cos, sin, sg, n_heads, q), (x, g, cos, sin, sg)


def _head_norm_rope_bwd(n_heads, q, res, dy):
    x, g, cos, sin, sg = res
    dx, dg = _hnr_call(x, g, cos, sin, sg, n_heads, q, dy=dy)
    zero = lambda t: None if t is None else jnp.zeros_like(t)
    return dx, dg, zero(cos), zero(sin), zero(sg)


head_norm_rope.defvjp(_head_norm_rope_fwd, _head_norm_rope_bwd)


def _mla_k_call(kv, kr, g, cos, sin, sg, dkn=None):
    b, s, _ = kv.shape
    ts = _pick(s, (256, 128, 64))
    hw = MLA_NOPE + MLA_V
    kn_w = MLA_HEADS * MLA_QK
    q = MLA_ROPE // 4

    def body(kv_ref, kr_ref, g_ref, cos_ref, sin_ref, sg_ref, *rest):
        gv = g_ref[...]
        krv = kr_ref[...]
        if dkn is None:
            (o_ref,) = rest
            for h in range(MLA_HEADS):
                kh = jnp.concatenate([kv_ref[:, h * hw:h * hw + MLA_NOPE], krv], axis=-1)
                o_ref[:, h * MLA_QK:(h + 1) * MLA_QK] = _norm_head(kh, gv)[0]
            o_ref[...] = _rope_full(o_ref[...], cos_ref[...], sin_ref[...], sg_ref[...], q)
            return
        dkn_ref, dkv_ref, dkr_ref, dg_ref, dy_ref = rest
        dy_ref[...] = _rope_full_t(dkn_ref[...], cos_ref[...], sin_ref[...], sg_ref[...], q)
        dg = jnp.zeros((1, MLA_QK), F32)
        dkr = jnp.zeros((ts, MLA_ROPE), F32)
        for h in range(MLA_HEADS):
            kh = jnp.concatenate([kv_ref[:, h * hw:h * hw + MLA_NOPE], krv], axis=-1)
            _, yn, r = _norm_head(kh, gv)
            dxh, dgh = _norm_head_bwd(dy_ref[:, h * MLA_QK:(h + 1) * MLA_QK], yn, r, gv)
            dkv_ref[:, h * hw:h * hw + MLA_NOPE] = dxh[:, :MLA_NOPE]
            dkv_ref[:, h * hw + MLA_NOPE:(h + 1) * hw] = jnp.zeros((ts, MLA_V), F32)
            dkr = dkr + dxh[:, MLA_NOPE:]
            dg = dg + dgh
        dkr_ref[...] = dkr
        _accum_out(dg_ref, dg, _first_step())

    row = lambda w: pl.BlockSpec((None, ts, w), lambda bi, i: (bi, i, 0))
    tab = pl.BlockSpec((ts, kn_w), lambda bi, i: (i, 0))
    whole = lambda a: pl.BlockSpec(a.shape, lambda bi, i: (0, 0))
    ins = [kv, kr, g, cos, sin, sg]
    specs = [row(kv.shape[2]), row(MLA_ROPE), whole(g), tab, tab, whole(sg)]
    scratch = []
    if dkn is None:
        out_shape, out_specs = jax.ShapeDtypeStruct((b, s, kn_w), F32), row(kn_w)
    else:
        ins.append(dkn)
        specs.append(row(kn_w))
        out_shape = [jax.ShapeDtypeStruct(kv.shape, F32), jax.ShapeDtypeStruct(kr.shape, F32),
                     jax.ShapeDtypeStruct(g.shape, F32)]
        out_specs = [row(kv.shape[2]), row(MLA_ROPE), whole(g)]
        scratch = [pltpu.VMEM((ts, kn_w), F32)]
    return pl.pallas_call(
        body, out_shape=out_shape, grid=(b, s // ts), in_specs=specs, out_specs=out_specs, scratch_shapes=scratch,
        name=f"mla_k_{'b' if dkn is not None else 'f'}", compiler_params=_params(),
    )(*ins)


@jax.custom_vjp
def mla_k_prep(kv, kr, g, cos, sin, sg):
    return _mla_k_call(kv, kr, g, cos, sin, sg)


def _mla_k_prep_fwd(kv, kr, g, cos, sin, sg):
    return _mla_k_call(kv, kr, g, cos, sin, sg), (kv, kr, g, cos, sin, sg)


def _mla_k_prep_bwd(res, dkn):
    kv, kr, g, cos, sin, sg = res
    dkv, dkr, dg = _mla_k_call(kv, kr, g, cos, sin, sg, dkn=dkn)
    return dkv, dkr, dg, jnp.zeros_like(cos), jnp.zeros_like(sin), jnp.zeros_like(sg)


mla_k_prep.defvjp(_mla_k_prep_fwd, _mla_k_prep_bwd)


class _HeadLayout:
    def __init__(self, groups, dq, dv, q_off, k_off, v_off, o_off, wq, wk, wv, wo, scale):
        self.groups, self.dq, self.dv, self.scale = groups, dq, dv, scale
        self.q_off, self.k_off, self.v_off, self.o_off = q_off, k_off, v_off, o_off
        self.wq, self.wk, self.wv, self.wo = wq, wk, wv, wo
        self.n_h = len(q_off)


def _gqa_layout():
    rep = GQA_Q_HEADS // GQA_KV_HEADS
    n_h = GQA_Q_HEADS // 2
    return _HeadLayout(2, HEAD_DIM, HEAD_DIM, [h * HEAD_DIM for h in range(n_h)], [(h // rep) * HEAD_DIM for h in range(n_h)],
                       [(h // rep) * HEAD_DIM for h in range(n_h)], [h * HEAD_DIM for h in range(n_h)],
                       n_h * HEAD_DIM, (n_h // rep) * HEAD_DIM, (n_h // rep) * HEAD_DIM, n_h * HEAD_DIM, HEAD_DIM ** -0.5)


def _mla_layout():
    n_h = MLA_HEADS // 2
    hw = MLA_NOPE + MLA_V
    return _HeadLayout(2, MLA_QK, MLA_V, [h * MLA_QK for h in range(n_h)], [h * MLA_QK for h in range(n_h)],
                       [h * hw + MLA_NOPE for h in range(n_h)], [h * MLA_V for h in range(n_h)],
                       n_h * MLA_QK, n_h * MLA_QK, n_h * hw, n_h * MLA_V, MLA_QK ** -0.5)


def _attn_tm_fwd(q, k, v, lay, n_ctx):
    b, s, _ = q.shape
    tq = min(256, n_ctx)
    nc = n_ctx // tq

    def body(q_ref, k_ref, v_ref, o_ref, lse_ref):
        def run(n_keys):
            for h in range(lay.n_h):
                qo, ko, vo, oo = lay.q_off[h], lay.k_off[h], lay.v_off[h], lay.o_off[h]
                qv = (q_ref[:, qo:qo + lay.dq] * lay.scale).astype(BF16)
                sc = lax.dot_general(qv, k_ref[0:n_keys, ko:ko + lay.dq].astype(BF16), _NT, preferred_element_type=F32)
                m = jnp.max(sc, axis=-1, keepdims=True)
                p = jnp.exp(sc - m)
                l = jnp.sum(p, axis=-1, keepdims=True)
                o = jnp.dot(p.astype(BF16), v_ref[0:n_keys, vo:vo + lay.dv].astype(BF16), preferred_element_type=F32)
                o_ref[:, oo:oo + lay.dv] = o / l
                lse_ref[:, h:h + 1] = m + jnp.log(l)

        pl.when(pl.program_id(2) < nc)(lambda: run(n_ctx))
        pl.when(pl.program_id(2) >= nc)(lambda: run(s))

    return pl.pallas_call(
        body, out_shape=[jax.ShapeDtypeStruct((b, s, lay.groups * lay.wo), F32),
                         jax.ShapeDtypeStruct((b, lay.groups, s, lay.n_h), F32)],
        grid=(b, lay.groups, s // tq),
        in_specs=[pl.BlockSpec((None, tq, lay.wq), lambda bi, g, i: (bi, i, g)),
                  pl.BlockSpec((None, s, lay.wk), lambda bi, g, i: (bi, 0, g)),
                  pl.BlockSpec((None, s, lay.wv), lambda bi, g, i: (bi, 0, g))],
        out_specs=[pl.BlockSpec((None, tq, lay.wo), lambda bi, g, i: (bi, i, g)),
                   pl.BlockSpec((None, None, tq, lay.n_h), lambda bi, g, i: (bi, g, i, 0))],
        name=f"attn_tm_f_{lay.dq}", compiler_params=_params(),
    )(q, k, v)


def _attn_tm_delta(o, do, lay):
    b, s, _ = o.shape
    ts = _pick(s, (256, 128, 64))

    def body(o_ref, do_ref, d_ref):
        for h in range(lay.n_h):
            oo = lay.o_off[h]
            d_ref[:, h:h + 1] = jnp.sum(o_ref[:, oo:oo + lay.dv] * do_ref[:, oo:oo + lay.dv], axis=-1, keepdims=True)

    blk = pl.BlockSpec((None, ts, lay.wo), lambda bi, g, i: (bi, i, g))
    return pl.pallas_call(
        body, out_shape=jax.ShapeDtypeStruct((b, lay.groups, s, lay.n_h), F32), grid=(b, lay.groups, s // ts),
        in_specs=[blk, blk], out_specs=pl.BlockSpec((None, None, ts, lay.n_h), lambda bi, g, i: (bi, g, i, 0)),
        name=f"attn_tm_delta_{lay.dq}", compiler_params=_params(),
    )(o, do)


def _attn_tm_bwd(q, k, v, lse, delta, do, lay, n_ctx):
    b, s, _ = q.shape
    tk = min(256, n_ctx)
    nc = n_ctx // tk

    def body(q_ref, k_ref, v_ref, lse_ref, delta_ref, do_ref, dq_ref, dk_ref, dv_ref):
        @pl.when(pl.program_id(2) == 0)
        def _():
            dq_ref[...] = jnp.zeros_like(dq_ref)

        def run(r0):
            dk_acc, dv_acc = {}, {}
            for h in range(lay.n_h):
                qo, ko, vo, oo = lay.q_off[h], lay.k_off[h], lay.v_off[h], lay.o_off[h]
                kh = k_ref[:, ko:ko + lay.dq].astype(BF16)
                vh = v_ref[:, vo:vo + lay.dv].astype(BF16)
                qv = (q_ref[r0:s, qo:qo + lay.dq] * lay.scale).astype(BF16)
                dob = do_ref[r0:s, oo:oo + lay.dv].astype(BF16)
                sc = lax.dot_general(qv, kh, _NT, preferred_element_type=F32)
                p = jnp.exp(sc - lse_ref[r0:s, h:h + 1])
                dvh = lax.dot_general(p.astype(BF16), dob, _TN, preferred_element_type=F32)
                dp = lax.dot_general(dob, vh, _NT, preferred_element_type=F32)
                dsb = (p * (dp - delta_ref[r0:s, h:h + 1])).astype(BF16)
                dkh = lax.dot_general(dsb, qv, _TN, preferred_element_type=F32)
                dq_ref[r0:s, qo:qo + lay.dq] += jnp.dot(dsb, kh, preferred_element_type=F32) * lay.scale
                dk_acc[ko] = dkh if ko not in dk_acc else dk_acc[ko] + dkh
                dv_acc[vo] = dvh if vo not in dv_acc else dv_acc[vo] + dvh
            if len(dv_acc) * lay.dv != lay.wv:
                dv_ref[...] = jnp.zeros_like(dv_ref)
            for ko, val in dk_acc.items():
                dk_ref[:, ko:ko + lay.dq] = val
            for vo, val in dv_acc.items():
                dv_ref[:, vo:vo + lay.dv] = val

        pl.when(pl.program_id(2) < nc)(lambda: run(0))
        pl.when(pl.program_id(2) >= nc)(lambda: run(n_ctx))

    full = lambda w: pl.BlockSpec((None, s, w), lambda bi, g, j: (bi, 0, g))
    blk = lambda w: pl.BlockSpec((None, tk, w), lambda bi, g, j: (bi, j, g))
    stat = pl.BlockSpec((None, None, s, lay.n_h), lambda bi, g, j: (bi, g, 0, 0))
    return pl.pallas_call(
        body, out_shape=[jax.ShapeDtypeStruct(q.shape, F32), jax.ShapeDtypeStruct(k.shape, F32),
                         jax.ShapeDtypeStruct(v.shape, F32)],
        grid=(b, lay.groups, s // tk),
        in_specs=[full(lay.wq), blk(lay.wk), blk(lay.wv), stat, stat, full(lay.wo)],
        out_specs=[full(lay.wq), blk(lay.wk), blk(lay.wv)],
        name=f"attn_tm_b_{lay.dq}", compiler_params=_params(),
    )(q, k, v, lse, delta, do)


def _make_attention_tm(lay):
    @functools.partial(jax.custom_vjp, nondiff_argnums=(3,))
    def op(q, k, v, n_ctx):
        return _attn_tm_fwd(q, k, v, lay, n_ctx)[0]

    def fwd(q, k, v, n_ctx):
        o, lse = _attn_tm_fwd(q, k, v, lay, n_ctx)
        return o, (q, k, v, o, lse)

    def bwd(n_ctx, res, do):
        q, k, v, o, lse = res
        return _attn_tm_bwd(q, k, v, lse, _attn_tm_delta(o, do, lay), do, lay, n_ctx)

    op.defvjp(fwd, bwd)
    return op


gqa_attention = _make_attention_tm(_gqa_layout())
mla_attention = _make_attention_tm(_mla_layout())

NA_GROUPS = 2


def _na_tm_specs(s, nc, rows):
    hg = NA_HEADS // NA_GROUPS
    w = hg * HEAD_DIM
    qs = pl.BlockSpec((None, GRID_W, w), lambda bi, g, i: (bi, i, g))
    ks = pl.BlockSpec((None, s, w), lambda bi, g, i: (bi, 0, g))
    bs = pl.BlockSpec((hg, None, GRID_W, NA_BAND), lambda bi, g, i: (g, _na_geometry(i, nc, rows)[1], 0, 0))
    ls = pl.BlockSpec((None, None, GRID_W, hg), lambda bi, g, i: (bi, g, i, 0))
    return hg, w, qs, ks, bs, ls


def _na_tm_scores(q_ref, k_ref, bias_ref, hd, n_ctx, start, scale):
    sl = slice(hd * HEAD_DIM, (hd + 1) * HEAD_DIM)
    qv = (q_ref[:, sl] * scale).astype(BF16)
    kc = k_ref[0:n_ctx, sl].astype(BF16)
    kb = k_ref[pl.ds(start, NA_BAND), sl].astype(BF16)
    s_c = lax.dot_general(qv, kc, _NT, preferred_element_type=F32)
    s_l = lax.dot_general(qv, kb, _NT, preferred_element_type=F32) + bias_ref[hd]
    return sl, qv, kc, kb, s_c, s_l


def _na_tm_fwd(q, k, v, bias, n_ctx):
    b, s, _ = q.shape
    nc = n_ctx // GRID_W
    rows = (s - n_ctx) // GRID_W
    scale = HEAD_DIM ** -0.5
    hg, w, qs, ks, bs, ls = _na_tm_specs(s, nc, rows)

    def body(q_ref, k_ref, v_ref, bias_ref, o_ref, lse_ref):
        rs, _ = _na_geometry(pl.program_id(2), nc, rows)
        start = pl.multiple_of(n_ctx + rs * GRID_W, GRID_W)
        for hd in range(hg):
            sl, _, _, _, s_c, s_l = _na_tm_scores(q_ref, k_ref, bias_ref, hd, n_ctx, start, scale)
            m = jnp.maximum(jnp.max(s_c, axis=-1, keepdims=True), jnp.max(s_l, axis=-1, keepdims=True))
            p_c = jnp.exp(s_c - m)
            p_l = jnp.exp(s_l - m)
            l = jnp.sum(p_c, axis=-1, keepdims=True) + jnp.sum(p_l, axis=-1, keepdims=True)
            o = jnp.dot(p_c.astype(BF16), v_ref[0:n_ctx, sl].astype(BF16), preferred_element_type=F32)
            o = o + jnp.dot(p_l.astype(BF16), v_ref[pl.ds(start, NA_BAND), sl].astype(BF16), preferred_element_type=F32)
            o_ref[:, sl] = o / l
            lse_ref[:, hd:hd + 1] = m + jnp.log(l)

    return pl.pallas_call(
        body, out_shape=[jax.ShapeDtypeStruct(q.shape, F32), jax.ShapeDtypeStruct((b, NA_GROUPS, s, hg), F32)],
        grid=(b, NA_GROUPS, s // GRID_W), in_specs=[qs, ks, ks, bs], out_specs=[qs, ls],
        name=f"na_tm_f_{s}", compiler_params=_params(),
    )(q, k, v, bias)


def _na_tm_bwd(q, k, v, bias, o, lse, do, n_ctx):
    b, s, _ = q.shape
    nc = n_ctx // GRID_W
    rows = (s - n_ctx) // GRID_W
    scale = HEAD_DIM ** -0.5
    n_cls = NA_WIN_R + 1
    hg, w, qs, ks, bs, ls = _na_tm_specs(s, nc, rows)

    def body(q_ref, k_ref, v_ref, bias_ref, o_ref, lse_ref, do_ref, dq_ref, dk_ref, dv_ref, db_ref):
        i = pl.program_id(2)
        rs, cls = _na_geometry(i, nc, rows)
        _, cls_prev = _na_geometry(i - 1, nc, rows)
        start = pl.multiple_of(n_ctx + rs * GRID_W, GRID_W)
        first = jnp.logical_or(i == 0, cls != cls_prev)

        @pl.when(i == 0)
        def _():
            dk_ref[...] = jnp.zeros_like(dk_ref)
            dv_ref[...] = jnp.zeros_like(dv_ref)

        for hd in range(hg):
            sl, qv, kc, kb, s_c, s_l = _na_tm_scores(q_ref, k_ref, bias_ref, hd, n_ctx, start, scale)
            lse_v = lse_ref[:, hd:hd + 1]
            p_c = jnp.exp(s_c - lse_v)
            p_l = jnp.exp(s_l - lse_v)
            dov = do_ref[:, sl]
            dob = dov.astype(BF16)
            delta = jnp.sum(dov * o_ref[:, sl], axis=-1, keepdims=True)
            vc = v_ref[0:n_ctx, sl].astype(BF16)
            vb = v_ref[pl.ds(start, NA_BAND), sl].astype(BF16)
            ds_c = p_c * (lax.dot_general(dob, vc, _NT, preferred_element_type=F32) - delta)
            ds_l = p_l * (lax.dot_general(dob, vb, _NT, preferred_element_type=F32) - delta)
            dsc_b = ds_c.astype(BF16)
            dsl_b = ds_l.astype(BF16)
            dq_ref[:, sl] = (jnp.dot(dsc_b, kc, preferred_element_type=F32)
                             + jnp.dot(dsl_b, kb, preferred_element_type=F32)) * scale
            dk_ref[0:n_ctx, sl] += lax.dot_general(dsc_b, qv, _TN, preferred_element_type=F32)
            dk_ref[pl.ds(start, NA_BAND), sl] += lax.dot_general(dsl_b, qv, _TN, preferred_element_type=F32)
            dv_ref[0:n_ctx, sl] += lax.dot_general(p_c.astype(BF16), dob, _TN, preferred_element_type=F32)
            dv_ref[pl.ds(start, NA_BAND), sl] += lax.dot_general(p_l.astype(BF16), dob, _TN, preferred_element_type=F32)

            @pl.when(first)
            def _(hd=hd, ds_l=ds_l):
                db_ref[hd] = ds_l

            @pl.when(jnp.logical_not(first))
            def _(hd=hd, ds_l=ds_l):
                db_ref[hd] += ds_l

    dbs = pl.BlockSpec((None, hg, None, GRID_W, NA_BAND), lambda bi, g, i: (bi, g, _na_geometry(i, nc, rows)[1], 0, 0))
    return pl.pallas_call(
        body,
        out_shape=[jax.ShapeDtypeStruct(q.shape, F32), jax.ShapeDtypeStruct(q.shape, F32), jax.ShapeDtypeStruct(q.shape, F32),
                   jax.ShapeDtypeStruct((b, NA_HEADS, n_cls, GRID_W, NA_BAND), F32)],
        grid=(b, NA_GROUPS, s // GRID_W), in_specs=[qs, ks, ks, bs, qs, ls, qs], out_specs=[qs, ks, ks, dbs],
        name=f"na_tm_b_{s}", compiler_params=_params(),
    )(q, k, v, bias, o, lse, do)


@functools.partial(jax.custom_vjp, nondiff_argnums=(4,))
def na_attention_tm(q, k, v, bias, n_ctx):
    return _na_tm_fwd(q, k, v, bias, n_ctx)[0]


def _na_attention_tm_fwd(q, k, v, bias, n_ctx):
    o, lse = _na_tm_fwd(q, k, v, bias, n_ctx)
    return o, (q, k, v, bias, o, lse)


def _na_attention_tm_bwd(n_ctx, res, do):
    q, k, v, bias, o, lse = res
    dq, dk, dv, db = _na_tm_bwd(q, k, v, bias, o, lse, do, n_ctx)
    return dq, dk, dv, _sum_rows(db.reshape(db.shape[0], -1, NA_BAND), db.shape[0]).reshape(db.shape[1:])


na_attention_tm.defvjp(_na_attention_tm_fwd, _na_attention_tm_bwd)


def _cmul(ar, ai, br, bi):
    return ar * br - ai * bi, ar * bi + ai * br


def _s5_chunk(n_ctx):
    return min(256, n_ctx)


def _s5_tables(a_re, a_im, t_len, rev):
    a_re, a_im = lax.stop_gradient(a_re), lax.stop_gradient(a_im)
    mag = jnp.sqrt(a_re * a_re + a_im * a_im)
    th = jnp.arctan2(a_im, a_re)
    t = jnp.arange(t_len + 1, dtype=F32)[:, None]
    pm = jnp.where(t == 0, 1.0, jnp.exp(t * jnp.log(jnp.maximum(mag, 1e-37))) * (mag > 0))
    pw = jnp.stack([pm * jnp.cos(t * th), pm * jnp.sin(t * th)])
    steps = jnp.concatenate([pw[:, min(2 ** i, t_len)][:, None] for i in range(8)], axis=1)
    tile = pw[:, 1:9]
    a8k = pw[:, 0:t_len:8]
    if rev:
        tile, a8k = tile[:, ::-1], a8k[:, ::-1]
    misc = jnp.concatenate([pw[:, t_len:t_len + 1], jnp.zeros((2, 7, pw.shape[-1]), F32)], axis=1)
    return jnp.concatenate([steps, tile, misc, a8k], axis=1)


def _scan_chunk(x_re, x_im, tab_ref, hin_re, hin_im, rev, t_len, xs_ref, es_ref):
    outs = [_scan_slab(x_re[:, k:k + LANE], x_im[:, k:k + LANE], tab_ref, hin_re[:, k:k + LANE], hin_im[:, k:k + LANE],
                       rev, t_len, xs_ref, es_ref, k) for k in range(0, x_re.shape[-1], LANE)]
    return tuple(jnp.concatenate([o[t] for o in outs], axis=-1) for t in range(4))


def _scan_slab(x_re, x_im, tab_ref, hin_re, hin_im, rev, t_len, xs_ref, es_ref, k0):
    lanes = LANE
    n2 = t_len // 8
    tab_ref = tab_ref.at[:, :, k0:k0 + LANE]
    rin = lax.broadcasted_iota(jnp.int32, (t_len, lanes), 0) & 7
    for li, sh in enumerate((1, 2, 4)):
        m_re, m_im = tab_ref[0, li:li + 1, :], tab_ref[1, li:li + 1, :]
        amt = sh if not rev else t_len - sh
        c_re, c_im = _cmul(m_re, m_im, pltpu.roll(x_re, amt, 0), pltpu.roll(x_im, amt, 0))
        ok = (rin >= sh) if not rev else (rin < 8 - sh)
        x_re = x_re + jnp.where(ok, c_re, 0.0)
        x_im = x_im + jnp.where(ok, c_im, 0.0)
    xr_ref, xi_ref = xs_ref
    xr_ref[...] = x_re
    xi_ref[...] = x_im
    off = 0 if rev else 7
    e_re = xr_ref[pl.ds(off, n2, stride=8), :]
    e_im = xi_ref[pl.ds(off, n2, stride=8), :]
    row2 = lax.broadcasted_iota(jnp.int32, (n2, lanes), 0)
    sh, li = 1, 3
    while sh < n2:
        m_re, m_im = tab_ref[0, li:li + 1, :], tab_ref[1, li:li + 1, :]
        amt = sh if not rev else n2 - sh
        c_re, c_im = _cmul(m_re, m_im, pltpu.roll(e_re, amt, 0), pltpu.roll(e_im, amt, 0))
        ok = (row2 >= sh) if not rev else (row2 < n2 - sh)
        e_re = e_re + jnp.where(ok, c_re, 0.0)
        e_im = e_im + jnp.where(ok, c_im, 0.0)
        sh, li = sh * 2, li + 1
    es_ref[0] = e_re
    es_ref[1] = e_im
    last = 0 if rev else n2 - 1
    t_re, t_im = _cmul(tab_ref[0, 16:17, :], tab_ref[1, 16:17, :], hin_re, hin_im)
    hout_re = es_ref[0, last:last + 1, :] + t_re
    hout_im = es_ref[1, last:last + 1, :] + t_im
    amt = 1 if not rev else n2 - 1
    ok = (row2 >= 1) if not rev else (row2 < n2 - 1)
    k_re, k_im = _cmul(tab_ref[0, 24:24 + n2, :], tab_ref[1, 24:24 + n2, :], hin_re, hin_im)
    c_re = jnp.where(ok, pltpu.roll(e_re, amt, 0), 0.0) + k_re
    c_im = jnp.where(ok, pltpu.roll(e_im, amt, 0), 0.0) + k_im
    tp_re, tp_im = tab_ref[0, 8:16, :][None], tab_ref[1, 8:16, :][None]
    add_re, add_im = _cmul(tp_re, tp_im, c_re[:, None, :], c_im[:, None, :])
    h_re = xr_ref[...] + add_re.reshape(t_len, lanes)
    h_im = xi_ref[...] + add_im.reshape(t_len, lanes)
    return h_re, h_im, hout_re, hout_im


def _s5_order(j, n_chunks, nc, rev):
    if not rev:
        return j
    return jnp.where(j < nc, nc - 1 - j, n_chunks - 1 - (j - nc))


def _s5_fwd(u, tab, b_bd, c_bd, n_ctx, rev):
    b, s, w = u.shape
    lanes = b_bd.shape[-1]
    t_len = _s5_chunk(n_ctx)
    n_chunks, nc = s // t_len, n_ctx // t_len

    def body(u_ref, tab_ref, b_ref, c_ref, y_ref, h_ref, hin_ref, carry_ref, xr_ref, xi_ref, es_ref):
        xs_ref = (xr_ref, xi_ref)

        @pl.when(pl.program_id(1) == 0)
        def _():
            carry_ref[...] = jnp.zeros_like(carry_ref)

        ub = u_ref[...].astype(BF16)
        x_re = jnp.dot(ub, b_ref[0].astype(BF16), preferred_element_type=F32)
        x_im = jnp.dot(ub, b_ref[1].astype(BF16), preferred_element_type=F32)
        hin_re, hin_im = carry_ref[0, 0:1, :], carry_ref[1, 0:1, :]
        hin_ref[...] = carry_ref[...]
        h_re, h_im, ho_re, ho_im = _scan_chunk(x_re, x_im, tab_ref, hin_re, hin_im, rev, t_len, xs_ref, es_ref)
        carry_ref[0] = jnp.broadcast_to(ho_re, (8, lanes))
        carry_ref[1] = jnp.broadcast_to(ho_im, (8, lanes))
        h_ref[0] = h_re
        h_ref[1] = h_im
        y_ref[...] = (jnp.dot(h_re.astype(BF16), c_ref[0].astype(BF16), preferred_element_type=F32)
                      - jnp.dot(h_im.astype(BF16), c_ref[1].astype(BF16), preferred_element_type=F32))

    order = lambda j: _s5_order(j, n_chunks, nc, rev)
    whole = lambda arr: pl.BlockSpec(arr.shape, lambda bi, j: (0,) * arr.ndim)
    return pl.pallas_call(
        body,
        out_shape=[jax.ShapeDtypeStruct((b, s, w), F32), jax.ShapeDtypeStruct((2, b, s, lanes), F32),
                   jax.ShapeDtypeStruct((2, b, n_chunks, 8, lanes), F32)],
        grid=(b, n_chunks),
        in_specs=[pl.BlockSpec((None, t_len, w), lambda bi, j: (bi, order(j), 0)), whole(tab), whole(b_bd), whole(c_bd)],
        out_specs=[pl.BlockSpec((None, t_len, w), lambda bi, j: (bi, order(j), 0)),
                   pl.BlockSpec((2, None, t_len, lanes), lambda bi, j: (0, bi, order(j), 0)),
                   pl.BlockSpec((2, None, None, 8, lanes), lambda bi, j: (0, bi, order(j), 0, 0))],
        scratch_shapes=[pltpu.VMEM((2, 8, lanes), F32), pltpu.VMEM((t_len, LANE), F32), pltpu.VMEM((t_len, LANE), F32),
                        pltpu.VMEM((2, t_len // 8, LANE), F32)],
        name=f"s5_f_{s}_{int(rev)}", compiler_params=_params(),
    )(u, tab, b_bd, c_bd)


def _s5_bwd(u, tab_adj, b_bd, c_bd, h, hin, dy, n_ctx, rev):
    b, s, w = u.shape
    lanes = b_bd.shape[-1]
    t_len = _s5_chunk(n_ctx)
    n_chunks, nc = s // t_len, n_ctx // t_len
    arev = not rev

    def body(u_ref, tab_ref, b_ref, c_ref, h_ref, hin_ref, dy_ref, du_ref, db_ref, dc_ref, da_ref,
             carry_ref, xr_ref, xi_ref, es_ref):
        xs_ref = (xr_ref, xi_ref)
        first = jnp.logical_and(pl.program_id(0) == 0, pl.program_id(1) == 0)

        @pl.when(pl.program_id(1) == 0)
        def _():
            carry_ref[...] = jnp.zeros_like(carry_ref)

        dyv = dy_ref[...]
        dyb = dyv.astype(BF16)
        dn = (((1,), (1,)), ((), ()))
        dt = (((0,), (0,)), ((), ()))
        x_re = lax.dot_general(dyb, c_ref[0].astype(BF16), dn, preferred_element_type=F32)
        x_im = -lax.dot_general(dyb, c_ref[1].astype(BF16), dn, preferred_element_type=F32)
        g_re, g_im, go_re, go_im = _scan_chunk(x_re, x_im, tab_ref, carry_ref[0, 0:1, :], carry_ref[1, 0:1, :],
                                               arev, t_len, xs_ref, es_ref)
        carry_ref[0] = jnp.broadcast_to(go_re, (8, lanes))
        carry_ref[1] = jnp.broadcast_to(go_im, (8, lanes))
        h_re, h_im = h_ref[0], h_ref[1]
        gb_re, gb_im = g_re.astype(BF16), g_im.astype(BF16)
        du_ref[...] = (lax.dot_general(gb_re, b_ref[0].astype(BF16), dn, preferred_element_type=F32)
                       + lax.dot_general(gb_im, b_ref[1].astype(BF16), dn, preferred_element_type=F32))
        ub = u_ref[...].astype(BF16)
        db_re = lax.dot_general(ub, gb_re, dt, preferred_element_type=F32)
        db_im = lax.dot_general(ub, gb_im, dt, preferred_element_type=F32)
        dc_re = lax.dot_general(h_re.astype(BF16), dyb, dt, preferred_element_type=F32)
        dc_im = -lax.dot_general(h_im.astype(BF16), dyb, dt, preferred_element_type=F32)
        row = lax.broadcasted_iota(jnp.int32, (t_len, lanes), 0)
        amt = 1 if not rev else t_len - 1
        edge = (row == 0) if not rev else (row == t_len - 1)
        hp_re = jnp.where(edge, hin_ref[0, 0:1, :], pltpu.roll(h_re, amt, 0))
        hp_im = jnp.where(edge, hin_ref[1, 0:1, :], pltpu.roll(h_im, amt, 0))
        da_re = jnp.sum(g_re * hp_re + g_im * hp_im, axis=0, keepdims=True)
        da_im = jnp.sum(g_im * hp_re - g_re * hp_im, axis=0, keepdims=True)

        @pl.when(first)
        def _():
            db_ref[0], db_ref[1] = db_re, db_im
            dc_ref[0], dc_ref[1] = dc_re, dc_im
            da_ref[0] = jnp.broadcast_to(da_re, (8, lanes))
            da_ref[1] = jnp.broadcast_to(da_im, (8, lanes))

        @pl.when(jnp.logical_not(first))
        def _():
            db_ref[0] += db_re
            db_ref[1] += db_im
            dc_ref[0] += dc_re
            dc_ref[1] += dc_im
            da_ref[0] += jnp.broadcast_to(da_re, (8, lanes))
            da_ref[1] += jnp.broadcast_to(da_im, (8, lanes))

    order = lambda j: _s5_order(n_chunks - 1 - j, n_chunks, nc, rev)
    whole = lambda arr: pl.BlockSpec(arr.shape, lambda bi, j: (0,) * arr.ndim)
    us = pl.BlockSpec((None, t_len, w), lambda bi, j: (bi, order(j), 0))
    return pl.pallas_call(
        body,
        out_shape=[jax.ShapeDtypeStruct((b, s, w), F32), jax.ShapeDtypeStruct(b_bd.shape, F32),
                   jax.ShapeDtypeStruct(c_bd.shape, F32), jax.ShapeDtypeStruct((2, 8, lanes), F32)],
        grid=(b, n_chunks),
        in_specs=[us, whole(tab_adj), whole(b_bd), whole(c_bd),
                  pl.BlockSpec((2, None, t_len, lanes), lambda bi, j: (0, bi, order(j), 0)),
                  pl.BlockSpec((2, None, None, 8, lanes), lambda bi, j: (0, bi, order(j), 0, 0)), us],
        out_specs=[us, whole(b_bd), whole(c_bd), pl.BlockSpec((2, 8, lanes), lambda bi, j: (0, 0, 0))],
        scratch_shapes=[pltpu.VMEM((2, 8, lanes), F32), pltpu.VMEM((t_len, LANE), F32), pltpu.VMEM((t_len, LANE), F32),
                        pltpu.VMEM((2, t_len // 8, LANE), F32)],
        name=f"s5_b_{s}_{int(rev)}", compiler_params=_params(),
    )(u, tab_adj, b_bd, c_bd, h, hin, dy)


@functools.partial(jax.custom_vjp, nondiff_argnums=(4, 5))
def s5_direction(u, a, b_bd, c_bd, n_ctx, rev):
    tab = _s5_tables(a[0], a[1], _s5_chunk(n_ctx), rev)
    return _s5_fwd(u, tab, b_bd, c_bd, n_ctx, rev)[0]


def _s5_direction_fwd(u, a, b_bd, c_bd, n_ctx, rev):
    tab = _s5_tables(a[0], a[1], _s5_chunk(n_ctx), rev)
    y, h, hin = _s5_fwd(u, tab, b_bd, c_bd, n_ctx, rev)
    return y, (u, a, b_bd, c_bd, h, hin)


def _s5_direction_bwd(n_ctx, rev, res, dy):
    u, a, b_bd, c_bd, h, hin = res
    tab_adj = _s5_tables(a[0], -a[1], _s5_chunk(n_ctx), not rev)
    du, db, dc, da = _s5_bwd(u, tab_adj, b_bd, c_bd, h, hin, dy, n_ctx, rev)
    return du, da[:, 0, :], db, dc


s5_direction.defvjp(_s5_direction_fwd, _s5_direction_bwd)


def _s5_discretize(lam_re, lam_im, log_dt, b_re, b_im):
    dt = jnp.exp(log_dt)[:, None]
    mag = jnp.exp(lam_re * dt)
    a_re = mag * jnp.cos(lam_im * dt)
    a_im = mag * jnp.sin(lam_im * dt)
    den = jnp.square(lam_re) + jnp.square(lam_im)
    f_re = ((a_re - 1.0) * lam_re + a_im * lam_im) / den
    f_im = (a_im * lam_re - (a_re - 1.0) * lam_im) / den
    bb_re = f_re[..., None] * b_re - f_im[..., None] * b_im
    bb_im = f_re[..., None] * b_im + f_im[..., None] * b_re
    return a_re, a_im, bb_re, bb_im


def _block_diag(t):
    g, r, c = t.shape
    return (jnp.eye(g, dtype=F32)[:, None, :, None] * t[:, :, None, :]).reshape(g * r, g * c)


def _loss_head(y, target):
    b, n, d = y.shape
    ts = _pick(n, (256, 128, 64))

    def body(y_ref, t_ref, loss_ref, dy_ref):
        first = jnp.logical_and(pl.program_id(0) == 0, pl.program_id(1) == 0)
        err = y_ref[...] - t_ref[...]
        dy_ref[...] = err * (1.0 / d)
        part = 0.5 * jnp.sum(jnp.sum(err * err, axis=-1, keepdims=True) * (1.0 / d), axis=0, keepdims=True)
        part = jnp.broadcast_to(part, (8, LANE))

        @pl.when(first)
        def _():
            loss_ref[...] = part

        @pl.when(jnp.logical_not(first))
        def _():
            loss_ref[...] += part

    blk = pl.BlockSpec((None, ts, d), lambda bi, i: (bi, i, 0))
    return pl.pallas_call(
        body, out_shape=[jax.ShapeDtypeStruct((8, LANE), F32), jax.ShapeDtypeStruct((b, n, d), F32)],
        grid=(b, n // ts), in_specs=[blk, blk], out_specs=[pl.BlockSpec((8, LANE), lambda bi, i: (0, 0)), blk],
        name="loss_head", compiler_params=_params(),
    )(y, target)


def _adamw(w, g, m, v):
    shape = w.shape
    n = int(np.prod(shape))
    cols = shape[-1]
    r = n // cols
    tr = _pick(r, (512, 256, 128, 64, 32, 16, 8))
    c1 = 1.0 / (1.0 - ADAM_B1 ** ADAM_STEP)
    c2 = 1.0 / (1.0 - ADAM_B2 ** ADAM_STEP)

    def body(w_ref, g_ref, m_ref, v_ref, d_ref, mo_ref, vo_ref):
        gv = g_ref[...]
        m2 = ADAM_B1 * m_ref[...] + (1.0 - ADAM_B1) * gv
        v2 = ADAM_B2 * v_ref[...] + (1.0 - ADAM_B2) * (gv * gv)
        d_ref[...] = -ADAM_LR * ((m2 * c1) / (jnp.sqrt(v2 * c2) + ADAM_EPS) + ADAM_WD * w_ref[...])
        mo_ref[...] = m2
        vo_ref[...] = v2

    blk = pl.BlockSpec((tr, cols), lambda i: (i, 0))
    outs = pl.pallas_call(
        body, out_shape=[jax.ShapeDtypeStruct((r, cols), F32)] * 3, grid=(r // tr,),
        in_specs=[blk] * 4, out_specs=[blk] * 3, name=f"adamw_{r}x{cols}", compiler_params=_params(),
    )(*[t.reshape(r, cols) for t in (w, g, m, v)])
    return tuple(o.reshape(shape) for o in outs)


def _sum_rows(x, n):
    _, r, c = x.shape
    tr = _pick(r, (512, 256, 128, 64, 32, 16, 8))

    def body(x_ref, o_ref):
        acc = x_ref[0]
        for j in range(1, n):
            acc = acc + x_ref[j]
        o_ref[...] = acc

    return pl.pallas_call(
        body, out_shape=jax.ShapeDtypeStruct((r, c), F32), grid=(r // tr,),
        in_specs=[pl.BlockSpec((n, tr, c), lambda i: (0, i, 0))], out_specs=pl.BlockSpec((tr, c), lambda i: (i, 0)),
        name=f"sum{n}_{r}x{c}", compiler_params=_params(),
    )(x)


def _accumulate(parts, out_dtype):
    r, c = parts[0].shape[-2:]
    tr = _pick(r, (512, 256, 128, 64, 32, 16))

    def body(*refs):
        acc = None
        for ref in refs[:-1]:
            terms = [ref[j] for j in range(ref.shape[0])] if len(ref.shape) == 3 else [ref[...]]
            for t in terms:
                acc = t.astype(F32) if acc is None else acc + t.astype(F32)
        refs[-1][...] = acc.astype(out_dtype)

    specs = [pl.BlockSpec((p.shape[0], tr, c), lambda i: (0, i, 0)) if p.ndim == 3 else pl.BlockSpec((tr, c), lambda i: (i, 0))
             for p in parts]
    tag = "_".join(str(p.shape[0]) if p.ndim == 3 else "1" for p in parts)
    return pl.pallas_call(
        body, out_shape=jax.ShapeDtypeStruct((r, c), out_dtype), grid=(r // tr,), in_specs=specs,
        out_specs=pl.BlockSpec((tr, c), lambda i: (i, 0)), name=f"accumulate_{tag}_{r}x{c}_{jnp.dtype(out_dtype).name}",
        compiler_params=_params(),
    )(*parts)


def _add2(x, y):
    shape = x.shape
    c = shape[-1]
    r = int(np.prod(shape)) // c
    tr = _pick(r, (512, 256, 128, 64, 32, 16, 8))

    def body(x_ref, y_ref, o_ref):
        o_ref[...] = x_ref[...] + y_ref[...]

    blk = pl.BlockSpec((tr, c), lambda i: (i, 0))
    return pl.pallas_call(
        body, out_shape=jax.ShapeDtypeStruct((r, c), F32), grid=(r // tr,), in_specs=[blk, blk], out_specs=blk,
        name=f"add2_{r}x{c}", compiler_params=_params(),
    )(x.reshape(r, c), y.reshape(r, c)).reshape(shape)


_FLIPS = ((1, 0), (0, 1), (1, 1))


def _me():
    return lax.axis_index("x"), lax.axis_index("y"), lax.axis_index("c")


def allgather8(v):
    m_per, n = v.shape

    def body(x_ref, out_ref, send_sems, recv_sems, local_sem):
        x, y, c = _me()
        me, sibling = (x, y, c), (x, y, 1 - c)
        chips = [(1 - x, y), (x, 1 - y), (1 - x, 1 - y)]

        def rows(px, py, pc):
            return out_ref.at[pl.ds((4 * px + 2 * py + pc) * m_per, m_per), :]

        def copy(k, block, to, src=None):
            return pltpu.make_async_remote_copy(
                src_ref=rows(*block) if src is None else src, dst_ref=rows(*block),
                send_sem=send_sems.at[k], recv_sem=recv_sems.at[k], device_id=to, device_id_type=MESH)

        mine = pltpu.make_async_copy(x_ref, rows(*me), local_sem)
        mine.start()
        first = [copy(0, me, sibling, src=x_ref)]
        first += [copy(1 + j, me, (*chip, c), src=x_ref) for j, chip in enumerate(chips)]
        for cp in first:
            cp.start()
        passed = [copy(4 + j, (*chip, c), sibling) for j, chip in enumerate(chips)]
        for j, chip in enumerate(chips):
            copy(1 + j, (*chip, c), me).wait_recv()
            passed[j].start()
        copy(0, sibling, me).wait_recv()
        for j, chip in enumerate(chips):
            copy(4 + j, (*chip, 1 - c), me).wait_recv()
        for cp in first + passed:
            cp.wait_send()
        mine.wait()

    return pl.pallas_call(
        body, out_shape=jax.ShapeDtypeStruct((N_DEV * m_per, n), v.dtype), in_specs=[VMEM_SPEC], out_specs=VMEM_SPEC,
        scratch_shapes=[pltpu.SemaphoreType.DMA((7,)), pltpu.SemaphoreType.DMA((7,)), pltpu.SemaphoreType.DMA],
        name=f"allgather8_{m_per}x{n}", compiler_params=_params(),
    )(v)


def _row_chunks(rows, tile_rows, want):
    n = want
    while n > 1 and rows % (n * tile_rows):
        n //= 2
    return [(i * (rows // n), rows // n) for i in range(n)]


def _remote(src, dst, send_sem, recv_sem, to):
    return pltpu.make_async_remote_copy(src_ref=src, dst_ref=dst, send_sem=send_sem, recv_sem=recv_sem, device_id=to,
                                        device_id_type=MESH)


def plane_allgather(big, small):
    rows = big.shape[0]
    rh = rows // 2
    tile = 16 if big.dtype == BF16 else 8
    ch_full = _row_chunks(rows, tile, 8)
    ch_half = _row_chunks(rh, tile, 4)

    def body(big_ref, small_ref, obig_ref, osmall_ref, send_sems, recv_sems, fwd_send, fwd_recv, own_send, own_recv):
        x, y, c = _me()
        me = 2 * x + y
        sibling = (x, y, 1 - c)
        mine = pl.ds(c * rh, rh)
        other = pl.ds((1 - c) * rh, rh)
        peers = [((x + fx) & 1, (y + fy) & 1) for fx, fy in _FLIPS]
        for st, sz in ch_full:
            sl = pl.ds(st, sz)
            _remote(big_ref.at[sl], obig_ref.at[me, sl], own_send.at[0], own_recv.at[0], sibling).start()
        _remote(small_ref, osmall_ref.at[me], own_send.at[1], own_recv.at[1], sibling).start()
        for j, (px, py) in enumerate(peers):
            for st, sz in ch_half:
                sl = pl.ds(c * rh + st, sz)
                _remote(big_ref.at[sl], obig_ref.at[me, sl], send_sems.at[j], recv_sems.at[j], (px, py, c)).start()
            _remote(small_ref, osmall_ref.at[me], send_sems.at[3 + j], recv_sems.at[3 + j], (px, py, c)).start()
        for j, (px, py) in enumerate(peers):
            pidx = 2 * px + py
            _remote(big_ref.at[mine], obig_ref.at[pidx, mine], send_sems.at[j], recv_sems.at[j], (px, py, c)).wait_recv()
            for st, sz in ch_half:
                sl = pl.ds(c * rh + st, sz)
                _remote(obig_ref.at[pidx, sl], obig_ref.at[pidx, sl], fwd_send.at[j], fwd_recv.at[j], sibling).start()
            _remote(small_ref, osmall_ref.at[pidx], send_sems.at[3 + j], recv_sems.at[3 + j], (px, py, c)).wait_recv()
        for j, (px, py) in enumerate(peers):
            pidx = 2 * px + py
            _remote(obig_ref.at[pidx, other], obig_ref.at[pidx, other], fwd_send.at[j], fwd_recv.at[j], sibling).wait_recv()
        for j, (px, py) in enumerate(peers):
            pidx = 2 * px + py
            _remote(big_ref.at[mine], obig_ref.at[me, mine], send_sems.at[j], recv_sems.at[j], (px, py, c)).wait_send()
            _remote(small_ref, osmall_ref.at[me], send_sems.at[3 + j], recv_sems.at[3 + j], (px, py, c)).wait_send()
            _remote(obig_ref.at[pidx, mine], obig_ref.at[pidx, mine], fwd_send.at[j], fwd_recv.at[j], sibling).wait_send()
        _remote(big_ref, obig_ref.at[me], own_send.at[0], own_recv.at[0], sibling).wait()
        _remote(small_ref, osmall_ref.at[me], own_send.at[1], own_recv.at[1], sibling).wait()

    return pl.pallas_call(
        body, out_shape=[jax.ShapeDtypeStruct((N_PLANE,) + big.shape, big.dtype),
                         jax.ShapeDtypeStruct((N_PLANE,) + small.shape, small.dtype)],
        in_specs=[ANY, ANY], out_specs=[ANY, ANY],
        scratch_shapes=[pltpu.SemaphoreType.DMA((6,)), pltpu.SemaphoreType.DMA((6,)), pltpu.SemaphoreType.DMA((3,)),
                        pltpu.SemaphoreType.DMA((3,)), pltpu.SemaphoreType.DMA((2,)), pltpu.SemaphoreType.DMA((2,))],
        name="plane_allgather", compiler_params=_params(),
    )(big, small)


def plane_scatter(p):
    tile = 16 if p.dtype == BF16 else 8
    chunks = _row_chunks(p.shape[1], tile, 4)

    def body(p_ref, out_ref, send_sems, recv_sems):
        x, y, c = _me()
        peers = [((x + fx) & 1, (y + fy) & 1) for fx, fy in _FLIPS]
        for j, (px, py) in enumerate(peers):
            for st, sz in chunks:
                sl = pl.ds(st, sz)
                _remote(p_ref.at[2 * px + py, sl], out_ref.at[j, sl], send_sems.at[j], recv_sems.at[j], (px, py, c)).start()
        for j, (px, py) in enumerate(peers):
            _remote(p_ref.at[0], out_ref.at[j], send_sems.at[j], recv_sems.at[j], (px, py, c)).wait_recv()
        for j, (px, py) in enumerate(peers):
            _remote(p_ref.at[0], out_ref.at[j], send_sems.at[j], recv_sems.at[j], (px, py, c)).wait_send()

    return pl.pallas_call(
        body, out_shape=jax.ShapeDtypeStruct((len(_FLIPS),) + p.shape[1:], p.dtype), in_specs=[ANY], out_specs=ANY,
        scratch_shapes=[pltpu.SemaphoreType.DMA((3,)), pltpu.SemaphoreType.DMA((3,))],
        name="plane_scatter", compiler_params=_params(),
    )(p)


def sibling_halves(buf):
    n_blk, _, rows, cols = buf.shape
    tile = 16 if buf.dtype == BF16 else 8
    chunks = _row_chunks(rows, tile, 2)

    def body(buf_ref, got_ref, send_sem, recv_sem):
        x, y, c = _me()
        for j in range(n_blk):
            for st, sz in chunks:
                sl = pl.ds(st, sz)
                _remote(buf_ref.at[j, 1 - c, sl], got_ref.at[j, sl], send_sem, recv_sem, (x, y, 1 - c)).start()
        _remote(got_ref, got_ref, send_sem, recv_sem, (x, y, 1 - c)).wait()

    return pl.pallas_call(
        body, out_shape=jax.ShapeDtypeStruct((n_blk, rows, cols), buf.dtype), in_specs=[ANY], out_specs=ANY,
        scratch_shapes=[pltpu.SemaphoreType.DMA, pltpu.SemaphoreType.DMA],
        name="sibling_halves", compiler_params=_params(),
    )(buf)


def sibling_swap(s):
    tile = 16 if s.dtype == BF16 else 8
    chunks = _row_chunks(s.shape[0], tile, 8)

    def body(s_ref, got_ref, send_sem, recv_sem):
        x, y, c = _me()
        for st, sz in chunks:
            sl = pl.ds(st, sz)
            _remote(s_ref.at[sl], got_ref.at[sl], send_sem, recv_sem, (x, y, 1 - c)).start()
        _remote(s_ref, got_ref, send_sem, recv_sem, (x, y, 1 - c)).wait()

    return pl.pallas_call(
        body, out_shape=jax.ShapeDtypeStruct(s.shape, s.dtype), in_specs=[ANY], out_specs=ANY,
        scratch_shapes=[pltpu.SemaphoreType.DMA, pltpu.SemaphoreType.DMA],
        name="sibling_swap", compiler_params=_params(),
    )(s)


def _heads(t, n_heads):
    b, s, w = t.shape
    return jnp.transpose(t.reshape(b, s, n_heads, w // n_heads), (0, 2, 1, 3)).reshape(b * n_heads, s, w // n_heads)


def _unheads(t, b):
    bh, s, d = t.shape
    return jnp.transpose(t.reshape(b, bh // b, s, d), (0, 2, 1, 3)).reshape(b, s, (bh // b) * d)


def _op(cache, fn, name, kinds, out_dims, **kw):
    key = (name, tuple(out_dims), tuple(sorted(kw.items())))
    if key not in cache:
        cache[key] = make_rowwise(fn, name, kinds, out_dims, **kw)
    return cache[key]


def _even_mixer(ops, a, w, n_ctx):
    b, s, d = a.shape
    proj = linear(a.reshape(b * s, d), w["e_w_in"]).reshape(b, s, -1)
    q, k, v, u = jnp.split(proj, [GQA_Q_W, GQA_Q_W + GQA_KV_W, GQA_Q_W + 2 * GQA_KV_W], axis=-1)
    cos, sin = _rope_tables(n_ctx, s - n_ctx, HEAD_DIM, 0, HEAD_DIM)
    shift = HEAD_DIM // 4
    qn = head_norm_rope(q, w["e_g_q"][None], jnp.tile(cos, (1, GQA_Q_HEADS)), jnp.tile(sin, (1, GQA_Q_HEADS)),
                        jnp.asarray(_rope_signs(HEAD_DIM, 0, HEAD_DIM, GQA_Q_HEADS)), GQA_Q_HEADS, shift)
    kn = head_norm_rope(k, w["e_g_k"][None], jnp.tile(cos, (1, GQA_KV_HEADS)), jnp.tile(sin, (1, GQA_KV_HEADS)),
                        jnp.asarray(_rope_signs(HEAD_DIM, 0, HEAD_DIM, GQA_KV_HEADS)), GQA_KV_HEADS, shift)
    att = gqa_attention(qn, kn, v, n_ctx)
    ys = []
    for dr in range(2):
        a_re, a_im, bb_re, bb_im = _s5_discretize(w["ssm_lam_re"][dr], w["ssm_lam_im"][dr], w["ssm_log_dt"][dr],
                                                  w["ssm_b_re"][dr], w["ssm_b_im"][dr])
        a_flat = jnp.stack([a_re.reshape(-1), a_im.reshape(-1)])
        b_bd = jnp.stack([_block_diag(jnp.swapaxes(bb_re, 1, 2)), _block_diag(jnp.swapaxes(bb_im, 1, 2))])
        c_bd = jnp.stack([_block_diag(jnp.swapaxes(w["ssm_c_re"][dr], 1, 2)),
                          _block_diag(jnp.swapaxes(w["ssm_c_im"][dr], 1, 2))])
        ys.append(s5_direction(u, a_flat, b_bd, c_bd, n_ctx, dr == 1))
    pre = _op(ops, _fn_glu_pre, "glu_pre", ("row", "row", "row", "glob"), (SSM_WIDTH,))
    post = _op(ops, _fn_glu_post, "glu_post", ("row", "row", "glob"), (SSM_WIDTH,))
    z = pre(u, ys[0], ys[1], w["ssm_d"][None])[0]
    t = linear(z.reshape(b * s, SSM_WIDTH), w["ssm_w_glu"]).reshape(b, s, SSM_WIDTH)
    ssm = post(z, t, w["ssm_b_glu"][None])[0]
    mix = jnp.concatenate([att, ssm], axis=-1)
    return linear(mix.reshape(b * s, -1), w["e_w_out"]).reshape(b, s, d)


def _odd_mixer(ops, a, w, n_ctx):
    b, s, d = a.shape
    w_in = jnp.pad(w["o_w_in"], ((0, 0), (0, ODD_IN_PAD - ODD_IN_W)))
    proj = linear(a.reshape(b * s, d), w_in).reshape(b, s, -1)
    c1 = MLA_Q_RANK
    c2 = c1 + MLA_KV_RANK
    c3 = c2 + MLA_ROPE
    cq, ckv, kr, nq, nk, nv, _ = jnp.split(proj, [c1, c2, c3, c3 + NA_W, c3 + 2 * NA_W, ODD_IN_W], axis=-1)
    nrm = lambda wd: _op(ops, _fn_norm, f"norm{wd}", ("row", "glob"), (wd,))
    cqn = nrm(MLA_Q_RANK)(cq, w["mla_g_cq"][None])[0]
    ckvn = nrm(MLA_KV_RANK)(ckv, w["mla_g_ckv"][None])[0]
    q = linear(cqn.reshape(b * s, -1), w["mla_w_uq"]).reshape(b, s, -1)
    kv = linear(ckvn.reshape(b * s, -1), w["mla_w_ukv"]).reshape(b, s, -1)
    cos, sin = _rope_tables(n_ctx, s - n_ctx, MLA_QK, MLA_NOPE, MLA_ROPE)
    cos, sin = jnp.tile(cos, (1, MLA_HEADS)), jnp.tile(sin, (1, MLA_HEADS))
    sg = jnp.asarray(_rope_signs(MLA_QK, MLA_NOPE, MLA_ROPE, MLA_HEADS))
    mq = head_norm_rope(q, w["mla_g_q"][None], cos, sin, sg, MLA_HEADS, MLA_ROPE // 4)
    mk = mla_k_prep(kv, kr, w["mla_g_k"][None], cos, sin, sg)
    mla = mla_attention(mq, mk, kv, n_ctx)
    nqn = head_norm_rope(nq, w["na_g_q"][None], None, None, None, NA_HEADS, 0)
    nkn = head_norm_rope(nk, w["na_g_k"][None], None, None, None, NA_HEADS, 0)
    na = na_attention_tm(nqn, nkn, nv, na_bias_table(w["na_rpb"]), n_ctx)
    mix = jnp.concatenate([mla, na], axis=-1)
    return linear(mix.reshape(b * s, -1), w["o_w_out"]).reshape(b, s, d)


_EVEN_KEYS = ("e_w_in", "e_w_out", "e_g_q", "e_g_k", "ssm_lam_re", "ssm_lam_im", "ssm_log_dt", "ssm_b_re", "ssm_b_im",
              "ssm_c_re", "ssm_c_im", "ssm_d", "ssm_w_glu", "ssm_b_glu")
_ODD_KEYS = ("o_w_in", "o_w_out", "mla_g_cq", "mla_g_ckv", "mla_w_uq", "mla_w_ukv", "mla_g_q", "mla_g_k", "na_g_q",
             "na_g_k", "na_rpb")


def _trunk(x_all, mods, w, n_ctx):
    ops = {}
    depth = mods.shape[0]
    b, s, d = x_all.shape
    modulate = _op(ops, _fn_modulate, "modulate", ("row", "glob", "seg", "seg"), (d,), nctx_rows=n_ctx)
    gated = make_gated_add(d, n_ctx)
    x = x_all
    for i in range(depth):
        j = i // 2
        m = [mods[i][:, :, r:r + 1, :] for r in range(N_MOD)]
        a = modulate(x, w["g_norm1"][i][None], m[0], m[1])[0]
        if i % 2 == 0:
            o = _even_mixer(ops, a, {k: w[k][j] for k in _EVEN_KEYS}, n_ctx)
        else:
            o = _odd_mixer(ops, a, {k: w[k][j] for k in _ODD_KEYS}, n_ctx)
        x = gated(x, o, m[2])
        a2 = modulate(x, w["g_norm2"][i][None], m[3], m[4])[0]
        f = ffn(a2.reshape(b * s, d), w["w_ff1"][i], w["w_ff2"][i]).reshape(b, s, d)
        x = gated(x, f, m[5])
    return x[:, n_ctx:]


def local_step(x, ctx, mods, w, loss_target):
    n_ctx = ctx.shape[1]
    x_all = jnp.concatenate([ctx, x], axis=1)
    y, vjp = jax.vjp(lambda xa, md, ww: _trunk(xa, md, ww, n_ctx), x_all, mods, w)
    loss_tile, dy = _loss_head(y, loss_target)
    dx_all, dmods, dw = vjp(dy)
    return loss_tile[0, 0], dx_all[:, n_ctx:], dmods, dw


_SHARDED = (("w_ff1", 2), ("w_ff2", 1), ("e_w_in", 2), ("e_w_out", 1), ("o_w_in", 2), ("o_w_out", 1),
            ("mla_w_uq", 2), ("mla_w_ukv", 2), ("ssm_w_glu", 1))
_SHARDED_SMALL = (("mla_g_cq", 1), ("mla_g_ckv", 1))
_REPLICATED = ("g_norm1", "g_norm2", "e_g_q", "e_g_k", "ssm_lam_re", "ssm_lam_im", "ssm_log_dt", "ssm_b_re", "ssm_b_im",
               "ssm_c_re", "ssm_c_im", "ssm_d", "ssm_b_glu", "mla_g_q", "mla_g_k", "na_g_q", "na_g_k", "na_rpb")
_WEIGHTS = ("c_ctx", "w_mod", "b_mod", "g_norm1", "g_norm2", "w_ff1", "w_ff2", "e_w_in", "e_w_out", "e_g_q", "e_g_k",
            "ssm_lam_re", "ssm_lam_im", "ssm_log_dt", "ssm_b_re", "ssm_b_im", "ssm_c_re", "ssm_c_im", "ssm_d",
            "ssm_w_glu", "ssm_b_glu", "o_w_in", "o_w_out", "mla_g_cq", "mla_g_ckv", "mla_w_uq", "mla_w_ukv", "mla_g_q",
            "mla_g_k", "na_g_q", "na_g_k", "na_rpb")
_PACK_ROWS = 64


def _pack(arrs, dtype, cols=1024, row_mult=_PACK_ROWS):
    blocks, tail, off = [], [], 0
    for a in arrs:
        n = int(np.prod(a.shape))
        if not tail and off % cols == 0 and n % cols == 0:
            blocks.append(a.astype(dtype).reshape(-1, cols))
        else:
            tail.append(a.astype(dtype).reshape(-1))
        off += n
    rows = -(-off // cols)
    pad = (-rows) % row_mult * cols + rows * cols - off
    if tail or pad:
        blocks.append(jnp.concatenate(tail + [jnp.zeros((pad,), dtype)]).reshape(-1, cols))
    return jnp.concatenate(blocks, axis=0)


def _unpack(packed, shapes):
    cols = packed.shape[-1]
    packed = packed.reshape(-1, cols)
    out, off = [], 0
    for sh in shapes:
        n = int(np.prod(sh))
        if off % cols == 0 and n % cols == 0:
            out.append(packed[off // cols:(off + n) // cols].reshape(sh))
        else:
            r0, r1 = off // cols, -(-(off + n) // cols)
            out.append(packed[r0:r1].reshape(-1)[off - r0 * cols:off - r0 * cols + n].reshape(sh))
        off += n
    return out


def _silu(t):
    return t * jax.nn.sigmoid(t)


def kernel(x, c, ctx, c_ctx, w_mod, b_mod, g_norm1, g_norm2, w_ff1, w_ff2, e_w_in, e_w_out, e_g_q, e_g_k, ssm_lam_re, ssm_lam_im, ssm_log_dt, ssm_b_re, ssm_b_im, ssm_c_re, ssm_c_im, ssm_d, ssm_w_glu, ssm_b_glu, o_w_in, o_w_out, mla_g_cq, mla_g_ckv, mla_w_uq, mla_w_ukv, mla_g_q, mla_g_k, na_g_q, na_g_k, na_rpb, loss_target, m_c_ctx, m_w_mod, m_b_mod, m_g_norm1, m_g_norm2, m_w_ff1, m_w_ff2, m_e_w_in, m_e_w_out, m_e_g_q, m_e_g_k, m_ssm_lam_re, m_ssm_lam_im, m_ssm_log_dt, m_ssm_b_re, m_ssm_b_im, m_ssm_c_re, m_ssm_c_im, m_ssm_d, m_ssm_w_glu, m_ssm_b_glu, m_o_w_in, m_o_w_out, m_mla_g_cq, m_mla_g_ckv, m_mla_w_uq, m_mla_w_ukv, m_mla_g_q, m_mla_g_k, m_na_g_q, m_na_g_k, m_na_rpb, v_c_ctx, v_w_mod, v_b_mod, v_g_norm1, v_g_norm2, v_w_ff1, v_w_ff2, v_e_w_in, v_e_w_out, v_e_g_q, v_e_g_k, v_ssm_lam_re, v_ssm_lam_im, v_ssm_log_dt, v_ssm_b_re, v_ssm_b_im, v_ssm_c_re, v_ssm_c_im, v_ssm_d, v_ssm_w_glu, v_ssm_b_glu, v_o_w_in, v_o_w_out, v_mla_g_cq, v_mla_g_ckv, v_mla_w_uq, v_mla_w_ukv, v_mla_g_q, v_mla_g_k, v_na_g_q, v_na_g_k, v_na_rpb):
    env = dict(locals())
    weights = {n: env[n] for n in _WEIGHTS}
    mom_m = {n: env["m_" + n] for n in _WEIGHTS}
    mom_v = {n: env["v_" + n] for n in _WEIGHTS}
    ax, ay, ac = _me()
    plane = 2 * ax + ay
    dev = 4 * ax + 2 * ay + ac
    b_loc, d = c.shape
    depth = w_mod.shape[0]
    n_all = N_DEV * b_loc
    mod_cols = w_mod.shape[2]

    big = _pack([weights[n] for n, _ in _SHARDED], BF16)
    small = _pack([weights[n] for n, _ in _SHARDED_SMALL], F32, cols=LANE, row_mult=8)
    g_big, g_small = plane_allgather(big, small)
    full = {n: weights[n] for n in _REPLICATED}
    parts = [_unpack(g_big[j], [weights[n].shape for n, _ in _SHARDED]) for j in range(N_PLANE)]
    for t, (n, axis) in enumerate(_SHARDED):
        full[n] = [jnp.concatenate([parts[j][t][l] for j in range(N_PLANE)], axis=axis - 1).astype(F32)
                   for l in range(weights[n].shape[0])]
    parts_s = [_unpack(g_small[j], [weights[n].shape for n, _ in _SHARDED_SMALL]) for j in range(N_PLANE)]
    for t, (n, axis) in enumerate(_SHARDED_SMALL):
        full[n] = jnp.concatenate([parts_s[j][t] for j in range(N_PLANE)], axis=axis)

    rows_pad = 8 * ((n_all + 1 + 7) // 8)
    c_all = allgather8(jnp.pad(c, ((0, 8 - b_loc), (0, 0)))).reshape(N_DEV, 8, d)[:, :b_loc].reshape(n_all, d)
    cond_raw = jnp.concatenate([c_all, c_ctx[None], jnp.zeros((rows_pad - n_all - 1, d), F32)], axis=0)
    b_cols = lax.dynamic_slice_in_dim(b_mod, plane * mod_cols, mod_cols, axis=1)
    mod_loc = jnp.stack([_mm(cond_raw, w_mod[i], a_act="silu") + b_cols[i][None] for i in range(depth)])
    mod_g = allgather8(mod_loc.reshape(depth * rows_pad, mod_cols)).reshape(N_PLANE, 2, depth, rows_pad, mod_cols)
    mod_all = jnp.concatenate([mod_g[j, 0] for j in range(N_PLANE)], axis=-1)
    m_lat = lax.dynamic_slice_in_dim(mod_all, dev * b_loc, b_loc, axis=1)
    m_ctx = jnp.broadcast_to(mod_all[:, n_all][:, None], m_lat.shape)
    mods = jnp.stack([m_ctx, m_lat], axis=2).reshape(depth, b_loc, 2, N_MOD, d)

    loss_part, grad_x, dmods, dw = local_step(x, ctx, mods, full, loss_target)
    loss = lax.psum(loss_part, ("x", "y", "c"))

    dm = dmods.reshape(depth, b_loc, 2, N_MOD * d)
    dm_rows = jnp.concatenate([dm[:, :, 1], jnp.sum(dm[:, :, 0], axis=1, keepdims=True)], axis=1)
    rep_shapes = [weights[n].shape for n in _REPLICATED]
    small_pack = _pack([dm_rows] + [dw[n] for n in _REPLICATED], F32, cols=1024, row_mult=8)
    sp_rows = small_pack.shape[0]
    gathered = allgather8(small_pack).reshape(N_DEV, sp_rows, 1024)
    n_dm = depth * (b_loc + 1) * N_MOD * d
    dm_all = gathered.reshape(N_DEV, -1)[:, :n_dm].reshape(N_DEV, depth, b_loc + 1, N_MOD * d)
    rep_sum = _sum_rows(gathered, N_DEV).reshape(-1)
    rep_grads = dict(zip(_REPLICATED, _unpack(rep_sum[n_dm:], rep_shapes)))
    d_ctx_row = rep_sum[:n_dm].reshape(depth, b_loc + 1, N_MOD * d)[:, b_loc]
    d_lat_rows = jnp.transpose(dm_all[:, :, :b_loc], (1, 0, 2, 3)).reshape(depth, n_all, N_MOD * d)
    d_mod_all = jnp.concatenate([d_lat_rows, d_ctx_row[:, None],
                                 jnp.zeros((depth, rows_pad - n_all - 1, N_MOD * d), F32)], axis=1)
    grads = dict(rep_grads)
    grads["b_mod"] = jnp.sum(d_mod_all, axis=1)
    d_cols = lax.dynamic_slice_in_dim(d_mod_all, plane * mod_cols, mod_cols, axis=2)
    grads["w_mod"] = jnp.stack([_mm(cond_raw, d_cols[i], ta=True, a_act="silu") for i in range(depth)])
    d_cond = _mm(d_cols[0], w_mod[0], tb=True)
    for i in range(1, depth):
        d_cond = _add2(d_cond, _mm(d_cols[i], w_mod[i], tb=True))
    d_cond_g = allgather8(d_cond[n_all:n_all + 8] if rows_pad - n_all >= 8 else
                          jnp.pad(d_cond[n_all:], ((0, 8 - (rows_pad - n_all)), (0, 0)))).reshape(N_PLANE, 2, 8, d)
    d_silu = _sum_rows(d_cond_g[:, 0], N_PLANE)[0]
    sg = jax.nn.sigmoid(c_ctx)
    grads["c_ctx"] = d_silu * (sg * (1.0 + c_ctx * (1.0 - sg)))

    def shards_of(g, axis, j):
        layers = g if isinstance(g, (list, tuple)) else [g]
        ax = axis - 1 if isinstance(g, (list, tuple)) else axis
        n = layers[0].shape[ax] // N_PLANE
        return [lax.slice_in_dim(t, j * n, (j + 1) * n, axis=ax) for t in layers]

    send = jnp.stack([_pack([t for n, axis in _SHARDED + _SHARDED_SMALL for t in shards_of(dw[n], axis, j)], BF16)
                      for j in range(N_PLANE)])
    rows_h = send.shape[1] // 2
    send = send.reshape(N_PLANE, 2, rows_h, 1024)
    mine = lax.dynamic_index_in_dim(send, ac, 1, keepdims=False).reshape(N_PLANE * rows_h, 1024)
    theirs = sibling_halves(send).reshape(N_PLANE * rows_h, 1024)
    chip_sum = _accumulate([mine, theirs], BF16).reshape(N_PLANE, rows_h, 1024)
    own = lax.dynamic_index_in_dim(chip_sum, plane, 0, keepdims=False)
    done = _accumulate([own, plane_scatter(chip_sum)], BF16)
    both = jnp.stack([done, sibling_swap(done)])
    flat = jnp.where(ac == 0, both, both[::-1]).astype(F32).reshape(-1, 1024)
    shard_shapes = [weights[n].shape for n, _ in _SHARDED] + [weights[n].shape for n, _ in _SHARDED_SMALL]
    for (n, _), g in zip(_SHARDED + _SHARDED_SMALL, _unpack(flat, shard_shapes)):
        grads[n] = g

    big_names = ("w_mod",) + tuple(n for n, _ in _SHARDED)
    small_names = tuple(n for n in _WEIGHTS if n not in big_names)
    delta, new_m, new_v = {}, {}, {}
    for n in big_names:
        delta[n], new_m[n], new_v[n] = _adamw(weights[n], grads[n], mom_m[n], mom_v[n])
    sm_shapes = [weights[n].shape for n in small_names]
    packed = [_pack([src[n] for n in small_names], F32, cols=1024, row_mult=8)
              for src in (weights, grads, mom_m, mom_v)]
    for dst, res in zip((delta, new_m, new_v), _adamw(*packed)):
        dst.update(dict(zip(small_names, _unpack(res, sm_shapes))))

    return (loss, grad_x, *[grads[n] for n in _WEIGHTS], *[delta[n] for n in _WEIGHTS],
            *[new_m[n] for n in _WEIGHTS], *[new_v[n] for n in _WEIGHTS])
```

```python
import functools
import math

import numpy as np
import jax
import jax.numpy as jnp
from jax import lax
from jax.experimental import pallas as pl
from jax.experimental.pallas import tpu as pltpu

F32 = jnp.float32
BF16 = jnp.bfloat16
HI = lax.Precision.HIGHEST
MESH = pl.DeviceIdType.MESH
ANY = pl.BlockSpec(memory_space=pl.ANY)
VMEM_SPEC = pl.BlockSpec(memory_space=pltpu.VMEM)

GRID_W = 64
HEAD_DIM = 64
ROPE_BASE = 10000.0
EPS = 1e-6
N_MOD = 6
GQA_Q_HEADS, GQA_KV_HEADS = 12, 4
GQA_Q_W, GQA_KV_W = GQA_Q_HEADS * HEAD_DIM, GQA_KV_HEADS * HEAD_DIM
SSM_WIDTH, SSM_GROUP, SSM_STATE = 256, 16, 64
SSM_GROUPS = SSM_WIDTH // SSM_GROUP
SSM_LANES = SSM_GROUPS * SSM_STATE
MLA_HEADS, MLA_Q_RANK, MLA_KV_RANK, MLA_NOPE, MLA_ROPE, MLA_V = 8, 512, 256, 64, 32, 64
MLA_QK = MLA_NOPE + MLA_ROPE
NA_HEADS, NA_WIN_R, NA_WIN_C = 8, 8, 16
NA_W = NA_HEADS * HEAD_DIM
NA_BAND = NA_WIN_R * GRID_W
ODD_IN_W = MLA_Q_RANK + MLA_KV_RANK + MLA_ROPE + 3 * NA_W
ODD_IN_PAD = 2560
ADAM_LR, ADAM_B1, ADAM_B2, ADAM_EPS, ADAM_WD, ADAM_STEP = 0.001, 0.9, 0.999, 1e-08, 0.01, 10
NEG = -1e30
VMEM_LIMIT = 56 * 1024 * 1024
LANE = 128
MM_TILE_M = (1152, 1024, 768, 512, 256, 128)
MM_TILE_N = (1280, 1024, 768, 512, 256, 128)
MM_TILE_K = (1152, 1024, 768, 512, 256, 128)
N_PLANE = 4
N_DEV = 8


def _pick(n, cands):
    for c in cands:
        if n % c == 0:
            return c
    return n


def _params(**kw):
    return pltpu.CompilerParams(vmem_limit_bytes=VMEM_LIMIT, **kw)


def _mm(a, b, *, ta=False, tb=False, a_act=None, epi=None, e=None, exact=False, out_dtype=F32):
    m, kd = (a.shape[1], a.shape[0]) if ta else a.shape
    n = b.shape[0] if tb else b.shape[1]
    tm = _pick(m, MM_TILE_M)
    tn = _pick(n, MM_TILE_N)
    tk = _pick(kd, MM_TILE_K)
    nk = kd // tk
    dn = (((0 if ta else 1,), (1 if tb else 0,)), ((), ()))
    narrow = jnp.dtype(out_dtype) != jnp.dtype(F32)
    assert not (narrow and epi is not None)

    def body(*refs):
        if narrow:
            a_ref, b_ref, out_ref, o_ref = refs
        elif epi is None:
            a_ref, b_ref, o_ref = refs
        else:
            a_ref, b_ref, e_ref, o_ref = refs
        k = pl.program_id(2)
        av = a_ref[...]
        if a_act == "relu2":
            av = jnp.square(jnp.maximum(av, 0.0))
        elif a_act == "silu":
            av = av * jax.nn.sigmoid(av)
        bv = b_ref[...]
        if exact:
            p = lax.dot_general(av, bv, dn, precision=HI, preferred_element_type=F32)
        else:
            p = lax.dot_general(av.astype(BF16), bv.astype(BF16), dn, preferred_element_type=F32)

        @pl.when(k == 0)
        def _():
            o_ref[...] = p

        @pl.when(k > 0)
        def _():
            o_ref[...] += p

        if epi == "drelu2":
            @pl.when(k == nk - 1)
            def _():
                o_ref[...] = o_ref[...] * (2.0 * jnp.maximum(e_ref[...], 0.0))

        if narrow:
            @pl.when(k == nk - 1)
            def _():
                out_ref[...] = o_ref[...].astype(out_dtype)

    a_spec = pl.BlockSpec((tk, tm), lambda i, j, k: (k, i)) if ta else pl.BlockSpec((tm, tk), lambda i, j, k: (i, k))
    b_spec = pl.BlockSpec((tn, tk), lambda i, j, k: (j, k)) if tb else pl.BlockSpec((tk, tn), lambda i, j, k: (k, j))
    o_spec = pl.BlockSpec((tm, tn), lambda i, j, k: (i, j))
    ins, specs = [a, b], [a_spec, b_spec]
    if epi is not None:
        ins.append(e)
        specs.append(o_spec)
    name = f"mm_{m}x{kd}x{n}_{int(ta)}{int(tb)}_{a_act}_{epi}_{int(exact)}_{jnp.dtype(out_dtype).name}"
    return pl.pallas_call(
        body, out_shape=jax.ShapeDtypeStruct((m, n), out_dtype), grid=(m // tm, n // tn, nk),
        in_specs=specs, out_specs=o_spec, name=name, compiler_params=_params(),
        scratch_shapes=[pltpu.VMEM((tm, tn), F32)] if narrow else [],
    )(*ins)


@functools.partial(jax.custom_vjp, nondiff_argnums=(2,))
def _linear(a, w, exact):
    return _mm(a, w, exact=exact)


def _linear_fwd(a, w, exact):
    return _mm(a, w, exact=exact), (a, w)


def _linear_bwd(exact, res, g):
    a, w = res
    return _mm(g, w, tb=True, exact=exact), _mm(a, g, ta=True, exact=exact, out_dtype=w.dtype)


_linear.defvjp(_linear_fwd, _linear_bwd)


def linear(a, w, exact=False):
    return _linear(a, w, exact)


@jax.custom_vjp
def ffn(a, w1, w2):
    return _mm(_mm(a, w1), w2, a_act="relu2")


def _ffn_fwd(a, w1, w2):
    h1 = _mm(a, w1)
    return _mm(h1, w2, a_act="relu2"), (a, w1, w2, h1)


def _ffn_bwd(res, g):
    a, w1, w2, h1 = res
    dh1 = _mm(g, w2, tb=True, epi="drelu2", e=h1)
    dw2 = _mm(h1, g, ta=True, a_act="relu2", out_dtype=w2.dtype)
    return _mm(dh1, w1, tb=True), _mm(a, dh1, ta=True, out_dtype=w1.dtype), dw2


ffn.defvjp(_ffn_fwd, _ffn_bwd)


def make_rowwise(fn, name, kinds, out_dims, nctx_rows=0, whole_seq=False):
    n_in = len(kinds)
    n_out = len(out_dims)
    diff = [i for i, kd in enumerate(kinds) if kd in ("row", "glob", "seg")]

    def layout(args):
        row0 = args[kinds.index("row")]
        g, s = row0.shape[0], row0.shape[1]
        ts = s if whole_seq else (min(256, nctx_rows) if nctx_rows else _pick(s, (256, 128, 64)))
        nctx = nctx_rows // ts
        return g, s, ts, nctx

    def spec_of(kind, arr, ts, nctx):
        if kind == "row":
            return pl.BlockSpec((None, ts, arr.shape[2]), lambda g, i: (g, i, 0))
        if kind == "tab":
            return pl.BlockSpec((ts, arr.shape[1]), lambda g, i: (i, 0))
        if kind in ("const", "glob"):
            return pl.BlockSpec(arr.shape, lambda g, i: (0, 0))
        return pl.BlockSpec((None, None) + arr.shape[2:], lambda g, i: (g, (i >= nctx).astype(jnp.int32), 0, 0))

    def fwd_call(*args):
        g, s, ts, nctx = layout(args)

        def body(*refs):
            vals = [r[...] for r in refs[:n_in]]
            outs = fn(*vals)
            for o_ref, o in zip(refs[n_in:], outs):
                o_ref[...] = o

        return pl.pallas_call(
            body, out_shape=[jax.ShapeDtypeStruct((g, s, d), F32) for d in out_dims], grid=(g, s // ts),
            in_specs=[spec_of(kd, a, ts, nctx) for kd, a in zip(kinds, args)],
            out_specs=[pl.BlockSpec((None, ts, d), lambda g_, i: (g_, i, 0)) for d in out_dims],
            name=f"{name}_f_{g}x{s}", compiler_params=_params(),
        )(*args)

    def bwd_call(args, cts):
        g, s, ts, nctx = layout(args)

        def body(*refs):
            in_refs, ct_refs, out_refs = refs[:n_in], refs[n_in:n_in + n_out], refs[n_in + n_out:]
            gi, i = pl.program_id(0), pl.program_id(1)
            vals = [r[...] for r in in_refs]

            def f(*dv):
                full = list(vals)
                for idx, v in zip(diff, dv):
                    full[idx] = v
                return tuple(fn(*full))

            _, vjp = jax.vjp(f, *[vals[idx] for idx in diff])
            grads = vjp(tuple(r[...] for r in ct_refs))
            for idx, o_ref, gr in zip(diff, out_refs, grads):
                if kinds[idx] == "row":
                    o_ref[...] = gr
                    continue
                if kinds[idx] == "glob":
                    first = jnp.logical_and(gi == 0, i == 0)
                else:
                    first = jnp.logical_or(i == 0, i == nctx)

                @pl.when(first)
                def _(o_ref=o_ref, gr=gr):
                    o_ref[...] = gr

                @pl.when(jnp.logical_not(first))
                def _(o_ref=o_ref, gr=gr):
                    o_ref[...] += gr

        in_specs = [spec_of(kd, a, ts, nctx) for kd, a in zip(kinds, args)]
        in_specs += [pl.BlockSpec((None, ts, d), lambda g_, i: (g_, i, 0)) for d in out_dims]
        return pl.pallas_call(
            body, out_shape=[jax.ShapeDtypeStruct(args[idx].shape, F32) for idx in diff], grid=(g, s // ts),
            in_specs=in_specs, out_specs=[spec_of(kinds[idx], args[idx], ts, nctx) for idx in diff],
            name=f"{name}_b_{g}x{s}", compiler_params=_params(),
        )(*args, *cts)

    @jax.custom_vjp
    def op(*args):
        return tuple(fwd_call(*args))

    def op_fwd(*args):
        return tuple(fwd_call(*args)), args

    def op_bwd(args, cts):
        grads = bwd_call(args, cts)
        full = [None] * n_in
        for idx, gr in zip(diff, grads):
            full[idx] = gr
        return tuple(jnp.zeros_like(a) if gfull is None else gfull for a, gfull in zip(args, full))

    op.defvjp(op_fwd, op_bwd)
    op.fwd_call, op.bwd_call = fwd_call, bwd_call
    return op


def make_gated_add(d, n_ctx):
    add = make_rowwise(_fn_gated_add, "gated", ("row", "row", "seg"), (d,), nctx_rows=n_ctx)
    mul = make_rowwise(_fn_gate_mul, "gate_mul", ("row", "seg"), (d,), nctx_rows=n_ctx)

    @jax.custom_vjp
    def op(x, o, gate):
        return add.fwd_call(x, o, gate)[0]

    def fwd(x, o, gate):
        return add.fwd_call(x, o, gate)[0], (o, gate)

    def bwd(res, ct):
        do, dgate = mul.bwd_call(res, (ct,))
        return ct, do, dgate

    op.defvjp(fwd, bwd)
    return op


def _rms(x):
    return lax.rsqrt(jnp.mean(x * x, axis=-1, keepdims=True) + EPS)


def _fn_modulate(x, g, shift, scale):
    return ((x * _rms(x) * g) * (1.0 + scale) + shift,)


def _fn_gated_add(x, o, gate):
    return (x + gate * o,)


def _fn_gate_mul(o, gate):
    return (gate * o,)


def _fn_norm(x, g):
    return (x * _rms(x) * g,)


def _fn_norm_rope(x, cos, sin, rot, g):
    y = x * _rms(x) * g
    r = jnp.dot(y, rot, precision=HI, preferred_element_type=F32)
    return (y * cos + r * sin,)


def _fn_glu_pre(u, y0, y1, d):
    return (jax.nn.gelu(d * u + y0 + y1),)


def _fn_glu_post(z, t, bg):
    return (z * jax.nn.sigmoid(t + bg),)


def _rope_matrix(dh, start, rot_dim):
    r = np.zeros((dh, dh), np.float32)
    q = rot_dim // 4
    for j in range(rot_dim):
        if (j // q) % 2 == 0:
            r[start + j + q, start + j] = -1.0
        else:
            r[start + j - q, start + j] = 1.0
    return r


def _rope_tables(n_ctx, n_lat, dh, start, rot_dim):
    t = jnp.arange(n_lat)
    rows = (t // GRID_W).astype(F32)
    cols = (t % GRID_W).astype(F32)
    axis_dim = rot_dim // 2
    freqs = ROPE_BASE ** (-jnp.arange(0, axis_dim, 2, dtype=F32) / axis_dim)
    ang_r = rows[:, None] * freqs
    ang_c = cols[:, None] * freqs
    ang = jnp.concatenate([ang_r, ang_r, ang_c, ang_c], axis=-1)
    cos = jnp.concatenate([jnp.ones((n_lat, start), F32), jnp.cos(ang)], axis=-1)
    sin = jnp.concatenate([jnp.zeros((n_lat, start), F32), jnp.sin(ang)], axis=-1)
    cos = jnp.concatenate([jnp.ones((n_ctx, dh), F32), cos], axis=0)
    sin = jnp.concatenate([jnp.zeros((n_ctx, dh), F32), sin], axis=0)
    return cos, sin


_NT = (((1,), (1,)), ((), ()))
_TN = (((0,), (0,)), ((), ()))


def _attn_fwd(q, k, v, group, n_ctx, scale):
    b, h, s, dq = q.shape
    dv = v.shape[-1]
    tq = min(256, n_ctx)
    nc = n_ctx // tq

    def body(q_ref, k_ref, v_ref, o_ref, lse_ref):
        qv = (q_ref[...] * scale).astype(BF16)

        def run(n_keys):
            sc = lax.dot_general(qv, k_ref[0:n_keys, :].astype(BF16), _NT, preferred_element_type=F32)
            m = jnp.max(sc, axis=-1, keepdims=True)
            p = jnp.exp(sc - m)
            l = jnp.sum(p, axis=-1, keepdims=True)
            o = jnp.dot(p.astype(BF16), v_ref[0:n_keys, :].astype(BF16), preferred_element_type=F32)
            o_ref[...] = o / l
            lse_ref[...] = m + jnp.log(l)

        pl.when(pl.program_id(2) < nc)(lambda: run(n_ctx))
        pl.when(pl.program_id(2) >= nc)(lambda: run(s))

    return pl.pallas_call(
        body, out_shape=[jax.ShapeDtypeStruct((b, h, s, dv), F32), jax.ShapeDtypeStruct((b, h, s, 1), F32)],
        grid=(b, h, s // tq),
        in_specs=[pl.BlockSpec((None, None, tq, dq), lambda bi, hi, i: (bi, hi, i, 0)),
                  pl.BlockSpec((None, None, s, dq), lambda bi, hi, i: (bi, lax.div(hi, group), 0, 0)),
                  pl.BlockSpec((None, None, s, dv), lambda bi, hi, i: (bi, lax.div(hi, group), 0, 0))],
        out_specs=[pl.BlockSpec((None, None, tq, dv), lambda bi, hi, i: (bi, hi, i, 0)),
                   pl.BlockSpec((None, None, tq, 1), lambda bi, hi, i: (bi, hi, i, 0))],
        name=f"attn_f_{h}x{s}x{dq}", compiler_params=_params(),
    )(q, k, v)


def _attn_dq(q, k, v, o, lse, do, group, n_ctx, scale):
    b, h, s, dq = q.shape
    dv = v.shape[-1]
    tq = min(256, n_ctx)
    nc = n_ctx // tq

    def body(q_ref, k_ref, v_ref, o_ref, lse_ref, do_ref, dq_ref, delta_ref):
        qv = (q_ref[...] * scale).astype(BF16)
        dov = do_ref[...]
        delta = jnp.sum(dov * o_ref[...], axis=-1, keepdims=True)
        delta_ref[...] = delta

        def run(n_keys):
            kv = k_ref[0:n_keys, :].astype(BF16)
            sc = lax.dot_general(qv, kv, _NT, preferred_element_type=F32)
            p = jnp.exp(sc - lse_ref[...])
            dp = lax.dot_general(dov.astype(BF16), v_ref[0:n_keys, :].astype(BF16), _NT, preferred_element_type=F32)
            ds = p * (dp - delta)
            dq_ref[...] = jnp.dot(ds.astype(BF16), kv, preferred_element_type=F32) * scale

        pl.when(pl.program_id(2) < nc)(lambda: run(n_ctx))
        pl.when(pl.program_id(2) >= nc)(lambda: run(s))

    qs = lambda d: pl.BlockSpec((None, None, tq, d), lambda bi, hi, i: (bi, hi, i, 0))
    ks = lambda d: pl.BlockSpec((None, None, s, d), lambda bi, hi, i: (bi, lax.div(hi, group), 0, 0))
    return pl.pallas_call(
        body, out_shape=[jax.ShapeDtypeStruct((b, h, s, dq), F32), jax.ShapeDtypeStruct((b, h, s, 1), F32)],
        grid=(b, h, s // tq),
        in_specs=[qs(dq), ks(dq), ks(dv), qs(dv), qs(1), qs(dv)], out_specs=[qs(dq), qs(1)],
        name=f"attn_dq_{h}x{s}x{dq}", compiler_params=_params(),
    )(q, k, v, o, lse, do)


def _attn_dkv(q, k, v, lse, delta, do, group, n_ctx, scale):
    b, h, s, dq = q.shape
    hk = k.shape[1]
    dv = v.shape[-1]
    tk = min(256, n_ctx)
    nc = n_ctx // tk

    def body(q_ref, k_ref, v_ref, lse_ref, delta_ref, do_ref, dk_ref, dv_ref):
        kv = k_ref[...].astype(BF16)
        vv = v_ref[...].astype(BF16)

        def run(r0):
            dk = jnp.zeros((tk, dq), F32)
            dvv = jnp.zeros((tk, dv), F32)
            for g in range(group):
                qg = (q_ref[g, r0:s, :] * scale).astype(BF16)
                dog = do_ref[g, r0:s, :].astype(BF16)
                sc = lax.dot_general(qg, kv, _NT, preferred_element_type=F32)
                p = jnp.exp(sc - lse_ref[g, r0:s, :])
                dvv = dvv + lax.dot_general(p.astype(BF16), dog, _TN, preferred_element_type=F32)
                dp = lax.dot_general(dog, vv, _NT, preferred_element_type=F32)
                ds = p * (dp - delta_ref[g, r0:s, :])
                dk = dk + lax.dot_general(ds.astype(BF16), qg, _TN, preferred_element_type=F32)
            dk_ref[...] = dk
            dv_ref[...] = dvv

        pl.when(pl.program_id(2) < nc)(lambda: run(0))
        pl.when(pl.program_id(2) >= nc)(lambda: run(n_ctx))

    gs = lambda d: pl.BlockSpec((None, group, s, d), lambda bi, hi, j: (bi, hi, 0, 0))
    ks = lambda d: pl.BlockSpec((None, None, tk, d), lambda bi, hi, j: (bi, hi, j, 0))
    return pl.pallas_call(
        body, out_shape=[jax.ShapeDtypeStruct((b, hk, s, dq), F32), jax.ShapeDtypeStruct((b, hk, s, dv), F32)],
        grid=(b, hk, s // tk),
        in_specs=[gs(dq), ks(dq), ks(dv), gs(1), gs(1), gs(dv)], out_specs=[ks(dq), ks(dv)],
        name=f"attn_dkv_{h}x{s}x{dq}", compiler_params=_params(),
    )(q, k, v, lse, delta, do)


@functools.partial(jax.custom_vjp, nondiff_argnums=(3, 4, 5))
def attention(q, k, v, group, n_ctx, scale):
    return _attn_fwd(q, k, v, group, n_ctx, scale)[0]


def _attention_fwd(q, k, v, group, n_ctx, scale):
    o, lse = _attn_fwd(q, k, v, group, n_ctx, scale)
    return o, (q, k, v, o, lse)


def _attention_bwd(group, n_ctx, scale, res, do):
    q, k, v, o, lse = res
    dq, delta = _attn_dq(q, k, v, o, lse, do, group, n_ctx, scale)
    dk, dv = _attn_dkv(q, k, v, lse, delta, do, group, n_ctx, scale)
    return dq, dk, dv


attention.defvjp(_attention_fwd, _attention_bwd)


def _na_geometry(i, nc, rows):
    r = i - nc
    rs = jnp.clip(r - NA_WIN_R // 2, 0, rows - NA_WIN_R)
    is_ctx = i < nc
    cls = jnp.where(is_ctx, NA_WIN_R, r - rs)
    return jnp.where(is_ctx, 0, rs), cls


def _na_scores(q_ref, k_ref, bias_ref, hd, n_ctx, start, scale):
    qv = (q_ref[hd] * scale).astype(BF16)
    kc = k_ref[hd, 0:n_ctx, :].astype(BF16)
    kb = k_ref[hd, pl.ds(start, NA_BAND), :].astype(BF16)
    s_c = lax.dot_general(qv, kc, _NT, preferred_element_type=F32)
    s_l = lax.dot_general(qv, kb, _NT, preferred_element_type=F32) + bias_ref[hd]
    return qv, kc, kb, s_c, s_l


NA_HEADS_FWD = 4
NA_HEADS_BWD = 2


def _na_specs(hp, s, dh, nc, rows):
    qs = lambda d: pl.BlockSpec((None, hp, GRID_W, d), lambda bi, hg, i: (bi, hg, i, 0))
    ks = pl.BlockSpec((None, hp, s, dh), lambda bi, hg, i: (bi, hg, 0, 0))
    bs = pl.BlockSpec((hp, None, GRID_W, NA_BAND), lambda bi, hg, i: (hg, _na_geometry(i, nc, rows)[1], 0, 0))
    return qs, ks, bs


def _na_fwd(q, k, v, bias, n_ctx):
    b, h, s, dh = q.shape
    nc = n_ctx // GRID_W
    rows = (s - n_ctx) // GRID_W
    scale = dh ** -0.5
    hp = math.gcd(h, NA_HEADS_FWD)

    def body(q_ref, k_ref, v_ref, bias_ref, o_ref, lse_ref):
        rs, _ = _na_geometry(pl.program_id(2), nc, rows)
        start = pl.multiple_of(n_ctx + rs * GRID_W, GRID_W)
        for hd in range(hp):
            _, _, _, s_c, s_l = _na_scores(q_ref, k_ref, bias_ref, hd, n_ctx, start, scale)
            m = jnp.maximum(jnp.max(s_c, axis=-1, keepdims=True), jnp.max(s_l, axis=-1, keepdims=True))
            p_c = jnp.exp(s_c - m)
            p_l = jnp.exp(s_l - m)
            l = jnp.sum(p_c, axis=-1, keepdims=True) + jnp.sum(p_l, axis=-1, keepdims=True)
            o = jnp.dot(p_c.astype(BF16), v_ref[hd, 0:n_ctx, :].astype(BF16), preferred_element_type=F32)
            o = o + jnp.dot(p_l.astype(BF16), v_ref[hd, pl.ds(start, NA_BAND), :].astype(BF16),
                            preferred_element_type=F32)
            o_ref[hd] = o / l
            lse_ref[hd] = m + jnp.log(l)

    qs, ks, bs = _na_specs(hp, s, dh, nc, rows)
    return pl.pallas_call(
        body, out_shape=[jax.ShapeDtypeStruct((b, h, s, dh), F32), jax.ShapeDtypeStruct((b, h, s, 1), F32)],
        grid=(b, h // hp, s // GRID_W), in_specs=[qs(dh), ks, ks, bs], out_specs=[qs(dh), qs(1)],
        name=f"na_f_{s}", compiler_params=_params(),
    )(q, k, v, bias)


def _na_bwd(q, k, v, bias, o, lse, do, n_ctx):
    b, h, s, dh = q.shape
    nc = n_ctx // GRID_W
    rows = (s - n_ctx) // GRID_W
    scale = dh ** -0.5
    n_cls = NA_WIN_R + 1
    hp = math.gcd(h, NA_HEADS_BWD)

    def body(q_ref, k_ref, v_ref, bias_ref, o_ref, lse_ref, do_ref, dq_ref, dk_ref, dv_ref, db_ref):
        i = pl.program_id(2)
        rs, cls = _na_geometry(i, nc, rows)
        _, cls_prev = _na_geometry(i - 1, nc, rows)
        start = pl.multiple_of(n_ctx + rs * GRID_W, GRID_W)
        first = jnp.logical_or(i == 0, cls != cls_prev)

        @pl.when(i == 0)
        def _():
            dk_ref[...] = jnp.zeros_like(dk_ref)
            dv_ref[...] = jnp.zeros_like(dv_ref)

        for hd in range(hp):
            qv, kc, kb, s_c, s_l = _na_scores(q_ref, k_ref, bias_ref, hd, n_ctx, start, scale)
            lse_v = lse_ref[hd]
            p_c = jnp.exp(s_c - lse_v)
            p_l = jnp.exp(s_l - lse_v)
            dov = do_ref[hd]
            dob = dov.astype(BF16)
            delta = jnp.sum(dov * o_ref[hd], axis=-1, keepdims=True)
            vc = v_ref[hd, 0:n_ctx, :].astype(BF16)
            vb = v_ref[hd, pl.ds(start, NA_BAND), :].astype(BF16)
            ds_c = p_c * (lax.dot_general(dob, vc, _NT, preferred_element_type=F32) - delta)
            ds_l = p_l * (lax.dot_general(dob, vb, _NT, preferred_element_type=F32) - delta)
            dsc_b = ds_c.astype(BF16)
            dsl_b = ds_l.astype(BF16)
            dq_ref[hd] = (jnp.dot(dsc_b, kc, preferred_element_type=F32)
                          + jnp.dot(dsl_b, kb, preferred_element_type=F32)) * scale
            dk_ref[hd, 0:n_ctx, :] += lax.dot_general(dsc_b, qv, _TN, preferred_element_type=F32)
            dk_ref[hd, pl.ds(start, NA_BAND), :] += lax.dot_general(dsl_b, qv, _TN, preferred_element_type=F32)
            dv_ref[hd, 0:n_ctx, :] += lax.dot_general(p_c.astype(BF16), dob, _TN, preferred_element_type=F32)
            dv_ref[hd, pl.ds(start, NA_BAND), :] += lax.dot_general(p_l.astype(BF16), dob, _TN, preferred_element_type=F32)

            @pl.when(first)
            def _(hd=hd, ds_l=ds_l):
                db_ref[hd] = ds_l

            @pl.when(jnp.logical_not(first))
            def _(hd=hd, ds_l=ds_l):
                db_ref[hd] += ds_l

    qs, ks, bs = _na_specs(hp, s, dh, nc, rows)
    dbs = pl.BlockSpec((None, hp, None, GRID_W, NA_BAND),
                       lambda bi, hg, i: (bi, hg, _na_geometry(i, nc, rows)[1], 0, 0))
    return pl.pallas_call(
        body,
        out_shape=[jax.ShapeDtypeStruct((b, h, s, dh), F32), jax.ShapeDtypeStruct((b, h, s, dh), F32),
                   jax.ShapeDtypeStruct((b, h, s, dh), F32), jax.ShapeDtypeStruct((b, h, n_cls, GRID_W, NA_BAND), F32)],
        grid=(b, h // hp, s // GRID_W), in_specs=[qs(dh), ks, ks, bs, qs(dh), qs(1), qs(dh)],
        out_specs=[qs(dh), ks, ks, dbs], name=f"na_b_{s}", compiler_params=_params(),
    )(q, k, v, bias, o, lse, do)


@functools.partial(jax.custom_vjp, nondiff_argnums=(4,))
def na_attention(q, k, v, bias, n_ctx):
    return _na_fwd(q, k, v, bias, n_ctx)[0]


def _na_attention_fwd(q, k, v, bias, n_ctx):
    o, lse = _na_fwd(q, k, v, bias, n_ctx)
    return o, (q, k, v, bias, o, lse)


def _na_attention_bwd(n_ctx, res, do):
    q, k, v, bias, o, lse = res
    dq, dk, dv, db = _na_bwd(q, k, v, bias, o, lse, do, n_ctx)
    return dq, dk, dv, jnp.sum(db, axis=0)


na_attention.defvjp(_na_attention_fwd, _na_attention_bwd)


def _na_onehots():
    q = np.arange(GRID_W)[:, None]
    col = np.arange(GRID_W)[None, :]
    cs = np.clip(q - NA_WIN_C // 2, 0, GRID_W - NA_WIN_C)
    valid = (col >= cs) & (col < cs + NA_WIN_C)
    cidx = col - q + (NA_WIN_C - 1)
    n_b = 2 * NA_WIN_C - 1
    col_hot = np.zeros((LANE, GRID_W * GRID_W), np.float32)
    for qq in range(GRID_W):
        for cc in range(GRID_W):
            if valid[qq, cc]:
                col_hot[cidx[qq, cc], qq * GRID_W + cc] = 1.0
    row_hot = np.zeros((NA_WIN_R, NA_WIN_R, 2 * NA_WIN_R - 1), np.float32)
    for c in range(NA_WIN_R):
        for j in range(NA_WIN_R):
            row_hot[c, j, j - c + NA_WIN_R - 1] = 1.0
    mask = np.where(valid, 0.0, NEG).astype(np.float32)
    return col_hot, row_hot, mask, n_b


def na_bias_table(rpb):
    h = rpb.shape[0]
    col_hot, row_hot, mask, n_b = _na_onehots()
    t1 = jnp.einsum("cja,hab->hcjb", jnp.asarray(row_hot), rpb)
    t1 = jnp.pad(t1.reshape(h * NA_WIN_R * NA_WIN_R, n_b), ((0, 0), (0, LANE - n_b)))
    t2 = linear(t1, jnp.asarray(col_hot), True)
    t2 = t2.reshape(h, NA_WIN_R, NA_WIN_R, GRID_W, GRID_W) + jnp.asarray(mask)
    tab = jnp.transpose(t2, (0, 1, 3, 2, 4)).reshape(h, NA_WIN_R, GRID_W, NA_BAND)
    return jnp.concatenate([tab, jnp.full((h, 1, GRID_W, NA_BAND), NEG, F32)], axis=1)


def _first_step():
    return jnp.logical_and(pl.program_id(0) == 0, pl.program_id(1) == 0)


def _accum_out(ref, val, first):
    @pl.when(first)
    def _():
        ref[...] = val

    @pl.when(jnp.logical_not(first))
    def _():
        ref[...] += val


def _norm_head(xh, g):
    r = _rms(xh)
    yn = xh * r
    return yn * g, yn, r


def _norm_head_bwd(dy, yn, r, g):
    dg = jnp.sum(dy * yn, axis=0, keepdims=True)
    dyn = dy * g
    return r * (dyn - yn * jnp.mean(dyn * yn, axis=-1, keepdims=True)), dg


def _rope_signs(dh, start, rot_dim, n_heads):
    q = rot_dim // 4
    pos = np.arange(dh)
    quarter = (pos - start) // q
    inr = pos >= start
    sg = np.zeros((8, n_heads * dh), np.float32)
    sg[0] = np.tile(np.where(inr & (quarter % 2 == 0), -1.0, 0.0), n_heads)
    sg[1] = np.tile(np.where(inr & (quarter % 2 == 1), 1.0, 0.0), n_heads)
    return sg


def _rope_full(y, cos, sin, sg, q):
    w = y.shape[-1]
    rot = sg[0:1] * pltpu.roll(y, w - q, 1) + sg[1:2] * pltpu.roll(y, q, 1)
    return y * cos + rot * sin


def _rope_full_t(dy, cos, sin, sg, q):
    w = dy.shape[-1]
    z = dy * sin
    return dy * cos - sg[1:2] * pltpu.roll(z, q, 1) - sg[0:1] * pltpu.roll(z, w - q, 1)


def _hnr_call(x, g, cos, sin, sg, n_heads, q, dy=None):
    b, s, w = x.shape
    dh = w // n_heads
    ts = _pick(s, (256, 128, 64))
    rope = cos is not None

    def body(*refs):
        refs = list(refs)
        x_ref, g_ref = refs[0], refs[1]
        k = 2
        if rope:
            cos_ref, sin_ref, sg_ref = refs[2], refs[3], refs[4]
            k = 5
        gv = g_ref[...]
        if dy is None:
            o_ref = refs[k]
            for h in range(n_heads):
                sl = slice(h * dh, (h + 1) * dh)
                o_ref[:, sl] = _norm_head(x_ref[:, sl], gv)[0]
            if rope:
                o_ref[...] = _rope_full(o_ref[...], cos_ref[...], sin_ref[...], sg_ref[...], q)
            return
        dy_ref, dx_ref, dg_ref = refs[k], refs[k + 1], refs[k + 2]
        src = dy_ref
        if rope:
            dx_ref[...] = _rope_full_t(dy_ref[...], cos_ref[...], sin_ref[...], sg_ref[...], q)
            src = dx_ref
        dg = jnp.zeros((1, dh), F32)
        for h in range(n_heads):
            sl = slice(h * dh, (h + 1) * dh)
            _, yn, r = _norm_head(x_ref[:, sl], gv)
            dxh, dgh = _norm_head_bwd(src[:, sl], yn, r, gv)
            dx_ref[:, sl] = dxh
            dg = dg + dgh
        _accum_out(dg_ref, dg, _first_step())

    row = pl.BlockSpec((None, ts, w), lambda bi, i: (bi, i, 0))
    whole = lambda a: pl.BlockSpec(a.shape, lambda bi, i: (0, 0))
    ins, specs = [x, g], [row, whole(g)]
    if rope:
        ins += [cos, sin, sg]
        specs += [pl.BlockSpec((ts, w), lambda bi, i: (i, 0)), pl.BlockSpec((ts, w), lambda bi, i: (i, 0)), whole(sg)]
    if dy is None:
        out_shape, out_specs = jax.ShapeDtypeStruct(x.shape, F32), row
    else:
        ins.append(dy)
        specs.append(row)
        out_shape = [jax.ShapeDtypeStruct(x.shape, F32), jax.ShapeDtypeStruct(g.shape, F32)]
        out_specs = [row, whole(g)]
    return pl.pallas_call(
        body, out_shape=out_shape, grid=(b, s // ts), in_specs=specs, out_specs=out_specs,
        name=f"hnr_{'b' if dy is not None else 'f'}_{n_heads}x{dh}_{int(rope)}", compiler_params=_params(),
    )(*ins)


@functools.partial(jax.custom_vjp, nondiff_argnums=(5, 6))
def head_norm_rope(x, g, cos, sin, sg, n_heads, q):
    return _hnr_call(x, g, cos, sin, sg, n_heads, q)


def _head_norm_rope_fwd(x, g, cos, sin, sg, n_heads, q):
    return _hnr_call(x, g, cos, sin, sg, n_heads, q), (x, g, cos, sin, sg)


def _head_norm_rope_bwd(n_heads, q, res, dy):
    x, g, cos, sin, sg = res
    dx, dg = _hnr_call(x, g, cos, sin, sg, n_heads, q, dy=dy)
    zero = lambda t: None if t is None else jnp.zeros_like(t)
    return dx, dg, zero(cos), zero(sin), zero(sg)


head_norm_rope.defvjp(_head_norm_rope_fwd, _head_norm_rope_bwd)


def _mla_k_call(kv, kr, g, cos, sin, sg, dkn=None):
    b, s, _ = kv.shape
    ts = _pick(s, (256, 128, 64))
    hw = MLA_NOPE + MLA_V
    kn_w = MLA_HEADS * MLA_QK
    q = MLA_ROPE // 4

    def body(kv_ref, kr_ref, g_ref, cos_ref, sin_ref, sg_ref, *rest):
        gv = g_ref[...]
        krv = kr_ref[...]
        if dkn is None:
            (o_ref,) = rest
            for h in range(MLA_HEADS):
                kh = jnp.concatenate([kv_ref[:, h * hw:h * hw + MLA_NOPE], krv], axis=-1)
                o_ref[:, h * MLA_QK:(h + 1) * MLA_QK] = _norm_head(kh, gv)[0]
            o_ref[...] = _rope_full(o_ref[...], cos_ref[...], sin_ref[...], sg_ref[...], q)
            return
        dkn_ref, dkv_ref, dkr_ref, dg_ref, dy_ref = rest
        dy_ref[...] = _rope_full_t(dkn_ref[...], cos_ref[...], sin_ref[...], sg_ref[...], q)
        dg = jnp.zeros((1, MLA_QK), F32)
        dkr = jnp.zeros((ts, MLA_ROPE), F32)
        for h in range(MLA_HEADS):
            kh = jnp.concatenate([kv_ref[:, h * hw:h * hw + MLA_NOPE], krv], axis=-1)
            _, yn, r = _norm_head(kh, gv)
            dxh, dgh = _norm_head_bwd(dy_ref[:, h * MLA_QK:(h + 1) * MLA_QK], yn, r, gv)
            dkv_ref[:, h * hw:h * hw + MLA_NOPE] = dxh[:, :MLA_NOPE]
            dkv_ref[:, h * hw + MLA_NOPE:(h + 1) * hw] = jnp.zeros((ts, MLA_V), F32)
            dkr = dkr + dxh[:, MLA_NOPE:]
            dg = dg + dgh
        dkr_ref[...] = dkr
        _accum_out(dg_ref, dg, _first_step())

    row = lambda w: pl.BlockSpec((None, ts, w), lambda bi, i: (bi, i, 0))
    tab = pl.BlockSpec((ts, kn_w), lambda bi, i: (i, 0))
    whole = lambda a: pl.BlockSpec(a.shape, lambda bi, i: (0, 0))
    ins = [kv, kr, g, cos, sin, sg]
    specs = [row(kv.shape[2]), row(MLA_ROPE), whole(g), tab, tab, whole(sg)]
    scratch = []
    if dkn is None:
        out_shape, out_specs = jax.ShapeDtypeStruct((b, s, kn_w), F32), row(kn_w)
    else:
        ins.append(dkn)
        specs.append(row(kn_w))
        out_shape = [jax.ShapeDtypeStruct(kv.shape, F32), jax.ShapeDtypeStruct(kr.shape, F32),
                     jax.ShapeDtypeStruct(g.shape, F32)]
        out_specs = [row(kv.shape[2]), row(MLA_ROPE), whole(g)]
        scratch = [pltpu.VMEM((ts, kn_w), F32)]
    return pl.pallas_call(
        body, out_shape=out_shape, grid=(b, s // ts), in_specs=specs, out_specs=out_specs, scratch_shapes=scratch,
        name=f"mla_k_{'b' if dkn is not None else 'f'}", compiler_params=_params(),
    )(*ins)


@jax.custom_vjp
def mla_k_prep(kv, kr, g, cos, sin, sg):
    return _mla_k_call(kv, kr, g, cos, sin, sg)


def _mla_k_prep_fwd(kv, kr, g, cos, sin, sg):
    return _mla_k_call(kv, kr, g, cos, sin, sg), (kv, kr, g, cos, sin, sg)


def _mla_k_prep_bwd(res, dkn):
    kv, kr, g, cos, sin, sg = res
    dkv, dkr, dg = _mla_k_call(kv, kr, g, cos, sin, sg, dkn=dkn)
    return dkv, dkr, dg, jnp.zeros_like(cos), jnp.zeros_like(sin), jnp.zeros_like(sg)


mla_k_prep.defvjp(_mla_k_prep_fwd, _mla_k_prep_bwd)


class _HeadLayout:
    def __init__(self, groups, dq, dv, q_off, k_off, v_off, o_off, wq, wk, wv, wo, scale):
        self.groups, self.dq, self.dv, self.scale = groups, dq, dv, scale
        self.q_off, self.k_off, self.v_off, self.o_off = q_off, k_off, v_off, o_off
        self.wq, self.wk, self.wv, self.wo = wq, wk, wv, wo
        self.n_h = len(q_off)


def _gqa_layout():
    rep = GQA_Q_HEADS // GQA_KV_HEADS
    n_h = GQA_Q_HEADS // 2
    return _HeadLayout(2, HEAD_DIM, HEAD_DIM, [h * HEAD_DIM for h in range(n_h)], [(h // rep) * HEAD_DIM for h in range(n_h)],
                       [(h // rep) * HEAD_DIM for h in range(n_h)], [h * HEAD_DIM for h in range(n_h)],
                       n_h * HEAD_DIM, (n_h // rep) * HEAD_DIM, (n_h // rep) * HEAD_DIM, n_h * HEAD_DIM, HEAD_DIM ** -0.5)


def _mla_layout():
    n_h = MLA_HEADS // 2
    hw = MLA_NOPE + MLA_V
    return _HeadLayout(2, MLA_QK, MLA_V, [h * MLA_QK for h in range(n_h)], [h * MLA_QK for h in range(n_h)],
                       [h * hw + MLA_NOPE for h in range(n_h)], [h * MLA_V for h in range(n_h)],
                       n_h * MLA_QK, n_h * MLA_QK, n_h * hw, n_h * MLA_V, MLA_QK ** -0.5)


def _attn_tm_fwd(q, k, v, lay, n_ctx):
    b, s, _ = q.shape
    tq = min(256, n_ctx)
    nc = n_ctx // tq

    def body(q_ref, k_ref, v_ref, o_ref, lse_ref):
        def run(n_keys):
            for h in range(lay.n_h):
                qo, ko, vo, oo = lay.q_off[h], lay.k_off[h], lay.v_off[h], lay.o_off[h]
                qv = (q_ref[:, qo:qo + lay.dq] * lay.scale).astype(BF16)
                sc = lax.dot_general(qv, k_ref[0:n_keys, ko:ko + lay.dq].astype(BF16), _NT, preferred_element_type=F32)
                m = jnp.max(sc, axis=-1, keepdims=True)
                p = jnp.exp(sc - m)
                l = jnp.sum(p, axis=-1, keepdims=True)
                o = jnp.dot(p.astype(BF16), v_ref[0:n_keys, vo:vo + lay.dv].astype(BF16), preferred_element_type=F32)
                o_ref[:, oo:oo + lay.dv] = o / l
                lse_ref[:, h:h + 1] = m + jnp.log(l)

        pl.when(pl.program_id(2) < nc)(lambda: run(n_ctx))
        pl.when(pl.program_id(2) >= nc)(lambda: run(s))

    return pl.pallas_call(
        body, out_shape=[jax.ShapeDtypeStruct((b, s, lay.groups * lay.wo), F32),
                         jax.ShapeDtypeStruct((b, lay.groups, s, lay.n_h), F32)],
        grid=(b, lay.groups, s // tq),
        in_specs=[pl.BlockSpec((None, tq, lay.wq), lambda bi, g, i: (bi, i, g)),
                  pl.BlockSpec((None, s, lay.wk), lambda bi, g, i: (bi, 0, g)),
                  pl.BlockSpec((None, s, lay.wv), lambda bi, g, i: (bi, 0, g))],
        out_specs=[pl.BlockSpec((None, tq, lay.wo), lambda bi, g, i: (bi, i, g)),
                   pl.BlockSpec((None, None, tq, lay.n_h), lambda bi, g, i: (bi, g, i, 0))],
        name=f"attn_tm_f_{lay.dq}", compiler_params=_params(),
    )(q, k, v)


def _attn_tm_delta(o, do, lay):
    b, s, _ = o.shape
    ts = _pick(s, (256, 128, 64))

    def body(o_ref, do_ref, d_ref):
        for h in range(lay.n_h):
            oo = lay.o_off[h]
            d_ref[:, h:h + 1] = jnp.sum(o_ref[:, oo:oo + lay.dv] * do_ref[:, oo:oo + lay.dv], axis=-1, keepdims=True)

    blk = pl.BlockSpec((None, ts, lay.wo), lambda bi, g, i: (bi, i, g))
    return pl.pallas_call(
        body, out_shape=jax.ShapeDtypeStruct((b, lay.groups, s, lay.n_h), F32), grid=(b, lay.groups, s // ts),
        in_specs=[blk, blk], out_specs=pl.BlockSpec((None, None, ts, lay.n_h), lambda bi, g, i: (bi, g, i, 0)),
        name=f"attn_tm_delta_{lay.dq}", compiler_params=_params(),
    )(o, do)


def _attn_tm_bwd(q, k, v, lse, delta, do, lay, n_ctx):
    b, s, _ = q.shape
    tk = min(256, n_ctx)
    nc = n_ctx // tk

    def body(q_ref, k_ref, v_ref, lse_ref, delta_ref, do_ref, dq_ref, dk_ref, dv_ref):
        @pl.when(pl.program_id(2) == 0)
        def _():
            dq_ref[...] = jnp.zeros_like(dq_ref)

        def run(r0):
            dk_acc, dv_acc = {}, {}
            for h in range(lay.n_h):
                qo, ko, vo, oo = lay.q_off[h], lay.k_off[h], lay.v_off[h], lay.o_off[h]
                kh = k_ref[:, ko:ko + lay.dq].astype(BF16)
                vh = v_ref[:, vo:vo + lay.dv].astype(BF16)
                qv = (q_ref[r0:s, qo:qo + lay.dq] * lay.scale).astype(BF16)
                dob = do_ref[r0:s, oo:oo + lay.dv].astype(BF16)
                sc = lax.dot_general(qv, kh, _NT, preferred_element_type=F32)
                p = jnp.exp(sc - lse_ref[r0:s, h:h + 1])
                dvh = lax.dot_general(p.astype(BF16), dob, _TN, preferred_element_type=F32)
                dp = lax.dot_general(dob, vh, _NT, preferred_element_type=F32)
                dsb = (p * (dp - delta_ref[r0:s, h:h + 1])).astype(BF16)
                dkh = lax.dot_general(dsb, qv, _TN, preferred_element_type=F32)
                dq_ref[r0:s, qo:qo + lay.dq] += jnp.dot(dsb, kh, preferred_element_type=F32) * lay.scale
                dk_acc[ko] = dkh if ko not in dk_acc else dk_acc[ko] + dkh
                dv_acc[vo] = dvh if vo not in dv_acc else dv_acc[vo] + dvh
            if len(dv_acc) * lay.dv != lay.wv:
                dv_ref[...] = jnp.zeros_like(dv_ref)
            for ko, val in dk_acc.items():
                dk_ref[:, ko:ko + lay.dq] = val
            for vo, val in dv_acc.items():
                dv_ref[:, vo:vo + lay.dv] = val

        pl.when(pl.program_id(2) < nc)(lambda: run(0))
        pl.when(pl.program_id(2) >= nc)(lambda: run(n_ctx))

    full = lambda w: pl.BlockSpec((None, s, w), lambda bi, g, j: (bi, 0, g))
    blk = lambda w: pl.BlockSpec((None, tk, w), lambda bi, g, j: (bi, j, g))
    stat = pl.BlockSpec((None, None, s, lay.n_h), lambda bi, g, j: (bi, g, 0, 0))
    return pl.pallas_call(
        body, out_shape=[jax.ShapeDtypeStruct(q.shape, F32), jax.ShapeDtypeStruct(k.shape, F32),
                         jax.ShapeDtypeStruct(v.shape, F32)],
        grid=(b, lay.groups, s // tk),
        in_specs=[full(lay.wq), blk(lay.wk), blk(lay.wv), stat, stat, full(lay.wo)],
        out_specs=[full(lay.wq), blk(lay.wk), blk(lay.wv)],
        name=f"attn_tm_b_{lay.dq}", compiler_params=_params(),
    )(q, k, v, lse, delta, do)


def _make_attention_tm(lay):
    @functools.partial(jax.custom_vjp, nondiff_argnums=(3,))
    def op(q, k, v, n_ctx):
        return _attn_tm_fwd(q, k, v, lay, n_ctx)[0]

    def fwd(q, k, v, n_ctx):
        o, lse = _attn_tm_fwd(q, k, v, lay, n_ctx)
        return o, (q, k, v, o, lse)

    def bwd(n_ctx, res, do):
        q, k, v, o, lse = res
        return _attn_tm_bwd(q, k, v, lse, _attn_tm_delta(o, do, lay), do, lay, n_ctx)

    op.defvjp(fwd, bwd)
    return op


gqa_attention = _make_attention_tm(_gqa_layout())
mla_attention = _make_attention_tm(_mla_layout())

NA_GROUPS = 2


def _na_tm_specs(s, nc, rows):
    hg = NA_HEADS // NA_GROUPS
    w = hg * HEAD_DIM
    qs = pl.BlockSpec((None, GRID_W, w), lambda bi, g, i: (bi, i, g))
    ks = pl.BlockSpec((None, s, w), lambda bi, g, i: (bi, 0, g))
    bs = pl.BlockSpec((hg, None, GRID_W, NA_BAND), lambda bi, g, i: (g, _na_geometry(i, nc, rows)[1], 0, 0))
    ls = pl.BlockSpec((None, None, GRID_W, hg), lambda bi, g, i: (bi, g, i, 0))
    return hg, w, qs, ks, bs, ls


def _na_tm_scores(q_ref, k_ref, bias_ref, hd, n_ctx, start, scale):
    sl = slice(hd * HEAD_DIM, (hd + 1) * HEAD_DIM)
    qv = (q_ref[:, sl] * scale).astype(BF16)
    kc = k_ref[0:n_ctx, sl].astype(BF16)
    kb = k_ref[pl.ds(start, NA_BAND), sl].astype(BF16)
    s_c = lax.dot_general(qv, kc, _NT, preferred_element_type=F32)
    s_l = lax.dot_general(qv, kb, _NT, preferred_element_type=F32) + bias_ref[hd]
    return sl, qv, kc, kb, s_c, s_l


def _na_tm_fwd(q, k, v, bias, n_ctx):
    b, s, _ = q.shape
    nc = n_ctx // GRID_W
    rows = (s - n_ctx) // GRID_W
    scale = HEAD_DIM ** -0.5
    hg, w, qs, ks, bs, ls = _na_tm_specs(s, nc, rows)

    def body(q_ref, k_ref, v_ref, bias_ref, o_ref, lse_ref):
        rs, _ = _na_geometry(pl.program_id(2), nc, rows)
        start = pl.multiple_of(n_ctx + rs * GRID_W, GRID_W)
        for hd in range(hg):
            sl, _, _, _, s_c, s_l = _na_tm_scores(q_ref, k_ref, bias_ref, hd, n_ctx, start, scale)
            m = jnp.maximum(jnp.max(s_c, axis=-1, keepdims=True), jnp.max(s_l, axis=-1, keepdims=True))
            p_c = jnp.exp(s_c - m)
            p_l = jnp.exp(s_l - m)
            l = jnp.sum(p_c, axis=-1, keepdims=True) + jnp.sum(p_l, axis=-1, keepdims=True)
            o = jnp.dot(p_c.astype(BF16), v_ref[0:n_ctx, sl].astype(BF16), preferred_element_type=F32)
            o = o + jnp.dot(p_l.astype(BF16), v_ref[pl.ds(start, NA_BAND), sl].astype(BF16), preferred_element_type=F32)
            o_ref[:, sl] = o / l
            lse_ref[:, hd:hd + 1] = m + jnp.log(l)

    return pl.pallas_call(
        body, out_shape=[jax.ShapeDtypeStruct(q.shape, F32), jax.ShapeDtypeStruct((b, NA_GROUPS, s, hg), F32)],
        grid=(b, NA_GROUPS, s // GRID_W), in_specs=[qs, ks, ks, bs], out_specs=[qs, ls],
        name=f"na_tm_f_{s}", compiler_params=_params(),
    )(q, k, v, bias)


def _na_tm_bwd(q, k, v, bias, o, lse, do, n_ctx):
    b, s, _ = q.shape
    nc = n_ctx // GRID_W
    rows = (s - n_ctx) // GRID_W
    scale = HEAD_DIM ** -0.5
    n_cls = NA_WIN_R + 1
    hg, w, qs, ks, bs, ls = _na_tm_specs(s, nc, rows)

    def body(q_ref, k_ref, v_ref, bias_ref, o_ref, lse_ref, do_ref, dq_ref, dk_ref, dv_ref, db_ref):
        i = pl.program_id(2)
        rs, cls = _na_geometry(i, nc, rows)
        _, cls_prev = _na_geometry(i - 1, nc, rows)
        start = pl.multiple_of(n_ctx + rs * GRID_W, GRID_W)
        first = jnp.logical_or(i == 0, cls != cls_prev)

        @pl.when(i == 0)
        def _():
            dk_ref[...] = jnp.zeros_like(dk_ref)
            dv_ref[...] = jnp.zeros_like(dv_ref)

        @pl.when(first)
        def _():
            db_ref[...] = jnp.zeros_like(db_ref)

        for hd in range(hg):
            sl, qv, kc, kb, s_c, s_l = _na_tm_scores(q_ref, k_ref, bias_ref, hd, n_ctx, start, scale)
            lse_v = lse_ref[:, hd:hd + 1]
            p_c = jnp.exp(s_c - lse_v)
            p_l = jnp.exp(s_l - lse_v)
            dov = do_ref[:, sl]
            dob = dov.astype(BF16)
            delta = jnp.sum(dov * o_ref[:, sl], axis=-1, keepdims=True)
            vc = v_ref[0:n_ctx, sl].astype(BF16)
            vb = v_ref[pl.ds(start, NA_BAND), sl].astype(BF16)
            ds_c = p_c * (lax.dot_general(dob, vc, _NT, preferred_element_type=F32) - delta)
            ds_l = p_l * (lax.dot_general(dob, vb, _NT, preferred_element_type=F32) - delta)
            dsc_b = ds_c.astype(BF16)
            dsl_b = ds_l.astype(BF16)
            dq_ref[:, sl] = (jnp.dot(dsc_b, kc, preferred_element_type=F32)
                             + jnp.dot(dsl_b, kb, preferred_element_type=F32)) * scale
            dk_ref[0:n_ctx, sl] += lax.dot_general(dsc_b, qv, _TN, preferred_element_type=F32)
            dk_ref[pl.ds(start, NA_BAND), sl] += lax.dot_general(dsl_b, qv, _TN, preferred_element_type=F32)
            dv_ref[0:n_ctx, sl] += lax.dot_general(p_c.astype(BF16), dob, _TN, preferred_element_type=F32)
            dv_ref[pl.ds(start, NA_BAND), sl] += lax.dot_general(p_l.astype(BF16), dob, _TN, preferred_element_type=F32)
            db_ref[hd] += ds_l

    dbs = pl.BlockSpec((None, hg, None, GRID_W, NA_BAND), lambda bi, g, i: (bi, g, _na_geometry(i, nc, rows)[1], 0, 0))
    return pl.pallas_call(
        body,
        out_shape=[jax.ShapeDtypeStruct(q.shape, F32), jax.ShapeDtypeStruct(q.shape, F32), jax.ShapeDtypeStruct(q.shape, F32),
                   jax.ShapeDtypeStruct((b, NA_HEADS, n_cls, GRID_W, NA_BAND), F32)],
        grid=(b, NA_GROUPS, s // GRID_W), in_specs=[qs, ks, ks, bs, qs, ls, qs], out_specs=[qs, ks, ks, dbs],
        name=f"na_tm_b_{s}", compiler_params=_params(),
    )(q, k, v, bias, o, lse, do)


@functools.partial(jax.custom_vjp, nondiff_argnums=(4,))
def na_attention_tm(q, k, v, bias, n_ctx):
    return _na_tm_fwd(q, k, v, bias, n_ctx)[0]


def _na_attention_tm_fwd(q, k, v, bias, n_ctx):
    o, lse = _na_tm_fwd(q, k, v, bias, n_ctx)
    return o, (q, k, v, bias, o, lse)


def _na_attention_tm_bwd(n_ctx, res, do):
    q, k, v, bias, o, lse = res
    dq, dk, dv, db = _na_tm_bwd(q, k, v, bias, o, lse, do, n_ctx)
    return dq, dk, dv, _sum_rows(db.reshape(db.shape[0], -1, NA_BAND), db.shape[0]).reshape(db.shape[1:])


na_attention_tm.defvjp(_na_attention_tm_fwd, _na_attention_tm_bwd)


def _cmul(ar, ai, br, bi):
    return ar * br - ai * bi, ar * bi + ai * br


def _s5_chunk(n_ctx):
    return min(256, n_ctx)


def _s5_tables(a_re, a_im, t_len, rev):
    a_re, a_im = lax.stop_gradient(a_re), lax.stop_gradient(a_im)
    mag = jnp.sqrt(a_re * a_re + a_im * a_im)
    th = jnp.arctan2(a_im, a_re)
    t = jnp.arange(t_len + 1, dtype=F32)[:, None]
    pm = jnp.where(t == 0, 1.0, jnp.exp(t * jnp.log(jnp.maximum(mag, 1e-37))) * (mag > 0))
    pw = jnp.stack([pm * jnp.cos(t * th), pm * jnp.sin(t * th)])
    steps = jnp.concatenate([pw[:, min(2 ** i, t_len)][:, None] for i in range(8)], axis=1)
    tile = pw[:, 1:9]
    a8k = pw[:, 0:t_len:8]
    if rev:
        tile, a8k = tile[:, ::-1], a8k[:, ::-1]
    misc = jnp.concatenate([pw[:, t_len:t_len + 1], jnp.zeros((2, 7, pw.shape[-1]), F32)], axis=1)
    return jnp.concatenate([steps, tile, misc, a8k], axis=1)


def _scan_chunk(x_re, x_im, tab_ref, hin_re, hin_im, rev, t_len, xs_ref, es_ref):
    outs = [_scan_slab(x_re[:, k:k + LANE], x_im[:, k:k + LANE], tab_ref, hin_re[:, k:k + LANE], hin_im[:, k:k + LANE],
                       rev, t_len, xs_ref, es_ref, k) for k in range(0, x_re.shape[-1], LANE)]
    return tuple(jnp.concatenate([o[t] for o in outs], axis=-1) for t in range(4))


def _scan_slab(x_re, x_im, tab_ref, hin_re, hin_im, rev, t_len, xs_ref, es_ref, k0):
    lanes = LANE
    n2 = t_len // 8
    tab_ref = tab_ref.at[:, :, k0:k0 + LANE]
    rin = lax.broadcasted_iota(jnp.int32, (t_len, lanes), 0) & 7
    for li, sh in enumerate((1, 2, 4)):
        m_re, m_im = tab_ref[0, li:li + 1, :], tab_ref[1, li:li + 1, :]
        amt = sh if not rev else t_len - sh
        c_re, c_im = _cmul(m_re, m_im, pltpu.roll(x_re, amt, 0), pltpu.roll(x_im, amt, 0))
        ok = (rin >= sh) if not rev else (rin < 8 - sh)
        x_re = x_re + jnp.where(ok, c_re, 0.0)
        x_im = x_im + jnp.where(ok, c_im, 0.0)
    xr_ref, xi_ref = xs_ref
    xr_ref[...] = x_re
    xi_ref[...] = x_im
    off = 0 if rev else 7
    e_re = xr_ref[pl.ds(off, n2, stride=8), :]
    e_im = xi_ref[pl.ds(off, n2, stride=8), :]
    row2 = lax.broadcasted_iota(jnp.int32, (n2, lanes), 0)
    sh, li = 1, 3
    while sh < n2:
        m_re, m_im = tab_ref[0, li:li + 1, :], tab_ref[1, li:li + 1, :]
        amt = sh if not rev else n2 - sh
        c_re, c_im = _cmul(m_re, m_im, pltpu.roll(e_re, amt, 0), pltpu.roll(e_im, amt, 0))
        ok = (row2 >= sh) if not rev else (row2 < n2 - sh)
        e_re = e_re + jnp.where(ok, c_re, 0.0)
        e_im = e_im + jnp.where(ok, c_im, 0.0)
        sh, li = sh * 2, li + 1
    es_ref[0] = e_re
    es_ref[1] = e_im
    last = 0 if rev else n2 - 1
    t_re, t_im = _cmul(tab_ref[0, 16:17, :], tab_ref[1, 16:17, :], hin_re, hin_im)
    hout_re = es_ref[0, last:last + 1, :] + t_re
    hout_im = es_ref[1, last:last + 1, :] + t_im
    amt = 1 if not rev else n2 - 1
    ok = (row2 >= 1) if not rev else (row2 < n2 - 1)
    k_re, k_im = _cmul(tab_ref[0, 24:24 + n2, :], tab_ref[1, 24:24 + n2, :], hin_re, hin_im)
    c_re = jnp.where(ok, pltpu.roll(e_re, amt, 0), 0.0) + k_re
    c_im = jnp.where(ok, pltpu.roll(e_im, amt, 0), 0.0) + k_im
    tp_re, tp_im = tab_ref[0, 8:16, :][None], tab_ref[1, 8:16, :][None]
    add_re, add_im = _cmul(tp_re, tp_im, c_re[:, None, :], c_im[:, None, :])
    h_re = xr_ref[...] + add_re.reshape(t_len, lanes)
    h_im = xi_ref[...] + add_im.reshape(t_len, lanes)
    return h_re, h_im, hout_re, hout_im


def _s5_order(j, n_chunks, nc, rev):
    if not rev:
        return j
    return jnp.where(j < nc, nc - 1 - j, n_chunks - 1 - (j - nc))


def _s5_fwd(u, tab, b_bd, c_bd, n_ctx, rev):
    b, s, w = u.shape
    lanes = b_bd.shape[-1]
    t_len = _s5_chunk(n_ctx)
    n_chunks, nc = s // t_len, n_ctx // t_len

    def body(u_ref, tab_ref, b_ref, c_ref, y_ref, h_ref, hin_ref, carry_ref, xr_ref, xi_ref, es_ref):
        xs_ref = (xr_ref, xi_ref)

        @pl.when(pl.program_id(1) == 0)
        def _():
            carry_ref[...] = jnp.zeros_like(carry_ref)

        ub = u_ref[...].astype(BF16)
        x_re = jnp.dot(ub, b_ref[0].astype(BF16), preferred_element_type=F32)
        x_im = jnp.dot(ub, b_ref[1].astype(BF16), preferred_element_type=F32)
        hin_re, hin_im = carry_ref[0, 0:1, :], carry_ref[1, 0:1, :]
        hin_ref[...] = carry_ref[...]
        h_re, h_im, ho_re, ho_im = _scan_chunk(x_re, x_im, tab_ref, hin_re, hin_im, rev, t_len, xs_ref, es_ref)
        carry_ref[0] = jnp.broadcast_to(ho_re, (8, lanes))
        carry_ref[1] = jnp.broadcast_to(ho_im, (8, lanes))
        h_ref[0] = h_re
        h_ref[1] = h_im
        y_ref[...] = (jnp.dot(h_re.astype(BF16), c_ref[0].astype(BF16), preferred_element_type=F32)
                      - jnp.dot(h_im.astype(BF16), c_ref[1].astype(BF16), preferred_element_type=F32))

    order = lambda j: _s5_order(j, n_chunks, nc, rev)
    whole = lambda arr: pl.BlockSpec(arr.shape, lambda bi, j: (0,) * arr.ndim)
    return pl.pallas_call(
        body,
        out_shape=[jax.ShapeDtypeStruct((b, s, w), F32), jax.ShapeDtypeStruct((2, b, s, lanes), F32),
                   jax.ShapeDtypeStruct((2, b, n_chunks, 8, lanes), F32)],
        grid=(b, n_chunks),
        in_specs=[pl.BlockSpec((None, t_len, w), lambda bi, j: (bi, order(j), 0)), whole(tab), whole(b_bd), whole(c_bd)],
        out_specs=[pl.BlockSpec((None, t_len, w), lambda bi, j: (bi, order(j), 0)),
                   pl.BlockSpec((2, None, t_len, lanes), lambda bi, j: (0, bi, order(j), 0)),
                   pl.BlockSpec((2, None, None, 8, lanes), lambda bi, j: (0, bi, order(j), 0, 0))],
        scratch_shapes=[pltpu.VMEM((2, 8, lanes), F32), pltpu.VMEM((t_len, LANE), F32), pltpu.VMEM((t_len, LANE), F32),
                        pltpu.VMEM((2, t_len // 8, LANE), F32)],
        name=f"s5_f_{s}_{int(rev)}", compiler_params=_params(),
    )(u, tab, b_bd, c_bd)


def _s5_bwd(u, tab_adj, b_bd, c_bd, h, hin, dy, n_ctx, rev):
    b, s, w = u.shape
    lanes = b_bd.shape[-1]
    t_len = _s5_chunk(n_ctx)
    n_chunks, nc = s // t_len, n_ctx // t_len
    arev = not rev

    def body(u_ref, tab_ref, b_ref, c_ref, h_ref, hin_ref, dy_ref, du_ref, db_ref, dc_ref, da_ref,
             carry_ref, xr_ref, xi_ref, es_ref):
        xs_ref = (xr_ref, xi_ref)
        first = jnp.logical_and(pl.program_id(0) == 0, pl.program_id(1) == 0)

        @pl.when(pl.program_id(1) == 0)
        def _():
            carry_ref[...] = jnp.zeros_like(carry_ref)

        dyv = dy_ref[...]
        dyb = dyv.astype(BF16)
        dn = (((1,), (1,)), ((), ()))
        dt = (((0,), (0,)), ((), ()))
        x_re = lax.dot_general(dyb, c_ref[0].astype(BF16), dn, preferred_element_type=F32)
        x_im = -lax.dot_general(dyb, c_ref[1].astype(BF16), dn, preferred_element_type=F32)
        g_re, g_im, go_re, go_im = _scan_chunk(x_re, x_im, tab_ref, carry_ref[0, 0:1, :], carry_ref[1, 0:1, :],
                                               arev, t_len, xs_ref, es_ref)
        carry_ref[0] = jnp.broadcast_to(go_re, (8, lanes))
        carry_ref[1] = jnp.broadcast_to(go_im, (8, lanes))
        h_re, h_im = h_ref[0], h_ref[1]
        gb_re, gb_im = g_re.astype(BF16), g_im.astype(BF16)
        du_ref[...] = (lax.dot_general(gb_re, b_ref[0].astype(BF16), dn, preferred_element_type=F32)
                       + lax.dot_general(gb_im, b_ref[1].astype(BF16), dn, preferred_element_type=F32))
        ub = u_ref[...].astype(BF16)
        db_re = lax.dot_general(ub, gb_re, dt, preferred_element_type=F32)
        db_im = lax.dot_general(ub, gb_im, dt, preferred_element_type=F32)
        dc_re = lax.dot_general(h_re.astype(BF16), dyb, dt, preferred_element_type=F32)
        dc_im = -lax.dot_general(h_im.astype(BF16), dyb, dt, preferred_element_type=F32)
        row = lax.broadcasted_iota(jnp.int32, (t_len, lanes), 0)
        amt = 1 if not rev else t_len - 1
        edge = (row == 0) if not rev else (row == t_len - 1)
        hp_re = jnp.where(edge, hin_ref[0, 0:1, :], pltpu.roll(h_re, amt, 0))
        hp_im = jnp.where(edge, hin_ref[1, 0:1, :], pltpu.roll(h_im, amt, 0))
        da_re = jnp.sum(g_re * hp_re + g_im * hp_im, axis=0, keepdims=True)
        da_im = jnp.sum(g_im * hp_re - g_re * hp_im, axis=0, keepdims=True)

        @pl.when(first)
        def _():
            db_ref[0], db_ref[1] = db_re, db_im
            dc_ref[0], dc_ref[1] = dc_re, dc_im
            da_ref[0] = jnp.broadcast_to(da_re, (8, lanes))
            da_ref[1] = jnp.broadcast_to(da_im, (8, lanes))

        @pl.when(jnp.logical_not(first))
        def _():
            db_ref[0] += db_re
            db_ref[1] += db_im
            dc_ref[0] += dc_re
            dc_ref[1] += dc_im
            da_ref[0] += jnp.broadcast_to(da_re, (8, lanes))
            da_ref[1] += jnp.broadcast_to(da_im, (8, lanes))

    order = lambda j: _s5_order(n_chunks - 1 - j, n_chunks, nc, rev)
    whole = lambda arr: pl.BlockSpec(arr.shape, lambda bi, j: (0,) * arr.ndim)
    us = pl.BlockSpec((None, t_len, w), lambda bi, j: (bi, order(j), 0))
    return pl.pallas_call(
        body,
        out_shape=[jax.ShapeDtypeStruct((b, s, w), F32), jax.ShapeDtypeStruct(b_bd.shape, F32),
                   jax.ShapeDtypeStruct(c_bd.shape, F32), jax.ShapeDtypeStruct((2, 8, lanes), F32)],
        grid=(b, n_chunks),
        in_specs=[us, whole(tab_adj), whole(b_bd), whole(c_bd),
                  pl.BlockSpec((2, None, t_len, lanes), lambda bi, j: (0, bi, order(j), 0)),
                  pl.BlockSpec((2, None, None, 8, lanes), lambda bi, j: (0, bi, order(j), 0, 0)), us],
        out_specs=[us, whole(b_bd), whole(c_bd), pl.BlockSpec((2, 8, lanes), lambda bi, j: (0, 0, 0))],
        scratch_shapes=[pltpu.VMEM((2, 8, lanes), F32), pltpu.VMEM((t_len, LANE), F32), pltpu.VMEM((t_len, LANE), F32),
                        pltpu.VMEM((2, t_len // 8, LANE), F32)],
        name=f"s5_b_{s}_{int(rev)}", compiler_params=_params(),
    )(u, tab_adj, b_bd, c_bd, h, hin, dy)


@functools.partial(jax.custom_vjp, nondiff_argnums=(4, 5))
def s5_direction(u, a, b_bd, c_bd, n_ctx, rev):
    tab = _s5_tables(a[0], a[1], _s5_chunk(n_ctx), rev)
    return _s5_fwd(u, tab, b_bd, c_bd, n_ctx, rev)[0]


def _s5_direction_fwd(u, a, b_bd, c_bd, n_ctx, rev):
    tab = _s5_tables(a[0], a[1], _s5_chunk(n_ctx), rev)
    y, h, hin = _s5_fwd(u, tab, b_bd, c_bd, n_ctx, rev)
    return y, (u, a, b_bd, c_bd, h, hin)


def _s5_direction_bwd(n_ctx, rev, res, dy):
    u, a, b_bd, c_bd, h, hin = res
    tab_adj = _s5_tables(a[0], -a[1], _s5_chunk(n_ctx), not rev)
    du, db, dc, da = _s5_bwd(u, tab_adj, b_bd, c_bd, h, hin, dy, n_ctx, rev)
    return du, da[:, 0, :], db, dc


s5_direction.defvjp(_s5_direction_fwd, _s5_direction_bwd)


def _s5_discretize(lam_re, lam_im, log_dt, b_re, b_im):
    dt = jnp.exp(log_dt)[:, None]
    mag = jnp.exp(lam_re * dt)
    a_re = mag * jnp.cos(lam_im * dt)
    a_im = mag * jnp.sin(lam_im * dt)
    den = jnp.square(lam_re) + jnp.square(lam_im)
    f_re = ((a_re - 1.0) * lam_re + a_im * lam_im) / den
    f_im = (a_im * lam_re - (a_re - 1.0) * lam_im) / den
    bb_re = f_re[..., None] * b_re - f_im[..., None] * b_im
    bb_im = f_re[..., None] * b_im + f_im[..., None] * b_re
    return a_re, a_im, bb_re, bb_im


def _block_diag(t):
    g, r, c = t.shape
    return (jnp.eye(g, dtype=F32)[:, None, :, None] * t[:, :, None, :]).reshape(g * r, g * c)


def _loss_head(y, target):
    b, n, d = y.shape
    ts = _pick(n, (256, 128, 64))

    def body(y_ref, t_ref, loss_ref, dy_ref):
        first = jnp.logical_and(pl.program_id(0) == 0, pl.program_id(1) == 0)
        err = y_ref[...] - t_ref[...]
        dy_ref[...] = err * (1.0 / d)
        part = 0.5 * jnp.sum(jnp.sum(err * err, axis=-1, keepdims=True) * (1.0 / d), axis=0, keepdims=True)
        part = jnp.broadcast_to(part, (8, LANE))

        @pl.when(first)
        def _():
            loss_ref[...] = part

        @pl.when(jnp.logical_not(first))
        def _():
            loss_ref[...] += part

    blk = pl.BlockSpec((None, ts, d), lambda bi, i: (bi, i, 0))
    return pl.pallas_call(
        body, out_shape=[jax.ShapeDtypeStruct((8, LANE), F32), jax.ShapeDtypeStruct((b, n, d), F32)],
        grid=(b, n // ts), in_specs=[blk, blk], out_specs=[pl.BlockSpec((8, LANE), lambda bi, i: (0, 0)), blk],
        name="loss_head", compiler_params=_params(),
    )(y, target)


def _adamw(w, g, m, v):
    shape = w.shape
    n = int(np.prod(shape))
    cols = shape[-1]
    r = n // cols
    tr = _pick(r, (512, 256, 128, 64, 32, 16, 8))
    c1 = 1.0 / (1.0 - ADAM_B1 ** ADAM_STEP)
    c2 = 1.0 / (1.0 - ADAM_B2 ** ADAM_STEP)

    def body(w_ref, g_ref, m_ref, v_ref, d_ref, mo_ref, vo_ref):
        gv = g_ref[...]
        m2 = ADAM_B1 * m_ref[...] + (1.0 - ADAM_B1) * gv
        v2 = ADAM_B2 * v_ref[...] + (1.0 - ADAM_B2) * (gv * gv)
        d_ref[...] = -ADAM_LR * ((m2 * c1) / (jnp.sqrt(v2 * c2) + ADAM_EPS) + ADAM_WD * w_ref[...])
        mo_ref[...] = m2
        vo_ref[...] = v2

    blk = pl.BlockSpec((tr, cols), lambda i: (i, 0))
    outs = pl.pallas_call(
        body, out_shape=[jax.ShapeDtypeStruct((r, cols), F32)] * 3, grid=(r // tr,),
        in_specs=[blk] * 4, out_specs=[blk] * 3, name=f"adamw_{r}x{cols}", compiler_params=_params(),
    )(*[t.reshape(r, cols) for t in (w, g, m, v)])
    return tuple(o.reshape(shape) for o in outs)


def _sum_rows(x, n):
    _, r, c = x.shape
    tr = _pick(r, (512, 256, 128, 64, 32, 16, 8))

    def body(x_ref, o_ref):
        acc = x_ref[0]
        for j in range(1, n):
            acc = acc + x_ref[j]
        o_ref[...] = acc

    return pl.pallas_call(
        body, out_shape=jax.ShapeDtypeStruct((r, c), F32), grid=(r // tr,),
        in_specs=[pl.BlockSpec((n, tr, c), lambda i: (0, i, 0))], out_specs=pl.BlockSpec((tr, c), lambda i: (i, 0)),
        name=f"sum{n}_{r}x{c}", compiler_params=_params(),
    )(x)


def _accumulate(parts, out_dtype):
    r, c = parts[0].shape[-2:]
    tr = _pick(r, (512, 256, 128, 64, 32, 16))

    def body(*refs):
        acc = None
        for ref in refs[:-1]:
            terms = [ref[j] for j in range(ref.shape[0])] if len(ref.shape) == 3 else [ref[...]]
            for t in terms:
                acc = t.astype(F32) if acc is None else acc + t.astype(F32)
        refs[-1][...] = acc.astype(out_dtype)

    specs = [pl.BlockSpec((p.shape[0], tr, c), lambda i: (0, i, 0)) if p.ndim == 3 else pl.BlockSpec((tr, c), lambda i: (i, 0))
             for p in parts]
    tag = "_".join(str(p.shape[0]) if p.ndim == 3 else "1" for p in parts)
    return pl.pallas_call(
        body, out_shape=jax.ShapeDtypeStruct((r, c), out_dtype), grid=(r // tr,), in_specs=specs,
        out_specs=pl.BlockSpec((tr, c), lambda i: (i, 0)), name=f"accumulate_{tag}_{r}x{c}_{jnp.dtype(out_dtype).name}",
        compiler_params=_params(),
    )(*parts)


def _add2(x, y):
    shape = x.shape
    c = shape[-1]
    r = int(np.prod(shape)) // c
    tr = _pick(r, (512, 256, 128, 64, 32, 16, 8))

    def body(x_ref, y_ref, o_ref):
        o_ref[...] = x_ref[...] + y_ref[...]

    blk = pl.BlockSpec((tr, c), lambda i: (i, 0))
    return pl.pallas_call(
        body, out_shape=jax.ShapeDtypeStruct((r, c), F32), grid=(r // tr,), in_specs=[blk, blk], out_specs=blk,
        name=f"add2_{r}x{c}", compiler_params=_params(),
    )(x.reshape(r, c), y.reshape(r, c)).reshape(shape)


_FLIPS = ((1, 0), (0, 1), (1, 1))


def _me():
    return lax.axis_index("x"), lax.axis_index("y"), lax.axis_index("c")


def allgather8(v):
    m_per, n = v.shape

    def body(x_ref, out_ref, send_sems, recv_sems, local_sem):
        x, y, c = _me()
        me, sibling = (x, y, c), (x, y, 1 - c)
        chips = [(1 - x, y), (x, 1 - y), (1 - x, 1 - y)]

        def rows(px, py, pc):
            return out_ref.at[pl.ds((4 * px + 2 * py + pc) * m_per, m_per), :]

        def copy(k, block, to, src=None):
            return pltpu.make_async_remote_copy(
                src_ref=rows(*block) if src is None else src, dst_ref=rows(*block),
                send_sem=send_sems.at[k], recv_sem=recv_sems.at[k], device_id=to, device_id_type=MESH)

        mine = pltpu.make_async_copy(x_ref, rows(*me), local_sem)
        mine.start()
        first = [copy(0, me, sibling, src=x_ref)]
        first += [copy(1 + j, me, (*chip, c), src=x_ref) for j, chip in enumerate(chips)]
        for cp in first:
            cp.start()
        passed = [copy(4 + j, (*chip, c), sibling) for j, chip in enumerate(chips)]
        for j, chip in enumerate(chips):
            copy(1 + j, (*chip, c), me).wait_recv()
            passed[j].start()
        copy(0, sibling, me).wait_recv()
        for j, chip in enumerate(chips):
            copy(4 + j, (*chip, 1 - c), me).wait_recv()
        for cp in first + passed:
            cp.wait_send()
        mine.wait()

    return pl.pallas_call(
        body, out_shape=jax.ShapeDtypeStruct((N_DEV * m_per, n), v.dtype), in_specs=[VMEM_SPEC], out_specs=VMEM_SPEC,
        scratch_shapes=[pltpu.SemaphoreType.DMA((7,)), pltpu.SemaphoreType.DMA((7,)), pltpu.SemaphoreType.DMA],
        name=f"allgather8_{m_per}x{n}", compiler_params=_params(),
    )(v)


def _row_chunks(rows, tile_rows, want):
    n = want
    while n > 1 and rows % (n * tile_rows):
        n //= 2
    return [(i * (rows // n), rows // n) for i in range(n)]


def _remote(src, dst, send_sem, recv_sem, to):
    return pltpu.make_async_remote_copy(src_ref=src, dst_ref=dst, send_sem=send_sem, recv_sem=recv_sem, device_id=to,
                                        device_id_type=MESH)


def plane_allgather(big, small):
    rows = big.shape[0]
    rh = rows // 2
    tile = 16 if big.dtype == BF16 else 8
    ch_full = _row_chunks(rows, tile, 8)
    ch_half = _row_chunks(rh, tile, 4)

    def body(big_ref, small_ref, obig_ref, osmall_ref, send_sems, recv_sems, fwd_send, fwd_recv, own_send, own_recv):
        x, y, c = _me()
        me = 2 * x + y
        sibling = (x, y, 1 - c)
        mine = pl.ds(c * rh, rh)
        other = pl.ds((1 - c) * rh, rh)
        peers = [((x + fx) & 1, (y + fy) & 1) for fx, fy in _FLIPS]
        for st, sz in ch_full:
            sl = pl.ds(st, sz)
            _remote(big_ref.at[sl], obig_ref.at[me, sl], own_send.at[0], own_recv.at[0], sibling).start()
        _remote(small_ref, osmall_ref.at[me], own_send.at[1], own_recv.at[1], sibling).start()
        for j, (px, py) in enumerate(peers):
            for st, sz in ch_half:
                sl = pl.ds(c * rh + st, sz)
                _remote(big_ref.at[sl], obig_ref.at[me, sl], send_sems.at[j], recv_sems.at[j], (px, py, c)).start()
            _remote(small_ref, osmall_ref.at[me], send_sems.at[3 + j], recv_sems.at[3 + j], (px, py, c)).start()
        for j, (px, py) in enumerate(peers):
            pidx = 2 * px + py
            _remote(big_ref.at[mine], obig_ref.at[pidx, mine], send_sems.at[j], recv_sems.at[j], (px, py, c)).wait_recv()
            for st, sz in ch_half:
                sl = pl.ds(c * rh + st, sz)
                _remote(obig_ref.at[pidx, sl], obig_ref.at[pidx, sl], fwd_send.at[j], fwd_recv.at[j], sibling).start()
            _remote(small_ref, osmall_ref.at[pidx], send_sems.at[3 + j], recv_sems.at[3 + j], (px, py, c)).wait_recv()
        for j, (px, py) in enumerate(peers):
            pidx = 2 * px + py
            _remote(obig_ref.at[pidx, other], obig_ref.at[pidx, other], fwd_send.at[j], fwd_recv.at[j], sibling).wait_recv()
        for j, (px, py) in enumerate(peers):
            pidx = 2 * px + py
            _remote(big_ref.at[mine], obig_ref.at[me, mine], send_sems.at[j], recv_sems.at[j], (px, py, c)).wait_send()
            _remote(small_ref, osmall_ref.at[me], send_sems.at[3 + j], recv_sems.at[3 + j], (px, py, c)).wait_send()
            _remote(obig_ref.at[pidx, mine], obig_ref.at[pidx, mine], fwd_send.at[j], fwd_recv.at[j], sibling).wait_send()
        _remote(big_ref, obig_ref.at[me], own_send.at[0], own_recv.at[0], sibling).wait()
        _remote(small_ref, osmall_ref.at[me], own_send.at[1], own_recv.at[1], sibling).wait()

    return pl.pallas_call(
        body, out_shape=[jax.ShapeDtypeStruct((N_PLANE,) + big.shape, big.dtype),
                         jax.ShapeDtypeStruct((N_PLANE,) + small.shape, small.dtype)],
        in_specs=[ANY, ANY], out_specs=[ANY, ANY],
        scratch_shapes=[pltpu.SemaphoreType.DMA((6,)), pltpu.SemaphoreType.DMA((6,)), pltpu.SemaphoreType.DMA((3,)),
                        pltpu.SemaphoreType.DMA((3,)), pltpu.SemaphoreType.DMA((2,)), pltpu.SemaphoreType.DMA((2,))],
        name="plane_allgather", compiler_params=_params(),
    )(big, small)


def plane_scatter(p):
    tile = 16 if p.dtype == BF16 else 8
    chunks = _row_chunks(p.shape[1], tile, 4)

    def body(p_ref, out_ref, send_sems, recv_sems):
        x, y, c = _me()
        peers = [((x + fx) & 1, (y + fy) & 1) for fx, fy in _FLIPS]
        for j, (px, py) in enumerate(peers):
            for st, sz in chunks:
                sl = pl.ds(st, sz)
                _remote(p_ref.at[2 * px + py, sl], out_ref.at[j, sl], send_sems.at[j], recv_sems.at[j], (px, py, c)).start()
        for j, (px, py) in enumerate(peers):
            _remote(p_ref.at[0], out_ref.at[j], send_sems.at[j], recv_sems.at[j], (px, py, c)).wait_recv()
        for j, (px, py) in enumerate(peers):
            _remote(p_ref.at[0], out_ref.at[j], send_sems.at[j], recv_sems.at[j], (px, py, c)).wait_send()

    return pl.pallas_call(
        body, out_shape=jax.ShapeDtypeStruct((len(_FLIPS),) + p.shape[1:], p.dtype), in_specs=[ANY], out_specs=ANY,
        scratch_shapes=[pltpu.SemaphoreType.DMA((3,)), pltpu.SemaphoreType.DMA((3,))],
        name="plane_scatter", compiler_params=_params(),
    )(p)


def sibling_halves(buf):
    n_blk, _, rows, cols = buf.shape
    tile = 16 if buf.dtype == BF16 else 8
    chunks = _row_chunks(rows, tile, 2)

    def body(buf_ref, got_ref, send_sem, recv_sem):
        x, y, c = _me()
        for j in range(n_blk):
            for st, sz in chunks:
                sl = pl.ds(st, sz)
                _remote(buf_ref.at[j, 1 - c, sl], got_ref.at[j, sl], send_sem, recv_sem, (x, y, 1 - c)).start()
        _remote(got_ref, got_ref, send_sem, recv_sem, (x, y, 1 - c)).wait()

    return pl.pallas_call(
        body, out_shape=jax.ShapeDtypeStruct((n_blk, rows, cols), buf.dtype), in_specs=[ANY], out_specs=ANY,
        scratch_shapes=[pltpu.SemaphoreType.DMA, pltpu.SemaphoreType.DMA],
        name="sibling_halves", compiler_params=_params(),
    )(buf)


def sibling_swap(s):
    tile = 16 if s.dtype == BF16 else 8
    chunks = _row_chunks(s.shape[0], tile, 8)

    def body(s_ref, got_ref, send_sem, recv_sem):
        x, y, c = _me()
        for st, sz in chunks:
            sl = pl.ds(st, sz)
            _remote(s_ref.at[sl], got_ref.at[sl], send_sem, recv_sem, (x, y, 1 - c)).start()
        _remote(s_ref, got_ref, send_sem, recv_sem, (x, y, 1 - c)).wait()

    return pl.pallas_call(
        body, out_shape=jax.ShapeDtypeStruct(s.shape, s.dtype), in_specs=[ANY], out_specs=ANY,
        scratch_shapes=[pltpu.SemaphoreType.DMA, pltpu.SemaphoreType.DMA],
        name="sibling_swap", compiler_params=_params(),
    )(s)


def _heads(t, n_heads):
    b, s, w = t.shape
    return jnp.transpose(t.reshape(b, s, n_heads, w // n_heads), (0, 2, 1, 3)).reshape(b * n_heads, s, w // n_heads)


def _unheads(t, b):
    bh, s, d = t.shape
    return jnp.transpose(t.reshape(b, bh // b, s, d), (0, 2, 1, 3)).reshape(b, s, (bh // b) * d)


def _op(cache, fn, name, kinds, out_dims, **kw):
    key = (name, tuple(out_dims), tuple(sorted(kw.items())))
    if key not in cache:
        cache[key] = make_rowwise(fn, name, kinds, out_dims, **kw)
    return cache[key]


def _even_mixer(ops, a, w, n_ctx):
    b, s, d = a.shape
    proj = linear(a.reshape(b * s, d), w["e_w_in"]).reshape(b, s, -1)
    q, k, v, u = jnp.split(proj, [GQA_Q_W, GQA_Q_W + GQA_KV_W, GQA_Q_W + 2 * GQA_KV_W], axis=-1)
    cos, sin = _rope_tables(n_ctx, s - n_ctx, HEAD_DIM, 0, HEAD_DIM)
    shift = HEAD_DIM // 4
    qn = head_norm_rope(q, w["e_g_q"][None], jnp.tile(cos, (1, GQA_Q_HEADS)), jnp.tile(sin, (1, GQA_Q_HEADS)),
                        jnp.asarray(_rope_signs(HEAD_DIM, 0, HEAD_DIM, GQA_Q_HEADS)), GQA_Q_HEADS, shift)
    kn = head_norm_rope(k, w["e_g_k"][None], jnp.tile(cos, (1, GQA_KV_HEADS)), jnp.tile(sin, (1, GQA_KV_HEADS)),
                        jnp.asarray(_rope_signs(HEAD_DIM, 0, HEAD_DIM, GQA_KV_HEADS)), GQA_KV_HEADS, shift)
    att = gqa_attention(qn, kn, v, n_ctx)
    ys = []
    for dr in range(2):
        a_re, a_im, bb_re, bb_im = _s5_discretize(w["ssm_lam_re"][dr], w["ssm_lam_im"][dr], w["ssm_log_dt"][dr],
                                                  w["ssm_b_re"][dr], w["ssm_b_im"][dr])
        a_flat = jnp.stack([a_re.reshape(-1), a_im.reshape(-1)])
        b_bd = jnp.stack([_block_diag(jnp.swapaxes(bb_re, 1, 2)), _block_diag(jnp.swapaxes(bb_im, 1, 2))])
        c_bd = jnp.stack([_block_diag(jnp.swapaxes(w["ssm_c_re"][dr], 1, 2)),
                          _block_diag(jnp.swapaxes(w["ssm_c_im"][dr], 1, 2))])
        ys.append(s5_direction(u, a_flat, b_bd, c_bd, n_ctx, dr == 1))
    pre = _op(ops, _fn_glu_pre, "glu_pre", ("row", "row", "row", "glob"), (SSM_WIDTH,))
    post = _op(ops, _fn_glu_post, "glu_post", ("row", "row", "glob"), (SSM_WIDTH,))
    z = pre(u, ys[0], ys[1], w["ssm_d"][None])[0]
    t = linear(z.reshape(b * s, SSM_WIDTH), w["ssm_w_glu"]).reshape(b, s, SSM_WIDTH)
    ssm = post(z, t, w["ssm_b_glu"][None])[0]
    mix = jnp.concatenate([att, ssm], axis=-1)
    return linear(mix.reshape(b * s, -1), w["e_w_out"]).reshape(b, s, d)


def _odd_mixer(ops, a, w, n_ctx):
    b, s, d = a.shape
    w_in = jnp.pad(w["o_w_in"], ((0, 0), (0, ODD_IN_PAD - ODD_IN_W)))
    proj = linear(a.reshape(b * s, d), w_in).reshape(b, s, -1)
    c1 = MLA_Q_RANK
    c2 = c1 + MLA_KV_RANK
    c3 = c2 + MLA_ROPE
    cq, ckv, kr, nq, nk, nv, _ = jnp.split(proj, [c1, c2, c3, c3 + NA_W, c3 + 2 * NA_W, ODD_IN_W], axis=-1)
    nrm = lambda wd: _op(ops, _fn_norm, f"norm{wd}", ("row", "glob"), (wd,))
    cqn = nrm(MLA_Q_RANK)(cq, w["mla_g_cq"][None])[0]
    ckvn = nrm(MLA_KV_RANK)(ckv, w["mla_g_ckv"][None])[0]
    q = linear(cqn.reshape(b * s, -1), w["mla_w_uq"]).reshape(b, s, -1)
    kv = linear(ckvn.reshape(b * s, -1), w["mla_w_ukv"]).reshape(b, s, -1)
    cos, sin = _rope_tables(n_ctx, s - n_ctx, MLA_QK, MLA_NOPE, MLA_ROPE)
    cos, sin = jnp.tile(cos, (1, MLA_HEADS)), jnp.tile(sin, (1, MLA_HEADS))
    sg = jnp.asarray(_rope_signs(MLA_QK, MLA_NOPE, MLA_ROPE, MLA_HEADS))
    mq = head_norm_rope(q, w["mla_g_q"][None], cos, sin, sg, MLA_HEADS, MLA_ROPE // 4)
    mk = mla_k_prep(kv, kr, w["mla_g_k"][None], cos, sin, sg)
    mla = mla_attention(mq, mk, kv, n_ctx)
    nqn = head_norm_rope(nq, w["na_g_q"][None], None, None, None, NA_HEADS, 0)
    nkn = head_norm_rope(nk, w["na_g_k"][None], None, None, None, NA_HEADS, 0)
    na = na_attention_tm(nqn, nkn, nv, na_bias_table(w["na_rpb"]), n_ctx)
    mix = jnp.concatenate([mla, na], axis=-1)
    return linear(mix.reshape(b * s, -1), w["o_w_out"]).reshape(b, s, d)


_EVEN_KEYS = ("e_w_in", "e_w_out", "e_g_q", "e_g_k", "ssm_lam_re", "ssm_lam_im", "ssm_log_dt", "ssm_b_re", "ssm_b_im",
              "ssm_c_re", "ssm_c_im", "ssm_d", "ssm_w_glu", "ssm_b_glu")
_ODD_KEYS = ("o_w_in", "o_w_out", "mla_g_cq", "mla_g_ckv", "mla_w_uq", "mla_w_ukv", "mla_g_q", "mla_g_k", "na_g_q",
             "na_g_k", "na_rpb")


def _trunk(x_all, mods, w, n_ctx):
    ops = {}
    depth = mods.shape[0]
    b, s, d = x_all.shape
    modulate = _op(ops, _fn_modulate, "modulate", ("row", "glob", "seg", "seg"), (d,), nctx_rows=n_ctx)
    gated = make_gated_add(d, n_ctx)
    x = x_all
    for i in range(depth):
        j = i // 2
        m = [mods[i][:, :, r:r + 1, :] for r in range(N_MOD)]
        a = modulate(x, w["g_norm1"][i][None], m[0], m[1])[0]
        if i % 2 == 0:
            o = _even_mixer(ops, a, {k: w[k][j] for k in _EVEN_KEYS}, n_ctx)
        else:
            o = _odd_mixer(ops, a, {k: w[k][j] for k in _ODD_KEYS}, n_ctx)
        x = gated(x, o, m[2])
        a2 = modulate(x, w["g_norm2"][i][None], m[3], m[4])[0]
        f = ffn(a2.reshape(b * s, d), w["w_ff1"][i], w["w_ff2"][i]).reshape(b, s, d)
        x = gated(x, f, m[5])
    return x[:, n_ctx:]


def local_step(x, ctx, mods, w, loss_target):
    n_ctx = ctx.shape[1]
    x_all = jnp.concatenate([ctx, x], axis=1)
    y, vjp = jax.vjp(lambda xa, md, ww: _trunk(xa, md, ww, n_ctx), x_all, mods, w)
    loss_tile, dy = _loss_head(y, loss_target)
    dx_all, dmods, dw = vjp(dy)
    return loss_tile[0, 0], dx_all[:, n_ctx:], dmods, dw


_SHARDED = (("w_ff1", 2), ("w_ff2", 1), ("e_w_in", 2), ("e_w_out", 1), ("o_w_in", 2), ("o_w_out", 1),
            ("mla_w_uq", 2), ("mla_w_ukv", 2), ("ssm_w_glu", 1))
_SHARDED_SMALL = (("mla_g_cq", 1), ("mla_g_ckv", 1))
_REPLICATED = ("g_norm1", "g_norm2", "e_g_q", "e_g_k", "ssm_lam_re", "ssm_lam_im", "ssm_log_dt", "ssm_b_re", "ssm_b_im",
               "ssm_c_re", "ssm_c_im", "ssm_d", "ssm_b_glu", "mla_g_q", "mla_g_k", "na_g_q", "na_g_k", "na_rpb")
_WEIGHTS = ("c_ctx", "w_mod", "b_mod", "g_norm1", "g_norm2", "w_ff1", "w_ff2", "e_w_in", "e_w_out", "e_g_q", "e_g_k",
            "ssm_lam_re", "ssm_lam_im", "ssm_log_dt", "ssm_b_re", "ssm_b_im", "ssm_c_re", "ssm_c_im", "ssm_d",
            "ssm_w_glu", "ssm_b_glu", "o_w_in", "o_w_out", "mla_g_cq", "mla_g_ckv", "mla_w_uq", "mla_w_ukv", "mla_g_q",
            "mla_g_k", "na_g_q", "na_g_k", "na_rpb")
_PACK_ROWS = 64


def _pack(arrs, dtype, cols=1024, row_mult=_PACK_ROWS):
    blocks, tail, off = [], [], 0
    for a in arrs:
        n = int(np.prod(a.shape))
        if not tail and off % cols == 0 and n % cols == 0:
            blocks.append(a.astype(dtype).reshape(-1, cols))
        else:
            tail.append(a.astype(dtype).reshape(-1))
        off += n
    rows = -(-off // cols)
    pad = (-rows) % row_mult * cols + rows * cols - off
    if tail or pad:
        blocks.append(jnp.concatenate(tail + [jnp.zeros((pad,), dtype)]).reshape(-1, cols))
    return jnp.concatenate(blocks, axis=0)


def _unpack(packed, shapes):
    cols = packed.shape[-1]
    packed = packed.reshape(-1, cols)
    out, off = [], 0
    for sh in shapes:
        n = int(np.prod(sh))
        if off % cols == 0 and n % cols == 0:
            out.append(packed[off // cols:(off + n) // cols].reshape(sh))
        else:
            r0, r1 = off // cols, -(-(off + n) // cols)
            out.append(packed[r0:r1].reshape(-1)[off - r0 * cols:off - r0 * cols + n].reshape(sh))
        off += n
    return out


def _silu(t):
    return t * jax.nn.sigmoid(t)


def kernel(x, c, ctx, c_ctx, w_mod, b_mod, g_norm1, g_norm2, w_ff1, w_ff2, e_w_in, e_w_out, e_g_q, e_g_k, ssm_lam_re, ssm_lam_im, ssm_log_dt, ssm_b_re, ssm_b_im, ssm_c_re, ssm_c_im, ssm_d, ssm_w_glu, ssm_b_glu, o_w_in, o_w_out, mla_g_cq, mla_g_ckv, mla_w_uq, mla_w_ukv, mla_g_q, mla_g_k, na_g_q, na_g_k, na_rpb, loss_target, m_c_ctx, m_w_mod, m_b_mod, m_g_norm1, m_g_norm2, m_w_ff1, m_w_ff2, m_e_w_in, m_e_w_out, m_e_g_q, m_e_g_k, m_ssm_lam_re, m_ssm_lam_im, m_ssm_log_dt, m_ssm_b_re, m_ssm_b_im, m_ssm_c_re, m_ssm_c_im, m_ssm_d, m_ssm_w_glu, m_ssm_b_glu, m_o_w_in, m_o_w_out, m_mla_g_cq, m_mla_g_ckv, m_mla_w_uq, m_mla_w_ukv, m_mla_g_q, m_mla_g_k, m_na_g_q, m_na_g_k, m_na_rpb, v_c_ctx, v_w_mod, v_b_mod, v_g_norm1, v_g_norm2, v_w_ff1, v_w_ff2, v_e_w_in, v_e_w_out, v_e_g_q, v_e_g_k, v_ssm_lam_re, v_ssm_lam_im, v_ssm_log_dt, v_ssm_b_re, v_ssm_b_im, v_ssm_c_re, v_ssm_c_im, v_ssm_d, v_ssm_w_glu, v_ssm_b_glu, v_o_w_in, v_o_w_out, v_mla_g_cq, v_mla_g_ckv, v_mla_w_uq, v_mla_w_ukv, v_mla_g_q, v_mla_g_k, v_na_g_q, v_na_g_k, v_na_rpb):
    env = dict(locals())
    weights = {n: env[n] for n in _WEIGHTS}
    mom_m = {n: env["m_" + n] for n in _WEIGHTS}
    mom_v = {n: env["v_" + n] for n in _WEIGHTS}
    ax, ay, ac = _me()
    plane = 2 * ax + ay
    dev = 4 * ax + 2 * ay + ac
    b_loc, d = c.shape
    depth = w_mod.shape[0]
    n_all = N_DEV * b_loc
    mod_cols = w_mod.shape[2]

    big = _pack([weights[n] for n, _ in _SHARDED], BF16)
    small = _pack([weights[n] for n, _ in _SHARDED_SMALL], F32, cols=LANE, row_mult=8)
    g_big, g_small = plane_allgather(big, small)
    full = {n: weights[n] for n in _REPLICATED}
    parts = [_unpack(g_big[j], [weights[n].shape for n, _ in _SHARDED]) for j in range(N_PLANE)]
    for t, (n, axis) in enumerate(_SHARDED):
        full[n] = [jnp.concatenate([parts[j][t][l] for j in range(N_PLANE)], axis=axis - 1)
                   for l in range(weights[n].shape[0])]
    parts_s = [_unpack(g_small[j], [weights[n].shape for n, _ in _SHARDED_SMALL]) for j in range(N_PLANE)]
    for t, (n, axis) in enumerate(_SHARDED_SMALL):
        full[n] = jnp.concatenate([parts_s[j][t] for j in range(N_PLANE)], axis=axis)

    rows_pad = 8 * ((n_all + 1 + 7) // 8)
    c_all = allgather8(jnp.pad(c, ((0, 8 - b_loc), (0, 0)))).reshape(N_DEV, 8, d)[:, :b_loc].reshape(n_all, d)
    cond_raw = jnp.concatenate([c_all, c_ctx[None], jnp.zeros((rows_pad - n_all - 1, d), F32)], axis=0)
    b_cols = lax.dynamic_slice_in_dim(b_mod, plane * mod_cols, mod_cols, axis=1)
    mod_loc = jnp.stack([_mm(cond_raw, w_mod[i], a_act="silu") + b_cols[i][None] for i in range(depth)])
    mod_g = allgather8(mod_loc.reshape(depth * rows_pad, mod_cols)).reshape(N_PLANE, 2, depth, rows_pad, mod_cols)
    mod_all = jnp.concatenate([mod_g[j, 0] for j in range(N_PLANE)], axis=-1)
    m_lat = lax.dynamic_slice_in_dim(mod_all, dev * b_loc, b_loc, axis=1)
    m_ctx = jnp.broadcast_to(mod_all[:, n_all][:, None], m_lat.shape)
    mods = jnp.stack([m_ctx, m_lat], axis=2).reshape(depth, b_loc, 2, N_MOD, d)

    loss_part, grad_x, dmods, dw = local_step(x, ctx, mods, full, loss_target)
    loss = lax.psum(loss_part, ("x", "y", "c"))

    dm = dmods.reshape(depth, b_loc, 2, N_MOD * d)
    dm_rows = jnp.concatenate([dm[:, :, 1], jnp.sum(dm[:, :, 0], axis=1, keepdims=True)], axis=1)
    rep_shapes = [weights[n].shape for n in _REPLICATED]
    small_pack = _pack([dm_rows] + [dw[n] for n in _REPLICATED], F32, cols=1024, row_mult=8)
    sp_rows = small_pack.shape[0]
    gathered = allgather8(small_pack).reshape(N_DEV, sp_rows, 1024)
    n_dm = depth * (b_loc + 1) * N_MOD * d
    dm_all = gathered.reshape(N_DEV, -1)[:, :n_dm].reshape(N_DEV, depth, b_loc + 1, N_MOD * d)
    rep_sum = _sum_rows(gathered, N_DEV).reshape(-1)
    rep_grads = dict(zip(_REPLICATED, _unpack(rep_sum[n_dm:], rep_shapes)))
    d_ctx_row = rep_sum[:n_dm].reshape(depth, b_loc + 1, N_MOD * d)[:, b_loc]
    d_lat_rows = jnp.transpose(dm_all[:, :, :b_loc], (1, 0, 2, 3)).reshape(depth, n_all, N_MOD * d)
    d_mod_all = jnp.concatenate([d_lat_rows, d_ctx_row[:, None],
                                 jnp.zeros((depth, rows_pad - n_all - 1, N_MOD * d), F32)], axis=1)
    grads = dict(rep_grads)
    grads["b_mod"] = jnp.sum(d_mod_all, axis=1)
    d_cols = lax.dynamic_slice_in_dim(d_mod_all, plane * mod_cols, mod_cols, axis=2)
    grads["w_mod"] = jnp.stack([_mm(cond_raw, d_cols[i], ta=True, a_act="silu") for i in range(depth)])
    d_cond = _mm(d_cols[0], w_mod[0], tb=True)
    for i in range(1, depth):
        d_cond = _add2(d_cond, _mm(d_cols[i], w_mod[i], tb=True))
    d_cond_g = allgather8(d_cond[n_all:n_all + 8] if rows_pad - n_all >= 8 else
                          jnp.pad(d_cond[n_all:], ((0, 8 - (rows_pad - n_all)), (0, 0)))).reshape(N_PLANE, 2, 8, d)
    d_silu = _sum_rows(d_cond_g[:, 0], N_PLANE)[0]
    sg = jax.nn.sigmoid(c_ctx)
    grads["c_ctx"] = d_silu * (sg * (1.0 + c_ctx * (1.0 - sg)))

    def shards_of(g, axis, j):
        layers = g if isinstance(g, (list, tuple)) else [g]
        ax = axis - 1 if isinstance(g, (list, tuple)) else axis
        n = layers[0].shape[ax] // N_PLANE
        return [lax.slice_in_dim(t, j * n, (j + 1) * n, axis=ax) for t in layers]

    send = jnp.stack([_pack([t for n, axis in _SHARDED + _SHARDED_SMALL for t in shards_of(dw[n], axis, j)], BF16)
                      for j in range(N_PLANE)])
    rows_h = send.shape[1] // 2
    send = send.reshape(N_PLANE, 2, rows_h, 1024)
    mine = lax.dynamic_index_in_dim(send, ac, 1, keepdims=False).reshape(N_PLANE * rows_h, 1024)
    theirs = sibling_halves(send).reshape(N_PLANE * rows_h, 1024)
    chip_sum = _accumulate([mine, theirs], BF16).reshape(N_PLANE, rows_h, 1024)
    own = lax.dynamic_index_in_dim(chip_sum, plane, 0, keepdims=False)
    done = _accumulate([own, plane_scatter(chip_sum)], BF16)
    both = jnp.stack([done, sibling_swap(done)])
    flat = jnp.where(ac == 0, both, both[::-1]).astype(F32).reshape(-1, 1024)
    shard_shapes = [weights[n].shape for n, _ in _SHARDED] + [weights[n].shape for n, _ in _SHARDED_SMALL]
    for (n, _), g in zip(_SHARDED + _SHARDED_SMALL, _unpack(flat, shard_shapes)):
        grads[n] = g

    big_names = ("w_mod",) + tuple(n for n, _ in _SHARDED)
    small_names = tuple(n for n in _WEIGHTS if n not in big_names)
    delta, new_m, new_v = {}, {}, {}
    for n in big_names:
        delta[n], new_m[n], new_v[n] = _adamw(weights[n], grads[n], mom_m[n], mom_v[n])
    sm_shapes = [weights[n].shape for n in small_names]
    packed = [_pack([src[n] for n in small_names], F32, cols=1024, row_mult=8)
              for src in (weights, grads, mom_m, mom_v)]
    for dst, res in zip((delta, new_m, new_v), _adamw(*packed)):
        dst.update(dict(zip(small_names, _unpack(res, sm_shapes))))

    return (loss, grad_x, *[grads[n] for n in _WEIGHTS], *[delta[n] for n in _WEIGHTS],
            *[new_m[n] for n in _WEIGHTS], *[new_v[n] for n in _WEIGHTS])
```

```python
import functools
import math

import numpy as np
import jax
import jax.numpy as jnp
from jax import lax
from jax.experimental import pallas as pl
from jax.experimental.pallas import tpu as pltpu

F32 = jnp.float32
BF16 = jnp.bfloat16
HI = lax.Precision.HIGHEST
MESH = pl.DeviceIdType.MESH
ANY = pl.BlockSpec(memory_space=pl.ANY)
VMEM_SPEC = pl.BlockSpec(memory_space=pltpu.VMEM)

GRID_W = 64
HEAD_DIM = 64
ROPE_BASE = 10000.0
EPS = 1e-6
N_MOD = 6
GQA_Q_HEADS, GQA_KV_HEADS = 12, 4
GQA_Q_W, GQA_KV_W = GQA_Q_HEADS * HEAD_DIM, GQA_KV_HEADS * HEAD_DIM
SSM_WIDTH, SSM_GROUP, SSM_STATE = 256, 16, 64
SSM_GROUPS = SSM_WIDTH // SSM_GROUP
SSM_LANES = SSM_GROUPS * SSM_STATE
MLA_HEADS, MLA_Q_RANK, MLA_KV_RANK, MLA_NOPE, MLA_ROPE, MLA_V = 8, 512, 256, 64, 32, 64
MLA_QK = MLA_NOPE + MLA_ROPE
NA_HEADS, NA_WIN_R, NA_WIN_C = 8, 8, 16
NA_W = NA_HEADS * HEAD_DIM
NA_BAND = NA_WIN_R * GRID_W
ODD_IN_W = MLA_Q_RANK + MLA_KV_RANK + MLA_ROPE + 3 * NA_W
ODD_IN_PAD = 2560
ADAM_LR, ADAM_B1, ADAM_B2, ADAM_EPS, ADAM_WD, ADAM_STEP = 0.001, 0.9, 0.999, 1e-08, 0.01, 10
NEG = -1e30
VMEM_LIMIT = 56 * 1024 * 1024
LANE = 128
MM_TILE_M = (1152, 1024, 768, 512, 256, 128)
MM_TILE_N = (1280, 1024, 768, 512, 256, 128)
MM_TILE_K = (1152, 1024, 768, 512, 256, 128)
N_PLANE = 4
N_DEV = 8


def _pick(n, cands):
    for c in cands:
        if n % c == 0:
            return c
    return n


def _params(**kw):
    return pltpu.CompilerParams(vmem_limit_bytes=VMEM_LIMIT, **kw)


def _mm(a, b, *, ta=False, tb=False, a_act=None, epi=None, e=None, exact=False, out_dtype=F32):
    m, kd = (a.shape[1], a.shape[0]) if ta else a.shape
    n = b.shape[0] if tb else b.shape[1]
    tm = _pick(m, MM_TILE_M)
    tn = _pick(n, MM_TILE_N)
    tk = _pick(kd, MM_TILE_K)
    nk = kd // tk
    dn = (((0 if ta else 1,), (1 if tb else 0,)), ((), ()))
    narrow = jnp.dtype(out_dtype) != jnp.dtype(F32)
    assert not (narrow and epi is not None)

    def body(*refs):
        if narrow:
            a_ref, b_ref, out_ref, o_ref = refs
        elif epi is None:
            a_ref, b_ref, o_ref = refs
        else:
            a_ref, b_ref, e_ref, o_ref = refs
        k = pl.program_id(2)
        av = a_ref[...]
        if a_act == "relu2":
            av = jnp.square(jnp.maximum(av, 0.0))
        elif a_act == "silu":
            av = av * jax.nn.sigmoid(av)
        bv = b_ref[...]
        if exact:
            p = lax.dot_general(av, bv, dn, precision=HI, preferred_element_type=F32)
        else:
            p = lax.dot_general(av.astype(BF16), bv.astype(BF16), dn, preferred_element_type=F32)

        @pl.when(k == 0)
        def _():
            o_ref[...] = p

        @pl.when(k > 0)
        def _():
            o_ref[...] += p

        if epi == "drelu2":
            @pl.when(k == nk - 1)
            def _():
                o_ref[...] = o_ref[...] * (2.0 * jnp.maximum(e_ref[...], 0.0))

        if narrow:
            @pl.when(k == nk - 1)
            def _():
                out_ref[...] = o_ref[...].astype(out_dtype)

    a_spec = pl.BlockSpec((tk, tm), lambda i, j, k: (k, i)) if ta else pl.BlockSpec((tm, tk), lambda i, j, k: (i, k))
    b_spec = pl.BlockSpec((tn, tk), lambda i, j, k: (j, k)) if tb else pl.BlockSpec((tk, tn), lambda i, j, k: (k, j))
    o_spec = pl.BlockSpec((tm, tn), lambda i, j, k: (i, j))
    ins, specs = [a, b], [a_spec, b_spec]
    if epi is not None:
        ins.append(e)
        specs.append(o_spec)
    name = f"mm_{m}x{kd}x{n}_{int(ta)}{int(tb)}_{a_act}_{epi}_{int(exact)}_{jnp.dtype(out_dtype).name}"
    return pl.pallas_call(
        body, out_shape=jax.ShapeDtypeStruct((m, n), out_dtype), grid=(m // tm, n // tn, nk),
        in_specs=specs, out_specs=o_spec, name=name, compiler_params=_params(),
        scratch_shapes=[pltpu.VMEM((tm, tn), F32)] if narrow else [],
    )(*ins)


@functools.partial(jax.custom_vjp, nondiff_argnums=(2,))
def _linear(a, w, exact):
    return _mm(a, w, exact=exact)


def _linear_fwd(a, w, exact):
    return _mm(a, w, exact=exact), (a, w)


def _linear_bwd(exact, res, g):
    a, w = res
    return _mm(g, w, tb=True, exact=exact), _mm(a, g, ta=True, exact=exact, out_dtype=w.dtype)


_linear.defvjp(_linear_fwd, _linear_bwd)


def linear(a, w, exact=False):
    return _linear(a, w, exact)


@jax.custom_vjp
def ffn(a, w1, w2):
    return _mm(_mm(a, w1), w2, a_act="relu2")


def _ffn_fwd(a, w1, w2):
    h1 = _mm(a, w1)
    return _mm(h1, w2, a_act="relu2"), (a, w1, w2, h1)


def _ffn_bwd(res, g):
    a, w1, w2, h1 = res
    dh1 = _mm(g, w2, tb=True, epi="drelu2", e=h1)
    dw2 = _mm(h1, g, ta=True, a_act="relu2", out_dtype=w2.dtype)
    return _mm(dh1, w1, tb=True), _mm(a, dh1, ta=True, out_dtype=w1.dtype), dw2


ffn.defvjp(_ffn_fwd, _ffn_bwd)


def make_rowwise(fn, name, kinds, out_dims, nctx_rows=0, whole_seq=False):
    n_in = len(kinds)
    n_out = len(out_dims)
    diff = [i for i, kd in enumerate(kinds) if kd in ("row", "glob", "seg")]

    def layout(args):
        row0 = args[kinds.index("row")]
        g, s = row0.shape[0], row0.shape[1]
        ts = s if whole_seq else (min(256, nctx_rows) if nctx_rows else _pick(s, (256, 128, 64)))
        nctx = nctx_rows // ts
        return g, s, ts, nctx

    def spec_of(kind, arr, ts, nctx):
        if kind == "row":
            return pl.BlockSpec((None, ts, arr.shape[2]), lambda g, i: (g, i, 0))
        if kind == "tab":
            return pl.BlockSpec((ts, arr.shape[1]), lambda g, i: (i, 0))
        if kind in ("const", "glob"):
            return pl.BlockSpec(arr.shape, lambda g, i: (0, 0))
        return pl.BlockSpec((None, None) + arr.shape[2:], lambda g, i: (g, (i >= nctx).astype(jnp.int32), 0, 0))

    def fwd_call(*args):
        g, s, ts, nctx = layout(args)

        def body(*refs):
            vals = [r[...] for r in refs[:n_in]]
            outs = fn(*vals)
            for o_ref, o in zip(refs[n_in:], outs):
                o_ref[...] = o

        return pl.pallas_call(
            body, out_shape=[jax.ShapeDtypeStruct((g, s, d), F32) for d in out_dims], grid=(g, s // ts),
            in_specs=[spec_of(kd, a, ts, nctx) for kd, a in zip(kinds, args)],
            out_specs=[pl.BlockSpec((None, ts, d), lambda g_, i: (g_, i, 0)) for d in out_dims],
            name=f"{name}_f_{g}x{s}", compiler_params=_params(),
        )(*args)

    def bwd_call(args, cts):
        g, s, ts, nctx = layout(args)

        def body(*refs):
            in_refs, ct_refs, out_refs = refs[:n_in], refs[n_in:n_in + n_out], refs[n_in + n_out:]
            gi, i = pl.program_id(0), pl.program_id(1)
            vals = [r[...] for r in in_refs]

            def f(*dv):
                full = list(vals)
                for idx, v in zip(diff, dv):
                    full[idx] = v
                return tuple(fn(*full))

            _, vjp = jax.vjp(f, *[vals[idx] for idx in diff])
            grads = vjp(tuple(r[...] for r in ct_refs))
            for idx, o_ref, gr in zip(diff, out_refs, grads):
                if kinds[idx] == "row":
                    o_ref[...] = gr
                    continue
                if kinds[idx] == "glob":
                    first = jnp.logical_and(gi == 0, i == 0)
                else:
                    first = jnp.logical_or(i == 0, i == nctx)

                @pl.when(first)
                def _(o_ref=o_ref, gr=gr):
                    o_ref[...] = gr

                @pl.when(jnp.logical_not(first))
                def _(o_ref=o_ref, gr=gr):
                    o_ref[...] += gr

        in_specs = [spec_of(kd, a, ts, nctx) for kd, a in zip(kinds, args)]
        in_specs += [pl.BlockSpec((None, ts, d), lambda g_, i: (g_, i, 0)) for d in out_dims]
        return pl.pallas_call(
            body, out_shape=[jax.ShapeDtypeStruct(args[idx].shape, F32) for idx in diff], grid=(g, s // ts),
            in_specs=in_specs, out_specs=[spec_of(kinds[idx], args[idx], ts, nctx) for idx in diff],
            name=f"{name}_b_{g}x{s}", compiler_params=_params(),
        )(*args, *cts)

    @jax.custom_vjp
    def op(*args):
        return tuple(fwd_call(*args))

    def op_fwd(*args):
        return tuple(fwd_call(*args)), args

    def op_bwd(args, cts):
        grads = bwd_call(args, cts)
        full = [None] * n_in
        for idx, gr in zip(diff, grads):
            full[idx] = gr
        return tuple(jnp.zeros_like(a) if gfull is None else gfull for a, gfull in zip(args, full))

    op.defvjp(op_fwd, op_bwd)
    op.fwd_call, op.bwd_call = fwd_call, bwd_call
    return op


def make_modulate(d, n_ctx):
    one = make_rowwise(_fn_modulate, "modulate", ("row", "glob", "seg", "seg"), (d,), nctx_rows=n_ctx)
    two = make_rowwise(_fn_modulate_keep, "modulate_keep", ("row", "glob", "seg", "seg"), (d, d), nctx_rows=n_ctx)

    @jax.custom_vjp
    def op(x, g, shift, scale):
        return one.fwd_call(x, g, shift, scale)[0], x

    def fwd(x, g, shift, scale):
        return (one.fwd_call(x, g, shift, scale)[0], x), (x, g, shift, scale)

    def bwd(res, cts):
        return tuple(two.bwd_call(res, cts))

    op.defvjp(fwd, bwd)
    return op


def make_gated_add(d, n_ctx):
    add = make_rowwise(_fn_gated_add, "gated", ("row", "row", "seg"), (d,), nctx_rows=n_ctx)
    mul = make_rowwise(_fn_gate_mul, "gate_mul", ("row", "seg"), (d,), nctx_rows=n_ctx)

    @jax.custom_vjp
    def op(x, o, gate):
        return add.fwd_call(x, o, gate)[0]

    def fwd(x, o, gate):
        return add.fwd_call(x, o, gate)[0], (o, gate)

    def bwd(res, ct):
        do, dgate = mul.bwd_call(res, (ct,))
        return ct, do, dgate

    op.defvjp(fwd, bwd)
    return op


def _rms(x):
    return lax.rsqrt(jnp.mean(x * x, axis=-1, keepdims=True) + EPS)


def _fn_modulate(x, g, shift, scale):
    return ((x * _rms(x) * g) * (1.0 + scale) + shift,)


def _fn_modulate_keep(x, g, shift, scale):
    return _fn_modulate(x, g, shift, scale) + (x,)


def _fn_gated_add(x, o, gate):
    return (x + gate * o,)


def _fn_gate_mul(o, gate):
    return (gate * o,)


def _fn_norm(x, g):
    return (x * _rms(x) * g,)


def _fn_norm_rope(x, cos, sin, rot, g):
    y = x * _rms(x) * g
    r = jnp.dot(y, rot, precision=HI, preferred_element_type=F32)
    return (y * cos + r * sin,)


def _fn_glu_pre(u, y0, y1, d):
    return (jax.nn.gelu(d * u + y0 + y1),)


def _fn_glu_post(z, t, bg):
    return (z * jax.nn.sigmoid(t + bg),)


def _rope_matrix(dh, start, rot_dim):
    r = np.zeros((dh, dh), np.float32)
    q = rot_dim // 4
    for j in range(rot_dim):
        if (j // q) % 2 == 0:
            r[start + j + q, start + j] = -1.0
        else:
            r[start + j - q, start + j] = 1.0
    return r


def _rope_tables(n_ctx, n_lat, dh, start, rot_dim):
    t = jnp.arange(n_lat)
    rows = (t // GRID_W).astype(F32)
    cols = (t % GRID_W).astype(F32)
    axis_dim = rot_dim // 2
    freqs = ROPE_BASE ** (-jnp.arange(0, axis_dim, 2, dtype=F32) / axis_dim)
    ang_r = rows[:, None] * freqs
    ang_c = cols[:, None] * freqs
    ang = jnp.concatenate([ang_r, ang_r, ang_c, ang_c], axis=-1)
    cos = jnp.concatenate([jnp.ones((n_lat, start), F32), jnp.cos(ang)], axis=-1)
    sin = jnp.concatenate([jnp.zeros((n_lat, start), F32), jnp.sin(ang)], axis=-1)
    cos = jnp.concatenate([jnp.ones((n_ctx, dh), F32), cos], axis=0)
    sin = jnp.concatenate([jnp.zeros((n_ctx, dh), F32), sin], axis=0)
    return cos, sin


_NT = (((1,), (1,)), ((), ()))
_TN = (((0,), (0,)), ((), ()))


def _attn_fwd(q, k, v, group, n_ctx, scale):
    b, h, s, dq = q.shape
    dv = v.shape[-1]
    tq = min(256, n_ctx)
    nc = n_ctx // tq

    def body(q_ref, k_ref, v_ref, o_ref, lse_ref):
        qv = (q_ref[...] * scale).astype(BF16)

        def run(n_keys):
            sc = lax.dot_general(qv, k_ref[0:n_keys, :].astype(BF16), _NT, preferred_element_type=F32)
            m = jnp.max(sc, axis=-1, keepdims=True)
            p = jnp.exp(sc - m)
            l = jnp.sum(p, axis=-1, keepdims=True)
            o = jnp.dot(p.astype(BF16), v_ref[0:n_keys, :].astype(BF16), preferred_element_type=F32)
            o_ref[...] = o / l
            lse_ref[...] = m + jnp.log(l)

        pl.when(pl.program_id(2) < nc)(lambda: run(n_ctx))
        pl.when(pl.program_id(2) >= nc)(lambda: run(s))

    return pl.pallas_call(
        body, out_shape=[jax.ShapeDtypeStruct((b, h, s, dv), F32), jax.ShapeDtypeStruct((b, h, s, 1), F32)],
        grid=(b, h, s // tq),
        in_specs=[pl.BlockSpec((None, None, tq, dq), lambda bi, hi, i: (bi, hi, i, 0)),
                  pl.BlockSpec((None, None, s, dq), lambda bi, hi, i: (bi, lax.div(hi, group), 0, 0)),
                  pl.BlockSpec((None, None, s, dv), lambda bi, hi, i: (bi, lax.div(hi, group), 0, 0))],
        out_specs=[pl.BlockSpec((None, None, tq, dv), lambda bi, hi, i: (bi, hi, i, 0)),
                   pl.BlockSpec((None, None, tq, 1), lambda bi, hi, i: (bi, hi, i, 0))],
        name=f"attn_f_{h}x{s}x{dq}", compiler_params=_params(),
    )(q, k, v)


def _attn_dq(q, k, v, o, lse, do, group, n_ctx, scale):
    b, h, s, dq = q.shape
    dv = v.shape[-1]
    tq = min(256, n_ctx)
    nc = n_ctx // tq

    def body(q_ref, k_ref, v_ref, o_ref, lse_ref, do_ref, dq_ref, delta_ref):
        qv = (q_ref[...] * scale).astype(BF16)
        dov = do_ref[...]
        delta = jnp.sum(dov * o_ref[...], axis=-1, keepdims=True)
        delta_ref[...] = delta

        def run(n_keys):
            kv = k_ref[0:n_keys, :].astype(BF16)
            sc = lax.dot_general(qv, kv, _NT, preferred_element_type=F32)
            p = jnp.exp(sc - lse_ref[...])
            dp = lax.dot_general(dov.astype(BF16), v_ref[0:n_keys, :].astype(BF16), _NT, preferred_element_type=F32)
            ds = p * (dp - delta)
            dq_ref[...] = jnp.dot(ds.astype(BF16), kv, preferred_element_type=F32) * scale

        pl.when(pl.program_id(2) < nc)(lambda: run(n_ctx))
        pl.when(pl.program_id(2) >= nc)(lambda: run(s))

    qs = lambda d: pl.BlockSpec((None, None, tq, d), lambda bi, hi, i: (bi, hi, i, 0))
    ks = lambda d: pl.BlockSpec((None, None, s, d), lambda bi, hi, i: (bi, lax.div(hi, group), 0, 0))
    return pl.pallas_call(
        body, out_shape=[jax.ShapeDtypeStruct((b, h, s, dq), F32), jax.ShapeDtypeStruct((b, h, s, 1), F32)],
        grid=(b, h, s // tq),
        in_specs=[qs(dq), ks(dq), ks(dv), qs(dv), qs(1), qs(dv)], out_specs=[qs(dq), qs(1)],
        name=f"attn_dq_{h}x{s}x{dq}", compiler_params=_params(),
    )(q, k, v, o, lse, do)


def _attn_dkv(q, k, v, lse, delta, do, group, n_ctx, scale):
    b, h, s, dq = q.shape
    hk = k.shape[1]
    dv = v.shape[-1]
    tk = min(256, n_ctx)
    nc = n_ctx // tk

    def body(q_ref, k_ref, v_ref, lse_ref, delta_ref, do_ref, dk_ref, dv_ref):
        kv = k_ref[...].astype(BF16)
        vv = v_ref[...].astype(BF16)

        def run(r0):
            dk = jnp.zeros((tk, dq), F32)
            dvv = jnp.zeros((tk, dv), F32)
            for g in range(group):
                qg = (q_ref[g, r0:s, :] * scale).astype(BF16)
                dog = do_ref[g, r0:s, :].astype(BF16)
                sc = lax.dot_general(qg, kv, _NT, preferred_element_type=F32)
                p = jnp.exp(sc - lse_ref[g, r0:s, :])
                dvv = dvv + lax.dot_general(p.astype(BF16), dog, _TN, preferred_element_type=F32)
                dp = lax.dot_general(dog, vv, _NT, preferred_element_type=F32)
                ds = p * (dp - delta_ref[g, r0:s, :])
                dk = dk + lax.dot_general(ds.astype(BF16), qg, _TN, preferred_element_type=F32)
            dk_ref[...] = dk
            dv_ref[...] = dvv

        pl.when(pl.program_id(2) < nc)(lambda: run(0))
        pl.when(pl.program_id(2) >= nc)(lambda: run(n_ctx))

    gs = lambda d: pl.BlockSpec((None, group, s, d), lambda bi, hi, j: (bi, hi, 0, 0))
    ks = lambda d: pl.BlockSpec((None, None, tk, d), lambda bi, hi, j: (bi, hi, j, 0))
    return pl.pallas_call(
        body, out_shape=[jax.ShapeDtypeStruct((b, hk, s, dq), F32), jax.ShapeDtypeStruct((b, hk, s, dv), F32)],
        grid=(b, hk, s // tk),
        in_specs=[gs(dq), ks(dq), ks(dv), gs(1), gs(1), gs(dv)], out_specs=[ks(dq), ks(dv)],
        name=f"attn_dkv_{h}x{s}x{dq}", compiler_params=_params(),
    )(q, k, v, lse, delta, do)


@functools.partial(jax.custom_vjp, nondiff_argnums=(3, 4, 5))
def attention(q, k, v, group, n_ctx, scale):
    return _attn_fwd(q, k, v, group, n_ctx, scale)[0]


def _attention_fwd(q, k, v, group, n_ctx, scale):
    o, lse = _attn_fwd(q, k, v, group, n_ctx, scale)
    return o, (q, k, v, o, lse)


def _attention_bwd(group, n_ctx, scale, res, do):
    q, k, v, o, lse = res
    dq, delta = _attn_dq(q, k, v, o, lse, do, group, n_ctx, scale)
    dk, dv = _attn_dkv(q, k, v, lse, delta, do, group, n_ctx, scale)
    return dq, dk, dv


attention.defvjp(_attention_fwd, _attention_bwd)


def _na_geometry(i, nc, rows):
    r = i - nc
    rs = jnp.clip(r - NA_WIN_R // 2, 0, rows - NA_WIN_R)
    is_ctx = i < nc
    cls = jnp.where(is_ctx, NA_WIN_R, r - rs)
    return jnp.where(is_ctx, 0, rs), cls


def _na_scores(q_ref, k_ref, bias_ref, hd, n_ctx, start, scale):
    qv = (q_ref[hd] * scale).astype(BF16)
    kc = k_ref[hd, 0:n_ctx, :].astype(BF16)
    kb = k_ref[hd, pl.ds(start, NA_BAND), :].astype(BF16)
    s_c = lax.dot_general(qv, kc, _NT, preferred_element_type=F32)
    s_l = lax.dot_general(qv, kb, _NT, preferred_element_type=F32) + bias_ref[hd]
    return qv, kc, kb, s_c, s_l


NA_HEADS_FWD = 4
NA_HEADS_BWD = 2


def _na_specs(hp, s, dh, nc, rows):
    qs = lambda d: pl.BlockSpec((None, hp, GRID_W, d), lambda bi, hg, i: (bi, hg, i, 0))
    ks = pl.BlockSpec((None, hp, s, dh), lambda bi, hg, i: (bi, hg, 0, 0))
    bs = pl.BlockSpec((hp, None, GRID_W, NA_BAND), lambda bi, hg, i: (hg, _na_geometry(i, nc, rows)[1], 0, 0))
    return qs, ks, bs


def _na_fwd(q, k, v, bias, n_ctx):
    b, h, s, dh = q.shape
    nc = n_ctx // GRID_W
    rows = (s - n_ctx) // GRID_W
    scale = dh ** -0.5
    hp = math.gcd(h, NA_HEADS_FWD)

    def body(q_ref, k_ref, v_ref, bias_ref, o_ref, lse_ref):
        rs, _ = _na_geometry(pl.program_id(2), nc, rows)
        start = pl.multiple_of(n_ctx + rs * GRID_W, GRID_W)
        for hd in range(hp):
            _, _, _, s_c, s_l = _na_scores(q_ref, k_ref, bias_ref, hd, n_ctx, start, scale)
            m = jnp.maximum(jnp.max(s_c, axis=-1, keepdims=True), jnp.max(s_l, axis=-1, keepdims=True))
            p_c = jnp.exp(s_c - m)
            p_l = jnp.exp(s_l - m)
            l = jnp.sum(p_c, axis=-1, keepdims=True) + jnp.sum(p_l, axis=-1, keepdims=True)
            o = jnp.dot(p_c.astype(BF16), v_ref[hd, 0:n_ctx, :].astype(BF16), preferred_element_type=F32)
            o = o + jnp.dot(p_l.astype(BF16), v_ref[hd, pl.ds(start, NA_BAND), :].astype(BF16),
                            preferred_element_type=F32)
            o_ref[hd] = o / l
            lse_ref[hd] = m + jnp.log(l)

    qs, ks, bs = _na_specs(hp, s, dh, nc, rows)
    return pl.pallas_call(
        body, out_shape=[jax.ShapeDtypeStruct((b, h, s, dh), F32), jax.ShapeDtypeStruct((b, h, s, 1), F32)],
        grid=(b, h // hp, s // GRID_W), in_specs=[qs(dh), ks, ks, bs], out_specs=[qs(dh), qs(1)],
        name=f"na_f_{s}", compiler_params=_params(),
    )(q, k, v, bias)


def _na_bwd(q, k, v, bias, o, lse, do, n_ctx):
    b, h, s, dh = q.shape
    nc = n_ctx // GRID_W
    rows = (s - n_ctx) // GRID_W
    scale = dh ** -0.5
    n_cls = NA_WIN_R + 1
    hp = math.gcd(h, NA_HEADS_BWD)

    def body(q_ref, k_ref, v_ref, bias_ref, o_ref, lse_ref, do_ref, dq_ref, dk_ref, dv_ref, db_ref):
        i = pl.program_id(2)
        rs, cls = _na_geometry(i, nc, rows)
        _, cls_prev = _na_geometry(i - 1, nc, rows)
        start = pl.multiple_of(n_ctx + rs * GRID_W, GRID_W)
        first = jnp.logical_or(i == 0, cls != cls_prev)

        @pl.when(i == 0)
        def _():
            dk_ref[...] = jnp.zeros_like(dk_ref)
            dv_ref[...] = jnp.zeros_like(dv_ref)

        for hd in range(hp):
            qv, kc, kb, s_c, s_l = _na_scores(q_ref, k_ref, bias_ref, hd, n_ctx, start, scale)
            lse_v = lse_ref[hd]
            p_c = jnp.exp(s_c - lse_v)
            p_l = jnp.exp(s_l - lse_v)
            dov = do_ref[hd]
            dob = dov.astype(BF16)
            delta = jnp.sum(dov * o_ref[hd], axis=-1, keepdims=True)
            vc = v_ref[hd, 0:n_ctx, :].astype(BF16)
            vb = v_ref[hd, pl.ds(start, NA_BAND), :].astype(BF16)
            ds_c = p_c * (lax.dot_general(dob, vc, _NT, preferred_element_type=F32) - delta)
            ds_l = p_l * (lax.dot_general(dob, vb, _NT, preferred_element_type=F32) - delta)
            dsc_b = ds_c.astype(BF16)
            dsl_b = ds_l.astype(BF16)
            dq_ref[hd] = (jnp.dot(dsc_b, kc, preferred_element_type=F32)
                          + jnp.dot(dsl_b, kb, preferred_element_type=F32)) * scale
            dk_ref[hd, 0:n_ctx, :] += lax.dot_general(dsc_b, qv, _TN, preferred_element_type=F32)
            dk_ref[hd, pl.ds(start, NA_BAND), :] += lax.dot_general(dsl_b, qv, _TN, preferred_element_type=F32)
            dv_ref[hd, 0:n_ctx, :] += lax.dot_general(p_c.astype(BF16), dob, _TN, preferred_element_type=F32)
            dv_ref[hd, pl.ds(start, NA_BAND), :] += lax.dot_general(p_l.astype(BF16), dob, _TN, preferred_element_type=F32)

            @pl.when(first)
            def _(hd=hd, ds_l=ds_l):
                db_ref[hd] = ds_l

            @pl.when(jnp.logical_not(first))
            def _(hd=hd, ds_l=ds_l):
                db_ref[hd] += ds_l

    qs, ks, bs = _na_specs(hp, s, dh, nc, rows)
    dbs = pl.BlockSpec((None, hp, None, GRID_W, NA_BAND),
                       lambda bi, hg, i: (bi, hg, _na_geometry(i, nc, rows)[1], 0, 0))
    return pl.pallas_call(
        body,
        out_shape=[jax.ShapeDtypeStruct((b, h, s, dh), F32), jax.ShapeDtypeStruct((b, h, s, dh), F32),
                   jax.ShapeDtypeStruct((b, h, s, dh), F32), jax.ShapeDtypeStruct((b, h, n_cls, GRID_W, NA_BAND), F32)],
        grid=(b, h // hp, s // GRID_W), in_specs=[qs(dh), ks, ks, bs, qs(dh), qs(1), qs(dh)],
        out_specs=[qs(dh), ks, ks, dbs], name=f"na_b_{s}", compiler_params=_params(),
    )(q, k, v, bias, o, lse, do)


@functools.partial(jax.custom_vjp, nondiff_argnums=(4,))
def na_attention(q, k, v, bias, n_ctx):
    return _na_fwd(q, k, v, bias, n_ctx)[0]


def _na_attention_fwd(q, k, v, bias, n_ctx):
    o, lse = _na_fwd(q, k, v, bias, n_ctx)
    return o, (q, k, v, bias, o, lse)


def _na_attention_bwd(n_ctx, res, do):
    q, k, v, bias, o, lse = res
    dq, dk, dv, db = _na_bwd(q, k, v, bias, o, lse, do, n_ctx)
    return dq, dk, dv, jnp.sum(db, axis=0)


na_attention.defvjp(_na_attention_fwd, _na_attention_bwd)


def _na_onehots():
    q = np.arange(GRID_W)[:, None]
    col = np.arange(GRID_W)[None, :]
    cs = np.clip(q - NA_WIN_C // 2, 0, GRID_W - NA_WIN_C)
    valid = (col >= cs) & (col < cs + NA_WIN_C)
    cidx = col - q + (NA_WIN_C - 1)
    n_b = 2 * NA_WIN_C - 1
    col_hot = np.zeros((LANE, GRID_W * GRID_W), np.float32)
    for qq in range(GRID_W):
        for cc in range(GRID_W):
            if valid[qq, cc]:
                col_hot[cidx[qq, cc], qq * GRID_W + cc] = 1.0
    row_hot = np.zeros((NA_WIN_R, NA_WIN_R, 2 * NA_WIN_R - 1), np.float32)
    for c in range(NA_WIN_R):
        for j in range(NA_WIN_R):
            row_hot[c, j, j - c + NA_WIN_R - 1] = 1.0
    mask = np.where(valid, 0.0, NEG).astype(np.float32)
    return col_hot, row_hot, mask, n_b


def na_bias_table(rpb):
    h = rpb.shape[0]
    col_hot, row_hot, mask, n_b = _na_onehots()
    t1 = jnp.einsum("cja,hab->hcjb", jnp.asarray(row_hot), rpb)
    t1 = jnp.pad(t1.reshape(h * NA_WIN_R * NA_WIN_R, n_b), ((0, 0), (0, LANE - n_b)))
    t2 = linear(t1, jnp.asarray(col_hot), True)
    t2 = t2.reshape(h, NA_WIN_R, NA_WIN_R, GRID_W, GRID_W) + jnp.asarray(mask)
    tab = jnp.transpose(t2, (0, 1, 3, 2, 4)).reshape(h, NA_WIN_R, GRID_W, NA_BAND)
    return jnp.concatenate([tab, jnp.full((h, 1, GRID_W, NA_BAND), NEG, F32)], axis=1)


def _first_step():
    return jnp.logical_and(pl.program_id(0) == 0, pl.program_id(1) == 0)


def _accum_out(ref, val, first):
    @pl.when(first)
    def _():
        ref[...] = val

    @pl.when(jnp.logical_not(first))
    def _():
        ref[...] += val


def _norm_head(xh, g):
    r = _rms(xh)
    yn = xh * r
    return yn * g, yn, r


def _norm_head_bwd(dy, yn, r, g):
    dg = jnp.sum(dy * yn, axis=0, keepdims=True)
    dyn = dy * g
    return r * (dyn - yn * jnp.mean(dyn * yn, axis=-1, keepdims=True)), dg


def _rope_signs(dh, start, rot_dim, n_heads):
    q = rot_dim // 4
    pos = np.arange(dh)
    quarter = (pos - start) // q
    inr = pos >= start
    sg = np.zeros((8, n_heads * dh), np.float32)
    sg[0] = np.tile(np.where(inr & (quarter % 2 == 0), -1.0, 0.0), n_heads)
    sg[1] = np.tile(np.where(inr & (quarter % 2 == 1), 1.0, 0.0), n_heads)
    return sg


def _rope_full(y, cos, sin, sg, q):
    w = y.shape[-1]
    rot = sg[0:1] * pltpu.roll(y, w - q, 1) + sg[1:2] * pltpu.roll(y, q, 1)
    return y * cos + rot * sin


def _rope_full_t(dy, cos, sin, sg, q):
    w = dy.shape[-1]
    z = dy * sin
    return dy * cos - sg[1:2] * pltpu.roll(z, q, 1) - sg[0:1] * pltpu.roll(z, w - q, 1)


def _hnr_call(x, g, cos, sin, sg, n_heads, q, dy=None):
    b, s, w = x.shape
    dh = w // n_heads
    ts = _pick(s, (256, 128, 64))
    rope = cos is not None

    def body(*refs):
        refs = list(refs)
        x_ref, g_ref = refs[0], refs[1]
        k = 2
        if rope:
            cos_ref, sin_ref, sg_ref = refs[2], refs[3], refs[4]
            k = 5
        gv = g_ref[...]
        if dy is None:
            o_ref = refs[k]
            for h in range(n_heads):
                sl = slice(h * dh, (h + 1) * dh)
                o_ref[:, sl] = _norm_head(x_ref[:, sl], gv)[0]
            if rope:
                o_ref[...] = _rope_full(o_ref[...], cos_ref[...], sin_ref[...], sg_ref[...], q)
            return
        dy_ref, dx_ref, dg_ref = refs[k], refs[k + 1], refs[k + 2]
        src = dy_ref
        if rope:
            dx_ref[...] = _rope_full_t(dy_ref[...], cos_ref[...], sin_ref[...], sg_ref[...], q)
            src = dx_ref
        dg = jnp.zeros((1, dh), F32)
        for h in range(n_heads):
            sl = slice(h * dh, (h + 1) * dh)
            _, yn, r = _norm_head(x_ref[:, sl], gv)
            dxh, dgh = _norm_head_bwd(src[:, sl], yn, r, gv)
            dx_ref[:, sl] = dxh
            dg = dg + dgh
        _accum_out(dg_ref, dg, _first_step())

    row = pl.BlockSpec((None, ts, w), lambda bi, i: (bi, i, 0))
    whole = lambda a: pl.BlockSpec(a.shape, lambda bi, i: (0, 0))
    ins, specs = [x, g], [row, whole(g)]
    if rope:
        ins += [cos, sin, sg]
        specs += [pl.BlockSpec((ts, w), lambda bi, i: (i, 0)), pl.BlockSpec((ts, w), lambda bi, i: (i, 0)), whole(sg)]
    if dy is None:
        out_shape, out_specs = jax.ShapeDtypeStruct(x.shape, F32), row
    else:
        ins.append(dy)
        specs.append(row)
        out_shape = [jax.ShapeDtypeStruct(x.shape, F32), jax.ShapeDtypeStruct(g.shape, F32)]
        out_specs = [row, whole(g)]
    return pl.pallas_call(
        body, out_shape=out_shape, grid=(b, s // ts), in_specs=specs, out_specs=out_specs,
        name=f"hnr_{'b' if dy is not None else 'f'}_{n_heads}x{dh}_{int(rope)}", compiler_params=_params(),
    )(*ins)


@functools.partial(jax.custom_vjp, nondiff_argnums=(5, 6))
def head_norm_rope(x, g, cos, sin, sg, n_heads, q):
    return _hnr_call(x, g, cos, sin, sg, n_heads, q)


def _head_norm_rope_fwd(x, g, cos, sin, sg, n_heads, q):
    return _hnr_call(x, g, cos, sin, sg, n_heads, q), (x, g, cos, sin, sg)


def _head_norm_rope_bwd(n_heads, q, res, dy):
    x, g, cos, sin, sg = res
    dx, dg = _hnr_call(x, g, cos, sin, sg, n_heads, q, dy=dy)
    zero = lambda t: None if t is None else jnp.zeros_like(t)
    return dx, dg, zero(cos), zero(sin), zero(sg)


head_norm_rope.defvjp(_head_norm_rope_fwd, _head_norm_rope_bwd)


def _mla_k_call(kv, kr, g, cos, sin, sg, dkn=None):
    b, s, _ = kv.shape
    ts = _pick(s, (256, 128, 64))
    hw = MLA_NOPE + MLA_V
    kn_w = MLA_HEADS * MLA_QK
    q = MLA_ROPE // 4

    def body(kv_ref, kr_ref, g_ref, cos_ref, sin_ref, sg_ref, *rest):
        gv = g_ref[...]
        krv = kr_ref[...]
        if dkn is None:
            (o_ref,) = rest
            for h in range(MLA_HEADS):
                kh = jnp.concatenate([kv_ref[:, h * hw:h * hw + MLA_NOPE], krv], axis=-1)
                o_ref[:, h * MLA_QK:(h + 1) * MLA_QK] = _norm_head(kh, gv)[0]
            o_ref[...] = _rope_full(o_ref[...], cos_ref[...], sin_ref[...], sg_ref[...], q)
            return
        dkn_ref, dkv_ref, dkr_ref, dg_ref, dy_ref = rest
        dy_ref[...] = _rope_full_t(dkn_ref[...], cos_ref[...], sin_ref[...], sg_ref[...], q)
        dg = jnp.zeros((1, MLA_QK), F32)
        dkr = jnp.zeros((ts, MLA_ROPE), F32)
        for h in range(MLA_HEADS):
            kh = jnp.concatenate([kv_ref[:, h * hw:h * hw + MLA_NOPE], krv], axis=-1)
            _, yn, r = _norm_head(kh, gv)
            dxh, dgh = _norm_head_bwd(dy_ref[:, h * MLA_QK:(h + 1) * MLA_QK], yn, r, gv)
            dkv_ref[:, h * hw:h * hw + MLA_NOPE] = dxh[:, :MLA_NOPE]
            dkv_ref[:, h * hw + MLA_NOPE:(h + 1) * hw] = jnp.zeros((ts, MLA_V), F32)
            dkr = dkr + dxh[:, MLA_NOPE:]
            dg = dg + dgh
        dkr_ref[...] = dkr
        _accum_out(dg_ref, dg, _first_step())

    row = lambda w: pl.BlockSpec((None, ts, w), lambda bi, i: (bi, i, 0))
    tab = pl.BlockSpec((ts, kn_w), lambda bi, i: (i, 0))
    whole = lambda a: pl.BlockSpec(a.shape, lambda bi, i: (0, 0))
    ins = [kv, kr, g, cos, sin, sg]
    specs = [row(kv.shape[2]), row(MLA_ROPE), whole(g), tab, tab, whole(sg)]
    scratch = []
    if dkn is None:
        out_shape, out_specs = jax.ShapeDtypeStruct((b, s, kn_w), F32), row(kn_w)
    else:
        ins.append(dkn)
        specs.append(row(kn_w))
        out_shape = [jax.ShapeDtypeStruct(kv.shape, F32), jax.ShapeDtypeStruct(kr.shape, F32),
                     jax.ShapeDtypeStruct(g.shape, F32)]
        out_specs = [row(kv.shape[2]), row(MLA_ROPE), whole(g)]
        scratch = [pltpu.VMEM((ts, kn_w), F32)]
    return pl.pallas_call(
        body, out_shape=out_shape, grid=(b, s // ts), in_specs=specs, out_specs=out_specs, scratch_shapes=scratch,
        name=f"mla_k_{'b' if dkn is not None else 'f'}", compiler_params=_params(),
    )(*ins)


@jax.custom_vjp
def mla_k_prep(kv, kr, g, cos, sin, sg):
    return _mla_k_call(kv, kr, g, cos, sin, sg)


def _mla_k_prep_fwd(kv, kr, g, cos, sin, sg):
    return _mla_k_call(kv, kr, g, cos, sin, sg), (kv, kr, g, cos, sin, sg)


def _mla_k_prep_bwd(res, dkn):
    kv, kr, g, cos, sin, sg = res
    dkv, dkr, dg = _mla_k_call(kv, kr, g, cos, sin, sg, dkn=dkn)
    return dkv, dkr, dg, jnp.zeros_like(cos), jnp.zeros_like(sin), jnp.zeros_like(sg)


mla_k_prep.defvjp(_mla_k_prep_fwd, _mla_k_prep_bwd)


class _HeadLayout:
    def __init__(self, groups, dq, dv, q_off, k_off, v_off, o_off, wq, wk, wv, wo, scale):
        self.groups, self.dq, self.dv, self.scale = groups, dq, dv, scale
        self.q_off, self.k_off, self.v_off, self.o_off = q_off, k_off, v_off, o_off
        self.wq, self.wk, self.wv, self.wo = wq, wk, wv, wo
        self.n_h = len(q_off)


def _gqa_layout():
    rep = GQA_Q_HEADS // GQA_KV_HEADS
    n_h = GQA_Q_HEADS // 2
    return _HeadLayout(2, HEAD_DIM, HEAD_DIM, [h * HEAD_DIM for h in range(n_h)], [(h // rep) * HEAD_DIM for h in range(n_h)],
                       [(h // rep) * HEAD_DIM for h in range(n_h)], [h * HEAD_DIM for h in range(n_h)],
                       n_h * HEAD_DIM, (n_h // rep) * HEAD_DIM, (n_h // rep) * HEAD_DIM, n_h * HEAD_DIM, HEAD_DIM ** -0.5)


def _mla_layout():
    n_h = MLA_HEADS // 2
    hw = MLA_NOPE + MLA_V
    return _HeadLayout(2, MLA_QK, MLA_V, [h * MLA_QK for h in range(n_h)], [h * MLA_QK for h in range(n_h)],
                       [h * hw + MLA_NOPE for h in range(n_h)], [h * MLA_V for h in range(n_h)],
                       n_h * MLA_QK, n_h * MLA_QK, n_h * hw, n_h * MLA_V, MLA_QK ** -0.5)


def _attn_tm_fwd(q, k, v, lay, n_ctx):
    b, s, _ = q.shape
    tq = min(256, n_ctx)
    nc = n_ctx // tq

    def body(q_ref, k_ref, v_ref, o_ref, lse_ref):
        def run(n_keys):
            for h in range(lay.n_h):
                qo, ko, vo, oo = lay.q_off[h], lay.k_off[h], lay.v_off[h], lay.o_off[h]
                qv = (q_ref[:, qo:qo + lay.dq] * lay.scale).astype(BF16)
                sc = lax.dot_general(qv, k_ref[0:n_keys, ko:ko + lay.dq].astype(BF16), _NT, preferred_element_type=F32)
                m = jnp.max(sc, axis=-1, keepdims=True)
                p = jnp.exp(sc - m)
                l = jnp.sum(p, axis=-1, keepdims=True)
                o = jnp.dot(p.astype(BF16), v_ref[0:n_keys, vo:vo + lay.dv].astype(BF16), preferred_element_type=F32)
                o_ref[:, oo:oo + lay.dv] = o / l
                lse_ref[:, h:h + 1] = m + jnp.log(l)

        pl.when(pl.program_id(2) < nc)(lambda: run(n_ctx))
        pl.when(pl.program_id(2) >= nc)(lambda: run(s))

    return pl.pallas_call(
        body, out_shape=[jax.ShapeDtypeStruct((b, s, lay.groups * lay.wo), F32),
                         jax.ShapeDtypeStruct((b, lay.groups, s, lay.n_h), F32)],
        grid=(b, lay.groups, s // tq),
        in_specs=[pl.BlockSpec((None, tq, lay.wq), lambda bi, g, i: (bi, i, g)),
                  pl.BlockSpec((None, s, lay.wk), lambda bi, g, i: (bi, 0, g)),
                  pl.BlockSpec((None, s, lay.wv), lambda bi, g, i: (bi, 0, g))],
        out_specs=[pl.BlockSpec((None, tq, lay.wo), lambda bi, g, i: (bi, i, g)),
                   pl.BlockSpec((None, None, tq, lay.n_h), lambda bi, g, i: (bi, g, i, 0))],
        name=f"attn_tm_f_{lay.dq}", compiler_params=_params(),
    )(q, k, v)


def _attn_tm_delta(o, do, lay):
    b, s, _ = o.shape
    ts = _pick(s, (256, 128, 64))

    def body(o_ref, do_ref, d_ref):
        for h in range(lay.n_h):
            oo = lay.o_off[h]
            d_ref[:, h:h + 1] = jnp.sum(o_ref[:, oo:oo + lay.dv] * do_ref[:, oo:oo + lay.dv], axis=-1, keepdims=True)

    blk = pl.BlockSpec((None, ts, lay.wo), lambda bi, g, i: (bi, i, g))
    return pl.pallas_call(
        body, out_shape=jax.ShapeDtypeStruct((b, lay.groups, s, lay.n_h), F32), grid=(b, lay.groups, s // ts),
        in_specs=[blk, blk], out_specs=pl.BlockSpec((None, None, ts, lay.n_h), lambda bi, g, i: (bi, g, i, 0)),
        name=f"attn_tm_delta_{lay.dq}", compiler_params=_params(),
    )(o, do)


def _attn_tm_bwd(q, k, v, lse, o, do, lay, n_ctx):
    b, s, _ = q.shape
    tk = min(256, n_ctx)
    nc = n_ctx // tk

    def body(q_ref, k_ref, v_ref, lse_ref, o_ref, do_ref, dq_ref, dk_ref, dv_ref, delta_ref):
        @pl.when(pl.program_id(2) == 0)
        def _():
            dq_ref[...] = jnp.zeros_like(dq_ref)
            for h in range(lay.n_h):
                oo = lay.o_off[h]
                delta_ref[:, h:h + 1] = jnp.sum(o_ref[:, oo:oo + lay.dv] * do_ref[:, oo:oo + lay.dv], axis=-1,
                                                keepdims=True)

        def run(r0):
            dk_acc, dv_acc = {}, {}
            for h in range(lay.n_h):
                qo, ko, vo, oo = lay.q_off[h], lay.k_off[h], lay.v_off[h], lay.o_off[h]
                kh = k_ref[:, ko:ko + lay.dq].astype(BF16)
                vh = v_ref[:, vo:vo + lay.dv].astype(BF16)
                qv = (q_ref[r0:s, qo:qo + lay.dq] * lay.scale).astype(BF16)
                dob = do_ref[r0:s, oo:oo + lay.dv].astype(BF16)
                sc = lax.dot_general(qv, kh, _NT, preferred_element_type=F32)
                p = jnp.exp(sc - lse_ref[r0:s, h:h + 1])
                dvh = lax.dot_general(p.astype(BF16), dob, _TN, preferred_element_type=F32)
                dp = lax.dot_general(dob, vh, _NT, preferred_element_type=F32)
                dsb = (p * (dp - delta_ref[r0:s, h:h + 1])).astype(BF16)
                dkh = lax.dot_general(dsb, qv, _TN, preferred_element_type=F32)
                dq_ref[r0:s, qo:qo + lay.dq] += jnp.dot(dsb, kh, preferred_element_type=F32) * lay.scale
                dk_acc[ko] = dkh if ko not in dk_acc else dk_acc[ko] + dkh
                dv_acc[vo] = dvh if vo not in dv_acc else dv_acc[vo] + dvh
            if len(dv_acc) * lay.dv != lay.wv:
                dv_ref[...] = jnp.zeros_like(dv_ref)
            for ko, val in dk_acc.items():
                dk_ref[:, ko:ko + lay.dq] = val
            for vo, val in dv_acc.items():
                dv_ref[:, vo:vo + lay.dv] = val

        pl.when(pl.program_id(2) < nc)(lambda: run(0))
        pl.when(pl.program_id(2) >= nc)(lambda: run(n_ctx))

    full = lambda w: pl.BlockSpec((None, s, w), lambda bi, g, j: (bi, 0, g))
    blk = lambda w: pl.BlockSpec((None, tk, w), lambda bi, g, j: (bi, j, g))
    stat = pl.BlockSpec((None, None, s, lay.n_h), lambda bi, g, j: (bi, g, 0, 0))
    return pl.pallas_call(
        body, out_shape=[jax.ShapeDtypeStruct(q.shape, F32), jax.ShapeDtypeStruct(k.shape, F32),
                         jax.ShapeDtypeStruct(v.shape, F32)],
        grid=(b, lay.groups, s // tk),
        in_specs=[full(lay.wq), blk(lay.wk), blk(lay.wv), stat, full(lay.wo), full(lay.wo)],
        out_specs=[full(lay.wq), blk(lay.wk), blk(lay.wv)],
        scratch_shapes=[pltpu.VMEM((s, lay.n_h), F32)],
        name=f"attn_tm_b_{lay.dq}", compiler_params=_params(),
    )(q, k, v, lse, o, do)


def _make_attention_tm(lay):
    @functools.partial(jax.custom_vjp, nondiff_argnums=(3,))
    def op(q, k, v, n_ctx):
        return _attn_tm_fwd(q, k, v, lay, n_ctx)[0]

    def fwd(q, k, v, n_ctx):
        o, lse = _attn_tm_fwd(q, k, v, lay, n_ctx)
        return o, (q, k, v, o, lse)

    def bwd(n_ctx, res, do):
        q, k, v, o, lse = res
        return _attn_tm_bwd(q, k, v, lse, o, do, lay, n_ctx)

    op.defvjp(fwd, bwd)
    return op


gqa_attention = _make_attention_tm(_gqa_layout())
mla_attention = _make_attention_tm(_mla_layout())

NA_GROUPS = 2


def _na_tm_specs(s, nc, rows):
    hg = NA_HEADS // NA_GROUPS
    w = hg * HEAD_DIM
    qs = pl.BlockSpec((None, GRID_W, w), lambda bi, g, i: (bi, i, g))
    ks = pl.BlockSpec((None, s, w), lambda bi, g, i: (bi, 0, g))
    bs = pl.BlockSpec((hg, None, GRID_W, NA_BAND), lambda bi, g, i: (g, _na_geometry(i, nc, rows)[1], 0, 0))
    ls = pl.BlockSpec((None, None, GRID_W, hg), lambda bi, g, i: (bi, g, i, 0))
    return hg, w, qs, ks, bs, ls


def _na_tm_scores(q_ref, k_ref, bias_ref, hd, n_ctx, start, scale):
    sl = slice(hd * HEAD_DIM, (hd + 1) * HEAD_DIM)
    qv = (q_ref[:, sl] * scale).astype(BF16)
    kc = k_ref[0:n_ctx, sl].astype(BF16)
    kb = k_ref[pl.ds(start, NA_BAND), sl].astype(BF16)
    s_c = lax.dot_general(qv, kc, _NT, preferred_element_type=F32)
    s_l = lax.dot_general(qv, kb, _NT, preferred_element_type=F32) + bias_ref[hd]
    return sl, qv, kc, kb, s_c, s_l


def _na_tm_fwd(q, k, v, bias, n_ctx):
    b, s, _ = q.shape
    nc = n_ctx // GRID_W
    rows = (s - n_ctx) // GRID_W
    scale = HEAD_DIM ** -0.5
    hg, w, qs, ks, bs, ls = _na_tm_specs(s, nc, rows)

    def body(q_ref, k_ref, v_ref, bias_ref, o_ref, lse_ref):
        rs, _ = _na_geometry(pl.program_id(2), nc, rows)
        start = pl.multiple_of(n_ctx + rs * GRID_W, GRID_W)
        for hd in range(hg):
            sl, _, _, _, s_c, s_l = _na_tm_scores(q_ref, k_ref, bias_ref, hd, n_ctx, start, scale)
            m = jnp.maximum(jnp.max(s_c, axis=-1, keepdims=True), jnp.max(s_l, axis=-1, keepdims=True))
            p_c = jnp.exp(s_c - m)
            p_l = jnp.exp(s_l - m)
            l = jnp.sum(p_c, axis=-1, keepdims=True) + jnp.sum(p_l, axis=-1, keepdims=True)
            o = jnp.dot(p_c.astype(BF16), v_ref[0:n_ctx, sl].astype(BF16), preferred_element_type=F32)
            o = o + jnp.dot(p_l.astype(BF16), v_ref[pl.ds(start, NA_BAND), sl].astype(BF16), preferred_element_type=F32)
            o_ref[:, sl] = o / l
            lse_ref[:, hd:hd + 1] = m + jnp.log(l)

    return pl.pallas_call(
        body, out_shape=[jax.ShapeDtypeStruct(q.shape, F32), jax.ShapeDtypeStruct((b, NA_GROUPS, s, hg), F32)],
        grid=(b, NA_GROUPS, s // GRID_W), in_specs=[qs, ks, ks, bs], out_specs=[qs, ls],
        name=f"na_tm_f_{s}", compiler_params=_params(),
    )(q, k, v, bias)


def _na_tm_bwd(q, k, v, bias, o, lse, do, n_ctx):
    b, s, _ = q.shape
    nc = n_ctx // GRID_W
    rows = (s - n_ctx) // GRID_W
    scale = HEAD_DIM ** -0.5
    n_cls = NA_WIN_R + 1
    hg, w, qs, ks, bs, ls = _na_tm_specs(s, nc, rows)

    def body(q_ref, k_ref, v_ref, bias_ref, o_ref, lse_ref, do_ref, dq_ref, dk_ref, dv_ref, db_ref):
        i = pl.program_id(2)
        rs, cls = _na_geometry(i, nc, rows)
        _, cls_prev = _na_geometry(i - 1, nc, rows)
        start = pl.multiple_of(n_ctx + rs * GRID_W, GRID_W)
        first = jnp.logical_or(i == 0, cls != cls_prev)

        @pl.when(i == 0)
        def _():
            dk_ref[...] = jnp.zeros_like(dk_ref)
            dv_ref[...] = jnp.zeros_like(dv_ref)

        @pl.when(first)
        def _():
            db_ref[...] = jnp.zeros_like(db_ref)

        for hd in range(hg):
            sl, qv, kc, kb, s_c, s_l = _na_tm_scores(q_ref, k_ref, bias_ref, hd, n_ctx, start, scale)
            lse_v = lse_ref[:, hd:hd + 1]
            p_c = jnp.exp(s_c - lse_v)
            p_l = jnp.exp(s_l - lse_v)
            dov = do_ref[:, sl]
            dob = dov.astype(BF16)
            delta = jnp.sum(dov * o_ref[:, sl], axis=-1, keepdims=True)
            vc = v_ref[0:n_ctx, sl].astype(BF16)
            vb = v_ref[pl.ds(start, NA_BAND), sl].astype(BF16)
            ds_c = p_c * (lax.dot_general(dob, vc, _NT, preferred_element_type=F32) - delta)
            ds_l = p_l * (lax.dot_general(dob, vb, _NT, preferred_element_type=F32) - delta)
            dsc_b = ds_c.astype(BF16)
            dsl_b = ds_l.astype(BF16)
            dq_ref[:, sl] = (jnp.dot(dsc_b, kc, preferred_element_type=F32)
                             + jnp.dot(dsl_b, kb, preferred_element_type=F32)) * scale
            dk_ref[0:n_ctx, sl] += lax.dot_general(dsc_b, qv, _TN, preferred_element_type=F32)
            dk_ref[pl.ds(start, NA_BAND), sl] += lax.dot_general(dsl_b, qv, _TN, preferred_element_type=F32)
            dv_ref[0:n_ctx, sl] += lax.dot_general(p_c.astype(BF16), dob, _TN, preferred_element_type=F32)
            dv_ref[pl.ds(start, NA_BAND), sl] += lax.dot_general(p_l.astype(BF16), dob, _TN, preferred_element_type=F32)
            db_ref[hd] += ds_l

    dbs = pl.BlockSpec((None, hg, None, GRID_W, NA_BAND), lambda bi, g, i: (bi, g, _na_geometry(i, nc, rows)[1], 0, 0))
    return pl.pallas_call(
        body,
        out_shape=[jax.ShapeDtypeStruct(q.shape, F32), jax.ShapeDtypeStruct(q.shape, F32), jax.ShapeDtypeStruct(q.shape, F32),
                   jax.ShapeDtypeStruct((b, NA_HEADS, n_cls, GRID_W, NA_BAND), F32)],
        grid=(b, NA_GROUPS, s // GRID_W), in_specs=[qs, ks, ks, bs, qs, ls, qs], out_specs=[qs, ks, ks, dbs],
        name=f"na_tm_b_{s}", compiler_params=_params(),
    )(q, k, v, bias, o, lse, do)


@functools.partial(jax.custom_vjp, nondiff_argnums=(4,))
def na_attention_tm(q, k, v, bias, n_ctx):
    return _na_tm_fwd(q, k, v, bias, n_ctx)[0]


def _na_attention_tm_fwd(q, k, v, bias, n_ctx):
    o, lse = _na_tm_fwd(q, k, v, bias, n_ctx)
    return o, (q, k, v, bias, o, lse)


def _na_attention_tm_bwd(n_ctx, res, do):
    q, k, v, bias, o, lse = res
    dq, dk, dv, db = _na_tm_bwd(q, k, v, bias, o, lse, do, n_ctx)
    return dq, dk, dv, _sum_rows(db.reshape(db.shape[0], -1, NA_BAND), db.shape[0]).reshape(db.shape[1:])


na_attention_tm.defvjp(_na_attention_tm_fwd, _na_attention_tm_bwd)


def _cmul(ar, ai, br, bi):
    return ar * br - ai * bi, ar * bi + ai * br


def _s5_chunk(n_ctx):
    return min(256, n_ctx)


def _s5_powers(a_re, a_im, t_len):
    a_re, a_im = lax.stop_gradient(a_re), lax.stop_gradient(a_im)
    mag = jnp.sqrt(a_re * a_re + a_im * a_im)
    th = jnp.arctan2(a_im, a_re)
    t = jnp.arange(t_len + 1, dtype=F32)[:, None]
    pm = jnp.where(t == 0, 1.0, jnp.exp(t * jnp.log(jnp.maximum(mag, 1e-37))) * (mag > 0))
    return jnp.stack([pm * jnp.cos(t * th), pm * jnp.sin(t * th)])


def _s5_tables(pw, t_len, rev, conj=False):
    if conj:
        pw = pw * jnp.asarray([1.0, -1.0], F32)[:, None, None]
    steps = jnp.concatenate([pw[:, min(2 ** i, t_len)][:, None] for i in range(8)], axis=1)
    tile = pw[:, 1:9]
    a8k = pw[:, 0:t_len:8]
    if rev:
        tile, a8k = tile[:, ::-1], a8k[:, ::-1]
    misc = jnp.concatenate([pw[:, t_len:t_len + 1], jnp.zeros((2, 7, pw.shape[-1]), F32)], axis=1)
    return jnp.concatenate([steps, tile, misc, a8k], axis=1)


def _scan_chunk(x_re, x_im, tab_ref, hin_re, hin_im, rev, t_len, xs_ref, es_ref):
    outs = [_scan_slab(x_re[:, k:k + LANE], x_im[:, k:k + LANE], tab_ref, hin_re[:, k:k + LANE], hin_im[:, k:k + LANE],
                       rev, t_len, xs_ref, es_ref, k) for k in range(0, x_re.shape[-1], LANE)]
    return tuple(jnp.concatenate([o[t] for o in outs], axis=-1) for t in range(4))


def _scan_slab(x_re, x_im, tab_ref, hin_re, hin_im, rev, t_len, xs_ref, es_ref, k0):
    lanes = LANE
    n2 = t_len // 8
    tab_ref = tab_ref.at[:, :, k0:k0 + LANE]
    rin = lax.broadcasted_iota(jnp.int32, (t_len, lanes), 0) & 7
    for li, sh in enumerate((1, 2, 4)):
        m_re, m_im = tab_ref[0, li:li + 1, :], tab_ref[1, li:li + 1, :]
        amt = sh if not rev else t_len - sh
        c_re, c_im = _cmul(m_re, m_im, pltpu.roll(x_re, amt, 0), pltpu.roll(x_im, amt, 0))
        ok = (rin >= sh) if not rev else (rin < 8 - sh)
        x_re = x_re + jnp.where(ok, c_re, 0.0)
        x_im = x_im + jnp.where(ok, c_im, 0.0)
    xr_ref, xi_ref = xs_ref
    xr_ref[...] = x_re
    xi_ref[...] = x_im
    off = 0 if rev else 7
    e_re = xr_ref[pl.ds(off, n2, stride=8), :]
    e_im = xi_ref[pl.ds(off, n2, stride=8), :]
    row2 = lax.broadcasted_iota(jnp.int32, (n2, lanes), 0)
    sh, li = 1, 3
    while sh < n2:
        m_re, m_im = tab_ref[0, li:li + 1, :], tab_ref[1, li:li + 1, :]
        amt = sh if not rev else n2 - sh
        c_re, c_im = _cmul(m_re, m_im, pltpu.roll(e_re, amt, 0), pltpu.roll(e_im, amt, 0))
        ok = (row2 >= sh) if not rev else (row2 < n2 - sh)
        e_re = e_re + jnp.where(ok, c_re, 0.0)
        e_im = e_im + jnp.where(ok, c_im, 0.0)
        sh, li = sh * 2, li + 1
    es_ref[0] = e_re
    es_ref[1] = e_im
    last = 0 if rev else n2 - 1
    t_re, t_im = _cmul(tab_ref[0, 16:17, :], tab_ref[1, 16:17, :], hin_re, hin_im)
    hout_re = es_ref[0, last:last + 1, :] + t_re
    hout_im = es_ref[1, last:last + 1, :] + t_im
    amt = 1 if not rev else n2 - 1
    ok = (row2 >= 1) if not rev else (row2 < n2 - 1)
    k_re, k_im = _cmul(tab_ref[0, 24:24 + n2, :], tab_ref[1, 24:24 + n2, :], hin_re, hin_im)
    c_re = jnp.where(ok, pltpu.roll(e_re, amt, 0), 0.0) + k_re
    c_im = jnp.where(ok, pltpu.roll(e_im, amt, 0), 0.0) + k_im
    tp_re, tp_im = tab_ref[0, 8:16, :][None], tab_ref[1, 8:16, :][None]
    add_re, add_im = _cmul(tp_re, tp_im, c_re[:, None, :], c_im[:, None, :])
    h_re = xr_ref[...] + add_re.reshape(t_len, lanes)
    h_im = xi_ref[...] + add_im.reshape(t_len, lanes)
    return h_re, h_im, hout_re, hout_im


def _s5_order(j, n_chunks, nc, rev):
    if not rev:
        return j
    return jnp.where(j < nc, nc - 1 - j, n_chunks - 1 - (j - nc))


def _s5_fwd(u, tab, b_bd, c_bd, n_ctx, rev):
    b, s, w = u.shape
    lanes = b_bd.shape[-1]
    t_len = _s5_chunk(n_ctx)
    n_chunks, nc = s // t_len, n_ctx // t_len

    def body(u_ref, tab_ref, b_ref, c_ref, y_ref, h_ref, hin_ref, carry_ref, xr_ref, xi_ref, es_ref):
        xs_ref = (xr_ref, xi_ref)

        @pl.when(pl.program_id(1) == 0)
        def _():
            carry_ref[...] = jnp.zeros_like(carry_ref)

        ub = u_ref[...].astype(BF16)
        x_re = jnp.dot(ub, b_ref[0].astype(BF16), preferred_element_type=F32)
        x_im = jnp.dot(ub, b_ref[1].astype(BF16), preferred_element_type=F32)
        hin_re, hin_im = carry_ref[0, 0:1, :], carry_ref[1, 0:1, :]
        hin_ref[...] = carry_ref[...]
        h_re, h_im, ho_re, ho_im = _scan_chunk(x_re, x_im, tab_ref, hin_re, hin_im, rev, t_len, xs_ref, es_ref)
        carry_ref[0] = jnp.broadcast_to(ho_re, (8, lanes))
        carry_ref[1] = jnp.broadcast_to(ho_im, (8, lanes))
        h_ref[0] = h_re
        h_ref[1] = h_im
        y_ref[...] = (jnp.dot(h_re.astype(BF16), c_ref[0].astype(BF16), preferred_element_type=F32)
                      - jnp.dot(h_im.astype(BF16), c_ref[1].astype(BF16), preferred_element_type=F32))

    order = lambda j: _s5_order(j, n_chunks, nc, rev)
    whole = lambda arr: pl.BlockSpec(arr.shape, lambda bi, j: (0,) * arr.ndim)
    return pl.pallas_call(
        body,
        out_shape=[jax.ShapeDtypeStruct((b, s, w), F32), jax.ShapeDtypeStruct((2, b, s, lanes), F32),
                   jax.ShapeDtypeStruct((2, b, n_chunks, 8, lanes), F32)],
        grid=(b, n_chunks),
        in_specs=[pl.BlockSpec((None, t_len, w), lambda bi, j: (bi, order(j), 0)), whole(tab), whole(b_bd), whole(c_bd)],
        out_specs=[pl.BlockSpec((None, t_len, w), lambda bi, j: (bi, order(j), 0)),
                   pl.BlockSpec((2, None, t_len, lanes), lambda bi, j: (0, bi, order(j), 0)),
                   pl.BlockSpec((2, None, None, 8, lanes), lambda bi, j: (0, bi, order(j), 0, 0))],
        scratch_shapes=[pltpu.VMEM((2, 8, lanes), F32), pltpu.VMEM((t_len, LANE), F32), pltpu.VMEM((t_len, LANE), F32),
                        pltpu.VMEM((2, t_len // 8, LANE), F32)],
        name=f"s5_f_{s}_{int(rev)}", compiler_params=_params(),
    )(u, tab, b_bd, c_bd)


def _s5_bwd(u, tab_adj, b_bd, c_bd, h, hin, dy, n_ctx, rev):
    b, s, w = u.shape
    lanes = b_bd.shape[-1]
    t_len = _s5_chunk(n_ctx)
    n_chunks, nc = s // t_len, n_ctx // t_len
    arev = not rev

    def body(u_ref, tab_ref, b_ref, c_ref, h_ref, hin_ref, dy_ref, du_ref, db_ref, dc_ref, da_ref,
             carry_ref, xr_ref, xi_ref, es_ref):
        xs_ref = (xr_ref, xi_ref)
        first = jnp.logical_and(pl.program_id(0) == 0, pl.program_id(1) == 0)

        @pl.when(pl.program_id(1) == 0)
        def _():
            carry_ref[...] = jnp.zeros_like(carry_ref)

        dyv = dy_ref[...]
        dyb = dyv.astype(BF16)
        dn = (((1,), (1,)), ((), ()))
        dt = (((0,), (0,)), ((), ()))
        x_re = lax.dot_general(dyb, c_ref[0].astype(BF16), dn, preferred_element_type=F32)
        x_im = -lax.dot_general(dyb, c_ref[1].astype(BF16), dn, preferred_element_type=F32)
        g_re, g_im, go_re, go_im = _scan_chunk(x_re, x_im, tab_ref, carry_ref[0, 0:1, :], carry_ref[1, 0:1, :],
                                               arev, t_len, xs_ref, es_ref)
        carry_ref[0] = jnp.broadcast_to(go_re, (8, lanes))
        carry_ref[1] = jnp.broadcast_to(go_im, (8, lanes))
        h_re, h_im = h_ref[0], h_ref[1]
        gb_re, gb_im = g_re.astype(BF16), g_im.astype(BF16)
        du_ref[...] = (lax.dot_general(gb_re, b_ref[0].astype(BF16), dn, preferred_element_type=F32)
                       + lax.dot_general(gb_im, b_ref[1].astype(BF16), dn, preferred_element_type=F32))
        ub = u_ref[...].astype(BF16)
        db_re = lax.dot_general(ub, gb_re, dt, preferred_element_type=F32)
        db_im = lax.dot_general(ub, gb_im, dt, preferred_element_type=F32)
        dc_re = lax.dot_general(h_re.astype(BF16), dyb, dt, preferred_element_type=F32)
        dc_im = -lax.dot_general(h_im.astype(BF16), dyb, dt, preferred_element_type=F32)
        row = lax.broadcasted_iota(jnp.int32, (t_len, lanes), 0)
        amt = 1 if not rev else t_len - 1
        edge = (row == 0) if not rev else (row == t_len - 1)
        hp_re = jnp.where(edge, hin_ref[0, 0:1, :], pltpu.roll(h_re, amt, 0))
        hp_im = jnp.where(edge, hin_ref[1, 0:1, :], pltpu.roll(h_im, amt, 0))
        da_re = jnp.sum(g_re * hp_re + g_im * hp_im, axis=0, keepdims=True)
        da_im = jnp.sum(g_im * hp_re - g_re * hp_im, axis=0, keepdims=True)

        @pl.when(first)
        def _():
            db_ref[0], db_ref[1] = db_re, db_im
            dc_ref[0], dc_ref[1] = dc_re, dc_im
            da_ref[0] = jnp.broadcast_to(da_re, (8, lanes))
            da_ref[1] = jnp.broadcast_to(da_im, (8, lanes))

        @pl.when(jnp.logical_not(first))
        def _():
            db_ref[0] += db_re
            db_ref[1] += db_im
            dc_ref[0] += dc_re
            dc_ref[1] += dc_im
            da_ref[0] += jnp.broadcast_to(da_re, (8, lanes))
            da_ref[1] += jnp.broadcast_to(da_im, (8, lanes))

    order = lambda j: _s5_order(n_chunks - 1 - j, n_chunks, nc, rev)
    whole = lambda arr: pl.BlockSpec(arr.shape, lambda bi, j: (0,) * arr.ndim)
    us = pl.BlockSpec((None, t_len, w), lambda bi, j: (bi, order(j), 0))
    return pl.pallas_call(
        body,
        out_shape=[jax.ShapeDtypeStruct((b, s, w), F32), jax.ShapeDtypeStruct(b_bd.shape, F32),
                   jax.ShapeDtypeStruct(c_bd.shape, F32), jax.ShapeDtypeStruct((2, 8, lanes), F32)],
        grid=(b, n_chunks),
        in_specs=[us, whole(tab_adj), whole(b_bd), whole(c_bd),
                  pl.BlockSpec((2, None, t_len, lanes), lambda bi, j: (0, bi, order(j), 0)),
                  pl.BlockSpec((2, None, None, 8, lanes), lambda bi, j: (0, bi, order(j), 0, 0)), us],
        out_specs=[us, whole(b_bd), whole(c_bd), pl.BlockSpec((2, 8, lanes), lambda bi, j: (0, 0, 0))],
        scratch_shapes=[pltpu.VMEM((2, 8, lanes), F32), pltpu.VMEM((t_len, LANE), F32), pltpu.VMEM((t_len, LANE), F32),
                        pltpu.VMEM((2, t_len // 8, LANE), F32)],
        name=f"s5_b_{s}_{int(rev)}", compiler_params=_params(),
    )(u, tab_adj, b_bd, c_bd, h, hin, dy)


@functools.partial(jax.custom_vjp, nondiff_argnums=(4, 5))
def s5_direction(u, a, b_bd, c_bd, n_ctx, rev):
    t_len = _s5_chunk(n_ctx)
    return _s5_fwd(u, _s5_tables(_s5_powers(a[0], a[1], t_len), t_len, rev), b_bd, c_bd, n_ctx, rev)[0]


def _s5_direction_fwd(u, a, b_bd, c_bd, n_ctx, rev):
    t_len = _s5_chunk(n_ctx)
    pw = _s5_powers(a[0], a[1], t_len)
    y, h, hin = _s5_fwd(u, _s5_tables(pw, t_len, rev), b_bd, c_bd, n_ctx, rev)
    return y, (u, pw, b_bd, c_bd, h, hin)


def _s5_direction_bwd(n_ctx, rev, res, dy):
    u, pw, b_bd, c_bd, h, hin = res
    tab_adj = _s5_tables(pw, _s5_chunk(n_ctx), not rev, conj=True)
    du, db, dc, da = _s5_bwd(u, tab_adj, b_bd, c_bd, h, hin, dy, n_ctx, rev)
    return du, da[:, 0, :], db, dc


s5_direction.defvjp(_s5_direction_fwd, _s5_direction_bwd)


def _s5_discretize(lam_re, lam_im, log_dt, b_re, b_im):
    dt = jnp.exp(log_dt)[:, None]
    mag = jnp.exp(lam_re * dt)
    a_re = mag * jnp.cos(lam_im * dt)
    a_im = mag * jnp.sin(lam_im * dt)
    den = jnp.square(lam_re) + jnp.square(lam_im)
    f_re = ((a_re - 1.0) * lam_re + a_im * lam_im) / den
    f_im = (a_im * lam_re - (a_re - 1.0) * lam_im) / den
    bb_re = f_re[..., None] * b_re - f_im[..., None] * b_im
    bb_im = f_re[..., None] * b_im + f_im[..., None] * b_re
    return a_re, a_im, bb_re, bb_im


def _block_diag(t):
    g, r, c = t.shape
    return (jnp.eye(g, dtype=F32)[:, None, :, None] * t[:, :, None, :]).reshape(g * r, g * c)


def _loss_head(y, target):
    b, n, d = y.shape
    ts = _pick(n, (256, 128, 64))

    def body(y_ref, t_ref, loss_ref, dy_ref):
        first = jnp.logical_and(pl.program_id(0) == 0, pl.program_id(1) == 0)
        err = y_ref[...] - t_ref[...]
        dy_ref[...] = err * (1.0 / d)
        part = 0.5 * jnp.sum(jnp.sum(err * err, axis=-1, keepdims=True) * (1.0 / d), axis=0, keepdims=True)
        part = jnp.broadcast_to(part, (8, LANE))

        @pl.when(first)
        def _():
            loss_ref[...] = part

        @pl.when(jnp.logical_not(first))
        def _():
            loss_ref[...] += part

    blk = pl.BlockSpec((None, ts, d), lambda bi, i: (bi, i, 0))
    return pl.pallas_call(
        body, out_shape=[jax.ShapeDtypeStruct((8, LANE), F32), jax.ShapeDtypeStruct((b, n, d), F32)],
        grid=(b, n // ts), in_specs=[blk, blk], out_specs=[pl.BlockSpec((8, LANE), lambda bi, i: (0, 0)), blk],
        name="loss_head", compiler_params=_params(),
    )(y, target)


def _adamw(w, g, m, v):
    shape = w.shape
    n = int(np.prod(shape))
    cols = shape[-1]
    r = n // cols
    tr = _pick(r, (512, 256, 128, 64, 32, 16, 8))
    c1 = 1.0 / (1.0 - ADAM_B1 ** ADAM_STEP)
    c2 = 1.0 / (1.0 - ADAM_B2 ** ADAM_STEP)

    def body(w_ref, g_ref, m_ref, v_ref, d_ref, mo_ref, vo_ref):
        gv = g_ref[...]
        m2 = ADAM_B1 * m_ref[...] + (1.0 - ADAM_B1) * gv
        v2 = ADAM_B2 * v_ref[...] + (1.0 - ADAM_B2) * (gv * gv)
        d_ref[...] = -ADAM_LR * ((m2 * c1) / (jnp.sqrt(v2 * c2) + ADAM_EPS) + ADAM_WD * w_ref[...])
        mo_ref[...] = m2
        vo_ref[...] = v2

    blk = pl.BlockSpec((tr, cols), lambda i: (i, 0))
    outs = pl.pallas_call(
        body, out_shape=[jax.ShapeDtypeStruct((r, cols), F32)] * 3, grid=(r // tr,),
        in_specs=[blk] * 4, out_specs=[blk] * 3, name=f"adamw_{r}x{cols}", compiler_params=_params(),
    )(*[t.reshape(r, cols) for t in (w, g, m, v)])
    return tuple(o.reshape(shape) for o in outs)


def _sum_rows(x, n):
    _, r, c = x.shape
    tr = _pick(r, (512, 256, 128, 64, 32, 16, 8))

    def body(x_ref, o_ref):
        acc = x_ref[0]
        for j in range(1, n):
            acc = acc + x_ref[j]
        o_ref[...] = acc

    return pl.pallas_call(
        body, out_shape=jax.ShapeDtypeStruct((r, c), F32), grid=(r // tr,),
        in_specs=[pl.BlockSpec((n, tr, c), lambda i: (0, i, 0))], out_specs=pl.BlockSpec((tr, c), lambda i: (i, 0)),
        name=f"sum{n}_{r}x{c}", compiler_params=_params(),
    )(x)


def _accumulate(parts, out_dtype):
    r, c = parts[0].shape[-2:]
    tr = _pick(r, (512, 256, 128, 64, 32, 16))

    def body(*refs):
        acc = None
        for ref in refs[:-1]:
            terms = [ref[j] for j in range(ref.shape[0])] if len(ref.shape) == 3 else [ref[...]]
            for t in terms:
                acc = t.astype(F32) if acc is None else acc + t.astype(F32)
        refs[-1][...] = acc.astype(out_dtype)

    specs = [pl.BlockSpec((p.shape[0], tr, c), lambda i: (0, i, 0)) if p.ndim == 3 else pl.BlockSpec((tr, c), lambda i: (i, 0))
             for p in parts]
    tag = "_".join(str(p.shape[0]) if p.ndim == 3 else "1" for p in parts)
    return pl.pallas_call(
        body, out_shape=jax.ShapeDtypeStruct((r, c), out_dtype), grid=(r // tr,), in_specs=specs,
        out_specs=pl.BlockSpec((tr, c), lambda i: (i, 0)), name=f"accumulate_{tag}_{r}x{c}_{jnp.dtype(out_dtype).name}",
        compiler_params=_params(),
    )(*parts)


def _add2(x, y):
    shape = x.shape
    c = shape[-1]
    r = int(np.prod(shape)) // c
    tr = _pick(r, (512, 256, 128, 64, 32, 16, 8))

    def body(x_ref, y_ref, o_ref):
        o_ref[...] = x_ref[...] + y_ref[...]

    blk = pl.BlockSpec((tr, c), lambda i: (i, 0))
    return pl.pallas_call(
        body, out_shape=jax.ShapeDtypeStruct((r, c), F32), grid=(r // tr,), in_specs=[blk, blk], out_specs=blk,
        name=f"add2_{r}x{c}", compiler_params=_params(),
    )(x.reshape(r, c), y.reshape(r, c)).reshape(shape)


_FLIPS = ((1, 0), (0, 1), (1, 1))


def _me():
    return lax.axis_index("x"), lax.axis_index("y"), lax.axis_index("c")


def allgather8(v):
    m_per, n = v.shape

    def body(x_ref, out_ref, send_sems, recv_sems, local_sem):
        x, y, c = _me()
        me, sibling = (x, y, c), (x, y, 1 - c)
        chips = [(1 - x, y), (x, 1 - y), (1 - x, 1 - y)]

        def rows(px, py, pc):
            return out_ref.at[pl.ds((4 * px + 2 * py + pc) * m_per, m_per), :]

        def copy(k, block, to, src=None):
            return pltpu.make_async_remote_copy(
                src_ref=rows(*block) if src is None else src, dst_ref=rows(*block),
                send_sem=send_sems.at[k], recv_sem=recv_sems.at[k], device_id=to, device_id_type=MESH)

        mine = pltpu.make_async_copy(x_ref, rows(*me), local_sem)
        mine.start()
        first = [copy(0, me, sibling, src=x_ref)]
        first += [copy(1 + j, me, (*chip, c), src=x_ref) for j, chip in enumerate(chips)]
        for cp in first:
            cp.start()
        passed = [copy(4 + j, (*chip, c), sibling) for j, chip in enumerate(chips)]
        for j, chip in enumerate(chips):
            copy(1 + j, (*chip, c), me).wait_recv()
            passed[j].start()
        copy(0, sibling, me).wait_recv()
        for j, chip in enumerate(chips):
            copy(4 + j, (*chip, 1 - c), me).wait_recv()
        for cp in first + passed:
            cp.wait_send()
        mine.wait()

    return pl.pallas_call(
        body, out_shape=jax.ShapeDtypeStruct((N_DEV * m_per, n), v.dtype), in_specs=[VMEM_SPEC], out_specs=VMEM_SPEC,
        scratch_shapes=[pltpu.SemaphoreType.DMA((7,)), pltpu.SemaphoreType.DMA((7,)), pltpu.SemaphoreType.DMA],
        name=f"allgather8_{m_per}x{n}", compiler_params=_params(),
    )(v)


def _row_chunks(rows, tile_rows, want):
    n = want
    while n > 1 and rows % (n * tile_rows):
        n //= 2
    return [(i * (rows // n), rows // n) for i in range(n)]


def _remote(src, dst, send_sem, recv_sem, to):
    return pltpu.make_async_remote_copy(src_ref=src, dst_ref=dst, send_sem=send_sem, recv_sem=recv_sem, device_id=to,
                                        device_id_type=MESH)


def plane_allgather(big, small):
    rows = big.shape[0]
    rh = rows // 2
    tile = 16 if big.dtype == BF16 else 8
    ch_full = _row_chunks(rows, tile, 8)
    ch_half = _row_chunks(rh, tile, 4)

    def body(big_ref, small_ref, obig_ref, osmall_ref, send_sems, recv_sems, fwd_send, fwd_recv, own_send, own_recv):
        x, y, c = _me()
        me = 2 * x + y
        sibling = (x, y, 1 - c)
        mine = pl.ds(c * rh, rh)
        other = pl.ds((1 - c) * rh, rh)
        peers = [((x + fx) & 1, (y + fy) & 1) for fx, fy in _FLIPS]
        for st, sz in ch_full:
            sl = pl.ds(st, sz)
            _remote(big_ref.at[sl], obig_ref.at[me, sl], own_send.at[0], own_recv.at[0], sibling).start()
        _remote(small_ref, osmall_ref.at[me], own_send.at[1], own_recv.at[1], sibling).start()
        for j, (px, py) in enumerate(peers):
            for st, sz in ch_half:
                sl = pl.ds(c * rh + st, sz)
                _remote(big_ref.at[sl], obig_ref.at[me, sl], send_sems.at[j], recv_sems.at[j], (px, py, c)).start()
            _remote(small_ref, osmall_ref.at[me], send_sems.at[3 + j], recv_sems.at[3 + j], (px, py, c)).start()
        for j, (px, py) in enumerate(peers):
            pidx = 2 * px + py
            _remote(big_ref.at[mine], obig_ref.at[pidx, mine], send_sems.at[j], recv_sems.at[j], (px, py, c)).wait_recv()
            for st, sz in ch_half:
                sl = pl.ds(c * rh + st, sz)
                _remote(obig_ref.at[pidx, sl], obig_ref.at[pidx, sl], fwd_send.at[j], fwd_recv.at[j], sibling).start()
            _remote(small_ref, osmall_ref.at[pidx], send_sems.at[3 + j], recv_sems.at[3 + j], (px, py, c)).wait_recv()
        for j, (px, py) in enumerate(peers):
            pidx = 2 * px + py
            _remote(obig_ref.at[pidx, other], obig_ref.at[pidx, other], fwd_send.at[j], fwd_recv.at[j], sibling).wait_recv()
        for j, (px, py) in enumerate(peers):
            pidx = 2 * px + py
            _remote(big_ref.at[mine], obig_ref.at[me, mine], send_sems.at[j], recv_sems.at[j], (px, py, c)).wait_send()
            _remote(small_ref, osmall_ref.at[me], send_sems.at[3 + j], recv_sems.at[3 + j], (px, py, c)).wait_send()
            _remote(obig_ref.at[pidx, mine], obig_ref.at[pidx, mine], fwd_send.at[j], fwd_recv.at[j], sibling).wait_send()
        _remote(big_ref, obig_ref.at[me], own_send.at[0], own_recv.at[0], sibling).wait()
        _remote(small_ref, osmall_ref.at[me], own_send.at[1], own_recv.at[1], sibling).wait()

    return pl.pallas_call(
        body, out_shape=[jax.ShapeDtypeStruct((N_PLANE,) + big.shape, big.dtype),
                         jax.ShapeDtypeStruct((N_PLANE,) + small.shape, small.dtype)],
        in_specs=[ANY, ANY], out_specs=[ANY, ANY],
        scratch_shapes=[pltpu.SemaphoreType.DMA((6,)), pltpu.SemaphoreType.DMA((6,)), pltpu.SemaphoreType.DMA((3,)),
                        pltpu.SemaphoreType.DMA((3,)), pltpu.SemaphoreType.DMA((2,)), pltpu.SemaphoreType.DMA((2,))],
        name="plane_allgather", compiler_params=_params(),
    )(big, small)


def plane_scatter(p):
    tile = 16 if p.dtype == BF16 else 8
    chunks = _row_chunks(p.shape[1], tile, 4)

    def body(p_ref, out_ref, send_sems, recv_sems):
        x, y, c = _me()
        peers = [((x + fx) & 1, (y + fy) & 1) for fx, fy in _FLIPS]
        for j, (px, py) in enumerate(peers):
            for st, sz in chunks:
                sl = pl.ds(st, sz)
                _remote(p_ref.at[2 * px + py, sl], out_ref.at[j, sl], send_sems.at[j], recv_sems.at[j], (px, py, c)).start()
        for j, (px, py) in enumerate(peers):
            _remote(p_ref.at[0], out_ref.at[j], send_sems.at[j], recv_sems.at[j], (px, py, c)).wait_recv()
        for j, (px, py) in enumerate(peers):
            _remote(p_ref.at[0], out_ref.at[j], send_sems.at[j], recv_sems.at[j], (px, py, c)).wait_send()

    return pl.pallas_call(
        body, out_shape=jax.ShapeDtypeStruct((len(_FLIPS),) + p.shape[1:], p.dtype), in_specs=[ANY], out_specs=ANY,
        scratch_shapes=[pltpu.SemaphoreType.DMA((3,)), pltpu.SemaphoreType.DMA((3,))],
        name="plane_scatter", compiler_params=_params(),
    )(p)


def sibling_halves(buf):
    n_blk, _, rows, cols = buf.shape
    tile = 16 if buf.dtype == BF16 else 8
    chunks = _row_chunks(rows, tile, 2)

    def body(buf_ref, got_ref, send_sem, recv_sem):
        x, y, c = _me()
        for j in range(n_blk):
            for st, sz in chunks:
                sl = pl.ds(st, sz)
                _remote(buf_ref.at[j, 1 - c, sl], got_ref.at[j, sl], send_sem, recv_sem, (x, y, 1 - c)).start()
        _remote(got_ref, got_ref, send_sem, recv_sem, (x, y, 1 - c)).wait()

    return pl.pallas_call(
        body, out_shape=jax.ShapeDtypeStruct((n_blk, rows, cols), buf.dtype), in_specs=[ANY], out_specs=ANY,
        scratch_shapes=[pltpu.SemaphoreType.DMA, pltpu.SemaphoreType.DMA],
        name="sibling_halves", compiler_params=_params(),
    )(buf)


def sibling_swap(s):
    tile = 16 if s.dtype == BF16 else 8
    chunks = _row_chunks(s.shape[0], tile, 8)

    def body(s_ref, got_ref, send_sem, recv_sem):
        x, y, c = _me()
        for st, sz in chunks:
            sl = pl.ds(st, sz)
            _remote(s_ref.at[sl], got_ref.at[sl], send_sem, recv_sem, (x, y, 1 - c)).start()
        _remote(s_ref, got_ref, send_sem, recv_sem, (x, y, 1 - c)).wait()

    return pl.pallas_call(
        body, out_shape=jax.ShapeDtypeStruct(s.shape, s.dtype), in_specs=[ANY], out_specs=ANY,
        scratch_shapes=[pltpu.SemaphoreType.DMA, pltpu.SemaphoreType.DMA],
        name="sibling_swap", compiler_params=_params(),
    )(s)


def _heads(t, n_heads):
    b, s, w = t.shape
    return jnp.transpose(t.reshape(b, s, n_heads, w // n_heads), (0, 2, 1, 3)).reshape(b * n_heads, s, w // n_heads)


def _unheads(t, b):
    bh, s, d = t.shape
    return jnp.transpose(t.reshape(b, bh // b, s, d), (0, 2, 1, 3)).reshape(b, s, (bh // b) * d)


def _op(cache, fn, name, kinds, out_dims, **kw):
    key = (name, tuple(out_dims), tuple(sorted(kw.items())))
    if key not in cache:
        cache[key] = make_rowwise(fn, name, kinds, out_dims, **kw)
    return cache[key]


def _even_mixer(ops, a, w, n_ctx):
    b, s, d = a.shape
    proj = linear(a.reshape(b * s, d), w["e_w_in"]).reshape(b, s, -1)
    q, k, v, u = jnp.split(proj, [GQA_Q_W, GQA_Q_W + GQA_KV_W, GQA_Q_W + 2 * GQA_KV_W], axis=-1)
    cos, sin = _rope_tables(n_ctx, s - n_ctx, HEAD_DIM, 0, HEAD_DIM)
    shift = HEAD_DIM // 4
    qn = head_norm_rope(q, w["e_g_q"][None], jnp.tile(cos, (1, GQA_Q_HEADS)), jnp.tile(sin, (1, GQA_Q_HEADS)),
                        jnp.asarray(_rope_signs(HEAD_DIM, 0, HEAD_DIM, GQA_Q_HEADS)), GQA_Q_HEADS, shift)
    kn = head_norm_rope(k, w["e_g_k"][None], jnp.tile(cos, (1, GQA_KV_HEADS)), jnp.tile(sin, (1, GQA_KV_HEADS)),
                        jnp.asarray(_rope_signs(HEAD_DIM, 0, HEAD_DIM, GQA_KV_HEADS)), GQA_KV_HEADS, shift)
    att = gqa_attention(qn, kn, v, n_ctx)
    ys = []
    for dr in range(2):
        a_re, a_im, bb_re, bb_im = _s5_discretize(w["ssm_lam_re"][dr], w["ssm_lam_im"][dr], w["ssm_log_dt"][dr],
                                                  w["ssm_b_re"][dr], w["ssm_b_im"][dr])
        a_flat = jnp.stack([a_re.reshape(-1), a_im.reshape(-1)])
        b_bd = jnp.stack([_block_diag(jnp.swapaxes(bb_re, 1, 2)), _block_diag(jnp.swapaxes(bb_im, 1, 2))])
        c_bd = jnp.stack([_block_diag(jnp.swapaxes(w["ssm_c_re"][dr], 1, 2)),
                          _block_diag(jnp.swapaxes(w["ssm_c_im"][dr], 1, 2))])
        ys.append(s5_direction(u, a_flat, b_bd, c_bd, n_ctx, dr == 1))
    pre = _op(ops, _fn_glu_pre, "glu_pre", ("row", "row", "row", "glob"), (SSM_WIDTH,))
    post = _op(ops, _fn_glu_post, "glu_post", ("row", "row", "glob"), (SSM_WIDTH,))
    z = pre(u, ys[0], ys[1], w["ssm_d"][None])[0]
    t = linear(z.reshape(b * s, SSM_WIDTH), w["ssm_w_glu"]).reshape(b, s, SSM_WIDTH)
    ssm = post(z, t, w["ssm_b_glu"][None])[0]
    mix = jnp.concatenate([att, ssm], axis=-1)
    return linear(mix.reshape(b * s, -1), w["e_w_out"]).reshape(b, s, d)


def _odd_mixer(ops, a, w, n_ctx):
    b, s, d = a.shape
    w_in = jnp.pad(w["o_w_in"], ((0, 0), (0, ODD_IN_PAD - ODD_IN_W)))
    proj = linear(a.reshape(b * s, d), w_in).reshape(b, s, -1)
    c1 = MLA_Q_RANK
    c2 = c1 + MLA_KV_RANK
    c3 = c2 + MLA_ROPE
    cq, ckv, kr, nq, nk, nv, _ = jnp.split(proj, [c1, c2, c3, c3 + NA_W, c3 + 2 * NA_W, ODD_IN_W], axis=-1)
    nrm = lambda wd: _op(ops, _fn_norm, f"norm{wd}", ("row", "glob"), (wd,))
    cqn = nrm(MLA_Q_RANK)(cq, w["mla_g_cq"][None])[0]
    ckvn = nrm(MLA_KV_RANK)(ckv, w["mla_g_ckv"][None])[0]
    q = linear(cqn.reshape(b * s, -1), w["mla_w_uq"]).reshape(b, s, -1)
    kv = linear(ckvn.reshape(b * s, -1), w["mla_w_ukv"]).reshape(b, s, -1)
    cos, sin = _rope_tables(n_ctx, s - n_ctx, MLA_QK, MLA_NOPE, MLA_ROPE)
    cos, sin = jnp.tile(cos, (1, MLA_HEADS)), jnp.tile(sin, (1, MLA_HEADS))
    sg = jnp.asarray(_rope_signs(MLA_QK, MLA_NOPE, MLA_ROPE, MLA_HEADS))
    mq = head_norm_rope(q, w["mla_g_q"][None], cos, sin, sg, MLA_HEADS, MLA_ROPE // 4)
    mk = mla_k_prep(kv, kr, w["mla_g_k"][None], cos, sin, sg)
    mla = mla_attention(mq, mk, kv, n_ctx)
    nqn = head_norm_rope(nq, w["na_g_q"][None], None, None, None, NA_HEADS, 0)
    nkn = head_norm_rope(nk, w["na_g_k"][None], None, None, None, NA_HEADS, 0)
    na = na_attention_tm(nqn, nkn, nv, na_bias_table(w["na_rpb"]), n_ctx)
    mix = jnp.concatenate([mla, na], axis=-1)
    return linear(mix.reshape(b * s, -1), w["o_w_out"]).reshape(b, s, d)


_EVEN_KEYS = ("e_w_in", "e_w_out", "e_g_q", "e_g_k", "ssm_lam_re", "ssm_lam_im", "ssm_log_dt", "ssm_b_re", "ssm_b_im",
              "ssm_c_re", "ssm_c_im", "ssm_d", "ssm_w_glu", "ssm_b_glu")
_ODD_KEYS = ("o_w_in", "o_w_out", "mla_g_cq", "mla_g_ckv", "mla_w_uq", "mla_w_ukv", "mla_g_q", "mla_g_k", "na_g_q",
             "na_g_k", "na_rpb")


def _trunk(x_all, mods, w, n_ctx):
    ops = {}
    depth = mods.shape[0]
    b, s, d = x_all.shape
    modulate = make_modulate(d, n_ctx)
    gated = make_gated_add(d, n_ctx)
    x = x_all
    for i in range(depth):
        j = i // 2
        m = [mods[i][:, :, r:r + 1, :] for r in range(N_MOD)]
        a, x = modulate(x, w["g_norm1"][i][None], m[0], m[1])
        if i % 2 == 0:
            o = _even_mixer(ops, a, {k: w[k][j] for k in _EVEN_KEYS}, n_ctx)
        else:
            o = _odd_mixer(ops, a, {k: w[k][j] for k in _ODD_KEYS}, n_ctx)
        x = gated(x, o, m[2])
        a2, x = modulate(x, w["g_norm2"][i][None], m[3], m[4])
        f = ffn(a2.reshape(b * s, d), w["w_ff1"][i], w["w_ff2"][i]).reshape(b, s, d)
        x = gated(x, f, m[5])
    return x[:, n_ctx:]


def local_step(x, ctx, mods, w, loss_target):
    n_ctx = ctx.shape[1]
    x_all = jnp.concatenate([ctx, x], axis=1)
    y, vjp = jax.vjp(lambda xa, md, ww: _trunk(xa, md, ww, n_ctx), x_all, mods, w)
    loss_tile, dy = _loss_head(y, loss_target)
    dx_all, dmods, dw = vjp(dy)
    return loss_tile[0, 0], dx_all[:, n_ctx:], dmods, dw


_SHARDED = (("w_ff1", 2), ("w_ff2", 1), ("e_w_in", 2), ("e_w_out", 1), ("o_w_in", 2), ("o_w_out", 1),
            ("mla_w_uq", 2), ("mla_w_ukv", 2), ("ssm_w_glu", 1))
_SHARDED_SMALL = (("mla_g_cq", 1), ("mla_g_ckv", 1))
_REPLICATED = ("g_norm1", "g_norm2", "e_g_q", "e_g_k", "ssm_lam_re", "ssm_lam_im", "ssm_log_dt", "ssm_b_re", "ssm_b_im",
               "ssm_c_re", "ssm_c_im", "ssm_d", "ssm_b_glu", "mla_g_q", "mla_g_k", "na_g_q", "na_g_k", "na_rpb")
_WEIGHTS = ("c_ctx", "w_mod", "b_mod", "g_norm1", "g_norm2", "w_ff1", "w_ff2", "e_w_in", "e_w_out", "e_g_q", "e_g_k",
            "ssm_lam_re", "ssm_lam_im", "ssm_log_dt", "ssm_b_re", "ssm_b_im", "ssm_c_re", "ssm_c_im", "ssm_d",
            "ssm_w_glu", "ssm_b_glu", "o_w_in", "o_w_out", "mla_g_cq", "mla_g_ckv", "mla_w_uq", "mla_w_ukv", "mla_g_q",
            "mla_g_k", "na_g_q", "na_g_k", "na_rpb")
_PACK_ROWS = 64


def _pack(arrs, dtype, cols=1024, row_mult=_PACK_ROWS):
    blocks, tail, off = [], [], 0
    for a in arrs:
        n = int(np.prod(a.shape))
        if not tail and off % cols == 0 and n % cols == 0:
            blocks.append(a.astype(dtype).reshape(-1, cols))
        else:
            tail.append(a.astype(dtype).reshape(-1))
        off += n
    rows = -(-off // cols)
    pad = (-rows) % row_mult * cols + rows * cols - off
    if tail or pad:
        blocks.append(jnp.concatenate(tail + [jnp.zeros((pad,), dtype)]).reshape(-1, cols))
    return jnp.concatenate(blocks, axis=0)


def _unpack(packed, shapes):
    cols = packed.shape[-1]
    packed = packed.reshape(-1, cols)
    out, off = [], 0
    for sh in shapes:
        n = int(np.prod(sh))
        if off % cols == 0 and n % cols == 0:
            out.append(packed[off // cols:(off + n) // cols].reshape(sh))
        else:
            r0, r1 = off // cols, -(-(off + n) // cols)
            out.append(packed[r0:r1].reshape(-1)[off - r0 * cols:off - r0 * cols + n].reshape(sh))
        off += n
    return out


def _silu(t):
    return t * jax.nn.sigmoid(t)


def kernel(x, c, ctx, c_ctx, w_mod, b_mod, g_norm1, g_norm2, w_ff1, w_ff2, e_w_in, e_w_out, e_g_q, e_g_k, ssm_lam_re, ssm_lam_im, ssm_log_dt, ssm_b_re, ssm_b_im, ssm_c_re, ssm_c_im, ssm_d, ssm_w_glu, ssm_b_glu, o_w_in, o_w_out, mla_g_cq, mla_g_ckv, mla_w_uq, mla_w_ukv, mla_g_q, mla_g_k, na_g_q, na_g_k, na_rpb, loss_target, m_c_ctx, m_w_mod, m_b_mod, m_g_norm1, m_g_norm2, m_w_ff1, m_w_ff2, m_e_w_in, m_e_w_out, m_e_g_q, m_e_g_k, m_ssm_lam_re, m_ssm_lam_im, m_ssm_log_dt, m_ssm_b_re, m_ssm_b_im, m_ssm_c_re, m_ssm_c_im, m_ssm_d, m_ssm_w_glu, m_ssm_b_glu, m_o_w_in, m_o_w_out, m_mla_g_cq, m_mla_g_ckv, m_mla_w_uq, m_mla_w_ukv, m_mla_g_q, m_mla_g_k, m_na_g_q, m_na_g_k, m_na_rpb, v_c_ctx, v_w_mod, v_b_mod, v_g_norm1, v_g_norm2, v_w_ff1, v_w_ff2, v_e_w_in, v_e_w_out, v_e_g_q, v_e_g_k, v_ssm_lam_re, v_ssm_lam_im, v_ssm_log_dt, v_ssm_b_re, v_ssm_b_im, v_ssm_c_re, v_ssm_c_im, v_ssm_d, v_ssm_w_glu, v_ssm_b_glu, v_o_w_in, v_o_w_out, v_mla_g_cq, v_mla_g_ckv, v_mla_w_uq, v_mla_w_ukv, v_mla_g_q, v_mla_g_k, v_na_g_q, v_na_g_k, v_na_rpb):
    env = dict(locals())
    weights = {n: env[n] for n in _WEIGHTS}
    mom_m = {n: env["m_" + n] for n in _WEIGHTS}
    mom_v = {n: env["v_" + n] for n in _WEIGHTS}
    ax, ay, ac = _me()
    plane = 2 * ax + ay
    dev = 4 * ax + 2 * ay + ac
    b_loc, d = c.shape
    depth = w_mod.shape[0]
    n_all = N_DEV * b_loc
    mod_cols = w_mod.shape[2]

    big = _pack([weights[n] for n, _ in _SHARDED], BF16)
    small = _pack([weights[n] for n, _ in _SHARDED_SMALL], F32, cols=LANE, row_mult=8)
    g_big, g_small = plane_allgather(big, small)
    full = {n: weights[n] for n in _REPLICATED}
    parts = [_unpack(g_big[j], [weights[n].shape for n, _ in _SHARDED]) for j in range(N_PLANE)]
    for t, (n, axis) in enumerate(_SHARDED):
        full[n] = [jnp.concatenate([parts[j][t][l] for j in range(N_PLANE)], axis=axis - 1)
                   for l in range(weights[n].shape[0])]
    parts_s = [_unpack(g_small[j], [weights[n].shape for n, _ in _SHARDED_SMALL]) for j in range(N_PLANE)]
    for t, (n, axis) in enumerate(_SHARDED_SMALL):
        full[n] = jnp.concatenate([parts_s[j][t] for j in range(N_PLANE)], axis=axis)

    rows_pad = 8 * ((n_all + 1 + 7) // 8)
    c_all = allgather8(jnp.pad(c, ((0, 8 - b_loc), (0, 0)))).reshape(N_DEV, 8, d)[:, :b_loc].reshape(n_all, d)
    cond_raw = jnp.concatenate([c_all, c_ctx[None], jnp.zeros((rows_pad - n_all - 1, d), F32)], axis=0)
    b_cols = lax.dynamic_slice_in_dim(b_mod, plane * mod_cols, mod_cols, axis=1)
    mod_loc = jnp.stack([_mm(cond_raw, w_mod[i], a_act="silu") + b_cols[i][None] for i in range(depth)])
    mod_g = allgather8(mod_loc.reshape(depth * rows_pad, mod_cols)).reshape(N_PLANE, 2, depth, rows_pad, mod_cols)
    mod_all = jnp.concatenate([mod_g[j, 0] for j in range(N_PLANE)], axis=-1)
    m_lat = lax.dynamic_slice_in_dim(mod_all, dev * b_loc, b_loc, axis=1)
    m_ctx = jnp.broadcast_to(mod_all[:, n_all][:, None], m_lat.shape)
    mods = jnp.stack([m_ctx, m_lat], axis=2).reshape(depth, b_loc, 2, N_MOD, d)

    loss_part, grad_x, dmods, dw = local_step(x, ctx, mods, full, loss_target)

    dm = dmods.reshape(depth, b_loc, 2, N_MOD * d)
    dm_rows = jnp.concatenate([dm[:, :, 1], jnp.sum(dm[:, :, 0], axis=1, keepdims=True)], axis=1)
    rep_shapes = [weights[n].shape for n in _REPLICATED] + [(1,)]
    small_pack = _pack([dm_rows] + [dw[n] for n in _REPLICATED] + [loss_part.reshape(1)], F32, cols=1024, row_mult=8)
    sp_rows = small_pack.shape[0]
    gathered = allgather8(small_pack).reshape(N_DEV, sp_rows, 1024)
    n_dm = depth * (b_loc + 1) * N_MOD * d
    dm_all = gathered.reshape(N_DEV, -1)[:, :n_dm].reshape(N_DEV, depth, b_loc + 1, N_MOD * d)
    rep_sum = _sum_rows(gathered, N_DEV).reshape(-1)
    rep_parts = _unpack(rep_sum[n_dm:], rep_shapes)
    rep_grads = dict(zip(_REPLICATED, rep_parts[:-1]))
    loss = rep_parts[-1][0]
    d_ctx_row = rep_sum[:n_dm].reshape(depth, b_loc + 1, N_MOD * d)[:, b_loc]
    d_lat_rows = jnp.transpose(dm_all[:, :, :b_loc], (1, 0, 2, 3)).reshape(depth, n_all, N_MOD * d)
    d_mod_all = jnp.concatenate([d_lat_rows, d_ctx_row[:, None],
                                 jnp.zeros((depth, rows_pad - n_all - 1, N_MOD * d), F32)], axis=1)
    grads = dict(rep_grads)
    grads["b_mod"] = jnp.sum(d_mod_all, axis=1)
    d_cols = lax.dynamic_slice_in_dim(d_mod_all, plane * mod_cols, mod_cols, axis=2)
    grads["w_mod"] = jnp.stack([_mm(cond_raw, d_cols[i], ta=True, a_act="silu") for i in range(depth)])
    d_cond = _mm(d_cols[0], w_mod[0], tb=True)
    for i in range(1, depth):
        d_cond = _add2(d_cond, _mm(d_cols[i], w_mod[i], tb=True))
    d_cond_g = allgather8(d_cond[n_all:n_all + 8] if rows_pad - n_all >= 8 else
                          jnp.pad(d_cond[n_all:], ((0, 8 - (rows_pad - n_all)), (0, 0)))).reshape(N_PLANE, 2, 8, d)
    d_silu = _sum_rows(d_cond_g[:, 0], N_PLANE)[0]
    sg = jax.nn.sigmoid(c_ctx)
    grads["c_ctx"] = d_silu * (sg * (1.0 + c_ctx * (1.0 - sg)))

    def shards_of(g, axis, j):
        layers = g if isinstance(g, (list, tuple)) else [g]
        ax = axis - 1 if isinstance(g, (list, tuple)) else axis
        n = layers[0].shape[ax] // N_PLANE
        return [lax.slice_in_dim(t, j * n, (j + 1) * n, axis=ax) for t in layers]

    send = jnp.stack([_pack([t for n, axis in _SHARDED + _SHARDED_SMALL for t in shards_of(dw[n], axis, j)], BF16)
                      for j in range(N_PLANE)])
    rows_h = send.shape[1] // 2
    send = send.reshape(N_PLANE, 2, rows_h, 1024)
    mine = lax.dynamic_index_in_dim(send, ac, 1, keepdims=False).reshape(N_PLANE * rows_h, 1024)
    theirs = sibling_halves(send).reshape(N_PLANE * rows_h, 1024)
    chip_sum = _accumulate([mine, theirs], BF16).reshape(N_PLANE, rows_h, 1024)
    own = lax.dynamic_index_in_dim(chip_sum, plane, 0, keepdims=False)
    done = _accumulate([own, plane_scatter(chip_sum)], BF16)
    both = jnp.stack([done, sibling_swap(done)])
    flat = jnp.where(ac == 0, both, both[::-1]).astype(F32).reshape(-1, 1024)
    shard_shapes = [weights[n].shape for n, _ in _SHARDED] + [weights[n].shape for n, _ in _SHARDED_SMALL]
    for (n, _), g in zip(_SHARDED + _SHARDED_SMALL, _unpack(flat, shard_shapes)):
        grads[n] = g

    big_names = ("w_mod",) + tuple(n for n, _ in _SHARDED)
    small_names = tuple(n for n in _WEIGHTS if n not in big_names)
    delta, new_m, new_v = {}, {}, {}
    for n in big_names:
        delta[n], new_m[n], new_v[n] = _adamw(weights[n], grads[n], mom_m[n], mom_v[n])
    sm_shapes = [weights[n].shape for n in small_names]
    packed = [_pack([src[n] for n in small_names], F32, cols=1024, row_mult=8)
              for src in (weights, grads, mom_m, mom_v)]
    for dst, res in zip((delta, new_m, new_v), _adamw(*packed)):
        dst.update(dict(zip(small_names, _unpack(res, sm_shapes))))

    return (loss, grad_x, *[grads[n] for n in _WEIGHTS], *[delta[n] for n in _WEIGHTS],
            *[new_m[n] for n in _WEIGHTS], *[new_v[n] for n in _WEIGHTS])
```

```python
import functools

import numpy as np
import jax
import jax.numpy as jnp
from jax import lax
from jax.experimental import pallas as pl
from jax.experimental.pallas import tpu as pltpu

F32 = jnp.float32
BF16 = jnp.bfloat16
HI = lax.Precision.HIGHEST
MESH = pl.DeviceIdType.MESH
ANY = pl.BlockSpec(memory_space=pl.ANY)
VMEM_SPEC = pl.BlockSpec(memory_space=pltpu.VMEM)

GRID_W = 64
HEAD_DIM = 64
ROPE_BASE = 10000.0
EPS = 1e-6
N_MOD = 6
GQA_Q_HEADS, GQA_KV_HEADS = 12, 4
GQA_Q_W, GQA_KV_W = GQA_Q_HEADS * HEAD_DIM, GQA_KV_HEADS * HEAD_DIM
SSM_WIDTH, SSM_GROUP, SSM_STATE = 256, 16, 64
SSM_GROUPS = SSM_WIDTH // SSM_GROUP
SSM_LANES = SSM_GROUPS * SSM_STATE
MLA_HEADS, MLA_Q_RANK, MLA_KV_RANK, MLA_NOPE, MLA_ROPE, MLA_V = 8, 512, 256, 64, 32, 64
MLA_QK = MLA_NOPE + MLA_ROPE
NA_HEADS, NA_WIN_R, NA_WIN_C = 8, 8, 16
NA_W = NA_HEADS * HEAD_DIM
NA_BAND = NA_WIN_R * GRID_W
ODD_IN_W = MLA_Q_RANK + MLA_KV_RANK + MLA_ROPE + 3 * NA_W
ODD_IN_PAD = 2560
ADAM_LR, ADAM_B1, ADAM_B2, ADAM_EPS, ADAM_WD, ADAM_STEP = 0.001, 0.9, 0.999, 1e-08, 0.01, 10
NEG = -1e30
VMEM_LIMIT = 56 * 1024 * 1024
LANE = 128
MM_TILE_M = (1152, 1024, 768, 512, 256, 128)
MM_TILE_N = (1280, 1024, 768, 512, 256, 128)
MM_TILE_K = (1152, 1024, 768, 512, 256, 128)
N_PLANE = 4
N_DEV = 8


def _pick(n, cands):
    for c in cands:
        if n % c == 0:
            return c
    return n


def _params(**kw):
    return pltpu.CompilerParams(vmem_limit_bytes=VMEM_LIMIT, **kw)


def _mm(a, b, *, ta=False, tb=False, a_act=None, epi=None, e=None, exact=False, out_dtype=F32):
    m, kd = (a.shape[1], a.shape[0]) if ta else a.shape
    n = b.shape[0] if tb else b.shape[1]
    tm = _pick(m, MM_TILE_M)
    tn = _pick(n, MM_TILE_N)
    tk = _pick(kd, MM_TILE_K)
    nk = kd // tk
    dn = (((0 if ta else 1,), (1 if tb else 0,)), ((), ()))
    narrow = jnp.dtype(out_dtype) != jnp.dtype(F32)
    assert not (narrow and epi is not None)

    def body(*refs):
        if narrow:
            a_ref, b_ref, out_ref, o_ref = refs
        elif epi is None:
            a_ref, b_ref, o_ref = refs
        else:
            a_ref, b_ref, e_ref, o_ref = refs
        k = pl.program_id(2)
        av = a_ref[...]
        if a_act == "relu2":
            av = jnp.square(jnp.maximum(av, 0.0))
        elif a_act == "silu":
            av = av * jax.nn.sigmoid(av)
        bv = b_ref[...]
        if exact:
            p = lax.dot_general(av, bv, dn, precision=HI, preferred_element_type=F32)
        else:
            p = lax.dot_general(av.astype(BF16), bv.astype(BF16), dn, preferred_element_type=F32)

        @pl.when(k == 0)
        def _():
            o_ref[...] = p

        @pl.when(k > 0)
        def _():
            o_ref[...] += p

        if epi == "drelu2":
            @pl.when(k == nk - 1)
            def _():
                o_ref[...] = o_ref[...] * (2.0 * jnp.maximum(e_ref[...], 0.0))

        if narrow:
            @pl.when(k == nk - 1)
            def _():
                out_ref[...] = o_ref[...].astype(out_dtype)

    a_spec = pl.BlockSpec((tk, tm), lambda i, j, k: (k, i)) if ta else pl.BlockSpec((tm, tk), lambda i, j, k: (i, k))
    b_spec = pl.BlockSpec((tn, tk), lambda i, j, k: (j, k)) if tb else pl.BlockSpec((tk, tn), lambda i, j, k: (k, j))
    o_spec = pl.BlockSpec((tm, tn), lambda i, j, k: (i, j))
    ins, specs = [a, b], [a_spec, b_spec]
    if epi is not None:
        ins.append(e)
        specs.append(o_spec)
    name = f"mm_{m}x{kd}x{n}_{int(ta)}{int(tb)}_{a_act}_{epi}_{int(exact)}_{jnp.dtype(out_dtype).name}"
    return pl.pallas_call(
        body, out_shape=jax.ShapeDtypeStruct((m, n), out_dtype), grid=(m // tm, n // tn, nk),
        in_specs=specs, out_specs=o_spec, name=name, compiler_params=_params(),
        scratch_shapes=[pltpu.VMEM((tm, tn), F32)] if narrow else [],
    )(*ins)


@functools.partial(jax.custom_vjp, nondiff_argnums=(2,))
def _linear(a, w, exact):
    return _mm(a, w, exact=exact)


def _linear_fwd(a, w, exact):
    return _mm(a, w, exact=exact), (a, w)


def _linear_bwd(exact, res, g):
    a, w = res
    return _mm(g, w, tb=True, exact=exact), _mm(a, g, ta=True, exact=exact, out_dtype=w.dtype)


_linear.defvjp(_linear_fwd, _linear_bwd)


def linear(a, w, exact=False):
    return _linear(a, w, exact)


@jax.custom_vjp
def ffn(a, w1, w2):
    return _mm(_mm(a, w1), w2, a_act="relu2")


def _ffn_fwd(a, w1, w2):
    h1 = _mm(a, w1)
    return _mm(h1, w2, a_act="relu2"), (a, w1, w2, h1)


def _ffn_bwd(res, g):
    a, w1, w2, h1 = res
    dh1 = _mm(g, w2, tb=True, epi="drelu2", e=h1)
    dw2 = _mm(h1, g, ta=True, a_act="relu2", out_dtype=w2.dtype)
    return _mm(dh1, w1, tb=True), _mm(a, dh1, ta=True, out_dtype=w1.dtype), dw2


ffn.defvjp(_ffn_fwd, _ffn_bwd)


def make_rowwise(fn, name, kinds, out_dims, nctx_rows=0, whole_seq=False):
    n_in = len(kinds)
    n_out = len(out_dims)
    diff = [i for i, kd in enumerate(kinds) if kd in ("row", "glob", "seg")]

    def layout(args):
        row0 = args[kinds.index("row")]
        g, s = row0.shape[0], row0.shape[1]
        ts = s if whole_seq else (min(256, nctx_rows) if nctx_rows else _pick(s, (256, 128, 64)))
        nctx = nctx_rows // ts
        return g, s, ts, nctx

    def spec_of(kind, arr, ts, nctx):
        if kind == "row":
            return pl.BlockSpec((None, ts, arr.shape[2]), lambda g, i: (g, i, 0))
        if kind == "tab":
            return pl.BlockSpec((ts, arr.shape[1]), lambda g, i: (i, 0))
        if kind in ("const", "glob"):
            return pl.BlockSpec(arr.shape, lambda g, i: (0, 0))
        return pl.BlockSpec((None, None) + arr.shape[2:], lambda g, i: (g, (i >= nctx).astype(jnp.int32), 0, 0))

    def fwd_call(*args):
        g, s, ts, nctx = layout(args)

        def body(*refs):
            vals = [r[...] for r in refs[:n_in]]
            outs = fn(*vals)
            for o_ref, o in zip(refs[n_in:], outs):
                o_ref[...] = o

        return pl.pallas_call(
            body, out_shape=[jax.ShapeDtypeStruct((g, s, d), F32) for d in out_dims], grid=(g, s // ts),
            in_specs=[spec_of(kd, a, ts, nctx) for kd, a in zip(kinds, args)],
            out_specs=[pl.BlockSpec((None, ts, d), lambda g_, i: (g_, i, 0)) for d in out_dims],
            name=f"{name}_f_{g}x{s}", compiler_params=_params(),
        )(*args)

    def bwd_call(args, cts):
        g, s, ts, nctx = layout(args)

        def body(*refs):
            in_refs, ct_refs, out_refs = refs[:n_in], refs[n_in:n_in + n_out], refs[n_in + n_out:]
            gi, i = pl.program_id(0), pl.program_id(1)
            vals = [r[...] for r in in_refs]

            def f(*dv):
                full = list(vals)
                for idx, v in zip(diff, dv):
                    full[idx] = v
                return tuple(fn(*full))

            _, vjp = jax.vjp(f, *[vals[idx] for idx in diff])
            grads = vjp(tuple(r[...] for r in ct_refs))
            for idx, o_ref, gr in zip(diff, out_refs, grads):
                if kinds[idx] == "row":
                    o_ref[...] = gr
                    continue
                if kinds[idx] == "glob":
                    first = jnp.logical_and(gi == 0, i == 0)
                else:
                    first = jnp.logical_or(i == 0, i == nctx)

                @pl.when(first)
                def _(o_ref=o_ref, gr=gr):
                    o_ref[...] = gr

                @pl.when(jnp.logical_not(first))
                def _(o_ref=o_ref, gr=gr):
                    o_ref[...] += gr

        in_specs = [spec_of(kd, a, ts, nctx) for kd, a in zip(kinds, args)]
        in_specs += [pl.BlockSpec((None, ts, d), lambda g_, i: (g_, i, 0)) for d in out_dims]
        return pl.pallas_call(
            body, out_shape=[jax.ShapeDtypeStruct(args[idx].shape, F32) for idx in diff], grid=(g, s // ts),
            in_specs=in_specs, out_specs=[spec_of(kinds[idx], args[idx], ts, nctx) for idx in diff],
            name=f"{name}_b_{g}x{s}", compiler_params=_params(),
        )(*args, *cts)

    @jax.custom_vjp
    def op(*args):
        return tuple(fwd_call(*args))

    def op_fwd(*args):
        return tuple(fwd_call(*args)), args

    def op_bwd(args, cts):
        grads = bwd_call(args, cts)
        full = [None] * n_in
        for idx, gr in zip(diff, grads):
            full[idx] = gr
        return tuple(jnp.zeros_like(a) if gfull is None else gfull for a, gfull in zip(args, full))

    op.defvjp(op_fwd, op_bwd)
    op.fwd_call, op.bwd_call = fwd_call, bwd_call
    return op


def make_modulate(d, n_ctx):
    one = make_rowwise(_fn_modulate, "modulate", ("row", "glob", "seg", "seg"), (d,), nctx_rows=n_ctx)
    two = make_rowwise(_fn_modulate_keep, "modulate_keep", ("row", "glob", "seg", "seg"), (d, d), nctx_rows=n_ctx)

    @jax.custom_vjp
    def op(x, g, shift, scale):
        return one.fwd_call(x, g, shift, scale)[0], x

    def fwd(x, g, shift, scale):
        return (one.fwd_call(x, g, shift, scale)[0], x), (x, g, shift, scale)

    def bwd(res, cts):
        return tuple(two.bwd_call(res, cts))

    op.defvjp(fwd, bwd)
    return op


def make_gated_add(d, n_ctx):
    add = make_rowwise(_fn_gated_add, "gated", ("row", "row", "seg"), (d,), nctx_rows=n_ctx)
    mul = make_rowwise(_fn_gate_mul, "gate_mul", ("row", "seg"), (d,), nctx_rows=n_ctx)

    @jax.custom_vjp
    def op(x, o, gate):
        return add.fwd_call(x, o, gate)[0]

    def fwd(x, o, gate):
        return add.fwd_call(x, o, gate)[0], (o, gate)

    def bwd(res, ct):
        do, dgate = mul.bwd_call(res, (ct,))
        return ct, do, dgate

    op.defvjp(fwd, bwd)
    return op


def _rms(x):
    return lax.rsqrt(jnp.mean(x * x, axis=-1, keepdims=True) + EPS)


def _fn_modulate(x, g, shift, scale):
    return ((x * _rms(x) * g) * (1.0 + scale) + shift,)


def _fn_modulate_keep(x, g, shift, scale):
    return _fn_modulate(x, g, shift, scale) + (x,)


def _fn_gated_add(x, o, gate):
    return (x + gate * o,)


def _fn_gate_mul(o, gate):
    return (gate * o,)


def _fn_norm(x, g):
    return (x * _rms(x) * g,)


def _fn_glu_pre(u, y0, y1, d):
    return (jax.nn.gelu(d * u + y0 + y1),)


def _fn_glu_post(z, t, bg):
    return (z * jax.nn.sigmoid(t + bg),)


def _rope_tables(n_ctx, n_lat, dh, start, rot_dim):
    t = jnp.arange(n_lat)
    rows = (t // GRID_W).astype(F32)
    cols = (t % GRID_W).astype(F32)
    axis_dim = rot_dim // 2
    freqs = ROPE_BASE ** (-jnp.arange(0, axis_dim, 2, dtype=F32) / axis_dim)
    ang_r = rows[:, None] * freqs
    ang_c = cols[:, None] * freqs
    ang = jnp.concatenate([ang_r, ang_r, ang_c, ang_c], axis=-1)
    cos = jnp.concatenate([jnp.ones((n_lat, start), F32), jnp.cos(ang)], axis=-1)
    sin = jnp.concatenate([jnp.zeros((n_lat, start), F32), jnp.sin(ang)], axis=-1)
    cos = jnp.concatenate([jnp.ones((n_ctx, dh), F32), cos], axis=0)
    sin = jnp.concatenate([jnp.zeros((n_ctx, dh), F32), sin], axis=0)
    return cos, sin


_NT = (((1,), (1,)), ((), ()))
_TN = (((0,), (0,)), ((), ()))


def _na_geometry(i, nc, rows):
    r = i - nc
    rs = jnp.clip(r - NA_WIN_R // 2, 0, rows - NA_WIN_R)
    is_ctx = i < nc
    cls = jnp.where(is_ctx, NA_WIN_R, r - rs)
    return jnp.where(is_ctx, 0, rs), cls


def _na_onehots():
    q = np.arange(GRID_W)[:, None]
    col = np.arange(GRID_W)[None, :]
    cs = np.clip(q - NA_WIN_C // 2, 0, GRID_W - NA_WIN_C)
    valid = (col >= cs) & (col < cs + NA_WIN_C)
    cidx = col - q + (NA_WIN_C - 1)
    n_b = 2 * NA_WIN_C - 1
    col_hot = np.zeros((LANE, GRID_W * GRID_W), np.float32)
    for qq in range(GRID_W):
        for cc in range(GRID_W):
            if valid[qq, cc]:
                col_hot[cidx[qq, cc], qq * GRID_W + cc] = 1.0
    row_hot = np.zeros((NA_WIN_R, NA_WIN_R, 2 * NA_WIN_R - 1), np.float32)
    for c in range(NA_WIN_R):
        for j in range(NA_WIN_R):
            row_hot[c, j, j - c + NA_WIN_R - 1] = 1.0
    mask = np.where(valid, 0.0, NEG).astype(np.float32)
    return col_hot, row_hot, mask, n_b


def na_bias_table(rpb):
    h = rpb.shape[0]
    col_hot, row_hot, mask, n_b = _na_onehots()
    t1 = jnp.einsum("cja,hab->hcjb", jnp.asarray(row_hot), rpb)
    t1 = jnp.pad(t1.reshape(h * NA_WIN_R * NA_WIN_R, n_b), ((0, 0), (0, LANE - n_b)))
    t2 = linear(t1, jnp.asarray(col_hot), True)
    t2 = t2.reshape(h, NA_WIN_R, NA_WIN_R, GRID_W, GRID_W) + jnp.asarray(mask)
    tab = jnp.transpose(t2, (0, 1, 3, 2, 4)).reshape(h, NA_WIN_R, GRID_W, NA_BAND)
    return jnp.concatenate([tab, jnp.full((h, 1, GRID_W, NA_BAND), NEG, F32)], axis=1)


def _first_step():
    return jnp.logical_and(pl.program_id(0) == 0, pl.program_id(1) == 0)


def _accum_out(ref, val, first):
    @pl.when(first)
    def _():
        ref[...] = val

    @pl.when(jnp.logical_not(first))
    def _():
        ref[...] += val


def _norm_head(xh, g):
    r = _rms(xh)
    yn = xh * r
    return yn * g, yn, r


def _norm_head_bwd(dy, yn, r, g):
    dg = jnp.sum(dy * yn, axis=0, keepdims=True)
    dyn = dy * g
    return r * (dyn - yn * jnp.mean(dyn * yn, axis=-1, keepdims=True)), dg


def _rope_signs(dh, start, rot_dim, n_heads):
    q = rot_dim // 4
    pos = np.arange(dh)
    quarter = (pos - start) // q
    inr = pos >= start
    sg = np.zeros((8, n_heads * dh), np.float32)
    sg[0] = np.tile(np.where(inr & (quarter % 2 == 0), -1.0, 0.0), n_heads)
    sg[1] = np.tile(np.where(inr & (quarter % 2 == 1), 1.0, 0.0), n_heads)
    return sg


def _rope_full(y, cos, sin, sg, q):
    w = y.shape[-1]
    rot = sg[0:1] * pltpu.roll(y, w - q, 1) + sg[1:2] * pltpu.roll(y, q, 1)
    return y * cos + rot * sin


def _rope_full_t(dy, cos, sin, sg, q):
    w = dy.shape[-1]
    z = dy * sin
    return dy * cos - sg[1:2] * pltpu.roll(z, q, 1) - sg[0:1] * pltpu.roll(z, w - q, 1)


def _hnr_call(x, g, cos, sin, sg, n_heads, q, dy=None):
    b, s, w = x.shape
    dh = w // n_heads
    ts = _pick(s, (256, 128, 64))
    rope = cos is not None

    def body(*refs):
        refs = list(refs)
        x_ref, g_ref = refs[0], refs[1]
        k = 2
        if rope:
            cos_ref, sin_ref, sg_ref = refs[2], refs[3], refs[4]
            k = 5
        gv = g_ref[...]
        if dy is None:
            o_ref = refs[k]
            for h in range(n_heads):
                sl = slice(h * dh, (h + 1) * dh)
                o_ref[:, sl] = _norm_head(x_ref[:, sl], gv)[0]
            if rope:
                o_ref[...] = _rope_full(o_ref[...], cos_ref[...], sin_ref[...], sg_ref[...], q)
            return
        dy_ref, dx_ref, dg_ref = refs[k], refs[k + 1], refs[k + 2]
        src = dy_ref
        if rope:
            dx_ref[...] = _rope_full_t(dy_ref[...], cos_ref[...], sin_ref[...], sg_ref[...], q)
            src = dx_ref
        dg = jnp.zeros((1, dh), F32)
        for h in range(n_heads):
            sl = slice(h * dh, (h + 1) * dh)
            _, yn, r = _norm_head(x_ref[:, sl], gv)
            dxh, dgh = _norm_head_bwd(src[:, sl], yn, r, gv)
            dx_ref[:, sl] = dxh
            dg = dg + dgh
        _accum_out(dg_ref, dg, _first_step())

    row = pl.BlockSpec((None, ts, w), lambda bi, i: (bi, i, 0))
    whole = lambda a: pl.BlockSpec(a.shape, lambda bi, i: (0, 0))
    ins, specs = [x, g], [row, whole(g)]
    if rope:
        ins += [cos, sin, sg]
        specs += [pl.BlockSpec((ts, w), lambda bi, i: (i, 0)), pl.BlockSpec((ts, w), lambda bi, i: (i, 0)), whole(sg)]
    if dy is None:
        out_shape, out_specs = jax.ShapeDtypeStruct(x.shape, F32), row
    else:
        ins.append(dy)
        specs.append(row)
        out_shape = [jax.ShapeDtypeStruct(x.shape, F32), jax.ShapeDtypeStruct(g.shape, F32)]
        out_specs = [row, whole(g)]
    return pl.pallas_call(
        body, out_shape=out_shape, grid=(b, s // ts), in_specs=specs, out_specs=out_specs,
        name=f"hnr_{'b' if dy is not None else 'f'}_{n_heads}x{dh}_{int(rope)}", compiler_params=_params(),
    )(*ins)


@functools.partial(jax.custom_vjp, nondiff_argnums=(5, 6))
def head_norm_rope(x, g, cos, sin, sg, n_heads, q):
    return _hnr_call(x, g, cos, sin, sg, n_heads, q)


def _head_norm_rope_fwd(x, g, cos, sin, sg, n_heads, q):
    return _hnr_call(x, g, cos, sin, sg, n_heads, q), (x, g, cos, sin, sg)


def _head_norm_rope_bwd(n_heads, q, res, dy):
    x, g, cos, sin, sg = res
    dx, dg = _hnr_call(x, g, cos, sin, sg, n_heads, q, dy=dy)
    zero = lambda t: None if t is None else jnp.zeros_like(t)
    return dx, dg, zero(cos), zero(sin), zero(sg)


head_norm_rope.defvjp(_head_norm_rope_fwd, _head_norm_rope_bwd)


def _mla_k_call(kv, kr, g, cos, sin, sg, dkn=None):
    b, s, _ = kv.shape
    ts = _pick(s, (256, 128, 64))
    hw = MLA_NOPE + MLA_V
    kn_w = MLA_HEADS * MLA_QK
    q = MLA_ROPE // 4

    def body(kv_ref, kr_ref, g_ref, cos_ref, sin_ref, sg_ref, *rest):
        gv = g_ref[...]
        krv = kr_ref[...]
        if dkn is None:
            (o_ref,) = rest
            for h in range(MLA_HEADS):
                kh = jnp.concatenate([kv_ref[:, h * hw:h * hw + MLA_NOPE], krv], axis=-1)
                o_ref[:, h * MLA_QK:(h + 1) * MLA_QK] = _norm_head(kh, gv)[0]
            o_ref[...] = _rope_full(o_ref[...], cos_ref[...], sin_ref[...], sg_ref[...], q)
            return
        dkn_ref, dkv_ref, dkr_ref, dg_ref, dy_ref = rest
        dy_ref[...] = _rope_full_t(dkn_ref[...], cos_ref[...], sin_ref[...], sg_ref[...], q)
        dg = jnp.zeros((1, MLA_QK), F32)
        dkr = jnp.zeros((ts, MLA_ROPE), F32)
        for h in range(MLA_HEADS):
            kh = jnp.concatenate([kv_ref[:, h * hw:h * hw + MLA_NOPE], krv], axis=-1)
            _, yn, r = _norm_head(kh, gv)
            dxh, dgh = _norm_head_bwd(dy_ref[:, h * MLA_QK:(h + 1) * MLA_QK], yn, r, gv)
            dkv_ref[:, h * hw:h * hw + MLA_NOPE] = dxh[:, :MLA_NOPE]
            dkv_ref[:, h * hw + MLA_NOPE:(h + 1) * hw] = jnp.zeros((ts, MLA_V), F32)
            dkr = dkr + dxh[:, MLA_NOPE:]
            dg = dg + dgh
        dkr_ref[...] = dkr
        _accum_out(dg_ref, dg, _first_step())

    row = lambda w: pl.BlockSpec((None, ts, w), lambda bi, i: (bi, i, 0))
    tab = pl.BlockSpec((ts, kn_w), lambda bi, i: (i, 0))
    whole = lambda a: pl.BlockSpec(a.shape, lambda bi, i: (0, 0))
    ins = [kv, kr, g, cos, sin, sg]
    specs = [row(kv.shape[2]), row(MLA_ROPE), whole(g), tab, tab, whole(sg)]
    scratch = []
    if dkn is None:
        out_shape, out_specs = jax.ShapeDtypeStruct((b, s, kn_w), F32), row(kn_w)
    else:
        ins.append(dkn)
        specs.append(row(kn_w))
        out_shape = [jax.ShapeDtypeStruct(kv.shape, F32), jax.ShapeDtypeStruct(kr.shape, F32),
                     jax.ShapeDtypeStruct(g.shape, F32)]
        out_specs = [row(kv.shape[2]), row(MLA_ROPE), whole(g)]
        scratch = [pltpu.VMEM((ts, kn_w), F32)]
    return pl.pallas_call(
        body, out_shape=out_shape, grid=(b, s // ts), in_specs=specs, out_specs=out_specs, scratch_shapes=scratch,
        name=f"mla_k_{'b' if dkn is not None else 'f'}", compiler_params=_params(),
    )(*ins)


@jax.custom_vjp
def mla_k_prep(kv, kr, g, cos, sin, sg):
    return _mla_k_call(kv, kr, g, cos, sin, sg)


def _mla_k_prep_fwd(kv, kr, g, cos, sin, sg):
    return _mla_k_call(kv, kr, g, cos, sin, sg), (kv, kr, g, cos, sin, sg)


def _mla_k_prep_bwd(res, dkn):
    kv, kr, g, cos, sin, sg = res
    dkv, dkr, dg = _mla_k_call(kv, kr, g, cos, sin, sg, dkn=dkn)
    return dkv, dkr, dg, jnp.zeros_like(cos), jnp.zeros_like(sin), jnp.zeros_like(sg)


mla_k_prep.defvjp(_mla_k_prep_fwd, _mla_k_prep_bwd)


class _HeadLayout:
    def __init__(self, groups, dq, dv, q_off, k_off, v_off, o_off, wq, wk, wv, wo, scale):
        self.groups, self.dq, self.dv, self.scale = groups, dq, dv, scale
        self.q_off, self.k_off, self.v_off, self.o_off = q_off, k_off, v_off, o_off
        self.wq, self.wk, self.wv, self.wo = wq, wk, wv, wo
        self.n_h = len(q_off)


def _gqa_layout():
    rep = GQA_Q_HEADS // GQA_KV_HEADS
    n_h = GQA_Q_HEADS // 2
    return _HeadLayout(2, HEAD_DIM, HEAD_DIM, [h * HEAD_DIM for h in range(n_h)], [(h // rep) * HEAD_DIM for h in range(n_h)],
                       [(h // rep) * HEAD_DIM for h in range(n_h)], [h * HEAD_DIM for h in range(n_h)],
                       n_h * HEAD_DIM, (n_h // rep) * HEAD_DIM, (n_h // rep) * HEAD_DIM, n_h * HEAD_DIM, HEAD_DIM ** -0.5)


def _mla_layout():
    n_h = MLA_HEADS // 2
    hw = MLA_NOPE + MLA_V
    return _HeadLayout(2, MLA_QK, MLA_V, [h * MLA_QK for h in range(n_h)], [h * MLA_QK for h in range(n_h)],
                       [h * hw + MLA_NOPE for h in range(n_h)], [h * MLA_V for h in range(n_h)],
                       n_h * MLA_QK, n_h * MLA_QK, n_h * hw, n_h * MLA_V, MLA_QK ** -0.5)


def _attn_tm_fwd(q, k, v, lay, n_ctx):
    b, s, _ = q.shape
    tq = min(256, n_ctx)
    nc = n_ctx // tq

    def body(q_ref, k_ref, v_ref, o_ref, lse_ref):
        def run(n_keys):
            for h in range(lay.n_h):
                qo, ko, vo, oo = lay.q_off[h], lay.k_off[h], lay.v_off[h], lay.o_off[h]
                qv = (q_ref[:, qo:qo + lay.dq] * lay.scale).astype(BF16)
                sc = lax.dot_general(qv, k_ref[0:n_keys, ko:ko + lay.dq].astype(BF16), _NT, preferred_element_type=F32)
                m = jnp.max(sc, axis=-1, keepdims=True)
                p = jnp.exp(sc - m)
                l = jnp.sum(p, axis=-1, keepdims=True)
                o = jnp.dot(p.astype(BF16), v_ref[0:n_keys, vo:vo + lay.dv].astype(BF16), preferred_element_type=F32)
                o_ref[:, oo:oo + lay.dv] = o / l
                lse_ref[:, h:h + 1] = m + jnp.log(l)

        pl.when(pl.program_id(2) < nc)(lambda: run(n_ctx))
        pl.when(pl.program_id(2) >= nc)(lambda: run(s))

    return pl.pallas_call(
        body, out_shape=[jax.ShapeDtypeStruct((b, s, lay.groups * lay.wo), F32),
                         jax.ShapeDtypeStruct((b, lay.groups, s, lay.n_h), F32)],
        grid=(b, lay.groups, s // tq),
        in_specs=[pl.BlockSpec((None, tq, lay.wq), lambda bi, g, i: (bi, i, g)),
                  pl.BlockSpec((None, s, lay.wk), lambda bi, g, i: (bi, 0, g)),
                  pl.BlockSpec((None, s, lay.wv), lambda bi, g, i: (bi, 0, g))],
        out_specs=[pl.BlockSpec((None, tq, lay.wo), lambda bi, g, i: (bi, i, g)),
                   pl.BlockSpec((None, None, tq, lay.n_h), lambda bi, g, i: (bi, g, i, 0))],
        name=f"attn_tm_f_{lay.dq}", compiler_params=_params(),
    )(q, k, v)


def _attn_tm_bwd(q, k, v, lse, o, do, lay, n_ctx):
    b, s, _ = q.shape
    tk = min(256, n_ctx)
    nc = n_ctx // tk

    def body(q_ref, k_ref, v_ref, lse_ref, o_ref, do_ref, dq_ref, dk_ref, dv_ref, delta_ref):
        @pl.when(pl.program_id(2) == 0)
        def _():
            dq_ref[...] = jnp.zeros_like(dq_ref)
            for h in range(lay.n_h):
                oo = lay.o_off[h]
                delta_ref[:, h:h + 1] = jnp.sum(o_ref[:, oo:oo + lay.dv] * do_ref[:, oo:oo + lay.dv], axis=-1,
                                                keepdims=True)

        def run(r0):
            dk_acc, dv_acc = {}, {}
            for h in range(lay.n_h):
                qo, ko, vo, oo = lay.q_off[h], lay.k_off[h], lay.v_off[h], lay.o_off[h]
                kh = k_ref[:, ko:ko + lay.dq].astype(BF16)
                vh = v_ref[:, vo:vo + lay.dv].astype(BF16)
                qv = (q_ref[r0:s, qo:qo + lay.dq] * lay.scale).astype(BF16)
                dob = do_ref[r0:s, oo:oo + lay.dv].astype(BF16)
                sc = lax.dot_general(qv, kh, _NT, preferred_element_type=F32)
                p = jnp.exp(sc - lse_ref[r0:s, h:h + 1])
                dvh = lax.dot_general(p.astype(BF16), dob, _TN, preferred_element_type=F32)
                dp = lax.dot_general(dob, vh, _NT, preferred_element_type=F32)
                dsb = (p * (dp - delta_ref[r0:s, h:h + 1])).astype(BF16)
                dkh = lax.dot_general(dsb, qv, _TN, preferred_element_type=F32)
                dq_ref[r0:s, qo:qo + lay.dq] += jnp.dot(dsb, kh, preferred_element_type=F32) * lay.scale
                dk_acc[ko] = dkh if ko not in dk_acc else dk_acc[ko] + dkh
                dv_acc[vo] = dvh if vo not in dv_acc else dv_acc[vo] + dvh
            if len(dv_acc) * lay.dv != lay.wv:
                dv_ref[...] = jnp.zeros_like(dv_ref)
            for ko, val in dk_acc.items():
                dk_ref[:, ko:ko + lay.dq] = val
            for vo, val in dv_acc.items():
                dv_ref[:, vo:vo + lay.dv] = val

        pl.when(pl.program_id(2) < nc)(lambda: run(0))
        pl.when(pl.program_id(2) >= nc)(lambda: run(n_ctx))

    full = lambda w: pl.BlockSpec((None, s, w), lambda bi, g, j: (bi, 0, g))
    blk = lambda w: pl.BlockSpec((None, tk, w), lambda bi, g, j: (bi, j, g))
    stat = pl.BlockSpec((None, None, s, lay.n_h), lambda bi, g, j: (bi, g, 0, 0))
    return pl.pallas_call(
        body, out_shape=[jax.ShapeDtypeStruct(q.shape, F32), jax.ShapeDtypeStruct(k.shape, F32),
                         jax.ShapeDtypeStruct(v.shape, F32)],
        grid=(b, lay.groups, s // tk),
        in_specs=[full(lay.wq), blk(lay.wk), blk(lay.wv), stat, full(lay.wo), full(lay.wo)],
        out_specs=[full(lay.wq), blk(lay.wk), blk(lay.wv)],
        scratch_shapes=[pltpu.VMEM((s, lay.n_h), F32)],
        name=f"attn_tm_b_{lay.dq}", compiler_params=_params(),
    )(q, k, v, lse, o, do)


def _make_attention_tm(lay):
    @functools.partial(jax.custom_vjp, nondiff_argnums=(3,))
    def op(q, k, v, n_ctx):
        return _attn_tm_fwd(q, k, v, lay, n_ctx)[0]

    def fwd(q, k, v, n_ctx):
        o, lse = _attn_tm_fwd(q, k, v, lay, n_ctx)
        return o, (q, k, v, o, lse)

    def bwd(n_ctx, res, do):
        q, k, v, o, lse = res
        return _attn_tm_bwd(q, k, v, lse, o, do, lay, n_ctx)

    op.defvjp(fwd, bwd)
    return op


gqa_attention = _make_attention_tm(_gqa_layout())
mla_attention = _make_attention_tm(_mla_layout())

NA_GROUPS = 1


def _na_tm_specs(s, nc, rows):
    hg = NA_HEADS // NA_GROUPS
    w = hg * HEAD_DIM
    qs = pl.BlockSpec((None, GRID_W, w), lambda bi, g, i: (bi, i, g))
    ks = pl.BlockSpec((None, s, w), lambda bi, g, i: (bi, 0, g))
    bs = pl.BlockSpec((hg, None, GRID_W, NA_BAND), lambda bi, g, i: (g, _na_geometry(i, nc, rows)[1], 0, 0))
    ls = pl.BlockSpec((None, None, GRID_W, hg), lambda bi, g, i: (bi, g, i, 0))
    return hg, w, qs, ks, bs, ls


def _na_tm_scores(q_ref, k_ref, bias_ref, hd, n_ctx, start, scale):
    sl = slice(hd * HEAD_DIM, (hd + 1) * HEAD_DIM)
    qv = (q_ref[:, sl] * scale).astype(BF16)
    kc = k_ref[0:n_ctx, sl].astype(BF16)
    kb = k_ref[pl.ds(start, NA_BAND), sl].astype(BF16)
    s_c = lax.dot_general(qv, kc, _NT, preferred_element_type=F32)
    s_l = lax.dot_general(qv, kb, _NT, preferred_element_type=F32) + bias_ref[hd]
    return sl, qv, kc, kb, s_c, s_l


def _na_tm_fwd(q, k, v, bias, n_ctx):
    b, s, _ = q.shape
    nc = n_ctx // GRID_W
    rows = (s - n_ctx) // GRID_W
    scale = HEAD_DIM ** -0.5
    hg, w, qs, ks, bs, ls = _na_tm_specs(s, nc, rows)

    def body(q_ref, k_ref, v_ref, bias_ref, o_ref, lse_ref):
        rs, _ = _na_geometry(pl.program_id(2), nc, rows)
        start = pl.multiple_of(n_ctx + rs * GRID_W, GRID_W)
        for hd in range(hg):
            sl, _, _, _, s_c, s_l = _na_tm_scores(q_ref, k_ref, bias_ref, hd, n_ctx, start, scale)
            m = jnp.maximum(jnp.max(s_c, axis=-1, keepdims=True), jnp.max(s_l, axis=-1, keepdims=True))
            p_c = jnp.exp(s_c - m)
            p_l = jnp.exp(s_l - m)
            l = jnp.sum(p_c, axis=-1, keepdims=True) + jnp.sum(p_l, axis=-1, keepdims=True)
            o = jnp.dot(p_c.astype(BF16), v_ref[0:n_ctx, sl].astype(BF16), preferred_element_type=F32)
            o = o + jnp.dot(p_l.astype(BF16), v_ref[pl.ds(start, NA_BAND), sl].astype(BF16), preferred_element_type=F32)
            o_ref[:, sl] = o / l
            lse_ref[:, hd:hd + 1] = m + jnp.log(l)

    return pl.pallas_call(
        body, out_shape=[jax.ShapeDtypeStruct(q.shape, F32), jax.ShapeDtypeStruct((b, NA_GROUPS, s, hg), F32)],
        grid=(b, NA_GROUPS, s // GRID_W), in_specs=[qs, ks, ks, bs], out_specs=[qs, ls],
        name=f"na_tm_f_{s}", compiler_params=_params(),
    )(q, k, v, bias)


def _na_tm_bwd(q, k, v, bias, o, lse, do, n_ctx):
    b, s, _ = q.shape
    nc = n_ctx // GRID_W
    rows = (s - n_ctx) // GRID_W
    scale = HEAD_DIM ** -0.5
    n_cls = NA_WIN_R + 1
    hg, w, qs, ks, bs, ls = _na_tm_specs(s, nc, rows)

    def body(q_ref, k_ref, v_ref, bias_ref, o_ref, lse_ref, do_ref, dq_ref, dk_ref, dv_ref, db_ref):
        i = pl.program_id(2)
        rs, cls = _na_geometry(i, nc, rows)
        _, cls_prev = _na_geometry(i - 1, nc, rows)
        start = pl.multiple_of(n_ctx + rs * GRID_W, GRID_W)
        first = jnp.logical_or(i == 0, cls != cls_prev)

        @pl.when(i == 0)
        def _():
            dk_ref[...] = jnp.zeros_like(dk_ref)
            dv_ref[...] = jnp.zeros_like(dv_ref)

        @pl.when(first)
        def _():
            db_ref[...] = jnp.zeros_like(db_ref)

        for hd in range(hg):
            sl, qv, kc, kb, s_c, s_l = _na_tm_scores(q_ref, k_ref, bias_ref, hd, n_ctx, start, scale)
            lse_v = lse_ref[:, hd:hd + 1]
            p_c = jnp.exp(s_c - lse_v)
            p_l = jnp.exp(s_l - lse_v)
            dov = do_ref[:, sl]
            dob = dov.astype(BF16)
            delta = jnp.sum(dov * o_ref[:, sl], axis=-1, keepdims=True)
            vc = v_ref[0:n_ctx, sl].astype(BF16)
            vb = v_ref[pl.ds(start, NA_BAND), sl].astype(BF16)
            ds_c = p_c * (lax.dot_general(dob, vc, _NT, preferred_element_type=F32) - delta)
            ds_l = p_l * (lax.dot_general(dob, vb, _NT, preferred_element_type=F32) - delta)
            dsc_b = ds_c.astype(BF16)
            dsl_b = ds_l.astype(BF16)
            dq_ref[:, sl] = (jnp.dot(dsc_b, kc, preferred_element_type=F32)
                             + jnp.dot(dsl_b, kb, preferred_element_type=F32)) * scale
            dk_ref[0:n_ctx, sl] += lax.dot_general(dsc_b, qv, _TN, preferred_element_type=F32)
            dk_ref[pl.ds(start, NA_BAND), sl] += lax.dot_general(dsl_b, qv, _TN, preferred_element_type=F32)
            dv_ref[0:n_ctx, sl] += lax.dot_general(p_c.astype(BF16), dob, _TN, preferred_element_type=F32)
            dv_ref[pl.ds(start, NA_BAND), sl] += lax.dot_general(p_l.astype(BF16), dob, _TN, preferred_element_type=F32)
            db_ref[hd] += ds_l

    dbs = pl.BlockSpec((None, hg, None, GRID_W, NA_BAND), lambda bi, g, i: (bi, g, _na_geometry(i, nc, rows)[1], 0, 0))
    return pl.pallas_call(
        body,
        out_shape=[jax.ShapeDtypeStruct(q.shape, F32), jax.ShapeDtypeStruct(q.shape, F32), jax.ShapeDtypeStruct(q.shape, F32),
                   jax.ShapeDtypeStruct((b, NA_HEADS, n_cls, GRID_W, NA_BAND), F32)],
        grid=(b, NA_GROUPS, s // GRID_W), in_specs=[qs, ks, ks, bs, qs, ls, qs], out_specs=[qs, ks, ks, dbs],
        name=f"na_tm_b_{s}", compiler_params=_params(),
    )(q, k, v, bias, o, lse, do)


@functools.partial(jax.custom_vjp, nondiff_argnums=(4,))
def na_attention_tm(q, k, v, bias, n_ctx):
    return _na_tm_fwd(q, k, v, bias, n_ctx)[0]


def _na_attention_tm_fwd(q, k, v, bias, n_ctx):
    o, lse = _na_tm_fwd(q, k, v, bias, n_ctx)
    return o, (q, k, v, bias, o, lse)


def _na_attention_tm_bwd(n_ctx, res, do):
    q, k, v, bias, o, lse = res
    dq, dk, dv, db = _na_tm_bwd(q, k, v, bias, o, lse, do, n_ctx)
    return dq, dk, dv, _sum_rows(db.reshape(db.shape[0], -1, NA_BAND), db.shape[0]).reshape(db.shape[1:])


na_attention_tm.defvjp(_na_attention_tm_fwd, _na_attention_tm_bwd)


def _cmul(ar, ai, br, bi):
    return ar * br - ai * bi, ar * bi + ai * br


def _s5_chunk(n_ctx):
    return min(256, n_ctx)


def _s5_powers(a_re, a_im, t_len):
    a_re, a_im = lax.stop_gradient(a_re), lax.stop_gradient(a_im)
    mag = jnp.sqrt(a_re * a_re + a_im * a_im)
    th = jnp.arctan2(a_im, a_re)
    t = jnp.arange(t_len + 1, dtype=F32)[:, None]
    pm = jnp.where(t == 0, 1.0, jnp.exp(t * jnp.log(jnp.maximum(mag, 1e-37))) * (mag > 0))
    return jnp.stack([pm * jnp.cos(t * th), pm * jnp.sin(t * th)])


def _s5_tables(pw, t_len, rev, conj=False):
    if conj:
        pw = pw * jnp.asarray([1.0, -1.0], F32)[:, None, None]
    steps = jnp.concatenate([pw[:, min(2 ** i, t_len)][:, None] for i in range(8)], axis=1)
    tile = pw[:, 1:9]
    a8k = pw[:, 0:t_len:8]
    if rev:
        tile, a8k = tile[:, ::-1], a8k[:, ::-1]
    misc = jnp.concatenate([pw[:, t_len:t_len + 1], jnp.zeros((2, 7, pw.shape[-1]), F32)], axis=1)
    return jnp.concatenate([steps, tile, misc, a8k], axis=1)


def _scan_chunk(x_re, x_im, tab_ref, hin_re, hin_im, rev, t_len, xs_ref, es_ref):
    outs = [_scan_slab(x_re[:, k:k + LANE], x_im[:, k:k + LANE], tab_ref, hin_re[:, k:k + LANE], hin_im[:, k:k + LANE],
                       rev, t_len, xs_ref, es_ref, k) for k in range(0, x_re.shape[-1], LANE)]
    return tuple(jnp.concatenate([o[t] for o in outs], axis=-1) for t in range(4))


def _scan_slab(x_re, x_im, tab_ref, hin_re, hin_im, rev, t_len, xs_ref, es_ref, k0):
    lanes = LANE
    n2 = t_len // 8
    tab_ref = tab_ref.at[:, :, k0:k0 + LANE]
    rin = lax.broadcasted_iota(jnp.int32, (t_len, lanes), 0) & 7
    for li, sh in enumerate((1, 2, 4)):
        m_re, m_im = tab_ref[0, li:li + 1, :], tab_ref[1, li:li + 1, :]
        amt = sh if not rev else t_len - sh
        c_re, c_im = _cmul(m_re, m_im, pltpu.roll(x_re, amt, 0), pltpu.roll(x_im, amt, 0))
        ok = (rin >= sh) if not rev else (rin < 8 - sh)
        x_re = x_re + jnp.where(ok, c_re, 0.0)
        x_im = x_im + jnp.where(ok, c_im, 0.0)
    xr_ref, xi_ref = xs_ref
    xr_ref[...] = x_re
    xi_ref[...] = x_im
    off = 0 if rev else 7
    e_re = xr_ref[pl.ds(off, n2, stride=8), :]
    e_im = xi_ref[pl.ds(off, n2, stride=8), :]
    row2 = lax.broadcasted_iota(jnp.int32, (n2, lanes), 0)
    sh, li = 1, 3
    while sh < n2:
        m_re, m_im = tab_ref[0, li:li + 1, :], tab_ref[1, li:li + 1, :]
        amt = sh if not rev else n2 - sh
        c_re, c_im = _cmul(m_re, m_im, pltpu.roll(e_re, amt, 0), pltpu.roll(e_im, amt, 0))
        ok = (row2 >= sh) if not rev else (row2 < n2 - sh)
        e_re = e_re + jnp.where(ok, c_re, 0.0)
        e_im = e_im + jnp.where(ok, c_im, 0.0)
        sh, li = sh * 2, li + 1
    es_ref[0] = e_re
    es_ref[1] = e_im
    last = 0 if rev else n2 - 1
    t_re, t_im = _cmul(tab_ref[0, 16:17, :], tab_ref[1, 16:17, :], hin_re, hin_im)
    hout_re = es_ref[0, last:last + 1, :] + t_re
    hout_im = es_ref[1, last:last + 1, :] + t_im
    amt = 1 if not rev else n2 - 1
    ok = (row2 >= 1) if not rev else (row2 < n2 - 1)
    k_re, k_im = _cmul(tab_ref[0, 24:24 + n2, :], tab_ref[1, 24:24 + n2, :], hin_re, hin_im)
    c_re = jnp.where(ok, pltpu.roll(e_re, amt, 0), 0.0) + k_re
    c_im = jnp.where(ok, pltpu.roll(e_im, amt, 0), 0.0) + k_im
    tp_re, tp_im = tab_ref[0, 8:16, :][None], tab_ref[1, 8:16, :][None]
    add_re, add_im = _cmul(tp_re, tp_im, c_re[:, None, :], c_im[:, None, :])
    h_re = xr_ref[...] + add_re.reshape(t_len, lanes)
    h_im = xi_ref[...] + add_im.reshape(t_len, lanes)
    return h_re, h_im, hout_re, hout_im


def _s5_order(j, n_chunks, nc, rev):
    if not rev:
        return j
    return jnp.where(j < nc, nc - 1 - j, n_chunks - 1 - (j - nc))


def _s5_fwd(u, tab, b_bd, c_bd, n_ctx, rev):
    b, s, w = u.shape
    lanes = b_bd.shape[-1]
    t_len = _s5_chunk(n_ctx)
    n_chunks, nc = s // t_len, n_ctx // t_len

    def body(u_ref, tab_ref, b_ref, c_ref, y_ref, h_ref, hin_ref, carry_ref, xr_ref, xi_ref, es_ref):
        xs_ref = (xr_ref, xi_ref)

        @pl.when(pl.program_id(1) == 0)
        def _():
            carry_ref[...] = jnp.zeros_like(carry_ref)

        ub = u_ref[...].astype(BF16)
        x_re = jnp.dot(ub, b_ref[0].astype(BF16), preferred_element_type=F32)
        x_im = jnp.dot(ub, b_ref[1].astype(BF16), preferred_element_type=F32)
        hin_re, hin_im = carry_ref[0, 0:1, :], carry_ref[1, 0:1, :]
        hin_ref[...] = carry_ref[...]
        h_re, h_im, ho_re, ho_im = _scan_chunk(x_re, x_im, tab_ref, hin_re, hin_im, rev, t_len, xs_ref, es_ref)
        carry_ref[0] = jnp.broadcast_to(ho_re, (8, lanes))
        carry_ref[1] = jnp.broadcast_to(ho_im, (8, lanes))
        h_ref[0] = h_re
        h_ref[1] = h_im
        y_ref[...] = (jnp.dot(h_re.astype(BF16), c_ref[0].astype(BF16), preferred_element_type=F32)
                      - jnp.dot(h_im.astype(BF16), c_ref[1].astype(BF16), preferred_element_type=F32))

    order = lambda j: _s5_order(j, n_chunks, nc, rev)
    whole = lambda arr: pl.BlockSpec(arr.shape, lambda bi, j: (0,) * arr.ndim)
    return pl.pallas_call(
        body,
        out_shape=[jax.ShapeDtypeStruct((b, s, w), F32), jax.ShapeDtypeStruct((2, b, s, lanes), F32),
                   jax.ShapeDtypeStruct((2, b, n_chunks, 8, lanes), F32)],
        grid=(b, n_chunks),
        in_specs=[pl.BlockSpec((None, t_len, w), lambda bi, j: (bi, order(j), 0)), whole(tab), whole(b_bd), whole(c_bd)],
        out_specs=[pl.BlockSpec((None, t_len, w), lambda bi, j: (bi, order(j), 0)),
                   pl.BlockSpec((2, None, t_len, lanes), lambda bi, j: (0, bi, order(j), 0)),
                   pl.BlockSpec((2, None, None, 8, lanes), lambda bi, j: (0, bi, order(j), 0, 0))],
        scratch_shapes=[pltpu.VMEM((2, 8, lanes), F32), pltpu.VMEM((t_len, LANE), F32), pltpu.VMEM((t_len, LANE), F32),
                        pltpu.VMEM((2, t_len // 8, LANE), F32)],
        name=f"s5_f_{s}_{int(rev)}", compiler_params=_params(),
    )(u, tab, b_bd, c_bd)


def _s5_bwd(u, tab_adj, b_bd, c_bd, h, hin, dy, n_ctx, rev):
    b, s, w = u.shape
    lanes = b_bd.shape[-1]
    t_len = _s5_chunk(n_ctx)
    n_chunks, nc = s // t_len, n_ctx // t_len
    arev = not rev

    def body(u_ref, tab_ref, b_ref, c_ref, h_ref, hin_ref, dy_ref, du_ref, db_ref, dc_ref, da_ref,
             carry_ref, xr_ref, xi_ref, es_ref):
        xs_ref = (xr_ref, xi_ref)
        first = jnp.logical_and(pl.program_id(0) == 0, pl.program_id(1) == 0)

        @pl.when(pl.program_id(1) == 0)
        def _():
            carry_ref[...] = jnp.zeros_like(carry_ref)

        dyv = dy_ref[...]
        dyb = dyv.astype(BF16)
        dn = (((1,), (1,)), ((), ()))
        dt = (((0,), (0,)), ((), ()))
        x_re = lax.dot_general(dyb, c_ref[0].astype(BF16), dn, preferred_element_type=F32)
        x_im = -lax.dot_general(dyb, c_ref[1].astype(BF16), dn, preferred_element_type=F32)
        g_re, g_im, go_re, go_im = _scan_chunk(x_re, x_im, tab_ref, carry_ref[0, 0:1, :], carry_ref[1, 0:1, :],
                                               arev, t_len, xs_ref, es_ref)
        carry_ref[0] = jnp.broadcast_to(go_re, (8, lanes))
        carry_ref[1] = jnp.broadcast_to(go_im, (8, lanes))
        h_re, h_im = h_ref[0], h_ref[1]
        gb_re, gb_im = g_re.astype(BF16), g_im.astype(BF16)
        du_ref[...] = (lax.dot_general(gb_re, b_ref[0].astype(BF16), dn, preferred_element_type=F32)
                       + lax.dot_general(gb_im, b_ref[1].astype(BF16), dn, preferred_element_type=F32))
        ub = u_ref[...].astype(BF16)
        db_re = lax.dot_general(ub, gb_re, dt, preferred_element_type=F32)
        db_im = lax.dot_general(ub, gb_im, dt, preferred_element_type=F32)
        dc_re = lax.dot_general(h_re.astype(BF16), dyb, dt, preferred_element_type=F32)
        dc_im = -lax.dot_general(h_im.astype(BF16), dyb, dt, preferred_element_type=F32)
        row = lax.broadcasted_iota(jnp.int32, (t_len, lanes), 0)
        amt = 1 if not rev else t_len - 1
        edge = (row == 0) if not rev else (row == t_len - 1)
        hp_re = jnp.where(edge, hin_ref[0, 0:1, :], pltpu.roll(h_re, amt, 0))
        hp_im = jnp.where(edge, hin_ref[1, 0:1, :], pltpu.roll(h_im, amt, 0))
        da_re = jnp.sum(g_re * hp_re + g_im * hp_im, axis=0, keepdims=True)
        da_im = jnp.sum(g_im * hp_re - g_re * hp_im, axis=0, keepdims=True)

        @pl.when(first)
        def _():
            db_ref[0], db_ref[1] = db_re, db_im
            dc_ref[0], dc_ref[1] = dc_re, dc_im
            da_ref[0] = jnp.broadcast_to(da_re, (8, lanes))
            da_ref[1] = jnp.broadcast_to(da_im, (8, lanes))

        @pl.when(jnp.logical_not(first))
        def _():
            db_ref[0] += db_re
            db_ref[1] += db_im
            dc_ref[0] += dc_re
            dc_ref[1] += dc_im
            da_ref[0] += jnp.broadcast_to(da_re, (8, lanes))
            da_ref[1] += jnp.broadcast_to(da_im, (8, lanes))

    order = lambda j: _s5_order(n_chunks - 1 - j, n_chunks, nc, rev)
    whole = lambda arr: pl.BlockSpec(arr.shape, lambda bi, j: (0,) * arr.ndim)
    us = pl.BlockSpec((None, t_len, w), lambda bi, j: (bi, order(j), 0))
    return pl.pallas_call(
        body,
        out_shape=[jax.ShapeDtypeStruct((b, s, w), F32), jax.ShapeDtypeStruct(b_bd.shape, F32),
                   jax.ShapeDtypeStruct(c_bd.shape, F32), jax.ShapeDtypeStruct((2, 8, lanes), F32)],
        grid=(b, n_chunks),
        in_specs=[us, whole(tab_adj), whole(b_bd), whole(c_bd),
                  pl.BlockSpec((2, None, t_len, lanes), lambda bi, j: (0, bi, order(j), 0)),
                  pl.BlockSpec((2, None, None, 8, lanes), lambda bi, j: (0, bi, order(j), 0, 0)), us],
        out_specs=[us, whole(b_bd), whole(c_bd), pl.BlockSpec((2, 8, lanes), lambda bi, j: (0, 0, 0))],
        scratch_shapes=[pltpu.VMEM((2, 8, lanes), F32), pltpu.VMEM((t_len, LANE), F32), pltpu.VMEM((t_len, LANE), F32),
                        pltpu.VMEM((2, t_len // 8, LANE), F32)],
        name=f"s5_b_{s}_{int(rev)}", compiler_params=_params(),
    )(u, tab_adj, b_bd, c_bd, h, hin, dy)


@functools.partial(jax.custom_vjp, nondiff_argnums=(4, 5))
def s5_direction(u, a, b_bd, c_bd, n_ctx, rev):
    t_len = _s5_chunk(n_ctx)
    return _s5_fwd(u, _s5_tables(_s5_powers(a[0], a[1], t_len), t_len, rev), b_bd, c_bd, n_ctx, rev)[0]


def _s5_direction_fwd(u, a, b_bd, c_bd, n_ctx, rev):
    t_len = _s5_chunk(n_ctx)
    pw = _s5_powers(a[0], a[1], t_len)
    y, h, hin = _s5_fwd(u, _s5_tables(pw, t_len, rev), b_bd, c_bd, n_ctx, rev)
    return y, (u, pw, b_bd, c_bd, h, hin)


def _s5_direction_bwd(n_ctx, rev, res, dy):
    u, pw, b_bd, c_bd, h, hin = res
    tab_adj = _s5_tables(pw, _s5_chunk(n_ctx), not rev, conj=True)
    du, db, dc, da = _s5_bwd(u, tab_adj, b_bd, c_bd, h, hin, dy, n_ctx, rev)
    return du, da[:, 0, :], db, dc


s5_direction.defvjp(_s5_direction_fwd, _s5_direction_bwd)


def _s5_discretize(lam_re, lam_im, log_dt, b_re, b_im):
    dt = jnp.exp(log_dt)[:, None]
    mag = jnp.exp(lam_re * dt)
    a_re = mag * jnp.cos(lam_im * dt)
    a_im = mag * jnp.sin(lam_im * dt)
    den = jnp.square(lam_re) + jnp.square(lam_im)
    f_re = ((a_re - 1.0) * lam_re + a_im * lam_im) / den
    f_im = (a_im * lam_re - (a_re - 1.0) * lam_im) / den
    bb_re = f_re[..., None] * b_re - f_im[..., None] * b_im
    bb_im = f_re[..., None] * b_im + f_im[..., None] * b_re
    return a_re, a_im, bb_re, bb_im


def _block_diag(t):
    g, r, c = t.shape
    return (jnp.eye(g, dtype=F32)[:, None, :, None] * t[:, :, None, :]).reshape(g * r, g * c)


def _loss_head(y, target):
    b, n, d = y.shape
    ts = _pick(n, (256, 128, 64))

    def body(y_ref, t_ref, loss_ref, dy_ref):
        first = jnp.logical_and(pl.program_id(0) == 0, pl.program_id(1) == 0)
        err = y_ref[...] - t_ref[...]
        dy_ref[...] = err * (1.0 / d)
        part = 0.5 * jnp.sum(jnp.sum(err * err, axis=-1, keepdims=True) * (1.0 / d), axis=0, keepdims=True)
        part = jnp.broadcast_to(part, (8, LANE))

        @pl.when(first)
        def _():
            loss_ref[...] = part

        @pl.when(jnp.logical_not(first))
        def _():
            loss_ref[...] += part

    blk = pl.BlockSpec((None, ts, d), lambda bi, i: (bi, i, 0))
    return pl.pallas_call(
        body, out_shape=[jax.ShapeDtypeStruct((8, LANE), F32), jax.ShapeDtypeStruct((b, n, d), F32)],
        grid=(b, n // ts), in_specs=[blk, blk], out_specs=[pl.BlockSpec((8, LANE), lambda bi, i: (0, 0)), blk],
        name="loss_head", compiler_params=_params(),
    )(y, target)


def _adamw(w, g, m, v):
    shape = w.shape
    n = int(np.prod(shape))
    cols = shape[-1]
    r = n // cols
    tr = _pick(r, (512, 256, 128, 64, 32, 16, 8))
    c1 = 1.0 / (1.0 - ADAM_B1 ** ADAM_STEP)
    c2 = 1.0 / (1.0 - ADAM_B2 ** ADAM_STEP)

    def body(w_ref, g_ref, m_ref, v_ref, d_ref, mo_ref, vo_ref):
        gv = g_ref[...]
        m2 = ADAM_B1 * m_ref[...] + (1.0 - ADAM_B1) * gv
        v2 = ADAM_B2 * v_ref[...] + (1.0 - ADAM_B2) * (gv * gv)
        d_ref[...] = -ADAM_LR * ((m2 * c1) / (jnp.sqrt(v2 * c2) + ADAM_EPS) + ADAM_WD * w_ref[...])
        mo_ref[...] = m2
        vo_ref[...] = v2

    blk = pl.BlockSpec((tr, cols), lambda i: (i, 0))
    outs = pl.pallas_call(
        body, out_shape=[jax.ShapeDtypeStruct((r, cols), F32)] * 3, grid=(r // tr,),
        in_specs=[blk] * 4, out_specs=[blk] * 3, name=f"adamw_{r}x{cols}", compiler_params=_params(),
    )(*[t.reshape(r, cols) for t in (w, g, m, v)])
    return tuple(o.reshape(shape) for o in outs)


def _sum_rows(x, n):
    _, r, c = x.shape
    tr = _pick(r, (512, 256, 128, 64, 32, 16, 8))

    def body(x_ref, o_ref):
        acc = x_ref[0]
        for j in range(1, n):
            acc = acc + x_ref[j]
        o_ref[...] = acc

    return pl.pallas_call(
        body, out_shape=jax.ShapeDtypeStruct((r, c), F32), grid=(r // tr,),
        in_specs=[pl.BlockSpec((n, tr, c), lambda i: (0, i, 0))], out_specs=pl.BlockSpec((tr, c), lambda i: (i, 0)),
        name=f"sum{n}_{r}x{c}", compiler_params=_params(),
    )(x)


def _accumulate(parts, out_dtype):
    r, c = parts[0].shape[-2:]
    tr = _pick(r, (512, 256, 128, 64, 32, 16))

    def body(*refs):
        acc = None
        for ref in refs[:-1]:
            terms = [ref[j] for j in range(ref.shape[0])] if len(ref.shape) == 3 else [ref[...]]
            for t in terms:
                acc = t.astype(F32) if acc is None else acc + t.astype(F32)
        refs[-1][...] = acc.astype(out_dtype)

    specs = [pl.BlockSpec((p.shape[0], tr, c), lambda i: (0, i, 0)) if p.ndim == 3 else pl.BlockSpec((tr, c), lambda i: (i, 0))
             for p in parts]
    tag = "_".join(str(p.shape[0]) if p.ndim == 3 else "1" for p in parts)
    return pl.pallas_call(
        body, out_shape=jax.ShapeDtypeStruct((r, c), out_dtype), grid=(r // tr,), in_specs=specs,
        out_specs=pl.BlockSpec((tr, c), lambda i: (i, 0)), name=f"accumulate_{tag}_{r}x{c}_{jnp.dtype(out_dtype).name}",
        compiler_params=_params(),
    )(*parts)


def _add2(x, y):
    shape = x.shape
    c = shape[-1]
    r = int(np.prod(shape)) // c
    tr = _pick(r, (512, 256, 128, 64, 32, 16, 8))

    def body(x_ref, y_ref, o_ref):
        o_ref[...] = x_ref[...] + y_ref[...]

    blk = pl.BlockSpec((tr, c), lambda i: (i, 0))
    return pl.pallas_call(
        body, out_shape=jax.ShapeDtypeStruct((r, c), F32), grid=(r // tr,), in_specs=[blk, blk], out_specs=blk,
        name=f"add2_{r}x{c}", compiler_params=_params(),
    )(x.reshape(r, c), y.reshape(r, c)).reshape(shape)


_FLIPS = ((1, 0), (0, 1), (1, 1))


def _me():
    return lax.axis_index("x"), lax.axis_index("y"), lax.axis_index("c")


def allgather8(v):
    m_per, n = v.shape

    def body(x_ref, out_ref, send_sems, recv_sems, local_sem):
        x, y, c = _me()
        me, sibling = (x, y, c), (x, y, 1 - c)
        chips = [(1 - x, y), (x, 1 - y), (1 - x, 1 - y)]

        def rows(px, py, pc):
            return out_ref.at[pl.ds((4 * px + 2 * py + pc) * m_per, m_per), :]

        def copy(k, block, to, src=None):
            return pltpu.make_async_remote_copy(
                src_ref=rows(*block) if src is None else src, dst_ref=rows(*block),
                send_sem=send_sems.at[k], recv_sem=recv_sems.at[k], device_id=to, device_id_type=MESH)

        mine = pltpu.make_async_copy(x_ref, rows(*me), local_sem)
        mine.start()
        first = [copy(0, me, sibling, src=x_ref)]
        first += [copy(1 + j, me, (*chip, c), src=x_ref) for j, chip in enumerate(chips)]
        for cp in first:
            cp.start()
        passed = [copy(4 + j, (*chip, c), sibling) for j, chip in enumerate(chips)]
        for j, chip in enumerate(chips):
            copy(1 + j, (*chip, c), me).wait_recv()
            passed[j].start()
        copy(0, sibling, me).wait_recv()
        for j, chip in enumerate(chips):
            copy(4 + j, (*chip, 1 - c), me).wait_recv()
        for cp in first + passed:
            cp.wait_send()
        mine.wait()

    return pl.pallas_call(
        body, out_shape=jax.ShapeDtypeStruct((N_DEV * m_per, n), v.dtype), in_specs=[VMEM_SPEC], out_specs=VMEM_SPEC,
        scratch_shapes=[pltpu.SemaphoreType.DMA((7,)), pltpu.SemaphoreType.DMA((7,)), pltpu.SemaphoreType.DMA],
        name=f"allgather8_{m_per}x{n}", compiler_params=_params(),
    )(v)


def _row_chunks(rows, tile_rows, want):
    n = want
    while n > 1 and rows % (n * tile_rows):
        n //= 2
    return [(i * (rows // n), rows // n) for i in range(n)]


def _remote(src, dst, send_sem, recv_sem, to):
    return pltpu.make_async_remote_copy(src_ref=src, dst_ref=dst, send_sem=send_sem, recv_sem=recv_sem, device_id=to,
                                        device_id_type=MESH)


def plane_allgather(big, small):
    rows = big.shape[0]
    rh = rows // 2
    tile = 16 if big.dtype == BF16 else 8
    ch_full = _row_chunks(rows, tile, 8)
    ch_half = _row_chunks(rh, tile, 4)

    def body(big_ref, small_ref, obig_ref, osmall_ref, send_sems, recv_sems, fwd_send, fwd_recv, own_send, own_recv):
        x, y, c = _me()
        me = 2 * x + y
        sibling = (x, y, 1 - c)
        mine = pl.ds(c * rh, rh)
        other = pl.ds((1 - c) * rh, rh)
        peers = [((x + fx) & 1, (y + fy) & 1) for fx, fy in _FLIPS]
        for st, sz in ch_full:
            sl = pl.ds(st, sz)
            _remote(big_ref.at[sl], obig_ref.at[me, sl], own_send.at[0], own_recv.at[0], sibling).start()
        _remote(small_ref, osmall_ref.at[me], own_send.at[1], own_recv.at[1], sibling).start()
        for j, (px, py) in enumerate(peers):
            for st, sz in ch_half:
                sl = pl.ds(c * rh + st, sz)
                _remote(big_ref.at[sl], obig_ref.at[me, sl], send_sems.at[j], recv_sems.at[j], (px, py, c)).start()
            _remote(small_ref, osmall_ref.at[me], send_sems.at[3 + j], recv_sems.at[3 + j], (px, py, c)).start()
        for j, (px, py) in enumerate(peers):
            pidx = 2 * px + py
            _remote(big_ref.at[mine], obig_ref.at[pidx, mine], send_sems.at[j], recv_sems.at[j], (px, py, c)).wait_recv()
            for st, sz in ch_half:
                sl = pl.ds(c * rh + st, sz)
                _remote(obig_ref.at[pidx, sl], obig_ref.at[pidx, sl], fwd_send.at[j], fwd_recv.at[j], sibling).start()
            _remote(small_ref, osmall_ref.at[pidx], send_sems.at[3 + j], recv_sems.at[3 + j], (px, py, c)).wait_recv()
        for j, (px, py) in enumerate(peers):
            pidx = 2 * px + py
            _remote(obig_ref.at[pidx, other], obig_ref.at[pidx, other], fwd_send.at[j], fwd_recv.at[j], sibling).wait_recv()
        for j, (px, py) in enumerate(peers):
            pidx = 2 * px + py
            _remote(big_ref.at[mine], obig_ref.at[me, mine], send_sems.at[j], recv_sems.at[j], (px, py, c)).wait_send()
            _remote(small_ref, osmall_ref.at[me], send_sems.at[3 + j], recv_sems.at[3 + j], (px, py, c)).wait_send()
            _remote(obig_ref.at[pidx, mine], obig_ref.at[pidx, mine], fwd_send.at[j], fwd_recv.at[j], sibling).wait_send()
        _remote(big_ref, obig_ref.at[me], own_send.at[0], own_recv.at[0], sibling).wait()
        _remote(small_ref, osmall_ref.at[me], own_send.at[1], own_recv.at[1], sibling).wait()

    return pl.pallas_call(
        body, out_shape=[jax.ShapeDtypeStruct((N_PLANE,) + big.shape, big.dtype),
                         jax.ShapeDtypeStruct((N_PLANE,) + small.shape, small.dtype)],
        in_specs=[ANY, ANY], out_specs=[ANY, ANY],
        scratch_shapes=[pltpu.SemaphoreType.DMA((6,)), pltpu.SemaphoreType.DMA((6,)), pltpu.SemaphoreType.DMA((3,)),
                        pltpu.SemaphoreType.DMA((3,)), pltpu.SemaphoreType.DMA((2,)), pltpu.SemaphoreType.DMA((2,))],
        name="plane_allgather", compiler_params=_params(),
    )(big, small)


def plane_scatter(p):
    tile = 16 if p.dtype == BF16 else 8
    chunks = _row_chunks(p.shape[1], tile, 4)

    def body(p_ref, out_ref, send_sems, recv_sems):
        x, y, c = _me()
        peers = [((x + fx) & 1, (y + fy) & 1) for fx, fy in _FLIPS]
        for j, (px, py) in enumerate(peers):
            for st, sz in chunks:
                sl = pl.ds(st, sz)
                _remote(p_ref.at[2 * px + py, sl], out_ref.at[j, sl], send_sems.at[j], recv_sems.at[j], (px, py, c)).start()
        for j, (px, py) in enumerate(peers):
            _remote(p_ref.at[0], out_ref.at[j], send_sems.at[j], recv_sems.at[j], (px, py, c)).wait_recv()
        for j, (px, py) in enumerate(peers):
            _remote(p_ref.at[0], out_ref.at[j], send_sems.at[j], recv_sems.at[j], (px, py, c)).wait_send()

    return pl.pallas_call(
        body, out_shape=jax.ShapeDtypeStruct((len(_FLIPS),) + p.shape[1:], p.dtype), in_specs=[ANY], out_specs=ANY,
        scratch_shapes=[pltpu.SemaphoreType.DMA((3,)), pltpu.SemaphoreType.DMA((3,))],
        name="plane_scatter", compiler_params=_params(),
    )(p)


def sibling_halves(buf):
    n_blk, _, rows, cols = buf.shape
    tile = 16 if buf.dtype == BF16 else 8
    chunks = _row_chunks(rows, tile, 2)

    def body(buf_ref, got_ref, send_sem, recv_sem):
        x, y, c = _me()
        for j in range(n_blk):
            for st, sz in chunks:
                sl = pl.ds(st, sz)
                _remote(buf_ref.at[j, 1 - c, sl], got_ref.at[j, sl], send_sem, recv_sem, (x, y, 1 - c)).start()
        _remote(got_ref, got_ref, send_sem, recv_sem, (x, y, 1 - c)).wait()

    return pl.pallas_call(
        body, out_shape=jax.ShapeDtypeStruct((n_blk, rows, cols), buf.dtype), in_specs=[ANY], out_specs=ANY,
        scratch_shapes=[pltpu.SemaphoreType.DMA, pltpu.SemaphoreType.DMA],
        name="sibling_halves", compiler_params=_params(),
    )(buf)


def sibling_swap(s):
    tile = 16 if s.dtype == BF16 else 8
    chunks = _row_chunks(s.shape[0], tile, 8)

    def body(s_ref, got_ref, send_sem, recv_sem):
        x, y, c = _me()
        for st, sz in chunks:
            sl = pl.ds(st, sz)
            _remote(s_ref.at[sl], got_ref.at[sl], send_sem, recv_sem, (x, y, 1 - c)).start()
        _remote(s_ref, got_ref, send_sem, recv_sem, (x, y, 1 - c)).wait()

    return pl.pallas_call(
        body, out_shape=jax.ShapeDtypeStruct(s.shape, s.dtype), in_specs=[ANY], out_specs=ANY,
        scratch_shapes=[pltpu.SemaphoreType.DMA, pltpu.SemaphoreType.DMA],
        name="sibling_swap", compiler_params=_params(),
    )(s)


def _op(cache, fn, name, kinds, out_dims, **kw):
    key = (name, tuple(out_dims), tuple(sorted(kw.items())))
    if key not in cache:
        cache[key] = make_rowwise(fn, name, kinds, out_dims, **kw)
    return cache[key]


def _even_mixer(ops, a, w, n_ctx):
    b, s, d = a.shape
    proj = linear(a.reshape(b * s, d), w["e_w_in"]).reshape(b, s, -1)
    q, k, v, u = jnp.split(proj, [GQA_Q_W, GQA_Q_W + GQA_KV_W, GQA_Q_W + 2 * GQA_KV_W], axis=-1)
    cos, sin = _rope_tables(n_ctx, s - n_ctx, HEAD_DIM, 0, HEAD_DIM)
    shift = HEAD_DIM // 4
    qn = head_norm_rope(q, w["e_g_q"][None], jnp.tile(cos, (1, GQA_Q_HEADS)), jnp.tile(sin, (1, GQA_Q_HEADS)),
                        jnp.asarray(_rope_signs(HEAD_DIM, 0, HEAD_DIM, GQA_Q_HEADS)), GQA_Q_HEADS, shift)
    kn = head_norm_rope(k, w["e_g_k"][None], jnp.tile(cos, (1, GQA_KV_HEADS)), jnp.tile(sin, (1, GQA_KV_HEADS)),
                        jnp.asarray(_rope_signs(HEAD_DIM, 0, HEAD_DIM, GQA_KV_HEADS)), GQA_KV_HEADS, shift)
    att = gqa_attention(qn, kn, v, n_ctx)
    ys = []
    for dr in range(2):
        a_re, a_im, bb_re, bb_im = _s5_discretize(w["ssm_lam_re"][dr], w["ssm_lam_im"][dr], w["ssm_log_dt"][dr],
                                                  w["ssm_b_re"][dr], w["ssm_b_im"][dr])
        a_flat = jnp.stack([a_re.reshape(-1), a_im.reshape(-1)])
        b_bd = jnp.stack([_block_diag(jnp.swapaxes(bb_re, 1, 2)), _block_diag(jnp.swapaxes(bb_im, 1, 2))])
        c_bd = jnp.stack([_block_diag(jnp.swapaxes(w["ssm_c_re"][dr], 1, 2)),
                          _block_diag(jnp.swapaxes(w["ssm_c_im"][dr], 1, 2))])
        ys.append(s5_direction(u, a_flat, b_bd, c_bd, n_ctx, dr == 1))
    pre = _op(ops, _fn_glu_pre, "glu_pre", ("row", "row", "row", "glob"), (SSM_WIDTH,))
    post = _op(ops, _fn_glu_post, "glu_post", ("row", "row", "glob"), (SSM_WIDTH,))
    z = pre(u, ys[0], ys[1], w["ssm_d"][None])[0]
    t = linear(z.reshape(b * s, SSM_WIDTH), w["ssm_w_glu"]).reshape(b, s, SSM_WIDTH)
    ssm = post(z, t, w["ssm_b_glu"][None])[0]
    mix = jnp.concatenate([att, ssm], axis=-1)
    return linear(mix.reshape(b * s, -1), w["e_w_out"]).reshape(b, s, d)


def _odd_mixer(ops, a, w, n_ctx):
    b, s, d = a.shape
    w_in = jnp.pad(w["o_w_in"], ((0, 0), (0, ODD_IN_PAD - ODD_IN_W)))
    proj = linear(a.reshape(b * s, d), w_in).reshape(b, s, -1)
    c1 = MLA_Q_RANK
    c2 = c1 + MLA_KV_RANK
    c3 = c2 + MLA_ROPE
    cq, ckv, kr, nq, nk, nv, _ = jnp.split(proj, [c1, c2, c3, c3 + NA_W, c3 + 2 * NA_W, ODD_IN_W], axis=-1)
    nrm = lambda wd: _op(ops, _fn_norm, f"norm{wd}", ("row", "glob"), (wd,))
    cqn = nrm(MLA_Q_RANK)(cq, w["mla_g_cq"][None])[0]
    ckvn = nrm(MLA_KV_RANK)(ckv, w["mla_g_ckv"][None])[0]
    q = linear(cqn.reshape(b * s, -1), w["mla_w_uq"]).reshape(b, s, -1)
    kv = linear(ckvn.reshape(b * s, -1), w["mla_w_ukv"]).reshape(b, s, -1)
    cos, sin = _rope_tables(n_ctx, s - n_ctx, MLA_QK, MLA_NOPE, MLA_ROPE)
    cos, sin = jnp.tile(cos, (1, MLA_HEADS)), jnp.tile(sin, (1, MLA_HEADS))
    sg = jnp.asarray(_rope_signs(MLA_QK, MLA_NOPE, MLA_ROPE, MLA_HEADS))
    mq = head_norm_rope(q, w["mla_g_q"][None], cos, sin, sg, MLA_HEADS, MLA_ROPE // 4)
    mk = mla_k_prep(kv, kr, w["mla_g_k"][None], cos, sin, sg)
    mla = mla_attention(mq, mk, kv, n_ctx)
    nqn = head_norm_rope(nq, w["na_g_q"][None], None, None, None, NA_HEADS, 0)
    nkn = head_norm_rope(nk, w["na_g_k"][None], None, None, None, NA_HEADS, 0)
    na = na_attention_tm(nqn, nkn, nv, na_bias_table(w["na_rpb"]), n_ctx)
    mix = jnp.concatenate([mla, na], axis=-1)
    return linear(mix.reshape(b * s, -1), w["o_w_out"]).reshape(b, s, d)


_EVEN_KEYS = ("e_w_in", "e_w_out", "e_g_q", "e_g_k", "ssm_lam_re", "ssm_lam_im", "ssm_log_dt", "ssm_b_re", "ssm_b_im",
              "ssm_c_re", "ssm_c_im", "ssm_d", "ssm_w_glu", "ssm_b_glu")
_ODD_KEYS = ("o_w_in", "o_w_out", "mla_g_cq", "mla_g_ckv", "mla_w_uq", "mla_w_ukv", "mla_g_q", "mla_g_k", "na_g_q",
             "na_g_k", "na_rpb")


def _trunk(x_all, mods, w, n_ctx):
    ops = {}
    depth = mods.shape[0]
    b, s, d = x_all.shape
    modulate = make_modulate(d, n_ctx)
    gated = make_gated_add(d, n_ctx)
    x = x_all
    for i in range(depth):
        j = i // 2
        m = [mods[i][:, :, r:r + 1, :] for r in range(N_MOD)]
        a, x = modulate(x, w["g_norm1"][i][None], m[0], m[1])
        if i % 2 == 0:
            o = _even_mixer(ops, a, {k: w[k][j] for k in _EVEN_KEYS}, n_ctx)
        else:
            o = _odd_mixer(ops, a, {k: w[k][j] for k in _ODD_KEYS}, n_ctx)
        x = gated(x, o, m[2])
        a2, x = modulate(x, w["g_norm2"][i][None], m[3], m[4])
        f = ffn(a2.reshape(b * s, d), w["w_ff1"][i], w["w_ff2"][i]).reshape(b, s, d)
        x = gated(x, f, m[5])
    return x[:, n_ctx:]


def local_step(x, ctx, mods, w, loss_target):
    n_ctx = ctx.shape[1]
    x_all = jnp.concatenate([ctx, x], axis=1)
    y, vjp = jax.vjp(lambda xa, md, ww: _trunk(xa, md, ww, n_ctx), x_all, mods, w)
    loss_tile, dy = _loss_head(y, loss_target)
    dx_all, dmods, dw = vjp(dy)
    return loss_tile[0, 0], dx_all[:, n_ctx:], dmods, dw


_SHARDED = (("w_ff1", 2), ("w_ff2", 1), ("e_w_in", 2), ("e_w_out", 1), ("o_w_in", 2), ("o_w_out", 1),
            ("mla_w_uq", 2), ("mla_w_ukv", 2), ("ssm_w_glu", 1))
_SHARDED_SMALL = (("mla_g_cq", 1), ("mla_g_ckv", 1))
_REPLICATED = ("g_norm1", "g_norm2", "e_g_q", "e_g_k", "ssm_lam_re", "ssm_lam_im", "ssm_log_dt", "ssm_b_re", "ssm_b_im",
               "ssm_c_re", "ssm_c_im", "ssm_d", "ssm_b_glu", "mla_g_q", "mla_g_k", "na_g_q", "na_g_k", "na_rpb")
_WEIGHTS = ("c_ctx", "w_mod", "b_mod", "g_norm1", "g_norm2", "w_ff1", "w_ff2", "e_w_in", "e_w_out", "e_g_q", "e_g_k",
            "ssm_lam_re", "ssm_lam_im", "ssm_log_dt", "ssm_b_re", "ssm_b_im", "ssm_c_re", "ssm_c_im", "ssm_d",
            "ssm_w_glu", "ssm_b_glu", "o_w_in", "o_w_out", "mla_g_cq", "mla_g_ckv", "mla_w_uq", "mla_w_ukv", "mla_g_q",
            "mla_g_k", "na_g_q", "na_g_k", "na_rpb")
_PACK_ROWS = 64


def _pack(arrs, dtype, cols=1024, row_mult=_PACK_ROWS):
    blocks, tail, off = [], [], 0
    for a in arrs:
        n = int(np.prod(a.shape))
        if not tail and off % cols == 0 and n % cols == 0:
            blocks.append(a.astype(dtype).reshape(-1, cols))
        else:
            tail.append(a.astype(dtype).reshape(-1))
        off += n
    rows = -(-off // cols)
    pad = (-rows) % row_mult * cols + rows * cols - off
    if tail or pad:
        blocks.append(jnp.concatenate(tail + [jnp.zeros((pad,), dtype)]).reshape(-1, cols))
    return jnp.concatenate(blocks, axis=0)


def _unpack(packed, shapes):
    cols = packed.shape[-1]
    packed = packed.reshape(-1, cols)
    out, off = [], 0
    for sh in shapes:
        n = int(np.prod(sh))
        if off % cols == 0 and n % cols == 0:
            out.append(packed[off // cols:(off + n) // cols].reshape(sh))
        else:
            r0, r1 = off // cols, -(-(off + n) // cols)
            out.append(packed[r0:r1].reshape(-1)[off - r0 * cols:off - r0 * cols + n].reshape(sh))
        off += n
    return out


def kernel(x, c, ctx, c_ctx, w_mod, b_mod, g_norm1, g_norm2, w_ff1, w_ff2, e_w_in, e_w_out, e_g_q, e_g_k, ssm_lam_re, ssm_lam_im, ssm_log_dt, ssm_b_re, ssm_b_im, ssm_c_re, ssm_c_im, ssm_d, ssm_w_glu, ssm_b_glu, o_w_in, o_w_out, mla_g_cq, mla_g_ckv, mla_w_uq, mla_w_ukv, mla_g_q, mla_g_k, na_g_q, na_g_k, na_rpb, loss_target, m_c_ctx, m_w_mod, m_b_mod, m_g_norm1, m_g_norm2, m_w_ff1, m_w_ff2, m_e_w_in, m_e_w_out, m_e_g_q, m_e_g_k, m_ssm_lam_re, m_ssm_lam_im, m_ssm_log_dt, m_ssm_b_re, m_ssm_b_im, m_ssm_c_re, m_ssm_c_im, m_ssm_d, m_ssm_w_glu, m_ssm_b_glu, m_o_w_in, m_o_w_out, m_mla_g_cq, m_mla_g_ckv, m_mla_w_uq, m_mla_w_ukv, m_mla_g_q, m_mla_g_k, m_na_g_q, m_na_g_k, m_na_rpb, v_c_ctx, v_w_mod, v_b_mod, v_g_norm1, v_g_norm2, v_w_ff1, v_w_ff2, v_e_w_in, v_e_w_out, v_e_g_q, v_e_g_k, v_ssm_lam_re, v_ssm_lam_im, v_ssm_log_dt, v_ssm_b_re, v_ssm_b_im, v_ssm_c_re, v_ssm_c_im, v_ssm_d, v_ssm_w_glu, v_ssm_b_glu, v_o_w_in, v_o_w_out, v_mla_g_cq, v_mla_g_ckv, v_mla_w_uq, v_mla_w_ukv, v_mla_g_q, v_mla_g_k, v_na_g_q, v_na_g_k, v_na_rpb):
    env = dict(locals())
    weights = {n: env[n] for n in _WEIGHTS}
    mom_m = {n: env["m_" + n] for n in _WEIGHTS}
    mom_v = {n: env["v_" + n] for n in _WEIGHTS}
    ax, ay, ac = _me()
    plane = 2 * ax + ay
    dev = 4 * ax + 2 * ay + ac
    b_loc, d = c.shape
    depth = w_mod.shape[0]
    n_all = N_DEV * b_loc
    mod_cols = w_mod.shape[2]

    big = _pack([weights[n] for n, _ in _SHARDED], BF16)
    small = _pack([weights[n] for n, _ in _SHARDED_SMALL], F32, cols=LANE, row_mult=8)
    g_big, g_small = plane_allgather(big, small)
    full = {n: weights[n] for n in _REPLICATED}
    parts = [_unpack(g_big[j], [weights[n].shape for n, _ in _SHARDED]) for j in range(N_PLANE)]
    for t, (n, axis) in enumerate(_SHARDED):
        full[n] = [jnp.concatenate([parts[j][t][l] for j in range(N_PLANE)], axis=axis - 1)
                   for l in range(weights[n].shape[0])]
    parts_s = [_unpack(g_small[j], [weights[n].shape for n, _ in _SHARDED_SMALL]) for j in range(N_PLANE)]
    for t, (n, axis) in enumerate(_SHARDED_SMALL):
        full[n] = jnp.concatenate([parts_s[j][t] for j in range(N_PLANE)], axis=axis)

    rows_pad = 8 * ((n_all + 1 + 7) // 8)
    c_all = allgather8(jnp.pad(c, ((0, 8 - b_loc), (0, 0)))).reshape(N_DEV, 8, d)[:, :b_loc].reshape(n_all, d)
    cond_raw = jnp.concatenate([c_all, c_ctx[None], jnp.zeros((rows_pad - n_all - 1, d), F32)], axis=0)
    b_cols = lax.dynamic_slice_in_dim(b_mod, plane * mod_cols, mod_cols, axis=1)
    mod_loc = jnp.stack([_mm(cond_raw, w_mod[i], a_act="silu") + b_cols[i][None] for i in range(depth)])
    mod_g = allgather8(mod_loc.reshape(depth * rows_pad, mod_cols)).reshape(N_PLANE, 2, depth, rows_pad, mod_cols)
    mod_all = jnp.concatenate([mod_g[j, 0] for j in range(N_PLANE)], axis=-1)
    m_lat = lax.dynamic_slice_in_dim(mod_all, dev * b_loc, b_loc, axis=1)
    m_ctx = jnp.broadcast_to(mod_all[:, n_all][:, None], m_lat.shape)
    mods = jnp.stack([m_ctx, m_lat], axis=2).reshape(depth, b_loc, 2, N_MOD, d)

    loss_part, grad_x, dmods, dw = local_step(x, ctx, mods, full, loss_target)

    dm = dmods.reshape(depth, b_loc, 2, N_MOD * d)
    dm_rows = jnp.concatenate([dm[:, :, 1], jnp.sum(dm[:, :, 0], axis=1, keepdims=True)], axis=1)
    rep_shapes = [weights[n].shape for n in _REPLICATED] + [(1,)]
    small_pack = _pack([dm_rows] + [dw[n] for n in _REPLICATED] + [loss_part.reshape(1)], F32, cols=1024, row_mult=8)
    sp_rows = small_pack.shape[0]
    gathered = allgather8(small_pack).reshape(N_DEV, sp_rows, 1024)
    n_dm = depth * (b_loc + 1) * N_MOD * d
    dm_all = gathered.reshape(N_DEV, -1)[:, :n_dm].reshape(N_DEV, depth, b_loc + 1, N_MOD * d)
    rep_sum = _sum_rows(gathered, N_DEV).reshape(-1)
    rep_parts = _unpack(rep_sum[n_dm:], rep_shapes)
    rep_grads = dict(zip(_REPLICATED, rep_parts[:-1]))
    loss = rep_parts[-1][0]
    d_ctx_row = rep_sum[:n_dm].reshape(depth, b_loc + 1, N_MOD * d)[:, b_loc]
    d_lat_rows = jnp.transpose(dm_all[:, :, :b_loc], (1, 0, 2, 3)).reshape(depth, n_all, N_MOD * d)
    d_mod_all = jnp.concatenate([d_lat_rows, d_ctx_row[:, None],
                                 jnp.zeros((depth, rows_pad - n_all - 1, N_MOD * d), F32)], axis=1)
    grads = dict(rep_grads)
    grads["b_mod"] = jnp.sum(d_mod_all, axis=1)
    d_cols = lax.dynamic_slice_in_dim(d_mod_all, plane * mod_cols, mod_cols, axis=2)
    grads["w_mod"] = jnp.stack([_mm(cond_raw, d_cols[i], ta=True, a_act="silu") for i in range(depth)])
    d_cond = _mm(d_cols[0], w_mod[0], tb=True)
    for i in range(1, depth):
        d_cond = _add2(d_cond, _mm(d_cols[i], w_mod[i], tb=True))
    d_cond_g = allgather8(d_cond[n_all:n_all + 8] if rows_pad - n_all >= 8 else
                          jnp.pad(d_cond[n_all:], ((0, 8 - (rows_pad - n_all)), (0, 0)))).reshape(N_PLANE, 2, 8, d)
    d_silu = _sum_rows(d_cond_g[:, 0], N_PLANE)[0]
    sg = jax.nn.sigmoid(c_ctx)
    grads["c_ctx"] = d_silu * (sg * (1.0 + c_ctx * (1.0 - sg)))

    def shards_of(g, axis, j):
        layers = g if isinstance(g, (list, tuple)) else [g]
        ax = axis - 1 if isinstance(g, (list, tuple)) else axis
        n = layers[0].shape[ax] // N_PLANE
        return [lax.slice_in_dim(t, j * n, (j + 1) * n, axis=ax) for t in layers]

    send = jnp.stack([_pack([t for n, axis in _SHARDED + _SHARDED_SMALL for t in shards_of(dw[n], axis, j)], BF16)
                      for j in range(N_PLANE)])
    rows_h = send.shape[1] // 2
    send = send.reshape(N_PLANE, 2, rows_h, 1024)
    mine = lax.dynamic_index_in_dim(send, ac, 1, keepdims=False).reshape(N_PLANE * rows_h, 1024)
    theirs = sibling_halves(send).reshape(N_PLANE * rows_h, 1024)
    chip_sum = _accumulate([mine, theirs], BF16).reshape(N_PLANE, rows_h, 1024)
    own = lax.dynamic_index_in_dim(chip_sum, plane, 0, keepdims=False)
    done = _accumulate([own, plane_scatter(chip_sum)], BF16)
    both = jnp.stack([done, sibling_swap(done)])
    flat = jnp.where(ac == 0, both, both[::-1]).astype(F32).reshape(-1, 1024)
    shard_shapes = [weights[n].shape for n, _ in _SHARDED] + [weights[n].shape for n, _ in _SHARDED_SMALL]
    for (n, _), g in zip(_SHARDED + _SHARDED_SMALL, _unpack(flat, shard_shapes)):
        grads[n] = g

    big_names = ("w_mod",) + tuple(n for n, _ in _SHARDED)
    small_names = tuple(n for n in _WEIGHTS if n not in big_names)
    delta, new_m, new_v = {}, {}, {}
    for n in big_names:
        delta[n], new_m[n], new_v[n] = _adamw(weights[n], grads[n], mom_m[n], mom_v[n])
    sm_shapes = [weights[n].shape for n in small_names]
    packed = [_pack([src[n] for n in small_names], F32, cols=1024, row_mult=8)
              for src in (weights, grads, mom_m, mom_v)]
    for dst, res in zip((delta, new_m, new_v), _adamw(*packed)):
        dst.update(dict(zip(small_names, _unpack(res, sm_shapes))))

    return (loss, grad_x, *[grads[n] for n in _WEIGHTS], *[delta[n] for n in _WEIGHTS],
            *[new_m[n] for n in _WEIGHTS], *[new_v[n] for n in _WEIGHTS])
```

```python
import functools

import numpy as np
import jax
import jax.numpy as jnp
from jax import lax
from jax.experimental import pallas as pl
from jax.experimental.pallas import tpu as pltpu

F32 = jnp.float32
BF16 = jnp.bfloat16
HI = lax.Precision.HIGHEST
MESH = pl.DeviceIdType.MESH
ANY = pl.BlockSpec(memory_space=pl.ANY)
VMEM_SPEC = pl.BlockSpec(memory_space=pltpu.VMEM)

GRID_W = 64
HEAD_DIM = 64
ROPE_BASE = 10000.0
EPS = 1e-6
N_MOD = 6
GQA_Q_HEADS, GQA_KV_HEADS = 12, 4
GQA_Q_W, GQA_KV_W = GQA_Q_HEADS * HEAD_DIM, GQA_KV_HEADS * HEAD_DIM
SSM_WIDTH, SSM_GROUP, SSM_STATE = 256, 16, 64
SSM_GROUPS = SSM_WIDTH // SSM_GROUP
SSM_LANES = SSM_GROUPS * SSM_STATE
MLA_HEADS, MLA_Q_RANK, MLA_KV_RANK, MLA_NOPE, MLA_ROPE, MLA_V = 8, 512, 256, 64, 32, 64
MLA_QK = MLA_NOPE + MLA_ROPE
NA_HEADS, NA_WIN_R, NA_WIN_C = 8, 8, 16
NA_W = NA_HEADS * HEAD_DIM
NA_BAND = NA_WIN_R * GRID_W
ODD_IN_W = MLA_Q_RANK + MLA_KV_RANK + MLA_ROPE + 3 * NA_W
ODD_IN_PAD = 2560
ADAM_LR, ADAM_B1, ADAM_B2, ADAM_EPS, ADAM_WD, ADAM_STEP = 0.001, 0.9, 0.999, 1e-08, 0.01, 10
NEG = -1e30
VMEM_LIMIT = 56 * 1024 * 1024
LANE = 128
MM_TILE_M = (1152, 1024, 768, 512, 256, 128)
MM_TILE_N = (1280, 1024, 768, 512, 256, 128)
MM_TILE_K = (1152, 1024, 768, 512, 256, 128)
ROW_TILES = (576, 512, 384, 256, 128, 64)
N_PLANE = 4
N_DEV = 8


def _pick(n, cands):
    for c in cands:
        if n % c == 0:
            return c
    return n


def _params(**kw):
    return pltpu.CompilerParams(vmem_limit_bytes=VMEM_LIMIT, **kw)


def _mm(a, b, *, ta=False, tb=False, a_act=None, epi=None, e=None, exact=False, out_dtype=F32):
    m, kd = (a.shape[1], a.shape[0]) if ta else a.shape
    n = b.shape[0] if tb else b.shape[1]
    tm = _pick(m, MM_TILE_M)
    tn = _pick(n, MM_TILE_N)
    tk = _pick(kd, MM_TILE_K)
    nk = kd // tk
    dn = (((0 if ta else 1,), (1 if tb else 0,)), ((), ()))
    narrow = jnp.dtype(out_dtype) != jnp.dtype(F32)
    assert not (narrow and epi is not None)

    def body(*refs):
        if narrow:
            a_ref, b_ref, out_ref, o_ref = refs
        elif epi is None:
            a_ref, b_ref, o_ref = refs
        else:
            a_ref, b_ref, e_ref, o_ref = refs
        k = pl.program_id(2)
        av = a_ref[...]
        if a_act == "relu2":
            av = jnp.square(jnp.maximum(av, 0.0))
        elif a_act == "silu":
            av = av * jax.nn.sigmoid(av)
        bv = b_ref[...]
        if exact:
            p = lax.dot_general(av, bv, dn, precision=HI, preferred_element_type=F32)
        else:
            p = lax.dot_general(av.astype(BF16), bv.astype(BF16), dn, preferred_element_type=F32)

        @pl.when(k == 0)
        def _():
            o_ref[...] = p

        @pl.when(k > 0)
        def _():
            o_ref[...] += p

        if epi == "drelu2":
            @pl.when(k == nk - 1)
            def _():
                o_ref[...] = o_ref[...] * (2.0 * jnp.maximum(e_ref[...], 0.0))

        if narrow:
            @pl.when(k == nk - 1)
            def _():
                out_ref[...] = o_ref[...].astype(out_dtype)

    a_spec = pl.BlockSpec((tk, tm), lambda i, j, k: (k, i)) if ta else pl.BlockSpec((tm, tk), lambda i, j, k: (i, k))
    b_spec = pl.BlockSpec((tn, tk), lambda i, j, k: (j, k)) if tb else pl.BlockSpec((tk, tn), lambda i, j, k: (k, j))
    o_spec = pl.BlockSpec((tm, tn), lambda i, j, k: (i, j))
    ins, specs = [a, b], [a_spec, b_spec]
    if epi is not None:
        ins.append(e)
        specs.append(o_spec)
    name = f"mm_{m}x{kd}x{n}_{int(ta)}{int(tb)}_{a_act}_{epi}_{int(exact)}_{jnp.dtype(out_dtype).name}"
    return pl.pallas_call(
        body, out_shape=jax.ShapeDtypeStruct((m, n), out_dtype), grid=(m // tm, n // tn, nk),
        in_specs=specs, out_specs=o_spec, name=name, compiler_params=_params(),
        scratch_shapes=[pltpu.VMEM((tm, tn), F32)] if narrow else [],
    )(*ins)


@functools.partial(jax.custom_vjp, nondiff_argnums=(2,))
def _linear(a, w, exact):
    return _mm(a, w, exact=exact)


def _linear_fwd(a, w, exact):
    return _mm(a, w, exact=exact), (a, w)


def _linear_bwd(exact, res, g):
    a, w = res
    return _mm(g, w, tb=True, exact=exact), _mm(a, g, ta=True, exact=exact, out_dtype=w.dtype)


_linear.defvjp(_linear_fwd, _linear_bwd)


def linear(a, w, exact=False):
    return _linear(a, w, exact)


@jax.custom_vjp
def ffn(a, w1, w2):
    return _mm(_mm(a, w1), w2, a_act="relu2")


def _ffn_fwd(a, w1, w2):
    h1 = _mm(a, w1)
    return _mm(h1, w2, a_act="relu2"), (a, w1, w2, h1)


def _ffn_bwd(res, g):
    a, w1, w2, h1 = res
    dh1 = _mm(g, w2, tb=True, epi="drelu2", e=h1)
    dw2 = _mm(h1, g, ta=True, a_act="relu2", out_dtype=w2.dtype)
    return _mm(dh1, w1, tb=True), _mm(a, dh1, ta=True, out_dtype=w1.dtype), dw2


ffn.defvjp(_ffn_fwd, _ffn_bwd)


def make_rowwise(fn, name, kinds, out_dims, nctx_rows=0, whole_seq=False):
    n_in = len(kinds)
    n_out = len(out_dims)
    diff = [i for i, kd in enumerate(kinds) if kd in ("row", "glob", "seg")]
    seg_idx = [i for i, kd in enumerate(kinds) if kd == "seg"]

    def layout(args):
        row0 = args[kinds.index("row")]
        g, s = row0.shape[0], row0.shape[1]
        ts = s if whole_seq else _pick(s, ROW_TILES)
        return g, s, ts, 0

    def spec_of(kind, arr, ts, nctx):
        if kind == "row":
            return pl.BlockSpec((None, ts, arr.shape[2]), lambda g, i: (g, i, 0))
        if kind == "tab":
            return pl.BlockSpec((ts, arr.shape[1]), lambda g, i: (i, 0))
        if kind in ("const", "glob"):
            return pl.BlockSpec(arr.shape, lambda g, i: (0, 0))
        return pl.BlockSpec((None,) + arr.shape[1:], lambda g, i: (g, 0, 0, 0))

    def with_segments(ts):
        if not seg_idx:
            return fn

        def wrapped(*vals):
            rows = pl.program_id(1) * ts + lax.broadcasted_iota(jnp.int32, (ts, 1), 0)
            vals = list(vals)
            for idx in seg_idx:
                vals[idx] = jnp.where(rows < nctx_rows, vals[idx][0], vals[idx][1])
            return fn(*vals)

        return wrapped

    def fwd_call(*args):
        g, s, ts, nctx = layout(args)
        fn = with_segments(ts)

        def body(*refs):
            vals = [r[...] for r in refs[:n_in]]
            outs = fn(*vals)
            for o_ref, o in zip(refs[n_in:], outs):
                o_ref[...] = o

        return pl.pallas_call(
            body, out_shape=[jax.ShapeDtypeStruct((g, s, d), F32) for d in out_dims], grid=(g, s // ts),
            in_specs=[spec_of(kd, a, ts, nctx) for kd, a in zip(kinds, args)],
            out_specs=[pl.BlockSpec((None, ts, d), lambda g_, i: (g_, i, 0)) for d in out_dims],
            name=f"{name}_f_{g}x{s}", compiler_params=_params(),
        )(*args)

    def bwd_call(args, cts):
        g, s, ts, nctx = layout(args)
        fn = with_segments(ts)

        def body(*refs):
            in_refs, ct_refs, out_refs = refs[:n_in], refs[n_in:n_in + n_out], refs[n_in + n_out:]
            gi, i = pl.program_id(0), pl.program_id(1)
            vals = [r[...] for r in in_refs]

            def f(*dv):
                full = list(vals)
                for idx, v in zip(diff, dv):
                    full[idx] = v
                return tuple(fn(*full))

            _, vjp = jax.vjp(f, *[vals[idx] for idx in diff])
            grads = vjp(tuple(r[...] for r in ct_refs))
            for idx, o_ref, gr in zip(diff, out_refs, grads):
                if kinds[idx] == "row":
                    o_ref[...] = gr
                    continue
                if kinds[idx] == "glob":
                    first = jnp.logical_and(gi == 0, i == 0)
                else:
                    first = i == 0

                @pl.when(first)
                def _(o_ref=o_ref, gr=gr):
                    o_ref[...] = gr

                @pl.when(jnp.logical_not(first))
                def _(o_ref=o_ref, gr=gr):
                    o_ref[...] += gr

        in_specs = [spec_of(kd, a, ts, nctx) for kd, a in zip(kinds, args)]
        in_specs += [pl.BlockSpec((None, ts, d), lambda g_, i: (g_, i, 0)) for d in out_dims]
        return pl.pallas_call(
            body, out_shape=[jax.ShapeDtypeStruct(args[idx].shape, F32) for idx in diff], grid=(g, s // ts),
            in_specs=in_specs, out_specs=[spec_of(kinds[idx], args[idx], ts, nctx) for idx in diff],
            name=f"{name}_b_{g}x{s}", compiler_params=_params(),
        )(*args, *cts)

    @jax.custom_vjp
    def op(*args):
        return tuple(fwd_call(*args))

    def op_fwd(*args):
        return tuple(fwd_call(*args)), args

    def op_bwd(args, cts):
        grads = bwd_call(args, cts)
        full = [None] * n_in
        for idx, gr in zip(diff, grads):
            full[idx] = gr
        return tuple(jnp.zeros_like(a) if gfull is None else gfull for a, gfull in zip(args, full))

    op.defvjp(op_fwd, op_bwd)
    op.fwd_call, op.bwd_call = fwd_call, bwd_call
    return op


def make_modulate(d, n_ctx):
    one = make_rowwise(_fn_modulate, "modulate", ("row", "glob", "seg", "seg"), (d,), nctx_rows=n_ctx)
    two = make_rowwise(_fn_modulate_keep, "modulate_keep", ("row", "glob", "seg", "seg"), (d, d), nctx_rows=n_ctx)

    @jax.custom_vjp
    def op(x, g, shift, scale):
        return one.fwd_call(x, g, shift, scale)[0], x

    def fwd(x, g, shift, scale):
        return (one.fwd_call(x, g, shift, scale)[0], x), (x, g, shift, scale)

    def bwd(res, cts):
        return tuple(two.bwd_call(res, cts))

    op.defvjp(fwd, bwd)
    return op


def make_gated_add(d, n_ctx):
    add = make_rowwise(_fn_gated_add, "gated", ("row", "row", "seg"), (d,), nctx_rows=n_ctx)
    mul = make_rowwise(_fn_gate_mul, "gate_mul", ("row", "seg"), (d,), nctx_rows=n_ctx)

    @jax.custom_vjp
    def op(x, o, gate):
        return add.fwd_call(x, o, gate)[0]

    def fwd(x, o, gate):
        return add.fwd_call(x, o, gate)[0], (o, gate)

    def bwd(res, ct):
        do, dgate = mul.bwd_call(res, (ct,))
        return ct, do, dgate

    op.defvjp(fwd, bwd)
    return op


def _rms(x):
    return lax.rsqrt(jnp.mean(x * x, axis=-1, keepdims=True) + EPS)


def _fn_modulate(x, g, shift, scale):
    return ((x * _rms(x) * g) * (1.0 + scale) + shift,)


def _fn_modulate_keep(x, g, shift, scale):
    return _fn_modulate(x, g, shift, scale) + (x,)


def _fn_gated_add(x, o, gate):
    return (x + gate * o,)


def _fn_gate_mul(o, gate):
    return (gate * o,)


def _fn_norm(x, g):
    return (x * _rms(x) * g,)


def _fn_glu_pre(u, y0, y1, d):
    return (jax.nn.gelu(d * u + y0 + y1),)


def _fn_glu_post(z, t, bg):
    return (z * jax.nn.sigmoid(t + bg),)


def _rope_tables(n_ctx, n_lat, dh, start, rot_dim):
    t = jnp.arange(n_lat)
    rows = (t // GRID_W).astype(F32)
    cols = (t % GRID_W).astype(F32)
    axis_dim = rot_dim // 2
    freqs = ROPE_BASE ** (-jnp.arange(0, axis_dim, 2, dtype=F32) / axis_dim)
    ang_r = rows[:, None] * freqs
    ang_c = cols[:, None] * freqs
    ang = jnp.concatenate([ang_r, ang_r, ang_c, ang_c], axis=-1)
    cos = jnp.concatenate([jnp.ones((n_lat, start), F32), jnp.cos(ang)], axis=-1)
    sin = jnp.concatenate([jnp.zeros((n_lat, start), F32), jnp.sin(ang)], axis=-1)
    cos = jnp.concatenate([jnp.ones((n_ctx, dh), F32), cos], axis=0)
    sin = jnp.concatenate([jnp.zeros((n_ctx, dh), F32), sin], axis=0)
    return cos, sin


_NT = (((1,), (1,)), ((), ()))
_TN = (((0,), (0,)), ((), ()))


def _na_geometry(i, nc, rows):
    r = i - nc
    rs = jnp.clip(r - NA_WIN_R // 2, 0, rows - NA_WIN_R)
    is_ctx = i < nc
    cls = jnp.where(is_ctx, NA_WIN_R, r - rs)
    return jnp.where(is_ctx, 0, rs), cls


def _na_onehots():
    q = np.arange(GRID_W)[:, None]
    col = np.arange(GRID_W)[None, :]
    cs = np.clip(q - NA_WIN_C // 2, 0, GRID_W - NA_WIN_C)
    valid = (col >= cs) & (col < cs + NA_WIN_C)
    cidx = col - q + (NA_WIN_C - 1)
    n_b = 2 * NA_WIN_C - 1
    col_hot = np.zeros((LANE, GRID_W * GRID_W), np.float32)
    for qq in range(GRID_W):
        for cc in range(GRID_W):
            if valid[qq, cc]:
                col_hot[cidx[qq, cc], qq * GRID_W + cc] = 1.0
    row_hot = np.zeros((NA_WIN_R, NA_WIN_R, 2 * NA_WIN_R - 1), np.float32)
    for c in range(NA_WIN_R):
        for j in range(NA_WIN_R):
            row_hot[c, j, j - c + NA_WIN_R - 1] = 1.0
    mask = np.where(valid, 0.0, NEG).astype(np.float32)
    return col_hot, row_hot, mask, n_b


def na_bias_table(rpb):
    h = rpb.shape[0]
    col_hot, row_hot, mask, n_b = _na_onehots()
    t1 = jnp.einsum("cja,hab->hcjb", jnp.asarray(row_hot), rpb)
    t1 = jnp.pad(t1.reshape(h * NA_WIN_R * NA_WIN_R, n_b), ((0, 0), (0, LANE - n_b)))
    t2 = linear(t1, jnp.asarray(col_hot), True)
    t2 = t2.reshape(h, NA_WIN_R, NA_WIN_R, GRID_W, GRID_W) + jnp.asarray(mask)
    tab = jnp.transpose(t2, (0, 1, 3, 2, 4)).reshape(h, NA_WIN_R, GRID_W, NA_BAND)
    return jnp.concatenate([tab, jnp.full((h, 1, GRID_W, NA_BAND), NEG, F32)], axis=1)


def _first_step():
    return jnp.logical_and(pl.program_id(0) == 0, pl.program_id(1) == 0)


def _accum_out(ref, val, first):
    @pl.when(first)
    def _():
        ref[...] = val

    @pl.when(jnp.logical_not(first))
    def _():
        ref[...] += val


def _norm_head(xh, g):
    r = _rms(xh)
    yn = xh * r
    return yn * g, yn, r


def _norm_head_bwd(dy, yn, r, g):
    dg = jnp.sum(dy * yn, axis=0, keepdims=True)
    dyn = dy * g
    return r * (dyn - yn * jnp.mean(dyn * yn, axis=-1, keepdims=True)), dg


def _rope_signs(dh, start, rot_dim, n_heads):
    q = rot_dim // 4
    pos = np.arange(dh)
    quarter = (pos - start) // q
    inr = pos >= start
    sg = np.zeros((8, n_heads * dh), np.float32)
    sg[0] = np.tile(np.where(inr & (quarter % 2 == 0), -1.0, 0.0), n_heads)
    sg[1] = np.tile(np.where(inr & (quarter % 2 == 1), 1.0, 0.0), n_heads)
    return sg


def _rope_full(y, cos, sin, sg, q):
    w = y.shape[-1]
    rot = sg[0:1] * pltpu.roll(y, w - q, 1) + sg[1:2] * pltpu.roll(y, q, 1)
    return y * cos + rot * sin


def _rope_full_t(dy, cos, sin, sg, q):
    w = dy.shape[-1]
    z = dy * sin
    return dy * cos - sg[1:2] * pltpu.roll(z, q, 1) - sg[0:1] * pltpu.roll(z, w - q, 1)


def _hnr_call(x, g, cos, sin, sg, n_heads, q, dy=None):
    b, s, w = x.shape
    dh = w // n_heads
    ts = _pick(s, ROW_TILES)
    rope = cos is not None

    def body(*refs):
        refs = list(refs)
        x_ref, g_ref = refs[0], refs[1]
        k = 2
        if rope:
            cos_ref, sin_ref, sg_ref = refs[2], refs[3], refs[4]
            k = 5
        gv = g_ref[...]
        if dy is None:
            o_ref = refs[k]
            for h in range(n_heads):
                sl = slice(h * dh, (h + 1) * dh)
                o_ref[:, sl] = _norm_head(x_ref[:, sl], gv)[0]
            if rope:
                o_ref[...] = _rope_full(o_ref[...], cos_ref[...], sin_ref[...], sg_ref[...], q)
            return
        dy_ref, dx_ref, dg_ref = refs[k], refs[k + 1], refs[k + 2]
        src = dy_ref
        if rope:
            dx_ref[...] = _rope_full_t(dy_ref[...], cos_ref[...], sin_ref[...], sg_ref[...], q)
            src = dx_ref
        dg = jnp.zeros((1, dh), F32)
        for h in range(n_heads):
            sl = slice(h * dh, (h + 1) * dh)
            _, yn, r = _norm_head(x_ref[:, sl], gv)
            dxh, dgh = _norm_head_bwd(src[:, sl], yn, r, gv)
            dx_ref[:, sl] = dxh
            dg = dg + dgh
        _accum_out(dg_ref, dg, _first_step())

    row = pl.BlockSpec((None, ts, w), lambda bi, i: (bi, i, 0))
    whole = lambda a: pl.BlockSpec(a.shape, lambda bi, i: (0, 0))
    ins, specs = [x, g], [row, whole(g)]
    if rope:
        ins += [cos, sin, sg]
        specs += [pl.BlockSpec((ts, w), lambda bi, i: (i, 0)), pl.BlockSpec((ts, w), lambda bi, i: (i, 0)), whole(sg)]
    if dy is None:
        out_shape, out_specs = jax.ShapeDtypeStruct(x.shape, F32), row
    else:
        ins.append(dy)
        specs.append(row)
        out_shape = [jax.ShapeDtypeStruct(x.shape, F32), jax.ShapeDtypeStruct(g.shape, F32)]
        out_specs = [row, whole(g)]
    return pl.pallas_call(
        body, out_shape=out_shape, grid=(b, s // ts), in_specs=specs, out_specs=out_specs,
        name=f"hnr_{'b' if dy is not None else 'f'}_{n_heads}x{dh}_{int(rope)}", compiler_params=_params(),
    )(*ins)


@functools.partial(jax.custom_vjp, nondiff_argnums=(5, 6))
def head_norm_rope(x, g, cos, sin, sg, n_heads, q):
    return _hnr_call(x, g, cos, sin, sg, n_heads, q)


def _head_norm_rope_fwd(x, g, cos, sin, sg, n_heads, q):
    return _hnr_call(x, g, cos, sin, sg, n_heads, q), (x, g, cos, sin, sg)


def _head_norm_rope_bwd(n_heads, q, res, dy):
    x, g, cos, sin, sg = res
    dx, dg = _hnr_call(x, g, cos, sin, sg, n_heads, q, dy=dy)
    zero = lambda t: None if t is None else jnp.zeros_like(t)
    return dx, dg, zero(cos), zero(sin), zero(sg)


head_norm_rope.defvjp(_head_norm_rope_fwd, _head_norm_rope_bwd)


def _mla_k_call(kv, kr, g, cos, sin, sg, dkn=None):
    b, s, _ = kv.shape
    ts = _pick(s, ROW_TILES)
    hw = MLA_NOPE + MLA_V
    kn_w = MLA_HEADS * MLA_QK
    q = MLA_ROPE // 4

    def body(kv_ref, kr_ref, g_ref, cos_ref, sin_ref, sg_ref, *rest):
        gv = g_ref[...]
        krv = kr_ref[...]
        if dkn is None:
            (o_ref,) = rest
            for h in range(MLA_HEADS):
                kh = jnp.concatenate([kv_ref[:, h * hw:h * hw + MLA_NOPE], krv], axis=-1)
                o_ref[:, h * MLA_QK:(h + 1) * MLA_QK] = _norm_head(kh, gv)[0]
            o_ref[...] = _rope_full(o_ref[...], cos_ref[...], sin_ref[...], sg_ref[...], q)
            return
        dkn_ref, dkv_ref, dkr_ref, dg_ref, dy_ref = rest
        dy_ref[...] = _rope_full_t(dkn_ref[...], cos_ref[...], sin_ref[...], sg_ref[...], q)
        dg = jnp.zeros((1, MLA_QK), F32)
        dkr = jnp.zeros((ts, MLA_ROPE), F32)
        for h in range(MLA_HEADS):
            kh = jnp.concatenate([kv_ref[:, h * hw:h * hw + MLA_NOPE], krv], axis=-1)
            _, yn, r = _norm_head(kh, gv)
            dxh, dgh = _norm_head_bwd(dy_ref[:, h * MLA_QK:(h + 1) * MLA_QK], yn, r, gv)
            dkv_ref[:, h * hw:h * hw + MLA_NOPE] = dxh[:, :MLA_NOPE]
            dkv_ref[:, h * hw + MLA_NOPE:(h + 1) * hw] = jnp.zeros((ts, MLA_V), F32)
            dkr = dkr + dxh[:, MLA_NOPE:]
            dg = dg + dgh
        dkr_ref[...] = dkr
        _accum_out(dg_ref, dg, _first_step())

    row = lambda w: pl.BlockSpec((None, ts, w), lambda bi, i: (bi, i, 0))
    tab = pl.BlockSpec((ts, kn_w), lambda bi, i: (i, 0))
    whole = lambda a: pl.BlockSpec(a.shape, lambda bi, i: (0, 0))
    ins = [kv, kr, g, cos, sin, sg]
    specs = [row(kv.shape[2]), row(MLA_ROPE), whole(g), tab, tab, whole(sg)]
    scratch = []
    if dkn is None:
        out_shape, out_specs = jax.ShapeDtypeStruct((b, s, kn_w), F32), row(kn_w)
    else:
        ins.append(dkn)
        specs.append(row(kn_w))
        out_shape = [jax.ShapeDtypeStruct(kv.shape, F32), jax.ShapeDtypeStruct(kr.shape, F32),
                     jax.ShapeDtypeStruct(g.shape, F32)]
        out_specs = [row(kv.shape[2]), row(MLA_ROPE), whole(g)]
        scratch = [pltpu.VMEM((ts, kn_w), F32)]
    return pl.pallas_call(
        body, out_shape=out_shape, grid=(b, s // ts), in_specs=specs, out_specs=out_specs, scratch_shapes=scratch,
        name=f"mla_k_{'b' if dkn is not None else 'f'}", compiler_params=_params(),
    )(*ins)


@jax.custom_vjp
def mla_k_prep(kv, kr, g, cos, sin, sg):
    return _mla_k_call(kv, kr, g, cos, sin, sg)


def _mla_k_prep_fwd(kv, kr, g, cos, sin, sg):
    return _mla_k_call(kv, kr, g, cos, sin, sg), (kv, kr, g, cos, sin, sg)


def _mla_k_prep_bwd(res, dkn):
    kv, kr, g, cos, sin, sg = res
    dkv, dkr, dg = _mla_k_call(kv, kr, g, cos, sin, sg, dkn=dkn)
    return dkv, dkr, dg, jnp.zeros_like(cos), jnp.zeros_like(sin), jnp.zeros_like(sg)


mla_k_prep.defvjp(_mla_k_prep_fwd, _mla_k_prep_bwd)


class _HeadLayout:
    def __init__(self, groups, dq, dv, q_off, k_off, v_off, o_off, wq, wk, wv, wo, scale):
        self.groups, self.dq, self.dv, self.scale = groups, dq, dv, scale
        self.q_off, self.k_off, self.v_off, self.o_off = q_off, k_off, v_off, o_off
        self.wq, self.wk, self.wv, self.wo = wq, wk, wv, wo
        self.n_h = len(q_off)


def _gqa_layout():
    rep = GQA_Q_HEADS // GQA_KV_HEADS
    n_h = GQA_Q_HEADS // 2
    return _HeadLayout(2, HEAD_DIM, HEAD_DIM, [h * HEAD_DIM for h in range(n_h)], [(h // rep) * HEAD_DIM for h in range(n_h)],
                       [(h // rep) * HEAD_DIM for h in range(n_h)], [h * HEAD_DIM for h in range(n_h)],
                       n_h * HEAD_DIM, (n_h // rep) * HEAD_DIM, (n_h // rep) * HEAD_DIM, n_h * HEAD_DIM, HEAD_DIM ** -0.5)


def _mla_layout():
    n_h = MLA_HEADS // 2
    hw = MLA_NOPE + MLA_V
    return _HeadLayout(2, MLA_QK, MLA_V, [h * MLA_QK for h in range(n_h)], [h * MLA_QK for h in range(n_h)],
                       [h * hw + MLA_NOPE for h in range(n_h)], [h * MLA_V for h in range(n_h)],
                       n_h * MLA_QK, n_h * MLA_QK, n_h * hw, n_h * MLA_V, MLA_QK ** -0.5)


def _attn_tm_fwd(q, k, v, lay, n_ctx):
    b, s, _ = q.shape
    tq = min(256, n_ctx)
    nc = n_ctx // tq

    def body(q_ref, k_ref, v_ref, o_ref, lse_ref):
        def run(n_keys):
            for h in range(lay.n_h):
                qo, ko, vo, oo = lay.q_off[h], lay.k_off[h], lay.v_off[h], lay.o_off[h]
                qv = (q_ref[:, qo:qo + lay.dq] * lay.scale).astype(BF16)
                sc = lax.dot_general(qv, k_ref[0:n_keys, ko:ko + lay.dq].astype(BF16), _NT, preferred_element_type=F32)
                m = jnp.max(sc, axis=-1, keepdims=True)
                p = jnp.exp(sc - m)
                l = jnp.sum(p, axis=-1, keepdims=True)
                o = jnp.dot(p.astype(BF16), v_ref[0:n_keys, vo:vo + lay.dv].astype(BF16), preferred_element_type=F32)
                o_ref[:, oo:oo + lay.dv] = o / l
                lse_ref[:, h:h + 1] = m + jnp.log(l)

        pl.when(pl.program_id(2) < nc)(lambda: run(n_ctx))
        pl.when(pl.program_id(2) >= nc)(lambda: run(s))

    return pl.pallas_call(
        body, out_shape=[jax.ShapeDtypeStruct((b, s, lay.groups * lay.wo), F32),
                         jax.ShapeDtypeStruct((b, lay.groups, s, lay.n_h), F32)],
        grid=(b, lay.groups, s // tq),
        in_specs=[pl.BlockSpec((None, tq, lay.wq), lambda bi, g, i: (bi, i, g)),
                  pl.BlockSpec((None, s, lay.wk), lambda bi, g, i: (bi, 0, g)),
                  pl.BlockSpec((None, s, lay.wv), lambda bi, g, i: (bi, 0, g))],
        out_specs=[pl.BlockSpec((None, tq, lay.wo), lambda bi, g, i: (bi, i, g)),
                   pl.BlockSpec((None, None, tq, lay.n_h), lambda bi, g, i: (bi, g, i, 0))],
        name=f"attn_tm_f_{lay.dq}", compiler_params=_params(),
    )(q, k, v)


def _attn_tm_bwd(q, k, v, lse, o, do, lay, n_ctx):
    b, s, _ = q.shape
    tk = min(256, n_ctx)
    nc = n_ctx // tk

    def body(q_ref, k_ref, v_ref, lse_ref, o_ref, do_ref, dq_ref, dk_ref, dv_ref, delta_ref):
        @pl.when(pl.program_id(2) == 0)
        def _():
            dq_ref[...] = jnp.zeros_like(dq_ref)
            for h in range(lay.n_h):
                oo = lay.o_off[h]
                delta_ref[:, h:h + 1] = jnp.sum(o_ref[:, oo:oo + lay.dv] * do_ref[:, oo:oo + lay.dv], axis=-1,
                                                keepdims=True)

        def run(r0):
            dk_acc, dv_acc = {}, {}
            for h in range(lay.n_h):
                qo, ko, vo, oo = lay.q_off[h], lay.k_off[h], lay.v_off[h], lay.o_off[h]
                kh = k_ref[:, ko:ko + lay.dq].astype(BF16)
                vh = v_ref[:, vo:vo + lay.dv].astype(BF16)
                qv = (q_ref[r0:s, qo:qo + lay.dq] * lay.scale).astype(BF16)
                dob = do_ref[r0:s, oo:oo + lay.dv].astype(BF16)
                sc = lax.dot_general(qv, kh, _NT, preferred_element_type=F32)
                p = jnp.exp(sc - lse_ref[r0:s, h:h + 1])
                dvh = lax.dot_general(p.astype(BF16), dob, _TN, preferred_element_type=F32)
                dp = lax.dot_general(dob, vh, _NT, preferred_element_type=F32)
                dsb = (p * (dp - delta_ref[r0:s, h:h + 1])).astype(BF16)
                dkh = lax.dot_general(dsb, qv, _TN, preferred_element_type=F32)
                dq_ref[r0:s, qo:qo + lay.dq] += jnp.dot(dsb, kh, preferred_element_type=F32) * lay.scale
                dk_acc[ko] = dkh if ko not in dk_acc else dk_acc[ko] + dkh
                dv_acc[vo] = dvh if vo not in dv_acc else dv_acc[vo] + dvh
            if len(dv_acc) * lay.dv != lay.wv:
                dv_ref[...] = jnp.zeros_like(dv_ref)
            for ko, val in dk_acc.items():
                dk_ref[:, ko:ko + lay.dq] = val
            for vo, val in dv_acc.items():
                dv_ref[:, vo:vo + lay.dv] = val

        pl.when(pl.program_id(2) < nc)(lambda: run(0))
        pl.when(pl.program_id(2) >= nc)(lambda: run(n_ctx))

    full = lambda w: pl.BlockSpec((None, s, w), lambda bi, g, j: (bi, 0, g))
    blk = lambda w: pl.BlockSpec((None, tk, w), lambda bi, g, j: (bi, j, g))
    stat = pl.BlockSpec((None, None, s, lay.n_h), lambda bi, g, j: (bi, g, 0, 0))
    return pl.pallas_call(
        body, out_shape=[jax.ShapeDtypeStruct(q.shape, F32), jax.ShapeDtypeStruct(k.shape, F32),
                         jax.ShapeDtypeStruct(v.shape, F32)],
        grid=(b, lay.groups, s // tk),
        in_specs=[full(lay.wq), blk(lay.wk), blk(lay.wv), stat, full(lay.wo), full(lay.wo)],
        out_specs=[full(lay.wq), blk(lay.wk), blk(lay.wv)],
        scratch_shapes=[pltpu.VMEM((s, lay.n_h), F32)],
        name=f"attn_tm_b_{lay.dq}", compiler_params=_params(),
    )(q, k, v, lse, o, do)


def _make_attention_tm(lay):
    @functools.partial(jax.custom_vjp, nondiff_argnums=(3,))
    def op(q, k, v, n_ctx):
        return _attn_tm_fwd(q, k, v, lay, n_ctx)[0]

    def fwd(q, k, v, n_ctx):
        o, lse = _attn_tm_fwd(q, k, v, lay, n_ctx)
        return o, (q, k, v, o, lse)

    def bwd(n_ctx, res, do):
        q, k, v, o, lse = res
        return _attn_tm_bwd(q, k, v, lse, o, do, lay, n_ctx)

    op.defvjp(fwd, bwd)
    return op


gqa_attention = _make_attention_tm(_gqa_layout())
mla_attention = _make_attention_tm(_mla_layout())

NA_GROUPS_FWD = 1
NA_GROUPS_BWD = 2


def _na_tm_specs(s, nc, rows, groups):
    hg = NA_HEADS // groups
    w = hg * HEAD_DIM
    qs = pl.BlockSpec((None, GRID_W, w), lambda bi, g, i: (bi, i, g))
    ks = pl.BlockSpec((None, s, w), lambda bi, g, i: (bi, 0, g))
    bs = pl.BlockSpec((hg, None, GRID_W, NA_BAND), lambda bi, g, i: (g, _na_geometry(i, nc, rows)[1], 0, 0))
    ls = pl.BlockSpec((None, None, GRID_W, hg), lambda bi, g, i: (bi, g, i, 0))
    return hg, w, qs, ks, bs, ls


def _na_tm_scores(q_ref, k_ref, bias_ref, hd, n_ctx, start, scale):
    sl = slice(hd * HEAD_DIM, (hd + 1) * HEAD_DIM)
    qv = (q_ref[:, sl] * scale).astype(BF16)
    kc = k_ref[0:n_ctx, sl].astype(BF16)
    kb = k_ref[pl.ds(start, NA_BAND), sl].astype(BF16)
    s_c = lax.dot_general(qv, kc, _NT, preferred_element_type=F32)
    s_l = lax.dot_general(qv, kb, _NT, preferred_element_type=F32) + bias_ref[hd]
    return sl, qv, kc, kb, s_c, s_l


def _na_tm_fwd(q, k, v, bias, n_ctx):
    b, s, _ = q.shape
    nc = n_ctx // GRID_W
    rows = (s - n_ctx) // GRID_W
    scale = HEAD_DIM ** -0.5
    hg, w, qs, ks, bs, ls = _na_tm_specs(s, nc, rows, NA_GROUPS_FWD)

    def body(q_ref, k_ref, v_ref, bias_ref, o_ref, lse_ref):
        rs, _ = _na_geometry(pl.program_id(2), nc, rows)
        start = pl.multiple_of(n_ctx + rs * GRID_W, GRID_W)
        for hd in range(hg):
            sl, _, _, _, s_c, s_l = _na_tm_scores(q_ref, k_ref, bias_ref, hd, n_ctx, start, scale)
            m = jnp.maximum(jnp.max(s_c, axis=-1, keepdims=True), jnp.max(s_l, axis=-1, keepdims=True))
            p_c = jnp.exp(s_c - m)
            p_l = jnp.exp(s_l - m)
            l = jnp.sum(p_c, axis=-1, keepdims=True) + jnp.sum(p_l, axis=-1, keepdims=True)
            o = jnp.dot(p_c.astype(BF16), v_ref[0:n_ctx, sl].astype(BF16), preferred_element_type=F32)
            o = o + jnp.dot(p_l.astype(BF16), v_ref[pl.ds(start, NA_BAND), sl].astype(BF16), preferred_element_type=F32)
            o_ref[:, sl] = o / l
            lse_ref[:, hd:hd + 1] = m + jnp.log(l)

    return pl.pallas_call(
        body, out_shape=[jax.ShapeDtypeStruct(q.shape, F32), jax.ShapeDtypeStruct((b, NA_GROUPS_FWD, s, hg), F32)],
        grid=(b, NA_GROUPS_FWD, s // GRID_W), in_specs=[qs, ks, ks, bs], out_specs=[qs, ls],
        name=f"na_tm_f_{s}", compiler_params=_params(),
    )(q, k, v, bias)


def _na_tm_bwd(q, k, v, bias, o, lse, do, n_ctx):
    b, s, _ = q.shape
    nc = n_ctx // GRID_W
    rows = (s - n_ctx) // GRID_W
    scale = HEAD_DIM ** -0.5
    n_cls = NA_WIN_R + 1
    hg, w, qs, ks, bs, ls = _na_tm_specs(s, nc, rows, NA_GROUPS_BWD)
    lse = jnp.transpose(lse, (0, 2, 1, 3)).reshape(b, s, NA_GROUPS_BWD, hg)
    lse = jnp.transpose(lse, (0, 2, 1, 3))

    def body(q_ref, k_ref, v_ref, bias_ref, o_ref, lse_ref, do_ref, dq_ref, dk_ref, dv_ref, db_ref):
        i = pl.program_id(2)
        rs, cls = _na_geometry(i, nc, rows)
        _, cls_prev = _na_geometry(i - 1, nc, rows)
        start = pl.multiple_of(n_ctx + rs * GRID_W, GRID_W)
        first = jnp.logical_or(i == 0, cls != cls_prev)

        @pl.when(i == 0)
        def _():
            dk_ref[...] = jnp.zeros_like(dk_ref)
            dv_ref[...] = jnp.zeros_like(dv_ref)

        @pl.when(first)
        def _():
            db_ref[...] = jnp.zeros_like(db_ref)

        for hd in range(hg):
            sl, qv, kc, kb, s_c, s_l = _na_tm_scores(q_ref, k_ref, bias_ref, hd, n_ctx, start, scale)
            lse_v = lse_ref[:, hd:hd + 1]
            p_c = jnp.exp(s_c - lse_v)
            p_l = jnp.exp(s_l - lse_v)
            dov = do_ref[:, sl]
            dob = dov.astype(BF16)
            delta = jnp.sum(dov * o_ref[:, sl], axis=-1, keepdims=True)
            vc = v_ref[0:n_ctx, sl].astype(BF16)
            vb = v_ref[pl.ds(start, NA_BAND), sl].astype(BF16)
            ds_c = p_c * (lax.dot_general(dob, vc, _NT, preferred_element_type=F32) - delta)
            ds_l = p_l * (lax.dot_general(dob, vb, _NT, preferred_element_type=F32) - delta)
            dsc_b = ds_c.astype(BF16)
            dsl_b = ds_l.astype(BF16)
            dq_ref[:, sl] = (jnp.dot(dsc_b, kc, preferred_element_type=F32)
                             + jnp.dot(dsl_b, kb, preferred_element_type=F32)) * scale
            dk_ref[0:n_ctx, sl] += lax.dot_general(dsc_b, qv, _TN, preferred_element_type=F32)
            dk_ref[pl.ds(start, NA_BAND), sl] += lax.dot_general(dsl_b, qv, _TN, preferred_element_type=F32)
            dv_ref[0:n_ctx, sl] += lax.dot_general(p_c.astype(BF16), dob, _TN, preferred_element_type=F32)
            dv_ref[pl.ds(start, NA_BAND), sl] += lax.dot_general(p_l.astype(BF16), dob, _TN, preferred_element_type=F32)
            db_ref[hd] += ds_l

    dbs = pl.BlockSpec((None, hg, None, GRID_W, NA_BAND), lambda bi, g, i: (bi, g, _na_geometry(i, nc, rows)[1], 0, 0))
    return pl.pallas_call(
        body,
        out_shape=[jax.ShapeDtypeStruct(q.shape, F32), jax.ShapeDtypeStruct(q.shape, F32), jax.ShapeDtypeStruct(q.shape, F32),
                   jax.ShapeDtypeStruct((b, NA_HEADS, n_cls, GRID_W, NA_BAND), F32)],
        grid=(b, NA_GROUPS_BWD, s // GRID_W), in_specs=[qs, ks, ks, bs, qs, ls, qs], out_specs=[qs, ks, ks, dbs],
        name=f"na_tm_b_{s}", compiler_params=_params(),
    )(q, k, v, bias, o, lse, do)


@functools.partial(jax.custom_vjp, nondiff_argnums=(4,))
def na_attention_tm(q, k, v, bias, n_ctx):
    return _na_tm_fwd(q, k, v, bias, n_ctx)[0]


def _na_attention_tm_fwd(q, k, v, bias, n_ctx):
    o, lse = _na_tm_fwd(q, k, v, bias, n_ctx)
    return o, (q, k, v, bias, o, lse)


def _na_attention_tm_bwd(n_ctx, res, do):
    q, k, v, bias, o, lse = res
    dq, dk, dv, db = _na_tm_bwd(q, k, v, bias, o, lse, do, n_ctx)
    return dq, dk, dv, _sum_rows(db.reshape(db.shape[0], -1, NA_BAND), db.shape[0]).reshape(db.shape[1:])


na_attention_tm.defvjp(_na_attention_tm_fwd, _na_attention_tm_bwd)


def _cmul(ar, ai, br, bi):
    return ar * br - ai * bi, ar * bi + ai * br


def _s5_chunk(n_ctx):
    return min(256, n_ctx)


def _s5_powers(a_re, a_im, t_len):
    a_re, a_im = lax.stop_gradient(a_re), lax.stop_gradient(a_im)
    mag = jnp.sqrt(a_re * a_re + a_im * a_im)
    th = jnp.arctan2(a_im, a_re)
    t = jnp.arange(t_len + 1, dtype=F32)[:, None]
    pm = jnp.where(t == 0, 1.0, jnp.exp(t * jnp.log(jnp.maximum(mag, 1e-37))) * (mag > 0))
    return jnp.stack([pm * jnp.cos(t * th), pm * jnp.sin(t * th)])


def _s5_tables(pw, t_len, rev, conj=False):
    if conj:
        pw = pw * jnp.asarray([1.0, -1.0], F32)[:, None, None]
    steps = jnp.concatenate([pw[:, min(2 ** i, t_len)][:, None] for i in range(8)], axis=1)
    tile = pw[:, 1:9]
    a8k = pw[:, 0:t_len:8]
    if rev:
        tile, a8k = tile[:, ::-1], a8k[:, ::-1]
    misc = jnp.concatenate([pw[:, t_len:t_len + 1], jnp.zeros((2, 7, pw.shape[-1]), F32)], axis=1)
    return jnp.concatenate([steps, tile, misc, a8k], axis=1)


def _scan_chunk(x_re, x_im, tab_ref, hin_re, hin_im, rev, t_len, xs_ref, es_ref):
    outs = [_scan_slab(x_re[:, k:k + LANE], x_im[:, k:k + LANE], tab_ref, hin_re[:, k:k + LANE], hin_im[:, k:k + LANE],
                       rev, t_len, xs_ref, es_ref, k) for k in range(0, x_re.shape[-1], LANE)]
    return tuple(jnp.concatenate([o[t] for o in outs], axis=-1) for t in range(4))


def _scan_slab(x_re, x_im, tab_ref, hin_re, hin_im, rev, t_len, xs_ref, es_ref, k0):
    lanes = LANE
    n2 = t_len // 8
    tab_ref = tab_ref.at[:, :, k0:k0 + LANE]
    rin = lax.broadcasted_iota(jnp.int32, (t_len, lanes), 0) & 7
    for li, sh in enumerate((1, 2, 4)):
        m_re, m_im = tab_ref[0, li:li + 1, :], tab_ref[1, li:li + 1, :]
        amt = sh if not rev else t_len - sh
        c_re, c_im = _cmul(m_re, m_im, pltpu.roll(x_re, amt, 0), pltpu.roll(x_im, amt, 0))
        ok = (rin >= sh) if not rev else (rin < 8 - sh)
        x_re = x_re + jnp.where(ok, c_re, 0.0)
        x_im = x_im + jnp.where(ok, c_im, 0.0)
    xr_ref, xi_ref = xs_ref
    xr_ref[...] = x_re
    xi_ref[...] = x_im
    off = 0 if rev else 7
    e_re = xr_ref[pl.ds(off, n2, stride=8), :]
    e_im = xi_ref[pl.ds(off, n2, stride=8), :]
    row2 = lax.broadcasted_iota(jnp.int32, (n2, lanes), 0)
    sh, li = 1, 3
    while sh < n2:
        m_re, m_im = tab_ref[0, li:li + 1, :], tab_ref[1, li:li + 1, :]
        amt = sh if not rev else n2 - sh
        c_re, c_im = _cmul(m_re, m_im, pltpu.roll(e_re, amt, 0), pltpu.roll(e_im, amt, 0))
        ok = (row2 >= sh) if not rev else (row2 < n2 - sh)
        e_re = e_re + jnp.where(ok, c_re, 0.0)
        e_im = e_im + jnp.where(ok, c_im, 0.0)
        sh, li = sh * 2, li + 1
    es_ref[0] = e_re
    es_ref[1] = e_im
    last = 0 if rev else n2 - 1
    t_re, t_im = _cmul(tab_ref[0, 16:17, :], tab_ref[1, 16:17, :], hin_re, hin_im)
    hout_re = es_ref[0, last:last + 1, :] + t_re
    hout_im = es_ref[1, last:last + 1, :] + t_im
    amt = 1 if not rev else n2 - 1
    ok = (row2 >= 1) if not rev else (row2 < n2 - 1)
    k_re, k_im = _cmul(tab_ref[0, 24:24 + n2, :], tab_ref[1, 24:24 + n2, :], hin_re, hin_im)
    c_re = jnp.where(ok, pltpu.roll(e_re, amt, 0), 0.0) + k_re
    c_im = jnp.where(ok, pltpu.roll(e_im, amt, 0), 0.0) + k_im
    tp_re, tp_im = tab_ref[0, 8:16, :][None], tab_ref[1, 8:16, :][None]
    add_re, add_im = _cmul(tp_re, tp_im, c_re[:, None, :], c_im[:, None, :])
    h_re = xr_ref[...] + add_re.reshape(t_len, lanes)
    h_im = xi_ref[...] + add_im.reshape(t_len, lanes)
    return h_re, h_im, hout_re, hout_im


def _s5_order(j, n_chunks, nc, rev):
    if not rev:
        return j
    return jnp.where(j < nc, nc - 1 - j, n_chunks - 1 - (j - nc))


def _s5_fwd(u, tab, b_bd, c_bd, n_ctx, rev):
    b, s, w = u.shape
    lanes = b_bd.shape[-1]
    t_len = _s5_chunk(n_ctx)
    n_chunks, nc = s // t_len, n_ctx // t_len

    def body(u_ref, tab_ref, b_ref, c_ref, y_ref, h_ref, hin_ref, carry_ref, xr_ref, xi_ref, es_ref):
        xs_ref = (xr_ref, xi_ref)

        @pl.when(pl.program_id(1) == 0)
        def _():
            carry_ref[...] = jnp.zeros_like(carry_ref)

        ub = u_ref[...].astype(BF16)
        x_re = jnp.dot(ub, b_ref[0].astype(BF16), preferred_element_type=F32)
        x_im = jnp.dot(ub, b_ref[1].astype(BF16), preferred_element_type=F32)
        hin_re, hin_im = carry_ref[0, 0:1, :], carry_ref[1, 0:1, :]
        hin_ref[...] = carry_ref[...]
        h_re, h_im, ho_re, ho_im = _scan_chunk(x_re, x_im, tab_ref, hin_re, hin_im, rev, t_len, xs_ref, es_ref)
        carry_ref[0] = jnp.broadcast_to(ho_re, (8, lanes))
        carry_ref[1] = jnp.broadcast_to(ho_im, (8, lanes))
        h_ref[0] = h_re
        h_ref[1] = h_im
        y_ref[...] = (jnp.dot(h_re.astype(BF16), c_ref[0].astype(BF16), preferred_element_type=F32)
                      - jnp.dot(h_im.astype(BF16), c_ref[1].astype(BF16), preferred_element_type=F32))

    order = lambda j: _s5_order(j, n_chunks, nc, rev)
    whole = lambda arr: pl.BlockSpec(arr.shape, lambda bi, j: (0,) * arr.ndim)
    return pl.pallas_call(
        body,
        out_shape=[jax.ShapeDtypeStruct((b, s, w), F32), jax.ShapeDtypeStruct((2, b, s, lanes), F32),
                   jax.ShapeDtypeStruct((2, b, n_chunks, 8, lanes), F32)],
        grid=(b, n_chunks),
        in_specs=[pl.BlockSpec((None, t_len, w), lambda bi, j: (bi, order(j), 0)), whole(tab), whole(b_bd), whole(c_bd)],
        out_specs=[pl.BlockSpec((None, t_len, w), lambda bi, j: (bi, order(j), 0)),
                   pl.BlockSpec((2, None, t_len, lanes), lambda bi, j: (0, bi, order(j), 0)),
                   pl.BlockSpec((2, None, None, 8, lanes), lambda bi, j: (0, bi, order(j), 0, 0))],
        scratch_shapes=[pltpu.VMEM((2, 8, lanes), F32), pltpu.VMEM((t_len, LANE), F32), pltpu.VMEM((t_len, LANE), F32),
                        pltpu.VMEM((2, t_len // 8, LANE), F32)],
        name=f"s5_f_{s}_{int(rev)}", compiler_params=_params(),
    )(u, tab, b_bd, c_bd)


def _s5_bwd(u, tab_adj, b_bd, c_bd, h, hin, dy, n_ctx, rev):
    b, s, w = u.shape
    lanes = b_bd.shape[-1]
    t_len = _s5_chunk(n_ctx)
    n_chunks, nc = s // t_len, n_ctx // t_len
    arev = not rev

    def body(u_ref, tab_ref, b_ref, c_ref, h_ref, hin_ref, dy_ref, du_ref, db_ref, dc_ref, da_ref,
             carry_ref, xr_ref, xi_ref, es_ref):
        xs_ref = (xr_ref, xi_ref)
        first = jnp.logical_and(pl.program_id(0) == 0, pl.program_id(1) == 0)

        @pl.when(pl.program_id(1) == 0)
        def _():
            carry_ref[...] = jnp.zeros_like(carry_ref)

        dyv = dy_ref[...]
        dyb = dyv.astype(BF16)
        dn = (((1,), (1,)), ((), ()))
        dt = (((0,), (0,)), ((), ()))
        x_re = lax.dot_general(dyb, c_ref[0].astype(BF16), dn, preferred_element_type=F32)
        x_im = -lax.dot_general(dyb, c_ref[1].astype(BF16), dn, preferred_element_type=F32)
        g_re, g_im, go_re, go_im = _scan_chunk(x_re, x_im, tab_ref, carry_ref[0, 0:1, :], carry_ref[1, 0:1, :],
                                               arev, t_len, xs_ref, es_ref)
        carry_ref[0] = jnp.broadcast_to(go_re, (8, lanes))
        carry_ref[1] = jnp.broadcast_to(go_im, (8, lanes))
        h_re, h_im = h_ref[0], h_ref[1]
        gb_re, gb_im = g_re.astype(BF16), g_im.astype(BF16)
        du_ref[...] = (lax.dot_general(gb_re, b_ref[0].astype(BF16), dn, preferred_element_type=F32)
                       + lax.dot_general(gb_im, b_ref[1].astype(BF16), dn, preferred_element_type=F32))
        ub = u_ref[...].astype(BF16)
        db_re = lax.dot_general(ub, gb_re, dt, preferred_element_type=F32)
        db_im = lax.dot_general(ub, gb_im, dt, preferred_element_type=F32)
        dc_re = lax.dot_general(h_re.astype(BF16), dyb, dt, preferred_element_type=F32)
        dc_im = -lax.dot_general(h_im.astype(BF16), dyb, dt, preferred_element_type=F32)
        row = lax.broadcasted_iota(jnp.int32, (t_len, lanes), 0)
        amt = 1 if not rev else t_len - 1
        edge = (row == 0) if not rev else (row == t_len - 1)
        hp_re = jnp.where(edge, hin_ref[0, 0:1, :], pltpu.roll(h_re, amt, 0))
        hp_im = jnp.where(edge, hin_ref[1, 0:1, :], pltpu.roll(h_im, amt, 0))
        da_re = jnp.sum(g_re * hp_re + g_im * hp_im, axis=0, keepdims=True)
        da_im = jnp.sum(g_im * hp_re - g_re * hp_im, axis=0, keepdims=True)

        @pl.when(first)
        def _():
            db_ref[0], db_ref[1] = db_re, db_im
            dc_ref[0], dc_ref[1] = dc_re, dc_im
            da_ref[0] = jnp.broadcast_to(da_re, (8, lanes))
            da_ref[1] = jnp.broadcast_to(da_im, (8, lanes))

        @pl.when(jnp.logical_not(first))
        def _():
            db_ref[0] += db_re
            db_ref[1] += db_im
            dc_ref[0] += dc_re
            dc_ref[1] += dc_im
            da_ref[0] += jnp.broadcast_to(da_re, (8, lanes))
            da_ref[1] += jnp.broadcast_to(da_im, (8, lanes))

    order = lambda j: _s5_order(n_chunks - 1 - j, n_chunks, nc, rev)
    whole = lambda arr: pl.BlockSpec(arr.shape, lambda bi, j: (0,) * arr.ndim)
    us = pl.BlockSpec((None, t_len, w), lambda bi, j: (bi, order(j), 0))
    return pl.pallas_call(
        body,
        out_shape=[jax.ShapeDtypeStruct((b, s, w), F32), jax.ShapeDtypeStruct(b_bd.shape, F32),
                   jax.ShapeDtypeStruct(c_bd.shape, F32), jax.ShapeDtypeStruct((2, 8, lanes), F32)],
        grid=(b, n_chunks),
        in_specs=[us, whole(tab_adj), whole(b_bd), whole(c_bd),
                  pl.BlockSpec((2, None, t_len, lanes), lambda bi, j: (0, bi, order(j), 0)),
                  pl.BlockSpec((2, None, None, 8, lanes), lambda bi, j: (0, bi, order(j), 0, 0)), us],
        out_specs=[us, whole(b_bd), whole(c_bd), pl.BlockSpec((2, 8, lanes), lambda bi, j: (0, 0, 0))],
        scratch_shapes=[pltpu.VMEM((2, 8, lanes), F32), pltpu.VMEM((t_len, LANE), F32), pltpu.VMEM((t_len, LANE), F32),
                        pltpu.VMEM((2, t_len // 8, LANE), F32)],
        name=f"s5_b_{s}_{int(rev)}", compiler_params=_params(),
    )(u, tab_adj, b_bd, c_bd, h, hin, dy)


@functools.partial(jax.custom_vjp, nondiff_argnums=(4, 5))
def s5_direction(u, a, b_bd, c_bd, n_ctx, rev):
    t_len = _s5_chunk(n_ctx)
    return _s5_fwd(u, _s5_tables(_s5_powers(a[0], a[1], t_len), t_len, rev), b_bd, c_bd, n_ctx, rev)[0]


def _s5_direction_fwd(u, a, b_bd, c_bd, n_ctx, rev):
    t_len = _s5_chunk(n_ctx)
    pw = _s5_powers(a[0], a[1], t_len)
    y, h, hin = _s5_fwd(u, _s5_tables(pw, t_len, rev), b_bd, c_bd, n_ctx, rev)
    return y, (u, pw, b_bd, c_bd, h, hin)


def _s5_direction_bwd(n_ctx, rev, res, dy):
    u, pw, b_bd, c_bd, h, hin = res
    tab_adj = _s5_tables(pw, _s5_chunk(n_ctx), not rev, conj=True)
    du, db, dc, da = _s5_bwd(u, tab_adj, b_bd, c_bd, h, hin, dy, n_ctx, rev)
    return du, da[:, 0, :], db, dc


s5_direction.defvjp(_s5_direction_fwd, _s5_direction_bwd)


def _s5_discretize(lam_re, lam_im, log_dt, b_re, b_im):
    dt = jnp.exp(log_dt)[:, None]
    mag = jnp.exp(lam_re * dt)
    a_re = mag * jnp.cos(lam_im * dt)
    a_im = mag * jnp.sin(lam_im * dt)
    den = jnp.square(lam_re) + jnp.square(lam_im)
    f_re = ((a_re - 1.0) * lam_re + a_im * lam_im) / den
    f_im = (a_im * lam_re - (a_re - 1.0) * lam_im) / den
    bb_re = f_re[..., None] * b_re - f_im[..., None] * b_im
    bb_im = f_re[..., None] * b_im + f_im[..., None] * b_re
    return a_re, a_im, bb_re, bb_im


def _block_diag(t):
    g, r, c = t.shape
    return (jnp.eye(g, dtype=F32)[:, None, :, None] * t[:, :, None, :]).reshape(g * r, g * c)


def _loss_head(y, target):
    b, n, d = y.shape
    ts = _pick(n, (256, 128, 64))

    def body(y_ref, t_ref, loss_ref, dy_ref):
        first = jnp.logical_and(pl.program_id(0) == 0, pl.program_id(1) == 0)
        err = y_ref[...] - t_ref[...]
        dy_ref[...] = err * (1.0 / d)
        part = 0.5 * jnp.sum(jnp.sum(err * err, axis=-1, keepdims=True) * (1.0 / d), axis=0, keepdims=True)
        part = jnp.broadcast_to(part, (8, LANE))

        @pl.when(first)
        def _():
            loss_ref[...] = part

        @pl.when(jnp.logical_not(first))
        def _():
            loss_ref[...] += part

    blk = pl.BlockSpec((None, ts, d), lambda bi, i: (bi, i, 0))
    return pl.pallas_call(
        body, out_shape=[jax.ShapeDtypeStruct((8, LANE), F32), jax.ShapeDtypeStruct((b, n, d), F32)],
        grid=(b, n // ts), in_specs=[blk, blk], out_specs=[pl.BlockSpec((8, LANE), lambda bi, i: (0, 0)), blk],
        name="loss_head", compiler_params=_params(),
    )(y, target)


def _adamw(w, g, m, v):
    shape = w.shape
    n = int(np.prod(shape))
    cols = shape[-1]
    r = n // cols
    tr = _pick(r, (512, 256, 128, 64, 32, 16, 8))
    c1 = 1.0 / (1.0 - ADAM_B1 ** ADAM_STEP)
    c2 = 1.0 / (1.0 - ADAM_B2 ** ADAM_STEP)

    def body(w_ref, g_ref, m_ref, v_ref, d_ref, mo_ref, vo_ref):
        gv = g_ref[...]
        m2 = ADAM_B1 * m_ref[...] + (1.0 - ADAM_B1) * gv
        v2 = ADAM_B2 * v_ref[...] + (1.0 - ADAM_B2) * (gv * gv)
        d_ref[...] = -ADAM_LR * ((m2 * c1) / (jnp.sqrt(v2 * c2) + ADAM_EPS) + ADAM_WD * w_ref[...])
        mo_ref[...] = m2
        vo_ref[...] = v2

    blk = pl.BlockSpec((tr, cols), lambda i: (i, 0))
    outs = pl.pallas_call(
        body, out_shape=[jax.ShapeDtypeStruct((r, cols), F32)] * 3, grid=(r // tr,),
        in_specs=[blk] * 4, out_specs=[blk] * 3, name=f"adamw_{r}x{cols}", compiler_params=_params(),
    )(*[t.reshape(r, cols) for t in (w, g, m, v)])
    return tuple(o.reshape(shape) for o in outs)


def _sum_rows(x, n):
    _, r, c = x.shape
    tr = _pick(r, (512, 256, 128, 64, 32, 16, 8))

    def body(x_ref, o_ref):
        acc = x_ref[0]
        for j in range(1, n):
            acc = acc + x_ref[j]
        o_ref[...] = acc

    return pl.pallas_call(
        body, out_shape=jax.ShapeDtypeStruct((r, c), F32), grid=(r // tr,),
        in_specs=[pl.BlockSpec((n, tr, c), lambda i: (0, i, 0))], out_specs=pl.BlockSpec((tr, c), lambda i: (i, 0)),
        name=f"sum{n}_{r}x{c}", compiler_params=_params(),
    )(x)


def _accumulate(parts, out_dtype):
    r, c = parts[0].shape[-2:]
    tr = _pick(r, (512, 256, 128, 64, 32, 16))

    def body(*refs):
        acc = None
        for ref in refs[:-1]:
            terms = [ref[j] for j in range(ref.shape[0])] if len(ref.shape) == 3 else [ref[...]]
            for t in terms:
                acc = t.astype(F32) if acc is None else acc + t.astype(F32)
        refs[-1][...] = acc.astype(out_dtype)

    specs = [pl.BlockSpec((p.shape[0], tr, c), lambda i: (0, i, 0)) if p.ndim == 3 else pl.BlockSpec((tr, c), lambda i: (i, 0))
             for p in parts]
    tag = "_".join(str(p.shape[0]) if p.ndim == 3 else "1" for p in parts)
    return pl.pallas_call(
        body, out_shape=jax.ShapeDtypeStruct((r, c), out_dtype), grid=(r // tr,), in_specs=specs,
        out_specs=pl.BlockSpec((tr, c), lambda i: (i, 0)), name=f"accumulate_{tag}_{r}x{c}_{jnp.dtype(out_dtype).name}",
        compiler_params=_params(),
    )(*parts)


def _add2(x, y):
    shape = x.shape
    c = shape[-1]
    r = int(np.prod(shape)) // c
    tr = _pick(r, (512, 256, 128, 64, 32, 16, 8))

    def body(x_ref, y_ref, o_ref):
        o_ref[...] = x_ref[...] + y_ref[...]

    blk = pl.BlockSpec((tr, c), lambda i: (i, 0))
    return pl.pallas_call(
        body, out_shape=jax.ShapeDtypeStruct((r, c), F32), grid=(r // tr,), in_specs=[blk, blk], out_specs=blk,
        name=f"add2_{r}x{c}", compiler_params=_params(),
    )(x.reshape(r, c), y.reshape(r, c)).reshape(shape)


_FLIPS = ((1, 0), (0, 1), (1, 1))


def _me():
    return lax.axis_index("x"), lax.axis_index("y"), lax.axis_index("c")


def allgather8(v):
    m_per, n = v.shape

    def body(x_ref, out_ref, send_sems, recv_sems, local_sem):
        x, y, c = _me()
        me, sibling = (x, y, c), (x, y, 1 - c)
        chips = [(1 - x, y), (x, 1 - y), (1 - x, 1 - y)]

        def rows(px, py, pc):
            return out_ref.at[pl.ds((4 * px + 2 * py + pc) * m_per, m_per), :]

        def copy(k, block, to, src=None):
            return pltpu.make_async_remote_copy(
                src_ref=rows(*block) if src is None else src, dst_ref=rows(*block),
                send_sem=send_sems.at[k], recv_sem=recv_sems.at[k], device_id=to, device_id_type=MESH)

        mine = pltpu.make_async_copy(x_ref, rows(*me), local_sem)
        mine.start()
        first = [copy(0, me, sibling, src=x_ref)]
        first += [copy(1 + j, me, (*chip, c), src=x_ref) for j, chip in enumerate(chips)]
        for cp in first:
            cp.start()
        passed = [copy(4 + j, (*chip, c), sibling) for j, chip in enumerate(chips)]
        for j, chip in enumerate(chips):
            copy(1 + j, (*chip, c), me).wait_recv()
            passed[j].start()
        copy(0, sibling, me).wait_recv()
        for j, chip in enumerate(chips):
            copy(4 + j, (*chip, 1 - c), me).wait_recv()
        for cp in first + passed:
            cp.wait_send()
        mine.wait()

    return pl.pallas_call(
        body, out_shape=jax.ShapeDtypeStruct((N_DEV * m_per, n), v.dtype), in_specs=[VMEM_SPEC], out_specs=VMEM_SPEC,
        scratch_shapes=[pltpu.SemaphoreType.DMA((7,)), pltpu.SemaphoreType.DMA((7,)), pltpu.SemaphoreType.DMA],
        name=f"allgather8_{m_per}x{n}", compiler_params=_params(),
    )(v)


def _row_chunks(rows, tile_rows, want):
    n = want
    while n > 1 and rows % (n * tile_rows):
        n //= 2
    return [(i * (rows // n), rows // n) for i in range(n)]


def _remote(src, dst, send_sem, recv_sem, to):
    return pltpu.make_async_remote_copy(src_ref=src, dst_ref=dst, send_sem=send_sem, recv_sem=recv_sem, device_id=to,
                                        device_id_type=MESH)


def plane_allgather(big, small):
    rows = big.shape[0]
    rh = rows // 2
    tile = 16 if big.dtype == BF16 else 8
    ch_full = _row_chunks(rows, tile, 8)
    ch_half = _row_chunks(rh, tile, 4)

    def body(big_ref, small_ref, obig_ref, osmall_ref, send_sems, recv_sems, fwd_send, fwd_recv, own_send, own_recv):
        x, y, c = _me()
        me = 2 * x + y
        sibling = (x, y, 1 - c)
        mine = pl.ds(c * rh, rh)
        other = pl.ds((1 - c) * rh, rh)
        peers = [((x + fx) & 1, (y + fy) & 1) for fx, fy in _FLIPS]
        for st, sz in ch_full:
            sl = pl.ds(st, sz)
            _remote(big_ref.at[sl], obig_ref.at[me, sl], own_send.at[0], own_recv.at[0], sibling).start()
        _remote(small_ref, osmall_ref.at[me], own_send.at[1], own_recv.at[1], sibling).start()
        for j, (px, py) in enumerate(peers):
            for st, sz in ch_half:
                sl = pl.ds(c * rh + st, sz)
                _remote(big_ref.at[sl], obig_ref.at[me, sl], send_sems.at[j], recv_sems.at[j], (px, py, c)).start()
            _remote(small_ref, osmall_ref.at[me], send_sems.at[3 + j], recv_sems.at[3 + j], (px, py, c)).start()
        for j, (px, py) in enumerate(peers):
            pidx = 2 * px + py
            _remote(big_ref.at[mine], obig_ref.at[pidx, mine], send_sems.at[j], recv_sems.at[j], (px, py, c)).wait_recv()
            for st, sz in ch_half:
                sl = pl.ds(c * rh + st, sz)
                _remote(obig_ref.at[pidx, sl], obig_ref.at[pidx, sl], fwd_send.at[j], fwd_recv.at[j], sibling).start()
            _remote(small_ref, osmall_ref.at[pidx], send_sems.at[3 + j], recv_sems.at[3 + j], (px, py, c)).wait_recv()
        for j, (px, py) in enumerate(peers):
            pidx = 2 * px + py
            _remote(obig_ref.at[pidx, other], obig_ref.at[pidx, other], fwd_send.at[j], fwd_recv.at[j], sibling).wait_recv()
        for j, (px, py) in enumerate(peers):
            pidx = 2 * px + py
            _remote(big_ref.at[mine], obig_ref.at[me, mine], send_sems.at[j], recv_sems.at[j], (px, py, c)).wait_send()
            _remote(small_ref, osmall_ref.at[me], send_sems.at[3 + j], recv_sems.at[3 + j], (px, py, c)).wait_send()
            _remote(obig_ref.at[pidx, mine], obig_ref.at[pidx, mine], fwd_send.at[j], fwd_recv.at[j], sibling).wait_send()
        _remote(big_ref, obig_ref.at[me], own_send.at[0], own_recv.at[0], sibling).wait()
        _remote(small_ref, osmall_ref.at[me], own_send.at[1], own_recv.at[1], sibling).wait()

    return pl.pallas_call(
        body, out_shape=[jax.ShapeDtypeStruct((N_PLANE,) + big.shape, big.dtype),
                         jax.ShapeDtypeStruct((N_PLANE,) + small.shape, small.dtype)],
        in_specs=[ANY, ANY], out_specs=[ANY, ANY],
        scratch_shapes=[pltpu.SemaphoreType.DMA((6,)), pltpu.SemaphoreType.DMA((6,)), pltpu.SemaphoreType.DMA((3,)),
                        pltpu.SemaphoreType.DMA((3,)), pltpu.SemaphoreType.DMA((2,)), pltpu.SemaphoreType.DMA((2,))],
        name="plane_allgather", compiler_params=_params(),
    )(big, small)


def plane_scatter(p):
    tile = 16 if p.dtype == BF16 else 8
    chunks = _row_chunks(p.shape[1], tile, 4)

    def body(p_ref, out_ref, send_sems, recv_sems):
        x, y, c = _me()
        peers = [((x + fx) & 1, (y + fy) & 1) for fx, fy in _FLIPS]
        for j, (px, py) in enumerate(peers):
            for st, sz in chunks:
                sl = pl.ds(st, sz)
                _remote(p_ref.at[2 * px + py, sl], out_ref.at[j, sl], send_sems.at[j], recv_sems.at[j], (px, py, c)).start()
        for j, (px, py) in enumerate(peers):
            _remote(p_ref.at[0], out_ref.at[j], send_sems.at[j], recv_sems.at[j], (px, py, c)).wait_recv()
        for j, (px, py) in enumerate(peers):
            _remote(p_ref.at[0], out_ref.at[j], send_sems.at[j], recv_sems.at[j], (px, py, c)).wait_send()

    return pl.pallas_call(
        body, out_shape=jax.ShapeDtypeStruct((len(_FLIPS),) + p.shape[1:], p.dtype), in_specs=[ANY], out_specs=ANY,
        scratch_shapes=[pltpu.SemaphoreType.DMA((3,)), pltpu.SemaphoreType.DMA((3,))],
        name="plane_scatter", compiler_params=_params(),
    )(p)


def sibling_halves(buf):
    n_blk, _, rows, cols = buf.shape
    tile = 16 if buf.dtype == BF16 else 8
    chunks = _row_chunks(rows, tile, 2)

    def body(buf_ref, got_ref, send_sem, recv_sem):
        x, y, c = _me()
        for j in range(n_blk):
            for st, sz in chunks:
                sl = pl.ds(st, sz)
                _remote(buf_ref.at[j, 1 - c, sl], got_ref.at[j, sl], send_sem, recv_sem, (x, y, 1 - c)).start()
        _remote(got_ref, got_ref, send_sem, recv_sem, (x, y, 1 - c)).wait()

    return pl.pallas_call(
        body, out_shape=jax.ShapeDtypeStruct((n_blk, rows, cols), buf.dtype), in_specs=[ANY], out_specs=ANY,
        scratch_shapes=[pltpu.SemaphoreType.DMA, pltpu.SemaphoreType.DMA],
        name="sibling_halves", compiler_params=_params(),
    )(buf)


def sibling_swap(s):
    tile = 16 if s.dtype == BF16 else 8
    chunks = _row_chunks(s.shape[0], tile, 8)

    def body(s_ref, got_ref, send_sem, recv_sem):
        x, y, c = _me()
        for st, sz in chunks:
            sl = pl.ds(st, sz)
            _remote(s_ref.at[sl], got_ref.at[sl], send_sem, recv_sem, (x, y, 1 - c)).start()
        _remote(s_ref, got_ref, send_sem, recv_sem, (x, y, 1 - c)).wait()

    return pl.pallas_call(
        body, out_shape=jax.ShapeDtypeStruct(s.shape, s.dtype), in_specs=[ANY], out_specs=ANY,
        scratch_shapes=[pltpu.SemaphoreType.DMA, pltpu.SemaphoreType.DMA],
        name="sibling_swap", compiler_params=_params(),
    )(s)


def _op(cache, fn, name, kinds, out_dims, **kw):
    key = (name, tuple(out_dims), tuple(sorted(kw.items())))
    if key not in cache:
        cache[key] = make_rowwise(fn, name, kinds, out_dims, **kw)
    return cache[key]


def _even_mixer(ops, a, w, n_ctx):
    b, s, d = a.shape
    proj = linear(a.reshape(b * s, d), w["e_w_in"]).reshape(b, s, -1)
    q, k, v, u = jnp.split(proj, [GQA_Q_W, GQA_Q_W + GQA_KV_W, GQA_Q_W + 2 * GQA_KV_W], axis=-1)
    cos, sin = _rope_tables(n_ctx, s - n_ctx, HEAD_DIM, 0, HEAD_DIM)
    shift = HEAD_DIM // 4
    qn = head_norm_rope(q, w["e_g_q"][None], jnp.tile(cos, (1, GQA_Q_HEADS)), jnp.tile(sin, (1, GQA_Q_HEADS)),
                        jnp.asarray(_rope_signs(HEAD_DIM, 0, HEAD_DIM, GQA_Q_HEADS)), GQA_Q_HEADS, shift)
    kn = head_norm_rope(k, w["e_g_k"][None], jnp.tile(cos, (1, GQA_KV_HEADS)), jnp.tile(sin, (1, GQA_KV_HEADS)),
                        jnp.asarray(_rope_signs(HEAD_DIM, 0, HEAD_DIM, GQA_KV_HEADS)), GQA_KV_HEADS, shift)
    att = gqa_attention(qn, kn, v, n_ctx)
    ys = []
    for dr in range(2):
        a_re, a_im, bb_re, bb_im = _s5_discretize(w["ssm_lam_re"][dr], w["ssm_lam_im"][dr], w["ssm_log_dt"][dr],
                                                  w["ssm_b_re"][dr], w["ssm_b_im"][dr])
        a_flat = jnp.stack([a_re.reshape(-1), a_im.reshape(-1)])
        b_bd = jnp.stack([_block_diag(jnp.swapaxes(bb_re, 1, 2)), _block_diag(jnp.swapaxes(bb_im, 1, 2))])
        c_bd = jnp.stack([_block_diag(jnp.swapaxes(w["ssm_c_re"][dr], 1, 2)),
                          _block_diag(jnp.swapaxes(w["ssm_c_im"][dr], 1, 2))])
        ys.append(s5_direction(u, a_flat, b_bd, c_bd, n_ctx, dr == 1))
    pre = _op(ops, _fn_glu_pre, "glu_pre", ("row", "row", "row", "glob"), (SSM_WIDTH,))
    post = _op(ops, _fn_glu_post, "glu_post", ("row", "row", "glob"), (SSM_WIDTH,))
    z = pre(u, ys[0], ys[1], w["ssm_d"][None])[0]
    t = linear(z.reshape(b * s, SSM_WIDTH), w["ssm_w_glu"]).reshape(b, s, SSM_WIDTH)
    ssm = post(z, t, w["ssm_b_glu"][None])[0]
    mix = jnp.concatenate([att, ssm], axis=-1)
    return linear(mix.reshape(b * s, -1), w["e_w_out"]).reshape(b, s, d)


def _odd_mixer(ops, a, w, n_ctx):
    b, s, d = a.shape
    w_in = jnp.pad(w["o_w_in"], ((0, 0), (0, ODD_IN_PAD - ODD_IN_W)))
    proj = linear(a.reshape(b * s, d), w_in).reshape(b, s, -1)
    c1 = MLA_Q_RANK
    c2 = c1 + MLA_KV_RANK
    c3 = c2 + MLA_ROPE
    cq, ckv, kr, nq, nk, nv, _ = jnp.split(proj, [c1, c2, c3, c3 + NA_W, c3 + 2 * NA_W, ODD_IN_W], axis=-1)
    nrm = lambda wd: _op(ops, _fn_norm, f"norm{wd}", ("row", "glob"), (wd,))
    cqn = nrm(MLA_Q_RANK)(cq, w["mla_g_cq"][None])[0]
    ckvn = nrm(MLA_KV_RANK)(ckv, w["mla_g_ckv"][None])[0]
    q = linear(cqn.reshape(b * s, -1), w["mla_w_uq"]).reshape(b, s, -1)
    kv = linear(ckvn.reshape(b * s, -1), w["mla_w_ukv"]).reshape(b, s, -1)
    cos, sin = _rope_tables(n_ctx, s - n_ctx, MLA_QK, MLA_NOPE, MLA_ROPE)
    cos, sin = jnp.tile(cos, (1, MLA_HEADS)), jnp.tile(sin, (1, MLA_HEADS))
    sg = jnp.asarray(_rope_signs(MLA_QK, MLA_NOPE, MLA_ROPE, MLA_HEADS))
    mq = head_norm_rope(q, w["mla_g_q"][None], cos, sin, sg, MLA_HEADS, MLA_ROPE // 4)
    mk = mla_k_prep(kv, kr, w["mla_g_k"][None], cos, sin, sg)
    mla = mla_attention(mq, mk, kv, n_ctx)
    nqn = head_norm_rope(nq, w["na_g_q"][None], None, None, None, NA_HEADS, 0)
    nkn = head_norm_rope(nk, w["na_g_k"][None], None, None, None, NA_HEADS, 0)
    na = na_attention_tm(nqn, nkn, nv, na_bias_table(w["na_rpb"]), n_ctx)
    mix = jnp.concatenate([mla, na], axis=-1)
    return linear(mix.reshape(b * s, -1), w["o_w_out"]).reshape(b, s, d)


_EVEN_KEYS = ("e_w_in", "e_w_out", "e_g_q", "e_g_k", "ssm_lam_re", "ssm_lam_im", "ssm_log_dt", "ssm_b_re", "ssm_b_im",
              "ssm_c_re", "ssm_c_im", "ssm_d", "ssm_w_glu", "ssm_b_glu")
_ODD_KEYS = ("o_w_in", "o_w_out", "mla_g_cq", "mla_g_ckv", "mla_w_uq", "mla_w_ukv", "mla_g_q", "mla_g_k", "na_g_q",
             "na_g_k", "na_rpb")


def _trunk(x_all, mods, w, n_ctx):
    ops = {}
    depth = mods.shape[0]
    b, s, d = x_all.shape
    modulate = make_modulate(d, n_ctx)
    gated = make_gated_add(d, n_ctx)
    x = x_all
    for i in range(depth):
        j = i // 2
        m = [mods[i][:, :, r:r + 1, :] for r in range(N_MOD)]
        a, x = modulate(x, w["g_norm1"][i][None], m[0], m[1])
        if i % 2 == 0:
            o = _even_mixer(ops, a, {k: w[k][j] for k in _EVEN_KEYS}, n_ctx)
        else:
            o = _odd_mixer(ops, a, {k: w[k][j] for k in _ODD_KEYS}, n_ctx)
        x = gated(x, o, m[2])
        a2, x = modulate(x, w["g_norm2"][i][None], m[3], m[4])
        f = ffn(a2.reshape(b * s, d), w["w_ff1"][i], w["w_ff2"][i]).reshape(b, s, d)
        x = gated(x, f, m[5])
    return x[:, n_ctx:]


def local_step(x, ctx, mods, w, loss_target):
    n_ctx = ctx.shape[1]
    x_all = jnp.concatenate([ctx, x], axis=1)
    y, vjp = jax.vjp(lambda xa, md, ww: _trunk(xa, md, ww, n_ctx), x_all, mods, w)
    loss_tile, dy = _loss_head(y, loss_target)
    dx_all, dmods, dw = vjp(dy)
    return loss_tile[0, 0], dx_all[:, n_ctx:], dmods, dw


_SHARDED = (("w_ff1", 2), ("w_ff2", 1), ("e_w_in", 2), ("e_w_out", 1), ("o_w_in", 2), ("o_w_out", 1),
            ("mla_w_uq", 2), ("mla_w_ukv", 2), ("ssm_w_glu", 1))
_SHARDED_SMALL = (("mla_g_cq", 1), ("mla_g_ckv", 1))
_REPLICATED = ("g_norm1", "g_norm2", "e_g_q", "e_g_k", "ssm_lam_re", "ssm_lam_im", "ssm_log_dt", "ssm_b_re", "ssm_b_im",
               "ssm_c_re", "ssm_c_im", "ssm_d", "ssm_b_glu", "mla_g_q", "mla_g_k", "na_g_q", "na_g_k", "na_rpb")
_WEIGHTS = ("c_ctx", "w_mod", "b_mod", "g_norm1", "g_norm2", "w_ff1", "w_ff2", "e_w_in", "e_w_out", "e_g_q", "e_g_k",
            "ssm_lam_re", "ssm_lam_im", "ssm_log_dt", "ssm_b_re", "ssm_b_im", "ssm_c_re", "ssm_c_im", "ssm_d",
            "ssm_w_glu", "ssm_b_glu", "o_w_in", "o_w_out", "mla_g_cq", "mla_g_ckv", "mla_w_uq", "mla_w_ukv", "mla_g_q",
            "mla_g_k", "na_g_q", "na_g_k", "na_rpb")
_PACK_ROWS = 64


def _pack(arrs, dtype, cols=1024, row_mult=_PACK_ROWS):
    blocks, tail, off = [], [], 0
    for a in arrs:
        n = int(np.prod(a.shape))
        if not tail and off % cols == 0 and n % cols == 0:
            blocks.append(a.astype(dtype).reshape(-1, cols))
        else:
            tail.append(a.astype(dtype).reshape(-1))
        off += n
    rows = -(-off // cols)
    pad = (-rows) % row_mult * cols + rows * cols - off
    if tail or pad:
        blocks.append(jnp.concatenate(tail + [jnp.zeros((pad,), dtype)]).reshape(-1, cols))
    return jnp.concatenate(blocks, axis=0)


def _unpack(packed, shapes):
    cols = packed.shape[-1]
    packed = packed.reshape(-1, cols)
    out, off = [], 0
    for sh in shapes:
        n = int(np.prod(sh))
        if off % cols == 0 and n % cols == 0:
            out.append(packed[off // cols:(off + n) // cols].reshape(sh))
        else:
            r0, r1 = off // cols, -(-(off + n) // cols)
            out.append(packed[r0:r1].reshape(-1)[off - r0 * cols:off - r0 * cols + n].reshape(sh))
        off += n
    return out


def kernel(x, c, ctx, c_ctx, w_mod, b_mod, g_norm1, g_norm2, w_ff1, w_ff2, e_w_in, e_w_out, e_g_q, e_g_k, ssm_lam_re, ssm_lam_im, ssm_log_dt, ssm_b_re, ssm_b_im, ssm_c_re, ssm_c_im, ssm_d, ssm_w_glu, ssm_b_glu, o_w_in, o_w_out, mla_g_cq, mla_g_ckv, mla_w_uq, mla_w_ukv, mla_g_q, mla_g_k, na_g_q, na_g_k, na_rpb, loss_target, m_c_ctx, m_w_mod, m_b_mod, m_g_norm1, m_g_norm2, m_w_ff1, m_w_ff2, m_e_w_in, m_e_w_out, m_e_g_q, m_e_g_k, m_ssm_lam_re, m_ssm_lam_im, m_ssm_log_dt, m_ssm_b_re, m_ssm_b_im, m_ssm_c_re, m_ssm_c_im, m_ssm_d, m_ssm_w_glu, m_ssm_b_glu, m_o_w_in, m_o_w_out, m_mla_g_cq, m_mla_g_ckv, m_mla_w_uq, m_mla_w_ukv, m_mla_g_q, m_mla_g_k, m_na_g_q, m_na_g_k, m_na_rpb, v_c_ctx, v_w_mod, v_b_mod, v_g_norm1, v_g_norm2, v_w_ff1, v_w_ff2, v_e_w_in, v_e_w_out, v_e_g_q, v_e_g_k, v_ssm_lam_re, v_ssm_lam_im, v_ssm_log_dt, v_ssm_b_re, v_ssm_b_im, v_ssm_c_re, v_ssm_c_im, v_ssm_d, v_ssm_w_glu, v_ssm_b_glu, v_o_w_in, v_o_w_out, v_mla_g_cq, v_mla_g_ckv, v_mla_w_uq, v_mla_w_ukv, v_mla_g_q, v_mla_g_k, v_na_g_q, v_na_g_k, v_na_rpb):
    env = dict(locals())
    weights = {n: env[n] for n in _WEIGHTS}
    mom_m = {n: env["m_" + n] for n in _WEIGHTS}
    mom_v = {n: env["v_" + n] for n in _WEIGHTS}
    ax, ay, ac = _me()
    plane = 2 * ax + ay
    dev = 4 * ax + 2 * ay + ac
    b_loc, d = c.shape
    depth = w_mod.shape[0]
    n_all = N_DEV * b_loc
    mod_cols = w_mod.shape[2]

    big = _pack([weights[n] for n, _ in _SHARDED], BF16)
    small = _pack([weights[n] for n, _ in _SHARDED_SMALL], F32, cols=LANE, row_mult=8)
    g_big, g_small = plane_allgather(big, small)
    full = {n: weights[n] for n in _REPLICATED}
    parts = [_unpack(g_big[j], [weights[n].shape for n, _ in _SHARDED]) for j in range(N_PLANE)]
    for t, (n, axis) in enumerate(_SHARDED):
        full[n] = [jnp.concatenate([parts[j][t][l] for j in range(N_PLANE)], axis=axis - 1)
                   for l in range(weights[n].shape[0])]
    parts_s = [_unpack(g_small[j], [weights[n].shape for n, _ in _SHARDED_SMALL]) for j in range(N_PLANE)]
    for t, (n, axis) in enumerate(_SHARDED_SMALL):
        full[n] = jnp.concatenate([parts_s[j][t] for j in range(N_PLANE)], axis=axis)

    rows_pad = 8 * ((n_all + 1 + 7) // 8)
    c_all = allgather8(jnp.pad(c, ((0, 8 - b_loc), (0, 0)))).reshape(N_DEV, 8, d)[:, :b_loc].reshape(n_all, d)
    cond_raw = jnp.concatenate([c_all, c_ctx[None], jnp.zeros((rows_pad - n_all - 1, d), F32)], axis=0)
    b_cols = lax.dynamic_slice_in_dim(b_mod, plane * mod_cols, mod_cols, axis=1)
    mod_loc = jnp.stack([_mm(cond_raw, w_mod[i], a_act="silu") + b_cols[i][None] for i in range(depth)])
    mod_g = allgather8(mod_loc.reshape(depth * rows_pad, mod_cols)).reshape(N_PLANE, 2, depth, rows_pad, mod_cols)
    mod_all = jnp.concatenate([mod_g[j, 0] for j in range(N_PLANE)], axis=-1)
    m_lat = lax.dynamic_slice_in_dim(mod_all, dev * b_loc, b_loc, axis=1)
    m_ctx = jnp.broadcast_to(mod_all[:, n_all][:, None], m_lat.shape)
    mods = jnp.stack([m_ctx, m_lat], axis=2).reshape(depth, b_loc, 2, N_MOD, d)

    loss_part, grad_x, dmods, dw = local_step(x, ctx, mods, full, loss_target)

    dm = dmods.reshape(depth, b_loc, 2, N_MOD * d)
    dm_rows = jnp.concatenate([dm[:, :, 1], jnp.sum(dm[:, :, 0], axis=1, keepdims=True)], axis=1)
    rep_shapes = [weights[n].shape for n in _REPLICATED] + [(1,)]
    small_pack = _pack([dm_rows] + [dw[n] for n in _REPLICATED] + [loss_part.reshape(1)], F32, cols=1024, row_mult=8)
    sp_rows = small_pack.shape[0]
    gathered = allgather8(small_pack).reshape(N_DEV, sp_rows, 1024)
    n_dm = depth * (b_loc + 1) * N_MOD * d
    dm_all = gathered.reshape(N_DEV, -1)[:, :n_dm].reshape(N_DEV, depth, b_loc + 1, N_MOD * d)
    rep_sum = _sum_rows(gathered, N_DEV).reshape(-1)
    rep_parts = _unpack(rep_sum[n_dm:], rep_shapes)
    rep_grads = dict(zip(_REPLICATED, rep_parts[:-1]))
    loss = rep_parts[-1][0]
    d_ctx_row = rep_sum[:n_dm].reshape(depth, b_loc + 1, N_MOD * d)[:, b_loc]
    d_lat_rows = jnp.transpose(dm_all[:, :, :b_loc], (1, 0, 2, 3)).reshape(depth, n_all, N_MOD * d)
    d_mod_all = jnp.concatenate([d_lat_rows, d_ctx_row[:, None],
                                 jnp.zeros((depth, rows_pad - n_all - 1, N_MOD * d), F32)], axis=1)
    grads = dict(rep_grads)
    grads["b_mod"] = jnp.sum(d_mod_all, axis=1)
    d_cols = lax.dynamic_slice_in_dim(d_mod_all, plane * mod_cols, mod_cols, axis=2)
    grads["w_mod"] = jnp.stack([_mm(cond_raw, d_cols[i], ta=True, a_act="silu") for i in range(depth)])
    d_cond = _mm(d_cols[0], w_mod[0], tb=True)
    for i in range(1, depth):
        d_cond = _add2(d_cond, _mm(d_cols[i], w_mod[i], tb=True))
    d_cond_g = allgather8(d_cond[n_all:n_all + 8] if rows_pad - n_all >= 8 else
                          jnp.pad(d_cond[n_all:], ((0, 8 - (rows_pad - n_all)), (0, 0)))).reshape(N_PLANE, 2, 8, d)
    d_silu = _sum_rows(d_cond_g[:, 0], N_PLANE)[0]
    sg = jax.nn.sigmoid(c_ctx)
    grads["c_ctx"] = d_silu * (sg * (1.0 + c_ctx * (1.0 - sg)))

    def shards_of(g, axis, j):
        layers = g if isinstance(g, (list, tuple)) else [g]
        ax = axis - 1 if isinstance(g, (list, tuple)) else axis
        n = layers[0].shape[ax] // N_PLANE
        return [lax.slice_in_dim(t, j * n, (j + 1) * n, axis=ax) for t in layers]

    send = jnp.stack([_pack([t for n, axis in _SHARDED + _SHARDED_SMALL for t in shards_of(dw[n], axis, j)], BF16)
                      for j in range(N_PLANE)])
    rows_h = send.shape[1] // 2
    send = send.reshape(N_PLANE, 2, rows_h, 1024)
    mine = lax.dynamic_index_in_dim(send, ac, 1, keepdims=False).reshape(N_PLANE * rows_h, 1024)
    theirs = sibling_halves(send).reshape(N_PLANE * rows_h, 1024)
    chip_sum = _accumulate([mine, theirs], BF16).reshape(N_PLANE, rows_h, 1024)
    own = lax.dynamic_index_in_dim(chip_sum, plane, 0, keepdims=False)
    done = _accumulate([own, plane_scatter(chip_sum)], BF16)
    both = jnp.stack([done, sibling_swap(done)])
    flat = jnp.where(ac == 0, both, both[::-1]).astype(F32).reshape(-1, 1024)
    shard_shapes = [weights[n].shape for n, _ in _SHARDED] + [weights[n].shape for n, _ in _SHARDED_SMALL]
    for (n, _), g in zip(_SHARDED + _SHARDED_SMALL, _unpack(flat, shard_shapes)):
        grads[n] = g

    big_names = ("w_mod",) + tuple(n for n, _ in _SHARDED)
    small_names = tuple(n for n in _WEIGHTS if n not in big_names)
    delta, new_m, new_v = {}, {}, {}
    for n in big_names:
        delta[n], new_m[n], new_v[n] = _adamw(weights[n], grads[n], mom_m[n], mom_v[n])
    sm_shapes = [weights[n].shape for n in small_names]
    packed = [_pack([src[n] for n in small_names], F32, cols=1024, row_mult=8)
              for src in (weights, grads, mom_m, mom_v)]
    for dst, res in zip((delta, new_m, new_v), _adamw(*packed)):
        dst.update(dict(zip(small_names, _unpack(res, sm_shapes))))

    return (loss, grad_x, *[grads[n] for n in _WEIGHTS], *[delta[n] for n in _WEIGHTS],
            *[new_m[n] for n in _WEIGHTS], *[new_v[n] for n in _WEIGHTS])
```

```python
import functools

import numpy as np
import jax
import jax.numpy as jnp
from jax import lax
from jax.experimental import pallas as pl
from jax.experimental.pallas import tpu as pltpu

F32 = jnp.float32
BF16 = jnp.bfloat16
HI = lax.Precision.HIGHEST
MESH = pl.DeviceIdType.MESH
ANY = pl.BlockSpec(memory_space=pl.ANY)
VMEM_SPEC = pl.BlockSpec(memory_space=pltpu.VMEM)

GRID_W = 64
HEAD_DIM = 64
ROPE_BASE = 10000.0
EPS = 1e-6
N_MOD = 6
GQA_Q_HEADS, GQA_KV_HEADS = 12, 4
GQA_Q_W, GQA_KV_W = GQA_Q_HEADS * HEAD_DIM, GQA_KV_HEADS * HEAD_DIM
SSM_WIDTH, SSM_GROUP, SSM_STATE = 256, 16, 64
SSM_GROUPS = SSM_WIDTH // SSM_GROUP
SSM_LANES = SSM_GROUPS * SSM_STATE
MLA_HEADS, MLA_Q_RANK, MLA_KV_RANK, MLA_NOPE, MLA_ROPE, MLA_V = 8, 512, 256, 64, 32, 64
MLA_QK = MLA_NOPE + MLA_ROPE
NA_HEADS, NA_WIN_R, NA_WIN_C = 8, 8, 16
NA_W = NA_HEADS * HEAD_DIM
NA_BAND = NA_WIN_R * GRID_W
ODD_IN_W = MLA_Q_RANK + MLA_KV_RANK + MLA_ROPE + 3 * NA_W
ODD_IN_PAD = 2560
ADAM_LR, ADAM_B1, ADAM_B2, ADAM_EPS, ADAM_WD, ADAM_STEP = 0.001, 0.9, 0.999, 1e-08, 0.01, 10
NEG = -1e30
VMEM_LIMIT = 56 * 1024 * 1024
LANE = 128
MM_TILE_M = (1152, 1024, 768, 512, 256, 128)
MM_TILE_N = (1280, 1024, 768, 512, 256, 128)
MM_TILE_K = (1152, 1024, 768, 512, 256, 128)
ROW_TILES = (576, 512, 384, 256, 128, 64)
N_PLANE = 4
N_DEV = 8


def _pick(n, cands):
    for c in cands:
        if n % c == 0:
            return c
    return n


def _params(**kw):
    return pltpu.CompilerParams(vmem_limit_bytes=VMEM_LIMIT, **kw)


def _mm(a, b, *, ta=False, tb=False, a_act=None, epi=None, e=None, exact=False, out_dtype=F32):
    m, kd = (a.shape[1], a.shape[0]) if ta else a.shape
    n = b.shape[0] if tb else b.shape[1]
    tm = _pick(m, MM_TILE_M)
    tn = _pick(n, MM_TILE_N)
    tk = _pick(kd, MM_TILE_K)
    nk = kd // tk
    dn = (((0 if ta else 1,), (1 if tb else 0,)), ((), ()))
    narrow = jnp.dtype(out_dtype) != jnp.dtype(F32)
    assert not (narrow and epi is not None)

    def body(*refs):
        if narrow:
            a_ref, b_ref, out_ref, o_ref = refs
        elif epi is None:
            a_ref, b_ref, o_ref = refs
        else:
            a_ref, b_ref, e_ref, o_ref = refs
        k = pl.program_id(2)
        av = a_ref[...]
        if a_act == "relu2":
            av = jnp.square(jnp.maximum(av, 0.0))
        elif a_act == "silu":
            av = av * jax.nn.sigmoid(av)
        bv = b_ref[...]
        if exact:
            p = lax.dot_general(av, bv, dn, precision=HI, preferred_element_type=F32)
        else:
            p = lax.dot_general(av.astype(BF16), bv.astype(BF16), dn, preferred_element_type=F32)

        @pl.when(k == 0)
        def _():
            o_ref[...] = p

        @pl.when(k > 0)
        def _():
            o_ref[...] += p

        if epi == "drelu2":
            @pl.when(k == nk - 1)
            def _():
                o_ref[...] = o_ref[...] * (2.0 * jnp.maximum(e_ref[...], 0.0))

        if narrow:
            @pl.when(k == nk - 1)
            def _():
                out_ref[...] = o_ref[...].astype(out_dtype)

    a_spec = pl.BlockSpec((tk, tm), lambda i, j, k: (k, i)) if ta else pl.BlockSpec((tm, tk), lambda i, j, k: (i, k))
    b_spec = pl.BlockSpec((tn, tk), lambda i, j, k: (j, k)) if tb else pl.BlockSpec((tk, tn), lambda i, j, k: (k, j))
    o_spec = pl.BlockSpec((tm, tn), lambda i, j, k: (i, j))
    ins, specs = [a, b], [a_spec, b_spec]
    if epi is not None:
        ins.append(e)
        specs.append(o_spec)
    name = f"mm_{m}x{kd}x{n}_{int(ta)}{int(tb)}_{a_act}_{epi}_{int(exact)}_{jnp.dtype(out_dtype).name}"
    return pl.pallas_call(
        body, out_shape=jax.ShapeDtypeStruct((m, n), out_dtype), grid=(m // tm, n // tn, nk),
        in_specs=specs, out_specs=o_spec, name=name, compiler_params=_params(),
        scratch_shapes=[pltpu.VMEM((tm, tn), F32)] if narrow else [],
    )(*ins)


@functools.partial(jax.custom_vjp, nondiff_argnums=(2,))
def _linear(a, w, exact):
    return _mm(a, w, exact=exact)


def _linear_fwd(a, w, exact):
    return _mm(a, w, exact=exact), (a, w)


def _linear_bwd(exact, res, g):
    a, w = res
    return _mm(g, w, tb=True, exact=exact), _mm(a, g, ta=True, exact=exact, out_dtype=w.dtype)


_linear.defvjp(_linear_fwd, _linear_bwd)


def linear(a, w, exact=False):
    return _linear(a, w, exact)


@jax.custom_vjp
def ffn(a, w1, w2):
    return _mm(_mm(a, w1), w2, a_act="relu2")


def _ffn_fwd(a, w1, w2):
    h1 = _mm(a, w1)
    return _mm(h1, w2, a_act="relu2"), (a, w1, w2, h1)


def _ffn_bwd(res, g):
    a, w1, w2, h1 = res
    dh1 = _mm(g, w2, tb=True, epi="drelu2", e=h1)
    dw2 = _mm(h1, g, ta=True, a_act="relu2", out_dtype=w2.dtype)
    return _mm(dh1, w1, tb=True), _mm(a, dh1, ta=True, out_dtype=w1.dtype), dw2


ffn.defvjp(_ffn_fwd, _ffn_bwd)


def make_rowwise(fn, name, kinds, out_dims, nctx_rows=0, whole_seq=False):
    n_in = len(kinds)
    n_out = len(out_dims)
    diff = [i for i, kd in enumerate(kinds) if kd in ("row", "glob", "seg")]
    seg_idx = [i for i, kd in enumerate(kinds) if kd == "seg"]

    def layout(args):
        row0 = args[kinds.index("row")]
        g, s = row0.shape[0], row0.shape[1]
        ts = s if whole_seq else _pick(s, ROW_TILES)
        return g, s, ts, 0

    def spec_of(kind, arr, ts, nctx):
        if kind == "row":
            return pl.BlockSpec((None, ts, arr.shape[2]), lambda g, i: (g, i, 0))
        if kind == "tab":
            return pl.BlockSpec((ts, arr.shape[1]), lambda g, i: (i, 0))
        if kind in ("const", "glob"):
            return pl.BlockSpec(arr.shape, lambda g, i: (0, 0))
        return pl.BlockSpec((None,) + arr.shape[1:], lambda g, i: (g, 0, 0, 0))

    def with_segments(ts):
        if not seg_idx:
            return fn

        def wrapped(*vals):
            rows = pl.program_id(1) * ts + lax.broadcasted_iota(jnp.int32, (ts, 1), 0)
            vals = list(vals)
            for idx in seg_idx:
                vals[idx] = jnp.where(rows < nctx_rows, vals[idx][0], vals[idx][1])
            return fn(*vals)

        return wrapped

    def fwd_call(*args):
        g, s, ts, nctx = layout(args)
        fn = with_segments(ts)

        def body(*refs):
            vals = [r[...] for r in refs[:n_in]]
            outs = fn(*vals)
            for o_ref, o in zip(refs[n_in:], outs):
                o_ref[...] = o

        return pl.pallas_call(
            body, out_shape=[jax.ShapeDtypeStruct((g, s, d), F32) for d in out_dims], grid=(g, s // ts),
            in_specs=[spec_of(kd, a, ts, nctx) for kd, a in zip(kinds, args)],
            out_specs=[pl.BlockSpec((None, ts, d), lambda g_, i: (g_, i, 0)) for d in out_dims],
            name=f"{name}_f_{g}x{s}", compiler_params=_params(),
        )(*args)

    def bwd_call(args, cts):
        g, s, ts, nctx = layout(args)
        fn = with_segments(ts)

        def body(*refs):
            in_refs, ct_refs, out_refs = refs[:n_in], refs[n_in:n_in + n_out], refs[n_in + n_out:]
            gi, i = pl.program_id(0), pl.program_id(1)
            vals = [r[...] for r in in_refs]

            def f(*dv):
                full = list(vals)
                for idx, v in zip(diff, dv):
                    full[idx] = v
                return tuple(fn(*full))

            _, vjp = jax.vjp(f, *[vals[idx] for idx in diff])
            grads = vjp(tuple(r[...] for r in ct_refs))
            for idx, o_ref, gr in zip(diff, out_refs, grads):
                if kinds[idx] == "row":
                    o_ref[...] = gr
                    continue
                if kinds[idx] == "glob":
                    first = jnp.logical_and(gi == 0, i == 0)
                else:
                    first = i == 0

                @pl.when(first)
                def _(o_ref=o_ref, gr=gr):
                    o_ref[...] = gr

                @pl.when(jnp.logical_not(first))
                def _(o_ref=o_ref, gr=gr):
                    o_ref[...] += gr

        in_specs = [spec_of(kd, a, ts, nctx) for kd, a in zip(kinds, args)]
        in_specs += [pl.BlockSpec((None, ts, d), lambda g_, i: (g_, i, 0)) for d in out_dims]
        return pl.pallas_call(
            body, out_shape=[jax.ShapeDtypeStruct(args[idx].shape, F32) for idx in diff], grid=(g, s // ts),
            in_specs=in_specs, out_specs=[spec_of(kinds[idx], args[idx], ts, nctx) for idx in diff],
            name=f"{name}_b_{g}x{s}", compiler_params=_params(),
        )(*args, *cts)

    @jax.custom_vjp
    def op(*args):
        return tuple(fwd_call(*args))

    def op_fwd(*args):
        return tuple(fwd_call(*args)), args

    def op_bwd(args, cts):
        grads = bwd_call(args, cts)
        full = [None] * n_in
        for idx, gr in zip(diff, grads):
            full[idx] = gr
        return tuple(jnp.zeros_like(a) if gfull is None else gfull for a, gfull in zip(args, full))

    op.defvjp(op_fwd, op_bwd)
    op.fwd_call, op.bwd_call = fwd_call, bwd_call
    return op


def make_modulate(d, n_ctx):
    one = make_rowwise(_fn_modulate, "modulate", ("row", "glob", "seg", "seg"), (d,), nctx_rows=n_ctx)
    two = make_rowwise(_fn_modulate_keep, "modulate_keep", ("row", "glob", "seg", "seg"), (d, d), nctx_rows=n_ctx)

    @jax.custom_vjp
    def op(x, g, shift, scale):
        return one.fwd_call(x, g, shift, scale)[0], x

    def fwd(x, g, shift, scale):
        return (one.fwd_call(x, g, shift, scale)[0], x), (x, g, shift, scale)

    def bwd(res, cts):
        return tuple(two.bwd_call(res, cts))

    op.defvjp(fwd, bwd)
    return op


def make_gated_add(d, n_ctx):
    add = make_rowwise(_fn_gated_add, "gated", ("row", "row", "seg"), (d,), nctx_rows=n_ctx)
    mul = make_rowwise(_fn_gate_mul, "gate_mul", ("row", "seg"), (d,), nctx_rows=n_ctx)

    @jax.custom_vjp
    def op(x, o, gate):
        return add.fwd_call(x, o, gate)[0]

    def fwd(x, o, gate):
        return add.fwd_call(x, o, gate)[0], (o, gate)

    def bwd(res, ct):
        do, dgate = mul.bwd_call(res, (ct,))
        return ct, do, dgate

    op.defvjp(fwd, bwd)
    return op


def _rms(x):
    return lax.rsqrt(jnp.mean(x * x, axis=-1, keepdims=True) + EPS)


def _fn_modulate(x, g, shift, scale):
    return ((x * _rms(x) * g) * (1.0 + scale) + shift,)


def _fn_modulate_keep(x, g, shift, scale):
    return _fn_modulate(x, g, shift, scale) + (x,)


def _fn_gated_add(x, o, gate):
    return (x + gate * o,)


def _fn_gate_mul(o, gate):
    return (gate * o,)


def _fn_norm(x, g):
    return (x * _rms(x) * g,)


def _fn_glu_pre(u, y0, y1, d):
    return (jax.nn.gelu(d * u + y0 + y1),)


def _fn_glu_post(z, t, bg):
    return (z * jax.nn.sigmoid(t + bg),)


def _rope_tables(n_ctx, n_lat, dh, start, rot_dim):
    t = jnp.arange(n_lat)
    rows = (t // GRID_W).astype(F32)
    cols = (t % GRID_W).astype(F32)
    axis_dim = rot_dim // 2
    freqs = ROPE_BASE ** (-jnp.arange(0, axis_dim, 2, dtype=F32) / axis_dim)
    ang_r = rows[:, None] * freqs
    ang_c = cols[:, None] * freqs
    ang = jnp.concatenate([ang_r, ang_r, ang_c, ang_c], axis=-1)
    cos = jnp.concatenate([jnp.ones((n_lat, start), F32), jnp.cos(ang)], axis=-1)
    sin = jnp.concatenate([jnp.zeros((n_lat, start), F32), jnp.sin(ang)], axis=-1)
    cos = jnp.concatenate([jnp.ones((n_ctx, dh), F32), cos], axis=0)
    sin = jnp.concatenate([jnp.zeros((n_ctx, dh), F32), sin], axis=0)
    return cos, sin


_NT = (((1,), (1,)), ((), ()))
_TN = (((0,), (0,)), ((), ()))


def _na_geometry(i, nc, rows):
    r = i - nc
    rs = jnp.clip(r - NA_WIN_R // 2, 0, rows - NA_WIN_R)
    is_ctx = i < nc
    cls = jnp.where(is_ctx, NA_WIN_R, r - rs)
    return jnp.where(is_ctx, 0, rs), cls


def _na_onehots():
    q = np.arange(GRID_W)[:, None]
    col = np.arange(GRID_W)[None, :]
    cs = np.clip(q - NA_WIN_C // 2, 0, GRID_W - NA_WIN_C)
    valid = (col >= cs) & (col < cs + NA_WIN_C)
    cidx = col - q + (NA_WIN_C - 1)
    n_b = 2 * NA_WIN_C - 1
    col_hot = np.zeros((LANE, GRID_W * GRID_W), np.float32)
    for qq in range(GRID_W):
        for cc in range(GRID_W):
            if valid[qq, cc]:
                col_hot[cidx[qq, cc], qq * GRID_W + cc] = 1.0
    row_hot = np.zeros((NA_WIN_R, NA_WIN_R, 2 * NA_WIN_R - 1), np.float32)
    for c in range(NA_WIN_R):
        for j in range(NA_WIN_R):
            row_hot[c, j, j - c + NA_WIN_R - 1] = 1.0
    mask = np.where(valid, 0.0, NEG).astype(np.float32)
    return col_hot, row_hot, mask, n_b


def na_bias_table(rpb):
    h = rpb.shape[0]
    col_hot, row_hot, mask, n_b = _na_onehots()
    t1 = jnp.einsum("cja,hab->hcjb", jnp.asarray(row_hot), rpb)
    t1 = jnp.pad(t1.reshape(h * NA_WIN_R * NA_WIN_R, n_b), ((0, 0), (0, LANE - n_b)))
    t2 = linear(t1, jnp.asarray(col_hot), True)
    t2 = t2.reshape(h, NA_WIN_R, NA_WIN_R, GRID_W, GRID_W) + jnp.asarray(mask)
    tab = jnp.transpose(t2, (0, 1, 3, 2, 4)).reshape(h, NA_WIN_R, GRID_W, NA_BAND)
    return jnp.concatenate([tab, jnp.full((h, 1, GRID_W, NA_BAND), NEG, F32)], axis=1)


def _first_step():
    return jnp.logical_and(pl.program_id(0) == 0, pl.program_id(1) == 0)


def _accum_out(ref, val, first):
    @pl.when(first)
    def _():
        ref[...] = val

    @pl.when(jnp.logical_not(first))
    def _():
        ref[...] += val


def _norm_head(xh, g):
    r = _rms(xh)
    yn = xh * r
    return yn * g, yn, r


def _norm_head_bwd(dy, yn, r, g):
    dg = jnp.sum(dy * yn, axis=0, keepdims=True)
    dyn = dy * g
    return r * (dyn - yn * jnp.mean(dyn * yn, axis=-1, keepdims=True)), dg


def _rope_signs(dh, start, rot_dim, n_heads):
    q = rot_dim // 4
    pos = np.arange(dh)
    quarter = (pos - start) // q
    inr = pos >= start
    sg = np.zeros((8, n_heads * dh), np.float32)
    sg[0] = np.tile(np.where(inr & (quarter % 2 == 0), -1.0, 0.0), n_heads)
    sg[1] = np.tile(np.where(inr & (quarter % 2 == 1), 1.0, 0.0), n_heads)
    return sg


def _rope_full(y, cos, sin, sg, q):
    w = y.shape[-1]
    rot = sg[0:1] * pltpu.roll(y, w - q, 1) + sg[1:2] * pltpu.roll(y, q, 1)
    return y * cos + rot * sin


def _rope_full_t(dy, cos, sin, sg, q):
    w = dy.shape[-1]
    z = dy * sin
    return dy * cos - sg[1:2] * pltpu.roll(z, q, 1) - sg[0:1] * pltpu.roll(z, w - q, 1)


def _hnr_call(x, g, cos, sin, sg, n_heads, q, dy=None):
    b, s, w = x.shape
    dh = w // n_heads
    ts = _pick(s, ROW_TILES)
    rope = cos is not None

    def body(*refs):
        refs = list(refs)
        x_ref, g_ref = refs[0], refs[1]
        k = 2
        if rope:
            cos_ref, sin_ref, sg_ref = refs[2], refs[3], refs[4]
            k = 5
        gv = g_ref[...]
        if dy is None:
            o_ref = refs[k]
            for h in range(n_heads):
                sl = slice(h * dh, (h + 1) * dh)
                o_ref[:, sl] = _norm_head(x_ref[:, sl], gv)[0]
            if rope:
                o_ref[...] = _rope_full(o_ref[...], cos_ref[...], sin_ref[...], sg_ref[...], q)
            return
        dy_ref, dx_ref, dg_ref = refs[k], refs[k + 1], refs[k + 2]
        src = dy_ref
        if rope:
            dx_ref[...] = _rope_full_t(dy_ref[...], cos_ref[...], sin_ref[...], sg_ref[...], q)
            src = dx_ref
        dg = jnp.zeros((1, dh), F32)
        for h in range(n_heads):
            sl = slice(h * dh, (h + 1) * dh)
            _, yn, r = _norm_head(x_ref[:, sl], gv)
            dxh, dgh = _norm_head_bwd(src[:, sl], yn, r, gv)
            dx_ref[:, sl] = dxh
            dg = dg + dgh
        _accum_out(dg_ref, dg, _first_step())

    row = pl.BlockSpec((None, ts, w), lambda bi, i: (bi, i, 0))
    whole = lambda a: pl.BlockSpec(a.shape, lambda bi, i: (0, 0))
    ins, specs = [x, g], [row, whole(g)]
    if rope:
        ins += [cos, sin, sg]
        specs += [pl.BlockSpec((ts, w), lambda bi, i: (i, 0)), pl.BlockSpec((ts, w), lambda bi, i: (i, 0)), whole(sg)]
    if dy is None:
        out_shape, out_specs = jax.ShapeDtypeStruct(x.shape, F32), row
    else:
        ins.append(dy)
        specs.append(row)
        out_shape = [jax.ShapeDtypeStruct(x.shape, F32), jax.ShapeDtypeStruct(g.shape, F32)]
        out_specs = [row, whole(g)]
    return pl.pallas_call(
        body, out_shape=out_shape, grid=(b, s // ts), in_specs=specs, out_specs=out_specs,
        name=f"hnr_{'b' if dy is not None else 'f'}_{n_heads}x{dh}_{int(rope)}", compiler_params=_params(),
    )(*ins)


@functools.partial(jax.custom_vjp, nondiff_argnums=(5, 6))
def head_norm_rope(x, g, cos, sin, sg, n_heads, q):
    return _hnr_call(x, g, cos, sin, sg, n_heads, q)


def _head_norm_rope_fwd(x, g, cos, sin, sg, n_heads, q):
    return _hnr_call(x, g, cos, sin, sg, n_heads, q), (x, g, cos, sin, sg)


def _head_norm_rope_bwd(n_heads, q, res, dy):
    x, g, cos, sin, sg = res
    dx, dg = _hnr_call(x, g, cos, sin, sg, n_heads, q, dy=dy)
    zero = lambda t: None if t is None else jnp.zeros_like(t)
    return dx, dg, zero(cos), zero(sin), zero(sg)


head_norm_rope.defvjp(_head_norm_rope_fwd, _head_norm_rope_bwd)


def _mla_k_call(kv, kr, g, cos, sin, sg, dkn=None):
    b, s, _ = kv.shape
    ts = _pick(s, ROW_TILES)
    hw = MLA_NOPE + MLA_V
    kn_w = MLA_HEADS * MLA_QK
    q = MLA_ROPE // 4

    def body(kv_ref, kr_ref, g_ref, cos_ref, sin_ref, sg_ref, *rest):
        gv = g_ref[...]
        krv = kr_ref[...]
        if dkn is None:
            (o_ref,) = rest
            for h in range(MLA_HEADS):
                kh = jnp.concatenate([kv_ref[:, h * hw:h * hw + MLA_NOPE], krv], axis=-1)
                o_ref[:, h * MLA_QK:(h + 1) * MLA_QK] = _norm_head(kh, gv)[0]
            o_ref[...] = _rope_full(o_ref[...], cos_ref[...], sin_ref[...], sg_ref[...], q)
            return
        dkn_ref, dkv_ref, dkr_ref, dg_ref, dy_ref = rest
        dy_ref[...] = _rope_full_t(dkn_ref[...], cos_ref[...], sin_ref[...], sg_ref[...], q)
        dg = jnp.zeros((1, MLA_QK), F32)
        dkr = jnp.zeros((ts, MLA_ROPE), F32)
        for h in range(MLA_HEADS):
            kh = jnp.concatenate([kv_ref[:, h * hw:h * hw + MLA_NOPE], krv], axis=-1)
            _, yn, r = _norm_head(kh, gv)
            dxh, dgh = _norm_head_bwd(dy_ref[:, h * MLA_QK:(h + 1) * MLA_QK], yn, r, gv)
            dkv_ref[:, h * hw:h * hw + MLA_NOPE] = dxh[:, :MLA_NOPE]
            dkv_ref[:, h * hw + MLA_NOPE:(h + 1) * hw] = jnp.zeros((ts, MLA_V), F32)
            dkr = dkr + dxh[:, MLA_NOPE:]
            dg = dg + dgh
        dkr_ref[...] = dkr
        _accum_out(dg_ref, dg, _first_step())

    row = lambda w: pl.BlockSpec((None, ts, w), lambda bi, i: (bi, i, 0))
    tab = pl.BlockSpec((ts, kn_w), lambda bi, i: (i, 0))
    whole = lambda a: pl.BlockSpec(a.shape, lambda bi, i: (0, 0))
    ins = [kv, kr, g, cos, sin, sg]
    specs = [row(kv.shape[2]), row(MLA_ROPE), whole(g), tab, tab, whole(sg)]
    scratch = []
    if dkn is None:
        out_shape, out_specs = jax.ShapeDtypeStruct((b, s, kn_w), F32), row(kn_w)
    else:
        ins.append(dkn)
        specs.append(row(kn_w))
        out_shape = [jax.ShapeDtypeStruct(kv.shape, F32), jax.ShapeDtypeStruct(kr.shape, F32),
                     jax.ShapeDtypeStruct(g.shape, F32)]
        out_specs = [row(kv.shape[2]), row(MLA_ROPE), whole(g)]
        scratch = [pltpu.VMEM((ts, kn_w), F32)]
    return pl.pallas_call(
        body, out_shape=out_shape, grid=(b, s // ts), in_specs=specs, out_specs=out_specs, scratch_shapes=scratch,
        name=f"mla_k_{'b' if dkn is not None else 'f'}", compiler_params=_params(),
    )(*ins)


@jax.custom_vjp
def mla_k_prep(kv, kr, g, cos, sin, sg):
    return _mla_k_call(kv, kr, g, cos, sin, sg)


def _mla_k_prep_fwd(kv, kr, g, cos, sin, sg):
    return _mla_k_call(kv, kr, g, cos, sin, sg), (kv, kr, g, cos, sin, sg)


def _mla_k_prep_bwd(res, dkn):
    kv, kr, g, cos, sin, sg = res
    dkv, dkr, dg = _mla_k_call(kv, kr, g, cos, sin, sg, dkn=dkn)
    return dkv, dkr, dg, jnp.zeros_like(cos), jnp.zeros_like(sin), jnp.zeros_like(sg)


mla_k_prep.defvjp(_mla_k_prep_fwd, _mla_k_prep_bwd)


class _HeadLayout:
    def __init__(self, groups, dq, dv, q_off, k_off, v_off, o_off, wq, wk, wv, wo, scale):
        self.groups, self.dq, self.dv, self.scale = groups, dq, dv, scale
        self.q_off, self.k_off, self.v_off, self.o_off = q_off, k_off, v_off, o_off
        self.wq, self.wk, self.wv, self.wo = wq, wk, wv, wo
        self.n_h = len(q_off)


def _gqa_layout():
    rep = GQA_Q_HEADS // GQA_KV_HEADS
    n_h = GQA_Q_HEADS // 2
    return _HeadLayout(2, HEAD_DIM, HEAD_DIM, [h * HEAD_DIM for h in range(n_h)], [(h // rep) * HEAD_DIM for h in range(n_h)],
                       [(h // rep) * HEAD_DIM for h in range(n_h)], [h * HEAD_DIM for h in range(n_h)],
                       n_h * HEAD_DIM, (n_h // rep) * HEAD_DIM, (n_h // rep) * HEAD_DIM, n_h * HEAD_DIM, HEAD_DIM ** -0.5)


def _mla_layout():
    n_h = MLA_HEADS // 2
    hw = MLA_NOPE + MLA_V
    return _HeadLayout(2, MLA_QK, MLA_V, [h * MLA_QK for h in range(n_h)], [h * MLA_QK for h in range(n_h)],
                       [h * hw + MLA_NOPE for h in range(n_h)], [h * MLA_V for h in range(n_h)],
                       n_h * MLA_QK, n_h * MLA_QK, n_h * hw, n_h * MLA_V, MLA_QK ** -0.5)


def _attn_tm_fwd(q, k, v, lay, n_ctx):
    b, s, _ = q.shape
    tq = min(256, n_ctx)
    nc = n_ctx // tq

    def body(q_ref, k_ref, v_ref, o_ref, lse_ref):
        def run(n_keys):
            for h in range(lay.n_h):
                qo, ko, vo, oo = lay.q_off[h], lay.k_off[h], lay.v_off[h], lay.o_off[h]
                qv = (q_ref[:, qo:qo + lay.dq] * lay.scale).astype(BF16)
                sc = lax.dot_general(qv, k_ref[0:n_keys, ko:ko + lay.dq].astype(BF16), _NT, preferred_element_type=F32)
                m = jnp.max(sc, axis=-1, keepdims=True)
                p = jnp.exp(sc - m)
                l = jnp.sum(p, axis=-1, keepdims=True)
                o = jnp.dot(p.astype(BF16), v_ref[0:n_keys, vo:vo + lay.dv].astype(BF16), preferred_element_type=F32)
                o_ref[:, oo:oo + lay.dv] = o / l
                lse_ref[:, h:h + 1] = m + jnp.log(l)

        pl.when(pl.program_id(2) < nc)(lambda: run(n_ctx))
        pl.when(pl.program_id(2) >= nc)(lambda: run(s))

    return pl.pallas_call(
        body, out_shape=[jax.ShapeDtypeStruct((b, s, lay.groups * lay.wo), F32),
                         jax.ShapeDtypeStruct((b, lay.groups, s, lay.n_h), F32)],
        grid=(b, lay.groups, s // tq),
        in_specs=[pl.BlockSpec((None, tq, lay.wq), lambda bi, g, i: (bi, i, g)),
                  pl.BlockSpec((None, s, lay.wk), lambda bi, g, i: (bi, 0, g)),
                  pl.BlockSpec((None, s, lay.wv), lambda bi, g, i: (bi, 0, g))],
        out_specs=[pl.BlockSpec((None, tq, lay.wo), lambda bi, g, i: (bi, i, g)),
                   pl.BlockSpec((None, None, tq, lay.n_h), lambda bi, g, i: (bi, g, i, 0))],
        name=f"attn_tm_f_{lay.dq}", compiler_params=_params(),
    )(q, k, v)


def _attn_tm_bwd(q, k, v, lse, o, do, lay, n_ctx):
    b, s, _ = q.shape
    tk = min(256, n_ctx)
    nc = n_ctx // tk

    def body(q_ref, k_ref, v_ref, lse_ref, o_ref, do_ref, dq_ref, dk_ref, dv_ref, delta_ref):
        @pl.when(pl.program_id(2) == 0)
        def _():
            dq_ref[...] = jnp.zeros_like(dq_ref)
            for h in range(lay.n_h):
                oo = lay.o_off[h]
                delta_ref[:, h:h + 1] = jnp.sum(o_ref[:, oo:oo + lay.dv] * do_ref[:, oo:oo + lay.dv], axis=-1,
                                                keepdims=True)

        def run(r0):
            dk_acc, dv_acc = {}, {}
            for h in range(lay.n_h):
                qo, ko, vo, oo = lay.q_off[h], lay.k_off[h], lay.v_off[h], lay.o_off[h]
                kh = k_ref[:, ko:ko + lay.dq].astype(BF16)
                vh = v_ref[:, vo:vo + lay.dv].astype(BF16)
                qv = (q_ref[r0:s, qo:qo + lay.dq] * lay.scale).astype(BF16)
                dob = do_ref[r0:s, oo:oo + lay.dv].astype(BF16)
                sc = lax.dot_general(qv, kh, _NT, preferred_element_type=F32)
                p = jnp.exp(sc - lse_ref[r0:s, h:h + 1])
                dvh = lax.dot_general(p.astype(BF16), dob, _TN, preferred_element_type=F32)
                dp = lax.dot_general(dob, vh, _NT, preferred_element_type=F32)
                dsb = (p * (dp - delta_ref[r0:s, h:h + 1])).astype(BF16)
                dkh = lax.dot_general(dsb, qv, _TN, preferred_element_type=F32)
                dq_ref[r0:s, qo:qo + lay.dq] += jnp.dot(dsb, kh, preferred_element_type=F32) * lay.scale
                dk_acc[ko] = dkh if ko not in dk_acc else dk_acc[ko] + dkh
                dv_acc[vo] = dvh if vo not in dv_acc else dv_acc[vo] + dvh
            if len(dv_acc) * lay.dv != lay.wv:
                dv_ref[...] = jnp.zeros_like(dv_ref)
            for ko, val in dk_acc.items():
                dk_ref[:, ko:ko + lay.dq] = val
            for vo, val in dv_acc.items():
                dv_ref[:, vo:vo + lay.dv] = val

        pl.when(pl.program_id(2) < nc)(lambda: run(0))
        pl.when(pl.program_id(2) >= nc)(lambda: run(n_ctx))

    full = lambda w: pl.BlockSpec((None, s, w), lambda bi, g, j: (bi, 0, g))
    blk = lambda w: pl.BlockSpec((None, tk, w), lambda bi, g, j: (bi, j, g))
    stat = pl.BlockSpec((None, None, s, lay.n_h), lambda bi, g, j: (bi, g, 0, 0))
    return pl.pallas_call(
        body, out_shape=[jax.ShapeDtypeStruct(q.shape, F32), jax.ShapeDtypeStruct(k.shape, F32),
                         jax.ShapeDtypeStruct(v.shape, F32)],
        grid=(b, lay.groups, s // tk),
        in_specs=[full(lay.wq), blk(lay.wk), blk(lay.wv), stat, full(lay.wo), full(lay.wo)],
        out_specs=[full(lay.wq), blk(lay.wk), blk(lay.wv)],
        scratch_shapes=[pltpu.VMEM((s, lay.n_h), F32)],
        name=f"attn_tm_b_{lay.dq}", compiler_params=_params(),
    )(q, k, v, lse, o, do)


def _make_attention_tm(lay):
    @functools.partial(jax.custom_vjp, nondiff_argnums=(3,))
    def op(q, k, v, n_ctx):
        return _attn_tm_fwd(q, k, v, lay, n_ctx)[0]

    def fwd(q, k, v, n_ctx):
        o, lse = _attn_tm_fwd(q, k, v, lay, n_ctx)
        return o, (q, k, v, o, lse)

    def bwd(n_ctx, res, do):
        q, k, v, o, lse = res
        return _attn_tm_bwd(q, k, v, lse, o, do, lay, n_ctx)

    op.defvjp(fwd, bwd)
    return op


gqa_attention = _make_attention_tm(_gqa_layout())
mla_attention = _make_attention_tm(_mla_layout())

NA_GROUPS_FWD = 1
NA_GROUPS_BWD = 2


def _na_tm_specs(s, nc, rows, groups):
    hg = NA_HEADS // groups
    w = hg * HEAD_DIM
    qs = pl.BlockSpec((None, GRID_W, w), lambda bi, g, i: (bi, i, g))
    ks = pl.BlockSpec((None, s, w), lambda bi, g, i: (bi, 0, g))
    bs = pl.BlockSpec((hg, None, GRID_W, NA_BAND), lambda bi, g, i: (g, _na_geometry(i, nc, rows)[1], 0, 0))
    ls = pl.BlockSpec((None, None, GRID_W, hg), lambda bi, g, i: (bi, g, i, 0))
    return hg, w, qs, ks, bs, ls


def _na_tm_scores(q_ref, k_ref, bias_ref, hd, n_ctx, start, scale):
    sl = slice(hd * HEAD_DIM, (hd + 1) * HEAD_DIM)
    qv = (q_ref[:, sl] * scale).astype(BF16)
    kc = k_ref[0:n_ctx, sl].astype(BF16)
    kb = k_ref[pl.ds(start, NA_BAND), sl].astype(BF16)
    s_c = lax.dot_general(qv, kc, _NT, preferred_element_type=F32)
    s_l = lax.dot_general(qv, kb, _NT, preferred_element_type=F32) + bias_ref[hd]
    return sl, qv, kc, kb, s_c, s_l


def _na_tm_fwd(q, k, v, bias, n_ctx):
    b, s, _ = q.shape
    nc = n_ctx // GRID_W
    rows = (s - n_ctx) // GRID_W
    scale = HEAD_DIM ** -0.5
    hg, w, qs, ks, bs, ls = _na_tm_specs(s, nc, rows, NA_GROUPS_FWD)

    def body(q_ref, k_ref, v_ref, bias_ref, o_ref, lse_ref):
        rs, _ = _na_geometry(pl.program_id(2), nc, rows)
        start = pl.multiple_of(n_ctx + rs * GRID_W, GRID_W)
        for hd in range(hg):
            sl, _, _, _, s_c, s_l = _na_tm_scores(q_ref, k_ref, bias_ref, hd, n_ctx, start, scale)
            m = jnp.maximum(jnp.max(s_c, axis=-1, keepdims=True), jnp.max(s_l, axis=-1, keepdims=True))
            p_c = jnp.exp(s_c - m)
            p_l = jnp.exp(s_l - m)
            l = jnp.sum(p_c, axis=-1, keepdims=True) + jnp.sum(p_l, axis=-1, keepdims=True)
            o = jnp.dot(p_c.astype(BF16), v_ref[0:n_ctx, sl].astype(BF16), preferred_element_type=F32)
            o = o + jnp.dot(p_l.astype(BF16), v_ref[pl.ds(start, NA_BAND), sl].astype(BF16), preferred_element_type=F32)
            o_ref[:, sl] = o / l
            lse_ref[:, hd:hd + 1] = m + jnp.log(l)

    return pl.pallas_call(
        body, out_shape=[jax.ShapeDtypeStruct(q.shape, F32), jax.ShapeDtypeStruct((b, NA_GROUPS_FWD, s, hg), F32)],
        grid=(b, NA_GROUPS_FWD, s // GRID_W), in_specs=[qs, ks, ks, bs], out_specs=[qs, ls],
        name=f"na_tm_f_{s}", compiler_params=_params(),
    )(q, k, v, bias)


def _na_tm_bwd(q, k, v, bias, o, lse, do, n_ctx):
    b, s, _ = q.shape
    nc = n_ctx // GRID_W
    rows = (s - n_ctx) // GRID_W
    scale = HEAD_DIM ** -0.5
    n_cls = NA_WIN_R + 1
    hg, w, qs, ks, bs, ls = _na_tm_specs(s, nc, rows, NA_GROUPS_BWD)
    lse = jnp.transpose(lse, (0, 2, 1, 3)).reshape(b, s, NA_GROUPS_BWD, hg)
    lse = jnp.transpose(lse, (0, 2, 1, 3))

    def body(q_ref, k_ref, v_ref, bias_ref, o_ref, lse_ref, do_ref, dq_ref, dk_ref, dv_ref, db_ref):
        i = pl.program_id(2)
        rs, cls = _na_geometry(i, nc, rows)
        _, cls_prev = _na_geometry(i - 1, nc, rows)
        start = pl.multiple_of(n_ctx + rs * GRID_W, GRID_W)
        first = jnp.logical_or(i == 0, cls != cls_prev)

        @pl.when(i == 0)
        def _():
            dk_ref[...] = jnp.zeros_like(dk_ref)
            dv_ref[...] = jnp.zeros_like(dv_ref)

        @pl.when(first)
        def _():
            db_ref[...] = jnp.zeros_like(db_ref)

        for hd in range(hg):
            sl, qv, kc, kb, s_c, s_l = _na_tm_scores(q_ref, k_ref, bias_ref, hd, n_ctx, start, scale)
            lse_v = lse_ref[:, hd:hd + 1]
            p_c = jnp.exp(s_c - lse_v)
            p_l = jnp.exp(s_l - lse_v)
            dov = do_ref[:, sl]
            dob = dov.astype(BF16)
            delta = jnp.sum(dov * o_ref[:, sl], axis=-1, keepdims=True)
            vc = v_ref[0:n_ctx, sl].astype(BF16)
            vb = v_ref[pl.ds(start, NA_BAND), sl].astype(BF16)
            ds_c = p_c * (lax.dot_general(dob, vc, _NT, preferred_element_type=F32) - delta)
            ds_l = p_l * (lax.dot_general(dob, vb, _NT, preferred_element_type=F32) - delta)
            dsc_b = ds_c.astype(BF16)
            dsl_b = ds_l.astype(BF16)
            dq_ref[:, sl] = (jnp.dot(dsc_b, kc, preferred_element_type=F32)
                             + jnp.dot(dsl_b, kb, preferred_element_type=F32)) * scale
            dk_ref[0:n_ctx, sl] += lax.dot_general(dsc_b, qv, _TN, preferred_element_type=F32)
            dk_ref[pl.ds(start, NA_BAND), sl] += lax.dot_general(dsl_b, qv, _TN, preferred_element_type=F32)
            dv_ref[0:n_ctx, sl] += lax.dot_general(p_c.astype(BF16), dob, _TN, preferred_element_type=F32)
            dv_ref[pl.ds(start, NA_BAND), sl] += lax.dot_general(p_l.astype(BF16), dob, _TN, preferred_element_type=F32)
            db_ref[hd] += ds_l

    dbs = pl.BlockSpec((None, hg, None, GRID_W, NA_BAND), lambda bi, g, i: (bi, g, _na_geometry(i, nc, rows)[1], 0, 0))
    return pl.pallas_call(
        body,
        out_shape=[jax.ShapeDtypeStruct(q.shape, F32), jax.ShapeDtypeStruct(q.shape, F32), jax.ShapeDtypeStruct(q.shape, F32),
                   jax.ShapeDtypeStruct((b, NA_HEADS, n_cls, GRID_W, NA_BAND), F32)],
        grid=(b, NA_GROUPS_BWD, s // GRID_W), in_specs=[qs, ks, ks, bs, qs, ls, qs], out_specs=[qs, ks, ks, dbs],
        name=f"na_tm_b_{s}", compiler_params=_params(),
    )(q, k, v, bias, o, lse, do)


@functools.partial(jax.custom_vjp, nondiff_argnums=(4,))
def na_attention_tm(q, k, v, bias, n_ctx):
    return _na_tm_fwd(q, k, v, bias, n_ctx)[0]


def _na_attention_tm_fwd(q, k, v, bias, n_ctx):
    o, lse = _na_tm_fwd(q, k, v, bias, n_ctx)
    return o, (q, k, v, bias, o, lse)


def _na_attention_tm_bwd(n_ctx, res, do):
    q, k, v, bias, o, lse = res
    dq, dk, dv, db = _na_tm_bwd(q, k, v, bias, o, lse, do, n_ctx)
    return dq, dk, dv, _sum_rows(db.reshape(db.shape[0], -1, NA_BAND), db.shape[0]).reshape(db.shape[1:])


na_attention_tm.defvjp(_na_attention_tm_fwd, _na_attention_tm_bwd)


def _cmul(ar, ai, br, bi):
    return ar * br - ai * bi, ar * bi + ai * br


def _s5_chunk(n_ctx):
    return min(256, n_ctx)


def _s5_powers(a_re, a_im, t_len):
    a_re, a_im = lax.stop_gradient(a_re), lax.stop_gradient(a_im)
    mag = jnp.sqrt(a_re * a_re + a_im * a_im)
    th = jnp.arctan2(a_im, a_re)
    t = jnp.arange(t_len + 1, dtype=F32)[:, None]
    pm = jnp.where(t == 0, 1.0, jnp.exp(t * jnp.log(jnp.maximum(mag, 1e-37))) * (mag > 0))
    return jnp.stack([pm * jnp.cos(t * th), pm * jnp.sin(t * th)])


def _s5_tables(pw, t_len, rev, conj=False):
    if conj:
        pw = pw * jnp.asarray([1.0, -1.0], F32)[:, None, None]
    steps = jnp.concatenate([pw[:, min(2 ** i, t_len)][:, None] for i in range(8)], axis=1)
    tile = pw[:, 1:9]
    a8k = pw[:, 0:t_len:8]
    if rev:
        tile, a8k = tile[:, ::-1], a8k[:, ::-1]
    misc = jnp.concatenate([pw[:, t_len:t_len + 1], jnp.zeros((2, 7, pw.shape[-1]), F32)], axis=1)
    return jnp.concatenate([steps, tile, misc, a8k], axis=1)


def _scan_chunk(x_re, x_im, tab_ref, hin_re, hin_im, rev, t_len, xs_ref, es_ref):
    outs = [_scan_slab(x_re[:, k:k + LANE], x_im[:, k:k + LANE], tab_ref, hin_re[:, k:k + LANE], hin_im[:, k:k + LANE],
                       rev, t_len, xs_ref, es_ref, k) for k in range(0, x_re.shape[-1], LANE)]
    return tuple(jnp.concatenate([o[t] for o in outs], axis=-1) for t in range(4))


def _scan_slab(x_re, x_im, tab_ref, hin_re, hin_im, rev, t_len, xs_ref, es_ref, k0):
    lanes = LANE
    n2 = t_len // 8
    tab_ref = tab_ref.at[:, :, k0:k0 + LANE]
    rin = lax.broadcasted_iota(jnp.int32, (t_len, lanes), 0) & 7
    for li, sh in enumerate((1, 2, 4)):
        m_re, m_im = tab_ref[0, li:li + 1, :], tab_ref[1, li:li + 1, :]
        amt = sh if not rev else t_len - sh
        c_re, c_im = _cmul(m_re, m_im, pltpu.roll(x_re, amt, 0), pltpu.roll(x_im, amt, 0))
        ok = (rin >= sh) if not rev else (rin < 8 - sh)
        x_re = x_re + jnp.where(ok, c_re, 0.0)
        x_im = x_im + jnp.where(ok, c_im, 0.0)
    xr_ref, xi_ref = xs_ref
    xr_ref[...] = x_re
    xi_ref[...] = x_im
    off = 0 if rev else 7
    e_re = xr_ref[pl.ds(off, n2, stride=8), :]
    e_im = xi_ref[pl.ds(off, n2, stride=8), :]
    row2 = lax.broadcasted_iota(jnp.int32, (n2, lanes), 0)
    sh, li = 1, 3
    while sh < n2:
        m_re, m_im = tab_ref[0, li:li + 1, :], tab_ref[1, li:li + 1, :]
        amt = sh if not rev else n2 - sh
        c_re, c_im = _cmul(m_re, m_im, pltpu.roll(e_re, amt, 0), pltpu.roll(e_im, amt, 0))
        ok = (row2 >= sh) if not rev else (row2 < n2 - sh)
        e_re = e_re + jnp.where(ok, c_re, 0.0)
        e_im = e_im + jnp.where(ok, c_im, 0.0)
        sh, li = sh * 2, li + 1
    es_ref[0] = e_re
    es_ref[1] = e_im
    last = 0 if rev else n2 - 1
    t_re, t_im = _cmul(tab_ref[0, 16:17, :], tab_ref[1, 16:17, :], hin_re, hin_im)
    hout_re = es_ref[0, last:last + 1, :] + t_re
    hout_im = es_ref[1, last:last + 1, :] + t_im
    amt = 1 if not rev else n2 - 1
    ok = (row2 >= 1) if not rev else (row2 < n2 - 1)
    k_re, k_im = _cmul(tab_ref[0, 24:24 + n2, :], tab_ref[1, 24:24 + n2, :], hin_re, hin_im)
    c_re = jnp.where(ok, pltpu.roll(e_re, amt, 0), 0.0) + k_re
    c_im = jnp.where(ok, pltpu.roll(e_im, amt, 0), 0.0) + k_im
    tp_re, tp_im = tab_ref[0, 8:16, :][None], tab_ref[1, 8:16, :][None]
    add_re, add_im = _cmul(tp_re, tp_im, c_re[:, None, :], c_im[:, None, :])
    h_re = xr_ref[...] + add_re.reshape(t_len, lanes)
    h_im = xi_ref[...] + add_im.reshape(t_len, lanes)
    return h_re, h_im, hout_re, hout_im


def _s5_order(j, n_chunks, nc, rev):
    if not rev:
        return j
    return jnp.where(j < nc, nc - 1 - j, n_chunks - 1 - (j - nc))


def _s5_fwd(u, tab, b_bd, c_bd, n_ctx, rev):
    b, s, w = u.shape
    lanes = b_bd.shape[-1]
    t_len = _s5_chunk(n_ctx)
    n_chunks, nc = s // t_len, n_ctx // t_len

    def body(u_ref, tab_ref, b_ref, c_ref, y_ref, h_ref, hin_ref, carry_ref, xr_ref, xi_ref, es_ref):
        xs_ref = (xr_ref, xi_ref)

        @pl.when(pl.program_id(1) == 0)
        def _():
            carry_ref[...] = jnp.zeros_like(carry_ref)

        ub = u_ref[...].astype(BF16)
        x_re = jnp.dot(ub, b_ref[0].astype(BF16), preferred_element_type=F32)
        x_im = jnp.dot(ub, b_ref[1].astype(BF16), preferred_element_type=F32)
        hin_re, hin_im = carry_ref[0, 0:1, :], carry_ref[1, 0:1, :]
        hin_ref[...] = carry_ref[...]
        h_re, h_im, ho_re, ho_im = _scan_chunk(x_re, x_im, tab_ref, hin_re, hin_im, rev, t_len, xs_ref, es_ref)
        carry_ref[0] = jnp.broadcast_to(ho_re, (8, lanes))
        carry_ref[1] = jnp.broadcast_to(ho_im, (8, lanes))
        h_ref[0] = h_re
        h_ref[1] = h_im
        y_ref[...] = (jnp.dot(h_re.astype(BF16), c_ref[0].astype(BF16), preferred_element_type=F32)
                      - jnp.dot(h_im.astype(BF16), c_ref[1].astype(BF16), preferred_element_type=F32))

    order = lambda j: _s5_order(j, n_chunks, nc, rev)
    whole = lambda arr: pl.BlockSpec(arr.shape, lambda bi, j: (0,) * arr.ndim)
    return pl.pallas_call(
        body,
        out_shape=[jax.ShapeDtypeStruct((b, s, w), F32), jax.ShapeDtypeStruct((2, b, s, lanes), F32),
                   jax.ShapeDtypeStruct((2, b, n_chunks, 8, lanes), F32)],
        grid=(b, n_chunks),
        in_specs=[pl.BlockSpec((None, t_len, w), lambda bi, j: (bi, order(j), 0)), whole(tab), whole(b_bd), whole(c_bd)],
        out_specs=[pl.BlockSpec((None, t_len, w), lambda bi, j: (bi, order(j), 0)),
                   pl.BlockSpec((2, None, t_len, lanes), lambda bi, j: (0, bi, order(j), 0)),
                   pl.BlockSpec((2, None, None, 8, lanes), lambda bi, j: (0, bi, order(j), 0, 0))],
        scratch_shapes=[pltpu.VMEM((2, 8, lanes), F32), pltpu.VMEM((t_len, LANE), F32), pltpu.VMEM((t_len, LANE), F32),
                        pltpu.VMEM((2, t_len // 8, LANE), F32)],
        name=f"s5_f_{s}_{int(rev)}", compiler_params=_params(),
    )(u, tab, b_bd, c_bd)


def _s5_bwd(u, tab_adj, b_bd, c_bd, h, hin, dy, n_ctx, rev):
    b, s, w = u.shape
    lanes = b_bd.shape[-1]
    t_len = _s5_chunk(n_ctx)
    n_chunks, nc = s // t_len, n_ctx // t_len
    arev = not rev

    def body(u_ref, tab_ref, b_ref, c_ref, h_ref, hin_ref, dy_ref, du_ref, db_ref, dc_ref, da_ref,
             carry_ref, xr_ref, xi_ref, es_ref):
        xs_ref = (xr_ref, xi_ref)
        first = jnp.logical_and(pl.program_id(0) == 0, pl.program_id(1) == 0)

        @pl.when(pl.program_id(1) == 0)
        def _():
            carry_ref[...] = jnp.zeros_like(carry_ref)

        dyv = dy_ref[...]
        dyb = dyv.astype(BF16)
        dn = (((1,), (1,)), ((), ()))
        dt = (((0,), (0,)), ((), ()))
        x_re = lax.dot_general(dyb, c_ref[0].astype(BF16), dn, preferred_element_type=F32)
        x_im = -lax.dot_general(dyb, c_ref[1].astype(BF16), dn, preferred_element_type=F32)
        g_re, g_im, go_re, go_im = _scan_chunk(x_re, x_im, tab_ref, carry_ref[0, 0:1, :], carry_ref[1, 0:1, :],
                                               arev, t_len, xs_ref, es_ref)
        carry_ref[0] = jnp.broadcast_to(go_re, (8, lanes))
        carry_ref[1] = jnp.broadcast_to(go_im, (8, lanes))
        h_re, h_im = h_ref[0], h_ref[1]
        gb_re, gb_im = g_re.astype(BF16), g_im.astype(BF16)
        du_ref[...] = (lax.dot_general(gb_re, b_ref[0].astype(BF16), dn, preferred_element_type=F32)
                       + lax.dot_general(gb_im, b_ref[1].astype(BF16), dn, preferred_element_type=F32))
        ub = u_ref[...].astype(BF16)
        db_re = lax.dot_general(ub, gb_re, dt, preferred_element_type=F32)
        db_im = lax.dot_general(ub, gb_im, dt, preferred_element_type=F32)
        dc_re = lax.dot_general(h_re.astype(BF16), dyb, dt, preferred_element_type=F32)
        dc_im = -lax.dot_general(h_im.astype(BF16), dyb, dt, preferred_element_type=F32)
        row = lax.broadcasted_iota(jnp.int32, (t_len, lanes), 0)
        amt = 1 if not rev else t_len - 1
        edge = (row == 0) if not rev else (row == t_len - 1)
        hp_re = jnp.where(edge, hin_ref[0, 0:1, :], pltpu.roll(h_re, amt, 0))
        hp_im = jnp.where(edge, hin_ref[1, 0:1, :], pltpu.roll(h_im, amt, 0))
        da_re = jnp.sum(g_re * hp_re + g_im * hp_im, axis=0, keepdims=True)
        da_im = jnp.sum(g_im * hp_re - g_re * hp_im, axis=0, keepdims=True)

        @pl.when(first)
        def _():
            db_ref[0], db_ref[1] = db_re, db_im
            dc_ref[0], dc_ref[1] = dc_re, dc_im
            da_ref[0] = jnp.broadcast_to(da_re, (8, lanes))
            da_ref[1] = jnp.broadcast_to(da_im, (8, lanes))

        @pl.when(jnp.logical_not(first))
        def _():
            db_ref[0] += db_re
            db_ref[1] += db_im
            dc_ref[0] += dc_re
            dc_ref[1] += dc_im
            da_ref[0] += jnp.broadcast_to(da_re, (8, lanes))
            da_ref[1] += jnp.broadcast_to(da_im, (8, lanes))

    order = lambda j: _s5_order(n_chunks - 1 - j, n_chunks, nc, rev)
    whole = lambda arr: pl.BlockSpec(arr.shape, lambda bi, j: (0,) * arr.ndim)
    us = pl.BlockSpec((None, t_len, w), lambda bi, j: (bi, order(j), 0))
    return pl.pallas_call(
        body,
        out_shape=[jax.ShapeDtypeStruct((b, s, w), F32), jax.ShapeDtypeStruct(b_bd.shape, F32),
                   jax.ShapeDtypeStruct(c_bd.shape, F32), jax.ShapeDtypeStruct((2, 8, lanes), F32)],
        grid=(b, n_chunks),
        in_specs=[us, whole(tab_adj), whole(b_bd), whole(c_bd),
                  pl.BlockSpec((2, None, t_len, lanes), lambda bi, j: (0, bi, order(j), 0)),
                  pl.BlockSpec((2, None, None, 8, lanes), lambda bi, j: (0, bi, order(j), 0, 0)), us],
        out_specs=[us, whole(b_bd), whole(c_bd), pl.BlockSpec((2, 8, lanes), lambda bi, j: (0, 0, 0))],
        scratch_shapes=[pltpu.VMEM((2, 8, lanes), F32), pltpu.VMEM((t_len, LANE), F32), pltpu.VMEM((t_len, LANE), F32),
                        pltpu.VMEM((2, t_len // 8, LANE), F32)],
        name=f"s5_b_{s}_{int(rev)}", compiler_params=_params(),
    )(u, tab_adj, b_bd, c_bd, h, hin, dy)


@functools.partial(jax.custom_vjp, nondiff_argnums=(4, 5))
def s5_direction(u, a, b_bd, c_bd, n_ctx, rev):
    t_len = _s5_chunk(n_ctx)
    return _s5_fwd(u, _s5_tables(_s5_powers(a[0], a[1], t_len), t_len, rev), b_bd, c_bd, n_ctx, rev)[0]


def _s5_direction_fwd(u, a, b_bd, c_bd, n_ctx, rev):
    t_len = _s5_chunk(n_ctx)
    pw = _s5_powers(a[0], a[1], t_len)
    y, h, hin = _s5_fwd(u, _s5_tables(pw, t_len, rev), b_bd, c_bd, n_ctx, rev)
    return y, (u, pw, b_bd, c_bd, h, hin)


def _s5_direction_bwd(n_ctx, rev, res, dy):
    u, pw, b_bd, c_bd, h, hin = res
    tab_adj = _s5_tables(pw, _s5_chunk(n_ctx), not rev, conj=True)
    du, db, dc, da = _s5_bwd(u, tab_adj, b_bd, c_bd, h, hin, dy, n_ctx, rev)
    return du, da[:, 0, :], db, dc


s5_direction.defvjp(_s5_direction_fwd, _s5_direction_bwd)


def _s5_discretize(lam_re, lam_im, log_dt, b_re, b_im):
    dt = jnp.exp(log_dt)[:, None]
    mag = jnp.exp(lam_re * dt)
    a_re = mag * jnp.cos(lam_im * dt)
    a_im = mag * jnp.sin(lam_im * dt)
    den = jnp.square(lam_re) + jnp.square(lam_im)
    f_re = ((a_re - 1.0) * lam_re + a_im * lam_im) / den
    f_im = (a_im * lam_re - (a_re - 1.0) * lam_im) / den
    bb_re = f_re[..., None] * b_re - f_im[..., None] * b_im
    bb_im = f_re[..., None] * b_im + f_im[..., None] * b_re
    return a_re, a_im, bb_re, bb_im


def _block_diag(t):
    g, r, c = t.shape
    return (jnp.eye(g, dtype=F32)[:, None, :, None] * t[:, :, None, :]).reshape(g * r, g * c)


def _loss_head(y, target):
    b, n, d = y.shape
    ts = _pick(n, (256, 128, 64))

    def body(y_ref, t_ref, loss_ref, dy_ref):
        first = jnp.logical_and(pl.program_id(0) == 0, pl.program_id(1) == 0)
        err = y_ref[...] - t_ref[...]
        dy_ref[...] = err * (1.0 / d)
        part = 0.5 * jnp.sum(jnp.sum(err * err, axis=-1, keepdims=True) * (1.0 / d), axis=0, keepdims=True)
        part = jnp.broadcast_to(part, (8, LANE))

        @pl.when(first)
        def _():
            loss_ref[...] = part

        @pl.when(jnp.logical_not(first))
        def _():
            loss_ref[...] += part

    blk = pl.BlockSpec((None, ts, d), lambda bi, i: (bi, i, 0))
    return pl.pallas_call(
        body, out_shape=[jax.ShapeDtypeStruct((8, LANE), F32), jax.ShapeDtypeStruct((b, n, d), F32)],
        grid=(b, n // ts), in_specs=[blk, blk], out_specs=[pl.BlockSpec((8, LANE), lambda bi, i: (0, 0)), blk],
        name="loss_head", compiler_params=_params(),
    )(y, target)


def _adamw(w, g, m, v):
    shape = w.shape
    n = int(np.prod(shape))
    cols = shape[-1]
    r = n // cols
    tr = _pick(r, (512, 256, 128, 64, 32, 16, 8))
    c1 = 1.0 / (1.0 - ADAM_B1 ** ADAM_STEP)
    c2 = 1.0 / (1.0 - ADAM_B2 ** ADAM_STEP)

    def body(w_ref, g_ref, m_ref, v_ref, d_ref, mo_ref, vo_ref):
        gv = g_ref[...]
        m2 = ADAM_B1 * m_ref[...] + (1.0 - ADAM_B1) * gv
        v2 = ADAM_B2 * v_ref[...] + (1.0 - ADAM_B2) * (gv * gv)
        d_ref[...] = -ADAM_LR * ((m2 * c1) / (jnp.sqrt(v2 * c2) + ADAM_EPS) + ADAM_WD * w_ref[...])
        mo_ref[...] = m2
        vo_ref[...] = v2

    blk = pl.BlockSpec((tr, cols), lambda i: (i, 0))
    outs = pl.pallas_call(
        body, out_shape=[jax.ShapeDtypeStruct((r, cols), F32)] * 3, grid=(r // tr,),
        in_specs=[blk] * 4, out_specs=[blk] * 3, name=f"adamw_{r}x{cols}", compiler_params=_params(),
    )(*[t.reshape(r, cols) for t in (w, g, m, v)])
    return tuple(o.reshape(shape) for o in outs)


def _sum_rows(x, n):
    _, r, c = x.shape
    tr = _pick(r, (512, 256, 128, 64, 32, 16, 8))

    def body(x_ref, o_ref):
        acc = x_ref[0]
        for j in range(1, n):
            acc = acc + x_ref[j]
        o_ref[...] = acc

    return pl.pallas_call(
        body, out_shape=jax.ShapeDtypeStruct((r, c), F32), grid=(r // tr,),
        in_specs=[pl.BlockSpec((n, tr, c), lambda i: (0, i, 0))], out_specs=pl.BlockSpec((tr, c), lambda i: (i, 0)),
        name=f"sum{n}_{r}x{c}", compiler_params=_params(),
    )(x)


def _accumulate(parts, out_dtype):
    r, c = parts[0].shape[-2:]
    tr = _pick(r, (512, 256, 128, 64, 32, 16))

    def body(*refs):
        acc = None
        for ref in refs[:-1]:
            terms = [ref[j] for j in range(ref.shape[0])] if len(ref.shape) == 3 else [ref[...]]
            for t in terms:
                acc = t.astype(F32) if acc is None else acc + t.astype(F32)
        refs[-1][...] = acc.astype(out_dtype)

    specs = [pl.BlockSpec((p.shape[0], tr, c), lambda i: (0, i, 0)) if p.ndim == 3 else pl.BlockSpec((tr, c), lambda i: (i, 0))
             for p in parts]
    tag = "_".join(str(p.shape[0]) if p.ndim == 3 else "1" for p in parts)
    return pl.pallas_call(
        body, out_shape=jax.ShapeDtypeStruct((r, c), out_dtype), grid=(r // tr,), in_specs=specs,
        out_specs=pl.BlockSpec((tr, c), lambda i: (i, 0)), name=f"accumulate_{tag}_{r}x{c}_{jnp.dtype(out_dtype).name}",
        compiler_params=_params(),
    )(*parts)


def _add2(x, y):
    shape = x.shape
    c = shape[-1]
    r = int(np.prod(shape)) // c
    tr = _pick(r, (512, 256, 128, 64, 32, 16, 8))

    def body(x_ref, y_ref, o_ref):
        o_ref[...] = x_ref[...] + y_ref[...]

    blk = pl.BlockSpec((tr, c), lambda i: (i, 0))
    return pl.pallas_call(
        body, out_shape=jax.ShapeDtypeStruct((r, c), F32), grid=(r // tr,), in_specs=[blk, blk], out_specs=blk,
        name=f"add2_{r}x{c}", compiler_params=_params(),
    )(x.reshape(r, c), y.reshape(r, c)).reshape(shape)


_FLIPS = ((1, 0), (0, 1), (1, 1))


def _me():
    return lax.axis_index("x"), lax.axis_index("y"), lax.axis_index("c")


def allgather8(v):
    m_per, n = v.shape

    def body(x_ref, out_ref, send_sems, recv_sems, local_sem):
        x, y, c = _me()
        me, sibling = (x, y, c), (x, y, 1 - c)
        chips = [(1 - x, y), (x, 1 - y), (1 - x, 1 - y)]

        def rows(px, py, pc):
            return out_ref.at[pl.ds((4 * px + 2 * py + pc) * m_per, m_per), :]

        def copy(k, block, to, src=None):
            return pltpu.make_async_remote_copy(
                src_ref=rows(*block) if src is None else src, dst_ref=rows(*block),
                send_sem=send_sems.at[k], recv_sem=recv_sems.at[k], device_id=to, device_id_type=MESH)

        mine = pltpu.make_async_copy(x_ref, rows(*me), local_sem)
        mine.start()
        first = [copy(0, me, sibling, src=x_ref)]
        first += [copy(1 + j, me, (*chip, c), src=x_ref) for j, chip in enumerate(chips)]
        for cp in first:
            cp.start()
        passed = [copy(4 + j, (*chip, c), sibling) for j, chip in enumerate(chips)]
        for j, chip in enumerate(chips):
            copy(1 + j, (*chip, c), me).wait_recv()
            passed[j].start()
        copy(0, sibling, me).wait_recv()
        for j, chip in enumerate(chips):
            copy(4 + j, (*chip, 1 - c), me).wait_recv()
        for cp in first + passed:
            cp.wait_send()
        mine.wait()

    return pl.pallas_call(
        body, out_shape=jax.ShapeDtypeStruct((N_DEV * m_per, n), v.dtype), in_specs=[VMEM_SPEC], out_specs=VMEM_SPEC,
        scratch_shapes=[pltpu.SemaphoreType.DMA((7,)), pltpu.SemaphoreType.DMA((7,)), pltpu.SemaphoreType.DMA],
        name=f"allgather8_{m_per}x{n}", compiler_params=_params(),
    )(v)


def _row_chunks(rows, tile_rows, want):
    n = want
    while n > 1 and rows % (n * tile_rows):
        n //= 2
    return [(i * (rows // n), rows // n) for i in range(n)]


def _remote(src, dst, send_sem, recv_sem, to):
    return pltpu.make_async_remote_copy(src_ref=src, dst_ref=dst, send_sem=send_sem, recv_sem=recv_sem, device_id=to,
                                        device_id_type=MESH)


def plane_allgather(big, small):
    rows = big.shape[0]
    rh = rows // 2
    rq = rh // 2
    tile = 16 if big.dtype == BF16 else 8
    assert rq % tile == 0
    ch_full = _row_chunks(rows, tile, 8)
    ch_half = _row_chunks(rh, tile, 4)

    def body(big_ref, small_ref, obig_ref, osmall_ref, send_sems, recv_sems, relay_send, relay_recv, fwd_send, fwd_recv,
             own_send, own_recv):
        x, y, c = _me()
        me = 2 * x + y
        sibling = (x, y, 1 - c)
        nbr_x, nbr_y, diag = (1 - x, y, c), (x, 1 - y, c), (1 - x, 1 - y, c)
        xi, yi, di = 2 * (1 - x) + y, 2 * x + (1 - y), 2 * (1 - x) + (1 - y)
        base, obase = c * rh, (1 - c) * rh
        mine, other = pl.ds(base, rh), pl.ds(obase, rh)
        qa, qb = pl.ds(base, rq), pl.ds(base + rq, rq)
        for st, sz in ch_full:
            sl = pl.ds(st, sz)
            _remote(big_ref.at[sl], obig_ref.at[me, sl], own_send.at[0], own_recv.at[0], sibling).start()
        _remote(small_ref, osmall_ref.at[me], own_send.at[1], own_recv.at[1], sibling).start()
        for j, peer in enumerate((nbr_x, nbr_y)):
            for st, sz in ch_half:
                sl = pl.ds(base + st, sz)
                _remote(big_ref.at[sl], obig_ref.at[me, sl], send_sems.at[j], recv_sems.at[j], peer).start()
        for j, peer in enumerate((nbr_x, nbr_y, diag)):
            _remote(small_ref, osmall_ref.at[me], send_sems.at[3 + j], recv_sems.at[3 + j], peer).start()

        def pass_on(k, slot, sl):
            _remote(obig_ref.at[slot, sl], obig_ref.at[slot, sl], fwd_send.at[k], fwd_recv.at[k], sibling).start()

        _remote(big_ref.at[mine], obig_ref.at[xi, mine], send_sems.at[0], recv_sems.at[0], nbr_x).wait_recv()
        _remote(obig_ref.at[xi, qa], obig_ref.at[xi, qa], relay_send.at[0], relay_recv.at[0], nbr_y).start()
        pass_on(0, xi, mine)
        _remote(big_ref.at[mine], obig_ref.at[yi, mine], send_sems.at[1], recv_sems.at[1], nbr_y).wait_recv()
        _remote(obig_ref.at[yi, qb], obig_ref.at[yi, qb], relay_send.at[1], relay_recv.at[1], nbr_x).start()
        pass_on(1, yi, mine)
        _remote(obig_ref.at[di, qa], obig_ref.at[di, qa], relay_send.at[0], relay_recv.at[0], nbr_y).wait_recv()
        pass_on(2, di, qa)
        _remote(obig_ref.at[di, qb], obig_ref.at[di, qb], relay_send.at[1], relay_recv.at[1], nbr_x).wait_recv()
        pass_on(3, di, qb)
        for j, (peer, slot) in enumerate(((nbr_x, xi), (nbr_y, yi), (diag, di))):
            _remote(small_ref, osmall_ref.at[slot], send_sems.at[3 + j], recv_sems.at[3 + j], peer).wait_recv()
        oqa, oqb = pl.ds(obase, rq), pl.ds(obase + rq, rq)
        for k, (slot, sl) in enumerate(((xi, other), (yi, other), (di, oqa), (di, oqb))):
            _remote(obig_ref.at[slot, sl], obig_ref.at[slot, sl], fwd_send.at[k], fwd_recv.at[k], sibling).wait_recv()
        for k, (slot, sl) in enumerate(((xi, mine), (yi, mine), (di, qa), (di, qb))):
            _remote(obig_ref.at[slot, sl], obig_ref.at[slot, sl], fwd_send.at[k], fwd_recv.at[k], sibling).wait_send()
        for j, peer in enumerate((nbr_x, nbr_y)):
            _remote(big_ref.at[mine], obig_ref.at[me, mine], send_sems.at[j], recv_sems.at[j], peer).wait_send()
        for j, peer in enumerate((nbr_x, nbr_y, diag)):
            _remote(small_ref, osmall_ref.at[me], send_sems.at[3 + j], recv_sems.at[3 + j], peer).wait_send()
        _remote(obig_ref.at[xi, qa], obig_ref.at[xi, qa], relay_send.at[0], relay_recv.at[0], nbr_y).wait_send()
        _remote(obig_ref.at[yi, qb], obig_ref.at[yi, qb], relay_send.at[1], relay_recv.at[1], nbr_x).wait_send()
        _remote(big_ref, obig_ref.at[me], own_send.at[0], own_recv.at[0], sibling).wait()
        _remote(small_ref, osmall_ref.at[me], own_send.at[1], own_recv.at[1], sibling).wait()

    dma = pltpu.SemaphoreType.DMA
    return pl.pallas_call(
        body, out_shape=[jax.ShapeDtypeStruct((N_PLANE,) + big.shape, big.dtype),
                         jax.ShapeDtypeStruct((N_PLANE,) + small.shape, small.dtype)],
        in_specs=[ANY, ANY], out_specs=[ANY, ANY],
        scratch_shapes=[dma((6,)), dma((6,)), dma((2,)), dma((2,)), dma((4,)), dma((4,)), dma((2,)), dma((2,))],
        name="plane_allgather", compiler_params=_params(),
    )(big, small)


def plane_scatter(p):
    tile = 16 if p.dtype == BF16 else 8
    chunks = _row_chunks(p.shape[1], tile, 4)

    def body(p_ref, out_ref, send_sems, recv_sems):
        x, y, c = _me()
        peers = [((x + fx) & 1, (y + fy) & 1) for fx, fy in _FLIPS]
        for j, (px, py) in enumerate(peers):
            for st, sz in chunks:
                sl = pl.ds(st, sz)
                _remote(p_ref.at[2 * px + py, sl], out_ref.at[j, sl], send_sems.at[j], recv_sems.at[j], (px, py, c)).start()
        for j, (px, py) in enumerate(peers):
            _remote(p_ref.at[0], out_ref.at[j], send_sems.at[j], recv_sems.at[j], (px, py, c)).wait_recv()
        for j, (px, py) in enumerate(peers):
            _remote(p_ref.at[0], out_ref.at[j], send_sems.at[j], recv_sems.at[j], (px, py, c)).wait_send()

    return pl.pallas_call(
        body, out_shape=jax.ShapeDtypeStruct((len(_FLIPS),) + p.shape[1:], p.dtype), in_specs=[ANY], out_specs=ANY,
        scratch_shapes=[pltpu.SemaphoreType.DMA((3,)), pltpu.SemaphoreType.DMA((3,))],
        name="plane_scatter", compiler_params=_params(),
    )(p)


def sibling_halves(buf):
    n_blk, _, rows, cols = buf.shape
    tile = 16 if buf.dtype == BF16 else 8
    chunks = _row_chunks(rows, tile, 2)

    def body(buf_ref, got_ref, send_sem, recv_sem):
        x, y, c = _me()
        for j in range(n_blk):
            for st, sz in chunks:
                sl = pl.ds(st, sz)
                _remote(buf_ref.at[j, 1 - c, sl], got_ref.at[j, sl], send_sem, recv_sem, (x, y, 1 - c)).start()
        _remote(got_ref, got_ref, send_sem, recv_sem, (x, y, 1 - c)).wait()

    return pl.pallas_call(
        body, out_shape=jax.ShapeDtypeStruct((n_blk, rows, cols), buf.dtype), in_specs=[ANY], out_specs=ANY,
        scratch_shapes=[pltpu.SemaphoreType.DMA, pltpu.SemaphoreType.DMA],
        name="sibling_halves", compiler_params=_params(),
    )(buf)


def sibling_swap(s):
    tile = 16 if s.dtype == BF16 else 8
    chunks = _row_chunks(s.shape[0], tile, 8)

    def body(s_ref, got_ref, send_sem, recv_sem):
        x, y, c = _me()
        for st, sz in chunks:
            sl = pl.ds(st, sz)
            _remote(s_ref.at[sl], got_ref.at[sl], send_sem, recv_sem, (x, y, 1 - c)).start()
        _remote(s_ref, got_ref, send_sem, recv_sem, (x, y, 1 - c)).wait()

    return pl.pallas_call(
        body, out_shape=jax.ShapeDtypeStruct(s.shape, s.dtype), in_specs=[ANY], out_specs=ANY,
        scratch_shapes=[pltpu.SemaphoreType.DMA, pltpu.SemaphoreType.DMA],
        name="sibling_swap", compiler_params=_params(),
    )(s)


def _op(cache, fn, name, kinds, out_dims, **kw):
    key = (name, tuple(out_dims), tuple(sorted(kw.items())))
    if key not in cache:
        cache[key] = make_rowwise(fn, name, kinds, out_dims, **kw)
    return cache[key]


def _even_mixer(ops, a, w, n_ctx):
    b, s, d = a.shape
    proj = linear(a.reshape(b * s, d), w["e_w_in"]).reshape(b, s, -1)
    q, k, v, u = jnp.split(proj, [GQA_Q_W, GQA_Q_W + GQA_KV_W, GQA_Q_W + 2 * GQA_KV_W], axis=-1)
    cos, sin = _rope_tables(n_ctx, s - n_ctx, HEAD_DIM, 0, HEAD_DIM)
    shift = HEAD_DIM // 4
    qn = head_norm_rope(q, w["e_g_q"][None], jnp.tile(cos, (1, GQA_Q_HEADS)), jnp.tile(sin, (1, GQA_Q_HEADS)),
                        jnp.asarray(_rope_signs(HEAD_DIM, 0, HEAD_DIM, GQA_Q_HEADS)), GQA_Q_HEADS, shift)
    kn = head_norm_rope(k, w["e_g_k"][None], jnp.tile(cos, (1, GQA_KV_HEADS)), jnp.tile(sin, (1, GQA_KV_HEADS)),
                        jnp.asarray(_rope_signs(HEAD_DIM, 0, HEAD_DIM, GQA_KV_HEADS)), GQA_KV_HEADS, shift)
    att = gqa_attention(qn, kn, v, n_ctx)
    ys = []
    for dr in range(2):
        a_re, a_im, bb_re, bb_im = _s5_discretize(w["ssm_lam_re"][dr], w["ssm_lam_im"][dr], w["ssm_log_dt"][dr],
                                                  w["ssm_b_re"][dr], w["ssm_b_im"][dr])
        a_flat = jnp.stack([a_re.reshape(-1), a_im.reshape(-1)])
        b_bd = jnp.stack([_block_diag(jnp.swapaxes(bb_re, 1, 2)), _block_diag(jnp.swapaxes(bb_im, 1, 2))])
        c_bd = jnp.stack([_block_diag(jnp.swapaxes(w["ssm_c_re"][dr], 1, 2)),
                          _block_diag(jnp.swapaxes(w["ssm_c_im"][dr], 1, 2))])
        ys.append(s5_direction(u, a_flat, b_bd, c_bd, n_ctx, dr == 1))
    pre = _op(ops, _fn_glu_pre, "glu_pre", ("row", "row", "row", "glob"), (SSM_WIDTH,))
    post = _op(ops, _fn_glu_post, "glu_post", ("row", "row", "glob"), (SSM_WIDTH,))
    z = pre(u, ys[0], ys[1], w["ssm_d"][None])[0]
    t = linear(z.reshape(b * s, SSM_WIDTH), w["ssm_w_glu"]).reshape(b, s, SSM_WIDTH)
    ssm = post(z, t, w["ssm_b_glu"][None])[0]
    mix = jnp.concatenate([att, ssm], axis=-1)
    return linear(mix.reshape(b * s, -1), w["e_w_out"]).reshape(b, s, d)


def _odd_mixer(ops, a, w, n_ctx):
    b, s, d = a.shape
    w_in = jnp.pad(w["o_w_in"], ((0, 0), (0, ODD_IN_PAD - ODD_IN_W)))
    proj = linear(a.reshape(b * s, d), w_in).reshape(b, s, -1)
    c1 = MLA_Q_RANK
    c2 = c1 + MLA_KV_RANK
    c3 = c2 + MLA_ROPE
    cq, ckv, kr, nq, nk, nv, _ = jnp.split(proj, [c1, c2, c3, c3 + NA_W, c3 + 2 * NA_W, ODD_IN_W], axis=-1)
    nrm = lambda wd: _op(ops, _fn_norm, f"norm{wd}", ("row", "glob"), (wd,))
    cqn = nrm(MLA_Q_RANK)(cq, w["mla_g_cq"][None])[0]
    ckvn = nrm(MLA_KV_RANK)(ckv, w["mla_g_ckv"][None])[0]
    q = linear(cqn.reshape(b * s, -1), w["mla_w_uq"]).reshape(b, s, -1)
    kv = linear(ckvn.reshape(b * s, -1), w["mla_w_ukv"]).reshape(b, s, -1)
    cos, sin = _rope_tables(n_ctx, s - n_ctx, MLA_QK, MLA_NOPE, MLA_ROPE)
    cos, sin = jnp.tile(cos, (1, MLA_HEADS)), jnp.tile(sin, (1, MLA_HEADS))
    sg = jnp.asarray(_rope_signs(MLA_QK, MLA_NOPE, MLA_ROPE, MLA_HEADS))
    mq = head_norm_rope(q, w["mla_g_q"][None], cos, sin, sg, MLA_HEADS, MLA_ROPE // 4)
    mk = mla_k_prep(kv, kr, w["mla_g_k"][None], cos, sin, sg)
    mla = mla_attention(mq, mk, kv, n_ctx)
    nqn = head_norm_rope(nq, w["na_g_q"][None], None, None, None, NA_HEADS, 0)
    nkn = head_norm_rope(nk, w["na_g_k"][None], None, None, None, NA_HEADS, 0)
    na = na_attention_tm(nqn, nkn, nv, na_bias_table(w["na_rpb"]), n_ctx)
    mix = jnp.concatenate([mla, na], axis=-1)
    return linear(mix.reshape(b * s, -1), w["o_w_out"]).reshape(b, s, d)


_EVEN_KEYS = ("e_w_in", "e_w_out", "e_g_q", "e_g_k", "ssm_lam_re", "ssm_lam_im", "ssm_log_dt", "ssm_b_re", "ssm_b_im",
              "ssm_c_re", "ssm_c_im", "ssm_d", "ssm_w_glu", "ssm_b_glu")
_ODD_KEYS = ("o_w_in", "o_w_out", "mla_g_cq", "mla_g_ckv", "mla_w_uq", "mla_w_ukv", "mla_g_q", "mla_g_k", "na_g_q",
             "na_g_k", "na_rpb")


def _trunk(x_all, mods, w, n_ctx):
    ops = {}
    depth = mods.shape[0]
    b, s, d = x_all.shape
    modulate = make_modulate(d, n_ctx)
    gated = make_gated_add(d, n_ctx)
    x = x_all
    for i in range(depth):
        j = i // 2
        m = [mods[i][:, :, r:r + 1, :] for r in range(N_MOD)]
        a, x = modulate(x, w["g_norm1"][i][None], m[0], m[1])
        if i % 2 == 0:
            o = _even_mixer(ops, a, {k: w[k][j] for k in _EVEN_KEYS}, n_ctx)
        else:
            o = _odd_mixer(ops, a, {k: w[k][j] for k in _ODD_KEYS}, n_ctx)
        x = gated(x, o, m[2])
        a2, x = modulate(x, w["g_norm2"][i][None], m[3], m[4])
        f = ffn(a2.reshape(b * s, d), w["w_ff1"][i], w["w_ff2"][i]).reshape(b, s, d)
        x = gated(x, f, m[5])
    return x[:, n_ctx:]


def local_step(x, ctx, mods, w, loss_target):
    n_ctx = ctx.shape[1]
    x_all = jnp.concatenate([ctx, x], axis=1)
    y, vjp = jax.vjp(lambda xa, md, ww: _trunk(xa, md, ww, n_ctx), x_all, mods, w)
    loss_tile, dy = _loss_head(y, loss_target)
    dx_all, dmods, dw = vjp(dy)
    return loss_tile[0, 0], dx_all[:, n_ctx:], dmods, dw


_SHARDED = (("w_ff1", 2), ("w_ff2", 1), ("e_w_in", 2), ("e_w_out", 1), ("o_w_in", 2), ("o_w_out", 1),
            ("mla_w_uq", 2), ("mla_w_ukv", 2), ("ssm_w_glu", 1))
_SHARDED_SMALL = (("mla_g_cq", 1), ("mla_g_ckv", 1))
_REPLICATED = ("g_norm1", "g_norm2", "e_g_q", "e_g_k", "ssm_lam_re", "ssm_lam_im", "ssm_log_dt", "ssm_b_re", "ssm_b_im",
               "ssm_c_re", "ssm_c_im", "ssm_d", "ssm_b_glu", "mla_g_q", "mla_g_k", "na_g_q", "na_g_k", "na_rpb")
_WEIGHTS = ("c_ctx", "w_mod", "b_mod", "g_norm1", "g_norm2", "w_ff1", "w_ff2", "e_w_in", "e_w_out", "e_g_q", "e_g_k",
            "ssm_lam_re", "ssm_lam_im", "ssm_log_dt", "ssm_b_re", "ssm_b_im", "ssm_c_re", "ssm_c_im", "ssm_d",
            "ssm_w_glu", "ssm_b_glu", "o_w_in", "o_w_out", "mla_g_cq", "mla_g_ckv", "mla_w_uq", "mla_w_ukv", "mla_g_q",
            "mla_g_k", "na_g_q", "na_g_k", "na_rpb")
_PACK_ROWS = 64


def _pack(arrs, dtype, cols=1024, row_mult=_PACK_ROWS):
    blocks, tail, off = [], [], 0
    for a in arrs:
        n = int(np.prod(a.shape))
        if not tail and off % cols == 0 and n % cols == 0:
            blocks.append(a.astype(dtype).reshape(-1, cols))
        else:
            tail.append(a.astype(dtype).reshape(-1))
        off += n
    rows = -(-off // cols)
    pad = (-rows) % row_mult * cols + rows * cols - off
    if tail or pad:
        blocks.append(jnp.concatenate(tail + [jnp.zeros((pad,), dtype)]).reshape(-1, cols))
    return jnp.concatenate(blocks, axis=0)


def _unpack(packed, shapes):
    cols = packed.shape[-1]
    packed = packed.reshape(-1, cols)
    out, off = [], 0
    for sh in shapes:
        n = int(np.prod(sh))
        if off % cols == 0 and n % cols == 0:
            out.append(packed[off // cols:(off + n) // cols].reshape(sh))
        else:
            r0, r1 = off // cols, -(-(off + n) // cols)
            out.append(packed[r0:r1].reshape(-1)[off - r0 * cols:off - r0 * cols + n].reshape(sh))
        off += n
    return out


def kernel(x, c, ctx, c_ctx, w_mod, b_mod, g_norm1, g_norm2, w_ff1, w_ff2, e_w_in, e_w_out, e_g_q, e_g_k, ssm_lam_re, ssm_lam_im, ssm_log_dt, ssm_b_re, ssm_b_im, ssm_c_re, ssm_c_im, ssm_d, ssm_w_glu, ssm_b_glu, o_w_in, o_w_out, mla_g_cq, mla_g_ckv, mla_w_uq, mla_w_ukv, mla_g_q, mla_g_k, na_g_q, na_g_k, na_rpb, loss_target, m_c_ctx, m_w_mod, m_b_mod, m_g_norm1, m_g_norm2, m_w_ff1, m_w_ff2, m_e_w_in, m_e_w_out, m_e_g_q, m_e_g_k, m_ssm_lam_re, m_ssm_lam_im, m_ssm_log_dt, m_ssm_b_re, m_ssm_b_im, m_ssm_c_re, m_ssm_c_im, m_ssm_d, m_ssm_w_glu, m_ssm_b_glu, m_o_w_in, m_o_w_out, m_mla_g_cq, m_mla_g_ckv, m_mla_w_uq, m_mla_w_ukv, m_mla_g_q, m_mla_g_k, m_na_g_q, m_na_g_k, m_na_rpb, v_c_ctx, v_w_mod, v_b_mod, v_g_norm1, v_g_norm2, v_w_ff1, v_w_ff2, v_e_w_in, v_e_w_out, v_e_g_q, v_e_g_k, v_ssm_lam_re, v_ssm_lam_im, v_ssm_log_dt, v_ssm_b_re, v_ssm_b_im, v_ssm_c_re, v_ssm_c_im, v_ssm_d, v_ssm_w_glu, v_ssm_b_glu, v_o_w_in, v_o_w_out, v_mla_g_cq, v_mla_g_ckv, v_mla_w_uq, v_mla_w_ukv, v_mla_g_q, v_mla_g_k, v_na_g_q, v_na_g_k, v_na_rpb):
    env = dict(locals())
    weights = {n: env[n] for n in _WEIGHTS}
    mom_m = {n: env["m_" + n] for n in _WEIGHTS}
    mom_v = {n: env["v_" + n] for n in _WEIGHTS}
    ax, ay, ac = _me()
    plane = 2 * ax + ay
    dev = 4 * ax + 2 * ay + ac
    b_loc, d = c.shape
    depth = w_mod.shape[0]
    n_all = N_DEV * b_loc
    mod_cols = w_mod.shape[2]

    big = _pack([weights[n] for n, _ in _SHARDED], BF16)
    small = _pack([weights[n] for n, _ in _SHARDED_SMALL], F32, cols=LANE, row_mult=8)
    g_big, g_small = plane_allgather(big, small)
    full = {n: weights[n] for n in _REPLICATED}
    parts = [_unpack(g_big[j], [weights[n].shape for n, _ in _SHARDED]) for j in range(N_PLANE)]
    for t, (n, axis) in enumerate(_SHARDED):
        full[n] = [jnp.concatenate([parts[j][t][l] for j in range(N_PLANE)], axis=axis - 1)
                   for l in range(weights[n].shape[0])]
    parts_s = [_unpack(g_small[j], [weights[n].shape for n, _ in _SHARDED_SMALL]) for j in range(N_PLANE)]
    for t, (n, axis) in enumerate(_SHARDED_SMALL):
        full[n] = jnp.concatenate([parts_s[j][t] for j in range(N_PLANE)], axis=axis)

    rows_pad = 8 * ((n_all + 1 + 7) // 8)
    c_all = allgather8(jnp.pad(c, ((0, 8 - b_loc), (0, 0)))).reshape(N_DEV, 8, d)[:, :b_loc].reshape(n_all, d)
    cond_raw = jnp.concatenate([c_all, c_ctx[None], jnp.zeros((rows_pad - n_all - 1, d), F32)], axis=0)
    b_cols = lax.dynamic_slice_in_dim(b_mod, plane * mod_cols, mod_cols, axis=1)
    mod_loc = jnp.stack([_mm(cond_raw, w_mod[i], a_act="silu") + b_cols[i][None] for i in range(depth)])
    mod_g = allgather8(mod_loc.reshape(depth * rows_pad, mod_cols)).reshape(N_PLANE, 2, depth, rows_pad, mod_cols)
    mod_all = jnp.concatenate([mod_g[j, 0] for j in range(N_PLANE)], axis=-1)
    m_lat = lax.dynamic_slice_in_dim(mod_all, dev * b_loc, b_loc, axis=1)
    m_ctx = jnp.broadcast_to(mod_all[:, n_all][:, None], m_lat.shape)
    mods = jnp.stack([m_ctx, m_lat], axis=2).reshape(depth, b_loc, 2, N_MOD, d)

    loss_part, grad_x, dmods, dw = local_step(x, ctx, mods, full, loss_target)

    dm = dmods.reshape(depth, b_loc, 2, N_MOD * d)
    dm_rows = jnp.concatenate([dm[:, :, 1], jnp.sum(dm[:, :, 0], axis=1, keepdims=True)], axis=1)
    rep_shapes = [weights[n].shape for n in _REPLICATED] + [(1,)]
    small_pack = _pack([dm_rows] + [dw[n] for n in _REPLICATED] + [loss_part.reshape(1)], F32, cols=1024, row_mult=8)
    sp_rows = small_pack.shape[0]
    gathered = allgather8(small_pack).reshape(N_DEV, sp_rows, 1024)
    n_dm = depth * (b_loc + 1) * N_MOD * d
    dm_all = gathered.reshape(N_DEV, -1)[:, :n_dm].reshape(N_DEV, depth, b_loc + 1, N_MOD * d)
    rep_sum = _sum_rows(gathered, N_DEV).reshape(-1)
    rep_parts = _unpack(rep_sum[n_dm:], rep_shapes)
    rep_grads = dict(zip(_REPLICATED, rep_parts[:-1]))
    loss = rep_parts[-1][0]
    d_ctx_row = rep_sum[:n_dm].reshape(depth, b_loc + 1, N_MOD * d)[:, b_loc]
    d_lat_rows = jnp.transpose(dm_all[:, :, :b_loc], (1, 0, 2, 3)).reshape(depth, n_all, N_MOD * d)
    d_mod_all = jnp.concatenate([d_lat_rows, d_ctx_row[:, None],
                                 jnp.zeros((depth, rows_pad - n_all - 1, N_MOD * d), F32)], axis=1)
    grads = dict(rep_grads)
    grads["b_mod"] = jnp.sum(d_mod_all, axis=1)
    d_cols = lax.dynamic_slice_in_dim(d_mod_all, plane * mod_cols, mod_cols, axis=2)
    grads["w_mod"] = jnp.stack([_mm(cond_raw, d_cols[i], ta=True, a_act="silu") for i in range(depth)])
    d_cond = _mm(d_cols[0], w_mod[0], tb=True)
    for i in range(1, depth):
        d_cond = _add2(d_cond, _mm(d_cols[i], w_mod[i], tb=True))
    d_cond_g = allgather8(d_cond[n_all:n_all + 8] if rows_pad - n_all >= 8 else
                          jnp.pad(d_cond[n_all:], ((0, 8 - (rows_pad - n_all)), (0, 0)))).reshape(N_PLANE, 2, 8, d)
    d_silu = _sum_rows(d_cond_g[:, 0], N_PLANE)[0]
    sg = jax.nn.sigmoid(c_ctx)
    grads["c_ctx"] = d_silu * (sg * (1.0 + c_ctx * (1.0 - sg)))

    def shards_of(g, axis, j):
        layers = g if isinstance(g, (list, tuple)) else [g]
        ax = axis - 1 if isinstance(g, (list, tuple)) else axis
        n = layers[0].shape[ax] // N_PLANE
        return [lax.slice_in_dim(t, j * n, (j + 1) * n, axis=ax) for t in layers]

    send = jnp.stack([_pack([t for n, axis in _SHARDED + _SHARDED_SMALL for t in shards_of(dw[n], axis, j)], BF16)
                      for j in range(N_PLANE)])
    rows_h = send.shape[1] // 2
    send = send.reshape(N_PLANE, 2, rows_h, 1024)
    mine = lax.dynamic_index_in_dim(send, ac, 1, keepdims=False).reshape(N_PLANE * rows_h, 1024)
    theirs = sibling_halves(send).reshape(N_PLANE * rows_h, 1024)
    chip_sum = _accumulate([mine, theirs], BF16).reshape(N_PLANE, rows_h, 1024)
    own = lax.dynamic_index_in_dim(chip_sum, plane, 0, keepdims=False)
    done = _accumulate([own, plane_scatter(chip_sum)], BF16)
    both = jnp.stack([done, sibling_swap(done)])
    flat = jnp.where(ac == 0, both, both[::-1]).astype(F32).reshape(-1, 1024)
    shard_shapes = [weights[n].shape for n, _ in _SHARDED] + [weights[n].shape for n, _ in _SHARDED_SMALL]
    for (n, _), g in zip(_SHARDED + _SHARDED_SMALL, _unpack(flat, shard_shapes)):
        grads[n] = g

    big_names = ("w_mod",) + tuple(n for n, _ in _SHARDED)
    small_names = tuple(n for n in _WEIGHTS if n not in big_names)
    delta, new_m, new_v = {}, {}, {}
    for n in big_names:
        delta[n], new_m[n], new_v[n] = _adamw(weights[n], grads[n], mom_m[n], mom_v[n])
    sm_shapes = [weights[n].shape for n in small_names]
    packed = [_pack([src[n] for n in small_names], F32, cols=1024, row_mult=8)
              for src in (weights, grads, mom_m, mom_v)]
    for dst, res in zip((delta, new_m, new_v), _adamw(*packed)):
        dst.update(dict(zip(small_names, _unpack(res, sm_shapes))))

    return (loss, grad_x, *[grads[n] for n in _WEIGHTS], *[delta[n] for n in _WEIGHTS],
            *[new_m[n] for n in _WEIGHTS], *[new_v[n] for n in _WEIGHTS])
```

```python
import functools

import numpy as np
import jax
import jax.numpy as jnp
from jax import lax
from jax.experimental import pallas as pl
from jax.experimental.pallas import tpu as pltpu

F32 = jnp.float32
BF16 = jnp.bfloat16
HI = lax.Precision.HIGHEST
MESH = pl.DeviceIdType.MESH
ANY = pl.BlockSpec(memory_space=pl.ANY)
VMEM_SPEC = pl.BlockSpec(memory_space=pltpu.VMEM)

GRID_W = 64
HEAD_DIM = 64
ROPE_BASE = 10000.0
EPS = 1e-6
N_MOD = 6
GQA_Q_HEADS, GQA_KV_HEADS = 12, 4
GQA_Q_W, GQA_KV_W = GQA_Q_HEADS * HEAD_DIM, GQA_KV_HEADS * HEAD_DIM
SSM_WIDTH, SSM_GROUP, SSM_STATE = 256, 16, 64
SSM_GROUPS = SSM_WIDTH // SSM_GROUP
SSM_LANES = SSM_GROUPS * SSM_STATE
MLA_HEADS, MLA_Q_RANK, MLA_KV_RANK, MLA_NOPE, MLA_ROPE, MLA_V = 8, 512, 256, 64, 32, 64
MLA_QK = MLA_NOPE + MLA_ROPE
NA_HEADS, NA_WIN_R, NA_WIN_C = 8, 8, 16
NA_W = NA_HEADS * HEAD_DIM
NA_BAND = NA_WIN_R * GRID_W
ODD_IN_W = MLA_Q_RANK + MLA_KV_RANK + MLA_ROPE + 3 * NA_W
ODD_IN_PAD = 2560
ADAM_LR, ADAM_B1, ADAM_B2, ADAM_EPS, ADAM_WD, ADAM_STEP = 0.001, 0.9, 0.999, 1e-08, 0.01, 10
NEG = -1e30
VMEM_LIMIT = 56 * 1024 * 1024
LANE = 128
MM_TILE_M = (1152, 1024, 768, 512, 256, 128)
MM_TILE_N = (1280, 1024, 768, 512, 256, 128)
MM_TILE_K = (1152, 1024, 768, 512, 256, 128)
ROW_TILES = (576, 512, 384, 256, 128, 64)
N_PLANE = 4
N_DEV = 8


def _pick(n, cands):
    for c in cands:
        if n % c == 0:
            return c
    return n


def _params(**kw):
    return pltpu.CompilerParams(vmem_limit_bytes=VMEM_LIMIT, **kw)


def _mm(a, b, *, ta=False, tb=False, a_act=None, epi=None, e=None, exact=False, out_dtype=F32):
    m, kd = (a.shape[1], a.shape[0]) if ta else a.shape
    n = b.shape[0] if tb else b.shape[1]
    tm = _pick(m, MM_TILE_M)
    tn = _pick(n, MM_TILE_N)
    tk = _pick(kd, MM_TILE_K)
    nk = kd // tk
    dn = (((0 if ta else 1,), (1 if tb else 0,)), ((), ()))
    narrow = jnp.dtype(out_dtype) != jnp.dtype(F32)
    assert not (narrow and epi is not None)

    def body(*refs):
        if narrow:
            a_ref, b_ref, out_ref, o_ref = refs
        elif epi is None:
            a_ref, b_ref, o_ref = refs
        else:
            a_ref, b_ref, e_ref, o_ref = refs
        k = pl.program_id(2)
        av = a_ref[...]
        if a_act == "relu2":
            av = jnp.square(jnp.maximum(av, 0.0))
        elif a_act == "silu":
            av = av * jax.nn.sigmoid(av)
        bv = b_ref[...]
        if exact:
            p = lax.dot_general(av, bv, dn, precision=HI, preferred_element_type=F32)
        else:
            p = lax.dot_general(av.astype(BF16), bv.astype(BF16), dn, preferred_element_type=F32)

        @pl.when(k == 0)
        def _():
            o_ref[...] = p

        @pl.when(k > 0)
        def _():
            o_ref[...] += p

        if epi == "drelu2":
            @pl.when(k == nk - 1)
            def _():
                o_ref[...] = o_ref[...] * (2.0 * jnp.maximum(e_ref[...], 0.0))

        if narrow:
            @pl.when(k == nk - 1)
            def _():
                out_ref[...] = o_ref[...].astype(out_dtype)

    a_spec = pl.BlockSpec((tk, tm), lambda i, j, k: (k, i)) if ta else pl.BlockSpec((tm, tk), lambda i, j, k: (i, k))
    b_spec = pl.BlockSpec((tn, tk), lambda i, j, k: (j, k)) if tb else pl.BlockSpec((tk, tn), lambda i, j, k: (k, j))
    o_spec = pl.BlockSpec((tm, tn), lambda i, j, k: (i, j))
    ins, specs = [a, b], [a_spec, b_spec]
    if epi is not None:
        ins.append(e)
        specs.append(o_spec)
    name = f"mm_{m}x{kd}x{n}_{int(ta)}{int(tb)}_{a_act}_{epi}_{int(exact)}_{jnp.dtype(out_dtype).name}"
    return pl.pallas_call(
        body, out_shape=jax.ShapeDtypeStruct((m, n), out_dtype), grid=(m // tm, n // tn, nk),
        in_specs=specs, out_specs=o_spec, name=name, compiler_params=_params(),
        scratch_shapes=[pltpu.VMEM((tm, tn), F32)] if narrow else [],
    )(*ins)


@functools.partial(jax.custom_vjp, nondiff_argnums=(2,))
def _linear(a, w, exact):
    return _mm(a, w, exact=exact)


def _linear_fwd(a, w, exact):
    return _mm(a, w, exact=exact), (a, w)


def _linear_bwd(exact, res, g):
    a, w = res
    return _mm(g, w, tb=True, exact=exact), _mm(a, g, ta=True, exact=exact, out_dtype=w.dtype)


_linear.defvjp(_linear_fwd, _linear_bwd)


def linear(a, w, exact=False):
    return _linear(a, w, exact)


@jax.custom_vjp
def ffn(a, w1, w2):
    return _mm(_mm(a, w1), w2, a_act="relu2")


def _ffn_fwd(a, w1, w2):
    h1 = _mm(a, w1)
    return _mm(h1, w2, a_act="relu2"), (a, w1, w2, h1)


def _ffn_bwd(res, g):
    a, w1, w2, h1 = res
    dh1 = _mm(g, w2, tb=True, epi="drelu2", e=h1)
    dw2 = _mm(h1, g, ta=True, a_act="relu2", out_dtype=w2.dtype)
    return _mm(dh1, w1, tb=True), _mm(a, dh1, ta=True, out_dtype=w1.dtype), dw2


ffn.defvjp(_ffn_fwd, _ffn_bwd)


def make_rowwise(fn, name, kinds, out_dims, nctx_rows=0, whole_seq=False):
    n_in = len(kinds)
    n_out = len(out_dims)
    diff = [i for i, kd in enumerate(kinds) if kd in ("row", "glob", "seg")]
    seg_idx = [i for i, kd in enumerate(kinds) if kd == "seg"]

    def layout(args):
        row0 = args[kinds.index("row")]
        g, s = row0.shape[0], row0.shape[1]
        ts = s if whole_seq else _pick(s, ROW_TILES)
        return g, s, ts, 0

    def spec_of(kind, arr, ts, nctx):
        if kind == "row":
            return pl.BlockSpec((None, ts, arr.shape[2]), lambda g, i: (g, i, 0))
        if kind == "tab":
            return pl.BlockSpec((ts, arr.shape[1]), lambda g, i: (i, 0))
        if kind in ("const", "glob"):
            return pl.BlockSpec(arr.shape, lambda g, i: (0, 0))
        return pl.BlockSpec((None,) + arr.shape[1:], lambda g, i: (g, 0, 0, 0))

    def with_segments(ts):
        if not seg_idx:
            return fn

        def wrapped(*vals):
            rows = pl.program_id(1) * ts + lax.broadcasted_iota(jnp.int32, (ts, 1), 0)
            vals = list(vals)
            for idx in seg_idx:
                vals[idx] = jnp.where(rows < nctx_rows, vals[idx][0], vals[idx][1])
            return fn(*vals)

        return wrapped

    def fwd_call(*args):
        g, s, ts, nctx = layout(args)
        fn = with_segments(ts)

        def body(*refs):
            vals = [r[...] for r in refs[:n_in]]
            outs = fn(*vals)
            for o_ref, o in zip(refs[n_in:], outs):
                o_ref[...] = o

        return pl.pallas_call(
            body, out_shape=[jax.ShapeDtypeStruct((g, s, d), F32) for d in out_dims], grid=(g, s // ts),
            in_specs=[spec_of(kd, a, ts, nctx) for kd, a in zip(kinds, args)],
            out_specs=[pl.BlockSpec((None, ts, d), lambda g_, i: (g_, i, 0)) for d in out_dims],
            name=f"{name}_f_{g}x{s}", compiler_params=_params(),
        )(*args)

    def bwd_call(args, cts):
        g, s, ts, nctx = layout(args)
        fn = with_segments(ts)

        def body(*refs):
            in_refs, ct_refs, out_refs = refs[:n_in], refs[n_in:n_in + n_out], refs[n_in + n_out:]
            gi, i = pl.program_id(0), pl.program_id(1)
            vals = [r[...] for r in in_refs]

            def f(*dv):
                full = list(vals)
                for idx, v in zip(diff, dv):
                    full[idx] = v
                return tuple(fn(*full))

            _, vjp = jax.vjp(f, *[vals[idx] for idx in diff])
            grads = vjp(tuple(r[...] for r in ct_refs))
            for idx, o_ref, gr in zip(diff, out_refs, grads):
                if kinds[idx] == "row":
                    o_ref[...] = gr
                    continue
                if kinds[idx] == "glob":
                    first = jnp.logical_and(gi == 0, i == 0)
                else:
                    first = i == 0

                @pl.when(first)
                def _(o_ref=o_ref, gr=gr):
                    o_ref[...] = gr

                @pl.when(jnp.logical_not(first))
                def _(o_ref=o_ref, gr=gr):
                    o_ref[...] += gr

        in_specs = [spec_of(kd, a, ts, nctx) for kd, a in zip(kinds, args)]
        in_specs += [pl.BlockSpec((None, ts, d), lambda g_, i: (g_, i, 0)) for d in out_dims]
        return pl.pallas_call(
            body, out_shape=[jax.ShapeDtypeStruct(args[idx].shape, F32) for idx in diff], grid=(g, s // ts),
            in_specs=in_specs, out_specs=[spec_of(kinds[idx], args[idx], ts, nctx) for idx in diff],
            name=f"{name}_b_{g}x{s}", compiler_params=_params(),
        )(*args, *cts)

    @jax.custom_vjp
    def op(*args):
        return tuple(fwd_call(*args))

    def op_fwd(*args):
        return tuple(fwd_call(*args)), args

    def op_bwd(args, cts):
        grads = bwd_call(args, cts)
        full = [None] * n_in
        for idx, gr in zip(diff, grads):
            full[idx] = gr
        return tuple(jnp.zeros_like(a) if gfull is None else gfull for a, gfull in zip(args, full))

    op.defvjp(op_fwd, op_bwd)
    op.fwd_call, op.bwd_call = fwd_call, bwd_call
    return op


def make_modulate(d, n_ctx):
    one = make_rowwise(_fn_modulate, "modulate", ("row", "glob", "seg", "seg"), (d,), nctx_rows=n_ctx)
    two = make_rowwise(_fn_modulate_keep, "modulate_keep", ("row", "glob", "seg", "seg"), (d, d), nctx_rows=n_ctx)

    @jax.custom_vjp
    def op(x, g, shift, scale):
        return one.fwd_call(x, g, shift, scale)[0], x

    def fwd(x, g, shift, scale):
        return (one.fwd_call(x, g, shift, scale)[0], x), (x, g, shift, scale)

    def bwd(res, cts):
        return tuple(two.bwd_call(res, cts))

    op.defvjp(fwd, bwd)
    return op


def make_gated_add(d, n_ctx):
    add = make_rowwise(_fn_gated_add, "gated", ("row", "row", "seg"), (d,), nctx_rows=n_ctx)
    mul = make_rowwise(_fn_gate_mul, "gate_mul", ("row", "seg"), (d,), nctx_rows=n_ctx)

    @jax.custom_vjp
    def op(x, o, gate):
        return add.fwd_call(x, o, gate)[0]

    def fwd(x, o, gate):
        return add.fwd_call(x, o, gate)[0], (o, gate)

    def bwd(res, ct):
        do, dgate = mul.bwd_call(res, (ct,))
        return ct, do, dgate

    op.defvjp(fwd, bwd)
    return op


def _rms(x):
    return lax.rsqrt(jnp.mean(x * x, axis=-1, keepdims=True) + EPS)


def _fn_modulate(x, g, shift, scale):
    return ((x * _rms(x) * g) * (1.0 + scale) + shift,)


def _fn_modulate_keep(x, g, shift, scale):
    return _fn_modulate(x, g, shift, scale) + (x,)


def _fn_gated_add(x, o, gate):
    return (x + gate * o,)


def _fn_gate_mul(o, gate):
    return (gate * o,)


def _fn_norm(x, g):
    return (x * _rms(x) * g,)


def _fn_glu_pre(u, y0, y1, d):
    return (jax.nn.gelu(d * u + y0 + y1),)


def _fn_glu_post(z, t, bg):
    return (z * jax.nn.sigmoid(t + bg),)


def _rope_tables(n_ctx, n_lat, dh, start, rot_dim):
    t = jnp.arange(n_lat)
    rows = (t // GRID_W).astype(F32)
    cols = (t % GRID_W).astype(F32)
    axis_dim = rot_dim // 2
    freqs = ROPE_BASE ** (-jnp.arange(0, axis_dim, 2, dtype=F32) / axis_dim)
    ang_r = rows[:, None] * freqs
    ang_c = cols[:, None] * freqs
    ang = jnp.concatenate([ang_r, ang_r, ang_c, ang_c], axis=-1)
    cos = jnp.concatenate([jnp.ones((n_lat, start), F32), jnp.cos(ang)], axis=-1)
    sin = jnp.concatenate([jnp.zeros((n_lat, start), F32), jnp.sin(ang)], axis=-1)
    cos = jnp.concatenate([jnp.ones((n_ctx, dh), F32), cos], axis=0)
    sin = jnp.concatenate([jnp.zeros((n_ctx, dh), F32), sin], axis=0)
    return cos, sin


_NT = (((1,), (1,)), ((), ()))
_TN = (((0,), (0,)), ((), ()))


def _na_geometry(i, nc, rows):
    r = i - nc
    rs = jnp.clip(r - NA_WIN_R // 2, 0, rows - NA_WIN_R)
    is_ctx = i < nc
    cls = jnp.where(is_ctx, NA_WIN_R, r - rs)
    return jnp.where(is_ctx, 0, rs), cls


def _na_onehots():
    q = np.arange(GRID_W)[:, None]
    col = np.arange(GRID_W)[None, :]
    cs = np.clip(q - NA_WIN_C // 2, 0, GRID_W - NA_WIN_C)
    valid = (col >= cs) & (col < cs + NA_WIN_C)
    cidx = col - q + (NA_WIN_C - 1)
    n_b = 2 * NA_WIN_C - 1
    col_hot = np.zeros((LANE, GRID_W * GRID_W), np.float32)
    for qq in range(GRID_W):
        for cc in range(GRID_W):
            if valid[qq, cc]:
                col_hot[cidx[qq, cc], qq * GRID_W + cc] = 1.0
    row_hot = np.zeros((NA_WIN_R, NA_WIN_R, 2 * NA_WIN_R - 1), np.float32)
    for c in range(NA_WIN_R):
        for j in range(NA_WIN_R):
            row_hot[c, j, j - c + NA_WIN_R - 1] = 1.0
    mask = np.where(valid, 0.0, NEG).astype(np.float32)
    return col_hot, row_hot, mask, n_b


def na_bias_table(rpb):
    h = rpb.shape[0]
    col_hot, row_hot, mask, n_b = _na_onehots()
    t1 = jnp.einsum("cja,hab->hcjb", jnp.asarray(row_hot), rpb)
    t1 = jnp.pad(t1.reshape(h * NA_WIN_R * NA_WIN_R, n_b), ((0, 0), (0, LANE - n_b)))
    t2 = linear(t1, jnp.asarray(col_hot), True)
    t2 = t2.reshape(h, NA_WIN_R, NA_WIN_R, GRID_W, GRID_W) + jnp.asarray(mask)
    tab = jnp.transpose(t2, (0, 1, 3, 2, 4)).reshape(h, NA_WIN_R, GRID_W, NA_BAND)
    return jnp.concatenate([tab, jnp.full((h, 1, GRID_W, NA_BAND), NEG, F32)], axis=1)


def _first_step():
    return jnp.logical_and(pl.program_id(0) == 0, pl.program_id(1) == 0)


def _accum_out(ref, val, first):
    @pl.when(first)
    def _():
        ref[...] = val

    @pl.when(jnp.logical_not(first))
    def _():
        ref[...] += val


def _norm_head(xh, g):
    r = _rms(xh)
    yn = xh * r
    return yn * g, yn, r


def _norm_head_bwd(dy, yn, r, g):
    dg = jnp.sum(dy * yn, axis=0, keepdims=True)
    dyn = dy * g
    return r * (dyn - yn * jnp.mean(dyn * yn, axis=-1, keepdims=True)), dg


def _rope_signs(dh, start, rot_dim, n_heads):
    q = rot_dim // 4
    pos = np.arange(dh)
    quarter = (pos - start) // q
    inr = pos >= start
    sg = np.zeros((8, n_heads * dh), np.float32)
    sg[0] = np.tile(np.where(inr & (quarter % 2 == 0), -1.0, 0.0), n_heads)
    sg[1] = np.tile(np.where(inr & (quarter % 2 == 1), 1.0, 0.0), n_heads)
    return sg


def _rope_full(y, cos, sin, sg, q):
    w = y.shape[-1]
    rot = sg[0:1] * pltpu.roll(y, w - q, 1) + sg[1:2] * pltpu.roll(y, q, 1)
    return y * cos + rot * sin


def _rope_full_t(dy, cos, sin, sg, q):
    w = dy.shape[-1]
    z = dy * sin
    return dy * cos - sg[1:2] * pltpu.roll(z, q, 1) - sg[0:1] * pltpu.roll(z, w - q, 1)


def _hnr_call(x, g, cos, sin, sg, n_heads, q, dy=None):
    b, s, w = x.shape
    dh = w // n_heads
    ts = _pick(s, ROW_TILES)
    rope = cos is not None

    def body(*refs):
        refs = list(refs)
        x_ref, g_ref = refs[0], refs[1]
        k = 2
        if rope:
            cos_ref, sin_ref, sg_ref = refs[2], refs[3], refs[4]
            k = 5
        gv = g_ref[...]
        if dy is None:
            o_ref = refs[k]
            for h in range(n_heads):
                sl = slice(h * dh, (h + 1) * dh)
                o_ref[:, sl] = _norm_head(x_ref[:, sl], gv)[0]
            if rope:
                o_ref[...] = _rope_full(o_ref[...], cos_ref[...], sin_ref[...], sg_ref[...], q)
            return
        dy_ref, dx_ref, dg_ref = refs[k], refs[k + 1], refs[k + 2]
        src = dy_ref
        if rope:
            dx_ref[...] = _rope_full_t(dy_ref[...], cos_ref[...], sin_ref[...], sg_ref[...], q)
            src = dx_ref
        dg = jnp.zeros((1, dh), F32)
        for h in range(n_heads):
            sl = slice(h * dh, (h + 1) * dh)
            _, yn, r = _norm_head(x_ref[:, sl], gv)
            dxh, dgh = _norm_head_bwd(src[:, sl], yn, r, gv)
            dx_ref[:, sl] = dxh
            dg = dg + dgh
        _accum_out(dg_ref, dg, _first_step())

    row = pl.BlockSpec((None, ts, w), lambda bi, i: (bi, i, 0))
    whole = lambda a: pl.BlockSpec(a.shape, lambda bi, i: (0, 0))
    ins, specs = [x, g], [row, whole(g)]
    if rope:
        ins += [cos, sin, sg]
        specs += [pl.BlockSpec((ts, w), lambda bi, i: (i, 0)), pl.BlockSpec((ts, w), lambda bi, i: (i, 0)), whole(sg)]
    if dy is None:
        out_shape, out_specs = jax.ShapeDtypeStruct(x.shape, F32), row
    else:
        ins.append(dy)
        specs.append(row)
        out_shape = [jax.ShapeDtypeStruct(x.shape, F32), jax.ShapeDtypeStruct(g.shape, F32)]
        out_specs = [row, whole(g)]
    return pl.pallas_call(
        body, out_shape=out_shape, grid=(b, s // ts), in_specs=specs, out_specs=out_specs,
        name=f"hnr_{'b' if dy is not None else 'f'}_{n_heads}x{dh}_{int(rope)}", compiler_params=_params(),
    )(*ins)


@functools.partial(jax.custom_vjp, nondiff_argnums=(5, 6))
def head_norm_rope(x, g, cos, sin, sg, n_heads, q):
    return _hnr_call(x, g, cos, sin, sg, n_heads, q)


def _head_norm_rope_fwd(x, g, cos, sin, sg, n_heads, q):
    return _hnr_call(x, g, cos, sin, sg, n_heads, q), (x, g, cos, sin, sg)


def _head_norm_rope_bwd(n_heads, q, res, dy):
    x, g, cos, sin, sg = res
    dx, dg = _hnr_call(x, g, cos, sin, sg, n_heads, q, dy=dy)
    zero = lambda t: None if t is None else jnp.zeros_like(t)
    return dx, dg, zero(cos), zero(sin), zero(sg)


head_norm_rope.defvjp(_head_norm_rope_fwd, _head_norm_rope_bwd)


def _mla_k_call(kv, kr, g, cos, sin, sg, dkn=None):
    b, s, _ = kv.shape
    ts = _pick(s, ROW_TILES)
    hw = MLA_NOPE + MLA_V
    kn_w = MLA_HEADS * MLA_QK
    q = MLA_ROPE // 4

    def body(kv_ref, kr_ref, g_ref, cos_ref, sin_ref, sg_ref, *rest):
        gv = g_ref[...]
        krv = kr_ref[...]
        if dkn is None:
            (o_ref,) = rest
            for h in range(MLA_HEADS):
                kh = jnp.concatenate([kv_ref[:, h * hw:h * hw + MLA_NOPE], krv], axis=-1)
                o_ref[:, h * MLA_QK:(h + 1) * MLA_QK] = _norm_head(kh, gv)[0]
            o_ref[...] = _rope_full(o_ref[...], cos_ref[...], sin_ref[...], sg_ref[...], q)
            return
        dkn_ref, dkv_ref, dkr_ref, dg_ref, dy_ref = rest
        dy_ref[...] = _rope_full_t(dkn_ref[...], cos_ref[...], sin_ref[...], sg_ref[...], q)
        dg = jnp.zeros((1, MLA_QK), F32)
        dkr = jnp.zeros((ts, MLA_ROPE), F32)
        for h in range(MLA_HEADS):
            kh = jnp.concatenate([kv_ref[:, h * hw:h * hw + MLA_NOPE], krv], axis=-1)
            _, yn, r = _norm_head(kh, gv)
            dxh, dgh = _norm_head_bwd(dy_ref[:, h * MLA_QK:(h + 1) * MLA_QK], yn, r, gv)
            dkv_ref[:, h * hw:h * hw + MLA_NOPE] = dxh[:, :MLA_NOPE]
            dkv_ref[:, h * hw + MLA_NOPE:(h + 1) * hw] = jnp.zeros((ts, MLA_V), F32)
            dkr = dkr + dxh[:, MLA_NOPE:]
            dg = dg + dgh
        dkr_ref[...] = dkr
        _accum_out(dg_ref, dg, _first_step())

    row = lambda w: pl.BlockSpec((None, ts, w), lambda bi, i: (bi, i, 0))
    tab = pl.BlockSpec((ts, kn_w), lambda bi, i: (i, 0))
    whole = lambda a: pl.BlockSpec(a.shape, lambda bi, i: (0, 0))
    ins = [kv, kr, g, cos, sin, sg]
    specs = [row(kv.shape[2]), row(MLA_ROPE), whole(g), tab, tab, whole(sg)]
    scratch = []
    if dkn is None:
        out_shape, out_specs = jax.ShapeDtypeStruct((b, s, kn_w), F32), row(kn_w)
    else:
        ins.append(dkn)
        specs.append(row(kn_w))
        out_shape = [jax.ShapeDtypeStruct(kv.shape, F32), jax.ShapeDtypeStruct(kr.shape, F32),
                     jax.ShapeDtypeStruct(g.shape, F32)]
        out_specs = [row(kv.shape[2]), row(MLA_ROPE), whole(g)]
        scratch = [pltpu.VMEM((ts, kn_w), F32)]
    return pl.pallas_call(
        body, out_shape=out_shape, grid=(b, s // ts), in_specs=specs, out_specs=out_specs, scratch_shapes=scratch,
        name=f"mla_k_{'b' if dkn is not None else 'f'}", compiler_params=_params(),
    )(*ins)


@jax.custom_vjp
def mla_k_prep(kv, kr, g, cos, sin, sg):
    return _mla_k_call(kv, kr, g, cos, sin, sg)


def _mla_k_prep_fwd(kv, kr, g, cos, sin, sg):
    return _mla_k_call(kv, kr, g, cos, sin, sg), (kv, kr, g, cos, sin, sg)


def _mla_k_prep_bwd(res, dkn):
    kv, kr, g, cos, sin, sg = res
    dkv, dkr, dg = _mla_k_call(kv, kr, g, cos, sin, sg, dkn=dkn)
    return dkv, dkr, dg, jnp.zeros_like(cos), jnp.zeros_like(sin), jnp.zeros_like(sg)


mla_k_prep.defvjp(_mla_k_prep_fwd, _mla_k_prep_bwd)


class _HeadLayout:
    def __init__(self, groups, dq, dv, q_off, k_off, v_off, o_off, wq, wk, wv, wo, scale):
        self.groups, self.dq, self.dv, self.scale = groups, dq, dv, scale
        self.q_off, self.k_off, self.v_off, self.o_off = q_off, k_off, v_off, o_off
        self.wq, self.wk, self.wv, self.wo = wq, wk, wv, wo
        self.n_h = len(q_off)


def _gqa_layout():
    rep = GQA_Q_HEADS // GQA_KV_HEADS
    n_h = GQA_Q_HEADS // 2
    return _HeadLayout(2, HEAD_DIM, HEAD_DIM, [h * HEAD_DIM for h in range(n_h)], [(h // rep) * HEAD_DIM for h in range(n_h)],
                       [(h // rep) * HEAD_DIM for h in range(n_h)], [h * HEAD_DIM for h in range(n_h)],
                       n_h * HEAD_DIM, (n_h // rep) * HEAD_DIM, (n_h // rep) * HEAD_DIM, n_h * HEAD_DIM, HEAD_DIM ** -0.5)


def _mla_layout():
    n_h = MLA_HEADS // 2
    hw = MLA_NOPE + MLA_V
    return _HeadLayout(2, MLA_QK, MLA_V, [h * MLA_QK for h in range(n_h)], [h * MLA_QK for h in range(n_h)],
                       [h * hw + MLA_NOPE for h in range(n_h)], [h * MLA_V for h in range(n_h)],
                       n_h * MLA_QK, n_h * MLA_QK, n_h * hw, n_h * MLA_V, MLA_QK ** -0.5)


def _attn_tm_fwd(q, k, v, lay, n_ctx):
    b, s, _ = q.shape
    tq = min(256, n_ctx)
    nc = n_ctx // tq

    def body(q_ref, k_ref, v_ref, o_ref, lse_ref):
        def run(n_keys):
            for h in range(lay.n_h):
                qo, ko, vo, oo = lay.q_off[h], lay.k_off[h], lay.v_off[h], lay.o_off[h]
                qv = (q_ref[:, qo:qo + lay.dq] * lay.scale).astype(BF16)
                sc = lax.dot_general(qv, k_ref[0:n_keys, ko:ko + lay.dq].astype(BF16), _NT, preferred_element_type=F32)
                m = jnp.max(sc, axis=-1, keepdims=True)
                p = jnp.exp(sc - m)
                l = jnp.sum(p, axis=-1, keepdims=True)
                o = jnp.dot(p.astype(BF16), v_ref[0:n_keys, vo:vo + lay.dv].astype(BF16), preferred_element_type=F32)
                o_ref[:, oo:oo + lay.dv] = o / l
                lse_ref[:, h:h + 1] = m + jnp.log(l)

        pl.when(pl.program_id(2) < nc)(lambda: run(n_ctx))
        pl.when(pl.program_id(2) >= nc)(lambda: run(s))

    return pl.pallas_call(
        body, out_shape=[jax.ShapeDtypeStruct((b, s, lay.groups * lay.wo), F32),
                         jax.ShapeDtypeStruct((b, lay.groups, s, lay.n_h), F32)],
        grid=(b, lay.groups, s // tq),
        in_specs=[pl.BlockSpec((None, tq, lay.wq), lambda bi, g, i: (bi, i, g)),
                  pl.BlockSpec((None, s, lay.wk), lambda bi, g, i: (bi, 0, g)),
                  pl.BlockSpec((None, s, lay.wv), lambda bi, g, i: (bi, 0, g))],
        out_specs=[pl.BlockSpec((None, tq, lay.wo), lambda bi, g, i: (bi, i, g)),
                   pl.BlockSpec((None, None, tq, lay.n_h), lambda bi, g, i: (bi, g, i, 0))],
        name=f"attn_tm_f_{lay.dq}", compiler_params=_params(),
    )(q, k, v)


def _attn_tm_bwd(q, k, v, lse, o, do, lay, n_ctx):
    b, s, _ = q.shape
    tk = min(256, n_ctx)
    nc = n_ctx // tk

    def body(q_ref, k_ref, v_ref, lse_ref, o_ref, do_ref, dq_ref, dk_ref, dv_ref, delta_ref):
        @pl.when(pl.program_id(2) == 0)
        def _():
            dq_ref[...] = jnp.zeros_like(dq_ref)
            for h in range(lay.n_h):
                oo = lay.o_off[h]
                delta_ref[:, h:h + 1] = jnp.sum(o_ref[:, oo:oo + lay.dv] * do_ref[:, oo:oo + lay.dv], axis=-1,
                                                keepdims=True)

        def run(r0):
            dk_acc, dv_acc = {}, {}
            for h in range(lay.n_h):
                qo, ko, vo, oo = lay.q_off[h], lay.k_off[h], lay.v_off[h], lay.o_off[h]
                kh = k_ref[:, ko:ko + lay.dq].astype(BF16)
                vh = v_ref[:, vo:vo + lay.dv].astype(BF16)
                qv = (q_ref[r0:s, qo:qo + lay.dq] * lay.scale).astype(BF16)
                dob = do_ref[r0:s, oo:oo + lay.dv].astype(BF16)
                sc = lax.dot_general(qv, kh, _NT, preferred_element_type=F32)
                p = jnp.exp(sc - lse_ref[r0:s, h:h + 1])
                dvh = lax.dot_general(p.astype(BF16), dob, _TN, preferred_element_type=F32)
                dp = lax.dot_general(dob, vh, _NT, preferred_element_type=F32)
                dsb = (p * (dp - delta_ref[r0:s, h:h + 1])).astype(BF16)
                dkh = lax.dot_general(dsb, qv, _TN, preferred_element_type=F32)
                dq_ref[r0:s, qo:qo + lay.dq] += jnp.dot(dsb, kh, preferred_element_type=F32) * lay.scale
                dk_acc[ko] = dkh if ko not in dk_acc else dk_acc[ko] + dkh
                dv_acc[vo] = dvh if vo not in dv_acc else dv_acc[vo] + dvh
            if len(dv_acc) * lay.dv != lay.wv:
                dv_ref[...] = jnp.zeros_like(dv_ref)
            for ko, val in dk_acc.items():
                dk_ref[:, ko:ko + lay.dq] = val
            for vo, val in dv_acc.items():
                dv_ref[:, vo:vo + lay.dv] = val

        pl.when(pl.program_id(2) < nc)(lambda: run(0))
        pl.when(pl.program_id(2) >= nc)(lambda: run(n_ctx))

    full = lambda w: pl.BlockSpec((None, s, w), lambda bi, g, j: (bi, 0, g))
    blk = lambda w: pl.BlockSpec((None, tk, w), lambda bi, g, j: (bi, j, g))
    stat = pl.BlockSpec((None, None, s, lay.n_h), lambda bi, g, j: (bi, g, 0, 0))
    return pl.pallas_call(
        body, out_shape=[jax.ShapeDtypeStruct(q.shape, F32), jax.ShapeDtypeStruct(k.shape, F32),
                         jax.ShapeDtypeStruct(v.shape, F32)],
        grid=(b, lay.groups, s // tk),
        in_specs=[full(lay.wq), blk(lay.wk), blk(lay.wv), stat, full(lay.wo), full(lay.wo)],
        out_specs=[full(lay.wq), blk(lay.wk), blk(lay.wv)],
        scratch_shapes=[pltpu.VMEM((s, lay.n_h), F32)],
        name=f"attn_tm_b_{lay.dq}", compiler_params=_params(),
    )(q, k, v, lse, o, do)


def _make_attention_tm(lay):
    @functools.partial(jax.custom_vjp, nondiff_argnums=(3,))
    def op(q, k, v, n_ctx):
        return _attn_tm_fwd(q, k, v, lay, n_ctx)[0]

    def fwd(q, k, v, n_ctx):
        o, lse = _attn_tm_fwd(q, k, v, lay, n_ctx)
        return o, (q, k, v, o, lse)

    def bwd(n_ctx, res, do):
        q, k, v, o, lse = res
        return _attn_tm_bwd(q, k, v, lse, o, do, lay, n_ctx)

    op.defvjp(fwd, bwd)
    return op


gqa_attention = _make_attention_tm(_gqa_layout())
mla_attention = _make_attention_tm(_mla_layout())

NA_GROUPS_FWD = 1
NA_GROUPS_BWD = 2


def _na_tm_specs(s, nc, rows, groups):
    hg = NA_HEADS // groups
    w = hg * HEAD_DIM
    qs = pl.BlockSpec((None, GRID_W, w), lambda bi, g, i: (bi, i, g))
    ks = pl.BlockSpec((None, s, w), lambda bi, g, i: (bi, 0, g))
    bs = pl.BlockSpec((hg, None, GRID_W, NA_BAND), lambda bi, g, i: (g, _na_geometry(i, nc, rows)[1], 0, 0))
    ls = pl.BlockSpec((None, None, GRID_W, hg), lambda bi, g, i: (bi, g, i, 0))
    return hg, w, qs, ks, bs, ls


def _na_tm_scores(q_ref, k_ref, bias_ref, hd, n_ctx, start, scale):
    sl = slice(hd * HEAD_DIM, (hd + 1) * HEAD_DIM)
    qv = (q_ref[:, sl] * scale).astype(BF16)
    kc = k_ref[0:n_ctx, sl].astype(BF16)
    kb = k_ref[pl.ds(start, NA_BAND), sl].astype(BF16)
    s_c = lax.dot_general(qv, kc, _NT, preferred_element_type=F32)
    s_l = lax.dot_general(qv, kb, _NT, preferred_element_type=F32) + bias_ref[hd]
    return sl, qv, kc, kb, s_c, s_l


def _na_tm_fwd(q, k, v, bias, n_ctx):
    b, s, _ = q.shape
    nc = n_ctx // GRID_W
    rows = (s - n_ctx) // GRID_W
    scale = HEAD_DIM ** -0.5
    hg, w, qs, ks, bs, ls = _na_tm_specs(s, nc, rows, NA_GROUPS_FWD)

    def body(q_ref, k_ref, v_ref, bias_ref, o_ref, lse_ref):
        rs, _ = _na_geometry(pl.program_id(2), nc, rows)
        start = pl.multiple_of(n_ctx + rs * GRID_W, GRID_W)
        for hd in range(hg):
            sl, _, _, _, s_c, s_l = _na_tm_scores(q_ref, k_ref, bias_ref, hd, n_ctx, start, scale)
            m = jnp.maximum(jnp.max(s_c, axis=-1, keepdims=True), jnp.max(s_l, axis=-1, keepdims=True))
            p_c = jnp.exp(s_c - m)
            p_l = jnp.exp(s_l - m)
            l = jnp.sum(p_c, axis=-1, keepdims=True) + jnp.sum(p_l, axis=-1, keepdims=True)
            o = jnp.dot(p_c.astype(BF16), v_ref[0:n_ctx, sl].astype(BF16), preferred_element_type=F32)
            o = o + jnp.dot(p_l.astype(BF16), v_ref[pl.ds(start, NA_BAND), sl].astype(BF16), preferred_element_type=F32)
            o_ref[:, sl] = o / l
            lse_ref[:, hd:hd + 1] = m + jnp.log(l)

    return pl.pallas_call(
        body, out_shape=[jax.ShapeDtypeStruct(q.shape, F32), jax.ShapeDtypeStruct((b, NA_GROUPS_FWD, s, hg), F32)],
        grid=(b, NA_GROUPS_FWD, s // GRID_W), in_specs=[qs, ks, ks, bs], out_specs=[qs, ls],
        name=f"na_tm_f_{s}", compiler_params=_params(),
    )(q, k, v, bias)


def _na_tm_bwd(q, k, v, bias, o, lse, do, n_ctx):
    b, s, _ = q.shape
    nc = n_ctx // GRID_W
    rows = (s - n_ctx) // GRID_W
    scale = HEAD_DIM ** -0.5
    n_cls = NA_WIN_R + 1
    hg, w, qs, ks, bs, ls = _na_tm_specs(s, nc, rows, NA_GROUPS_BWD)
    lse = jnp.transpose(lse, (0, 2, 1, 3)).reshape(b, s, NA_GROUPS_BWD, hg)
    lse = jnp.transpose(lse, (0, 2, 1, 3))

    def body(q_ref, k_ref, v_ref, bias_ref, o_ref, lse_ref, do_ref, dq_ref, dk_ref, dv_ref, db_ref):
        i = pl.program_id(2)
        rs, cls = _na_geometry(i, nc, rows)
        _, cls_prev = _na_geometry(i - 1, nc, rows)
        start = pl.multiple_of(n_ctx + rs * GRID_W, GRID_W)
        first = jnp.logical_or(i == 0, cls != cls_prev)

        @pl.when(i == 0)
        def _():
            dk_ref[...] = jnp.zeros_like(dk_ref)
            dv_ref[...] = jnp.zeros_like(dv_ref)

        @pl.when(first)
        def _():
            db_ref[...] = jnp.zeros_like(db_ref)

        for hd in range(hg):
            sl, qv, kc, kb, s_c, s_l = _na_tm_scores(q_ref, k_ref, bias_ref, hd, n_ctx, start, scale)
            lse_v = lse_ref[:, hd:hd + 1]
            p_c = jnp.exp(s_c - lse_v)
            p_l = jnp.exp(s_l - lse_v)
            dov = do_ref[:, sl]
            dob = dov.astype(BF16)
            delta = jnp.sum(dov * o_ref[:, sl], axis=-1, keepdims=True)
            vc = v_ref[0:n_ctx, sl].astype(BF16)
            vb = v_ref[pl.ds(start, NA_BAND), sl].astype(BF16)
            ds_c = p_c * (lax.dot_general(dob, vc, _NT, preferred_element_type=F32) - delta)
            ds_l = p_l * (lax.dot_general(dob, vb, _NT, preferred_element_type=F32) - delta)
            dsc_b = ds_c.astype(BF16)
            dsl_b = ds_l.astype(BF16)
            dq_ref[:, sl] = (jnp.dot(dsc_b, kc, preferred_element_type=F32)
                             + jnp.dot(dsl_b, kb, preferred_element_type=F32)) * scale
            dk_ref[0:n_ctx, sl] += lax.dot_general(dsc_b, qv, _TN, preferred_element_type=F32)
            dk_ref[pl.ds(start, NA_BAND), sl] += lax.dot_general(dsl_b, qv, _TN, preferred_element_type=F32)
            dv_ref[0:n_ctx, sl] += lax.dot_general(p_c.astype(BF16), dob, _TN, preferred_element_type=F32)
            dv_ref[pl.ds(start, NA_BAND), sl] += lax.dot_general(p_l.astype(BF16), dob, _TN, preferred_element_type=F32)
            db_ref[hd] += ds_l

    dbs = pl.BlockSpec((None, hg, None, GRID_W, NA_BAND), lambda bi, g, i: (bi, g, _na_geometry(i, nc, rows)[1], 0, 0))
    return pl.pallas_call(
        body,
        out_shape=[jax.ShapeDtypeStruct(q.shape, F32), jax.ShapeDtypeStruct(q.shape, F32), jax.ShapeDtypeStruct(q.shape, F32),
                   jax.ShapeDtypeStruct((b, NA_HEADS, n_cls, GRID_W, NA_BAND), F32)],
        grid=(b, NA_GROUPS_BWD, s // GRID_W), in_specs=[qs, ks, ks, bs, qs, ls, qs], out_specs=[qs, ks, ks, dbs],
        name=f"na_tm_b_{s}", compiler_params=_params(),
    )(q, k, v, bias, o, lse, do)


@functools.partial(jax.custom_vjp, nondiff_argnums=(4,))
def na_attention_tm(q, k, v, bias, n_ctx):
    return _na_tm_fwd(q, k, v, bias, n_ctx)[0]


def _na_attention_tm_fwd(q, k, v, bias, n_ctx):
    o, lse = _na_tm_fwd(q, k, v, bias, n_ctx)
    return o, (q, k, v, bias, o, lse)


def _na_attention_tm_bwd(n_ctx, res, do):
    q, k, v, bias, o, lse = res
    dq, dk, dv, db = _na_tm_bwd(q, k, v, bias, o, lse, do, n_ctx)
    return dq, dk, dv, _sum_rows(db.reshape(db.shape[0], -1, NA_BAND), db.shape[0]).reshape(db.shape[1:])


na_attention_tm.defvjp(_na_attention_tm_fwd, _na_attention_tm_bwd)


def _cmul(ar, ai, br, bi):
    return ar * br - ai * bi, ar * bi + ai * br


def _s5_chunk(n_ctx):
    return min(256, n_ctx)


def _s5_powers(a_re, a_im, t_len):
    a_re, a_im = lax.stop_gradient(a_re), lax.stop_gradient(a_im)
    mag = jnp.sqrt(a_re * a_re + a_im * a_im)
    th = jnp.arctan2(a_im, a_re)
    t = jnp.arange(t_len + 1, dtype=F32)[:, None]
    pm = jnp.where(t == 0, 1.0, jnp.exp(t * jnp.log(jnp.maximum(mag, 1e-37))) * (mag > 0))
    return jnp.stack([pm * jnp.cos(t * th), pm * jnp.sin(t * th)])


def _s5_tables(pw, t_len, rev, conj=False):
    if conj:
        pw = pw * jnp.asarray([1.0, -1.0], F32)[:, None, None]
    steps = jnp.concatenate([pw[:, min(2 ** i, t_len)][:, None] for i in range(8)], axis=1)
    tile = pw[:, 1:9]
    a8k = pw[:, 0:t_len:8]
    if rev:
        tile, a8k = tile[:, ::-1], a8k[:, ::-1]
    misc = jnp.concatenate([pw[:, t_len:t_len + 1], jnp.zeros((2, 7, pw.shape[-1]), F32)], axis=1)
    return jnp.concatenate([steps, tile, misc, a8k], axis=1)


def _scan_chunk(x_re, x_im, tab_ref, hin_re, hin_im, rev, t_len, xs_ref, es_ref):
    outs = [_scan_slab(x_re[:, k:k + LANE], x_im[:, k:k + LANE], tab_ref, hin_re[:, k:k + LANE], hin_im[:, k:k + LANE],
                       rev, t_len, xs_ref, es_ref, k) for k in range(0, x_re.shape[-1], LANE)]
    return tuple(jnp.concatenate([o[t] for o in outs], axis=-1) for t in range(4))


def _scan_slab(x_re, x_im, tab_ref, hin_re, hin_im, rev, t_len, xs_ref, es_ref, k0):
    lanes = LANE
    n2 = t_len // 8
    tab_ref = tab_ref.at[:, :, k0:k0 + LANE]
    rin = lax.broadcasted_iota(jnp.int32, (t_len, lanes), 0) & 7
    for li, sh in enumerate((1, 2, 4)):
        m_re, m_im = tab_ref[0, li:li + 1, :], tab_ref[1, li:li + 1, :]
        amt = sh if not rev else t_len - sh
        c_re, c_im = _cmul(m_re, m_im, pltpu.roll(x_re, amt, 0), pltpu.roll(x_im, amt, 0))
        ok = (rin >= sh) if not rev else (rin < 8 - sh)
        x_re = x_re + jnp.where(ok, c_re, 0.0)
        x_im = x_im + jnp.where(ok, c_im, 0.0)
    xr_ref, xi_ref = xs_ref
    xr_ref[...] = x_re
    xi_ref[...] = x_im
    off = 0 if rev else 7
    e_re = xr_ref[pl.ds(off, n2, stride=8), :]
    e_im = xi_ref[pl.ds(off, n2, stride=8), :]
    row2 = lax.broadcasted_iota(jnp.int32, (n2, lanes), 0)
    sh, li = 1, 3
    while sh < n2:
        m_re, m_im = tab_ref[0, li:li + 1, :], tab_ref[1, li:li + 1, :]
        amt = sh if not rev else n2 - sh
        c_re, c_im = _cmul(m_re, m_im, pltpu.roll(e_re, amt, 0), pltpu.roll(e_im, amt, 0))
        ok = (row2 >= sh) if not rev else (row2 < n2 - sh)
        e_re = e_re + jnp.where(ok, c_re, 0.0)
        e_im = e_im + jnp.where(ok, c_im, 0.0)
        sh, li = sh * 2, li + 1
    es_ref[0] = e_re
    es_ref[1] = e_im
    last = 0 if rev else n2 - 1
    t_re, t_im = _cmul(tab_ref[0, 16:17, :], tab_ref[1, 16:17, :], hin_re, hin_im)
    hout_re = es_ref[0, last:last + 1, :] + t_re
    hout_im = es_ref[1, last:last + 1, :] + t_im
    amt = 1 if not rev else n2 - 1
    ok = (row2 >= 1) if not rev else (row2 < n2 - 1)
    k_re, k_im = _cmul(tab_ref[0, 24:24 + n2, :], tab_ref[1, 24:24 + n2, :], hin_re, hin_im)
    c_re = jnp.where(ok, pltpu.roll(e_re, amt, 0), 0.0) + k_re
    c_im = jnp.where(ok, pltpu.roll(e_im, amt, 0), 0.0) + k_im
    tp_re, tp_im = tab_ref[0, 8:16, :][None], tab_ref[1, 8:16, :][None]
    add_re, add_im = _cmul(tp_re, tp_im, c_re[:, None, :], c_im[:, None, :])
    h_re = xr_ref[...] + add_re.reshape(t_len, lanes)
    h_im = xi_ref[...] + add_im.reshape(t_len, lanes)
    return h_re, h_im, hout_re, hout_im


def _s5_order(j, n_chunks, nc, rev):
    if not rev:
        return j
    return jnp.where(j < nc, nc - 1 - j, n_chunks - 1 - (j - nc))


def _s5_fwd(u, tab, b_bd, c_bd, n_ctx, rev):
    b, s, w = u.shape
    lanes = b_bd.shape[-1]
    t_len = _s5_chunk(n_ctx)
    n_chunks, nc = s // t_len, n_ctx // t_len

    def body(u_ref, tab_ref, b_ref, c_ref, y_ref, h_ref, hin_ref, carry_ref, xr_ref, xi_ref, es_ref):
        xs_ref = (xr_ref, xi_ref)

        @pl.when(pl.program_id(1) == 0)
        def _():
            carry_ref[...] = jnp.zeros_like(carry_ref)

        ub = u_ref[...].astype(BF16)
        x_re = jnp.dot(ub, b_ref[0].astype(BF16), preferred_element_type=F32)
        x_im = jnp.dot(ub, b_ref[1].astype(BF16), preferred_element_type=F32)
        hin_re, hin_im = carry_ref[0, 0:1, :], carry_ref[1, 0:1, :]
        hin_ref[...] = carry_ref[...]
        h_re, h_im, ho_re, ho_im = _scan_chunk(x_re, x_im, tab_ref, hin_re, hin_im, rev, t_len, xs_ref, es_ref)
        carry_ref[0] = jnp.broadcast_to(ho_re, (8, lanes))
        carry_ref[1] = jnp.broadcast_to(ho_im, (8, lanes))
        h_ref[0] = h_re
        h_ref[1] = h_im
        y_ref[...] = (jnp.dot(h_re.astype(BF16), c_ref[0].astype(BF16), preferred_element_type=F32)
                      - jnp.dot(h_im.astype(BF16), c_ref[1].astype(BF16), preferred_element_type=F32))

    order = lambda j: _s5_order(j, n_chunks, nc, rev)
    whole = lambda arr: pl.BlockSpec(arr.shape, lambda bi, j: (0,) * arr.ndim)
    return pl.pallas_call(
        body,
        out_shape=[jax.ShapeDtypeStruct((b, s, w), F32), jax.ShapeDtypeStruct((2, b, s, lanes), F32),
                   jax.ShapeDtypeStruct((2, b, n_chunks, 8, lanes), F32)],
        grid=(b, n_chunks),
        in_specs=[pl.BlockSpec((None, t_len, w), lambda bi, j: (bi, order(j), 0)), whole(tab), whole(b_bd), whole(c_bd)],
        out_specs=[pl.BlockSpec((None, t_len, w), lambda bi, j: (bi, order(j), 0)),
                   pl.BlockSpec((2, None, t_len, lanes), lambda bi, j: (0, bi, order(j), 0)),
                   pl.BlockSpec((2, None, None, 8, lanes), lambda bi, j: (0, bi, order(j), 0, 0))],
        scratch_shapes=[pltpu.VMEM((2, 8, lanes), F32), pltpu.VMEM((t_len, LANE), F32), pltpu.VMEM((t_len, LANE), F32),
                        pltpu.VMEM((2, t_len // 8, LANE), F32)],
        name=f"s5_f_{s}_{int(rev)}", compiler_params=_params(),
    )(u, tab, b_bd, c_bd)


def _s5_bwd(u, tab_adj, b_bd, c_bd, h, hin, dy, n_ctx, rev):
    b, s, w = u.shape
    lanes = b_bd.shape[-1]
    t_len = _s5_chunk(n_ctx)
    n_chunks, nc = s // t_len, n_ctx // t_len
    arev = not rev

    def body(u_ref, tab_ref, b_ref, c_ref, h_ref, hin_ref, dy_ref, du_ref, db_ref, dc_ref, da_ref,
             carry_ref, xr_ref, xi_ref, es_ref):
        xs_ref = (xr_ref, xi_ref)
        first = jnp.logical_and(pl.program_id(0) == 0, pl.program_id(1) == 0)

        @pl.when(pl.program_id(1) == 0)
        def _():
            carry_ref[...] = jnp.zeros_like(carry_ref)

        dyv = dy_ref[...]
        dyb = dyv.astype(BF16)
        dn = (((1,), (1,)), ((), ()))
        dt = (((0,), (0,)), ((), ()))
        x_re = lax.dot_general(dyb, c_ref[0].astype(BF16), dn, preferred_element_type=F32)
        x_im = -lax.dot_general(dyb, c_ref[1].astype(BF16), dn, preferred_element_type=F32)
        g_re, g_im, go_re, go_im = _scan_chunk(x_re, x_im, tab_ref, carry_ref[0, 0:1, :], carry_ref[1, 0:1, :],
                                               arev, t_len, xs_ref, es_ref)
        carry_ref[0] = jnp.broadcast_to(go_re, (8, lanes))
        carry_ref[1] = jnp.broadcast_to(go_im, (8, lanes))
        h_re, h_im = h_ref[0], h_ref[1]
        gb_re, gb_im = g_re.astype(BF16), g_im.astype(BF16)
        du_ref[...] = (lax.dot_general(gb_re, b_ref[0].astype(BF16), dn, preferred_element_type=F32)
                       + lax.dot_general(gb_im, b_ref[1].astype(BF16), dn, preferred_element_type=F32))
        ub = u_ref[...].astype(BF16)
        db_re = lax.dot_general(ub, gb_re, dt, preferred_element_type=F32)
        db_im = lax.dot_general(ub, gb_im, dt, preferred_element_type=F32)
        dc_re = lax.dot_general(h_re.astype(BF16), dyb, dt, preferred_element_type=F32)
        dc_im = -lax.dot_general(h_im.astype(BF16), dyb, dt, preferred_element_type=F32)
        row = lax.broadcasted_iota(jnp.int32, (t_len, lanes), 0)
        amt = 1 if not rev else t_len - 1
        edge = (row == 0) if not rev else (row == t_len - 1)
        hp_re = jnp.where(edge, hin_ref[0, 0:1, :], pltpu.roll(h_re, amt, 0))
        hp_im = jnp.where(edge, hin_ref[1, 0:1, :], pltpu.roll(h_im, amt, 0))
        da_re = jnp.sum(g_re * hp_re + g_im * hp_im, axis=0, keepdims=True)
        da_im = jnp.sum(g_im * hp_re - g_re * hp_im, axis=0, keepdims=True)

        @pl.when(first)
        def _():
            db_ref[0], db_ref[1] = db_re, db_im
            dc_ref[0], dc_ref[1] = dc_re, dc_im
            da_ref[0] = jnp.broadcast_to(da_re, (8, lanes))
            da_ref[1] = jnp.broadcast_to(da_im, (8, lanes))

        @pl.when(jnp.logical_not(first))
        def _():
            db_ref[0] += db_re
            db_ref[1] += db_im
            dc_ref[0] += dc_re
            dc_ref[1] += dc_im
            da_ref[0] += jnp.broadcast_to(da_re, (8, lanes))
            da_ref[1] += jnp.broadcast_to(da_im, (8, lanes))

    order = lambda j: _s5_order(n_chunks - 1 - j, n_chunks, nc, rev)
    whole = lambda arr: pl.BlockSpec(arr.shape, lambda bi, j: (0,) * arr.ndim)
    us = pl.BlockSpec((None, t_len, w), lambda bi, j: (bi, order(j), 0))
    return pl.pallas_call(
        body,
        out_shape=[jax.ShapeDtypeStruct((b, s, w), F32), jax.ShapeDtypeStruct(b_bd.shape, F32),
                   jax.ShapeDtypeStruct(c_bd.shape, F32), jax.ShapeDtypeStruct((2, 8, lanes), F32)],
        grid=(b, n_chunks),
        in_specs=[us, whole(tab_adj), whole(b_bd), whole(c_bd),
                  pl.BlockSpec((2, None, t_len, lanes), lambda bi, j: (0, bi, order(j), 0)),
                  pl.BlockSpec((2, None, None, 8, lanes), lambda bi, j: (0, bi, order(j), 0, 0)), us],
        out_specs=[us, whole(b_bd), whole(c_bd), pl.BlockSpec((2, 8, lanes), lambda bi, j: (0, 0, 0))],
        scratch_shapes=[pltpu.VMEM((2, 8, lanes), F32), pltpu.VMEM((t_len, LANE), F32), pltpu.VMEM((t_len, LANE), F32),
                        pltpu.VMEM((2, t_len // 8, LANE), F32)],
        name=f"s5_b_{s}_{int(rev)}", compiler_params=_params(),
    )(u, tab_adj, b_bd, c_bd, h, hin, dy)


@functools.partial(jax.custom_vjp, nondiff_argnums=(4, 5))
def s5_direction(u, a, b_bd, c_bd, n_ctx, rev):
    t_len = _s5_chunk(n_ctx)
    return _s5_fwd(u, _s5_tables(_s5_powers(a[0], a[1], t_len), t_len, rev), b_bd, c_bd, n_ctx, rev)[0]


def _s5_direction_fwd(u, a, b_bd, c_bd, n_ctx, rev):
    t_len = _s5_chunk(n_ctx)
    pw = _s5_powers(a[0], a[1], t_len)
    y, h, hin = _s5_fwd(u, _s5_tables(pw, t_len, rev), b_bd, c_bd, n_ctx, rev)
    return y, (u, pw, b_bd, c_bd, h, hin)


def _s5_direction_bwd(n_ctx, rev, res, dy):
    u, pw, b_bd, c_bd, h, hin = res
    tab_adj = _s5_tables(pw, _s5_chunk(n_ctx), not rev, conj=True)
    du, db, dc, da = _s5_bwd(u, tab_adj, b_bd, c_bd, h, hin, dy, n_ctx, rev)
    return du, da[:, 0, :], db, dc


s5_direction.defvjp(_s5_direction_fwd, _s5_direction_bwd)


def _s5_discretize(lam_re, lam_im, log_dt, b_re, b_im):
    dt = jnp.exp(log_dt)[:, None]
    mag = jnp.exp(lam_re * dt)
    a_re = mag * jnp.cos(lam_im * dt)
    a_im = mag * jnp.sin(lam_im * dt)
    den = jnp.square(lam_re) + jnp.square(lam_im)
    f_re = ((a_re - 1.0) * lam_re + a_im * lam_im) / den
    f_im = (a_im * lam_re - (a_re - 1.0) * lam_im) / den
    bb_re = f_re[..., None] * b_re - f_im[..., None] * b_im
    bb_im = f_re[..., None] * b_im + f_im[..., None] * b_re
    return a_re, a_im, bb_re, bb_im


def _block_diag(t):
    g, r, c = t.shape
    return (jnp.eye(g, dtype=F32)[:, None, :, None] * t[:, :, None, :]).reshape(g * r, g * c)


def _loss_head(y, target):
    b, n, d = y.shape
    ts = _pick(n, (256, 128, 64))

    def body(y_ref, t_ref, loss_ref, dy_ref):
        first = jnp.logical_and(pl.program_id(0) == 0, pl.program_id(1) == 0)
        err = y_ref[...] - t_ref[...]
        dy_ref[...] = err * (1.0 / d)
        part = 0.5 * jnp.sum(jnp.sum(err * err, axis=-1, keepdims=True) * (1.0 / d), axis=0, keepdims=True)
        part = jnp.broadcast_to(part, (8, LANE))

        @pl.when(first)
        def _():
            loss_ref[...] = part

        @pl.when(jnp.logical_not(first))
        def _():
            loss_ref[...] += part

    blk = pl.BlockSpec((None, ts, d), lambda bi, i: (bi, i, 0))
    return pl.pallas_call(
        body, out_shape=[jax.ShapeDtypeStruct((8, LANE), F32), jax.ShapeDtypeStruct((b, n, d), F32)],
        grid=(b, n // ts), in_specs=[blk, blk], out_specs=[pl.BlockSpec((8, LANE), lambda bi, i: (0, 0)), blk],
        name="loss_head", compiler_params=_params(),
    )(y, target)


def _adamw(w, g, m, v):
    shape = w.shape
    n = int(np.prod(shape))
    cols = shape[-1]
    r = n // cols
    tr = _pick(r, (512, 256, 128, 64, 32, 16, 8))
    c1 = 1.0 / (1.0 - ADAM_B1 ** ADAM_STEP)
    c2 = 1.0 / (1.0 - ADAM_B2 ** ADAM_STEP)

    def body(w_ref, g_ref, m_ref, v_ref, d_ref, mo_ref, vo_ref):
        gv = g_ref[...]
        m2 = ADAM_B1 * m_ref[...] + (1.0 - ADAM_B1) * gv
        v2 = ADAM_B2 * v_ref[...] + (1.0 - ADAM_B2) * (gv * gv)
        d_ref[...] = -ADAM_LR * ((m2 * c1) / (jnp.sqrt(v2 * c2) + ADAM_EPS) + ADAM_WD * w_ref[...])
        mo_ref[...] = m2
        vo_ref[...] = v2

    blk = pl.BlockSpec((tr, cols), lambda i: (i, 0))
    outs = pl.pallas_call(
        body, out_shape=[jax.ShapeDtypeStruct((r, cols), F32)] * 3, grid=(r // tr,),
        in_specs=[blk] * 4, out_specs=[blk] * 3, name=f"adamw_{r}x{cols}", compiler_params=_params(),
    )(*[t.reshape(r, cols) for t in (w, g, m, v)])
    return tuple(o.reshape(shape) for o in outs)


def _sum_rows(x, n):
    _, r, c = x.shape
    tr = _pick(r, (512, 256, 128, 64, 32, 16, 8))

    def body(x_ref, o_ref):
        acc = x_ref[0]
        for j in range(1, n):
            acc = acc + x_ref[j]
        o_ref[...] = acc

    return pl.pallas_call(
        body, out_shape=jax.ShapeDtypeStruct((r, c), F32), grid=(r // tr,),
        in_specs=[pl.BlockSpec((n, tr, c), lambda i: (0, i, 0))], out_specs=pl.BlockSpec((tr, c), lambda i: (i, 0)),
        name=f"sum{n}_{r}x{c}", compiler_params=_params(),
    )(x)


def _accumulate(parts, out_dtype):
    r, c = parts[0].shape[-2:]
    tr = _pick(r, (512, 256, 128, 64, 32, 16))

    def body(*refs):
        acc = None
        for ref in refs[:-1]:
            terms = [ref[j] for j in range(ref.shape[0])] if len(ref.shape) == 3 else [ref[...]]
            for t in terms:
                acc = t.astype(F32) if acc is None else acc + t.astype(F32)
        refs[-1][...] = acc.astype(out_dtype)

    specs = [pl.BlockSpec((p.shape[0], tr, c), lambda i: (0, i, 0)) if p.ndim == 3 else pl.BlockSpec((tr, c), lambda i: (i, 0))
             for p in parts]
    tag = "_".join(str(p.shape[0]) if p.ndim == 3 else "1" for p in parts)
    return pl.pallas_call(
        body, out_shape=jax.ShapeDtypeStruct((r, c), out_dtype), grid=(r // tr,), in_specs=specs,
        out_specs=pl.BlockSpec((tr, c), lambda i: (i, 0)), name=f"accumulate_{tag}_{r}x{c}_{jnp.dtype(out_dtype).name}",
        compiler_params=_params(),
    )(*parts)


def _add2(x, y):
    shape = x.shape
    c = shape[-1]
    r = int(np.prod(shape)) // c
    tr = _pick(r, (512, 256, 128, 64, 32, 16, 8))

    def body(x_ref, y_ref, o_ref):
        o_ref[...] = x_ref[...] + y_ref[...]

    blk = pl.BlockSpec((tr, c), lambda i: (i, 0))
    return pl.pallas_call(
        body, out_shape=jax.ShapeDtypeStruct((r, c), F32), grid=(r // tr,), in_specs=[blk, blk], out_specs=blk,
        name=f"add2_{r}x{c}", compiler_params=_params(),
    )(x.reshape(r, c), y.reshape(r, c)).reshape(shape)


_FLIPS = ((1, 0), (0, 1), (1, 1))


def _me():
    return lax.axis_index("x"), lax.axis_index("y"), lax.axis_index("c")


def allgather8(v):
    m_per, n = v.shape

    def body(x_ref, out_ref, send_sems, recv_sems, local_sem):
        x, y, c = _me()
        me, sibling = (x, y, c), (x, y, 1 - c)
        chips = [(1 - x, y), (x, 1 - y), (1 - x, 1 - y)]

        def rows(px, py, pc):
            return out_ref.at[pl.ds((4 * px + 2 * py + pc) * m_per, m_per), :]

        def copy(k, block, to, src=None):
            return pltpu.make_async_remote_copy(
                src_ref=rows(*block) if src is None else src, dst_ref=rows(*block),
                send_sem=send_sems.at[k], recv_sem=recv_sems.at[k], device_id=to, device_id_type=MESH)

        mine = pltpu.make_async_copy(x_ref, rows(*me), local_sem)
        mine.start()
        first = [copy(0, me, sibling, src=x_ref)]
        first += [copy(1 + j, me, (*chip, c), src=x_ref) for j, chip in enumerate(chips)]
        for cp in first:
            cp.start()
        passed = [copy(4 + j, (*chip, c), sibling) for j, chip in enumerate(chips)]
        for j, chip in enumerate(chips):
            copy(1 + j, (*chip, c), me).wait_recv()
            passed[j].start()
        copy(0, sibling, me).wait_recv()
        for j, chip in enumerate(chips):
            copy(4 + j, (*chip, 1 - c), me).wait_recv()
        for cp in first + passed:
            cp.wait_send()
        mine.wait()

    return pl.pallas_call(
        body, out_shape=jax.ShapeDtypeStruct((N_DEV * m_per, n), v.dtype), in_specs=[VMEM_SPEC], out_specs=VMEM_SPEC,
        scratch_shapes=[pltpu.SemaphoreType.DMA((7,)), pltpu.SemaphoreType.DMA((7,)), pltpu.SemaphoreType.DMA],
        name=f"allgather8_{m_per}x{n}", compiler_params=_params(),
    )(v)


def _row_chunks(rows, tile_rows, want):
    n = want
    while n > 1 and rows % (n * tile_rows):
        n //= 2
    return [(i * (rows // n), rows // n) for i in range(n)]


def _remote(src, dst, send_sem, recv_sem, to):
    return pltpu.make_async_remote_copy(src_ref=src, dst_ref=dst, send_sem=send_sem, recv_sem=recv_sem, device_id=to,
                                        device_id_type=MESH)


def plane_allgather(big, small):
    rows = big.shape[0]
    rh = rows // 2
    rq = rh // 2
    tile = 16 if big.dtype == BF16 else 8
    assert rq % tile == 0
    ch_full = _row_chunks(rows, tile, 8)
    ch_half = _row_chunks(rh, tile, 4)

    def body(big_ref, small_ref, obig_ref, osmall_ref, send_sems, recv_sems, relay_send, relay_recv, fwd_send, fwd_recv,
             own_send, own_recv):
        x, y, c = _me()
        me = 2 * x + y
        sibling = (x, y, 1 - c)
        nbr_x, nbr_y, diag = (1 - x, y, c), (x, 1 - y, c), (1 - x, 1 - y, c)
        xi, yi, di = 2 * (1 - x) + y, 2 * x + (1 - y), 2 * (1 - x) + (1 - y)
        base, obase = c * rh, (1 - c) * rh
        mine, other = pl.ds(base, rh), pl.ds(obase, rh)
        qa, qb = pl.ds(base, rq), pl.ds(base + rq, rq)
        for st, sz in ch_full:
            sl = pl.ds(st, sz)
            _remote(big_ref.at[sl], obig_ref.at[me, sl], own_send.at[0], own_recv.at[0], sibling).start()
        _remote(small_ref, osmall_ref.at[me], own_send.at[1], own_recv.at[1], sibling).start()
        for j, peer in enumerate((nbr_x, nbr_y)):
            for st, sz in ch_half:
                sl = pl.ds(base + st, sz)
                _remote(big_ref.at[sl], obig_ref.at[me, sl], send_sems.at[j], recv_sems.at[j], peer).start()
        for j, peer in enumerate((nbr_x, nbr_y, diag)):
            _remote(small_ref, osmall_ref.at[me], send_sems.at[3 + j], recv_sems.at[3 + j], peer).start()

        def pass_on(k, slot, sl):
            _remote(obig_ref.at[slot, sl], obig_ref.at[slot, sl], fwd_send.at[k], fwd_recv.at[k], sibling).start()

        _remote(big_ref.at[mine], obig_ref.at[xi, mine], send_sems.at[0], recv_sems.at[0], nbr_x).wait_recv()
        _remote(obig_ref.at[xi, qa], obig_ref.at[xi, qa], relay_send.at[0], relay_recv.at[0], nbr_y).start()
        pass_on(0, xi, mine)
        _remote(big_ref.at[mine], obig_ref.at[yi, mine], send_sems.at[1], recv_sems.at[1], nbr_y).wait_recv()
        _remote(obig_ref.at[yi, qb], obig_ref.at[yi, qb], relay_send.at[1], relay_recv.at[1], nbr_x).start()
        pass_on(1, yi, mine)
        _remote(obig_ref.at[di, qa], obig_ref.at[di, qa], relay_send.at[0], relay_recv.at[0], nbr_y).wait_recv()
        pass_on(2, di, qa)
        _remote(obig_ref.at[di, qb], obig_ref.at[di, qb], relay_send.at[1], relay_recv.at[1], nbr_x).wait_recv()
        pass_on(3, di, qb)
        for j, (peer, slot) in enumerate(((nbr_x, xi), (nbr_y, yi), (diag, di))):
            _remote(small_ref, osmall_ref.at[slot], send_sems.at[3 + j], recv_sems.at[3 + j], peer).wait_recv()
        oqa, oqb = pl.ds(obase, rq), pl.ds(obase + rq, rq)
        for k, (slot, sl) in enumerate(((xi, other), (yi, other), (di, oqa), (di, oqb))):
            _remote(obig_ref.at[slot, sl], obig_ref.at[slot, sl], fwd_send.at[k], fwd_recv.at[k], sibling).wait_recv()
        for k, (slot, sl) in enumerate(((xi, mine), (yi, mine), (di, qa), (di, qb))):
            _remote(obig_ref.at[slot, sl], obig_ref.at[slot, sl], fwd_send.at[k], fwd_recv.at[k], sibling).wait_send()
        for j, peer in enumerate((nbr_x, nbr_y)):
            _remote(big_ref.at[mine], obig_ref.at[me, mine], send_sems.at[j], recv_sems.at[j], peer).wait_send()
        for j, peer in enumerate((nbr_x, nbr_y, diag)):
            _remote(small_ref, osmall_ref.at[me], send_sems.at[3 + j], recv_sems.at[3 + j], peer).wait_send()
        _remote(obig_ref.at[xi, qa], obig_ref.at[xi, qa], relay_send.at[0], relay_recv.at[0], nbr_y).wait_send()
        _remote(obig_ref.at[yi, qb], obig_ref.at[yi, qb], relay_send.at[1], relay_recv.at[1], nbr_x).wait_send()
        _remote(big_ref, obig_ref.at[me], own_send.at[0], own_recv.at[0], sibling).wait()
        _remote(small_ref, osmall_ref.at[me], own_send.at[1], own_recv.at[1], sibling).wait()

    dma = pltpu.SemaphoreType.DMA
    return pl.pallas_call(
        body, out_shape=[jax.ShapeDtypeStruct((N_PLANE,) + big.shape, big.dtype),
                         jax.ShapeDtypeStruct((N_PLANE,) + small.shape, small.dtype)],
        in_specs=[ANY, ANY], out_specs=[ANY, ANY],
        scratch_shapes=[dma((6,)), dma((6,)), dma((2,)), dma((2,)), dma((4,)), dma((4,)), dma((2,)), dma((2,))],
        name="plane_allgather", compiler_params=_params(),
    )(big, small)


def neighbour_exchange(to_x, to_y):
    tile = 16 if to_x.dtype == BF16 else 8
    chunks = _row_chunks(to_x.shape[0], tile, 4)

    def body(ax_ref, ay_ref, fx_ref, fy_ref, send_sems, recv_sems):
        x, y, c = _me()
        for k, (src, dst, peer) in enumerate(((ax_ref, fx_ref, (1 - x, y, c)), (ay_ref, fy_ref, (x, 1 - y, c)))):
            for st, sz in chunks:
                sl = pl.ds(st, sz)
                _remote(src.at[sl], dst.at[sl], send_sems.at[k], recv_sems.at[k], peer).start()
        for k, (src, dst, peer) in enumerate(((ax_ref, fx_ref, (1 - x, y, c)), (ay_ref, fy_ref, (x, 1 - y, c)))):
            _remote(src, dst, send_sems.at[k], recv_sems.at[k], peer).wait()

    shape = jax.ShapeDtypeStruct(to_x.shape, to_x.dtype)
    return pl.pallas_call(
        body, out_shape=[shape, shape], in_specs=[ANY, ANY], out_specs=[ANY, ANY],
        scratch_shapes=[pltpu.SemaphoreType.DMA((2,)), pltpu.SemaphoreType.DMA((2,))],
        name=f"neighbour_exchange_{to_x.shape[0]}", compiler_params=_params(),
    )(to_x, to_y)


def sibling_halves(buf):
    n_blk, _, rows, cols = buf.shape
    tile = 16 if buf.dtype == BF16 else 8
    chunks = _row_chunks(rows, tile, 2)

    def body(buf_ref, got_ref, send_sem, recv_sem):
        x, y, c = _me()
        for j in range(n_blk):
            for st, sz in chunks:
                sl = pl.ds(st, sz)
                _remote(buf_ref.at[j, 1 - c, sl], got_ref.at[j, sl], send_sem, recv_sem, (x, y, 1 - c)).start()
        _remote(got_ref, got_ref, send_sem, recv_sem, (x, y, 1 - c)).wait()

    return pl.pallas_call(
        body, out_shape=jax.ShapeDtypeStruct((n_blk, rows, cols), buf.dtype), in_specs=[ANY], out_specs=ANY,
        scratch_shapes=[pltpu.SemaphoreType.DMA, pltpu.SemaphoreType.DMA],
        name="sibling_halves", compiler_params=_params(),
    )(buf)


def sibling_swap(s):
    tile = 16 if s.dtype == BF16 else 8
    chunks = _row_chunks(s.shape[0], tile, 8)

    def body(s_ref, got_ref, send_sem, recv_sem):
        x, y, c = _me()
        for st, sz in chunks:
            sl = pl.ds(st, sz)
            _remote(s_ref.at[sl], got_ref.at[sl], send_sem, recv_sem, (x, y, 1 - c)).start()
        _remote(s_ref, got_ref, send_sem, recv_sem, (x, y, 1 - c)).wait()

    return pl.pallas_call(
        body, out_shape=jax.ShapeDtypeStruct(s.shape, s.dtype), in_specs=[ANY], out_specs=ANY,
        scratch_shapes=[pltpu.SemaphoreType.DMA, pltpu.SemaphoreType.DMA],
        name="sibling_swap", compiler_params=_params(),
    )(s)


def _op(cache, fn, name, kinds, out_dims, **kw):
    key = (name, tuple(out_dims), tuple(sorted(kw.items())))
    if key not in cache:
        cache[key] = make_rowwise(fn, name, kinds, out_dims, **kw)
    return cache[key]


def _even_mixer(ops, a, w, n_ctx):
    b, s, d = a.shape
    proj = linear(a.reshape(b * s, d), w["e_w_in"]).reshape(b, s, -1)
    q, k, v, u = jnp.split(proj, [GQA_Q_W, GQA_Q_W + GQA_KV_W, GQA_Q_W + 2 * GQA_KV_W], axis=-1)
    cos, sin = _rope_tables(n_ctx, s - n_ctx, HEAD_DIM, 0, HEAD_DIM)
    shift = HEAD_DIM // 4
    qn = head_norm_rope(q, w["e_g_q"][None], jnp.tile(cos, (1, GQA_Q_HEADS)), jnp.tile(sin, (1, GQA_Q_HEADS)),
                        jnp.asarray(_rope_signs(HEAD_DIM, 0, HEAD_DIM, GQA_Q_HEADS)), GQA_Q_HEADS, shift)
    kn = head_norm_rope(k, w["e_g_k"][None], jnp.tile(cos, (1, GQA_KV_HEADS)), jnp.tile(sin, (1, GQA_KV_HEADS)),
                        jnp.asarray(_rope_signs(HEAD_DIM, 0, HEAD_DIM, GQA_KV_HEADS)), GQA_KV_HEADS, shift)
    att = gqa_attention(qn, kn, v, n_ctx)
    ys = []
    for dr in range(2):
        a_re, a_im, bb_re, bb_im = _s5_discretize(w["ssm_lam_re"][dr], w["ssm_lam_im"][dr], w["ssm_log_dt"][dr],
                                                  w["ssm_b_re"][dr], w["ssm_b_im"][dr])
        a_flat = jnp.stack([a_re.reshape(-1), a_im.reshape(-1)])
        b_bd = jnp.stack([_block_diag(jnp.swapaxes(bb_re, 1, 2)), _block_diag(jnp.swapaxes(bb_im, 1, 2))])
        c_bd = jnp.stack([_block_diag(jnp.swapaxes(w["ssm_c_re"][dr], 1, 2)),
                          _block_diag(jnp.swapaxes(w["ssm_c_im"][dr], 1, 2))])
        ys.append(s5_direction(u, a_flat, b_bd, c_bd, n_ctx, dr == 1))
    pre = _op(ops, _fn_glu_pre, "glu_pre", ("row", "row", "row", "glob"), (SSM_WIDTH,))
    post = _op(ops, _fn_glu_post, "glu_post", ("row", "row", "glob"), (SSM_WIDTH,))
    z = pre(u, ys[0], ys[1], w["ssm_d"][None])[0]
    t = linear(z.reshape(b * s, SSM_WIDTH), w["ssm_w_glu"]).reshape(b, s, SSM_WIDTH)
    ssm = post(z, t, w["ssm_b_glu"][None])[0]
    mix = jnp.concatenate([att, ssm], axis=-1)
    return linear(mix.reshape(b * s, -1), w["e_w_out"]).reshape(b, s, d)


def _odd_mixer(ops, a, w, n_ctx):
    b, s, d = a.shape
    w_in = jnp.pad(w["o_w_in"], ((0, 0), (0, ODD_IN_PAD - ODD_IN_W)))
    proj = linear(a.reshape(b * s, d), w_in).reshape(b, s, -1)
    c1 = MLA_Q_RANK
    c2 = c1 + MLA_KV_RANK
    c3 = c2 + MLA_ROPE
    cq, ckv, kr, nq, nk, nv, _ = jnp.split(proj, [c1, c2, c3, c3 + NA_W, c3 + 2 * NA_W, ODD_IN_W], axis=-1)
    nrm = lambda wd: _op(ops, _fn_norm, f"norm{wd}", ("row", "glob"), (wd,))
    cqn = nrm(MLA_Q_RANK)(cq, w["mla_g_cq"][None])[0]
    ckvn = nrm(MLA_KV_RANK)(ckv, w["mla_g_ckv"][None])[0]
    q = linear(cqn.reshape(b * s, -1), w["mla_w_uq"]).reshape(b, s, -1)
    kv = linear(ckvn.reshape(b * s, -1), w["mla_w_ukv"]).reshape(b, s, -1)
    cos, sin = _rope_tables(n_ctx, s - n_ctx, MLA_QK, MLA_NOPE, MLA_ROPE)
    cos, sin = jnp.tile(cos, (1, MLA_HEADS)), jnp.tile(sin, (1, MLA_HEADS))
    sg = jnp.asarray(_rope_signs(MLA_QK, MLA_NOPE, MLA_ROPE, MLA_HEADS))
    mq = head_norm_rope(q, w["mla_g_q"][None], cos, sin, sg, MLA_HEADS, MLA_ROPE // 4)
    mk = mla_k_prep(kv, kr, w["mla_g_k"][None], cos, sin, sg)
    mla = mla_attention(mq, mk, kv, n_ctx)
    nqn = head_norm_rope(nq, w["na_g_q"][None], None, None, None, NA_HEADS, 0)
    nkn = head_norm_rope(nk, w["na_g_k"][None], None, None, None, NA_HEADS, 0)
    na = na_attention_tm(nqn, nkn, nv, na_bias_table(w["na_rpb"]), n_ctx)
    mix = jnp.concatenate([mla, na], axis=-1)
    return linear(mix.reshape(b * s, -1), w["o_w_out"]).reshape(b, s, d)


_EVEN_KEYS = ("e_w_in", "e_w_out", "e_g_q", "e_g_k", "ssm_lam_re", "ssm_lam_im", "ssm_log_dt", "ssm_b_re", "ssm_b_im",
              "ssm_c_re", "ssm_c_im", "ssm_d", "ssm_w_glu", "ssm_b_glu")
_ODD_KEYS = ("o_w_in", "o_w_out", "mla_g_cq", "mla_g_ckv", "mla_w_uq", "mla_w_ukv", "mla_g_q", "mla_g_k", "na_g_q",
             "na_g_k", "na_rpb")


def _trunk(x_all, mods, w, n_ctx):
    ops = {}
    depth = mods.shape[0]
    b, s, d = x_all.shape
    modulate = make_modulate(d, n_ctx)
    gated = make_gated_add(d, n_ctx)
    x = x_all
    for i in range(depth):
        j = i // 2
        m = [mods[i][:, :, r:r + 1, :] for r in range(N_MOD)]
        a, x = modulate(x, w["g_norm1"][i][None], m[0], m[1])
        if i % 2 == 0:
            o = _even_mixer(ops, a, {k: w[k][j] for k in _EVEN_KEYS}, n_ctx)
        else:
            o = _odd_mixer(ops, a, {k: w[k][j] for k in _ODD_KEYS}, n_ctx)
        x = gated(x, o, m[2])
        a2, x = modulate(x, w["g_norm2"][i][None], m[3], m[4])
        f = ffn(a2.reshape(b * s, d), w["w_ff1"][i], w["w_ff2"][i]).reshape(b, s, d)
        x = gated(x, f, m[5])
    return x[:, n_ctx:]


def local_step(x, ctx, mods, w, loss_target):
    n_ctx = ctx.shape[1]
    x_all = jnp.concatenate([ctx, x], axis=1)
    y, vjp = jax.vjp(lambda xa, md, ww: _trunk(xa, md, ww, n_ctx), x_all, mods, w)
    loss_tile, dy = _loss_head(y, loss_target)
    dx_all, dmods, dw = vjp(dy)
    return loss_tile[0, 0], dx_all[:, n_ctx:], dmods, dw


_SHARDED = (("w_ff1", 2), ("w_ff2", 1), ("e_w_in", 2), ("e_w_out", 1), ("o_w_in", 2), ("o_w_out", 1),
            ("mla_w_uq", 2), ("mla_w_ukv", 2), ("ssm_w_glu", 1))
_SHARDED_SMALL = (("mla_g_cq", 1), ("mla_g_ckv", 1))
_REPLICATED = ("g_norm1", "g_norm2", "e_g_q", "e_g_k", "ssm_lam_re", "ssm_lam_im", "ssm_log_dt", "ssm_b_re", "ssm_b_im",
               "ssm_c_re", "ssm_c_im", "ssm_d", "ssm_b_glu", "mla_g_q", "mla_g_k", "na_g_q", "na_g_k", "na_rpb")
_WEIGHTS = ("c_ctx", "w_mod", "b_mod", "g_norm1", "g_norm2", "w_ff1", "w_ff2", "e_w_in", "e_w_out", "e_g_q", "e_g_k",
            "ssm_lam_re", "ssm_lam_im", "ssm_log_dt", "ssm_b_re", "ssm_b_im", "ssm_c_re", "ssm_c_im", "ssm_d",
            "ssm_w_glu", "ssm_b_glu", "o_w_in", "o_w_out", "mla_g_cq", "mla_g_ckv", "mla_w_uq", "mla_w_ukv", "mla_g_q",
            "mla_g_k", "na_g_q", "na_g_k", "na_rpb")
_PACK_ROWS = 64


def _pack(arrs, dtype, cols=1024, row_mult=_PACK_ROWS):
    blocks, tail, off = [], [], 0
    for a in arrs:
        n = int(np.prod(a.shape))
        if not tail and off % cols == 0 and n % cols == 0:
            blocks.append(a.astype(dtype).reshape(-1, cols))
        else:
            tail.append(a.astype(dtype).reshape(-1))
        off += n
    rows = -(-off // cols)
    pad = (-rows) % row_mult * cols + rows * cols - off
    if tail or pad:
        blocks.append(jnp.concatenate(tail + [jnp.zeros((pad,), dtype)]).reshape(-1, cols))
    return jnp.concatenate(blocks, axis=0)


def _unpack(packed, shapes):
    cols = packed.shape[-1]
    packed = packed.reshape(-1, cols)
    out, off = [], 0
    for sh in shapes:
        n = int(np.prod(sh))
        if off % cols == 0 and n % cols == 0:
            out.append(packed[off // cols:(off + n) // cols].reshape(sh))
        else:
            r0, r1 = off // cols, -(-(off + n) // cols)
            out.append(packed[r0:r1].reshape(-1)[off - r0 * cols:off - r0 * cols + n].reshape(sh))
        off += n
    return out


def kernel(x, c, ctx, c_ctx, w_mod, b_mod, g_norm1, g_norm2, w_ff1, w_ff2, e_w_in, e_w_out, e_g_q, e_g_k, ssm_lam_re, ssm_lam_im, ssm_log_dt, ssm_b_re, ssm_b_im, ssm_c_re, ssm_c_im, ssm_d, ssm_w_glu, ssm_b_glu, o_w_in, o_w_out, mla_g_cq, mla_g_ckv, mla_w_uq, mla_w_ukv, mla_g_q, mla_g_k, na_g_q, na_g_k, na_rpb, loss_target, m_c_ctx, m_w_mod, m_b_mod, m_g_norm1, m_g_norm2, m_w_ff1, m_w_ff2, m_e_w_in, m_e_w_out, m_e_g_q, m_e_g_k, m_ssm_lam_re, m_ssm_lam_im, m_ssm_log_dt, m_ssm_b_re, m_ssm_b_im, m_ssm_c_re, m_ssm_c_im, m_ssm_d, m_ssm_w_glu, m_ssm_b_glu, m_o_w_in, m_o_w_out, m_mla_g_cq, m_mla_g_ckv, m_mla_w_uq, m_mla_w_ukv, m_mla_g_q, m_mla_g_k, m_na_g_q, m_na_g_k, m_na_rpb, v_c_ctx, v_w_mod, v_b_mod, v_g_norm1, v_g_norm2, v_w_ff1, v_w_ff2, v_e_w_in, v_e_w_out, v_e_g_q, v_e_g_k, v_ssm_lam_re, v_ssm_lam_im, v_ssm_log_dt, v_ssm_b_re, v_ssm_b_im, v_ssm_c_re, v_ssm_c_im, v_ssm_d, v_ssm_w_glu, v_ssm_b_glu, v_o_w_in, v_o_w_out, v_mla_g_cq, v_mla_g_ckv, v_mla_w_uq, v_mla_w_ukv, v_mla_g_q, v_mla_g_k, v_na_g_q, v_na_g_k, v_na_rpb):
    env = dict(locals())
    weights = {n: env[n] for n in _WEIGHTS}
    mom_m = {n: env["m_" + n] for n in _WEIGHTS}
    mom_v = {n: env["v_" + n] for n in _WEIGHTS}
    ax, ay, ac = _me()
    plane = 2 * ax + ay
    dev = 4 * ax + 2 * ay + ac
    b_loc, d = c.shape
    depth = w_mod.shape[0]
    n_all = N_DEV * b_loc
    mod_cols = w_mod.shape[2]

    big = _pack([weights[n] for n, _ in _SHARDED], BF16)
    small = _pack([weights[n] for n, _ in _SHARDED_SMALL], F32, cols=LANE, row_mult=8)
    g_big, g_small = plane_allgather(big, small)
    full = {n: weights[n] for n in _REPLICATED}
    parts = [_unpack(g_big[j], [weights[n].shape for n, _ in _SHARDED]) for j in range(N_PLANE)]
    for t, (n, axis) in enumerate(_SHARDED):
        full[n] = [jnp.concatenate([parts[j][t][l] for j in range(N_PLANE)], axis=axis - 1)
                   for l in range(weights[n].shape[0])]
    parts_s = [_unpack(g_small[j], [weights[n].shape for n, _ in _SHARDED_SMALL]) for j in range(N_PLANE)]
    for t, (n, axis) in enumerate(_SHARDED_SMALL):
        full[n] = jnp.concatenate([parts_s[j][t] for j in range(N_PLANE)], axis=axis)

    rows_pad = 8 * ((n_all + 1 + 7) // 8)
    c_all = allgather8(jnp.pad(c, ((0, 8 - b_loc), (0, 0)))).reshape(N_DEV, 8, d)[:, :b_loc].reshape(n_all, d)
    cond_raw = jnp.concatenate([c_all, c_ctx[None], jnp.zeros((rows_pad - n_all - 1, d), F32)], axis=0)
    b_cols = lax.dynamic_slice_in_dim(b_mod, plane * mod_cols, mod_cols, axis=1)
    mod_loc = jnp.stack([_mm(cond_raw, w_mod[i], a_act="silu") + b_cols[i][None] for i in range(depth)])
    mod_g = allgather8(mod_loc.reshape(depth * rows_pad, mod_cols)).reshape(N_PLANE, 2, depth, rows_pad, mod_cols)
    mod_all = jnp.concatenate([mod_g[j, 0] for j in range(N_PLANE)], axis=-1)
    m_lat = lax.dynamic_slice_in_dim(mod_all, dev * b_loc, b_loc, axis=1)
    m_ctx = jnp.broadcast_to(mod_all[:, n_all][:, None], m_lat.shape)
    mods = jnp.stack([m_ctx, m_lat], axis=2).reshape(depth, b_loc, 2, N_MOD, d)

    loss_part, grad_x, dmods, dw = local_step(x, ctx, mods, full, loss_target)

    dm = dmods.reshape(depth, b_loc, 2, N_MOD * d)
    dm_rows = jnp.concatenate([dm[:, :, 1], jnp.sum(dm[:, :, 0], axis=1, keepdims=True)], axis=1)
    rep_shapes = [weights[n].shape for n in _REPLICATED] + [(1,)]
    small_pack = _pack([dm_rows] + [dw[n] for n in _REPLICATED] + [loss_part.reshape(1)], F32, cols=1024, row_mult=8)
    sp_rows = small_pack.shape[0]
    gathered = allgather8(small_pack).reshape(N_DEV, sp_rows, 1024)
    n_dm = depth * (b_loc + 1) * N_MOD * d
    dm_all = gathered.reshape(N_DEV, -1)[:, :n_dm].reshape(N_DEV, depth, b_loc + 1, N_MOD * d)
    rep_sum = _sum_rows(gathered, N_DEV).reshape(-1)
    rep_parts = _unpack(rep_sum[n_dm:], rep_shapes)
    rep_grads = dict(zip(_REPLICATED, rep_parts[:-1]))
    loss = rep_parts[-1][0]
    d_ctx_row = rep_sum[:n_dm].reshape(depth, b_loc + 1, N_MOD * d)[:, b_loc]
    d_lat_rows = jnp.transpose(dm_all[:, :, :b_loc], (1, 0, 2, 3)).reshape(depth, n_all, N_MOD * d)
    d_mod_all = jnp.concatenate([d_lat_rows, d_ctx_row[:, None],
                                 jnp.zeros((depth, rows_pad - n_all - 1, N_MOD * d), F32)], axis=1)
    grads = dict(rep_grads)
    grads["b_mod"] = jnp.sum(d_mod_all, axis=1)
    d_cols = lax.dynamic_slice_in_dim(d_mod_all, plane * mod_cols, mod_cols, axis=2)
    grads["w_mod"] = jnp.stack([_mm(cond_raw, d_cols[i], ta=True, a_act="silu") for i in range(depth)])
    d_cond = _mm(d_cols[0], w_mod[0], tb=True)
    for i in range(1, depth):
        d_cond = _add2(d_cond, _mm(d_cols[i], w_mod[i], tb=True))
    d_cond_g = allgather8(d_cond[n_all:n_all + 8] if rows_pad - n_all >= 8 else
                          jnp.pad(d_cond[n_all:], ((0, 8 - (rows_pad - n_all)), (0, 0)))).reshape(N_PLANE, 2, 8, d)
    d_silu = _sum_rows(d_cond_g[:, 0], N_PLANE)[0]
    sg = jax.nn.sigmoid(c_ctx)
    grads["c_ctx"] = d_silu * (sg * (1.0 + c_ctx * (1.0 - sg)))

    def shards_of(g, axis, j):
        layers = g if isinstance(g, (list, tuple)) else [g]
        ax = axis - 1 if isinstance(g, (list, tuple)) else axis
        n = layers[0].shape[ax] // N_PLANE
        return [lax.slice_in_dim(t, j * n, (j + 1) * n, axis=ax) for t in layers]

    send = jnp.stack([_pack([t for n, axis in _SHARDED + _SHARDED_SMALL for t in shards_of(dw[n], axis, j)], BF16)
                      for j in range(N_PLANE)])
    rows_h = send.shape[1] // 2
    send = send.reshape(N_PLANE, 2, rows_h, 1024)
    mine = lax.dynamic_index_in_dim(send, ac, 1, keepdims=False).reshape(N_PLANE * rows_h, 1024)
    theirs = sibling_halves(send).reshape(N_PLANE * rows_h, 1024)
    chip_sum = _accumulate([mine, theirs], BF16).reshape(N_PLANE, rows_h, 1024)
    pick = lambda k: lax.dynamic_index_in_dim(chip_sum, k, 0, keepdims=False)
    own, for_x, for_y, for_diag = pick(plane), pick(plane ^ 2), pick(plane ^ 1), pick(plane ^ 3)
    rows_q = rows_h // 2
    from_x, from_y = neighbour_exchange(for_diag[:rows_q], for_diag[rows_q:])
    zeros_q = jnp.zeros((rows_q, 1024), BF16)
    relayed = jnp.concatenate([zeros_q, from_y, from_x, zeros_q])
    merged = _accumulate([jnp.concatenate([for_x, for_y]), relayed], BF16)
    got_x, got_y = neighbour_exchange(merged[:rows_h], merged[rows_h:])
    done = _accumulate([own, got_x, got_y], BF16)
    both = jnp.stack([done, sibling_swap(done)])
    flat = jnp.where(ac == 0, both, both[::-1]).astype(F32).reshape(-1, 1024)
    shard_shapes = [weights[n].shape for n, _ in _SHARDED] + [weights[n].shape for n, _ in _SHARDED_SMALL]
    for (n, _), g in zip(_SHARDED + _SHARDED_SMALL, _unpack(flat, shard_shapes)):
        grads[n] = g

    big_names = ("w_mod",) + tuple(n for n, _ in _SHARDED)
    small_names = tuple(n for n in _WEIGHTS if n not in big_names)
    delta, new_m, new_v = {}, {}, {}
    for n in big_names:
        delta[n], new_m[n], new_v[n] = _adamw(weights[n], grads[n], mom_m[n], mom_v[n])
    sm_shapes = [weights[n].shape for n in small_names]
    packed = [_pack([src[n] for n in small_names], F32, cols=1024, row_mult=8)
              for src in (weights, grads, mom_m, mom_v)]
    for dst, res in zip((delta, new_m, new_v), _adamw(*packed)):
        dst.update(dict(zip(small_names, _unpack(res, sm_shapes))))

    return (loss, grad_x, *[grads[n] for n in _WEIGHTS], *[delta[n] for n in _WEIGHTS],
            *[new_m[n] for n in _WEIGHTS], *[new_v[n] for n in _WEIGHTS])
```

```python
import functools

import numpy as np
import jax
import jax.numpy as jnp
from jax import lax
from jax.experimental import pallas as pl
from jax.experimental.pallas import tpu as pltpu

F32 = jnp.float32
BF16 = jnp.bfloat16
HI = lax.Precision.HIGHEST
MESH = pl.DeviceIdType.MESH
ANY = pl.BlockSpec(memory_space=pl.ANY)
VMEM_SPEC = pl.BlockSpec(memory_space=pltpu.VMEM)

GRID_W = 64
HEAD_DIM = 64
ROPE_BASE = 10000.0
EPS = 1e-6
N_MOD = 6
GQA_Q_HEADS, GQA_KV_HEADS = 12, 4
GQA_Q_W, GQA_KV_W = GQA_Q_HEADS * HEAD_DIM, GQA_KV_HEADS * HEAD_DIM
SSM_WIDTH, SSM_GROUP, SSM_STATE = 256, 16, 64
SSM_GROUPS = SSM_WIDTH // SSM_GROUP
SSM_LANES = SSM_GROUPS * SSM_STATE
MLA_HEADS, MLA_Q_RANK, MLA_KV_RANK, MLA_NOPE, MLA_ROPE, MLA_V = 8, 512, 256, 64, 32, 64
MLA_QK = MLA_NOPE + MLA_ROPE
NA_HEADS, NA_WIN_R, NA_WIN_C = 8, 8, 16
NA_W = NA_HEADS * HEAD_DIM
NA_BAND = NA_WIN_R * GRID_W
ODD_IN_W = MLA_Q_RANK + MLA_KV_RANK + MLA_ROPE + 3 * NA_W
ODD_IN_PAD = 2560
ADAM_LR, ADAM_B1, ADAM_B2, ADAM_EPS, ADAM_WD, ADAM_STEP = 0.001, 0.9, 0.999, 1e-08, 0.01, 10
NEG = -1e30
VMEM_LIMIT = 56 * 1024 * 1024
LANE = 128
MM_TILE_M = (1152, 1024, 768, 512, 256, 128)
MM_TILE_N = (1280, 1024, 768, 512, 256, 128)
MM_TILE_K = (1152, 1024, 768, 512, 256, 128)
ROW_TILES = (576, 512, 384, 256, 128, 64)
N_PLANE = 4
N_DEV = 8


def _pick(n, cands):
    for c in cands:
        if n % c == 0:
            return c
    return n


def _params(**kw):
    return pltpu.CompilerParams(vmem_limit_bytes=VMEM_LIMIT, **kw)


def _mm(a, b, *, ta=False, tb=False, a_act=None, epi=None, e=None, exact=False, out_dtype=F32):
    m, kd = (a.shape[1], a.shape[0]) if ta else a.shape
    n = b.shape[0] if tb else b.shape[1]
    tm = _pick(m, MM_TILE_M)
    tn = _pick(n, MM_TILE_N)
    tk = _pick(kd, MM_TILE_K)
    nk = kd // tk
    dn = (((0 if ta else 1,), (1 if tb else 0,)), ((), ()))
    narrow = jnp.dtype(out_dtype) != jnp.dtype(F32)
    assert not (narrow and epi is not None)

    def body(*refs):
        if narrow:
            a_ref, b_ref, out_ref, o_ref = refs
        elif epi is None:
            a_ref, b_ref, o_ref = refs
        else:
            a_ref, b_ref, e_ref, o_ref = refs
        k = pl.program_id(2)
        av = a_ref[...]
        if a_act == "relu2":
            av = jnp.square(jnp.maximum(av, 0.0))
        elif a_act == "silu":
            av = av * jax.nn.sigmoid(av)
        bv = b_ref[...]
        if exact:
            p = lax.dot_general(av, bv, dn, precision=HI, preferred_element_type=F32)
        else:
            p = lax.dot_general(av.astype(BF16), bv.astype(BF16), dn, preferred_element_type=F32)

        @pl.when(k == 0)
        def _():
            o_ref[...] = p

        @pl.when(k > 0)
        def _():
            o_ref[...] += p

        if epi == "drelu2":
            @pl.when(k == nk - 1)
            def _():
                o_ref[...] = o_ref[...] * (2.0 * jnp.maximum(e_ref[...], 0.0))

        if narrow:
            @pl.when(k == nk - 1)
            def _():
                out_ref[...] = o_ref[...].astype(out_dtype)

    a_spec = pl.BlockSpec((tk, tm), lambda i, j, k: (k, i)) if ta else pl.BlockSpec((tm, tk), lambda i, j, k: (i, k))
    b_spec = pl.BlockSpec((tn, tk), lambda i, j, k: (j, k)) if tb else pl.BlockSpec((tk, tn), lambda i, j, k: (k, j))
    o_spec = pl.BlockSpec((tm, tn), lambda i, j, k: (i, j))
    ins, specs = [a, b], [a_spec, b_spec]
    if epi is not None:
        ins.append(e)
        specs.append(o_spec)
    name = f"mm_{m}x{kd}x{n}_{int(ta)}{int(tb)}_{a_act}_{epi}_{int(exact)}_{jnp.dtype(out_dtype).name}"
    return pl.pallas_call(
        body, out_shape=jax.ShapeDtypeStruct((m, n), out_dtype), grid=(m // tm, n // tn, nk),
        in_specs=specs, out_specs=o_spec, name=name, compiler_params=_params(),
        scratch_shapes=[pltpu.VMEM((tm, tn), F32)] if narrow else [],
    )(*ins)


@functools.partial(jax.custom_vjp, nondiff_argnums=(2,))
def _linear(a, w, exact):
    return _mm(a, w, exact=exact)


def _linear_fwd(a, w, exact):
    return _mm(a, w, exact=exact), (a, w)


def _linear_bwd(exact, res, g):
    a, w = res
    return _mm(g, w, tb=True, exact=exact), _mm(a, g, ta=True, exact=exact, out_dtype=w.dtype)


_linear.defvjp(_linear_fwd, _linear_bwd)


def linear(a, w, exact=False):
    return _linear(a, w, exact)


@jax.custom_vjp
def ffn(a, w1, w2):
    return _mm(_mm(a, w1), w2, a_act="relu2")


def _ffn_fwd(a, w1, w2):
    h1 = _mm(a, w1)
    return _mm(h1, w2, a_act="relu2"), (a, w1, w2, h1)


def _ffn_bwd(res, g):
    a, w1, w2, h1 = res
    dh1 = _mm(g, w2, tb=True, epi="drelu2", e=h1)
    dw2 = _mm(h1, g, ta=True, a_act="relu2", out_dtype=w2.dtype)
    return _mm(dh1, w1, tb=True), _mm(a, dh1, ta=True, out_dtype=w1.dtype), dw2


ffn.defvjp(_ffn_fwd, _ffn_bwd)


def make_rowwise(fn, name, kinds, out_dims, nctx_rows=0, whole_seq=False):
    n_in = len(kinds)
    n_out = len(out_dims)
    diff = [i for i, kd in enumerate(kinds) if kd in ("row", "glob", "seg")]
    seg_idx = [i for i, kd in enumerate(kinds) if kd == "seg"]

    def layout(args):
        row0 = args[kinds.index("row")]
        g, s = row0.shape[0], row0.shape[1]
        ts = s if whole_seq else _pick(s, ROW_TILES)
        return g, s, ts, 0

    def spec_of(kind, arr, ts, nctx):
        if kind == "row":
            return pl.BlockSpec((None, ts, arr.shape[2]), lambda g, i: (g, i, 0))
        if kind == "tab":
            return pl.BlockSpec((ts, arr.shape[1]), lambda g, i: (i, 0))
        if kind in ("const", "glob"):
            return pl.BlockSpec(arr.shape, lambda g, i: (0, 0))
        return pl.BlockSpec((None,) + arr.shape[1:], lambda g, i: (g, 0, 0, 0))

    def with_segments(ts):
        if not seg_idx:
            return fn

        def wrapped(*vals):
            rows = pl.program_id(1) * ts + lax.broadcasted_iota(jnp.int32, (ts, 1), 0)
            vals = list(vals)
            for idx in seg_idx:
                vals[idx] = jnp.where(rows < nctx_rows, vals[idx][0], vals[idx][1])
            return fn(*vals)

        return wrapped

    def fwd_call(*args):
        g, s, ts, nctx = layout(args)
        fn = with_segments(ts)

        def body(*refs):
            vals = [r[...] for r in refs[:n_in]]
            outs = fn(*vals)
            for o_ref, o in zip(refs[n_in:], outs):
                o_ref[...] = o

        return pl.pallas_call(
            body, out_shape=[jax.ShapeDtypeStruct((g, s, d), F32) for d in out_dims], grid=(g, s // ts),
            in_specs=[spec_of(kd, a, ts, nctx) for kd, a in zip(kinds, args)],
            out_specs=[pl.BlockSpec((None, ts, d), lambda g_, i: (g_, i, 0)) for d in out_dims],
            name=f"{name}_f_{g}x{s}", compiler_params=_params(),
        )(*args)

    def bwd_call(args, cts):
        g, s, ts, nctx = layout(args)
        fn = with_segments(ts)

        def body(*refs):
            in_refs, ct_refs, out_refs = refs[:n_in], refs[n_in:n_in + n_out], refs[n_in + n_out:]
            gi, i = pl.program_id(0), pl.program_id(1)
            vals = [r[...] for r in in_refs]

            def f(*dv):
                full = list(vals)
                for idx, v in zip(diff, dv):
                    full[idx] = v
                return tuple(fn(*full))

            _, vjp = jax.vjp(f, *[vals[idx] for idx in diff])
            grads = vjp(tuple(r[...] for r in ct_refs))
            for idx, o_ref, gr in zip(diff, out_refs, grads):
                if kinds[idx] == "row":
                    o_ref[...] = gr
                    continue
                if kinds[idx] == "glob":
                    first = jnp.logical_and(gi == 0, i == 0)
                else:
                    first = i == 0

                @pl.when(first)
                def _(o_ref=o_ref, gr=gr):
                    o_ref[...] = gr

                @pl.when(jnp.logical_not(first))
                def _(o_ref=o_ref, gr=gr):
                    o_ref[...] += gr

        in_specs = [spec_of(kd, a, ts, nctx) for kd, a in zip(kinds, args)]
        in_specs += [pl.BlockSpec((None, ts, d), lambda g_, i: (g_, i, 0)) for d in out_dims]
        return pl.pallas_call(
            body, out_shape=[jax.ShapeDtypeStruct(args[idx].shape, F32) for idx in diff], grid=(g, s // ts),
            in_specs=in_specs, out_specs=[spec_of(kinds[idx], args[idx], ts, nctx) for idx in diff],
            name=f"{name}_b_{g}x{s}", compiler_params=_params(),
        )(*args, *cts)

    @jax.custom_vjp
    def op(*args):
        return tuple(fwd_call(*args))

    def op_fwd(*args):
        return tuple(fwd_call(*args)), args

    def op_bwd(args, cts):
        grads = bwd_call(args, cts)
        full = [None] * n_in
        for idx, gr in zip(diff, grads):
            full[idx] = gr
        return tuple(jnp.zeros_like(a) if gfull is None else gfull for a, gfull in zip(args, full))

    op.defvjp(op_fwd, op_bwd)
    op.fwd_call, op.bwd_call = fwd_call, bwd_call
    return op


def make_modulate(d, n_ctx):
    one = make_rowwise(_fn_modulate, "modulate", ("row", "glob", "seg", "seg"), (d,), nctx_rows=n_ctx)
    two = make_rowwise(_fn_modulate_keep, "modulate_keep", ("row", "glob", "seg", "seg"), (d, d), nctx_rows=n_ctx)

    @jax.custom_vjp
    def op(x, g, shift, scale):
        return one.fwd_call(x, g, shift, scale)[0], x

    def fwd(x, g, shift, scale):
        return (one.fwd_call(x, g, shift, scale)[0], x), (x, g, shift, scale)

    def bwd(res, cts):
        return tuple(two.bwd_call(res, cts))

    op.defvjp(fwd, bwd)
    return op


def make_gated_add(d, n_ctx):
    add = make_rowwise(_fn_gated_add, "gated", ("row", "row", "seg"), (d,), nctx_rows=n_ctx)
    mul = make_rowwise(_fn_gate_mul, "gate_mul", ("row", "seg"), (d,), nctx_rows=n_ctx)

    @jax.custom_vjp
    def op(x, o, gate):
        return add.fwd_call(x, o, gate)[0]

    def fwd(x, o, gate):
        return add.fwd_call(x, o, gate)[0], (o, gate)

    def bwd(res, ct):
        do, dgate = mul.bwd_call(res, (ct,))
        return ct, do, dgate

    op.defvjp(fwd, bwd)
    return op


def _rms(x):
    return lax.rsqrt(jnp.mean(x * x, axis=-1, keepdims=True) + EPS)


def _fn_modulate(x, g, shift, scale):
    return ((x * _rms(x) * g) * (1.0 + scale) + shift,)


def _fn_modulate_keep(x, g, shift, scale):
    return _fn_modulate(x, g, shift, scale) + (x,)


def _fn_gated_add(x, o, gate):
    return (x + gate * o,)


def _fn_gate_mul(o, gate):
    return (gate * o,)


def _fn_norm(x, g):
    return (x * _rms(x) * g,)


def _fn_glu_pre(u, y0, y1, d):
    return (jax.nn.gelu(d * u + y0 + y1),)


def _fn_glu_post(z, t, bg):
    return (z * jax.nn.sigmoid(t + bg),)


def _rope_tables(n_ctx, n_lat, dh, start, rot_dim):
    t = jnp.arange(n_lat)
    rows = (t // GRID_W).astype(F32)
    cols = (t % GRID_W).astype(F32)
    axis_dim = rot_dim // 2
    freqs = ROPE_BASE ** (-jnp.arange(0, axis_dim, 2, dtype=F32) / axis_dim)
    ang_r = rows[:, None] * freqs
    ang_c = cols[:, None] * freqs
    ang = jnp.concatenate([ang_r, ang_r, ang_c, ang_c], axis=-1)
    cos = jnp.concatenate([jnp.ones((n_lat, start), F32), jnp.cos(ang)], axis=-1)
    sin = jnp.concatenate([jnp.zeros((n_lat, start), F32), jnp.sin(ang)], axis=-1)
    cos = jnp.concatenate([jnp.ones((n_ctx, dh), F32), cos], axis=0)
    sin = jnp.concatenate([jnp.zeros((n_ctx, dh), F32), sin], axis=0)
    return cos, sin


_NT = (((1,), (1,)), ((), ()))
_TN = (((0,), (0,)), ((), ()))


def _na_geometry(i, nc, rows):
    r = i - nc
    rs = jnp.clip(r - NA_WIN_R // 2, 0, rows - NA_WIN_R)
    is_ctx = i < nc
    cls = jnp.where(is_ctx, NA_WIN_R, r - rs)
    return jnp.where(is_ctx, 0, rs), cls


def _na_onehots():
    q = np.arange(GRID_W)[:, None]
    col = np.arange(GRID_W)[None, :]
    cs = np.clip(q - NA_WIN_C // 2, 0, GRID_W - NA_WIN_C)
    valid = (col >= cs) & (col < cs + NA_WIN_C)
    cidx = col - q + (NA_WIN_C - 1)
    n_b = 2 * NA_WIN_C - 1
    col_hot = np.zeros((LANE, GRID_W * GRID_W), np.float32)
    for qq in range(GRID_W):
        for cc in range(GRID_W):
            if valid[qq, cc]:
                col_hot[cidx[qq, cc], qq * GRID_W + cc] = 1.0
    row_hot = np.zeros((NA_WIN_R, NA_WIN_R, 2 * NA_WIN_R - 1), np.float32)
    for c in range(NA_WIN_R):
        for j in range(NA_WIN_R):
            row_hot[c, j, j - c + NA_WIN_R - 1] = 1.0
    mask = np.where(valid, 0.0, NEG).astype(np.float32)
    return col_hot, row_hot, mask, n_b


def na_bias_table(rpb):
    h = rpb.shape[0]
    col_hot, row_hot, mask, n_b = _na_onehots()
    t1 = jnp.einsum("cja,hab->hcjb", jnp.asarray(row_hot), rpb)
    t1 = jnp.pad(t1.reshape(h * NA_WIN_R * NA_WIN_R, n_b), ((0, 0), (0, LANE - n_b)))
    t2 = linear(t1, jnp.asarray(col_hot), True)
    t2 = t2.reshape(h, NA_WIN_R, NA_WIN_R, GRID_W, GRID_W) + jnp.asarray(mask)
    tab = jnp.transpose(t2, (0, 1, 3, 2, 4)).reshape(h, NA_WIN_R, GRID_W, NA_BAND)
    return jnp.concatenate([tab, jnp.full((h, 1, GRID_W, NA_BAND), NEG, F32)], axis=1)


def _first_step():
    return jnp.logical_and(pl.program_id(0) == 0, pl.program_id(1) == 0)


def _accum_out(ref, val, first):
    @pl.when(first)
    def _():
        ref[...] = val

    @pl.when(jnp.logical_not(first))
    def _():
        ref[...] += val


def _norm_head(xh, g):
    r = _rms(xh)
    yn = xh * r
    return yn * g, yn, r


def _norm_head_bwd(dy, yn, r, g):
    dg = jnp.sum(dy * yn, axis=0, keepdims=True)
    dyn = dy * g
    return r * (dyn - yn * jnp.mean(dyn * yn, axis=-1, keepdims=True)), dg


def _rope_signs(dh, start, rot_dim, n_heads):
    q = rot_dim // 4
    pos = np.arange(dh)
    quarter = (pos - start) // q
    inr = pos >= start
    sg = np.zeros((8, n_heads * dh), np.float32)
    sg[0] = np.tile(np.where(inr & (quarter % 2 == 0), -1.0, 0.0), n_heads)
    sg[1] = np.tile(np.where(inr & (quarter % 2 == 1), 1.0, 0.0), n_heads)
    return sg


def _rope_full(y, cos, sin, sg, q):
    w = y.shape[-1]
    rot = sg[0:1] * pltpu.roll(y, w - q, 1) + sg[1:2] * pltpu.roll(y, q, 1)
    return y * cos + rot * sin


def _rope_full_t(dy, cos, sin, sg, q):
    w = dy.shape[-1]
    z = dy * sin
    return dy * cos - sg[1:2] * pltpu.roll(z, q, 1) - sg[0:1] * pltpu.roll(z, w - q, 1)


def _hnr_call(x, g, cos, sin, sg, n_heads, q, dy=None):
    b, s, w = x.shape
    dh = w // n_heads
    ts = _pick(s, ROW_TILES)
    rope = cos is not None

    def body(*refs):
        refs = list(refs)
        x_ref, g_ref = refs[0], refs[1]
        k = 2
        if rope:
            cos_ref, sin_ref, sg_ref = refs[2], refs[3], refs[4]
            k = 5
        gv = g_ref[...]
        if dy is None:
            o_ref = refs[k]
            for h in range(n_heads):
                sl = slice(h * dh, (h + 1) * dh)
                o_ref[:, sl] = _norm_head(x_ref[:, sl], gv)[0]
            if rope:
                o_ref[...] = _rope_full(o_ref[...], cos_ref[...], sin_ref[...], sg_ref[...], q)
            return
        dy_ref, dx_ref, dg_ref = refs[k], refs[k + 1], refs[k + 2]
        src = dy_ref
        if rope:
            dx_ref[...] = _rope_full_t(dy_ref[...], cos_ref[...], sin_ref[...], sg_ref[...], q)
            src = dx_ref
        dg = jnp.zeros((1, dh), F32)
        for h in range(n_heads):
            sl = slice(h * dh, (h + 1) * dh)
            _, yn, r = _norm_head(x_ref[:, sl], gv)
            dxh, dgh = _norm_head_bwd(src[:, sl], yn, r, gv)
            dx_ref[:, sl] = dxh
            dg = dg + dgh
        _accum_out(dg_ref, dg, _first_step())

    row = pl.BlockSpec((None, ts, w), lambda bi, i: (bi, i, 0))
    whole = lambda a: pl.BlockSpec(a.shape, lambda bi, i: (0, 0))
    ins, specs = [x, g], [row, whole(g)]
    if rope:
        ins += [cos, sin, sg]
        specs += [pl.BlockSpec((ts, w), lambda bi, i: (i, 0)), pl.BlockSpec((ts, w), lambda bi, i: (i, 0)), whole(sg)]
    if dy is None:
        out_shape, out_specs = jax.ShapeDtypeStruct(x.shape, F32), row
    else:
        ins.append(dy)
        specs.append(row)
        out_shape = [jax.ShapeDtypeStruct(x.shape, F32), jax.ShapeDtypeStruct(g.shape, F32)]
        out_specs = [row, whole(g)]
    return pl.pallas_call(
        body, out_shape=out_shape, grid=(b, s // ts), in_specs=specs, out_specs=out_specs,
        name=f"hnr_{'b' if dy is not None else 'f'}_{n_heads}x{dh}_{int(rope)}", compiler_params=_params(),
    )(*ins)


@functools.partial(jax.custom_vjp, nondiff_argnums=(5, 6))
def head_norm_rope(x, g, cos, sin, sg, n_heads, q):
    return _hnr_call(x, g, cos, sin, sg, n_heads, q)


def _head_norm_rope_fwd(x, g, cos, sin, sg, n_heads, q):
    return _hnr_call(x, g, cos, sin, sg, n_heads, q), (x, g, cos, sin, sg)


def _head_norm_rope_bwd(n_heads, q, res, dy):
    x, g, cos, sin, sg = res
    dx, dg = _hnr_call(x, g, cos, sin, sg, n_heads, q, dy=dy)
    zero = lambda t: None if t is None else jnp.zeros_like(t)
    return dx, dg, zero(cos), zero(sin), zero(sg)


head_norm_rope.defvjp(_head_norm_rope_fwd, _head_norm_rope_bwd)


def _mla_k_call(kv, kr, g, cos, sin, sg, dkn=None):
    b, s, _ = kv.shape
    ts = _pick(s, ROW_TILES)
    hw = MLA_NOPE + MLA_V
    kn_w = MLA_HEADS * MLA_QK
    q = MLA_ROPE // 4

    def body(kv_ref, kr_ref, g_ref, cos_ref, sin_ref, sg_ref, *rest):
        gv = g_ref[...]
        krv = kr_ref[...]
        if dkn is None:
            (o_ref,) = rest
            for h in range(MLA_HEADS):
                kh = jnp.concatenate([kv_ref[:, h * hw:h * hw + MLA_NOPE], krv], axis=-1)
                o_ref[:, h * MLA_QK:(h + 1) * MLA_QK] = _norm_head(kh, gv)[0]
            o_ref[...] = _rope_full(o_ref[...], cos_ref[...], sin_ref[...], sg_ref[...], q)
            return
        dkn_ref, dkv_ref, dkr_ref, dg_ref, dy_ref = rest
        dy_ref[...] = _rope_full_t(dkn_ref[...], cos_ref[...], sin_ref[...], sg_ref[...], q)
        dg = jnp.zeros((1, MLA_QK), F32)
        dkr = jnp.zeros((ts, MLA_ROPE), F32)
        for h in range(MLA_HEADS):
            kh = jnp.concatenate([kv_ref[:, h * hw:h * hw + MLA_NOPE], krv], axis=-1)
            _, yn, r = _norm_head(kh, gv)
            dxh, dgh = _norm_head_bwd(dy_ref[:, h * MLA_QK:(h + 1) * MLA_QK], yn, r, gv)
            dkv_ref[:, h * hw:h * hw + MLA_NOPE] = dxh[:, :MLA_NOPE]
            dkv_ref[:, h * hw + MLA_NOPE:(h + 1) * hw] = jnp.zeros((ts, MLA_V), F32)
            dkr = dkr + dxh[:, MLA_NOPE:]
            dg = dg + dgh
        dkr_ref[...] = dkr
        _accum_out(dg_ref, dg, _first_step())

    row = lambda w: pl.BlockSpec((None, ts, w), lambda bi, i: (bi, i, 0))
    tab = pl.BlockSpec((ts, kn_w), lambda bi, i: (i, 0))
    whole = lambda a: pl.BlockSpec(a.shape, lambda bi, i: (0, 0))
    ins = [kv, kr, g, cos, sin, sg]
    specs = [row(kv.shape[2]), row(MLA_ROPE), whole(g), tab, tab, whole(sg)]
    scratch = []
    if dkn is None:
        out_shape, out_specs = jax.ShapeDtypeStruct((b, s, kn_w), F32), row(kn_w)
    else:
        ins.append(dkn)
        specs.append(row(kn_w))
        out_shape = [jax.ShapeDtypeStruct(kv.shape, F32), jax.ShapeDtypeStruct(kr.shape, F32),
                     jax.ShapeDtypeStruct(g.shape, F32)]
        out_specs = [row(kv.shape[2]), row(MLA_ROPE), whole(g)]
        scratch = [pltpu.VMEM((ts, kn_w), F32)]
    return pl.pallas_call(
        body, out_shape=out_shape, grid=(b, s // ts), in_specs=specs, out_specs=out_specs, scratch_shapes=scratch,
        name=f"mla_k_{'b' if dkn is not None else 'f'}", compiler_params=_params(),
    )(*ins)


@jax.custom_vjp
def mla_k_prep(kv, kr, g, cos, sin, sg):
    return _mla_k_call(kv, kr, g, cos, sin, sg)


def _mla_k_prep_fwd(kv, kr, g, cos, sin, sg):
    return _mla_k_call(kv, kr, g, cos, sin, sg), (kv, kr, g, cos, sin, sg)


def _mla_k_prep_bwd(res, dkn):
    kv, kr, g, cos, sin, sg = res
    dkv, dkr, dg = _mla_k_call(kv, kr, g, cos, sin, sg, dkn=dkn)
    return dkv, dkr, dg, jnp.zeros_like(cos), jnp.zeros_like(sin), jnp.zeros_like(sg)


mla_k_prep.defvjp(_mla_k_prep_fwd, _mla_k_prep_bwd)


class _HeadLayout:
    def __init__(self, groups, dq, dv, q_off, k_off, v_off, o_off, wq, wk, wv, wo, scale):
        self.groups, self.dq, self.dv, self.scale = groups, dq, dv, scale
        self.q_off, self.k_off, self.v_off, self.o_off = q_off, k_off, v_off, o_off
        self.wq, self.wk, self.wv, self.wo = wq, wk, wv, wo
        self.n_h = len(q_off)


def _gqa_layout():
    rep = GQA_Q_HEADS // GQA_KV_HEADS
    n_h = GQA_Q_HEADS // 2
    return _HeadLayout(2, HEAD_DIM, HEAD_DIM, [h * HEAD_DIM for h in range(n_h)], [(h // rep) * HEAD_DIM for h in range(n_h)],
                       [(h // rep) * HEAD_DIM for h in range(n_h)], [h * HEAD_DIM for h in range(n_h)],
                       n_h * HEAD_DIM, (n_h // rep) * HEAD_DIM, (n_h // rep) * HEAD_DIM, n_h * HEAD_DIM, HEAD_DIM ** -0.5)


def _mla_layout():
    n_h = MLA_HEADS // 2
    hw = MLA_NOPE + MLA_V
    return _HeadLayout(2, MLA_QK, MLA_V, [h * MLA_QK for h in range(n_h)], [h * MLA_QK for h in range(n_h)],
                       [h * hw + MLA_NOPE for h in range(n_h)], [h * MLA_V for h in range(n_h)],
                       n_h * MLA_QK, n_h * MLA_QK, n_h * hw, n_h * MLA_V, MLA_QK ** -0.5)


def _attn_tm_fwd(q, k, v, lay, n_ctx):
    b, s, _ = q.shape
    tq = min(256, n_ctx)
    nc = n_ctx // tq

    def body(q_ref, k_ref, v_ref, o_ref, lse_ref):
        def run(n_keys):
            for h in range(lay.n_h):
                qo, ko, vo, oo = lay.q_off[h], lay.k_off[h], lay.v_off[h], lay.o_off[h]
                qv = (q_ref[:, qo:qo + lay.dq] * lay.scale).astype(BF16)
                sc = lax.dot_general(qv, k_ref[0:n_keys, ko:ko + lay.dq].astype(BF16), _NT, preferred_element_type=F32)
                m = jnp.max(sc, axis=-1, keepdims=True)
                p = jnp.exp(sc - m)
                l = jnp.sum(p, axis=-1, keepdims=True)
                o = jnp.dot(p.astype(BF16), v_ref[0:n_keys, vo:vo + lay.dv].astype(BF16), preferred_element_type=F32)
                o_ref[:, oo:oo + lay.dv] = o / l
                lse_ref[:, h:h + 1] = m + jnp.log(l)

        pl.when(pl.program_id(2) < nc)(lambda: run(n_ctx))
        pl.when(pl.program_id(2) >= nc)(lambda: run(s))

    return pl.pallas_call(
        body, out_shape=[jax.ShapeDtypeStruct((b, s, lay.groups * lay.wo), F32),
                         jax.ShapeDtypeStruct((b, lay.groups, s, lay.n_h), F32)],
        grid=(b, lay.groups, s // tq),
        in_specs=[pl.BlockSpec((None, tq, lay.wq), lambda bi, g, i: (bi, i, g)),
                  pl.BlockSpec((None, s, lay.wk), lambda bi, g, i: (bi, 0, g)),
                  pl.BlockSpec((None, s, lay.wv), lambda bi, g, i: (bi, 0, g))],
        out_specs=[pl.BlockSpec((None, tq, lay.wo), lambda bi, g, i: (bi, i, g)),
                   pl.BlockSpec((None, None, tq, lay.n_h), lambda bi, g, i: (bi, g, i, 0))],
        name=f"attn_tm_f_{lay.dq}", compiler_params=_params(),
    )(q, k, v)


def _attn_tm_bwd(q, k, v, lse, o, do, lay, n_ctx):
    b, s, _ = q.shape
    tk = min(256, n_ctx)
    nc = n_ctx // tk

    def body(q_ref, k_ref, v_ref, lse_ref, o_ref, do_ref, dq_ref, dk_ref, dv_ref, delta_ref):
        @pl.when(pl.program_id(2) == 0)
        def _():
            dq_ref[...] = jnp.zeros_like(dq_ref)
            for h in range(lay.n_h):
                oo = lay.o_off[h]
                delta_ref[:, h:h + 1] = jnp.sum(o_ref[:, oo:oo + lay.dv] * do_ref[:, oo:oo + lay.dv], axis=-1,
                                                keepdims=True)

        def run(r0):
            dk_acc, dv_acc = {}, {}
            for h in range(lay.n_h):
                qo, ko, vo, oo = lay.q_off[h], lay.k_off[h], lay.v_off[h], lay.o_off[h]
                kh = k_ref[:, ko:ko + lay.dq].astype(BF16)
                vh = v_ref[:, vo:vo + lay.dv].astype(BF16)
                qv = (q_ref[r0:s, qo:qo + lay.dq] * lay.scale).astype(BF16)
                dob = do_ref[r0:s, oo:oo + lay.dv].astype(BF16)
                sc = lax.dot_general(qv, kh, _NT, preferred_element_type=F32)
                p = jnp.exp(sc - lse_ref[r0:s, h:h + 1])
                dvh = lax.dot_general(p.astype(BF16), dob, _TN, preferred_element_type=F32)
                dp = lax.dot_general(dob, vh, _NT, preferred_element_type=F32)
                dsb = (p * (dp - delta_ref[r0:s, h:h + 1])).astype(BF16)
                dkh = lax.dot_general(dsb, qv, _TN, preferred_element_type=F32)
                dq_ref[r0:s, qo:qo + lay.dq] += jnp.dot(dsb, kh, preferred_element_type=F32) * lay.scale
                dk_acc[ko] = dkh if ko not in dk_acc else dk_acc[ko] + dkh
                dv_acc[vo] = dvh if vo not in dv_acc else dv_acc[vo] + dvh
            if len(dv_acc) * lay.dv != lay.wv:
                dv_ref[...] = jnp.zeros_like(dv_ref)
            for ko, val in dk_acc.items():
                dk_ref[:, ko:ko + lay.dq] = val
            for vo, val in dv_acc.items():
                dv_ref[:, vo:vo + lay.dv] = val

        pl.when(pl.program_id(2) < nc)(lambda: run(0))
        pl.when(pl.program_id(2) >= nc)(lambda: run(n_ctx))

    full = lambda w: pl.BlockSpec((None, s, w), lambda bi, g, j: (bi, 0, g))
    blk = lambda w: pl.BlockSpec((None, tk, w), lambda bi, g, j: (bi, j, g))
    stat = pl.BlockSpec((None, None, s, lay.n_h), lambda bi, g, j: (bi, g, 0, 0))
    return pl.pallas_call(
        body, out_shape=[jax.ShapeDtypeStruct(q.shape, F32), jax.ShapeDtypeStruct(k.shape, F32),
                         jax.ShapeDtypeStruct(v.shape, F32)],
        grid=(b, lay.groups, s // tk),
        in_specs=[full(lay.wq), blk(lay.wk), blk(lay.wv), stat, full(lay.wo), full(lay.wo)],
        out_specs=[full(lay.wq), blk(lay.wk), blk(lay.wv)],
        scratch_shapes=[pltpu.VMEM((s, lay.n_h), F32)],
        name=f"attn_tm_b_{lay.dq}", compiler_params=_params(),
    )(q, k, v, lse, o, do)


def _make_attention_tm(lay):
    @functools.partial(jax.custom_vjp, nondiff_argnums=(3,))
    def op(q, k, v, n_ctx):
        return _attn_tm_fwd(q, k, v, lay, n_ctx)[0]

    def fwd(q, k, v, n_ctx):
        o, lse = _attn_tm_fwd(q, k, v, lay, n_ctx)
        return o, (q, k, v, o, lse)

    def bwd(n_ctx, res, do):
        q, k, v, o, lse = res
        return _attn_tm_bwd(q, k, v, lse, o, do, lay, n_ctx)

    op.defvjp(fwd, bwd)
    return op


gqa_attention = _make_attention_tm(_gqa_layout())
mla_attention = _make_attention_tm(_mla_layout())

NA_GROUPS_FWD = 1
NA_GROUPS_BWD = 2


def _na_tm_specs(s, nc, rows, groups):
    hg = NA_HEADS // groups
    w = hg * HEAD_DIM
    qs = pl.BlockSpec((None, GRID_W, w), lambda bi, g, i: (bi, i, g))
    ks = pl.BlockSpec((None, s, w), lambda bi, g, i: (bi, 0, g))
    bs = pl.BlockSpec((hg, None, GRID_W, NA_BAND), lambda bi, g, i: (g, _na_geometry(i, nc, rows)[1], 0, 0))
    ls = pl.BlockSpec((None, None, GRID_W, hg), lambda bi, g, i: (bi, g, i, 0))
    return hg, w, qs, ks, bs, ls


def _na_tm_scores(q_ref, k_ref, bias_ref, hd, n_ctx, start, scale):
    sl = slice(hd * HEAD_DIM, (hd + 1) * HEAD_DIM)
    qv = (q_ref[:, sl] * scale).astype(BF16)
    kc = k_ref[0:n_ctx, sl].astype(BF16)
    kb = k_ref[pl.ds(start, NA_BAND), sl].astype(BF16)
    s_c = lax.dot_general(qv, kc, _NT, preferred_element_type=F32)
    s_l = lax.dot_general(qv, kb, _NT, preferred_element_type=F32) + bias_ref[hd]
    return sl, qv, kc, kb, s_c, s_l


def _na_tm_fwd(q, k, v, bias, n_ctx):
    b, s, _ = q.shape
    nc = n_ctx // GRID_W
    rows = (s - n_ctx) // GRID_W
    scale = HEAD_DIM ** -0.5
    hg, w, qs, ks, bs, ls = _na_tm_specs(s, nc, rows, NA_GROUPS_FWD)

    def body(q_ref, k_ref, v_ref, bias_ref, o_ref, lse_ref):
        rs, _ = _na_geometry(pl.program_id(2), nc, rows)
        start = pl.multiple_of(n_ctx + rs * GRID_W, GRID_W)
        for hd in range(hg):
            sl, _, _, _, s_c, s_l = _na_tm_scores(q_ref, k_ref, bias_ref, hd, n_ctx, start, scale)
            m = jnp.maximum(jnp.max(s_c, axis=-1, keepdims=True), jnp.max(s_l, axis=-1, keepdims=True))
            p_c = jnp.exp(s_c - m)
            p_l = jnp.exp(s_l - m)
            l = jnp.sum(p_c, axis=-1, keepdims=True) + jnp.sum(p_l, axis=-1, keepdims=True)
            o = jnp.dot(p_c.astype(BF16), v_ref[0:n_ctx, sl].astype(BF16), preferred_element_type=F32)
            o = o + jnp.dot(p_l.astype(BF16), v_ref[pl.ds(start, NA_BAND), sl].astype(BF16), preferred_element_type=F32)
            o_ref[:, sl] = o / l
            lse_ref[:, hd:hd + 1] = m + jnp.log(l)

    return pl.pallas_call(
        body, out_shape=[jax.ShapeDtypeStruct(q.shape, F32), jax.ShapeDtypeStruct((b, NA_GROUPS_FWD, s, hg), F32)],
        grid=(b, NA_GROUPS_FWD, s // GRID_W), in_specs=[qs, ks, ks, bs], out_specs=[qs, ls],
        name=f"na_tm_f_{s}", compiler_params=_params(),
    )(q, k, v, bias)


def _na_tm_bwd(q, k, v, bias, o, lse, do, n_ctx):
    b, s, _ = q.shape
    nc = n_ctx // GRID_W
    rows = (s - n_ctx) // GRID_W
    scale = HEAD_DIM ** -0.5
    n_cls = NA_WIN_R + 1
    hg, w, qs, ks, bs, ls = _na_tm_specs(s, nc, rows, NA_GROUPS_BWD)
    lse = jnp.transpose(lse, (0, 2, 1, 3)).reshape(b, s, NA_GROUPS_BWD, hg)
    lse = jnp.transpose(lse, (0, 2, 1, 3))

    def body(q_ref, k_ref, v_ref, bias_ref, o_ref, lse_ref, do_ref, dq_ref, dk_ref, dv_ref, db_ref):
        i = pl.program_id(2)
        rs, cls = _na_geometry(i, nc, rows)
        _, cls_prev = _na_geometry(i - 1, nc, rows)
        start = pl.multiple_of(n_ctx + rs * GRID_W, GRID_W)
        first = jnp.logical_or(i == 0, cls != cls_prev)

        @pl.when(i == 0)
        def _():
            dk_ref[...] = jnp.zeros_like(dk_ref)
            dv_ref[...] = jnp.zeros_like(dv_ref)

        @pl.when(first)
        def _():
            db_ref[...] = jnp.zeros_like(db_ref)

        for hd in range(hg):
            sl, qv, kc, kb, s_c, s_l = _na_tm_scores(q_ref, k_ref, bias_ref, hd, n_ctx, start, scale)
            lse_v = lse_ref[:, hd:hd + 1]
            p_c = jnp.exp(s_c - lse_v)
            p_l = jnp.exp(s_l - lse_v)
            dov = do_ref[:, sl]
            dob = dov.astype(BF16)
            delta = jnp.sum(dov * o_ref[:, sl], axis=-1, keepdims=True)
            vc = v_ref[0:n_ctx, sl].astype(BF16)
            vb = v_ref[pl.ds(start, NA_BAND), sl].astype(BF16)
            ds_c = p_c * (lax.dot_general(dob, vc, _NT, preferred_element_type=F32) - delta)
            ds_l = p_l * (lax.dot_general(dob, vb, _NT, preferred_element_type=F32) - delta)
            dsc_b = ds_c.astype(BF16)
            dsl_b = ds_l.astype(BF16)
            dq_ref[:, sl] = (jnp.dot(dsc_b, kc, preferred_element_type=F32)
                             + jnp.dot(dsl_b, kb, preferred_element_type=F32)) * scale
            dk_ref[0:n_ctx, sl] += lax.dot_general(dsc_b, qv, _TN, preferred_element_type=F32)
            dk_ref[pl.ds(start, NA_BAND), sl] += lax.dot_general(dsl_b, qv, _TN, preferred_element_type=F32)
            dv_ref[0:n_ctx, sl] += lax.dot_general(p_c.astype(BF16), dob, _TN, preferred_element_type=F32)
            dv_ref[pl.ds(start, NA_BAND), sl] += lax.dot_general(p_l.astype(BF16), dob, _TN, preferred_element_type=F32)
            db_ref[hd] += ds_l

    dbs = pl.BlockSpec((None, hg, None, GRID_W, NA_BAND), lambda bi, g, i: (bi, g, _na_geometry(i, nc, rows)[1], 0, 0))
    return pl.pallas_call(
        body,
        out_shape=[jax.ShapeDtypeStruct(q.shape, F32), jax.ShapeDtypeStruct(q.shape, F32), jax.ShapeDtypeStruct(q.shape, F32),
                   jax.ShapeDtypeStruct((b, NA_HEADS, n_cls, GRID_W, NA_BAND), F32)],
        grid=(b, NA_GROUPS_BWD, s // GRID_W), in_specs=[qs, ks, ks, bs, qs, ls, qs], out_specs=[qs, ks, ks, dbs],
        name=f"na_tm_b_{s}", compiler_params=_params(),
    )(q, k, v, bias, o, lse, do)


@functools.partial(jax.custom_vjp, nondiff_argnums=(4,))
def na_attention_tm(q, k, v, bias, n_ctx):
    return _na_tm_fwd(q, k, v, bias, n_ctx)[0]


def _na_attention_tm_fwd(q, k, v, bias, n_ctx):
    o, lse = _na_tm_fwd(q, k, v, bias, n_ctx)
    return o, (q, k, v, bias, o, lse)


def _na_attention_tm_bwd(n_ctx, res, do):
    q, k, v, bias, o, lse = res
    dq, dk, dv, db = _na_tm_bwd(q, k, v, bias, o, lse, do, n_ctx)
    return dq, dk, dv, _sum_rows(db.reshape(db.shape[0], -1, NA_BAND), db.shape[0]).reshape(db.shape[1:])


na_attention_tm.defvjp(_na_attention_tm_fwd, _na_attention_tm_bwd)


def _cmul(ar, ai, br, bi):
    return ar * br - ai * bi, ar * bi + ai * br


def _s5_chunk(n_ctx):
    return min(256, n_ctx)


def _s5_powers(a_re, a_im, t_len):
    a_re, a_im = lax.stop_gradient(a_re), lax.stop_gradient(a_im)
    mag = jnp.sqrt(a_re * a_re + a_im * a_im)
    th = jnp.arctan2(a_im, a_re)
    t = jnp.arange(t_len + 1, dtype=F32)[:, None]
    pm = jnp.where(t == 0, 1.0, jnp.exp(t * jnp.log(jnp.maximum(mag, 1e-37))) * (mag > 0))
    return jnp.stack([pm * jnp.cos(t * th), pm * jnp.sin(t * th)])


def _s5_tables(pw, t_len, rev, conj=False):
    if conj:
        pw = pw * jnp.asarray([1.0, -1.0], F32)[:, None, None]
    steps = jnp.concatenate([pw[:, min(2 ** i, t_len)][:, None] for i in range(8)], axis=1)
    tile = pw[:, 1:9]
    a8k = pw[:, 0:t_len:8]
    if rev:
        tile, a8k = tile[:, ::-1], a8k[:, ::-1]
    misc = jnp.concatenate([pw[:, t_len:t_len + 1], jnp.zeros((2, 7, pw.shape[-1]), F32)], axis=1)
    return jnp.concatenate([steps, tile, misc, a8k], axis=1)


def _scan_chunk(x_re, x_im, tab_ref, hin_re, hin_im, rev, t_len, xs_ref, es_ref):
    outs = [_scan_slab(x_re[:, k:k + LANE], x_im[:, k:k + LANE], tab_ref, hin_re[:, k:k + LANE], hin_im[:, k:k + LANE],
                       rev, t_len, xs_ref, es_ref, k) for k in range(0, x_re.shape[-1], LANE)]
    return tuple(jnp.concatenate([o[t] for o in outs], axis=-1) for t in range(4))


def _scan_slab(x_re, x_im, tab_ref, hin_re, hin_im, rev, t_len, xs_ref, es_ref, k0):
    lanes = LANE
    n2 = t_len // 8
    tab_ref = tab_ref.at[:, :, k0:k0 + LANE]
    rin = lax.broadcasted_iota(jnp.int32, (t_len, lanes), 0) & 7
    for li, sh in enumerate((1, 2, 4)):
        m_re, m_im = tab_ref[0, li:li + 1, :], tab_ref[1, li:li + 1, :]
        amt = sh if not rev else t_len - sh
        c_re, c_im = _cmul(m_re, m_im, pltpu.roll(x_re, amt, 0), pltpu.roll(x_im, amt, 0))
        ok = (rin >= sh) if not rev else (rin < 8 - sh)
        x_re = x_re + jnp.where(ok, c_re, 0.0)
        x_im = x_im + jnp.where(ok, c_im, 0.0)
    xr_ref, xi_ref = xs_ref
    xr_ref[...] = x_re
    xi_ref[...] = x_im
    off = 0 if rev else 7
    e_re = xr_ref[pl.ds(off, n2, stride=8), :]
    e_im = xi_ref[pl.ds(off, n2, stride=8), :]
    row2 = lax.broadcasted_iota(jnp.int32, (n2, lanes), 0)
    sh, li = 1, 3
    while sh < n2:
        m_re, m_im = tab_ref[0, li:li + 1, :], tab_ref[1, li:li + 1, :]
        amt = sh if not rev else n2 - sh
        c_re, c_im = _cmul(m_re, m_im, pltpu.roll(e_re, amt, 0), pltpu.roll(e_im, amt, 0))
        ok = (row2 >= sh) if not rev else (row2 < n2 - sh)
        e_re = e_re + jnp.where(ok, c_re, 0.0)
        e_im = e_im + jnp.where(ok, c_im, 0.0)
        sh, li = sh * 2, li + 1
    es_ref[0] = e_re
    es_ref[1] = e_im
    last = 0 if rev else n2 - 1
    t_re, t_im = _cmul(tab_ref[0, 16:17, :], tab_ref[1, 16:17, :], hin_re, hin_im)
    hout_re = es_ref[0, last:last + 1, :] + t_re
    hout_im = es_ref[1, last:last + 1, :] + t_im
    amt = 1 if not rev else n2 - 1
    ok = (row2 >= 1) if not rev else (row2 < n2 - 1)
    k_re, k_im = _cmul(tab_ref[0, 24:24 + n2, :], tab_ref[1, 24:24 + n2, :], hin_re, hin_im)
    c_re = jnp.where(ok, pltpu.roll(e_re, amt, 0), 0.0) + k_re
    c_im = jnp.where(ok, pltpu.roll(e_im, amt, 0), 0.0) + k_im
    tp_re, tp_im = tab_ref[0, 8:16, :][None], tab_ref[1, 8:16, :][None]
    add_re, add_im = _cmul(tp_re, tp_im, c_re[:, None, :], c_im[:, None, :])
    h_re = xr_ref[...] + add_re.reshape(t_len, lanes)
    h_im = xi_ref[...] + add_im.reshape(t_len, lanes)
    return h_re, h_im, hout_re, hout_im


def _s5_order(j, n_chunks, nc, rev):
    if not rev:
        return j
    return jnp.where(j < nc, nc - 1 - j, n_chunks - 1 - (j - nc))


def _s5_fwd(u, tab, b_bd, c_bd, n_ctx, rev):
    b, s, w = u.shape
    lanes = b_bd.shape[-1]
    t_len = _s5_chunk(n_ctx)
    n_chunks, nc = s // t_len, n_ctx // t_len

    def body(u_ref, tab_ref, b_ref, c_ref, y_ref, h_ref, hin_ref, carry_ref, xr_ref, xi_ref, es_ref):
        xs_ref = (xr_ref, xi_ref)

        @pl.when(pl.program_id(1) == 0)
        def _():
            carry_ref[...] = jnp.zeros_like(carry_ref)

        ub = u_ref[...].astype(BF16)
        x_re = jnp.dot(ub, b_ref[0].astype(BF16), preferred_element_type=F32)
        x_im = jnp.dot(ub, b_ref[1].astype(BF16), preferred_element_type=F32)
        hin_re, hin_im = carry_ref[0, 0:1, :], carry_ref[1, 0:1, :]
        hin_ref[...] = carry_ref[...]
        h_re, h_im, ho_re, ho_im = _scan_chunk(x_re, x_im, tab_ref, hin_re, hin_im, rev, t_len, xs_ref, es_ref)
        carry_ref[0] = jnp.broadcast_to(ho_re, (8, lanes))
        carry_ref[1] = jnp.broadcast_to(ho_im, (8, lanes))
        h_ref[0] = h_re
        h_ref[1] = h_im
        y_ref[...] = (jnp.dot(h_re.astype(BF16), c_ref[0].astype(BF16), preferred_element_type=F32)
                      - jnp.dot(h_im.astype(BF16), c_ref[1].astype(BF16), preferred_element_type=F32))

    order = lambda j: _s5_order(j, n_chunks, nc, rev)
    whole = lambda arr: pl.BlockSpec(arr.shape, lambda bi, j: (0,) * arr.ndim)
    return pl.pallas_call(
        body,
        out_shape=[jax.ShapeDtypeStruct((b, s, w), F32), jax.ShapeDtypeStruct((2, b, s, lanes), F32),
                   jax.ShapeDtypeStruct((2, b, n_chunks, 8, lanes), F32)],
        grid=(b, n_chunks),
        in_specs=[pl.BlockSpec((None, t_len, w), lambda bi, j: (bi, order(j), 0)), whole(tab), whole(b_bd), whole(c_bd)],
        out_specs=[pl.BlockSpec((None, t_len, w), lambda bi, j: (bi, order(j), 0)),
                   pl.BlockSpec((2, None, t_len, lanes), lambda bi, j: (0, bi, order(j), 0)),
                   pl.BlockSpec((2, None, None, 8, lanes), lambda bi, j: (0, bi, order(j), 0, 0))],
        scratch_shapes=[pltpu.VMEM((2, 8, lanes), F32), pltpu.VMEM((t_len, LANE), F32), pltpu.VMEM((t_len, LANE), F32),
                        pltpu.VMEM((2, t_len // 8, LANE), F32)],
        name=f"s5_f_{s}_{int(rev)}", compiler_params=_params(),
    )(u, tab, b_bd, c_bd)


def _s5_bwd(u, tab_adj, b_bd, c_bd, h, hin, dy, n_ctx, rev):
    b, s, w = u.shape
    lanes = b_bd.shape[-1]
    t_len = _s5_chunk(n_ctx)
    n_chunks, nc = s // t_len, n_ctx // t_len
    arev = not rev

    def body(u_ref, tab_ref, b_ref, c_ref, h_ref, hin_ref, dy_ref, du_ref, db_ref, dc_ref, da_ref,
             carry_ref, xr_ref, xi_ref, es_ref):
        xs_ref = (xr_ref, xi_ref)
        first = jnp.logical_and(pl.program_id(0) == 0, pl.program_id(1) == 0)

        @pl.when(pl.program_id(1) == 0)
        def _():
            carry_ref[...] = jnp.zeros_like(carry_ref)

        dyv = dy_ref[...]
        dyb = dyv.astype(BF16)
        dn = (((1,), (1,)), ((), ()))
        dt = (((0,), (0,)), ((), ()))
        x_re = lax.dot_general(dyb, c_ref[0].astype(BF16), dn, preferred_element_type=F32)
        x_im = -lax.dot_general(dyb, c_ref[1].astype(BF16), dn, preferred_element_type=F32)
        g_re, g_im, go_re, go_im = _scan_chunk(x_re, x_im, tab_ref, carry_ref[0, 0:1, :], carry_ref[1, 0:1, :],
                                               arev, t_len, xs_ref, es_ref)
        carry_ref[0] = jnp.broadcast_to(go_re, (8, lanes))
        carry_ref[1] = jnp.broadcast_to(go_im, (8, lanes))
        h_re, h_im = h_ref[0], h_ref[1]
        gb_re, gb_im = g_re.astype(BF16), g_im.astype(BF16)
        du_ref[...] = (lax.dot_general(gb_re, b_ref[0].astype(BF16), dn, preferred_element_type=F32)
                       + lax.dot_general(gb_im, b_ref[1].astype(BF16), dn, preferred_element_type=F32))
        ub = u_ref[...].astype(BF16)
        db_re = lax.dot_general(ub, gb_re, dt, preferred_element_type=F32)
        db_im = lax.dot_general(ub, gb_im, dt, preferred_element_type=F32)
        dc_re = lax.dot_general(h_re.astype(BF16), dyb, dt, preferred_element_type=F32)
        dc_im = -lax.dot_general(h_im.astype(BF16), dyb, dt, preferred_element_type=F32)
        row = lax.broadcasted_iota(jnp.int32, (t_len, lanes), 0)
        amt = 1 if not rev else t_len - 1
        edge = (row == 0) if not rev else (row == t_len - 1)
        hp_re = jnp.where(edge, hin_ref[0, 0:1, :], pltpu.roll(h_re, amt, 0))
        hp_im = jnp.where(edge, hin_ref[1, 0:1, :], pltpu.roll(h_im, amt, 0))
        da_re = jnp.sum(g_re * hp_re + g_im * hp_im, axis=0, keepdims=True)
        da_im = jnp.sum(g_im * hp_re - g_re * hp_im, axis=0, keepdims=True)

        @pl.when(first)
        def _():
            db_ref[0], db_ref[1] = db_re, db_im
            dc_ref[0], dc_ref[1] = dc_re, dc_im
            da_ref[0] = jnp.broadcast_to(da_re, (8, lanes))
            da_ref[1] = jnp.broadcast_to(da_im, (8, lanes))

        @pl.when(jnp.logical_not(first))
        def _():
            db_ref[0] += db_re
            db_ref[1] += db_im
            dc_ref[0] += dc_re
            dc_ref[1] += dc_im
            da_ref[0] += jnp.broadcast_to(da_re, (8, lanes))
            da_ref[1] += jnp.broadcast_to(da_im, (8, lanes))

    order = lambda j: _s5_order(n_chunks - 1 - j, n_chunks, nc, rev)
    whole = lambda arr: pl.BlockSpec(arr.shape, lambda bi, j: (0,) * arr.ndim)
    us = pl.BlockSpec((None, t_len, w), lambda bi, j: (bi, order(j), 0))
    return pl.pallas_call(
        body,
        out_shape=[jax.ShapeDtypeStruct((b, s, w), F32), jax.ShapeDtypeStruct(b_bd.shape, F32),
                   jax.ShapeDtypeStruct(c_bd.shape, F32), jax.ShapeDtypeStruct((2, 8, lanes), F32)],
        grid=(b, n_chunks),
        in_specs=[us, whole(tab_adj), whole(b_bd), whole(c_bd),
                  pl.BlockSpec((2, None, t_len, lanes), lambda bi, j: (0, bi, order(j), 0)),
                  pl.BlockSpec((2, None, None, 8, lanes), lambda bi, j: (0, bi, order(j), 0, 0)), us],
        out_specs=[us, whole(b_bd), whole(c_bd), pl.BlockSpec((2, 8, lanes), lambda bi, j: (0, 0, 0))],
        scratch_shapes=[pltpu.VMEM((2, 8, lanes), F32), pltpu.VMEM((t_len, LANE), F32), pltpu.VMEM((t_len, LANE), F32),
                        pltpu.VMEM((2, t_len // 8, LANE), F32)],
        name=f"s5_b_{s}_{int(rev)}", compiler_params=_params(),
    )(u, tab_adj, b_bd, c_bd, h, hin, dy)


@functools.partial(jax.custom_vjp, nondiff_argnums=(4, 5))
def s5_direction(u, a, b_bd, c_bd, n_ctx, rev):
    t_len = _s5_chunk(n_ctx)
    return _s5_fwd(u, _s5_tables(_s5_powers(a[0], a[1], t_len), t_len, rev), b_bd, c_bd, n_ctx, rev)[0]


def _s5_direction_fwd(u, a, b_bd, c_bd, n_ctx, rev):
    t_len = _s5_chunk(n_ctx)
    pw = _s5_powers(a[0], a[1], t_len)
    y, h, hin = _s5_fwd(u, _s5_tables(pw, t_len, rev), b_bd, c_bd, n_ctx, rev)
    return y, (u, pw, b_bd, c_bd, h, hin)


def _s5_direction_bwd(n_ctx, rev, res, dy):
    u, pw, b_bd, c_bd, h, hin = res
    tab_adj = _s5_tables(pw, _s5_chunk(n_ctx), not rev, conj=True)
    du, db, dc, da = _s5_bwd(u, tab_adj, b_bd, c_bd, h, hin, dy, n_ctx, rev)
    return du, da[:, 0, :], db, dc


s5_direction.defvjp(_s5_direction_fwd, _s5_direction_bwd)


def _s5_discretize(lam_re, lam_im, log_dt, b_re, b_im):
    dt = jnp.exp(log_dt)[:, None]
    mag = jnp.exp(lam_re * dt)
    a_re = mag * jnp.cos(lam_im * dt)
    a_im = mag * jnp.sin(lam_im * dt)
    den = jnp.square(lam_re) + jnp.square(lam_im)
    f_re = ((a_re - 1.0) * lam_re + a_im * lam_im) / den
    f_im = (a_im * lam_re - (a_re - 1.0) * lam_im) / den
    bb_re = f_re[..., None] * b_re - f_im[..., None] * b_im
    bb_im = f_re[..., None] * b_im + f_im[..., None] * b_re
    return a_re, a_im, bb_re, bb_im


def _block_diag(t):
    g, r, c = t.shape
    return (jnp.eye(g, dtype=F32)[:, None, :, None] * t[:, :, None, :]).reshape(g * r, g * c)


def _loss_head(y, target):
    b, n, d = y.shape
    ts = _pick(n, (256, 128, 64))

    def body(y_ref, t_ref, loss_ref, dy_ref):
        first = jnp.logical_and(pl.program_id(0) == 0, pl.program_id(1) == 0)
        err = y_ref[...] - t_ref[...]
        dy_ref[...] = err * (1.0 / d)
        part = 0.5 * jnp.sum(jnp.sum(err * err, axis=-1, keepdims=True) * (1.0 / d), axis=0, keepdims=True)
        part = jnp.broadcast_to(part, (8, LANE))

        @pl.when(first)
        def _():
            loss_ref[...] = part

        @pl.when(jnp.logical_not(first))
        def _():
            loss_ref[...] += part

    blk = pl.BlockSpec((None, ts, d), lambda bi, i: (bi, i, 0))
    return pl.pallas_call(
        body, out_shape=[jax.ShapeDtypeStruct((8, LANE), F32), jax.ShapeDtypeStruct((b, n, d), F32)],
        grid=(b, n // ts), in_specs=[blk, blk], out_specs=[pl.BlockSpec((8, LANE), lambda bi, i: (0, 0)), blk],
        name="loss_head", compiler_params=_params(),
    )(y, target)


def _adamw(w, g, m, v):
    shape = w.shape
    n = int(np.prod(shape))
    cols = shape[-1]
    r = n // cols
    tr = _pick(r, (512, 256, 128, 64, 32, 16, 8))
    c1 = 1.0 / (1.0 - ADAM_B1 ** ADAM_STEP)
    c2 = 1.0 / (1.0 - ADAM_B2 ** ADAM_STEP)

    def body(w_ref, g_ref, m_ref, v_ref, d_ref, mo_ref, vo_ref):
        gv = g_ref[...]
        m2 = ADAM_B1 * m_ref[...] + (1.0 - ADAM_B1) * gv
        v2 = ADAM_B2 * v_ref[...] + (1.0 - ADAM_B2) * (gv * gv)
        d_ref[...] = -ADAM_LR * ((m2 * c1) / (jnp.sqrt(v2 * c2) + ADAM_EPS) + ADAM_WD * w_ref[...])
        mo_ref[...] = m2
        vo_ref[...] = v2

    blk = pl.BlockSpec((tr, cols), lambda i: (i, 0))
    outs = pl.pallas_call(
        body, out_shape=[jax.ShapeDtypeStruct((r, cols), F32)] * 3, grid=(r // tr,),
        in_specs=[blk] * 4, out_specs=[blk] * 3, name=f"adamw_{r}x{cols}", compiler_params=_params(),
    )(*[t.reshape(r, cols) for t in (w, g, m, v)])
    return tuple(o.reshape(shape) for o in outs)


def _sum_rows(x, n):
    _, r, c = x.shape
    tr = _pick(r, (512, 256, 128, 64, 32, 16, 8))

    def body(x_ref, o_ref):
        acc = x_ref[0]
        for j in range(1, n):
            acc = acc + x_ref[j]
        o_ref[...] = acc

    return pl.pallas_call(
        body, out_shape=jax.ShapeDtypeStruct((r, c), F32), grid=(r // tr,),
        in_specs=[pl.BlockSpec((n, tr, c), lambda i: (0, i, 0))], out_specs=pl.BlockSpec((tr, c), lambda i: (i, 0)),
        name=f"sum{n}_{r}x{c}", compiler_params=_params(),
    )(x)


def _accumulate(parts, out_dtype):
    r, c = parts[0].shape[-2:]
    tr = _pick(r, (1152, 1024, 768, 576, 512, 256, 128, 64, 32, 16))

    def body(*refs):
        acc = None
        for ref in refs[:-1]:
            terms = [ref[j] for j in range(ref.shape[0])] if len(ref.shape) == 3 else [ref[...]]
            for t in terms:
                acc = t.astype(F32) if acc is None else acc + t.astype(F32)
        refs[-1][...] = acc.astype(out_dtype)

    specs = [pl.BlockSpec((p.shape[0], tr, c), lambda i: (0, i, 0)) if p.ndim == 3 else pl.BlockSpec((tr, c), lambda i: (i, 0))
             for p in parts]
    tag = "_".join(str(p.shape[0]) if p.ndim == 3 else "1" for p in parts)
    return pl.pallas_call(
        body, out_shape=jax.ShapeDtypeStruct((r, c), out_dtype), grid=(r // tr,), in_specs=specs,
        out_specs=pl.BlockSpec((tr, c), lambda i: (i, 0)), name=f"accumulate_{tag}_{r}x{c}_{jnp.dtype(out_dtype).name}",
        compiler_params=_params(),
    )(*parts)


def _add2(x, y):
    shape = x.shape
    c = shape[-1]
    r = int(np.prod(shape)) // c
    tr = _pick(r, (512, 256, 128, 64, 32, 16, 8))

    def body(x_ref, y_ref, o_ref):
        o_ref[...] = x_ref[...] + y_ref[...]

    blk = pl.BlockSpec((tr, c), lambda i: (i, 0))
    return pl.pallas_call(
        body, out_shape=jax.ShapeDtypeStruct((r, c), F32), grid=(r // tr,), in_specs=[blk, blk], out_specs=blk,
        name=f"add2_{r}x{c}", compiler_params=_params(),
    )(x.reshape(r, c), y.reshape(r, c)).reshape(shape)


_FLIPS = ((1, 0), (0, 1), (1, 1))


def _me():
    return lax.axis_index("x"), lax.axis_index("y"), lax.axis_index("c")


def allgather8(v):
    m_per, n = v.shape

    def body(x_ref, out_ref, send_sems, recv_sems, local_sem):
        x, y, c = _me()
        me, sibling = (x, y, c), (x, y, 1 - c)
        chips = [(1 - x, y), (x, 1 - y), (1 - x, 1 - y)]

        def rows(px, py, pc):
            return out_ref.at[pl.ds((4 * px + 2 * py + pc) * m_per, m_per), :]

        def copy(k, block, to, src=None):
            return pltpu.make_async_remote_copy(
                src_ref=rows(*block) if src is None else src, dst_ref=rows(*block),
                send_sem=send_sems.at[k], recv_sem=recv_sems.at[k], device_id=to, device_id_type=MESH)

        mine = pltpu.make_async_copy(x_ref, rows(*me), local_sem)
        mine.start()
        first = [copy(0, me, sibling, src=x_ref)]
        first += [copy(1 + j, me, (*chip, c), src=x_ref) for j, chip in enumerate(chips)]
        for cp in first:
            cp.start()
        passed = [copy(4 + j, (*chip, c), sibling) for j, chip in enumerate(chips)]
        for j, chip in enumerate(chips):
            copy(1 + j, (*chip, c), me).wait_recv()
            passed[j].start()
        copy(0, sibling, me).wait_recv()
        for j, chip in enumerate(chips):
            copy(4 + j, (*chip, 1 - c), me).wait_recv()
        for cp in first + passed:
            cp.wait_send()
        mine.wait()

    return pl.pallas_call(
        body, out_shape=jax.ShapeDtypeStruct((N_DEV * m_per, n), v.dtype), in_specs=[VMEM_SPEC], out_specs=VMEM_SPEC,
        scratch_shapes=[pltpu.SemaphoreType.DMA((7,)), pltpu.SemaphoreType.DMA((7,)), pltpu.SemaphoreType.DMA],
        name=f"allgather8_{m_per}x{n}", compiler_params=_params(),
    )(v)


def _row_chunks(rows, tile_rows, want):
    n = want
    while n > 1 and rows % (n * tile_rows):
        n //= 2
    return [(i * (rows // n), rows // n) for i in range(n)]


def _remote(src, dst, send_sem, recv_sem, to):
    return pltpu.make_async_remote_copy(src_ref=src, dst_ref=dst, send_sem=send_sem, recv_sem=recv_sem, device_id=to,
                                        device_id_type=MESH)


def plane_allgather(big, small):
    rows = big.shape[0]
    rh = rows // 2
    rq = rh // 2
    tile = 16 if big.dtype == BF16 else 8
    assert rq % tile == 0
    ch_full = _row_chunks(rows, tile, 8)
    ch_half = _row_chunks(rh, tile, 4)

    def body(big_ref, small_ref, obig_ref, osmall_ref, send_sems, recv_sems, relay_send, relay_recv, fwd_send, fwd_recv,
             own_send, own_recv):
        x, y, c = _me()
        me = 2 * x + y
        sibling = (x, y, 1 - c)
        nbr_x, nbr_y, diag = (1 - x, y, c), (x, 1 - y, c), (1 - x, 1 - y, c)
        xi, yi, di = 2 * (1 - x) + y, 2 * x + (1 - y), 2 * (1 - x) + (1 - y)
        base, obase = c * rh, (1 - c) * rh
        mine, other = pl.ds(base, rh), pl.ds(obase, rh)
        qa, qb = pl.ds(base, rq), pl.ds(base + rq, rq)
        for st, sz in ch_full:
            sl = pl.ds(st, sz)
            _remote(big_ref.at[sl], obig_ref.at[me, sl], own_send.at[0], own_recv.at[0], sibling).start()
        _remote(small_ref, osmall_ref.at[me], own_send.at[1], own_recv.at[1], sibling).start()
        for j, peer in enumerate((nbr_x, nbr_y)):
            for st, sz in ch_half:
                sl = pl.ds(base + st, sz)
                _remote(big_ref.at[sl], obig_ref.at[me, sl], send_sems.at[j], recv_sems.at[j], peer).start()
        for j, peer in enumerate((nbr_x, nbr_y, diag)):
            _remote(small_ref, osmall_ref.at[me], send_sems.at[3 + j], recv_sems.at[3 + j], peer).start()

        def pass_on(k, slot, sl):
            _remote(obig_ref.at[slot, sl], obig_ref.at[slot, sl], fwd_send.at[k], fwd_recv.at[k], sibling).start()

        _remote(big_ref.at[mine], obig_ref.at[xi, mine], send_sems.at[0], recv_sems.at[0], nbr_x).wait_recv()
        _remote(obig_ref.at[xi, qa], obig_ref.at[xi, qa], relay_send.at[0], relay_recv.at[0], nbr_y).start()
        pass_on(0, xi, mine)
        _remote(big_ref.at[mine], obig_ref.at[yi, mine], send_sems.at[1], recv_sems.at[1], nbr_y).wait_recv()
        _remote(obig_ref.at[yi, qb], obig_ref.at[yi, qb], relay_send.at[1], relay_recv.at[1], nbr_x).start()
        pass_on(1, yi, mine)
        _remote(obig_ref.at[di, qa], obig_ref.at[di, qa], relay_send.at[0], relay_recv.at[0], nbr_y).wait_recv()
        pass_on(2, di, qa)
        _remote(obig_ref.at[di, qb], obig_ref.at[di, qb], relay_send.at[1], relay_recv.at[1], nbr_x).wait_recv()
        pass_on(3, di, qb)
        for j, (peer, slot) in enumerate(((nbr_x, xi), (nbr_y, yi), (diag, di))):
            _remote(small_ref, osmall_ref.at[slot], send_sems.at[3 + j], recv_sems.at[3 + j], peer).wait_recv()
        oqa, oqb = pl.ds(obase, rq), pl.ds(obase + rq, rq)
        for k, (slot, sl) in enumerate(((xi, other), (yi, other), (di, oqa), (di, oqb))):
            _remote(obig_ref.at[slot, sl], obig_ref.at[slot, sl], fwd_send.at[k], fwd_recv.at[k], sibling).wait_recv()
        for k, (slot, sl) in enumerate(((xi, mine), (yi, mine), (di, qa), (di, qb))):
            _remote(obig_ref.at[slot, sl], obig_ref.at[slot, sl], fwd_send.at[k], fwd_recv.at[k], sibling).wait_send()
        for j, peer in enumerate((nbr_x, nbr_y)):
            _remote(big_ref.at[mine], obig_ref.at[me, mine], send_sems.at[j], recv_sems.at[j], peer).wait_send()
        for j, peer in enumerate((nbr_x, nbr_y, diag)):
            _remote(small_ref, osmall_ref.at[me], send_sems.at[3 + j], recv_sems.at[3 + j], peer).wait_send()
        _remote(obig_ref.at[xi, qa], obig_ref.at[xi, qa], relay_send.at[0], relay_recv.at[0], nbr_y).wait_send()
        _remote(obig_ref.at[yi, qb], obig_ref.at[yi, qb], relay_send.at[1], relay_recv.at[1], nbr_x).wait_send()
        _remote(big_ref, obig_ref.at[me], own_send.at[0], own_recv.at[0], sibling).wait()
        _remote(small_ref, osmall_ref.at[me], own_send.at[1], own_recv.at[1], sibling).wait()

    dma = pltpu.SemaphoreType.DMA
    return pl.pallas_call(
        body, out_shape=[jax.ShapeDtypeStruct((N_PLANE,) + big.shape, big.dtype),
                         jax.ShapeDtypeStruct((N_PLANE,) + small.shape, small.dtype)],
        in_specs=[ANY, ANY], out_specs=[ANY, ANY],
        scratch_shapes=[dma((6,)), dma((6,)), dma((2,)), dma((2,)), dma((4,)), dma((4,)), dma((2,)), dma((2,))],
        name="plane_allgather", compiler_params=_params(),
    )(big, small)


def neighbour_exchange(to_x, to_y):
    tile = 16 if to_x.dtype == BF16 else 8
    chunks = _row_chunks(to_x.shape[0], tile, 4)

    def body(ax_ref, ay_ref, fx_ref, fy_ref, send_sems, recv_sems):
        x, y, c = _me()
        for k, (src, dst, peer) in enumerate(((ax_ref, fx_ref, (1 - x, y, c)), (ay_ref, fy_ref, (x, 1 - y, c)))):
            for st, sz in chunks:
                sl = pl.ds(st, sz)
                _remote(src.at[sl], dst.at[sl], send_sems.at[k], recv_sems.at[k], peer).start()
        for k, (src, dst, peer) in enumerate(((ax_ref, fx_ref, (1 - x, y, c)), (ay_ref, fy_ref, (x, 1 - y, c)))):
            _remote(src, dst, send_sems.at[k], recv_sems.at[k], peer).wait()

    shape = jax.ShapeDtypeStruct(to_x.shape, to_x.dtype)
    return pl.pallas_call(
        body, out_shape=[shape, shape], in_specs=[ANY, ANY], out_specs=[ANY, ANY],
        scratch_shapes=[pltpu.SemaphoreType.DMA((2,)), pltpu.SemaphoreType.DMA((2,))],
        name=f"neighbour_exchange_{to_x.shape[0]}", compiler_params=_params(),
    )(to_x, to_y)


def sibling_halves(buf):
    n_blk, _, rows, cols = buf.shape
    tile = 16 if buf.dtype == BF16 else 8
    chunks = _row_chunks(rows, tile, 2)

    def body(buf_ref, got_ref, send_sem, recv_sem):
        x, y, c = _me()
        for j in range(n_blk):
            for st, sz in chunks:
                sl = pl.ds(st, sz)
                _remote(buf_ref.at[j, 1 - c, sl], got_ref.at[j, sl], send_sem, recv_sem, (x, y, 1 - c)).start()
        _remote(got_ref, got_ref, send_sem, recv_sem, (x, y, 1 - c)).wait()

    return pl.pallas_call(
        body, out_shape=jax.ShapeDtypeStruct((n_blk, rows, cols), buf.dtype), in_specs=[ANY], out_specs=ANY,
        scratch_shapes=[pltpu.SemaphoreType.DMA, pltpu.SemaphoreType.DMA],
        name="sibling_halves", compiler_params=_params(),
    )(buf)


def sibling_swap(s):
    tile = 16 if s.dtype == BF16 else 8
    chunks = _row_chunks(s.shape[0], tile, 8)

    def body(s_ref, got_ref, send_sem, recv_sem):
        x, y, c = _me()
        for st, sz in chunks:
            sl = pl.ds(st, sz)
            _remote(s_ref.at[sl], got_ref.at[sl], send_sem, recv_sem, (x, y, 1 - c)).start()
        _remote(s_ref, got_ref, send_sem, recv_sem, (x, y, 1 - c)).wait()

    return pl.pallas_call(
        body, out_shape=jax.ShapeDtypeStruct(s.shape, s.dtype), in_specs=[ANY], out_specs=ANY,
        scratch_shapes=[pltpu.SemaphoreType.DMA, pltpu.SemaphoreType.DMA],
        name="sibling_swap", compiler_params=_params(),
    )(s)


def _op(cache, fn, name, kinds, out_dims, **kw):
    key = (name, tuple(out_dims), tuple(sorted(kw.items())))
    if key not in cache:
        cache[key] = make_rowwise(fn, name, kinds, out_dims, **kw)
    return cache[key]


def _even_mixer(ops, a, w, n_ctx):
    b, s, d = a.shape
    proj = linear(a.reshape(b * s, d), w["e_w_in"]).reshape(b, s, -1)
    q, k, v, u = jnp.split(proj, [GQA_Q_W, GQA_Q_W + GQA_KV_W, GQA_Q_W + 2 * GQA_KV_W], axis=-1)
    cos, sin = _rope_tables(n_ctx, s - n_ctx, HEAD_DIM, 0, HEAD_DIM)
    shift = HEAD_DIM // 4
    qn = head_norm_rope(q, w["e_g_q"][None], jnp.tile(cos, (1, GQA_Q_HEADS)), jnp.tile(sin, (1, GQA_Q_HEADS)),
                        jnp.asarray(_rope_signs(HEAD_DIM, 0, HEAD_DIM, GQA_Q_HEADS)), GQA_Q_HEADS, shift)
    kn = head_norm_rope(k, w["e_g_k"][None], jnp.tile(cos, (1, GQA_KV_HEADS)), jnp.tile(sin, (1, GQA_KV_HEADS)),
                        jnp.asarray(_rope_signs(HEAD_DIM, 0, HEAD_DIM, GQA_KV_HEADS)), GQA_KV_HEADS, shift)
    att = gqa_attention(qn, kn, v, n_ctx)
    ys = []
    for dr in range(2):
        a_re, a_im, bb_re, bb_im = _s5_discretize(w["ssm_lam_re"][dr], w["ssm_lam_im"][dr], w["ssm_log_dt"][dr],
                                                  w["ssm_b_re"][dr], w["ssm_b_im"][dr])
        a_flat = jnp.stack([a_re.reshape(-1), a_im.reshape(-1)])
        b_bd = jnp.stack([_block_diag(jnp.swapaxes(bb_re, 1, 2)), _block_diag(jnp.swapaxes(bb_im, 1, 2))])
        c_bd = jnp.stack([_block_diag(jnp.swapaxes(w["ssm_c_re"][dr], 1, 2)),
                          _block_diag(jnp.swapaxes(w["ssm_c_im"][dr], 1, 2))])
        ys.append(s5_direction(u, a_flat, b_bd, c_bd, n_ctx, dr == 1))
    pre = _op(ops, _fn_glu_pre, "glu_pre", ("row", "row", "row", "glob"), (SSM_WIDTH,))
    post = _op(ops, _fn_glu_post, "glu_post", ("row", "row", "glob"), (SSM_WIDTH,))
    z = pre(u, ys[0], ys[1], w["ssm_d"][None])[0]
    t = linear(z.reshape(b * s, SSM_WIDTH), w["ssm_w_glu"]).reshape(b, s, SSM_WIDTH)
    ssm = post(z, t, w["ssm_b_glu"][None])[0]
    mix = jnp.concatenate([att, ssm], axis=-1)
    return linear(mix.reshape(b * s, -1), w["e_w_out"]).reshape(b, s, d)


def _odd_mixer(ops, a, w, n_ctx):
    b, s, d = a.shape
    w_in = jnp.pad(w["o_w_in"], ((0, 0), (0, ODD_IN_PAD - ODD_IN_W)))
    proj = linear(a.reshape(b * s, d), w_in).reshape(b, s, -1)
    c1 = MLA_Q_RANK
    c2 = c1 + MLA_KV_RANK
    c3 = c2 + MLA_ROPE
    cq, ckv, kr, nq, nk, nv, _ = jnp.split(proj, [c1, c2, c3, c3 + NA_W, c3 + 2 * NA_W, ODD_IN_W], axis=-1)
    nrm = lambda wd: _op(ops, _fn_norm, f"norm{wd}", ("row", "glob"), (wd,))
    cqn = nrm(MLA_Q_RANK)(cq, w["mla_g_cq"][None])[0]
    ckvn = nrm(MLA_KV_RANK)(ckv, w["mla_g_ckv"][None])[0]
    q = linear(cqn.reshape(b * s, -1), w["mla_w_uq"]).reshape(b, s, -1)
    kv = linear(ckvn.reshape(b * s, -1), w["mla_w_ukv"]).reshape(b, s, -1)
    cos, sin = _rope_tables(n_ctx, s - n_ctx, MLA_QK, MLA_NOPE, MLA_ROPE)
    cos, sin = jnp.tile(cos, (1, MLA_HEADS)), jnp.tile(sin, (1, MLA_HEADS))
    sg = jnp.asarray(_rope_signs(MLA_QK, MLA_NOPE, MLA_ROPE, MLA_HEADS))
    mq = head_norm_rope(q, w["mla_g_q"][None], cos, sin, sg, MLA_HEADS, MLA_ROPE // 4)
    mk = mla_k_prep(kv, kr, w["mla_g_k"][None], cos, sin, sg)
    mla = mla_attention(mq, mk, kv, n_ctx)
    nqn = head_norm_rope(nq, w["na_g_q"][None], None, None, None, NA_HEADS, 0)
    nkn = head_norm_rope(nk, w["na_g_k"][None], None, None, None, NA_HEADS, 0)
    na = na_attention_tm(nqn, nkn, nv, na_bias_table(w["na_rpb"]), n_ctx)
    mix = jnp.concatenate([mla, na], axis=-1)
    return linear(mix.reshape(b * s, -1), w["o_w_out"]).reshape(b, s, d)


_EVEN_KEYS = ("e_w_in", "e_w_out", "e_g_q", "e_g_k", "ssm_lam_re", "ssm_lam_im", "ssm_log_dt", "ssm_b_re", "ssm_b_im",
              "ssm_c_re", "ssm_c_im", "ssm_d", "ssm_w_glu", "ssm_b_glu")
_ODD_KEYS = ("o_w_in", "o_w_out", "mla_g_cq", "mla_g_ckv", "mla_w_uq", "mla_w_ukv", "mla_g_q", "mla_g_k", "na_g_q",
             "na_g_k", "na_rpb")


def _trunk(x_all, mods, w, n_ctx):
    ops = {}
    depth = mods.shape[0]
    b, s, d = x_all.shape
    modulate = make_modulate(d, n_ctx)
    gated = make_gated_add(d, n_ctx)
    x = x_all
    for i in range(depth):
        j = i // 2
        m = [mods[i][:, :, r:r + 1, :] for r in range(N_MOD)]
        a, x = modulate(x, w["g_norm1"][i][None], m[0], m[1])
        if i % 2 == 0:
            o = _even_mixer(ops, a, {k: w[k][j] for k in _EVEN_KEYS}, n_ctx)
        else:
            o = _odd_mixer(ops, a, {k: w[k][j] for k in _ODD_KEYS}, n_ctx)
        x = gated(x, o, m[2])
        a2, x = modulate(x, w["g_norm2"][i][None], m[3], m[4])
        f = ffn(a2.reshape(b * s, d), w["w_ff1"][i], w["w_ff2"][i]).reshape(b, s, d)
        x = gated(x, f, m[5])
    return x[:, n_ctx:]


def local_step(x, ctx, mods, w, loss_target):
    n_ctx = ctx.shape[1]
    x_all = jnp.concatenate([ctx, x], axis=1)
    y, vjp = jax.vjp(lambda xa, md, ww: _trunk(xa, md, ww, n_ctx), x_all, mods, w)
    loss_tile, dy = _loss_head(y, loss_target)
    dx_all, dmods, dw = vjp(dy)
    return loss_tile[0, 0], dx_all[:, n_ctx:], dmods, dw


_SHARDED = (("w_ff1", 2), ("w_ff2", 1), ("e_w_in", 2), ("e_w_out", 1), ("o_w_in", 2), ("o_w_out", 1),
            ("mla_w_uq", 2), ("mla_w_ukv", 2), ("ssm_w_glu", 1))
_SHARDED_SMALL = (("mla_g_cq", 1), ("mla_g_ckv", 1))
_REPLICATED = ("g_norm1", "g_norm2", "e_g_q", "e_g_k", "ssm_lam_re", "ssm_lam_im", "ssm_log_dt", "ssm_b_re", "ssm_b_im",
               "ssm_c_re", "ssm_c_im", "ssm_d", "ssm_b_glu", "mla_g_q", "mla_g_k", "na_g_q", "na_g_k", "na_rpb")
_WEIGHTS = ("c_ctx", "w_mod", "b_mod", "g_norm1", "g_norm2", "w_ff1", "w_ff2", "e_w_in", "e_w_out", "e_g_q", "e_g_k",
            "ssm_lam_re", "ssm_lam_im", "ssm_log_dt", "ssm_b_re", "ssm_b_im", "ssm_c_re", "ssm_c_im", "ssm_d",
            "ssm_w_glu", "ssm_b_glu", "o_w_in", "o_w_out", "mla_g_cq", "mla_g_ckv", "mla_w_uq", "mla_w_ukv", "mla_g_q",
            "mla_g_k", "na_g_q", "na_g_k", "na_rpb")
_PACK_ROWS = 64


def _pack(arrs, dtype, cols=1024, row_mult=_PACK_ROWS):
    blocks, tail, off = [], [], 0
    for a in arrs:
        n = int(np.prod(a.shape))
        if not tail and off % cols == 0 and n % cols == 0:
            blocks.append(a.astype(dtype).reshape(-1, cols))
        else:
            tail.append(a.astype(dtype).reshape(-1))
        off += n
    rows = -(-off // cols)
    pad = (-rows) % row_mult * cols + rows * cols - off
    if tail or pad:
        blocks.append(jnp.concatenate(tail + [jnp.zeros((pad,), dtype)]).reshape(-1, cols))
    return jnp.concatenate(blocks, axis=0)


def _unpack(packed, shapes):
    cols = packed.shape[-1]
    packed = packed.reshape(-1, cols)
    out, off = [], 0
    for sh in shapes:
        n = int(np.prod(sh))
        if off % cols == 0 and n % cols == 0:
            out.append(packed[off // cols:(off + n) // cols].reshape(sh))
        else:
            r0, r1 = off // cols, -(-(off + n) // cols)
            out.append(packed[r0:r1].reshape(-1)[off - r0 * cols:off - r0 * cols + n].reshape(sh))
        off += n
    return out


def kernel(x, c, ctx, c_ctx, w_mod, b_mod, g_norm1, g_norm2, w_ff1, w_ff2, e_w_in, e_w_out, e_g_q, e_g_k, ssm_lam_re, ssm_lam_im, ssm_log_dt, ssm_b_re, ssm_b_im, ssm_c_re, ssm_c_im, ssm_d, ssm_w_glu, ssm_b_glu, o_w_in, o_w_out, mla_g_cq, mla_g_ckv, mla_w_uq, mla_w_ukv, mla_g_q, mla_g_k, na_g_q, na_g_k, na_rpb, loss_target, m_c_ctx, m_w_mod, m_b_mod, m_g_norm1, m_g_norm2, m_w_ff1, m_w_ff2, m_e_w_in, m_e_w_out, m_e_g_q, m_e_g_k, m_ssm_lam_re, m_ssm_lam_im, m_ssm_log_dt, m_ssm_b_re, m_ssm_b_im, m_ssm_c_re, m_ssm_c_im, m_ssm_d, m_ssm_w_glu, m_ssm_b_glu, m_o_w_in, m_o_w_out, m_mla_g_cq, m_mla_g_ckv, m_mla_w_uq, m_mla_w_ukv, m_mla_g_q, m_mla_g_k, m_na_g_q, m_na_g_k, m_na_rpb, v_c_ctx, v_w_mod, v_b_mod, v_g_norm1, v_g_norm2, v_w_ff1, v_w_ff2, v_e_w_in, v_e_w_out, v_e_g_q, v_e_g_k, v_ssm_lam_re, v_ssm_lam_im, v_ssm_log_dt, v_ssm_b_re, v_ssm_b_im, v_ssm_c_re, v_ssm_c_im, v_ssm_d, v_ssm_w_glu, v_ssm_b_glu, v_o_w_in, v_o_w_out, v_mla_g_cq, v_mla_g_ckv, v_mla_w_uq, v_mla_w_ukv, v_mla_g_q, v_mla_g_k, v_na_g_q, v_na_g_k, v_na_rpb):
    env = dict(locals())
    weights = {n: env[n] for n in _WEIGHTS}
    mom_m = {n: env["m_" + n] for n in _WEIGHTS}
    mom_v = {n: env["v_" + n] for n in _WEIGHTS}
    ax, ay, ac = _me()
    plane = 2 * ax + ay
    dev = 4 * ax + 2 * ay + ac
    b_loc, d = c.shape
    depth = w_mod.shape[0]
    n_all = N_DEV * b_loc
    mod_cols = w_mod.shape[2]

    big = _pack([weights[n] for n, _ in _SHARDED], BF16)
    small = _pack([weights[n] for n, _ in _SHARDED_SMALL], F32, cols=LANE, row_mult=8)
    g_big, g_small = plane_allgather(big, small)
    full = {n: weights[n] for n in _REPLICATED}
    parts = [_unpack(g_big[j], [weights[n].shape for n, _ in _SHARDED]) for j in range(N_PLANE)]
    for t, (n, axis) in enumerate(_SHARDED):
        full[n] = [jnp.concatenate([parts[j][t][l] for j in range(N_PLANE)], axis=axis - 1)
                   for l in range(weights[n].shape[0])]
    parts_s = [_unpack(g_small[j], [weights[n].shape for n, _ in _SHARDED_SMALL]) for j in range(N_PLANE)]
    for t, (n, axis) in enumerate(_SHARDED_SMALL):
        full[n] = jnp.concatenate([parts_s[j][t] for j in range(N_PLANE)], axis=axis)

    rows_pad = 8 * ((n_all + 1 + 7) // 8)
    c_all = allgather8(jnp.pad(c, ((0, 8 - b_loc), (0, 0)))).reshape(N_DEV, 8, d)[:, :b_loc].reshape(n_all, d)
    cond_raw = jnp.concatenate([c_all, c_ctx[None], jnp.zeros((rows_pad - n_all - 1, d), F32)], axis=0)
    b_cols = lax.dynamic_slice_in_dim(b_mod, plane * mod_cols, mod_cols, axis=1)
    mod_loc = jnp.stack([_mm(cond_raw, w_mod[i], a_act="silu") + b_cols[i][None] for i in range(depth)])
    mod_g = allgather8(mod_loc.reshape(depth * rows_pad, mod_cols)).reshape(N_PLANE, 2, depth, rows_pad, mod_cols)
    mod_all = jnp.concatenate([mod_g[j, 0] for j in range(N_PLANE)], axis=-1)
    m_lat = lax.dynamic_slice_in_dim(mod_all, dev * b_loc, b_loc, axis=1)
    m_ctx = jnp.broadcast_to(mod_all[:, n_all][:, None], m_lat.shape)
    mods = jnp.stack([m_ctx, m_lat], axis=2).reshape(depth, b_loc, 2, N_MOD, d)

    loss_part, grad_x, dmods, dw = local_step(x, ctx, mods, full, loss_target)

    dm = dmods.reshape(depth, b_loc, 2, N_MOD * d)
    dm_rows = jnp.concatenate([dm[:, :, 1], jnp.sum(dm[:, :, 0], axis=1, keepdims=True)], axis=1)
    rep_shapes = [weights[n].shape for n in _REPLICATED] + [(1,)]
    small_pack = _pack([dm_rows] + [dw[n] for n in _REPLICATED] + [loss_part.reshape(1)], F32, cols=1024, row_mult=8)
    sp_rows = small_pack.shape[0]
    gathered = allgather8(small_pack).reshape(N_DEV, sp_rows, 1024)
    n_dm = depth * (b_loc + 1) * N_MOD * d
    dm_all = gathered.reshape(N_DEV, -1)[:, :n_dm].reshape(N_DEV, depth, b_loc + 1, N_MOD * d)
    rep_sum = _sum_rows(gathered, N_DEV).reshape(-1)
    rep_parts = _unpack(rep_sum[n_dm:], rep_shapes)
    rep_grads = dict(zip(_REPLICATED, rep_parts[:-1]))
    loss = rep_parts[-1][0]
    d_ctx_row = rep_sum[:n_dm].reshape(depth, b_loc + 1, N_MOD * d)[:, b_loc]
    d_lat_rows = jnp.transpose(dm_all[:, :, :b_loc], (1, 0, 2, 3)).reshape(depth, n_all, N_MOD * d)
    d_mod_all = jnp.concatenate([d_lat_rows, d_ctx_row[:, None],
                                 jnp.zeros((depth, rows_pad - n_all - 1, N_MOD * d), F32)], axis=1)
    grads = dict(rep_grads)
    grads["b_mod"] = jnp.sum(d_mod_all, axis=1)
    d_cols = lax.dynamic_slice_in_dim(d_mod_all, plane * mod_cols, mod_cols, axis=2)
    grads["w_mod"] = jnp.stack([_mm(cond_raw, d_cols[i], ta=True, a_act="silu") for i in range(depth)])
    d_cond = _mm(d_cols[0], w_mod[0], tb=True)
    for i in range(1, depth):
        d_cond = _add2(d_cond, _mm(d_cols[i], w_mod[i], tb=True))
    d_cond_g = allgather8(d_cond[n_all:n_all + 8] if rows_pad - n_all >= 8 else
                          jnp.pad(d_cond[n_all:], ((0, 8 - (rows_pad - n_all)), (0, 0)))).reshape(N_PLANE, 2, 8, d)
    d_silu = _sum_rows(d_cond_g[:, 0], N_PLANE)[0]
    sg = jax.nn.sigmoid(c_ctx)
    grads["c_ctx"] = d_silu * (sg * (1.0 + c_ctx * (1.0 - sg)))

    def shards_of(g, axis, j):
        layers = g if isinstance(g, (list, tuple)) else [g]
        ax = axis - 1 if isinstance(g, (list, tuple)) else axis
        n = layers[0].shape[ax] // N_PLANE
        return [lax.slice_in_dim(t, j * n, (j + 1) * n, axis=ax) for t in layers]

    send = jnp.stack([_pack([t for n, axis in _SHARDED + _SHARDED_SMALL for t in shards_of(dw[n], axis, j)], BF16)
                      for j in range(N_PLANE)])
    rows_h = send.shape[1] // 2
    send = send.reshape(N_PLANE, 2, rows_h, 1024)
    mine = lax.dynamic_index_in_dim(send, ac, 1, keepdims=False).reshape(N_PLANE * rows_h, 1024)
    theirs = sibling_halves(send).reshape(N_PLANE * rows_h, 1024)
    chip_sum = _accumulate([mine, theirs], BF16).reshape(N_PLANE, rows_h, 1024)
    pick = lambda k: lax.dynamic_index_in_dim(chip_sum, k, 0, keepdims=False)
    own, for_x, for_y, for_diag = pick(plane), pick(plane ^ 2), pick(plane ^ 1), pick(plane ^ 3)
    rows_q = rows_h // 2
    from_x, from_y = neighbour_exchange(for_diag[:rows_q], for_diag[rows_q:])
    zeros_q = jnp.zeros((rows_q, 1024), BF16)
    relayed = jnp.concatenate([zeros_q, from_y, from_x, zeros_q])
    merged = _accumulate([jnp.concatenate([for_x, for_y]), relayed], BF16)
    got_x, got_y = neighbour_exchange(merged[:rows_h], merged[rows_h:])
    done = _accumulate([own, got_x, got_y], BF16)
    both = jnp.stack([done, sibling_swap(done)])
    flat = jnp.where(ac == 0, both, both[::-1]).astype(F32).reshape(-1, 1024)
    shard_shapes = [weights[n].shape for n, _ in _SHARDED] + [weights[n].shape for n, _ in _SHARDED_SMALL]
    for (n, _), g in zip(_SHARDED + _SHARDED_SMALL, _unpack(flat, shard_shapes)):
        grads[n] = g

    big_names = ("w_mod",) + tuple(n for n, _ in _SHARDED)
    small_names = tuple(n for n in _WEIGHTS if n not in big_names)
    delta, new_m, new_v = {}, {}, {}
    for n in big_names:
        delta[n], new_m[n], new_v[n] = _adamw(weights[n], grads[n], mom_m[n], mom_v[n])
    sm_shapes = [weights[n].shape for n in small_names]
    packed = [_pack([src[n] for n in small_names], F32, cols=1024, row_mult=8)
              for src in (weights, grads, mom_m, mom_v)]
    for dst, res in zip((delta, new_m, new_v), _adamw(*packed)):
        dst.update(dict(zip(small_names, _unpack(res, sm_shapes))))

    return (loss, grad_x, *[grads[n] for n in _WEIGHTS], *[delta[n] for n in _WEIGHTS],
            *[new_m[n] for n in _WEIGHTS], *[new_v[n] for n in _WEIGHTS])
```

```python
import functools

import numpy as np
import jax
import jax.numpy as jnp
from jax import lax
from jax.experimental import pallas as pl
from jax.experimental.pallas import tpu as pltpu

F32 = jnp.float32
BF16 = jnp.bfloat16
HI = lax.Precision.HIGHEST
MESH = pl.DeviceIdType.MESH
ANY = pl.BlockSpec(memory_space=pl.ANY)
VMEM_SPEC = pl.BlockSpec(memory_space=pltpu.VMEM)

GRID_W = 64
HEAD_DIM = 64
ROPE_BASE = 10000.0
EPS = 1e-6
N_MOD = 6
GQA_Q_HEADS, GQA_KV_HEADS = 12, 4
GQA_Q_W, GQA_KV_W = GQA_Q_HEADS * HEAD_DIM, GQA_KV_HEADS * HEAD_DIM
SSM_WIDTH, SSM_GROUP, SSM_STATE = 256, 16, 64
SSM_GROUPS = SSM_WIDTH // SSM_GROUP
SSM_LANES = SSM_GROUPS * SSM_STATE
MLA_HEADS, MLA_Q_RANK, MLA_KV_RANK, MLA_NOPE, MLA_ROPE, MLA_V = 8, 512, 256, 64, 32, 64
MLA_QK = MLA_NOPE + MLA_ROPE
NA_HEADS, NA_WIN_R, NA_WIN_C = 8, 8, 16
NA_W = NA_HEADS * HEAD_DIM
NA_BAND = NA_WIN_R * GRID_W
ODD_IN_W = MLA_Q_RANK + MLA_KV_RANK + MLA_ROPE + 3 * NA_W
ODD_IN_PAD = 2560
ADAM_LR, ADAM_B1, ADAM_B2, ADAM_EPS, ADAM_WD, ADAM_STEP = 0.001, 0.9, 0.999, 1e-08, 0.01, 10
NEG = -1e30
VMEM_LIMIT = 56 * 1024 * 1024
LANE = 128
MM_TILE_M = (1152, 1024, 768, 512, 256, 128)
MM_TILE_N = (1280, 1024, 768, 512, 256, 128)
MM_TILE_K = (1152, 1024, 768, 512, 256, 128)
ROW_TILES = (576, 512, 384, 256, 128, 64)
N_PLANE = 4
N_DEV = 8


def _pick(n, cands):
    for c in cands:
        if n % c == 0:
            return c
    return n


def _params(**kw):
    return pltpu.CompilerParams(vmem_limit_bytes=VMEM_LIMIT, **kw)


def _mm(a, b, *, ta=False, tb=False, a_act=None, epi=None, e=None, exact=False, out_dtype=F32):
    m, kd = (a.shape[1], a.shape[0]) if ta else a.shape
    n = b.shape[0] if tb else b.shape[1]
    tm = _pick(m, MM_TILE_M)
    tn = _pick(n, MM_TILE_N)
    tk = _pick(kd, MM_TILE_K)
    nk = kd // tk
    dn = (((0 if ta else 1,), (1 if tb else 0,)), ((), ()))
    narrow = jnp.dtype(out_dtype) != jnp.dtype(F32)
    assert not (narrow and epi is not None)

    def body(*refs):
        if narrow:
            a_ref, b_ref, out_ref, o_ref = refs
        elif epi is None:
            a_ref, b_ref, o_ref = refs
        else:
            a_ref, b_ref, e_ref, o_ref = refs
        k = pl.program_id(2)
        av = a_ref[...]
        if a_act == "relu2":
            av = jnp.square(jnp.maximum(av, 0.0))
        elif a_act == "silu":
            av = av * jax.nn.sigmoid(av)
        bv = b_ref[...]
        if exact:
            p = lax.dot_general(av, bv, dn, precision=HI, preferred_element_type=F32)
        else:
            p = lax.dot_general(av.astype(BF16), bv.astype(BF16), dn, preferred_element_type=F32)

        @pl.when(k == 0)
        def _():
            o_ref[...] = p

        @pl.when(k > 0)
        def _():
            o_ref[...] += p

        if epi == "drelu2":
            @pl.when(k == nk - 1)
            def _():
                o_ref[...] = o_ref[...] * (2.0 * jnp.maximum(e_ref[...], 0.0))

        if narrow:
            @pl.when(k == nk - 1)
            def _():
                out_ref[...] = o_ref[...].astype(out_dtype)

    a_spec = pl.BlockSpec((tk, tm), lambda i, j, k: (k, i)) if ta else pl.BlockSpec((tm, tk), lambda i, j, k: (i, k))
    b_spec = pl.BlockSpec((tn, tk), lambda i, j, k: (j, k)) if tb else pl.BlockSpec((tk, tn), lambda i, j, k: (k, j))
    o_spec = pl.BlockSpec((tm, tn), lambda i, j, k: (i, j))
    ins, specs = [a, b], [a_spec, b_spec]
    if epi is not None:
        ins.append(e)
        specs.append(o_spec)
    name = f"mm_{m}x{kd}x{n}_{int(ta)}{int(tb)}_{a_act}_{epi}_{int(exact)}_{jnp.dtype(out_dtype).name}"
    return pl.pallas_call(
        body, out_shape=jax.ShapeDtypeStruct((m, n), out_dtype), grid=(m // tm, n // tn, nk),
        in_specs=specs, out_specs=o_spec, name=name, compiler_params=_params(),
        scratch_shapes=[pltpu.VMEM((tm, tn), F32)] if narrow else [],
    )(*ins)


@functools.partial(jax.custom_vjp, nondiff_argnums=(2,))
def _linear(a, w, exact):
    return _mm(a, w, exact=exact)


def _linear_fwd(a, w, exact):
    return _mm(a, w, exact=exact), (a, w)


def _linear_bwd(exact, res, g):
    a, w = res
    return _mm(g, w, tb=True, exact=exact), _mm(a, g, ta=True, exact=exact, out_dtype=w.dtype)


_linear.defvjp(_linear_fwd, _linear_bwd)


def linear(a, w, exact=False):
    return _linear(a, w, exact)


@jax.custom_vjp
def ffn(a, w1, w2):
    return _mm(_mm(a, w1), w2, a_act="relu2")


def _ffn_fwd(a, w1, w2):
    h1 = _mm(a, w1)
    return _mm(h1, w2, a_act="relu2"), (a, w1, w2, h1)


def _ffn_bwd(res, g):
    a, w1, w2, h1 = res
    dh1 = _mm(g, w2, tb=True, epi="drelu2", e=h1)
    dw2 = _mm(h1, g, ta=True, a_act="relu2", out_dtype=w2.dtype)
    return _mm(dh1, w1, tb=True), _mm(a, dh1, ta=True, out_dtype=w1.dtype), dw2


ffn.defvjp(_ffn_fwd, _ffn_bwd)


def make_rowwise(fn, name, kinds, out_dims, nctx_rows=0, whole_seq=False):
    n_in = len(kinds)
    n_out = len(out_dims)
    diff = [i for i, kd in enumerate(kinds) if kd in ("row", "glob", "seg")]
    seg_idx = [i for i, kd in enumerate(kinds) if kd == "seg"]

    def layout(args):
        row0 = args[kinds.index("row")]
        g, s = row0.shape[0], row0.shape[1]
        ts = s if whole_seq else _pick(s, ROW_TILES)
        return g, s, ts, 0

    def spec_of(kind, arr, ts, nctx):
        if kind == "row":
            return pl.BlockSpec((None, ts, arr.shape[2]), lambda g, i: (g, i, 0))
        if kind == "tab":
            return pl.BlockSpec((ts, arr.shape[1]), lambda g, i: (i, 0))
        if kind in ("const", "glob"):
            return pl.BlockSpec(arr.shape, lambda g, i: (0, 0))
        return pl.BlockSpec((None,) + arr.shape[1:], lambda g, i: (g, 0, 0, 0))

    def with_segments(ts):
        if not seg_idx:
            return fn

        def wrapped(*vals):
            rows = pl.program_id(1) * ts + lax.broadcasted_iota(jnp.int32, (ts, 1), 0)
            vals = list(vals)
            for idx in seg_idx:
                vals[idx] = jnp.where(rows < nctx_rows, vals[idx][0], vals[idx][1])
            return fn(*vals)

        return wrapped

    def fwd_call(*args):
        g, s, ts, nctx = layout(args)
        fn = with_segments(ts)

        def body(*refs):
            vals = [r[...] for r in refs[:n_in]]
            outs = fn(*vals)
            for o_ref, o in zip(refs[n_in:], outs):
                o_ref[...] = o

        return pl.pallas_call(
            body, out_shape=[jax.ShapeDtypeStruct((g, s, d), F32) for d in out_dims], grid=(g, s // ts),
            in_specs=[spec_of(kd, a, ts, nctx) for kd, a in zip(kinds, args)],
            out_specs=[pl.BlockSpec((None, ts, d), lambda g_, i: (g_, i, 0)) for d in out_dims],
            name=f"{name}_f_{g}x{s}", compiler_params=_params(),
        )(*args)

    def bwd_call(args, cts):
        g, s, ts, nctx = layout(args)
        fn = with_segments(ts)

        def body(*refs):
            in_refs, ct_refs, out_refs = refs[:n_in], refs[n_in:n_in + n_out], refs[n_in + n_out:]
            gi, i = pl.program_id(0), pl.program_id(1)
            vals = [r[...] for r in in_refs]

            def f(*dv):
                full = list(vals)
                for idx, v in zip(diff, dv):
                    full[idx] = v
                return tuple(fn(*full))

            _, vjp = jax.vjp(f, *[vals[idx] for idx in diff])
            grads = vjp(tuple(r[...] for r in ct_refs))
            for idx, o_ref, gr in zip(diff, out_refs, grads):
                if kinds[idx] == "row":
                    o_ref[...] = gr
                    continue
                if kinds[idx] == "glob":
                    first = jnp.logical_and(gi == 0, i == 0)
                else:
                    first = i == 0

                @pl.when(first)
                def _(o_ref=o_ref, gr=gr):
                    o_ref[...] = gr

                @pl.when(jnp.logical_not(first))
                def _(o_ref=o_ref, gr=gr):
                    o_ref[...] += gr

        in_specs = [spec_of(kd, a, ts, nctx) for kd, a in zip(kinds, args)]
        in_specs += [pl.BlockSpec((None, ts, d), lambda g_, i: (g_, i, 0)) for d in out_dims]
        return pl.pallas_call(
            body, out_shape=[jax.ShapeDtypeStruct(args[idx].shape, F32) for idx in diff], grid=(g, s // ts),
            in_specs=in_specs, out_specs=[spec_of(kinds[idx], args[idx], ts, nctx) for idx in diff],
            name=f"{name}_b_{g}x{s}", compiler_params=_params(),
        )(*args, *cts)

    @jax.custom_vjp
    def op(*args):
        return tuple(fwd_call(*args))

    def op_fwd(*args):
        return tuple(fwd_call(*args)), args

    def op_bwd(args, cts):
        grads = bwd_call(args, cts)
        full = [None] * n_in
        for idx, gr in zip(diff, grads):
            full[idx] = gr
        return tuple(jnp.zeros_like(a) if gfull is None else gfull for a, gfull in zip(args, full))

    op.defvjp(op_fwd, op_bwd)
    op.fwd_call, op.bwd_call = fwd_call, bwd_call
    return op


def make_modulate(d, n_ctx):
    one = make_rowwise(_fn_modulate, "modulate", ("row", "glob", "seg", "seg"), (d,), nctx_rows=n_ctx)
    two = make_rowwise(_fn_modulate_keep, "modulate_keep", ("row", "glob", "seg", "seg"), (d, d), nctx_rows=n_ctx)

    @jax.custom_vjp
    def op(x, g, shift, scale):
        return one.fwd_call(x, g, shift, scale)[0], x

    def fwd(x, g, shift, scale):
        return (one.fwd_call(x, g, shift, scale)[0], x), (x, g, shift, scale)

    def bwd(res, cts):
        return tuple(two.bwd_call(res, cts))

    op.defvjp(fwd, bwd)
    return op


def make_gated_add(d, n_ctx):
    add = make_rowwise(_fn_gated_add, "gated", ("row", "row", "seg"), (d,), nctx_rows=n_ctx)
    mul = make_rowwise(_fn_gate_mul, "gate_mul", ("row", "seg"), (d,), nctx_rows=n_ctx)

    @jax.custom_vjp
    def op(x, o, gate):
        return add.fwd_call(x, o, gate)[0]

    def fwd(x, o, gate):
        return add.fwd_call(x, o, gate)[0], (o, gate)

    def bwd(res, ct):
        do, dgate = mul.bwd_call(res, (ct,))
        return ct, do, dgate

    op.defvjp(fwd, bwd)
    return op


def _rms(x):
    return lax.rsqrt(jnp.mean(x * x, axis=-1, keepdims=True) + EPS)


def _fn_modulate(x, g, shift, scale):
    return ((x * _rms(x) * g) * (1.0 + scale) + shift,)


def _fn_modulate_keep(x, g, shift, scale):
    return _fn_modulate(x, g, shift, scale) + (x,)


def _fn_gated_add(x, o, gate):
    return (x + gate * o,)


def _fn_gate_mul(o, gate):
    return (gate * o,)


def _fn_norm(x, g):
    return (x * _rms(x) * g,)


def _fn_glu_pre(u, y0, y1, d):
    return (jax.nn.gelu(d * u + y0 + y1),)


def _fn_glu_post(z, t, bg):
    return (z * jax.nn.sigmoid(t + bg),)


def _rope_tables(n_ctx, n_lat, dh, start, rot_dim):
    t = jnp.arange(n_lat)
    rows = (t // GRID_W).astype(F32)
    cols = (t % GRID_W).astype(F32)
    axis_dim = rot_dim // 2
    freqs = ROPE_BASE ** (-jnp.arange(0, axis_dim, 2, dtype=F32) / axis_dim)
    ang_r = rows[:, None] * freqs
    ang_c = cols[:, None] * freqs
    ang = jnp.concatenate([ang_r, ang_r, ang_c, ang_c], axis=-1)
    cos = jnp.concatenate([jnp.ones((n_lat, start), F32), jnp.cos(ang)], axis=-1)
    sin = jnp.concatenate([jnp.zeros((n_lat, start), F32), jnp.sin(ang)], axis=-1)
    cos = jnp.concatenate([jnp.ones((n_ctx, dh), F32), cos], axis=0)
    sin = jnp.concatenate([jnp.zeros((n_ctx, dh), F32), sin], axis=0)
    return cos, sin


_NT = (((1,), (1,)), ((), ()))
_TN = (((0,), (0,)), ((), ()))


def _na_geometry(i, nc, rows):
    r = i - nc
    rs = jnp.clip(r - NA_WIN_R // 2, 0, rows - NA_WIN_R)
    is_ctx = i < nc
    cls = jnp.where(is_ctx, NA_WIN_R, r - rs)
    return jnp.where(is_ctx, 0, rs), cls


def _na_onehots():
    q = np.arange(GRID_W)[:, None]
    col = np.arange(GRID_W)[None, :]
    cs = np.clip(q - NA_WIN_C // 2, 0, GRID_W - NA_WIN_C)
    valid = (col >= cs) & (col < cs + NA_WIN_C)
    cidx = col - q + (NA_WIN_C - 1)
    n_b = 2 * NA_WIN_C - 1
    col_hot = np.zeros((LANE, GRID_W * GRID_W), np.float32)
    for qq in range(GRID_W):
        for cc in range(GRID_W):
            if valid[qq, cc]:
                col_hot[cidx[qq, cc], qq * GRID_W + cc] = 1.0
    row_hot = np.zeros((NA_WIN_R, NA_WIN_R, 2 * NA_WIN_R - 1), np.float32)
    for c in range(NA_WIN_R):
        for j in range(NA_WIN_R):
            row_hot[c, j, j - c + NA_WIN_R - 1] = 1.0
    mask = np.where(valid, 0.0, NEG).astype(np.float32)
    return col_hot, row_hot, mask, n_b


def na_bias_table(rpb):
    h = rpb.shape[0]
    col_hot, row_hot, mask, n_b = _na_onehots()
    t1 = jnp.einsum("cja,hab->hcjb", jnp.asarray(row_hot), rpb)
    t1 = jnp.pad(t1.reshape(h * NA_WIN_R * NA_WIN_R, n_b), ((0, 0), (0, LANE - n_b)))
    t2 = linear(t1, jnp.asarray(col_hot), True)
    t2 = t2.reshape(h, NA_WIN_R, NA_WIN_R, GRID_W, GRID_W) + jnp.asarray(mask)
    tab = jnp.transpose(t2, (0, 1, 3, 2, 4)).reshape(h, NA_WIN_R, GRID_W, NA_BAND)
    return jnp.concatenate([tab, jnp.full((h, 1, GRID_W, NA_BAND), NEG, F32)], axis=1)


def _first_step():
    return jnp.logical_and(pl.program_id(0) == 0, pl.program_id(1) == 0)


def _accum_out(ref, val, first):
    @pl.when(first)
    def _():
        ref[...] = val

    @pl.when(jnp.logical_not(first))
    def _():
        ref[...] += val


def _norm_head(xh, g):
    r = _rms(xh)
    yn = xh * r
    return yn * g, yn, r


def _norm_head_bwd(dy, yn, r, g):
    dg = jnp.sum(dy * yn, axis=0, keepdims=True)
    dyn = dy * g
    return r * (dyn - yn * jnp.mean(dyn * yn, axis=-1, keepdims=True)), dg


def _rope_signs(dh, start, rot_dim, n_heads):
    q = rot_dim // 4
    pos = np.arange(dh)
    quarter = (pos - start) // q
    inr = pos >= start
    sg = np.zeros((8, n_heads * dh), np.float32)
    sg[0] = np.tile(np.where(inr & (quarter % 2 == 0), -1.0, 0.0), n_heads)
    sg[1] = np.tile(np.where(inr & (quarter % 2 == 1), 1.0, 0.0), n_heads)
    return sg


def _rope_full(y, cos, sin, sg, q):
    w = y.shape[-1]
    rot = sg[0:1] * pltpu.roll(y, w - q, 1) + sg[1:2] * pltpu.roll(y, q, 1)
    return y * cos + rot * sin


def _rope_full_t(dy, cos, sin, sg, q):
    w = dy.shape[-1]
    z = dy * sin
    return dy * cos - sg[1:2] * pltpu.roll(z, q, 1) - sg[0:1] * pltpu.roll(z, w - q, 1)


def _hnr_call(x, g, cos, sin, sg, n_heads, q, dy=None):
    b, s, w = x.shape
    dh = w // n_heads
    ts = _pick(s, ROW_TILES)
    rope = cos is not None

    def body(*refs):
        refs = list(refs)
        x_ref, g_ref = refs[0], refs[1]
        k = 2
        if rope:
            cos_ref, sin_ref, sg_ref = refs[2], refs[3], refs[4]
            k = 5
        gv = g_ref[...]
        if dy is None:
            o_ref = refs[k]
            for h in range(n_heads):
                sl = slice(h * dh, (h + 1) * dh)
                o_ref[:, sl] = _norm_head(x_ref[:, sl], gv)[0]
            if rope:
                o_ref[...] = _rope_full(o_ref[...], cos_ref[...], sin_ref[...], sg_ref[...], q)
            return
        dy_ref, dx_ref, dg_ref = refs[k], refs[k + 1], refs[k + 2]
        src = dy_ref
        if rope:
            dx_ref[...] = _rope_full_t(dy_ref[...], cos_ref[...], sin_ref[...], sg_ref[...], q)
            src = dx_ref
        dg = jnp.zeros((1, dh), F32)
        for h in range(n_heads):
            sl = slice(h * dh, (h + 1) * dh)
            _, yn, r = _norm_head(x_ref[:, sl], gv)
            dxh, dgh = _norm_head_bwd(src[:, sl], yn, r, gv)
            dx_ref[:, sl] = dxh
            dg = dg + dgh
        _accum_out(dg_ref, dg, _first_step())

    row = pl.BlockSpec((None, ts, w), lambda bi, i: (bi, i, 0))
    whole = lambda a: pl.BlockSpec(a.shape, lambda bi, i: (0, 0))
    ins, specs = [x, g], [row, whole(g)]
    if rope:
        ins += [cos, sin, sg]
        specs += [pl.BlockSpec((ts, w), lambda bi, i: (i, 0)), pl.BlockSpec((ts, w), lambda bi, i: (i, 0)), whole(sg)]
    if dy is None:
        out_shape, out_specs = jax.ShapeDtypeStruct(x.shape, F32), row
    else:
        ins.append(dy)
        specs.append(row)
        out_shape = [jax.ShapeDtypeStruct(x.shape, F32), jax.ShapeDtypeStruct(g.shape, F32)]
        out_specs = [row, whole(g)]
    return pl.pallas_call(
        body, out_shape=out_shape, grid=(b, s // ts), in_specs=specs, out_specs=out_specs,
        name=f"hnr_{'b' if dy is not None else 'f'}_{n_heads}x{dh}_{int(rope)}", compiler_params=_params(),
    )(*ins)


@functools.partial(jax.custom_vjp, nondiff_argnums=(5, 6))
def head_norm_rope(x, g, cos, sin, sg, n_heads, q):
    return _hnr_call(x, g, cos, sin, sg, n_heads, q)


def _head_norm_rope_fwd(x, g, cos, sin, sg, n_heads, q):
    return _hnr_call(x, g, cos, sin, sg, n_heads, q), (x, g, cos, sin, sg)


def _head_norm_rope_bwd(n_heads, q, res, dy):
    x, g, cos, sin, sg = res
    dx, dg = _hnr_call(x, g, cos, sin, sg, n_heads, q, dy=dy)
    zero = lambda t: None if t is None else jnp.zeros_like(t)
    return dx, dg, zero(cos), zero(sin), zero(sg)


head_norm_rope.defvjp(_head_norm_rope_fwd, _head_norm_rope_bwd)


def _mla_k_call(kv, kr, g, cos, sin, sg, dkn=None):
    b, s, _ = kv.shape
    ts = _pick(s, ROW_TILES)
    hw = MLA_NOPE + MLA_V
    kn_w = MLA_HEADS * MLA_QK
    q = MLA_ROPE // 4

    def body(kv_ref, kr_ref, g_ref, cos_ref, sin_ref, sg_ref, *rest):
        gv = g_ref[...]
        krv = kr_ref[...]
        if dkn is None:
            (o_ref,) = rest
            for h in range(MLA_HEADS):
                kh = jnp.concatenate([kv_ref[:, h * hw:h * hw + MLA_NOPE], krv], axis=-1)
                o_ref[:, h * MLA_QK:(h + 1) * MLA_QK] = _norm_head(kh, gv)[0]
            o_ref[...] = _rope_full(o_ref[...], cos_ref[...], sin_ref[...], sg_ref[...], q)
            return
        dkn_ref, dkv_ref, dkr_ref, dg_ref, dy_ref = rest
        dy_ref[...] = _rope_full_t(dkn_ref[...], cos_ref[...], sin_ref[...], sg_ref[...], q)
        dg = jnp.zeros((1, MLA_QK), F32)
        dkr = jnp.zeros((ts, MLA_ROPE), F32)
        for h in range(MLA_HEADS):
            kh = jnp.concatenate([kv_ref[:, h * hw:h * hw + MLA_NOPE], krv], axis=-1)
            _, yn, r = _norm_head(kh, gv)
            dxh, dgh = _norm_head_bwd(dy_ref[:, h * MLA_QK:(h + 1) * MLA_QK], yn, r, gv)
            dkv_ref[:, h * hw:h * hw + MLA_NOPE] = dxh[:, :MLA_NOPE]
            dkv_ref[:, h * hw + MLA_NOPE:(h + 1) * hw] = jnp.zeros((ts, MLA_V), F32)
            dkr = dkr + dxh[:, MLA_NOPE:]
            dg = dg + dgh
        dkr_ref[...] = dkr
        _accum_out(dg_ref, dg, _first_step())

    row = lambda w: pl.BlockSpec((None, ts, w), lambda bi, i: (bi, i, 0))
    tab = pl.BlockSpec((ts, kn_w), lambda bi, i: (i, 0))
    whole = lambda a: pl.BlockSpec(a.shape, lambda bi, i: (0, 0))
    ins = [kv, kr, g, cos, sin, sg]
    specs = [row(kv.shape[2]), row(MLA_ROPE), whole(g), tab, tab, whole(sg)]
    scratch = []
    if dkn is None:
        out_shape, out_specs = jax.ShapeDtypeStruct((b, s, kn_w), F32), row(kn_w)
    else:
        ins.append(dkn)
        specs.append(row(kn_w))
        out_shape = [jax.ShapeDtypeStruct(kv.shape, F32), jax.ShapeDtypeStruct(kr.shape, F32),
                     jax.ShapeDtypeStruct(g.shape, F32)]
        out_specs = [row(kv.shape[2]), row(MLA_ROPE), whole(g)]
        scratch = [pltpu.VMEM((ts, kn_w), F32)]
    return pl.pallas_call(
        body, out_shape=out_shape, grid=(b, s // ts), in_specs=specs, out_specs=out_specs, scratch_shapes=scratch,
        name=f"mla_k_{'b' if dkn is not None else 'f'}", compiler_params=_params(),
    )(*ins)


@jax.custom_vjp
def mla_k_prep(kv, kr, g, cos, sin, sg):
    return _mla_k_call(kv, kr, g, cos, sin, sg)


def _mla_k_prep_fwd(kv, kr, g, cos, sin, sg):
    return _mla_k_call(kv, kr, g, cos, sin, sg), (kv, kr, g, cos, sin, sg)


def _mla_k_prep_bwd(res, dkn):
    kv, kr, g, cos, sin, sg = res
    dkv, dkr, dg = _mla_k_call(kv, kr, g, cos, sin, sg, dkn=dkn)
    return dkv, dkr, dg, jnp.zeros_like(cos), jnp.zeros_like(sin), jnp.zeros_like(sg)


mla_k_prep.defvjp(_mla_k_prep_fwd, _mla_k_prep_bwd)


class _HeadLayout:
    def __init__(self, groups, dq, dv, q_off, k_off, v_off, o_off, wq, wk, wv, wo, scale):
        self.groups, self.dq, self.dv, self.scale = groups, dq, dv, scale
        self.q_off, self.k_off, self.v_off, self.o_off = q_off, k_off, v_off, o_off
        self.wq, self.wk, self.wv, self.wo = wq, wk, wv, wo
        self.n_h = len(q_off)


def _gqa_layout():
    rep = GQA_Q_HEADS // GQA_KV_HEADS
    n_h = GQA_Q_HEADS // 2
    return _HeadLayout(2, HEAD_DIM, HEAD_DIM, [h * HEAD_DIM for h in range(n_h)], [(h // rep) * HEAD_DIM for h in range(n_h)],
                       [(h // rep) * HEAD_DIM for h in range(n_h)], [h * HEAD_DIM for h in range(n_h)],
                       n_h * HEAD_DIM, (n_h // rep) * HEAD_DIM, (n_h // rep) * HEAD_DIM, n_h * HEAD_DIM, HEAD_DIM ** -0.5)


def _mla_layout():
    n_h = MLA_HEADS // 2
    hw = MLA_NOPE + MLA_V
    return _HeadLayout(2, MLA_QK, MLA_V, [h * MLA_QK for h in range(n_h)], [h * MLA_QK for h in range(n_h)],
                       [h * hw + MLA_NOPE for h in range(n_h)], [h * MLA_V for h in range(n_h)],
                       n_h * MLA_QK, n_h * MLA_QK, n_h * hw, n_h * MLA_V, MLA_QK ** -0.5)


def _attn_tm_fwd(q, k, v, lay, n_ctx):
    b, s, _ = q.shape
    tq = min(256, n_ctx)
    nc = n_ctx // tq

    def body(q_ref, k_ref, v_ref, o_ref, lse_ref):
        def run(n_keys):
            for h in range(lay.n_h):
                qo, ko, vo, oo = lay.q_off[h], lay.k_off[h], lay.v_off[h], lay.o_off[h]
                qv = (q_ref[:, qo:qo + lay.dq] * lay.scale).astype(BF16)
                sc = lax.dot_general(qv, k_ref[0:n_keys, ko:ko + lay.dq].astype(BF16), _NT, preferred_element_type=F32)
                m = jnp.max(sc, axis=-1, keepdims=True)
                p = jnp.exp(sc - m)
                l = jnp.sum(p, axis=-1, keepdims=True)
                o = jnp.dot(p.astype(BF16), v_ref[0:n_keys, vo:vo + lay.dv].astype(BF16), preferred_element_type=F32)
                o_ref[:, oo:oo + lay.dv] = o / l
                lse_ref[:, h:h + 1] = m + jnp.log(l)

        pl.when(pl.program_id(2) < nc)(lambda: run(n_ctx))
        pl.when(pl.program_id(2) >= nc)(lambda: run(s))

    return pl.pallas_call(
        body, out_shape=[jax.ShapeDtypeStruct((b, s, lay.groups * lay.wo), F32),
                         jax.ShapeDtypeStruct((b, lay.groups, s, lay.n_h), F32)],
        grid=(b, lay.groups, s // tq),
        in_specs=[pl.BlockSpec((None, tq, lay.wq), lambda bi, g, i: (bi, i, g)),
                  pl.BlockSpec((None, s, lay.wk), lambda bi, g, i: (bi, 0, g)),
                  pl.BlockSpec((None, s, lay.wv), lambda bi, g, i: (bi, 0, g))],
        out_specs=[pl.BlockSpec((None, tq, lay.wo), lambda bi, g, i: (bi, i, g)),
                   pl.BlockSpec((None, None, tq, lay.n_h), lambda bi, g, i: (bi, g, i, 0))],
        name=f"attn_tm_f_{lay.dq}", compiler_params=_params(),
    )(q, k, v)


def _attn_tm_bwd(q, k, v, lse, o, do, lay, n_ctx):
    b, s, _ = q.shape
    tk = min(256, n_ctx)
    nc = n_ctx // tk

    def body(q_ref, k_ref, v_ref, lse_ref, o_ref, do_ref, dq_ref, dk_ref, dv_ref, delta_ref):
        @pl.when(pl.program_id(2) == 0)
        def _():
            dq_ref[...] = jnp.zeros_like(dq_ref)
            for h in range(lay.n_h):
                oo = lay.o_off[h]
                delta_ref[:, h:h + 1] = jnp.sum(o_ref[:, oo:oo + lay.dv] * do_ref[:, oo:oo + lay.dv], axis=-1,
                                                keepdims=True)

        def run(r0):
            dk_acc, dv_acc = {}, {}
            for h in range(lay.n_h):
                qo, ko, vo, oo = lay.q_off[h], lay.k_off[h], lay.v_off[h], lay.o_off[h]
                kh = k_ref[:, ko:ko + lay.dq].astype(BF16)
                vh = v_ref[:, vo:vo + lay.dv].astype(BF16)
                qv = (q_ref[r0:s, qo:qo + lay.dq] * lay.scale).astype(BF16)
                dob = do_ref[r0:s, oo:oo + lay.dv].astype(BF16)
                sc = lax.dot_general(qv, kh, _NT, preferred_element_type=F32)
                p = jnp.exp(sc - lse_ref[r0:s, h:h + 1])
                dvh = lax.dot_general(p.astype(BF16), dob, _TN, preferred_element_type=F32)
                dp = lax.dot_general(dob, vh, _NT, preferred_element_type=F32)
                dsb = (p * (dp - delta_ref[r0:s, h:h + 1])).astype(BF16)
                dkh = lax.dot_general(dsb, qv, _TN, preferred_element_type=F32)
                dq_ref[r0:s, qo:qo + lay.dq] += jnp.dot(dsb, kh, preferred_element_type=F32) * lay.scale
                dk_acc[ko] = dkh if ko not in dk_acc else dk_acc[ko] + dkh
                dv_acc[vo] = dvh if vo not in dv_acc else dv_acc[vo] + dvh
            if len(dv_acc) * lay.dv != lay.wv:
                dv_ref[...] = jnp.zeros_like(dv_ref)
            for ko, val in dk_acc.items():
                dk_ref[:, ko:ko + lay.dq] = val
            for vo, val in dv_acc.items():
                dv_ref[:, vo:vo + lay.dv] = val

        pl.when(pl.program_id(2) < nc)(lambda: run(0))
        pl.when(pl.program_id(2) >= nc)(lambda: run(n_ctx))

    full = lambda w: pl.BlockSpec((None, s, w), lambda bi, g, j: (bi, 0, g))
    blk = lambda w: pl.BlockSpec((None, tk, w), lambda bi, g, j: (bi, j, g))
    stat = pl.BlockSpec((None, None, s, lay.n_h), lambda bi, g, j: (bi, g, 0, 0))
    return pl.pallas_call(
        body, out_shape=[jax.ShapeDtypeStruct(q.shape, F32), jax.ShapeDtypeStruct(k.shape, F32),
                         jax.ShapeDtypeStruct(v.shape, F32)],
        grid=(b, lay.groups, s // tk),
        in_specs=[full(lay.wq), blk(lay.wk), blk(lay.wv), stat, full(lay.wo), full(lay.wo)],
        out_specs=[full(lay.wq), blk(lay.wk), blk(lay.wv)],
        scratch_shapes=[pltpu.VMEM((s, lay.n_h), F32)],
        name=f"attn_tm_b_{lay.dq}", compiler_params=_params(),
    )(q, k, v, lse, o, do)


def _make_attention_tm(lay):
    @functools.partial(jax.custom_vjp, nondiff_argnums=(3,))
    def op(q, k, v, n_ctx):
        return _attn_tm_fwd(q, k, v, lay, n_ctx)[0]

    def fwd(q, k, v, n_ctx):
        o, lse = _attn_tm_fwd(q, k, v, lay, n_ctx)
        return o, (q, k, v, o, lse)

    def bwd(n_ctx, res, do):
        q, k, v, o, lse = res
        return _attn_tm_bwd(q, k, v, lse, o, do, lay, n_ctx)

    op.defvjp(fwd, bwd)
    return op


gqa_attention = _make_attention_tm(_gqa_layout())
mla_attention = _make_attention_tm(_mla_layout())

NA_GROUPS_FWD = 1
NA_GROUPS_BWD = 2


def _na_tm_specs(s, nc, rows, groups):
    hg = NA_HEADS // groups
    w = hg * HEAD_DIM
    qs = pl.BlockSpec((None, GRID_W, w), lambda bi, g, i: (bi, i, g))
    ks = pl.BlockSpec((None, s, w), lambda bi, g, i: (bi, 0, g))
    bs = pl.BlockSpec((hg, None, GRID_W, NA_BAND), lambda bi, g, i: (g, _na_geometry(i, nc, rows)[1], 0, 0))
    ls = pl.BlockSpec((None, None, GRID_W, hg), lambda bi, g, i: (bi, g, i, 0))
    return hg, w, qs, ks, bs, ls


def _na_tm_scores(q_ref, k_ref, bias_ref, hd, n_ctx, start, scale):
    sl = slice(hd * HEAD_DIM, (hd + 1) * HEAD_DIM)
    qv = (q_ref[:, sl] * scale).astype(BF16)
    kc = k_ref[0:n_ctx, sl].astype(BF16)
    kb = k_ref[pl.ds(start, NA_BAND), sl].astype(BF16)
    s_c = lax.dot_general(qv, kc, _NT, preferred_element_type=F32)
    s_l = lax.dot_general(qv, kb, _NT, preferred_element_type=F32) + bias_ref[hd]
    return sl, qv, kc, kb, s_c, s_l


def _na_tm_fwd(q, k, v, bias, n_ctx):
    b, s, _ = q.shape
    nc = n_ctx // GRID_W
    rows = (s - n_ctx) // GRID_W
    scale = HEAD_DIM ** -0.5
    hg, w, qs, ks, bs, ls = _na_tm_specs(s, nc, rows, NA_GROUPS_FWD)

    def body(q_ref, k_ref, v_ref, bias_ref, o_ref, lse_ref):
        rs, _ = _na_geometry(pl.program_id(2), nc, rows)
        start = pl.multiple_of(n_ctx + rs * GRID_W, GRID_W)
        for hd in range(hg):
            sl, _, _, _, s_c, s_l = _na_tm_scores(q_ref, k_ref, bias_ref, hd, n_ctx, start, scale)
            m = jnp.maximum(jnp.max(s_c, axis=-1, keepdims=True), jnp.max(s_l, axis=-1, keepdims=True))
            p_c = jnp.exp(s_c - m)
            p_l = jnp.exp(s_l - m)
            l = jnp.sum(p_c, axis=-1, keepdims=True) + jnp.sum(p_l, axis=-1, keepdims=True)
            o = jnp.dot(p_c.astype(BF16), v_ref[0:n_ctx, sl].astype(BF16), preferred_element_type=F32)
            o = o + jnp.dot(p_l.astype(BF16), v_ref[pl.ds(start, NA_BAND), sl].astype(BF16), preferred_element_type=F32)
            o_ref[:, sl] = o / l
            lse_ref[:, hd:hd + 1] = m + jnp.log(l)

    return pl.pallas_call(
        body, out_shape=[jax.ShapeDtypeStruct(q.shape, F32), jax.ShapeDtypeStruct((b, NA_GROUPS_FWD, s, hg), F32)],
        grid=(b, NA_GROUPS_FWD, s // GRID_W), in_specs=[qs, ks, ks, bs], out_specs=[qs, ls],
        name=f"na_tm_f_{s}", compiler_params=_params(),
    )(q, k, v, bias)


def _na_tm_bwd(q, k, v, bias, o, lse, do, n_ctx):
    b, s, _ = q.shape
    nc = n_ctx // GRID_W
    rows = (s - n_ctx) // GRID_W
    scale = HEAD_DIM ** -0.5
    n_cls = NA_WIN_R + 1
    hg, w, qs, ks, bs, ls = _na_tm_specs(s, nc, rows, NA_GROUPS_BWD)
    lse = jnp.transpose(lse, (0, 2, 1, 3)).reshape(b, s, NA_GROUPS_BWD, hg)
    lse = jnp.transpose(lse, (0, 2, 1, 3))

    def body(q_ref, k_ref, v_ref, bias_ref, o_ref, lse_ref, do_ref, dq_ref, dk_ref, dv_ref, db_ref):
        i = pl.program_id(2)
        rs, cls = _na_geometry(i, nc, rows)
        _, cls_prev = _na_geometry(i - 1, nc, rows)
        start = pl.multiple_of(n_ctx + rs * GRID_W, GRID_W)
        first = jnp.logical_or(i == 0, cls != cls_prev)

        @pl.when(i == 0)
        def _():
            dk_ref[...] = jnp.zeros_like(dk_ref)
            dv_ref[...] = jnp.zeros_like(dv_ref)

        @pl.when(first)
        def _():
            db_ref[...] = jnp.zeros_like(db_ref)

        for hd in range(hg):
            sl, qv, kc, kb, s_c, s_l = _na_tm_scores(q_ref, k_ref, bias_ref, hd, n_ctx, start, scale)
            lse_v = lse_ref[:, hd:hd + 1]
            p_c = jnp.exp(s_c - lse_v)
            p_l = jnp.exp(s_l - lse_v)
            dov = do_ref[:, sl]
            dob = dov.astype(BF16)
            delta = jnp.sum(dov * o_ref[:, sl], axis=-1, keepdims=True)
            vc = v_ref[0:n_ctx, sl].astype(BF16)
            vb = v_ref[pl.ds(start, NA_BAND), sl].astype(BF16)
            ds_c = p_c * (lax.dot_general(dob, vc, _NT, preferred_element_type=F32) - delta)
            ds_l = p_l * (lax.dot_general(dob, vb, _NT, preferred_element_type=F32) - delta)
            dsc_b = ds_c.astype(BF16)
            dsl_b = ds_l.astype(BF16)
            dq_ref[:, sl] = (jnp.dot(dsc_b, kc, preferred_element_type=F32)
                             + jnp.dot(dsl_b, kb, preferred_element_type=F32)) * scale
            dk_ref[0:n_ctx, sl] += lax.dot_general(dsc_b, qv, _TN, preferred_element_type=F32)
            dk_ref[pl.ds(start, NA_BAND), sl] += lax.dot_general(dsl_b, qv, _TN, preferred_element_type=F32)
            dv_ref[0:n_ctx, sl] += lax.dot_general(p_c.astype(BF16), dob, _TN, preferred_element_type=F32)
            dv_ref[pl.ds(start, NA_BAND), sl] += lax.dot_general(p_l.astype(BF16), dob, _TN, preferred_element_type=F32)
            db_ref[hd] += ds_l

    dbs = pl.BlockSpec((None, hg, None, GRID_W, NA_BAND), lambda bi, g, i: (bi, g, _na_geometry(i, nc, rows)[1], 0, 0))
    return pl.pallas_call(
        body,
        out_shape=[jax.ShapeDtypeStruct(q.shape, F32), jax.ShapeDtypeStruct(q.shape, F32), jax.ShapeDtypeStruct(q.shape, F32),
                   jax.ShapeDtypeStruct((b, NA_HEADS, n_cls, GRID_W, NA_BAND), F32)],
        grid=(b, NA_GROUPS_BWD, s // GRID_W), in_specs=[qs, ks, ks, bs, qs, ls, qs], out_specs=[qs, ks, ks, dbs],
        name=f"na_tm_b_{s}", compiler_params=_params(),
    )(q, k, v, bias, o, lse, do)


@functools.partial(jax.custom_vjp, nondiff_argnums=(4,))
def na_attention_tm(q, k, v, bias, n_ctx):
    return _na_tm_fwd(q, k, v, bias, n_ctx)[0]


def _na_attention_tm_fwd(q, k, v, bias, n_ctx):
    o, lse = _na_tm_fwd(q, k, v, bias, n_ctx)
    return o, (q, k, v, bias, o, lse)


def _na_attention_tm_bwd(n_ctx, res, do):
    q, k, v, bias, o, lse = res
    dq, dk, dv, db = _na_tm_bwd(q, k, v, bias, o, lse, do, n_ctx)
    return dq, dk, dv, _sum_rows(db.reshape(db.shape[0], -1, NA_BAND), db.shape[0]).reshape(db.shape[1:])


na_attention_tm.defvjp(_na_attention_tm_fwd, _na_attention_tm_bwd)


def _cmul(ar, ai, br, bi):
    return ar * br - ai * bi, ar * bi + ai * br


def _s5_chunk(n_ctx):
    return min(256, n_ctx)


def _s5_powers(a_re, a_im, t_len):
    exps = np.concatenate([np.minimum(2 ** np.arange(8), t_len), np.arange(1, 9), [t_len] + [0] * 7,
                           np.arange(0, t_len, 8)]).astype(np.float32)
    a_re, a_im = lax.stop_gradient(a_re), lax.stop_gradient(a_im)
    mag = jnp.sqrt(a_re * a_re + a_im * a_im)
    th = jnp.arctan2(a_im, a_re)
    t = jnp.asarray(exps)[:, None]
    pm = jnp.where(t == 0, 1.0, jnp.exp(t * jnp.log(jnp.maximum(mag, 1e-37))) * (mag > 0))
    return jnp.stack([pm * jnp.cos(t * th), pm * jnp.sin(t * th)])


def _s5_tables(pw, t_len, rev, conj=False):
    if conj:
        pw = pw * jnp.asarray([1.0, -1.0], F32)[:, None, None]
    if rev:
        pw = jnp.concatenate([pw[:, :8], pw[:, 8:16][:, ::-1], pw[:, 16:24], pw[:, 24:][:, ::-1]], axis=1)
    return pw


def _scan_chunk(x_re, x_im, tab_ref, hin_re, hin_im, rev, t_len, xs_ref, es_ref):
    outs = [_scan_slab(x_re[:, k:k + LANE], x_im[:, k:k + LANE], tab_ref, hin_re[:, k:k + LANE], hin_im[:, k:k + LANE],
                       rev, t_len, xs_ref, es_ref, k) for k in range(0, x_re.shape[-1], LANE)]
    return tuple(jnp.concatenate([o[t] for o in outs], axis=-1) for t in range(4))


def _scan_slab(x_re, x_im, tab_ref, hin_re, hin_im, rev, t_len, xs_ref, es_ref, k0):
    lanes = LANE
    n2 = t_len // 8
    tab_ref = tab_ref.at[:, :, k0:k0 + LANE]
    rin = lax.broadcasted_iota(jnp.int32, (t_len, lanes), 0) & 7
    for li, sh in enumerate((1, 2, 4)):
        m_re, m_im = tab_ref[0, li:li + 1, :], tab_ref[1, li:li + 1, :]
        amt = sh if not rev else t_len - sh
        c_re, c_im = _cmul(m_re, m_im, pltpu.roll(x_re, amt, 0), pltpu.roll(x_im, amt, 0))
        ok = (rin >= sh) if not rev else (rin < 8 - sh)
        x_re = x_re + jnp.where(ok, c_re, 0.0)
        x_im = x_im + jnp.where(ok, c_im, 0.0)
    xr_ref, xi_ref = xs_ref
    xr_ref[...] = x_re
    xi_ref[...] = x_im
    off = 0 if rev else 7
    e_re = xr_ref[pl.ds(off, n2, stride=8), :]
    e_im = xi_ref[pl.ds(off, n2, stride=8), :]
    row2 = lax.broadcasted_iota(jnp.int32, (n2, lanes), 0)
    sh, li = 1, 3
    while sh < n2:
        m_re, m_im = tab_ref[0, li:li + 1, :], tab_ref[1, li:li + 1, :]
        amt = sh if not rev else n2 - sh
        c_re, c_im = _cmul(m_re, m_im, pltpu.roll(e_re, amt, 0), pltpu.roll(e_im, amt, 0))
        ok = (row2 >= sh) if not rev else (row2 < n2 - sh)
        e_re = e_re + jnp.where(ok, c_re, 0.0)
        e_im = e_im + jnp.where(ok, c_im, 0.0)
        sh, li = sh * 2, li + 1
    es_ref[0] = e_re
    es_ref[1] = e_im
    last = 0 if rev else n2 - 1
    t_re, t_im = _cmul(tab_ref[0, 16:17, :], tab_ref[1, 16:17, :], hin_re, hin_im)
    hout_re = es_ref[0, last:last + 1, :] + t_re
    hout_im = es_ref[1, last:last + 1, :] + t_im
    amt = 1 if not rev else n2 - 1
    ok = (row2 >= 1) if not rev else (row2 < n2 - 1)
    k_re, k_im = _cmul(tab_ref[0, 24:24 + n2, :], tab_ref[1, 24:24 + n2, :], hin_re, hin_im)
    c_re = jnp.where(ok, pltpu.roll(e_re, amt, 0), 0.0) + k_re
    c_im = jnp.where(ok, pltpu.roll(e_im, amt, 0), 0.0) + k_im
    tp_re, tp_im = tab_ref[0, 8:16, :][None], tab_ref[1, 8:16, :][None]
    add_re, add_im = _cmul(tp_re, tp_im, c_re[:, None, :], c_im[:, None, :])
    h_re = xr_ref[...] + add_re.reshape(t_len, lanes)
    h_im = xi_ref[...] + add_im.reshape(t_len, lanes)
    return h_re, h_im, hout_re, hout_im


def _s5_order(j, n_chunks, nc, rev):
    if not rev:
        return j
    return jnp.where(j < nc, nc - 1 - j, n_chunks - 1 - (j - nc))


def _s5_fwd(u, tab, b_bd, c_bd, n_ctx, rev):
    b, s, w = u.shape
    lanes = b_bd.shape[-1]
    t_len = _s5_chunk(n_ctx)
    n_chunks, nc = s // t_len, n_ctx // t_len

    def body(u_ref, tab_ref, b_ref, c_ref, y_ref, h_ref, hin_ref, carry_ref, xr_ref, xi_ref, es_ref):
        xs_ref = (xr_ref, xi_ref)

        @pl.when(pl.program_id(1) == 0)
        def _():
            carry_ref[...] = jnp.zeros_like(carry_ref)

        ub = u_ref[...].astype(BF16)
        x_re = jnp.dot(ub, b_ref[0].astype(BF16), preferred_element_type=F32)
        x_im = jnp.dot(ub, b_ref[1].astype(BF16), preferred_element_type=F32)
        hin_re, hin_im = carry_ref[0, 0:1, :], carry_ref[1, 0:1, :]
        hin_ref[...] = carry_ref[...]
        h_re, h_im, ho_re, ho_im = _scan_chunk(x_re, x_im, tab_ref, hin_re, hin_im, rev, t_len, xs_ref, es_ref)
        carry_ref[0] = jnp.broadcast_to(ho_re, (8, lanes))
        carry_ref[1] = jnp.broadcast_to(ho_im, (8, lanes))
        h_ref[0] = h_re
        h_ref[1] = h_im
        y_ref[...] = (jnp.dot(h_re.astype(BF16), c_ref[0].astype(BF16), preferred_element_type=F32)
                      - jnp.dot(h_im.astype(BF16), c_ref[1].astype(BF16), preferred_element_type=F32))

    order = lambda j: _s5_order(j, n_chunks, nc, rev)
    whole = lambda arr: pl.BlockSpec(arr.shape, lambda bi, j: (0,) * arr.ndim)
    return pl.pallas_call(
        body,
        out_shape=[jax.ShapeDtypeStruct((b, s, w), F32), jax.ShapeDtypeStruct((2, b, s, lanes), F32),
                   jax.ShapeDtypeStruct((2, b, n_chunks, 8, lanes), F32)],
        grid=(b, n_chunks),
        in_specs=[pl.BlockSpec((None, t_len, w), lambda bi, j: (bi, order(j), 0)), whole(tab), whole(b_bd), whole(c_bd)],
        out_specs=[pl.BlockSpec((None, t_len, w), lambda bi, j: (bi, order(j), 0)),
                   pl.BlockSpec((2, None, t_len, lanes), lambda bi, j: (0, bi, order(j), 0)),
                   pl.BlockSpec((2, None, None, 8, lanes), lambda bi, j: (0, bi, order(j), 0, 0))],
        scratch_shapes=[pltpu.VMEM((2, 8, lanes), F32), pltpu.VMEM((t_len, LANE), F32), pltpu.VMEM((t_len, LANE), F32),
                        pltpu.VMEM((2, t_len // 8, LANE), F32)],
        name=f"s5_f_{s}_{int(rev)}", compiler_params=_params(),
    )(u, tab, b_bd, c_bd)


def _s5_bwd(u, tab_adj, b_bd, c_bd, h, hin, dy, n_ctx, rev):
    b, s, w = u.shape
    lanes = b_bd.shape[-1]
    t_len = _s5_chunk(n_ctx)
    n_chunks, nc = s // t_len, n_ctx // t_len
    arev = not rev

    def body(u_ref, tab_ref, b_ref, c_ref, h_ref, hin_ref, dy_ref, du_ref, db_ref, dc_ref, da_ref,
             carry_ref, xr_ref, xi_ref, es_ref):
        xs_ref = (xr_ref, xi_ref)
        first = jnp.logical_and(pl.program_id(0) == 0, pl.program_id(1) == 0)

        @pl.when(pl.program_id(1) == 0)
        def _():
            carry_ref[...] = jnp.zeros_like(carry_ref)

        dyv = dy_ref[...]
        dyb = dyv.astype(BF16)
        dn = (((1,), (1,)), ((), ()))
        dt = (((0,), (0,)), ((), ()))
        x_re = lax.dot_general(dyb, c_ref[0].astype(BF16), dn, preferred_element_type=F32)
        x_im = -lax.dot_general(dyb, c_ref[1].astype(BF16), dn, preferred_element_type=F32)
        g_re, g_im, go_re, go_im = _scan_chunk(x_re, x_im, tab_ref, carry_ref[0, 0:1, :], carry_ref[1, 0:1, :],
                                               arev, t_len, xs_ref, es_ref)
        carry_ref[0] = jnp.broadcast_to(go_re, (8, lanes))
        carry_ref[1] = jnp.broadcast_to(go_im, (8, lanes))
        h_re, h_im = h_ref[0], h_ref[1]
        gb_re, gb_im = g_re.astype(BF16), g_im.astype(BF16)
        du_ref[...] = (lax.dot_general(gb_re, b_ref[0].astype(BF16), dn, preferred_element_type=F32)
                       + lax.dot_general(gb_im, b_ref[1].astype(BF16), dn, preferred_element_type=F32))
        ub = u_ref[...].astype(BF16)
        db_re = lax.dot_general(ub, gb_re, dt, preferred_element_type=F32)
        db_im = lax.dot_general(ub, gb_im, dt, preferred_element_type=F32)
        dc_re = lax.dot_general(h_re.astype(BF16), dyb, dt, preferred_element_type=F32)
        dc_im = -lax.dot_general(h_im.astype(BF16), dyb, dt, preferred_element_type=F32)
        row = lax.broadcasted_iota(jnp.int32, (t_len, lanes), 0)
        amt = 1 if not rev else t_len - 1
        edge = (row == 0) if not rev else (row == t_len - 1)
        hp_re = jnp.where(edge, hin_ref[0, 0:1, :], pltpu.roll(h_re, amt, 0))
        hp_im = jnp.where(edge, hin_ref[1, 0:1, :], pltpu.roll(h_im, amt, 0))
        da_re = jnp.sum(g_re * hp_re + g_im * hp_im, axis=0, keepdims=True)
        da_im = jnp.sum(g_im * hp_re - g_re * hp_im, axis=0, keepdims=True)

        @pl.when(first)
        def _():
            db_ref[0], db_ref[1] = db_re, db_im
            dc_ref[0], dc_ref[1] = dc_re, dc_im
            da_ref[0] = jnp.broadcast_to(da_re, (8, lanes))
            da_ref[1] = jnp.broadcast_to(da_im, (8, lanes))

        @pl.when(jnp.logical_not(first))
        def _():
            db_ref[0] += db_re
            db_ref[1] += db_im
            dc_ref[0] += dc_re
            dc_ref[1] += dc_im
            da_ref[0] += jnp.broadcast_to(da_re, (8, lanes))
            da_ref[1] += jnp.broadcast_to(da_im, (8, lanes))

    order = lambda j: _s5_order(n_chunks - 1 - j, n_chunks, nc, rev)
    whole = lambda arr: pl.BlockSpec(arr.shape, lambda bi, j: (0,) * arr.ndim)
    us = pl.BlockSpec((None, t_len, w), lambda bi, j: (bi, order(j), 0))
    return pl.pallas_call(
        body,
        out_shape=[jax.ShapeDtypeStruct((b, s, w), F32), jax.ShapeDtypeStruct(b_bd.shape, F32),
                   jax.ShapeDtypeStruct(c_bd.shape, F32), jax.ShapeDtypeStruct((2, 8, lanes), F32)],
        grid=(b, n_chunks),
        in_specs=[us, whole(tab_adj), whole(b_bd), whole(c_bd),
                  pl.BlockSpec((2, None, t_len, lanes), lambda bi, j: (0, bi, order(j), 0)),
                  pl.BlockSpec((2, None, None, 8, lanes), lambda bi, j: (0, bi, order(j), 0, 0)), us],
        out_specs=[us, whole(b_bd), whole(c_bd), pl.BlockSpec((2, 8, lanes), lambda bi, j: (0, 0, 0))],
        scratch_shapes=[pltpu.VMEM((2, 8, lanes), F32), pltpu.VMEM((t_len, LANE), F32), pltpu.VMEM((t_len, LANE), F32),
                        pltpu.VMEM((2, t_len // 8, LANE), F32)],
        name=f"s5_b_{s}_{int(rev)}", compiler_params=_params(),
    )(u, tab_adj, b_bd, c_bd, h, hin, dy)


@functools.partial(jax.custom_vjp, nondiff_argnums=(4, 5))
def s5_direction(u, a, b_bd, c_bd, n_ctx, rev):
    t_len = _s5_chunk(n_ctx)
    return _s5_fwd(u, _s5_tables(_s5_powers(a[0], a[1], t_len), t_len, rev), b_bd, c_bd, n_ctx, rev)[0]


def _s5_direction_fwd(u, a, b_bd, c_bd, n_ctx, rev):
    t_len = _s5_chunk(n_ctx)
    pw = _s5_powers(a[0], a[1], t_len)
    y, h, hin = _s5_fwd(u, _s5_tables(pw, t_len, rev), b_bd, c_bd, n_ctx, rev)
    return y, (u, pw, b_bd, c_bd, h, hin)


def _s5_direction_bwd(n_ctx, rev, res, dy):
    u, pw, b_bd, c_bd, h, hin = res
    tab_adj = _s5_tables(pw, _s5_chunk(n_ctx), not rev, conj=True)
    du, db, dc, da = _s5_bwd(u, tab_adj, b_bd, c_bd, h, hin, dy, n_ctx, rev)
    return du, da[:, 0, :], db, dc


s5_direction.defvjp(_s5_direction_fwd, _s5_direction_bwd)


def _s5_discretize(lam_re, lam_im, log_dt, b_re, b_im):
    dt = jnp.exp(log_dt)[:, None]
    mag = jnp.exp(lam_re * dt)
    a_re = mag * jnp.cos(lam_im * dt)
    a_im = mag * jnp.sin(lam_im * dt)
    den = jnp.square(lam_re) + jnp.square(lam_im)
    f_re = ((a_re - 1.0) * lam_re + a_im * lam_im) / den
    f_im = (a_im * lam_re - (a_re - 1.0) * lam_im) / den
    bb_re = f_re[..., None] * b_re - f_im[..., None] * b_im
    bb_im = f_re[..., None] * b_im + f_im[..., None] * b_re
    return a_re, a_im, bb_re, bb_im


def _block_diag(t):
    g, r, c = t.shape
    return (jnp.eye(g, dtype=F32)[:, None, :, None] * t[:, :, None, :]).reshape(g * r, g * c)


def _loss_head(y, target):
    b, n, d = y.shape
    ts = _pick(n, (256, 128, 64))

    def body(y_ref, t_ref, loss_ref, dy_ref):
        first = jnp.logical_and(pl.program_id(0) == 0, pl.program_id(1) == 0)
        err = y_ref[...] - t_ref[...]
        dy_ref[...] = err * (1.0 / d)
        part = 0.5 * jnp.sum(jnp.sum(err * err, axis=-1, keepdims=True) * (1.0 / d), axis=0, keepdims=True)
        part = jnp.broadcast_to(part, (8, LANE))

        @pl.when(first)
        def _():
            loss_ref[...] = part

        @pl.when(jnp.logical_not(first))
        def _():
            loss_ref[...] += part

    blk = pl.BlockSpec((None, ts, d), lambda bi, i: (bi, i, 0))
    return pl.pallas_call(
        body, out_shape=[jax.ShapeDtypeStruct((8, LANE), F32), jax.ShapeDtypeStruct((b, n, d), F32)],
        grid=(b, n // ts), in_specs=[blk, blk], out_specs=[pl.BlockSpec((8, LANE), lambda bi, i: (0, 0)), blk],
        name="loss_head", compiler_params=_params(),
    )(y, target)


def _adamw(w, g, m, v):
    shape = w.shape
    n = int(np.prod(shape))
    cols = shape[-1]
    r = n // cols
    tr = _pick(r, (512, 256, 128, 64, 32, 16, 8))
    c1 = 1.0 / (1.0 - ADAM_B1 ** ADAM_STEP)
    c2 = 1.0 / (1.0 - ADAM_B2 ** ADAM_STEP)

    def body(w_ref, g_ref, m_ref, v_ref, d_ref, mo_ref, vo_ref):
        gv = g_ref[...]
        m2 = ADAM_B1 * m_ref[...] + (1.0 - ADAM_B1) * gv
        v2 = ADAM_B2 * v_ref[...] + (1.0 - ADAM_B2) * (gv * gv)
        d_ref[...] = -ADAM_LR * ((m2 * c1) / (jnp.sqrt(v2 * c2) + ADAM_EPS) + ADAM_WD * w_ref[...])
        mo_ref[...] = m2
        vo_ref[...] = v2

    blk = pl.BlockSpec((tr, cols), lambda i: (i, 0))
    outs = pl.pallas_call(
        body, out_shape=[jax.ShapeDtypeStruct((r, cols), F32)] * 3, grid=(r // tr,),
        in_specs=[blk] * 4, out_specs=[blk] * 3, name=f"adamw_{r}x{cols}", compiler_params=_params(),
    )(*[t.reshape(r, cols) for t in (w, g, m, v)])
    return tuple(o.reshape(shape) for o in outs)


def _sum_rows(x, n):
    _, r, c = x.shape
    tr = _pick(r, (512, 256, 128, 64, 32, 16, 8))

    def body(x_ref, o_ref):
        acc = x_ref[0]
        for j in range(1, n):
            acc = acc + x_ref[j]
        o_ref[...] = acc

    return pl.pallas_call(
        body, out_shape=jax.ShapeDtypeStruct((r, c), F32), grid=(r // tr,),
        in_specs=[pl.BlockSpec((n, tr, c), lambda i: (0, i, 0))], out_specs=pl.BlockSpec((tr, c), lambda i: (i, 0)),
        name=f"sum{n}_{r}x{c}", compiler_params=_params(),
    )(x)


def _accumulate(parts, out_dtype):
    r, c = parts[0].shape[-2:]
    tr = _pick(r, (1152, 1024, 768, 576, 512, 256, 128, 64, 32, 16))

    def body(*refs):
        acc = None
        for ref in refs[:-1]:
            terms = [ref[j] for j in range(ref.shape[0])] if len(ref.shape) == 3 else [ref[...]]
            for t in terms:
                acc = t.astype(F32) if acc is None else acc + t.astype(F32)
        refs[-1][...] = acc.astype(out_dtype)

    specs = [pl.BlockSpec((p.shape[0], tr, c), lambda i: (0, i, 0)) if p.ndim == 3 else pl.BlockSpec((tr, c), lambda i: (i, 0))
             for p in parts]
    tag = "_".join(str(p.shape[0]) if p.ndim == 3 else "1" for p in parts)
    return pl.pallas_call(
        body, out_shape=jax.ShapeDtypeStruct((r, c), out_dtype), grid=(r // tr,), in_specs=specs,
        out_specs=pl.BlockSpec((tr, c), lambda i: (i, 0)), name=f"accumulate_{tag}_{r}x{c}_{jnp.dtype(out_dtype).name}",
        compiler_params=_params(),
    )(*parts)


def _add2(x, y):
    shape = x.shape
    c = shape[-1]
    r = int(np.prod(shape)) // c
    tr = _pick(r, (512, 256, 128, 64, 32, 16, 8))

    def body(x_ref, y_ref, o_ref):
        o_ref[...] = x_ref[...] + y_ref[...]

    blk = pl.BlockSpec((tr, c), lambda i: (i, 0))
    return pl.pallas_call(
        body, out_shape=jax.ShapeDtypeStruct((r, c), F32), grid=(r // tr,), in_specs=[blk, blk], out_specs=blk,
        name=f"add2_{r}x{c}", compiler_params=_params(),
    )(x.reshape(r, c), y.reshape(r, c)).reshape(shape)


_FLIPS = ((1, 0), (0, 1), (1, 1))


def _me():
    return lax.axis_index("x"), lax.axis_index("y"), lax.axis_index("c")


def allgather8(v):
    m_per, n = v.shape

    def body(x_ref, out_ref, send_sems, recv_sems, local_sem):
        x, y, c = _me()
        me, sibling = (x, y, c), (x, y, 1 - c)
        chips = [(1 - x, y), (x, 1 - y), (1 - x, 1 - y)]

        def rows(px, py, pc):
            return out_ref.at[pl.ds((4 * px + 2 * py + pc) * m_per, m_per), :]

        def copy(k, block, to, src=None):
            return pltpu.make_async_remote_copy(
                src_ref=rows(*block) if src is None else src, dst_ref=rows(*block),
                send_sem=send_sems.at[k], recv_sem=recv_sems.at[k], device_id=to, device_id_type=MESH)

        mine = pltpu.make_async_copy(x_ref, rows(*me), local_sem)
        mine.start()
        first = [copy(0, me, sibling, src=x_ref)]
        first += [copy(1 + j, me, (*chip, c), src=x_ref) for j, chip in enumerate(chips)]
        for cp in first:
            cp.start()
        passed = [copy(4 + j, (*chip, c), sibling) for j, chip in enumerate(chips)]
        for j, chip in enumerate(chips):
            copy(1 + j, (*chip, c), me).wait_recv()
            passed[j].start()
        copy(0, sibling, me).wait_recv()
        for j, chip in enumerate(chips):
            copy(4 + j, (*chip, 1 - c), me).wait_recv()
        for cp in first + passed:
            cp.wait_send()
        mine.wait()

    return pl.pallas_call(
        body, out_shape=jax.ShapeDtypeStruct((N_DEV * m_per, n), v.dtype), in_specs=[VMEM_SPEC], out_specs=VMEM_SPEC,
        scratch_shapes=[pltpu.SemaphoreType.DMA((7,)), pltpu.SemaphoreType.DMA((7,)), pltpu.SemaphoreType.DMA],
        name=f"allgather8_{m_per}x{n}", compiler_params=_params(),
    )(v)


def _row_chunks(rows, tile_rows, want):
    n = want
    while n > 1 and rows % (n * tile_rows):
        n //= 2
    return [(i * (rows // n), rows // n) for i in range(n)]


def _remote(src, dst, send_sem, recv_sem, to):
    return pltpu.make_async_remote_copy(src_ref=src, dst_ref=dst, send_sem=send_sem, recv_sem=recv_sem, device_id=to,
                                        device_id_type=MESH)


def plane_allgather(big, small):
    rows = big.shape[0]
    rh = rows // 2
    rq = rh // 2
    tile = 16 if big.dtype == BF16 else 8
    assert rq % tile == 0
    ch_full = _row_chunks(rows, tile, 8)
    ch_half = _row_chunks(rh, tile, 4)

    def body(big_ref, small_ref, obig_ref, osmall_ref, send_sems, recv_sems, relay_send, relay_recv, fwd_send, fwd_recv,
             own_send, own_recv):
        x, y, c = _me()
        me = 2 * x + y
        sibling = (x, y, 1 - c)
        nbr_x, nbr_y, diag = (1 - x, y, c), (x, 1 - y, c), (1 - x, 1 - y, c)
        xi, yi, di = 2 * (1 - x) + y, 2 * x + (1 - y), 2 * (1 - x) + (1 - y)
        base, obase = c * rh, (1 - c) * rh
        mine, other = pl.ds(base, rh), pl.ds(obase, rh)
        qa, qb = pl.ds(base, rq), pl.ds(base + rq, rq)
        for st, sz in ch_full:
            sl = pl.ds(st, sz)
            _remote(big_ref.at[sl], obig_ref.at[me, sl], own_send.at[0], own_recv.at[0], sibling).start()
        _remote(small_ref, osmall_ref.at[me], own_send.at[1], own_recv.at[1], sibling).start()
        for j, peer in enumerate((nbr_x, nbr_y)):
            for st, sz in ch_half:
                sl = pl.ds(base + st, sz)
                _remote(big_ref.at[sl], obig_ref.at[me, sl], send_sems.at[j], recv_sems.at[j], peer).start()
        for j, peer in enumerate((nbr_x, nbr_y, diag)):
            _remote(small_ref, osmall_ref.at[me], send_sems.at[3 + j], recv_sems.at[3 + j], peer).start()

        def pass_on(k, slot, sl):
            _remote(obig_ref.at[slot, sl], obig_ref.at[slot, sl], fwd_send.at[k], fwd_recv.at[k], sibling).start()

        _remote(big_ref.at[mine], obig_ref.at[xi, mine], send_sems.at[0], recv_sems.at[0], nbr_x).wait_recv()
        _remote(obig_ref.at[xi, qa], obig_ref.at[xi, qa], relay_send.at[0], relay_recv.at[0], nbr_y).start()
        pass_on(0, xi, mine)
        _remote(big_ref.at[mine], obig_ref.at[yi, mine], send_sems.at[1], recv_sems.at[1], nbr_y).wait_recv()
        _remote(obig_ref.at[yi, qb], obig_ref.at[yi, qb], relay_send.at[1], relay_recv.at[1], nbr_x).start()
        pass_on(1, yi, mine)
        _remote(obig_ref.at[di, qa], obig_ref.at[di, qa], relay_send.at[0], relay_recv.at[0], nbr_y).wait_recv()
        pass_on(2, di, qa)
        _remote(obig_ref.at[di, qb], obig_ref.at[di, qb], relay_send.at[1], relay_recv.at[1], nbr_x).wait_recv()
        pass_on(3, di, qb)
        for j, (peer, slot) in enumerate(((nbr_x, xi), (nbr_y, yi), (diag, di))):
            _remote(small_ref, osmall_ref.at[slot], send_sems.at[3 + j], recv_sems.at[3 + j], peer).wait_recv()
        oqa, oqb = pl.ds(obase, rq), pl.ds(obase + rq, rq)
        for k, (slot, sl) in enumerate(((xi, other), (yi, other), (di, oqa), (di, oqb))):
            _remote(obig_ref.at[slot, sl], obig_ref.at[slot, sl], fwd_send.at[k], fwd_recv.at[k], sibling).wait_recv()
        for k, (slot, sl) in enumerate(((xi, mine), (yi, mine), (di, qa), (di, qb))):
            _remote(obig_ref.at[slot, sl], obig_ref.at[slot, sl], fwd_send.at[k], fwd_recv.at[k], sibling).wait_send()
        for j, peer in enumerate((nbr_x, nbr_y)):
            _remote(big_ref.at[mine], obig_ref.at[me, mine], send_sems.at[j], recv_sems.at[j], peer).wait_send()
        for j, peer in enumerate((nbr_x, nbr_y, diag)):
            _remote(small_ref, osmall_ref.at[me], send_sems.at[3 + j], recv_sems.at[3 + j], peer).wait_send()
        _remote(obig_ref.at[xi, qa], obig_ref.at[xi, qa], relay_send.at[0], relay_recv.at[0], nbr_y).wait_send()
        _remote(obig_ref.at[yi, qb], obig_ref.at[yi, qb], relay_send.at[1], relay_recv.at[1], nbr_x).wait_send()
        _remote(big_ref, obig_ref.at[me], own_send.at[0], own_recv.at[0], sibling).wait()
        _remote(small_ref, osmall_ref.at[me], own_send.at[1], own_recv.at[1], sibling).wait()

    dma = pltpu.SemaphoreType.DMA
    return pl.pallas_call(
        body, out_shape=[jax.ShapeDtypeStruct((N_PLANE,) + big.shape, big.dtype),
                         jax.ShapeDtypeStruct((N_PLANE,) + small.shape, small.dtype)],
        in_specs=[ANY, ANY], out_specs=[ANY, ANY],
        scratch_shapes=[dma((6,)), dma((6,)), dma((2,)), dma((2,)), dma((4,)), dma((4,)), dma((2,)), dma((2,))],
        name="plane_allgather", compiler_params=_params(),
    )(big, small)


def neighbour_exchange(to_x, to_y):
    tile = 16 if to_x.dtype == BF16 else 8
    chunks = _row_chunks(to_x.shape[0], tile, 4)

    def body(ax_ref, ay_ref, fx_ref, fy_ref, send_sems, recv_sems):
        x, y, c = _me()
        for k, (src, dst, peer) in enumerate(((ax_ref, fx_ref, (1 - x, y, c)), (ay_ref, fy_ref, (x, 1 - y, c)))):
            for st, sz in chunks:
                sl = pl.ds(st, sz)
                _remote(src.at[sl], dst.at[sl], send_sems.at[k], recv_sems.at[k], peer).start()
        for k, (src, dst, peer) in enumerate(((ax_ref, fx_ref, (1 - x, y, c)), (ay_ref, fy_ref, (x, 1 - y, c)))):
            _remote(src, dst, send_sems.at[k], recv_sems.at[k], peer).wait()

    shape = jax.ShapeDtypeStruct(to_x.shape, to_x.dtype)
    return pl.pallas_call(
        body, out_shape=[shape, shape], in_specs=[ANY, ANY], out_specs=[ANY, ANY],
        scratch_shapes=[pltpu.SemaphoreType.DMA((2,)), pltpu.SemaphoreType.DMA((2,))],
        name=f"neighbour_exchange_{to_x.shape[0]}", compiler_params=_params(),
    )(to_x, to_y)


def sibling_halves(buf):
    n_blk, _, rows, cols = buf.shape
    tile = 16 if buf.dtype == BF16 else 8
    chunks = _row_chunks(rows, tile, 2)

    def body(buf_ref, got_ref, send_sem, recv_sem):
        x, y, c = _me()
        for j in range(n_blk):
            for st, sz in chunks:
                sl = pl.ds(st, sz)
                _remote(buf_ref.at[j, 1 - c, sl], got_ref.at[j, sl], send_sem, recv_sem, (x, y, 1 - c)).start()
        _remote(got_ref, got_ref, send_sem, recv_sem, (x, y, 1 - c)).wait()

    return pl.pallas_call(
        body, out_shape=jax.ShapeDtypeStruct((n_blk, rows, cols), buf.dtype), in_specs=[ANY], out_specs=ANY,
        scratch_shapes=[pltpu.SemaphoreType.DMA, pltpu.SemaphoreType.DMA],
        name="sibling_halves", compiler_params=_params(),
    )(buf)


def sibling_swap(s):
    tile = 16 if s.dtype == BF16 else 8
    chunks = _row_chunks(s.shape[0], tile, 8)

    def body(s_ref, got_ref, send_sem, recv_sem):
        x, y, c = _me()
        for st, sz in chunks:
            sl = pl.ds(st, sz)
            _remote(s_ref.at[sl], got_ref.at[sl], send_sem, recv_sem, (x, y, 1 - c)).start()
        _remote(s_ref, got_ref, send_sem, recv_sem, (x, y, 1 - c)).wait()

    return pl.pallas_call(
        body, out_shape=jax.ShapeDtypeStruct(s.shape, s.dtype), in_specs=[ANY], out_specs=ANY,
        scratch_shapes=[pltpu.SemaphoreType.DMA, pltpu.SemaphoreType.DMA],
        name="sibling_swap", compiler_params=_params(),
    )(s)


def _op(cache, fn, name, kinds, out_dims, **kw):
    key = (name, tuple(out_dims), tuple(sorted(kw.items())))
    if key not in cache:
        cache[key] = make_rowwise(fn, name, kinds, out_dims, **kw)
    return cache[key]


def _even_mixer(ops, a, w, n_ctx):
    b, s, d = a.shape
    proj = linear(a.reshape(b * s, d), w["e_w_in"]).reshape(b, s, -1)
    q, k, v, u = jnp.split(proj, [GQA_Q_W, GQA_Q_W + GQA_KV_W, GQA_Q_W + 2 * GQA_KV_W], axis=-1)
    cos, sin = _rope_tables(n_ctx, s - n_ctx, HEAD_DIM, 0, HEAD_DIM)
    shift = HEAD_DIM // 4
    qn = head_norm_rope(q, w["e_g_q"][None], jnp.tile(cos, (1, GQA_Q_HEADS)), jnp.tile(sin, (1, GQA_Q_HEADS)),
                        jnp.asarray(_rope_signs(HEAD_DIM, 0, HEAD_DIM, GQA_Q_HEADS)), GQA_Q_HEADS, shift)
    kn = head_norm_rope(k, w["e_g_k"][None], jnp.tile(cos, (1, GQA_KV_HEADS)), jnp.tile(sin, (1, GQA_KV_HEADS)),
                        jnp.asarray(_rope_signs(HEAD_DIM, 0, HEAD_DIM, GQA_KV_HEADS)), GQA_KV_HEADS, shift)
    att = gqa_attention(qn, kn, v, n_ctx)
    ys = []
    for dr in range(2):
        a_re, a_im, bb_re, bb_im = _s5_discretize(w["ssm_lam_re"][dr], w["ssm_lam_im"][dr], w["ssm_log_dt"][dr],
                                                  w["ssm_b_re"][dr], w["ssm_b_im"][dr])
        a_flat = jnp.stack([a_re.reshape(-1), a_im.reshape(-1)])
        b_bd = jnp.stack([_block_diag(jnp.swapaxes(bb_re, 1, 2)), _block_diag(jnp.swapaxes(bb_im, 1, 2))])
        c_bd = jnp.stack([_block_diag(jnp.swapaxes(w["ssm_c_re"][dr], 1, 2)),
                          _block_diag(jnp.swapaxes(w["ssm_c_im"][dr], 1, 2))])
        ys.append(s5_direction(u, a_flat, b_bd, c_bd, n_ctx, dr == 1))
    pre = _op(ops, _fn_glu_pre, "glu_pre", ("row", "row", "row", "glob"), (SSM_WIDTH,))
    post = _op(ops, _fn_glu_post, "glu_post", ("row", "row", "glob"), (SSM_WIDTH,))
    z = pre(u, ys[0], ys[1], w["ssm_d"][None])[0]
    t = linear(z.reshape(b * s, SSM_WIDTH), w["ssm_w_glu"]).reshape(b, s, SSM_WIDTH)
    ssm = post(z, t, w["ssm_b_glu"][None])[0]
    mix = jnp.concatenate([att, ssm], axis=-1)
    return linear(mix.reshape(b * s, -1), w["e_w_out"]).reshape(b, s, d)


def _odd_mixer(ops, a, w, n_ctx):
    b, s, d = a.shape
    w_in = jnp.pad(w["o_w_in"], ((0, 0), (0, ODD_IN_PAD - ODD_IN_W)))
    proj = linear(a.reshape(b * s, d), w_in).reshape(b, s, -1)
    c1 = MLA_Q_RANK
    c2 = c1 + MLA_KV_RANK
    c3 = c2 + MLA_ROPE
    cq, ckv, kr, nq, nk, nv, _ = jnp.split(proj, [c1, c2, c3, c3 + NA_W, c3 + 2 * NA_W, ODD_IN_W], axis=-1)
    nrm = lambda wd: _op(ops, _fn_norm, f"norm{wd}", ("row", "glob"), (wd,))
    cqn = nrm(MLA_Q_RANK)(cq, w["mla_g_cq"][None])[0]
    ckvn = nrm(MLA_KV_RANK)(ckv, w["mla_g_ckv"][None])[0]
    q = linear(cqn.reshape(b * s, -1), w["mla_w_uq"]).reshape(b, s, -1)
    kv = linear(ckvn.reshape(b * s, -1), w["mla_w_ukv"]).reshape(b, s, -1)
    cos, sin = _rope_tables(n_ctx, s - n_ctx, MLA_QK, MLA_NOPE, MLA_ROPE)
    cos, sin = jnp.tile(cos, (1, MLA_HEADS)), jnp.tile(sin, (1, MLA_HEADS))
    sg = jnp.asarray(_rope_signs(MLA_QK, MLA_NOPE, MLA_ROPE, MLA_HEADS))
    mq = head_norm_rope(q, w["mla_g_q"][None], cos, sin, sg, MLA_HEADS, MLA_ROPE // 4)
    mk = mla_k_prep(kv, kr, w["mla_g_k"][None], cos, sin, sg)
    mla = mla_attention(mq, mk, kv, n_ctx)
    nqn = head_norm_rope(nq, w["na_g_q"][None], None, None, None, NA_HEADS, 0)
    nkn = head_norm_rope(nk, w["na_g_k"][None], None, None, None, NA_HEADS, 0)
    na = na_attention_tm(nqn, nkn, nv, na_bias_table(w["na_rpb"]), n_ctx)
    mix = jnp.concatenate([mla, na], axis=-1)
    return linear(mix.reshape(b * s, -1), w["o_w_out"]).reshape(b, s, d)


_EVEN_KEYS = ("e_w_in", "e_w_out", "e_g_q", "e_g_k", "ssm_lam_re", "ssm_lam_im", "ssm_log_dt", "ssm_b_re", "ssm_b_im",
              "ssm_c_re", "ssm_c_im", "ssm_d", "ssm_w_glu", "ssm_b_glu")
_ODD_KEYS = ("o_w_in", "o_w_out", "mla_g_cq", "mla_g_ckv", "mla_w_uq", "mla_w_ukv", "mla_g_q", "mla_g_k", "na_g_q",
             "na_g_k", "na_rpb")


def _trunk(x_all, mods, w, n_ctx):
    ops = {}
    depth = mods.shape[0]
    b, s, d = x_all.shape
    modulate = make_modulate(d, n_ctx)
    gated = make_gated_add(d, n_ctx)
    x = x_all
    for i in range(depth):
        j = i // 2
        m = [mods[i][:, :, r:r + 1, :] for r in range(N_MOD)]
        a, x = modulate(x, w["g_norm1"][i][None], m[0], m[1])
        if i % 2 == 0:
            o = _even_mixer(ops, a, {k: w[k][j] for k in _EVEN_KEYS}, n_ctx)
        else:
            o = _odd_mixer(ops, a, {k: w[k][j] for k in _ODD_KEYS}, n_ctx)
        x = gated(x, o, m[2])
        a2, x = modulate(x, w["g_norm2"][i][None], m[3], m[4])
        f = ffn(a2.reshape(b * s, d), w["w_ff1"][i], w["w_ff2"][i]).reshape(b, s, d)
        x = gated(x, f, m[5])
    return x[:, n_ctx:]


def local_step(x, ctx, mods, w, loss_target):
    n_ctx = ctx.shape[1]
    x_all = jnp.concatenate([ctx, x], axis=1)
    y, vjp = jax.vjp(lambda xa, md, ww: _trunk(xa, md, ww, n_ctx), x_all, mods, w)
    loss_tile, dy = _loss_head(y, loss_target)
    dx_all, dmods, dw = vjp(dy)
    return loss_tile[0, 0], dx_all[:, n_ctx:], dmods, dw


_SHARDED = (("w_ff1", 2), ("w_ff2", 1), ("e_w_in", 2), ("e_w_out", 1), ("o_w_in", 2), ("o_w_out", 1),
            ("mla_w_uq", 2), ("mla_w_ukv", 2), ("ssm_w_glu", 1))
_SHARDED_SMALL = (("mla_g_cq", 1), ("mla_g_ckv", 1))
_REPLICATED = ("g_norm1", "g_norm2", "e_g_q", "e_g_k", "ssm_lam_re", "ssm_lam_im", "ssm_log_dt", "ssm_b_re", "ssm_b_im",
               "ssm_c_re", "ssm_c_im", "ssm_d", "ssm_b_glu", "mla_g_q", "mla_g_k", "na_g_q", "na_g_k", "na_rpb")
_WEIGHTS = ("c_ctx", "w_mod", "b_mod", "g_norm1", "g_norm2", "w_ff1", "w_ff2", "e_w_in", "e_w_out", "e_g_q", "e_g_k",
            "ssm_lam_re", "ssm_lam_im", "ssm_log_dt", "ssm_b_re", "ssm_b_im", "ssm_c_re", "ssm_c_im", "ssm_d",
            "ssm_w_glu", "ssm_b_glu", "o_w_in", "o_w_out", "mla_g_cq", "mla_g_ckv", "mla_w_uq", "mla_w_ukv", "mla_g_q",
            "mla_g_k", "na_g_q", "na_g_k", "na_rpb")
_PACK_ROWS = 64


def _pack(arrs, dtype, cols=1024, row_mult=_PACK_ROWS):
    blocks, tail, off = [], [], 0
    for a in arrs:
        n = int(np.prod(a.shape))
        if not tail and off % cols == 0 and n % cols == 0:
            blocks.append(a.astype(dtype).reshape(-1, cols))
        else:
            tail.append(a.astype(dtype).reshape(-1))
        off += n
    rows = -(-off // cols)
    pad = (-rows) % row_mult * cols + rows * cols - off
    if tail or pad:
        blocks.append(jnp.concatenate(tail + [jnp.zeros((pad,), dtype)]).reshape(-1, cols))
    return jnp.concatenate(blocks, axis=0)


def _unpack(packed, shapes):
    cols = packed.shape[-1]
    packed = packed.reshape(-1, cols)
    out, off = [], 0
    for sh in shapes:
        n = int(np.prod(sh))
        if off % cols == 0 and n % cols == 0:
            out.append(packed[off // cols:(off + n) // cols].reshape(sh))
        else:
            r0, r1 = off // cols, -(-(off + n) // cols)
            out.append(packed[r0:r1].reshape(-1)[off - r0 * cols:off - r0 * cols + n].reshape(sh))
        off += n
    return out


def kernel(x, c, ctx, c_ctx, w_mod, b_mod, g_norm1, g_norm2, w_ff1, w_ff2, e_w_in, e_w_out, e_g_q, e_g_k, ssm_lam_re, ssm_lam_im, ssm_log_dt, ssm_b_re, ssm_b_im, ssm_c_re, ssm_c_im, ssm_d, ssm_w_glu, ssm_b_glu, o_w_in, o_w_out, mla_g_cq, mla_g_ckv, mla_w_uq, mla_w_ukv, mla_g_q, mla_g_k, na_g_q, na_g_k, na_rpb, loss_target, m_c_ctx, m_w_mod, m_b_mod, m_g_norm1, m_g_norm2, m_w_ff1, m_w_ff2, m_e_w_in, m_e_w_out, m_e_g_q, m_e_g_k, m_ssm_lam_re, m_ssm_lam_im, m_ssm_log_dt, m_ssm_b_re, m_ssm_b_im, m_ssm_c_re, m_ssm_c_im, m_ssm_d, m_ssm_w_glu, m_ssm_b_glu, m_o_w_in, m_o_w_out, m_mla_g_cq, m_mla_g_ckv, m_mla_w_uq, m_mla_w_ukv, m_mla_g_q, m_mla_g_k, m_na_g_q, m_na_g_k, m_na_rpb, v_c_ctx, v_w_mod, v_b_mod, v_g_norm1, v_g_norm2, v_w_ff1, v_w_ff2, v_e_w_in, v_e_w_out, v_e_g_q, v_e_g_k, v_ssm_lam_re, v_ssm_lam_im, v_ssm_log_dt, v_ssm_b_re, v_ssm_b_im, v_ssm_c_re, v_ssm_c_im, v_ssm_d, v_ssm_w_glu, v_ssm_b_glu, v_o_w_in, v_o_w_out, v_mla_g_cq, v_mla_g_ckv, v_mla_w_uq, v_mla_w_ukv, v_mla_g_q, v_mla_g_k, v_na_g_q, v_na_g_k, v_na_rpb):
    env = dict(locals())
    weights = {n: env[n] for n in _WEIGHTS}
    mom_m = {n: env["m_" + n] for n in _WEIGHTS}
    mom_v = {n: env["v_" + n] for n in _WEIGHTS}
    ax, ay, ac = _me()
    plane = 2 * ax + ay
    dev = 4 * ax + 2 * ay + ac
    b_loc, d = c.shape
    depth = w_mod.shape[0]
    n_all = N_DEV * b_loc
    mod_cols = w_mod.shape[2]

    big = _pack([weights[n] for n, _ in _SHARDED], BF16)
    small = _pack([weights[n] for n, _ in _SHARDED_SMALL], F32, cols=LANE, row_mult=8)
    g_big, g_small = plane_allgather(big, small)
    full = {n: weights[n] for n in _REPLICATED}
    parts = [_unpack(g_big[j], [weights[n].shape for n, _ in _SHARDED]) for j in range(N_PLANE)]
    for t, (n, axis) in enumerate(_SHARDED):
        full[n] = [jnp.concatenate([parts[j][t][l] for j in range(N_PLANE)], axis=axis - 1)
                   for l in range(weights[n].shape[0])]
    parts_s = [_unpack(g_small[j], [weights[n].shape for n, _ in _SHARDED_SMALL]) for j in range(N_PLANE)]
    for t, (n, axis) in enumerate(_SHARDED_SMALL):
        full[n] = jnp.concatenate([parts_s[j][t] for j in range(N_PLANE)], axis=axis)

    rows_pad = 8 * ((n_all + 1 + 7) // 8)
    c_all = allgather8(jnp.pad(c, ((0, 8 - b_loc), (0, 0)))).reshape(N_DEV, 8, d)[:, :b_loc].reshape(n_all, d)
    cond_raw = jnp.concatenate([c_all, c_ctx[None], jnp.zeros((rows_pad - n_all - 1, d), F32)], axis=0)
    b_cols = lax.dynamic_slice_in_dim(b_mod, plane * mod_cols, mod_cols, axis=1)
    mod_loc = jnp.stack([_mm(cond_raw, w_mod[i], a_act="silu") + b_cols[i][None] for i in range(depth)])
    mod_g = allgather8(mod_loc.reshape(depth * rows_pad, mod_cols)).reshape(N_PLANE, 2, depth, rows_pad, mod_cols)
    mod_all = jnp.concatenate([mod_g[j, 0] for j in range(N_PLANE)], axis=-1)
    m_lat = lax.dynamic_slice_in_dim(mod_all, dev * b_loc, b_loc, axis=1)
    m_ctx = jnp.broadcast_to(mod_all[:, n_all][:, None], m_lat.shape)
    mods = jnp.stack([m_ctx, m_lat], axis=2).reshape(depth, b_loc, 2, N_MOD, d)

    loss_part, grad_x, dmods, dw = local_step(x, ctx, mods, full, loss_target)

    dm = dmods.reshape(depth, b_loc, 2, N_MOD * d)
    dm_rows = jnp.concatenate([dm[:, :, 1], jnp.sum(dm[:, :, 0], axis=1, keepdims=True)], axis=1)
    rep_shapes = [weights[n].shape for n in _REPLICATED] + [(1,)]
    small_pack = _pack([dm_rows] + [dw[n] for n in _REPLICATED] + [loss_part.reshape(1)], F32, cols=1024, row_mult=8)
    sp_rows = small_pack.shape[0]
    gathered = allgather8(small_pack).reshape(N_DEV, sp_rows, 1024)
    n_dm = depth * (b_loc + 1) * N_MOD * d
    dm_all = gathered.reshape(N_DEV, -1)[:, :n_dm].reshape(N_DEV, depth, b_loc + 1, N_MOD * d)
    rep_sum = _sum_rows(gathered, N_DEV).reshape(-1)
    rep_parts = _unpack(rep_sum[n_dm:], rep_shapes)
    rep_grads = dict(zip(_REPLICATED, rep_parts[:-1]))
    loss = rep_parts[-1][0]
    d_ctx_row = rep_sum[:n_dm].reshape(depth, b_loc + 1, N_MOD * d)[:, b_loc]
    d_lat_rows = jnp.transpose(dm_all[:, :, :b_loc], (1, 0, 2, 3)).reshape(depth, n_all, N_MOD * d)
    d_mod_all = jnp.concatenate([d_lat_rows, d_ctx_row[:, None],
                                 jnp.zeros((depth, rows_pad - n_all - 1, N_MOD * d), F32)], axis=1)
    grads = dict(rep_grads)
    grads["b_mod"] = jnp.sum(d_mod_all, axis=1)
    d_cols = lax.dynamic_slice_in_dim(d_mod_all, plane * mod_cols, mod_cols, axis=2)
    grads["w_mod"] = jnp.stack([_mm(cond_raw, d_cols[i], ta=True, a_act="silu") for i in range(depth)])
    d_cond = _mm(d_cols[0], w_mod[0], tb=True)
    for i in range(1, depth):
        d_cond = _add2(d_cond, _mm(d_cols[i], w_mod[i], tb=True))
    d_cond_g = allgather8(d_cond[n_all:n_all + 8] if rows_pad - n_all >= 8 else
                          jnp.pad(d_cond[n_all:], ((0, 8 - (rows_pad - n_all)), (0, 0)))).reshape(N_PLANE, 2, 8, d)
    d_silu = _sum_rows(d_cond_g[:, 0], N_PLANE)[0]
    sg = jax.nn.sigmoid(c_ctx)
    grads["c_ctx"] = d_silu * (sg * (1.0 + c_ctx * (1.0 - sg)))

    def shards_of(g, axis, j):
        layers = g if isinstance(g, (list, tuple)) else [g]
        ax = axis - 1 if isinstance(g, (list, tuple)) else axis
        n = layers[0].shape[ax] // N_PLANE
        return [lax.slice_in_dim(t, j * n, (j + 1) * n, axis=ax) for t in layers]

    send = jnp.stack([_pack([t for n, axis in _SHARDED + _SHARDED_SMALL for t in shards_of(dw[n], axis, j)], BF16)
                      for j in range(N_PLANE)])
    rows_h = send.shape[1] // 2
    send = send.reshape(N_PLANE, 2, rows_h, 1024)
    mine = lax.dynamic_index_in_dim(send, ac, 1, keepdims=False).reshape(N_PLANE * rows_h, 1024)
    theirs = sibling_halves(send).reshape(N_PLANE * rows_h, 1024)
    chip_sum = _accumulate([mine, theirs], BF16).reshape(N_PLANE, rows_h, 1024)
    pick = lambda k: lax.dynamic_index_in_dim(chip_sum, k, 0, keepdims=False)
    own, for_x, for_y, for_diag = pick(plane), pick(plane ^ 2), pick(plane ^ 1), pick(plane ^ 3)
    rows_q = rows_h // 2
    from_x, from_y = neighbour_exchange(for_diag[:rows_q], for_diag[rows_q:])
    zeros_q = jnp.zeros((rows_q, 1024), BF16)
    relayed = jnp.concatenate([zeros_q, from_y, from_x, zeros_q])
    merged = _accumulate([jnp.concatenate([for_x, for_y]), relayed], BF16)
    got_x, got_y = neighbour_exchange(merged[:rows_h], merged[rows_h:])
    done = _accumulate([own, got_x, got_y], BF16)
    both = jnp.stack([done, sibling_swap(done)])
    flat = jnp.where(ac == 0, both, both[::-1]).astype(F32).reshape(-1, 1024)
    shard_shapes = [weights[n].shape for n, _ in _SHARDED] + [weights[n].shape for n, _ in _SHARDED_SMALL]
    for (n, _), g in zip(_SHARDED + _SHARDED_SMALL, _unpack(flat, shard_shapes)):
        grads[n] = g

    big_names = ("w_mod",) + tuple(n for n, _ in _SHARDED)
    small_names = tuple(n for n in _WEIGHTS if n not in big_names)
    delta, new_m, new_v = {}, {}, {}
    for n in big_names:
        delta[n], new_m[n], new_v[n] = _adamw(weights[n], grads[n], mom_m[n], mom_v[n])
    sm_shapes = [weights[n].shape for n in small_names]
    packed = [_pack([src[n] for n in small_names], F32, cols=1024, row_mult=8)
              for src in (weights, grads, mom_m, mom_v)]
    for dst, res in zip((delta, new_m, new_v), _adamw(*packed)):
        dst.update(dict(zip(small_names, _unpack(res, sm_shapes))))

    return (loss, grad_x, *[grads[n] for n in _WEIGHTS], *[delta[n] for n in _WEIGHTS],
            *[new_m[n] for n in _WEIGHTS], *[new_v[n] for n in _WEIGHTS])
```

```python
import functools

import numpy as np
import jax
import jax.numpy as jnp
from jax import lax
from jax.experimental import pallas as pl
from jax.experimental.pallas import tpu as pltpu

F32 = jnp.float32
BF16 = jnp.bfloat16
HI = lax.Precision.HIGHEST
MESH = pl.DeviceIdType.MESH
ANY = pl.BlockSpec(memory_space=pl.ANY)
VMEM_SPEC = pl.BlockSpec(memory_space=pltpu.VMEM)

GRID_W = 64
HEAD_DIM = 64
ROPE_BASE = 10000.0
EPS = 1e-6
N_MOD = 6
GQA_Q_HEADS, GQA_KV_HEADS = 12, 4
GQA_Q_W, GQA_KV_W = GQA_Q_HEADS * HEAD_DIM, GQA_KV_HEADS * HEAD_DIM
SSM_WIDTH, SSM_GROUP, SSM_STATE = 256, 16, 64
SSM_GROUPS = SSM_WIDTH // SSM_GROUP
SSM_LANES = SSM_GROUPS * SSM_STATE
MLA_HEADS, MLA_Q_RANK, MLA_KV_RANK, MLA_NOPE, MLA_ROPE, MLA_V = 8, 512, 256, 64, 32, 64
MLA_QK = MLA_NOPE + MLA_ROPE
NA_HEADS, NA_WIN_R, NA_WIN_C = 8, 8, 16
NA_W = NA_HEADS * HEAD_DIM
NA_BAND = NA_WIN_R * GRID_W
ODD_IN_W = MLA_Q_RANK + MLA_KV_RANK + MLA_ROPE + 3 * NA_W
ODD_IN_PAD = 2560
ADAM_LR, ADAM_B1, ADAM_B2, ADAM_EPS, ADAM_WD, ADAM_STEP = 0.001, 0.9, 0.999, 1e-08, 0.01, 10
NEG = -1e30
VMEM_LIMIT = 56 * 1024 * 1024
LANE = 128
MM_TILE_M = (1152, 1024, 768, 512, 256, 128)
MM_TILE_N = (1280, 1024, 768, 512, 256, 128)
MM_TILE_K = (1152, 1024, 768, 512, 256, 128)
ROW_TILES = (576, 512, 384, 256, 128, 64)
N_PLANE = 4
N_DEV = 8


def _pick(n, cands):
    for c in cands:
        if n % c == 0:
            return c
    return n


def _params(**kw):
    return pltpu.CompilerParams(vmem_limit_bytes=VMEM_LIMIT, **kw)


def _mm(a, b, *, ta=False, tb=False, a_act=None, epi=None, e=None, exact=False, out_dtype=F32):
    m, kd = (a.shape[1], a.shape[0]) if ta else a.shape
    n = b.shape[0] if tb else b.shape[1]
    tm = _pick(m, MM_TILE_M)
    tn = _pick(n, MM_TILE_N)
    tk = _pick(kd, MM_TILE_K)
    nk = kd // tk
    dn = (((0 if ta else 1,), (1 if tb else 0,)), ((), ()))
    narrow = jnp.dtype(out_dtype) != jnp.dtype(F32)
    assert not (narrow and epi is not None)

    def body(*refs):
        if narrow:
            a_ref, b_ref, out_ref, o_ref = refs
        elif epi is None:
            a_ref, b_ref, o_ref = refs
        else:
            a_ref, b_ref, e_ref, o_ref = refs
        k = pl.program_id(2)
        av = a_ref[...]
        if a_act == "relu2":
            av = jnp.square(jnp.maximum(av.astype(F32), 0.0))
        elif a_act == "silu":
            av = av * jax.nn.sigmoid(av)
        bv = b_ref[...]
        if exact:
            p = lax.dot_general(av, bv, dn, precision=HI, preferred_element_type=F32)
        else:
            p = lax.dot_general(av.astype(BF16), bv.astype(BF16), dn, preferred_element_type=F32)

        @pl.when(k == 0)
        def _():
            o_ref[...] = p

        @pl.when(k > 0)
        def _():
            o_ref[...] += p

        if epi == "drelu2":
            @pl.when(k == nk - 1)
            def _():
                o_ref[...] = o_ref[...] * (2.0 * jnp.maximum(e_ref[...].astype(F32), 0.0))

        if narrow:
            @pl.when(k == nk - 1)
            def _():
                out_ref[...] = o_ref[...].astype(out_dtype)

    a_spec = pl.BlockSpec((tk, tm), lambda i, j, k: (k, i)) if ta else pl.BlockSpec((tm, tk), lambda i, j, k: (i, k))
    b_spec = pl.BlockSpec((tn, tk), lambda i, j, k: (j, k)) if tb else pl.BlockSpec((tk, tn), lambda i, j, k: (k, j))
    o_spec = pl.BlockSpec((tm, tn), lambda i, j, k: (i, j))
    ins, specs = [a, b], [a_spec, b_spec]
    if epi is not None:
        ins.append(e)
        specs.append(o_spec)
    name = f"mm_{m}x{kd}x{n}_{int(ta)}{int(tb)}_{a_act}_{epi}_{int(exact)}_{jnp.dtype(out_dtype).name}"
    return pl.pallas_call(
        body, out_shape=jax.ShapeDtypeStruct((m, n), out_dtype), grid=(m // tm, n // tn, nk),
        in_specs=specs, out_specs=o_spec, name=name, compiler_params=_params(),
        scratch_shapes=[pltpu.VMEM((tm, tn), F32)] if narrow else [],
    )(*ins)


@functools.partial(jax.custom_vjp, nondiff_argnums=(2,))
def _linear(a, w, exact):
    return _mm(a, w, exact=exact)


def _linear_fwd(a, w, exact):
    return _mm(a, w, exact=exact), (a, w)


def _linear_bwd(exact, res, g):
    a, w = res
    return _mm(g, w, tb=True, exact=exact), _mm(a, g, ta=True, exact=exact, out_dtype=w.dtype)


_linear.defvjp(_linear_fwd, _linear_bwd)


def linear(a, w, exact=False):
    return _linear(a, w, exact)


@jax.custom_vjp
def ffn(a, w1, w2):
    return _mm(_mm(a, w1, out_dtype=BF16), w2, a_act="relu2")


def _ffn_fwd(a, w1, w2):
    h1 = _mm(a, w1, out_dtype=BF16)
    return _mm(h1, w2, a_act="relu2"), (a, w1, w2, h1)


def _ffn_bwd(res, g):
    a, w1, w2, h1 = res
    dh1 = _mm(g, w2, tb=True, epi="drelu2", e=h1)
    dw2 = _mm(h1, g, ta=True, a_act="relu2", out_dtype=w2.dtype)
    return _mm(dh1, w1, tb=True), _mm(a, dh1, ta=True, out_dtype=w1.dtype), dw2


ffn.defvjp(_ffn_fwd, _ffn_bwd)


def make_rowwise(fn, name, kinds, out_dims, nctx_rows=0, whole_seq=False):
    n_in = len(kinds)
    n_out = len(out_dims)
    diff = [i for i, kd in enumerate(kinds) if kd in ("row", "glob", "seg")]
    seg_idx = [i for i, kd in enumerate(kinds) if kd == "seg"]

    def layout(args):
        row0 = args[kinds.index("row")]
        g, s = row0.shape[0], row0.shape[1]
        ts = s if whole_seq else _pick(s, ROW_TILES)
        return g, s, ts, 0

    def spec_of(kind, arr, ts, nctx):
        if kind == "row":
            return pl.BlockSpec((None, ts, arr.shape[2]), lambda g, i: (g, i, 0))
        if kind == "tab":
            return pl.BlockSpec((ts, arr.shape[1]), lambda g, i: (i, 0))
        if kind in ("const", "glob"):
            return pl.BlockSpec(arr.shape, lambda g, i: (0, 0))
        return pl.BlockSpec((None,) + arr.shape[1:], lambda g, i: (g, 0, 0, 0))

    def with_segments(ts):
        if not seg_idx:
            return fn

        def wrapped(*vals):
            rows = pl.program_id(1) * ts + lax.broadcasted_iota(jnp.int32, (ts, 1), 0)
            vals = list(vals)
            for idx in seg_idx:
                vals[idx] = jnp.where(rows < nctx_rows, vals[idx][0], vals[idx][1])
            return fn(*vals)

        return wrapped

    def fwd_call(*args):
        g, s, ts, nctx = layout(args)
        fn = with_segments(ts)

        def body(*refs):
            vals = [r[...] for r in refs[:n_in]]
            outs = fn(*vals)
            for o_ref, o in zip(refs[n_in:], outs):
                o_ref[...] = o

        return pl.pallas_call(
            body, out_shape=[jax.ShapeDtypeStruct((g, s, d), F32) for d in out_dims], grid=(g, s // ts),
            in_specs=[spec_of(kd, a, ts, nctx) for kd, a in zip(kinds, args)],
            out_specs=[pl.BlockSpec((None, ts, d), lambda g_, i: (g_, i, 0)) for d in out_dims],
            name=f"{name}_f_{g}x{s}", compiler_params=_params(),
        )(*args)

    def bwd_call(args, cts):
        g, s, ts, nctx = layout(args)
        fn = with_segments(ts)

        def body(*refs):
            in_refs, ct_refs, out_refs = refs[:n_in], refs[n_in:n_in + n_out], refs[n_in + n_out:]
            gi, i = pl.program_id(0), pl.program_id(1)
            vals = [r[...] for r in in_refs]

            def f(*dv):
                full = list(vals)
                for idx, v in zip(diff, dv):
                    full[idx] = v
                return tuple(fn(*full))

            _, vjp = jax.vjp(f, *[vals[idx] for idx in diff])
            grads = vjp(tuple(r[...] for r in ct_refs))
            for idx, o_ref, gr in zip(diff, out_refs, grads):
                if kinds[idx] == "row":
                    o_ref[...] = gr
                    continue
                if kinds[idx] == "glob":
                    first = jnp.logical_and(gi == 0, i == 0)
                else:
                    first = i == 0

                @pl.when(first)
                def _(o_ref=o_ref, gr=gr):
                    o_ref[...] = gr

                @pl.when(jnp.logical_not(first))
                def _(o_ref=o_ref, gr=gr):
                    o_ref[...] += gr

        in_specs = [spec_of(kd, a, ts, nctx) for kd, a in zip(kinds, args)]
        in_specs += [pl.BlockSpec((None, ts, d), lambda g_, i: (g_, i, 0)) for d in out_dims]
        return pl.pallas_call(
            body, out_shape=[jax.ShapeDtypeStruct(args[idx].shape, F32) for idx in diff], grid=(g, s // ts),
            in_specs=in_specs, out_specs=[spec_of(kinds[idx], args[idx], ts, nctx) for idx in diff],
            name=f"{name}_b_{g}x{s}", compiler_params=_params(),
        )(*args, *cts)

    @jax.custom_vjp
    def op(*args):
        return tuple(fwd_call(*args))

    def op_fwd(*args):
        return tuple(fwd_call(*args)), args

    def op_bwd(args, cts):
        grads = bwd_call(args, cts)
        full = [None] * n_in
        for idx, gr in zip(diff, grads):
            full[idx] = gr
        return tuple(jnp.zeros_like(a) if gfull is None else gfull for a, gfull in zip(args, full))

    op.defvjp(op_fwd, op_bwd)
    op.fwd_call, op.bwd_call = fwd_call, bwd_call
    return op


def make_modulate(d, n_ctx):
    one = make_rowwise(_fn_modulate, "modulate", ("row", "glob", "seg", "seg"), (d,), nctx_rows=n_ctx)
    two = make_rowwise(_fn_modulate_keep, "modulate_keep", ("row", "glob", "seg", "seg"), (d, d), nctx_rows=n_ctx)

    @jax.custom_vjp
    def op(x, g, shift, scale):
        return one.fwd_call(x, g, shift, scale)[0], x

    def fwd(x, g, shift, scale):
        return (one.fwd_call(x, g, shift, scale)[0], x), (x, g, shift, scale)

    def bwd(res, cts):
        return tuple(two.bwd_call(res, cts))

    op.defvjp(fwd, bwd)
    return op


def make_gated_add(d, n_ctx):
    add = make_rowwise(_fn_gated_add, "gated", ("row", "row", "seg"), (d,), nctx_rows=n_ctx)
    mul = make_rowwise(_fn_gate_mul, "gate_mul", ("row", "seg"), (d,), nctx_rows=n_ctx)

    @jax.custom_vjp
    def op(x, o, gate):
        return add.fwd_call(x, o, gate)[0]

    def fwd(x, o, gate):
        return add.fwd_call(x, o, gate)[0], (o, gate)

    def bwd(res, ct):
        do, dgate = mul.bwd_call(res, (ct,))
        return ct, do, dgate

    op.defvjp(fwd, bwd)
    return op


def _rms(x):
    return lax.rsqrt(jnp.mean(x * x, axis=-1, keepdims=True) + EPS)


def _fn_modulate(x, g, shift, scale):
    return ((x * _rms(x) * g) * (1.0 + scale) + shift,)


def _fn_modulate_keep(x, g, shift, scale):
    return _fn_modulate(x, g, shift, scale) + (x,)


def _fn_gated_add(x, o, gate):
    return (x + gate * o,)


def _fn_gate_mul(o, gate):
    return (gate * o,)


def _fn_norm(x, g):
    return (x * _rms(x) * g,)


def _fn_glu_pre(u, y0, y1, d):
    return (jax.nn.gelu(d * u + y0 + y1),)


def _fn_glu_post(z, t, bg):
    return (z * jax.nn.sigmoid(t + bg),)


def _rope_tables(n_ctx, n_lat, dh, start, rot_dim):
    t = jnp.arange(n_lat)
    rows = (t // GRID_W).astype(F32)
    cols = (t % GRID_W).astype(F32)
    axis_dim = rot_dim // 2
    freqs = ROPE_BASE ** (-jnp.arange(0, axis_dim, 2, dtype=F32) / axis_dim)
    ang_r = rows[:, None] * freqs
    ang_c = cols[:, None] * freqs
    ang = jnp.concatenate([ang_r, ang_r, ang_c, ang_c], axis=-1)
    cos = jnp.concatenate([jnp.ones((n_lat, start), F32), jnp.cos(ang)], axis=-1)
    sin = jnp.concatenate([jnp.zeros((n_lat, start), F32), jnp.sin(ang)], axis=-1)
    cos = jnp.concatenate([jnp.ones((n_ctx, dh), F32), cos], axis=0)
    sin = jnp.concatenate([jnp.zeros((n_ctx, dh), F32), sin], axis=0)
    return cos, sin


_NT = (((1,), (1,)), ((), ()))
_TN = (((0,), (0,)), ((), ()))


def _na_geometry(i, nc, rows):
    r = i - nc
    rs = jnp.clip(r - NA_WIN_R // 2, 0, rows - NA_WIN_R)
    is_ctx = i < nc
    cls = jnp.where(is_ctx, NA_WIN_R, r - rs)
    return jnp.where(is_ctx, 0, rs), cls


def _na_onehots():
    q = np.arange(GRID_W)[:, None]
    col = np.arange(GRID_W)[None, :]
    cs = np.clip(q - NA_WIN_C // 2, 0, GRID_W - NA_WIN_C)
    valid = (col >= cs) & (col < cs + NA_WIN_C)
    cidx = col - q + (NA_WIN_C - 1)
    n_b = 2 * NA_WIN_C - 1
    col_hot = np.zeros((LANE, GRID_W * GRID_W), np.float32)
    for qq in range(GRID_W):
        for cc in range(GRID_W):
            if valid[qq, cc]:
                col_hot[cidx[qq, cc], qq * GRID_W + cc] = 1.0
    row_hot = np.zeros((NA_WIN_R, NA_WIN_R, 2 * NA_WIN_R - 1), np.float32)
    for c in range(NA_WIN_R):
        for j in range(NA_WIN_R):
            row_hot[c, j, j - c + NA_WIN_R - 1] = 1.0
    mask = np.where(valid, 0.0, NEG).astype(np.float32)
    return col_hot, row_hot, mask, n_b


def na_bias_table(rpb):
    h = rpb.shape[0]
    col_hot, row_hot, mask, n_b = _na_onehots()
    t1 = jnp.einsum("cja,hab->hcjb", jnp.asarray(row_hot), rpb)
    t1 = jnp.pad(t1.reshape(h * NA_WIN_R * NA_WIN_R, n_b), ((0, 0), (0, LANE - n_b)))
    t2 = linear(t1, jnp.asarray(col_hot), True)
    t2 = t2.reshape(h, NA_WIN_R, NA_WIN_R, GRID_W, GRID_W) + jnp.asarray(mask)
    tab = jnp.transpose(t2, (0, 1, 3, 2, 4)).reshape(h, NA_WIN_R, GRID_W, NA_BAND)
    return jnp.concatenate([tab, jnp.full((h, 1, GRID_W, NA_BAND), NEG, F32)], axis=1)


def _first_step():
    return jnp.logical_and(pl.program_id(0) == 0, pl.program_id(1) == 0)


def _accum_out(ref, val, first):
    @pl.when(first)
    def _():
        ref[...] = val

    @pl.when(jnp.logical_not(first))
    def _():
        ref[...] += val


def _norm_head(xh, g):
    r = _rms(xh)
    yn = xh * r
    return yn * g, yn, r


def _norm_head_bwd(dy, yn, r, g):
    dg = jnp.sum(dy * yn, axis=0, keepdims=True)
    dyn = dy * g
    return r * (dyn - yn * jnp.mean(dyn * yn, axis=-1, keepdims=True)), dg


def _rope_signs(dh, start, rot_dim, n_heads):
    q = rot_dim // 4
    pos = np.arange(dh)
    quarter = (pos - start) // q
    inr = pos >= start
    sg = np.zeros((8, n_heads * dh), np.float32)
    sg[0] = np.tile(np.where(inr & (quarter % 2 == 0), -1.0, 0.0), n_heads)
    sg[1] = np.tile(np.where(inr & (quarter % 2 == 1), 1.0, 0.0), n_heads)
    return sg


def _rope_full(y, cos, sin, sg, q):
    w = y.shape[-1]
    rot = sg[0:1] * pltpu.roll(y, w - q, 1) + sg[1:2] * pltpu.roll(y, q, 1)
    return y * cos + rot * sin


def _rope_full_t(dy, cos, sin, sg, q):
    w = dy.shape[-1]
    z = dy * sin
    return dy * cos - sg[1:2] * pltpu.roll(z, q, 1) - sg[0:1] * pltpu.roll(z, w - q, 1)


def _hnr_call(x, g, cos, sin, sg, n_heads, q, dy=None):
    b, s, w = x.shape
    dh = w // n_heads
    ts = _pick(s, ROW_TILES)
    rope = cos is not None

    def body(*refs):
        refs = list(refs)
        x_ref, g_ref = refs[0], refs[1]
        k = 2
        if rope:
            cos_ref, sin_ref, sg_ref = refs[2], refs[3], refs[4]
            k = 5
        gv = g_ref[...]
        if dy is None:
            o_ref = refs[k]
            for h in range(n_heads):
                sl = slice(h * dh, (h + 1) * dh)
                o_ref[:, sl] = _norm_head(x_ref[:, sl], gv)[0]
            if rope:
                o_ref[...] = _rope_full(o_ref[...], cos_ref[...], sin_ref[...], sg_ref[...], q)
            return
        dy_ref, dx_ref, dg_ref = refs[k], refs[k + 1], refs[k + 2]
        src = dy_ref
        if rope:
            dx_ref[...] = _rope_full_t(dy_ref[...], cos_ref[...], sin_ref[...], sg_ref[...], q)
            src = dx_ref
        dg = jnp.zeros((1, dh), F32)
        for h in range(n_heads):
            sl = slice(h * dh, (h + 1) * dh)
            _, yn, r = _norm_head(x_ref[:, sl], gv)
            dxh, dgh = _norm_head_bwd(src[:, sl], yn, r, gv)
            dx_ref[:, sl] = dxh
            dg = dg + dgh
        _accum_out(dg_ref, dg, _first_step())

    row = pl.BlockSpec((None, ts, w), lambda bi, i: (bi, i, 0))
    whole = lambda a: pl.BlockSpec(a.shape, lambda bi, i: (0, 0))
    ins, specs = [x, g], [row, whole(g)]
    if rope:
        ins += [cos, sin, sg]
        specs += [pl.BlockSpec((ts, w), lambda bi, i: (i, 0)), pl.BlockSpec((ts, w), lambda bi, i: (i, 0)), whole(sg)]
    if dy is None:
        out_shape, out_specs = jax.ShapeDtypeStruct(x.shape, F32), row
    else:
        ins.append(dy)
        specs.append(row)
        out_shape = [jax.ShapeDtypeStruct(x.shape, F32), jax.ShapeDtypeStruct(g.shape, F32)]
        out_specs = [row, whole(g)]
    return pl.pallas_call(
        body, out_shape=out_shape, grid=(b, s // ts), in_specs=specs, out_specs=out_specs,
        name=f"hnr_{'b' if dy is not None else 'f'}_{n_heads}x{dh}_{int(rope)}", compiler_params=_params(),
    )(*ins)


@functools.partial(jax.custom_vjp, nondiff_argnums=(5, 6))
def head_norm_rope(x, g, cos, sin, sg, n_heads, q):
    return _hnr_call(x, g, cos, sin, sg, n_heads, q)


def _head_norm_rope_fwd(x, g, cos, sin, sg, n_heads, q):
    return _hnr_call(x, g, cos, sin, sg, n_heads, q), (x, g, cos, sin, sg)


def _head_norm_rope_bwd(n_heads, q, res, dy):
    x, g, cos, sin, sg = res
    dx, dg = _hnr_call(x, g, cos, sin, sg, n_heads, q, dy=dy)
    zero = lambda t: None if t is None else jnp.zeros_like(t)
    return dx, dg, zero(cos), zero(sin), zero(sg)


head_norm_rope.defvjp(_head_norm_rope_fwd, _head_norm_rope_bwd)


def _mla_k_call(kv, kr, g, cos, sin, sg, dkn=None):
    b, s, _ = kv.shape
    ts = _pick(s, ROW_TILES)
    hw = MLA_NOPE + MLA_V
    kn_w = MLA_HEADS * MLA_QK
    q = MLA_ROPE // 4

    def body(kv_ref, kr_ref, g_ref, cos_ref, sin_ref, sg_ref, *rest):
        gv = g_ref[...]
        krv = kr_ref[...]
        if dkn is None:
            (o_ref,) = rest
            for h in range(MLA_HEADS):
                kh = jnp.concatenate([kv_ref[:, h * hw:h * hw + MLA_NOPE], krv], axis=-1)
                o_ref[:, h * MLA_QK:(h + 1) * MLA_QK] = _norm_head(kh, gv)[0]
            o_ref[...] = _rope_full(o_ref[...], cos_ref[...], sin_ref[...], sg_ref[...], q)
            return
        dkn_ref, dkv_ref, dkr_ref, dg_ref, dy_ref = rest
        dy_ref[...] = _rope_full_t(dkn_ref[...], cos_ref[...], sin_ref[...], sg_ref[...], q)
        dg = jnp.zeros((1, MLA_QK), F32)
        dkr = jnp.zeros((ts, MLA_ROPE), F32)
        for h in range(MLA_HEADS):
            kh = jnp.concatenate([kv_ref[:, h * hw:h * hw + MLA_NOPE], krv], axis=-1)
            _, yn, r = _norm_head(kh, gv)
            dxh, dgh = _norm_head_bwd(dy_ref[:, h * MLA_QK:(h + 1) * MLA_QK], yn, r, gv)
            dkv_ref[:, h * hw:h * hw + MLA_NOPE] = dxh[:, :MLA_NOPE]
            dkv_ref[:, h * hw + MLA_NOPE:(h + 1) * hw] = jnp.zeros((ts, MLA_V), F32)
            dkr = dkr + dxh[:, MLA_NOPE:]
            dg = dg + dgh
        dkr_ref[...] = dkr
        _accum_out(dg_ref, dg, _first_step())

    row = lambda w: pl.BlockSpec((None, ts, w), lambda bi, i: (bi, i, 0))
    tab = pl.BlockSpec((ts, kn_w), lambda bi, i: (i, 0))
    whole = lambda a: pl.BlockSpec(a.shape, lambda bi, i: (0, 0))
    ins = [kv, kr, g, cos, sin, sg]
    specs = [row(kv.shape[2]), row(MLA_ROPE), whole(g), tab, tab, whole(sg)]
    scratch = []
    if dkn is None:
        out_shape, out_specs = jax.ShapeDtypeStruct((b, s, kn_w), F32), row(kn_w)
    else:
        ins.append(dkn)
        specs.append(row(kn_w))
        out_shape = [jax.ShapeDtypeStruct(kv.shape, F32), jax.ShapeDtypeStruct(kr.shape, F32),
                     jax.ShapeDtypeStruct(g.shape, F32)]
        out_specs = [row(kv.shape[2]), row(MLA_ROPE), whole(g)]
        scratch = [pltpu.VMEM((ts, kn_w), F32)]
    return pl.pallas_call(
        body, out_shape=out_shape, grid=(b, s // ts), in_specs=specs, out_specs=out_specs, scratch_shapes=scratch,
        name=f"mla_k_{'b' if dkn is not None else 'f'}", compiler_params=_params(),
    )(*ins)


@jax.custom_vjp
def mla_k_prep(kv, kr, g, cos, sin, sg):
    return _mla_k_call(kv, kr, g, cos, sin, sg)


def _mla_k_prep_fwd(kv, kr, g, cos, sin, sg):
    return _mla_k_call(kv, kr, g, cos, sin, sg), (kv, kr, g, cos, sin, sg)


def _mla_k_prep_bwd(res, dkn):
    kv, kr, g, cos, sin, sg = res
    dkv, dkr, dg = _mla_k_call(kv, kr, g, cos, sin, sg, dkn=dkn)
    return dkv, dkr, dg, jnp.zeros_like(cos), jnp.zeros_like(sin), jnp.zeros_like(sg)


mla_k_prep.defvjp(_mla_k_prep_fwd, _mla_k_prep_bwd)


class _HeadLayout:
    def __init__(self, groups, dq, dv, q_off, k_off, v_off, o_off, wq, wk, wv, wo, scale):
        self.groups, self.dq, self.dv, self.scale = groups, dq, dv, scale
        self.q_off, self.k_off, self.v_off, self.o_off = q_off, k_off, v_off, o_off
        self.wq, self.wk, self.wv, self.wo = wq, wk, wv, wo
        self.n_h = len(q_off)


def _gqa_layout():
    rep = GQA_Q_HEADS // GQA_KV_HEADS
    n_h = GQA_Q_HEADS // 2
    return _HeadLayout(2, HEAD_DIM, HEAD_DIM, [h * HEAD_DIM for h in range(n_h)], [(h // rep) * HEAD_DIM for h in range(n_h)],
                       [(h // rep) * HEAD_DIM for h in range(n_h)], [h * HEAD_DIM for h in range(n_h)],
                       n_h * HEAD_DIM, (n_h // rep) * HEAD_DIM, (n_h // rep) * HEAD_DIM, n_h * HEAD_DIM, HEAD_DIM ** -0.5)


def _mla_layout():
    n_h = MLA_HEADS // 2
    hw = MLA_NOPE + MLA_V
    return _HeadLayout(2, MLA_QK, MLA_V, [h * MLA_QK for h in range(n_h)], [h * MLA_QK for h in range(n_h)],
                       [h * hw + MLA_NOPE for h in range(n_h)], [h * MLA_V for h in range(n_h)],
                       n_h * MLA_QK, n_h * MLA_QK, n_h * hw, n_h * MLA_V, MLA_QK ** -0.5)


def _attn_tm_fwd(q, k, v, lay, n_ctx):
    b, s, _ = q.shape
    tq = min(256, n_ctx)
    nc = n_ctx // tq

    def body(q_ref, k_ref, v_ref, o_ref, lse_ref):
        def run(n_keys):
            for h in range(lay.n_h):
                qo, ko, vo, oo = lay.q_off[h], lay.k_off[h], lay.v_off[h], lay.o_off[h]
                qv = (q_ref[:, qo:qo + lay.dq] * lay.scale).astype(BF16)
                sc = lax.dot_general(qv, k_ref[0:n_keys, ko:ko + lay.dq].astype(BF16), _NT, preferred_element_type=F32)
                m = jnp.max(sc, axis=-1, keepdims=True)
                p = jnp.exp(sc - m)
                l = jnp.sum(p, axis=-1, keepdims=True)
                o = jnp.dot(p.astype(BF16), v_ref[0:n_keys, vo:vo + lay.dv].astype(BF16), preferred_element_type=F32)
                o_ref[:, oo:oo + lay.dv] = o / l
                lse_ref[:, h:h + 1] = m + jnp.log(l)

        pl.when(pl.program_id(2) < nc)(lambda: run(n_ctx))
        pl.when(pl.program_id(2) >= nc)(lambda: run(s))

    return pl.pallas_call(
        body, out_shape=[jax.ShapeDtypeStruct((b, s, lay.groups * lay.wo), F32),
                         jax.ShapeDtypeStruct((b, lay.groups, s, lay.n_h), F32)],
        grid=(b, lay.groups, s // tq),
        in_specs=[pl.BlockSpec((None, tq, lay.wq), lambda bi, g, i: (bi, i, g)),
                  pl.BlockSpec((None, s, lay.wk), lambda bi, g, i: (bi, 0, g)),
                  pl.BlockSpec((None, s, lay.wv), lambda bi, g, i: (bi, 0, g))],
        out_specs=[pl.BlockSpec((None, tq, lay.wo), lambda bi, g, i: (bi, i, g)),
                   pl.BlockSpec((None, None, tq, lay.n_h), lambda bi, g, i: (bi, g, i, 0))],
        name=f"attn_tm_f_{lay.dq}", compiler_params=_params(),
    )(q, k, v)


def _attn_tm_bwd(q, k, v, lse, o, do, lay, n_ctx):
    b, s, _ = q.shape
    tk = min(256, n_ctx)
    nc = n_ctx // tk

    def body(q_ref, k_ref, v_ref, lse_ref, o_ref, do_ref, dq_ref, dk_ref, dv_ref, delta_ref):
        @pl.when(pl.program_id(2) == 0)
        def _():
            dq_ref[...] = jnp.zeros_like(dq_ref)
            for h in range(lay.n_h):
                oo = lay.o_off[h]
                delta_ref[:, h:h + 1] = jnp.sum(o_ref[:, oo:oo + lay.dv] * do_ref[:, oo:oo + lay.dv], axis=-1,
                                                keepdims=True)

        def run(r0):
            dk_acc, dv_acc = {}, {}
            for h in range(lay.n_h):
                qo, ko, vo, oo = lay.q_off[h], lay.k_off[h], lay.v_off[h], lay.o_off[h]
                kh = k_ref[:, ko:ko + lay.dq].astype(BF16)
                vh = v_ref[:, vo:vo + lay.dv].astype(BF16)
                qv = (q_ref[r0:s, qo:qo + lay.dq] * lay.scale).astype(BF16)
                dob = do_ref[r0:s, oo:oo + lay.dv].astype(BF16)
                sc = lax.dot_general(qv, kh, _NT, preferred_element_type=F32)
                p = jnp.exp(sc - lse_ref[r0:s, h:h + 1])
                dvh = lax.dot_general(p.astype(BF16), dob, _TN, preferred_element_type=F32)
                dp = lax.dot_general(dob, vh, _NT, preferred_element_type=F32)
                dsb = (p * (dp - delta_ref[r0:s, h:h + 1])).astype(BF16)
                dkh = lax.dot_general(dsb, qv, _TN, preferred_element_type=F32)
                dq_ref[r0:s, qo:qo + lay.dq] += jnp.dot(dsb, kh, preferred_element_type=F32) * lay.scale
                dk_acc[ko] = dkh if ko not in dk_acc else dk_acc[ko] + dkh
                dv_acc[vo] = dvh if vo not in dv_acc else dv_acc[vo] + dvh
            if len(dv_acc) * lay.dv != lay.wv:
                dv_ref[...] = jnp.zeros_like(dv_ref)
            for ko, val in dk_acc.items():
                dk_ref[:, ko:ko + lay.dq] = val
            for vo, val in dv_acc.items():
                dv_ref[:, vo:vo + lay.dv] = val

        pl.when(pl.program_id(2) < nc)(lambda: run(0))
        pl.when(pl.program_id(2) >= nc)(lambda: run(n_ctx))

    full = lambda w: pl.BlockSpec((None, s, w), lambda bi, g, j: (bi, 0, g))
    blk = lambda w: pl.BlockSpec((None, tk, w), lambda bi, g, j: (bi, j, g))
    stat = pl.BlockSpec((None, None, s, lay.n_h), lambda bi, g, j: (bi, g, 0, 0))
    return pl.pallas_call(
        body, out_shape=[jax.ShapeDtypeStruct(q.shape, F32), jax.ShapeDtypeStruct(k.shape, F32),
                         jax.ShapeDtypeStruct(v.shape, F32)],
        grid=(b, lay.groups, s // tk),
        in_specs=[full(lay.wq), blk(lay.wk), blk(lay.wv), stat, full(lay.wo), full(lay.wo)],
        out_specs=[full(lay.wq), blk(lay.wk), blk(lay.wv)],
        scratch_shapes=[pltpu.VMEM((s, lay.n_h), F32)],
        name=f"attn_tm_b_{lay.dq}", compiler_params=_params(),
    )(q, k, v, lse, o, do)


def _make_attention_tm(lay):
    @functools.partial(jax.custom_vjp, nondiff_argnums=(3,))
    def op(q, k, v, n_ctx):
        return _attn_tm_fwd(q, k, v, lay, n_ctx)[0]

    def fwd(q, k, v, n_ctx):
        o, lse = _attn_tm_fwd(q, k, v, lay, n_ctx)
        return o, (q, k, v, o, lse)

    def bwd(n_ctx, res, do):
        q, k, v, o, lse = res
        return _attn_tm_bwd(q, k, v, lse, o, do, lay, n_ctx)

    op.defvjp(fwd, bwd)
    return op


gqa_attention = _make_attention_tm(_gqa_layout())
mla_attention = _make_attention_tm(_mla_layout())

NA_GROUPS_FWD = 1
NA_GROUPS_BWD = 2


def _na_tm_specs(s, nc, rows, groups):
    hg = NA_HEADS // groups
    w = hg * HEAD_DIM
    qs = pl.BlockSpec((None, GRID_W, w), lambda bi, g, i: (bi, i, g))
    ks = pl.BlockSpec((None, s, w), lambda bi, g, i: (bi, 0, g))
    bs = pl.BlockSpec((hg, None, GRID_W, NA_BAND), lambda bi, g, i: (g, _na_geometry(i, nc, rows)[1], 0, 0))
    ls = pl.BlockSpec((None, None, GRID_W, hg), lambda bi, g, i: (bi, g, i, 0))
    return hg, w, qs, ks, bs, ls


def _na_tm_scores(q_ref, k_ref, bias_ref, hd, n_ctx, start, scale):
    sl = slice(hd * HEAD_DIM, (hd + 1) * HEAD_DIM)
    qv = (q_ref[:, sl] * scale).astype(BF16)
    kc = k_ref[0:n_ctx, sl].astype(BF16)
    kb = k_ref[pl.ds(start, NA_BAND), sl].astype(BF16)
    s_c = lax.dot_general(qv, kc, _NT, preferred_element_type=F32)
    s_l = lax.dot_general(qv, kb, _NT, preferred_element_type=F32) + bias_ref[hd]
    return sl, qv, kc, kb, s_c, s_l


def _na_tm_fwd(q, k, v, bias, n_ctx):
    b, s, _ = q.shape
    nc = n_ctx // GRID_W
    rows = (s - n_ctx) // GRID_W
    scale = HEAD_DIM ** -0.5
    hg, w, qs, ks, bs, ls = _na_tm_specs(s, nc, rows, NA_GROUPS_FWD)

    def body(q_ref, k_ref, v_ref, bias_ref, o_ref, lse_ref):
        rs, _ = _na_geometry(pl.program_id(2), nc, rows)
        start = pl.multiple_of(n_ctx + rs * GRID_W, GRID_W)
        for hd in range(hg):
            sl, _, _, _, s_c, s_l = _na_tm_scores(q_ref, k_ref, bias_ref, hd, n_ctx, start, scale)
            m = jnp.maximum(jnp.max(s_c, axis=-1, keepdims=True), jnp.max(s_l, axis=-1, keepdims=True))
            p_c = jnp.exp(s_c - m)
            p_l = jnp.exp(s_l - m)
            l = jnp.sum(p_c, axis=-1, keepdims=True) + jnp.sum(p_l, axis=-1, keepdims=True)
            o = jnp.dot(p_c.astype(BF16), v_ref[0:n_ctx, sl].astype(BF16), preferred_element_type=F32)
            o = o + jnp.dot(p_l.astype(BF16), v_ref[pl.ds(start, NA_BAND), sl].astype(BF16), preferred_element_type=F32)
            o_ref[:, sl] = o / l
            lse_ref[:, hd:hd + 1] = m + jnp.log(l)

    return pl.pallas_call(
        body, out_shape=[jax.ShapeDtypeStruct(q.shape, F32), jax.ShapeDtypeStruct((b, NA_GROUPS_FWD, s, hg), F32)],
        grid=(b, NA_GROUPS_FWD, s // GRID_W), in_specs=[qs, ks, ks, bs], out_specs=[qs, ls],
        name=f"na_tm_f_{s}", compiler_params=_params(),
    )(q, k, v, bias)


def _na_tm_bwd(q, k, v, bias, o, lse, do, n_ctx):
    b, s, _ = q.shape
    nc = n_ctx // GRID_W
    rows = (s - n_ctx) // GRID_W
    scale = HEAD_DIM ** -0.5
    n_cls = NA_WIN_R + 1
    hg, w, qs, ks, bs, ls = _na_tm_specs(s, nc, rows, NA_GROUPS_BWD)
    lse = jnp.transpose(lse, (0, 2, 1, 3)).reshape(b, s, NA_GROUPS_BWD, hg)
    lse = jnp.transpose(lse, (0, 2, 1, 3))

    def body(q_ref, k_ref, v_ref, bias_ref, o_ref, lse_ref, do_ref, dq_ref, dk_ref, dv_ref, db_ref):
        i = pl.program_id(2)
        rs, cls = _na_geometry(i, nc, rows)
        _, cls_prev = _na_geometry(i - 1, nc, rows)
        start = pl.multiple_of(n_ctx + rs * GRID_W, GRID_W)
        first = jnp.logical_or(i == 0, cls != cls_prev)

        @pl.when(i == 0)
        def _():
            dk_ref[...] = jnp.zeros_like(dk_ref)
            dv_ref[...] = jnp.zeros_like(dv_ref)

        @pl.when(first)
        def _():
            db_ref[...] = jnp.zeros_like(db_ref)

        for hd in range(hg):
            sl, qv, kc, kb, s_c, s_l = _na_tm_scores(q_ref, k_ref, bias_ref, hd, n_ctx, start, scale)
            lse_v = lse_ref[:, hd:hd + 1]
            p_c = jnp.exp(s_c - lse_v)
            p_l = jnp.exp(s_l - lse_v)
            dov = do_ref[:, sl]
            dob = dov.astype(BF16)
            delta = jnp.sum(dov * o_ref[:, sl], axis=-1, keepdims=True)
            vc = v_ref[0:n_ctx, sl].astype(BF16)
            vb = v_ref[pl.ds(start, NA_BAND), sl].astype(BF16)
            ds_c = p_c * (lax.dot_general(dob, vc, _NT, preferred_element_type=F32) - delta)
            ds_l = p_l * (lax.dot_general(dob, vb, _NT, preferred_element_type=F32) - delta)
            dsc_b = ds_c.astype(BF16)
            dsl_b = ds_l.astype(BF16)
            dq_ref[:, sl] = (jnp.dot(dsc_b, kc, preferred_element_type=F32)
                             + jnp.dot(dsl_b, kb, preferred_element_type=F32)) * scale
            dk_ref[0:n_ctx, sl] += lax.dot_general(dsc_b, qv, _TN, preferred_element_type=F32)
            dk_ref[pl.ds(start, NA_BAND), sl] += lax.dot_general(dsl_b, qv, _TN, preferred_element_type=F32)
            dv_ref[0:n_ctx, sl] += lax.dot_general(p_c.astype(BF16), dob, _TN, preferred_element_type=F32)
            dv_ref[pl.ds(start, NA_BAND), sl] += lax.dot_general(p_l.astype(BF16), dob, _TN, preferred_element_type=F32)
            db_ref[hd] += ds_l

    dbs = pl.BlockSpec((None, hg, None, GRID_W, NA_BAND), lambda bi, g, i: (bi, g, _na_geometry(i, nc, rows)[1], 0, 0))
    return pl.pallas_call(
        body,
        out_shape=[jax.ShapeDtypeStruct(q.shape, F32), jax.ShapeDtypeStruct(q.shape, F32), jax.ShapeDtypeStruct(q.shape, F32),
                   jax.ShapeDtypeStruct((b, NA_HEADS, n_cls, GRID_W, NA_BAND), F32)],
        grid=(b, NA_GROUPS_BWD, s // GRID_W), in_specs=[qs, ks, ks, bs, qs, ls, qs], out_specs=[qs, ks, ks, dbs],
        name=f"na_tm_b_{s}", compiler_params=_params(),
    )(q, k, v, bias, o, lse, do)


@functools.partial(jax.custom_vjp, nondiff_argnums=(4,))
def na_attention_tm(q, k, v, bias, n_ctx):
    return _na_tm_fwd(q, k, v, bias, n_ctx)[0]


def _na_attention_tm_fwd(q, k, v, bias, n_ctx):
    o, lse = _na_tm_fwd(q, k, v, bias, n_ctx)
    return o, (q, k, v, bias, o, lse)


def _na_attention_tm_bwd(n_ctx, res, do):
    q, k, v, bias, o, lse = res
    dq, dk, dv, db = _na_tm_bwd(q, k, v, bias, o, lse, do, n_ctx)
    return dq, dk, dv, _sum_rows(db.reshape(db.shape[0], -1, NA_BAND), db.shape[0]).reshape(db.shape[1:])


na_attention_tm.defvjp(_na_attention_tm_fwd, _na_attention_tm_bwd)


def _cmul(ar, ai, br, bi):
    return ar * br - ai * bi, ar * bi + ai * br


def _s5_chunk(n_ctx):
    return min(256, n_ctx)


def _s5_powers(a_re, a_im, t_len):
    exps = np.concatenate([np.minimum(2 ** np.arange(8), t_len), np.arange(1, 9), [t_len] + [0] * 7,
                           np.arange(0, t_len, 8)]).astype(np.float32)
    a_re, a_im = lax.stop_gradient(a_re), lax.stop_gradient(a_im)
    mag = jnp.sqrt(a_re * a_re + a_im * a_im)
    th = jnp.arctan2(a_im, a_re)
    t = jnp.asarray(exps)[:, None]
    pm = jnp.where(t == 0, 1.0, jnp.exp(t * jnp.log(jnp.maximum(mag, 1e-37))) * (mag > 0))
    return jnp.stack([pm * jnp.cos(t * th), pm * jnp.sin(t * th)])


def _s5_tables(pw, t_len, rev, conj=False):
    if conj:
        pw = pw * jnp.asarray([1.0, -1.0], F32)[:, None, None]
    if rev:
        pw = jnp.concatenate([pw[:, :8], pw[:, 8:16][:, ::-1], pw[:, 16:24], pw[:, 24:][:, ::-1]], axis=1)
    return pw


def _scan_chunk(x_re, x_im, tab_ref, hin_re, hin_im, rev, t_len, xs_ref, es_ref):
    outs = [_scan_slab(x_re[:, k:k + LANE], x_im[:, k:k + LANE], tab_ref, hin_re[:, k:k + LANE], hin_im[:, k:k + LANE],
                       rev, t_len, xs_ref, es_ref, k) for k in range(0, x_re.shape[-1], LANE)]
    return tuple(jnp.concatenate([o[t] for o in outs], axis=-1) for t in range(4))


def _scan_slab(x_re, x_im, tab_ref, hin_re, hin_im, rev, t_len, xs_ref, es_ref, k0):
    lanes = LANE
    n2 = t_len // 8
    tab_ref = tab_ref.at[:, :, k0:k0 + LANE]
    rin = lax.broadcasted_iota(jnp.int32, (t_len, lanes), 0) & 7
    for li, sh in enumerate((1, 2, 4)):
        m_re, m_im = tab_ref[0, li:li + 1, :], tab_ref[1, li:li + 1, :]
        amt = sh if not rev else t_len - sh
        c_re, c_im = _cmul(m_re, m_im, pltpu.roll(x_re, amt, 0), pltpu.roll(x_im, amt, 0))
        ok = (rin >= sh) if not rev else (rin < 8 - sh)
        x_re = x_re + jnp.where(ok, c_re, 0.0)
        x_im = x_im + jnp.where(ok, c_im, 0.0)
    xr_ref, xi_ref = xs_ref
    xr_ref[...] = x_re
    xi_ref[...] = x_im
    off = 0 if rev else 7
    e_re = xr_ref[pl.ds(off, n2, stride=8), :]
    e_im = xi_ref[pl.ds(off, n2, stride=8), :]
    row2 = lax.broadcasted_iota(jnp.int32, (n2, lanes), 0)
    sh, li = 1, 3
    while sh < n2:
        m_re, m_im = tab_ref[0, li:li + 1, :], tab_ref[1, li:li + 1, :]
        amt = sh if not rev else n2 - sh
        c_re, c_im = _cmul(m_re, m_im, pltpu.roll(e_re, amt, 0), pltpu.roll(e_im, amt, 0))
        ok = (row2 >= sh) if not rev else (row2 < n2 - sh)
        e_re = e_re + jnp.where(ok, c_re, 0.0)
        e_im = e_im + jnp.where(ok, c_im, 0.0)
        sh, li = sh * 2, li + 1
    es_ref[0] = e_re
    es_ref[1] = e_im
    last = 0 if rev else n2 - 1
    t_re, t_im = _cmul(tab_ref[0, 16:17, :], tab_ref[1, 16:17, :], hin_re, hin_im)
    hout_re = es_ref[0, last:last + 1, :] + t_re
    hout_im = es_ref[1, last:last + 1, :] + t_im
    amt = 1 if not rev else n2 - 1
    ok = (row2 >= 1) if not rev else (row2 < n2 - 1)
    k_re, k_im = _cmul(tab_ref[0, 24:24 + n2, :], tab_ref[1, 24:24 + n2, :], hin_re, hin_im)
    c_re = jnp.where(ok, pltpu.roll(e_re, amt, 0), 0.0) + k_re
    c_im = jnp.where(ok, pltpu.roll(e_im, amt, 0), 0.0) + k_im
    tp_re, tp_im = tab_ref[0, 8:16, :][None], tab_ref[1, 8:16, :][None]
    add_re, add_im = _cmul(tp_re, tp_im, c_re[:, None, :], c_im[:, None, :])
    h_re = xr_ref[...] + add_re.reshape(t_len, lanes)
    h_im = xi_ref[...] + add_im.reshape(t_len, lanes)
    return h_re, h_im, hout_re, hout_im


def _s5_order(j, n_chunks, nc, rev):
    if not rev:
        return j
    return jnp.where(j < nc, nc - 1 - j, n_chunks - 1 - (j - nc))


def _s5_fwd(u, tab, b_bd, c_bd, n_ctx, rev):
    b, s, w = u.shape
    lanes = b_bd.shape[-1]
    t_len = _s5_chunk(n_ctx)
    n_chunks, nc = s // t_len, n_ctx // t_len

    def body(u_ref, tab_ref, b_ref, c_ref, y_ref, h_ref, hin_ref, carry_ref, xr_ref, xi_ref, es_ref):
        xs_ref = (xr_ref, xi_ref)

        @pl.when(pl.program_id(1) == 0)
        def _():
            carry_ref[...] = jnp.zeros_like(carry_ref)

        ub = u_ref[...].astype(BF16)
        x_re = jnp.dot(ub, b_ref[0].astype(BF16), preferred_element_type=F32)
        x_im = jnp.dot(ub, b_ref[1].astype(BF16), preferred_element_type=F32)
        hin_re, hin_im = carry_ref[0, 0:1, :], carry_ref[1, 0:1, :]
        hin_ref[...] = carry_ref[...]
        h_re, h_im, ho_re, ho_im = _scan_chunk(x_re, x_im, tab_ref, hin_re, hin_im, rev, t_len, xs_ref, es_ref)
        carry_ref[0] = jnp.broadcast_to(ho_re, (8, lanes))
        carry_ref[1] = jnp.broadcast_to(ho_im, (8, lanes))
        h_ref[0] = h_re
        h_ref[1] = h_im
        y_ref[...] = (jnp.dot(h_re.astype(BF16), c_ref[0].astype(BF16), preferred_element_type=F32)
                      - jnp.dot(h_im.astype(BF16), c_ref[1].astype(BF16), preferred_element_type=F32))

    order = lambda j: _s5_order(j, n_chunks, nc, rev)
    whole = lambda arr: pl.BlockSpec(arr.shape, lambda bi, j: (0,) * arr.ndim)
    return pl.pallas_call(
        body,
        out_shape=[jax.ShapeDtypeStruct((b, s, w), F32), jax.ShapeDtypeStruct((2, b, s, lanes), F32),
                   jax.ShapeDtypeStruct((2, b, n_chunks, 8, lanes), F32)],
        grid=(b, n_chunks),
        in_specs=[pl.BlockSpec((None, t_len, w), lambda bi, j: (bi, order(j), 0)), whole(tab), whole(b_bd), whole(c_bd)],
        out_specs=[pl.BlockSpec((None, t_len, w), lambda bi, j: (bi, order(j), 0)),
                   pl.BlockSpec((2, None, t_len, lanes), lambda bi, j: (0, bi, order(j), 0)),
                   pl.BlockSpec((2, None, None, 8, lanes), lambda bi, j: (0, bi, order(j), 0, 0))],
        scratch_shapes=[pltpu.VMEM((2, 8, lanes), F32), pltpu.VMEM((t_len, LANE), F32), pltpu.VMEM((t_len, LANE), F32),
                        pltpu.VMEM((2, t_len // 8, LANE), F32)],
        name=f"s5_f_{s}_{int(rev)}", compiler_params=_params(),
    )(u, tab, b_bd, c_bd)


def _s5_bwd(u, tab_adj, b_bd, c_bd, h, hin, dy, n_ctx, rev):
    b, s, w = u.shape
    lanes = b_bd.shape[-1]
    t_len = _s5_chunk(n_ctx)
    n_chunks, nc = s // t_len, n_ctx // t_len
    arev = not rev

    def body(u_ref, tab_ref, b_ref, c_ref, h_ref, hin_ref, dy_ref, du_ref, db_ref, dc_ref, da_ref,
             carry_ref, xr_ref, xi_ref, es_ref):
        xs_ref = (xr_ref, xi_ref)
        first = jnp.logical_and(pl.program_id(0) == 0, pl.program_id(1) == 0)

        @pl.when(pl.program_id(1) == 0)
        def _():
            carry_ref[...] = jnp.zeros_like(carry_ref)

        dyv = dy_ref[...]
        dyb = dyv.astype(BF16)
        dn = (((1,), (1,)), ((), ()))
        dt = (((0,), (0,)), ((), ()))
        x_re = lax.dot_general(dyb, c_ref[0].astype(BF16), dn, preferred_element_type=F32)
        x_im = -lax.dot_general(dyb, c_ref[1].astype(BF16), dn, preferred_element_type=F32)
        g_re, g_im, go_re, go_im = _scan_chunk(x_re, x_im, tab_ref, carry_ref[0, 0:1, :], carry_ref[1, 0:1, :],
                                               arev, t_len, xs_ref, es_ref)
        carry_ref[0] = jnp.broadcast_to(go_re, (8, lanes))
        carry_ref[1] = jnp.broadcast_to(go_im, (8, lanes))
        h_re, h_im = h_ref[0], h_ref[1]
        gb_re, gb_im = g_re.astype(BF16), g_im.astype(BF16)
        du_ref[...] = (lax.dot_general(gb_re, b_ref[0].astype(BF16), dn, preferred_element_type=F32)
                       + lax.dot_general(gb_im, b_ref[1].astype(BF16), dn, preferred_element_type=F32))
        ub = u_ref[...].astype(BF16)
        db_re = lax.dot_general(ub, gb_re, dt, preferred_element_type=F32)
        db_im = lax.dot_general(ub, gb_im, dt, preferred_element_type=F32)
        dc_re = lax.dot_general(h_re.astype(BF16), dyb, dt, preferred_element_type=F32)
        dc_im = -lax.dot_general(h_im.astype(BF16), dyb, dt, preferred_element_type=F32)
        row = lax.broadcasted_iota(jnp.int32, (t_len, lanes), 0)
        amt = 1 if not rev else t_len - 1
        edge = (row == 0) if not rev else (row == t_len - 1)
        hp_re = jnp.where(edge, hin_ref[0, 0:1, :], pltpu.roll(h_re, amt, 0))
        hp_im = jnp.where(edge, hin_ref[1, 0:1, :], pltpu.roll(h_im, amt, 0))
        da_re = jnp.sum(g_re * hp_re + g_im * hp_im, axis=0, keepdims=True)
        da_im = jnp.sum(g_im * hp_re - g_re * hp_im, axis=0, keepdims=True)

        @pl.when(first)
        def _():
            db_ref[0], db_ref[1] = db_re, db_im
            dc_ref[0], dc_ref[1] = dc_re, dc_im
            da_ref[0] = jnp.broadcast_to(da_re, (8, lanes))
            da_ref[1] = jnp.broadcast_to(da_im, (8, lanes))

        @pl.when(jnp.logical_not(first))
        def _():
            db_ref[0] += db_re
            db_ref[1] += db_im
            dc_ref[0] += dc_re
            dc_ref[1] += dc_im
            da_ref[0] += jnp.broadcast_to(da_re, (8, lanes))
            da_ref[1] += jnp.broadcast_to(da_im, (8, lanes))

    order = lambda j: _s5_order(n_chunks - 1 - j, n_chunks, nc, rev)
    whole = lambda arr: pl.BlockSpec(arr.shape, lambda bi, j: (0,) * arr.ndim)
    us = pl.BlockSpec((None, t_len, w), lambda bi, j: (bi, order(j), 0))
    return pl.pallas_call(
        body,
        out_shape=[jax.ShapeDtypeStruct((b, s, w), F32), jax.ShapeDtypeStruct(b_bd.shape, F32),
                   jax.ShapeDtypeStruct(c_bd.shape, F32), jax.ShapeDtypeStruct((2, 8, lanes), F32)],
        grid=(b, n_chunks),
        in_specs=[us, whole(tab_adj), whole(b_bd), whole(c_bd),
                  pl.BlockSpec((2, None, t_len, lanes), lambda bi, j: (0, bi, order(j), 0)),
                  pl.BlockSpec((2, None, None, 8, lanes), lambda bi, j: (0, bi, order(j), 0, 0)), us],
        out_specs=[us, whole(b_bd), whole(c_bd), pl.BlockSpec((2, 8, lanes), lambda bi, j: (0, 0, 0))],
        scratch_shapes=[pltpu.VMEM((2, 8, lanes), F32), pltpu.VMEM((t_len, LANE), F32), pltpu.VMEM((t_len, LANE), F32),
                        pltpu.VMEM((2, t_len // 8, LANE), F32)],
        name=f"s5_b_{s}_{int(rev)}", compiler_params=_params(),
    )(u, tab_adj, b_bd, c_bd, h, hin, dy)


@functools.partial(jax.custom_vjp, nondiff_argnums=(4, 5))
def s5_direction(u, a, b_bd, c_bd, n_ctx, rev):
    t_len = _s5_chunk(n_ctx)
    return _s5_fwd(u, _s5_tables(_s5_powers(a[0], a[1], t_len), t_len, rev), b_bd, c_bd, n_ctx, rev)[0]


def _s5_direction_fwd(u, a, b_bd, c_bd, n_ctx, rev):
    t_len = _s5_chunk(n_ctx)
    pw = _s5_powers(a[0], a[1], t_len)
    y, h, hin = _s5_fwd(u, _s5_tables(pw, t_len, rev), b_bd, c_bd, n_ctx, rev)
    return y, (u, pw, b_bd, c_bd, h, hin)


def _s5_direction_bwd(n_ctx, rev, res, dy):
    u, pw, b_bd, c_bd, h, hin = res
    tab_adj = _s5_tables(pw, _s5_chunk(n_ctx), not rev, conj=True)
    du, db, dc, da = _s5_bwd(u, tab_adj, b_bd, c_bd, h, hin, dy, n_ctx, rev)
    return du, da[:, 0, :], db, dc


s5_direction.defvjp(_s5_direction_fwd, _s5_direction_bwd)


def _s5_discretize(lam_re, lam_im, log_dt, b_re, b_im):
    dt = jnp.exp(log_dt)[:, None]
    mag = jnp.exp(lam_re * dt)
    a_re = mag * jnp.cos(lam_im * dt)
    a_im = mag * jnp.sin(lam_im * dt)
    den = jnp.square(lam_re) + jnp.square(lam_im)
    f_re = ((a_re - 1.0) * lam_re + a_im * lam_im) / den
    f_im = (a_im * lam_re - (a_re - 1.0) * lam_im) / den
    bb_re = f_re[..., None] * b_re - f_im[..., None] * b_im
    bb_im = f_re[..., None] * b_im + f_im[..., None] * b_re
    return a_re, a_im, bb_re, bb_im


def _block_diag(t):
    g, r, c = t.shape
    return (jnp.eye(g, dtype=F32)[:, None, :, None] * t[:, :, None, :]).reshape(g * r, g * c)


def _loss_head(y, target):
    b, n, d = y.shape
    ts = _pick(n, (256, 128, 64))

    def body(y_ref, t_ref, loss_ref, dy_ref):
        first = jnp.logical_and(pl.program_id(0) == 0, pl.program_id(1) == 0)
        err = y_ref[...] - t_ref[...]
        dy_ref[...] = err * (1.0 / d)
        part = 0.5 * jnp.sum(jnp.sum(err * err, axis=-1, keepdims=True) * (1.0 / d), axis=0, keepdims=True)
        part = jnp.broadcast_to(part, (8, LANE))

        @pl.when(first)
        def _():
            loss_ref[...] = part

        @pl.when(jnp.logical_not(first))
        def _():
            loss_ref[...] += part

    blk = pl.BlockSpec((None, ts, d), lambda bi, i: (bi, i, 0))
    return pl.pallas_call(
        body, out_shape=[jax.ShapeDtypeStruct((8, LANE), F32), jax.ShapeDtypeStruct((b, n, d), F32)],
        grid=(b, n // ts), in_specs=[blk, blk], out_specs=[pl.BlockSpec((8, LANE), lambda bi, i: (0, 0)), blk],
        name="loss_head", compiler_params=_params(),
    )(y, target)


def _adamw(w, g, m, v):
    shape = w.shape
    n = int(np.prod(shape))
    cols = shape[-1]
    r = n // cols
    tr = _pick(r, (512, 256, 128, 64, 32, 16, 8))
    c1 = 1.0 / (1.0 - ADAM_B1 ** ADAM_STEP)
    c2 = 1.0 / (1.0 - ADAM_B2 ** ADAM_STEP)

    def body(w_ref, g_ref, m_ref, v_ref, d_ref, mo_ref, vo_ref):
        gv = g_ref[...]
        m2 = ADAM_B1 * m_ref[...] + (1.0 - ADAM_B1) * gv
        v2 = ADAM_B2 * v_ref[...] + (1.0 - ADAM_B2) * (gv * gv)
        d_ref[...] = -ADAM_LR * ((m2 * c1) / (jnp.sqrt(v2 * c2) + ADAM_EPS) + ADAM_WD * w_ref[...])
        mo_ref[...] = m2
        vo_ref[...] = v2

    blk = pl.BlockSpec((tr, cols), lambda i: (i, 0))
    outs = pl.pallas_call(
        body, out_shape=[jax.ShapeDtypeStruct((r, cols), F32)] * 3, grid=(r // tr,),
        in_specs=[blk] * 4, out_specs=[blk] * 3, name=f"adamw_{r}x{cols}", compiler_params=_params(),
    )(*[t.reshape(r, cols) for t in (w, g, m, v)])
    return tuple(o.reshape(shape) for o in outs)


def _sum_rows(x, n):
    _, r, c = x.shape
    tr = _pick(r, (512, 256, 128, 64, 32, 16, 8))

    def body(x_ref, o_ref):
        acc = x_ref[0]
        for j in range(1, n):
            acc = acc + x_ref[j]
        o_ref[...] = acc

    return pl.pallas_call(
        body, out_shape=jax.ShapeDtypeStruct((r, c), F32), grid=(r // tr,),
        in_specs=[pl.BlockSpec((n, tr, c), lambda i: (0, i, 0))], out_specs=pl.BlockSpec((tr, c), lambda i: (i, 0)),
        name=f"sum{n}_{r}x{c}", compiler_params=_params(),
    )(x)


def _accumulate(parts, out_dtype):
    r, c = parts[0].shape[-2:]
    tr = _pick(r, (1152, 1024, 768, 576, 512, 256, 128, 64, 32, 16))

    def body(*refs):
        acc = None
        for ref in refs[:-1]:
            terms = [ref[j] for j in range(ref.shape[0])] if len(ref.shape) == 3 else [ref[...]]
            for t in terms:
                acc = t.astype(F32) if acc is None else acc + t.astype(F32)
        refs[-1][...] = acc.astype(out_dtype)

    specs = [pl.BlockSpec((p.shape[0], tr, c), lambda i: (0, i, 0)) if p.ndim == 3 else pl.BlockSpec((tr, c), lambda i: (i, 0))
             for p in parts]
    tag = "_".join(str(p.shape[0]) if p.ndim == 3 else "1" for p in parts)
    return pl.pallas_call(
        body, out_shape=jax.ShapeDtypeStruct((r, c), out_dtype), grid=(r // tr,), in_specs=specs,
        out_specs=pl.BlockSpec((tr, c), lambda i: (i, 0)), name=f"accumulate_{tag}_{r}x{c}_{jnp.dtype(out_dtype).name}",
        compiler_params=_params(),
    )(*parts)


def _add2(x, y):
    shape = x.shape
    c = shape[-1]
    r = int(np.prod(shape)) // c
    tr = _pick(r, (512, 256, 128, 64, 32, 16, 8))

    def body(x_ref, y_ref, o_ref):
        o_ref[...] = x_ref[...] + y_ref[...]

    blk = pl.BlockSpec((tr, c), lambda i: (i, 0))
    return pl.pallas_call(
        body, out_shape=jax.ShapeDtypeStruct((r, c), F32), grid=(r // tr,), in_specs=[blk, blk], out_specs=blk,
        name=f"add2_{r}x{c}", compiler_params=_params(),
    )(x.reshape(r, c), y.reshape(r, c)).reshape(shape)


_FLIPS = ((1, 0), (0, 1), (1, 1))


def _me():
    return lax.axis_index("x"), lax.axis_index("y"), lax.axis_index("c")


def allgather8(v):
    m_per, n = v.shape

    def body(x_ref, out_ref, send_sems, recv_sems, local_sem):
        x, y, c = _me()
        me, sibling = (x, y, c), (x, y, 1 - c)
        chips = [(1 - x, y), (x, 1 - y), (1 - x, 1 - y)]

        def rows(px, py, pc):
            return out_ref.at[pl.ds((4 * px + 2 * py + pc) * m_per, m_per), :]

        def copy(k, block, to, src=None):
            return pltpu.make_async_remote_copy(
                src_ref=rows(*block) if src is None else src, dst_ref=rows(*block),
                send_sem=send_sems.at[k], recv_sem=recv_sems.at[k], device_id=to, device_id_type=MESH)

        mine = pltpu.make_async_copy(x_ref, rows(*me), local_sem)
        mine.start()
        first = [copy(0, me, sibling, src=x_ref)]
        first += [copy(1 + j, me, (*chip, c), src=x_ref) for j, chip in enumerate(chips)]
        for cp in first:
            cp.start()
        passed = [copy(4 + j, (*chip, c), sibling) for j, chip in enumerate(chips)]
        for j, chip in enumerate(chips):
            copy(1 + j, (*chip, c), me).wait_recv()
            passed[j].start()
        copy(0, sibling, me).wait_recv()
        for j, chip in enumerate(chips):
            copy(4 + j, (*chip, 1 - c), me).wait_recv()
        for cp in first + passed:
            cp.wait_send()
        mine.wait()

    return pl.pallas_call(
        body, out_shape=jax.ShapeDtypeStruct((N_DEV * m_per, n), v.dtype), in_specs=[VMEM_SPEC], out_specs=VMEM_SPEC,
        scratch_shapes=[pltpu.SemaphoreType.DMA((7,)), pltpu.SemaphoreType.DMA((7,)), pltpu.SemaphoreType.DMA],
        name=f"allgather8_{m_per}x{n}", compiler_params=_params(),
    )(v)


def _row_chunks(rows, tile_rows, want):
    n = want
    while n > 1 and rows % (n * tile_rows):
        n //= 2
    return [(i * (rows // n), rows // n) for i in range(n)]


def _remote(src, dst, send_sem, recv_sem, to):
    return pltpu.make_async_remote_copy(src_ref=src, dst_ref=dst, send_sem=send_sem, recv_sem=recv_sem, device_id=to,
                                        device_id_type=MESH)


def plane_allgather(big, small):
    rows = big.shape[0]
    rh = rows // 2
    rq = rh // 2
    tile = 16 if big.dtype == BF16 else 8
    assert rq % tile == 0
    ch_full = _row_chunks(rows, tile, 8)
    ch_half = _row_chunks(rh, tile, 4)

    def body(big_ref, small_ref, obig_ref, osmall_ref, send_sems, recv_sems, relay_send, relay_recv, fwd_send, fwd_recv,
             own_send, own_recv):
        x, y, c = _me()
        me = 2 * x + y
        sibling = (x, y, 1 - c)
        nbr_x, nbr_y, diag = (1 - x, y, c), (x, 1 - y, c), (1 - x, 1 - y, c)
        xi, yi, di = 2 * (1 - x) + y, 2 * x + (1 - y), 2 * (1 - x) + (1 - y)
        base, obase = c * rh, (1 - c) * rh
        mine, other = pl.ds(base, rh), pl.ds(obase, rh)
        qa, qb = pl.ds(base, rq), pl.ds(base + rq, rq)
        for st, sz in ch_full:
            sl = pl.ds(st, sz)
            _remote(big_ref.at[sl], obig_ref.at[me, sl], own_send.at[0], own_recv.at[0], sibling).start()
        _remote(small_ref, osmall_ref.at[me], own_send.at[1], own_recv.at[1], sibling).start()
        for j, peer in enumerate((nbr_x, nbr_y)):
            for st, sz in ch_half:
                sl = pl.ds(base + st, sz)
                _remote(big_ref.at[sl], obig_ref.at[me, sl], send_sems.at[j], recv_sems.at[j], peer).start()
        for j, peer in enumerate((nbr_x, nbr_y, diag)):
            _remote(small_ref, osmall_ref.at[me], send_sems.at[3 + j], recv_sems.at[3 + j], peer).start()

        def pass_on(k, slot, sl):
            _remote(obig_ref.at[slot, sl], obig_ref.at[slot, sl], fwd_send.at[k], fwd_recv.at[k], sibling).start()

        _remote(big_ref.at[mine], obig_ref.at[xi, mine], send_sems.at[0], recv_sems.at[0], nbr_x).wait_recv()
        _remote(obig_ref.at[xi, qa], obig_ref.at[xi, qa], relay_send.at[0], relay_recv.at[0], nbr_y).start()
        pass_on(0, xi, mine)
        _remote(big_ref.at[mine], obig_ref.at[yi, mine], send_sems.at[1], recv_sems.at[1], nbr_y).wait_recv()
        _remote(obig_ref.at[yi, qb], obig_ref.at[yi, qb], relay_send.at[1], relay_recv.at[1], nbr_x).start()
        pass_on(1, yi, mine)
        _remote(obig_ref.at[di, qa], obig_ref.at[di, qa], relay_send.at[0], relay_recv.at[0], nbr_y).wait_recv()
        pass_on(2, di, qa)
        _remote(obig_ref.at[di, qb], obig_ref.at[di, qb], relay_send.at[1], relay_recv.at[1], nbr_x).wait_recv()
        pass_on(3, di, qb)
        for j, (peer, slot) in enumerate(((nbr_x, xi), (nbr_y, yi), (diag, di))):
            _remote(small_ref, osmall_ref.at[slot], send_sems.at[3 + j], recv_sems.at[3 + j], peer).wait_recv()
        oqa, oqb = pl.ds(obase, rq), pl.ds(obase + rq, rq)
        for k, (slot, sl) in enumerate(((xi, other), (yi, other), (di, oqa), (di, oqb))):
            _remote(obig_ref.at[slot, sl], obig_ref.at[slot, sl], fwd_send.at[k], fwd_recv.at[k], sibling).wait_recv()
        for k, (slot, sl) in enumerate(((xi, mine), (yi, mine), (di, qa), (di, qb))):
            _remote(obig_ref.at[slot, sl], obig_ref.at[slot, sl], fwd_send.at[k], fwd_recv.at[k], sibling).wait_send()
        for j, peer in enumerate((nbr_x, nbr_y)):
            _remote(big_ref.at[mine], obig_ref.at[me, mine], send_sems.at[j], recv_sems.at[j], peer).wait_send()
        for j, peer in enumerate((nbr_x, nbr_y, diag)):
            _remote(small_ref, osmall_ref.at[me], send_sems.at[3 + j], recv_sems.at[3 + j], peer).wait_send()
        _remote(obig_ref.at[xi, qa], obig_ref.at[xi, qa], relay_send.at[0], relay_recv.at[0], nbr_y).wait_send()
        _remote(obig_ref.at[yi, qb], obig_ref.at[yi, qb], relay_send.at[1], relay_recv.at[1], nbr_x).wait_send()
        _remote(big_ref, obig_ref.at[me], own_send.at[0], own_recv.at[0], sibling).wait()
        _remote(small_ref, osmall_ref.at[me], own_send.at[1], own_recv.at[1], sibling).wait()

    dma = pltpu.SemaphoreType.DMA
    return pl.pallas_call(
        body, out_shape=[jax.ShapeDtypeStruct((N_PLANE,) + big.shape, big.dtype),
                         jax.ShapeDtypeStruct((N_PLANE,) + small.shape, small.dtype)],
        in_specs=[ANY, ANY], out_specs=[ANY, ANY],
        scratch_shapes=[dma((6,)), dma((6,)), dma((2,)), dma((2,)), dma((4,)), dma((4,)), dma((2,)), dma((2,))],
        name="plane_allgather", compiler_params=_params(),
    )(big, small)


def neighbour_exchange(to_x, to_y):
    tile = 16 if to_x.dtype == BF16 else 8
    chunks = _row_chunks(to_x.shape[0], tile, 4)

    def body(ax_ref, ay_ref, fx_ref, fy_ref, send_sems, recv_sems):
        x, y, c = _me()
        for k, (src, dst, peer) in enumerate(((ax_ref, fx_ref, (1 - x, y, c)), (ay_ref, fy_ref, (x, 1 - y, c)))):
            for st, sz in chunks:
                sl = pl.ds(st, sz)
                _remote(src.at[sl], dst.at[sl], send_sems.at[k], recv_sems.at[k], peer).start()
        for k, (src, dst, peer) in enumerate(((ax_ref, fx_ref, (1 - x, y, c)), (ay_ref, fy_ref, (x, 1 - y, c)))):
            _remote(src, dst, send_sems.at[k], recv_sems.at[k], peer).wait()

    shape = jax.ShapeDtypeStruct(to_x.shape, to_x.dtype)
    return pl.pallas_call(
        body, out_shape=[shape, shape], in_specs=[ANY, ANY], out_specs=[ANY, ANY],
        scratch_shapes=[pltpu.SemaphoreType.DMA((2,)), pltpu.SemaphoreType.DMA((2,))],
        name=f"neighbour_exchange_{to_x.shape[0]}", compiler_params=_params(),
    )(to_x, to_y)


def sibling_halves(buf):
    n_blk, _, rows, cols = buf.shape
    tile = 16 if buf.dtype == BF16 else 8
    chunks = _row_chunks(rows, tile, 2)

    def body(buf_ref, got_ref, send_sem, recv_sem):
        x, y, c = _me()
        for j in range(n_blk):
            for st, sz in chunks:
                sl = pl.ds(st, sz)
                _remote(buf_ref.at[j, 1 - c, sl], got_ref.at[j, sl], send_sem, recv_sem, (x, y, 1 - c)).start()
        _remote(got_ref, got_ref, send_sem, recv_sem, (x, y, 1 - c)).wait()

    return pl.pallas_call(
        body, out_shape=jax.ShapeDtypeStruct((n_blk, rows, cols), buf.dtype), in_specs=[ANY], out_specs=ANY,
        scratch_shapes=[pltpu.SemaphoreType.DMA, pltpu.SemaphoreType.DMA],
        name="sibling_halves", compiler_params=_params(),
    )(buf)


def sibling_swap(s):
    tile = 16 if s.dtype == BF16 else 8
    chunks = _row_chunks(s.shape[0], tile, 8)

    def body(s_ref, got_ref, send_sem, recv_sem):
        x, y, c = _me()
        for st, sz in chunks:
            sl = pl.ds(st, sz)
            _remote(s_ref.at[sl], got_ref.at[sl], send_sem, recv_sem, (x, y, 1 - c)).start()
        _remote(s_ref, got_ref, send_sem, recv_sem, (x, y, 1 - c)).wait()

    return pl.pallas_call(
        body, out_shape=jax.ShapeDtypeStruct(s.shape, s.dtype), in_specs=[ANY], out_specs=ANY,
        scratch_shapes=[pltpu.SemaphoreType.DMA, pltpu.SemaphoreType.DMA],
        name="sibling_swap", compiler_params=_params(),
    )(s)


def _op(cache, fn, name, kinds, out_dims, **kw):
    key = (name, tuple(out_dims), tuple(sorted(kw.items())))
    if key not in cache:
        cache[key] = make_rowwise(fn, name, kinds, out_dims, **kw)
    return cache[key]


def _even_mixer(ops, a, w, n_ctx):
    b, s, d = a.shape
    proj = linear(a.reshape(b * s, d), w["e_w_in"]).reshape(b, s, -1)
    q, k, v, u = jnp.split(proj, [GQA_Q_W, GQA_Q_W + GQA_KV_W, GQA_Q_W + 2 * GQA_KV_W], axis=-1)
    cos, sin = _rope_tables(n_ctx, s - n_ctx, HEAD_DIM, 0, HEAD_DIM)
    shift = HEAD_DIM // 4
    qn = head_norm_rope(q, w["e_g_q"][None], jnp.tile(cos, (1, GQA_Q_HEADS)), jnp.tile(sin, (1, GQA_Q_HEADS)),
                        jnp.asarray(_rope_signs(HEAD_DIM, 0, HEAD_DIM, GQA_Q_HEADS)), GQA_Q_HEADS, shift)
    kn = head_norm_rope(k, w["e_g_k"][None], jnp.tile(cos, (1, GQA_KV_HEADS)), jnp.tile(sin, (1, GQA_KV_HEADS)),
                        jnp.asarray(_rope_signs(HEAD_DIM, 0, HEAD_DIM, GQA_KV_HEADS)), GQA_KV_HEADS, shift)
    att = gqa_attention(qn, kn, v, n_ctx)
    ys = []
    for dr in range(2):
        a_re, a_im, bb_re, bb_im = _s5_discretize(w["ssm_lam_re"][dr], w["ssm_lam_im"][dr], w["ssm_log_dt"][dr],
                                                  w["ssm_b_re"][dr], w["ssm_b_im"][dr])
        a_flat = jnp.stack([a_re.reshape(-1), a_im.reshape(-1)])
        b_bd = jnp.stack([_block_diag(jnp.swapaxes(bb_re, 1, 2)), _block_diag(jnp.swapaxes(bb_im, 1, 2))])
        c_bd = jnp.stack([_block_diag(jnp.swapaxes(w["ssm_c_re"][dr], 1, 2)),
                          _block_diag(jnp.swapaxes(w["ssm_c_im"][dr], 1, 2))])
        ys.append(s5_direction(u, a_flat, b_bd, c_bd, n_ctx, dr == 1))
    pre = _op(ops, _fn_glu_pre, "glu_pre", ("row", "row", "row", "glob"), (SSM_WIDTH,))
    post = _op(ops, _fn_glu_post, "glu_post", ("row", "row", "glob"), (SSM_WIDTH,))
    z = pre(u, ys[0], ys[1], w["ssm_d"][None])[0]
    t = linear(z.reshape(b * s, SSM_WIDTH), w["ssm_w_glu"]).reshape(b, s, SSM_WIDTH)
    ssm = post(z, t, w["ssm_b_glu"][None])[0]
    mix = jnp.concatenate([att, ssm], axis=-1)
    return linear(mix.reshape(b * s, -1), w["e_w_out"]).reshape(b, s, d)


def _odd_mixer(ops, a, w, n_ctx):
    b, s, d = a.shape
    w_in = jnp.pad(w["o_w_in"], ((0, 0), (0, ODD_IN_PAD - ODD_IN_W)))
    proj = linear(a.reshape(b * s, d), w_in).reshape(b, s, -1)
    c1 = MLA_Q_RANK
    c2 = c1 + MLA_KV_RANK
    c3 = c2 + MLA_ROPE
    cq, ckv, kr, nq, nk, nv, _ = jnp.split(proj, [c1, c2, c3, c3 + NA_W, c3 + 2 * NA_W, ODD_IN_W], axis=-1)
    nrm = lambda wd: _op(ops, _fn_norm, f"norm{wd}", ("row", "glob"), (wd,))
    cqn = nrm(MLA_Q_RANK)(cq, w["mla_g_cq"][None])[0]
    ckvn = nrm(MLA_KV_RANK)(ckv, w["mla_g_ckv"][None])[0]
    q = linear(cqn.reshape(b * s, -1), w["mla_w_uq"]).reshape(b, s, -1)
    kv = linear(ckvn.reshape(b * s, -1), w["mla_w_ukv"]).reshape(b, s, -1)
    cos, sin = _rope_tables(n_ctx, s - n_ctx, MLA_QK, MLA_NOPE, MLA_ROPE)
    cos, sin = jnp.tile(cos, (1, MLA_HEADS)), jnp.tile(sin, (1, MLA_HEADS))
    sg = jnp.asarray(_rope_signs(MLA_QK, MLA_NOPE, MLA_ROPE, MLA_HEADS))
    mq = head_norm_rope(q, w["mla_g_q"][None], cos, sin, sg, MLA_HEADS, MLA_ROPE // 4)
    mk = mla_k_prep(kv, kr, w["mla_g_k"][None], cos, sin, sg)
    mla = mla_attention(mq, mk, kv, n_ctx)
    nqn = head_norm_rope(nq, w["na_g_q"][None], None, None, None, NA_HEADS, 0)
    nkn = head_norm_rope(nk, w["na_g_k"][None], None, None, None, NA_HEADS, 0)
    na = na_attention_tm(nqn, nkn, nv, na_bias_table(w["na_rpb"]), n_ctx)
    mix = jnp.concatenate([mla, na], axis=-1)
    return linear(mix.reshape(b * s, -1), w["o_w_out"]).reshape(b, s, d)


_EVEN_KEYS = ("e_w_in", "e_w_out", "e_g_q", "e_g_k", "ssm_lam_re", "ssm_lam_im", "ssm_log_dt", "ssm_b_re", "ssm_b_im",
              "ssm_c_re", "ssm_c_im", "ssm_d", "ssm_w_glu", "ssm_b_glu")
_ODD_KEYS = ("o_w_in", "o_w_out", "mla_g_cq", "mla_g_ckv", "mla_w_uq", "mla_w_ukv", "mla_g_q", "mla_g_k", "na_g_q",
             "na_g_k", "na_rpb")


def _trunk(x_all, mods, w, n_ctx):
    ops = {}
    depth = mods.shape[0]
    b, s, d = x_all.shape
    modulate = make_modulate(d, n_ctx)
    gated = make_gated_add(d, n_ctx)
    x = x_all
    for i in range(depth):
        j = i // 2
        m = [mods[i][:, :, r:r + 1, :] for r in range(N_MOD)]
        a, x = modulate(x, w["g_norm1"][i][None], m[0], m[1])
        if i % 2 == 0:
            o = _even_mixer(ops, a, {k: w[k][j] for k in _EVEN_KEYS}, n_ctx)
        else:
            o = _odd_mixer(ops, a, {k: w[k][j] for k in _ODD_KEYS}, n_ctx)
        x = gated(x, o, m[2])
        a2, x = modulate(x, w["g_norm2"][i][None], m[3], m[4])
        f = ffn(a2.reshape(b * s, d), w["w_ff1"][i], w["w_ff2"][i]).reshape(b, s, d)
        x = gated(x, f, m[5])
    return x[:, n_ctx:]


def local_step(x, ctx, mods, w, loss_target):
    n_ctx = ctx.shape[1]
    x_all = jnp.concatenate([ctx, x], axis=1)
    y, vjp = jax.vjp(lambda xa, md, ww: _trunk(xa, md, ww, n_ctx), x_all, mods, w)
    loss_tile, dy = _loss_head(y, loss_target)
    dx_all, dmods, dw = vjp(dy)
    return loss_tile[0, 0], dx_all[:, n_ctx:], dmods, dw


_SHARDED = (("w_ff1", 2), ("w_ff2", 1), ("e_w_in", 2), ("e_w_out", 1), ("o_w_in", 2), ("o_w_out", 1),
            ("mla_w_uq", 2), ("mla_w_ukv", 2), ("ssm_w_glu", 1))
_SHARDED_SMALL = (("mla_g_cq", 1), ("mla_g_ckv", 1))
_REPLICATED = ("g_norm1", "g_norm2", "e_g_q", "e_g_k", "ssm_lam_re", "ssm_lam_im", "ssm_log_dt", "ssm_b_re", "ssm_b_im",
               "ssm_c_re", "ssm_c_im", "ssm_d", "ssm_b_glu", "mla_g_q", "mla_g_k", "na_g_q", "na_g_k", "na_rpb")
_WEIGHTS = ("c_ctx", "w_mod", "b_mod", "g_norm1", "g_norm2", "w_ff1", "w_ff2", "e_w_in", "e_w_out", "e_g_q", "e_g_k",
            "ssm_lam_re", "ssm_lam_im", "ssm_log_dt", "ssm_b_re", "ssm_b_im", "ssm_c_re", "ssm_c_im", "ssm_d",
            "ssm_w_glu", "ssm_b_glu", "o_w_in", "o_w_out", "mla_g_cq", "mla_g_ckv", "mla_w_uq", "mla_w_ukv", "mla_g_q",
            "mla_g_k", "na_g_q", "na_g_k", "na_rpb")
_PACK_ROWS = 64


def _pack(arrs, dtype, cols=1024, row_mult=_PACK_ROWS):
    blocks, tail, off = [], [], 0
    for a in arrs:
        n = int(np.prod(a.shape))
        if not tail and off % cols == 0 and n % cols == 0:
            blocks.append(a.astype(dtype).reshape(-1, cols))
        else:
            tail.append(a.astype(dtype).reshape(-1))
        off += n
    rows = -(-off // cols)
    pad = (-rows) % row_mult * cols + rows * cols - off
    if tail or pad:
        blocks.append(jnp.concatenate(tail + [jnp.zeros((pad,), dtype)]).reshape(-1, cols))
    return jnp.concatenate(blocks, axis=0)


def _unpack(packed, shapes):
    cols = packed.shape[-1]
    packed = packed.reshape(-1, cols)
    out, off = [], 0
    for sh in shapes:
        n = int(np.prod(sh))
        if off % cols == 0 and n % cols == 0:
            out.append(packed[off // cols:(off + n) // cols].reshape(sh))
        else:
            r0, r1 = off // cols, -(-(off + n) // cols)
            out.append(packed[r0:r1].reshape(-1)[off - r0 * cols:off - r0 * cols + n].reshape(sh))
        off += n
    return out


def kernel(x, c, ctx, c_ctx, w_mod, b_mod, g_norm1, g_norm2, w_ff1, w_ff2, e_w_in, e_w_out, e_g_q, e_g_k, ssm_lam_re, ssm_lam_im, ssm_log_dt, ssm_b_re, ssm_b_im, ssm_c_re, ssm_c_im, ssm_d, ssm_w_glu, ssm_b_glu, o_w_in, o_w_out, mla_g_cq, mla_g_ckv, mla_w_uq, mla_w_ukv, mla_g_q, mla_g_k, na_g_q, na_g_k, na_rpb, loss_target, m_c_ctx, m_w_mod, m_b_mod, m_g_norm1, m_g_norm2, m_w_ff1, m_w_ff2, m_e_w_in, m_e_w_out, m_e_g_q, m_e_g_k, m_ssm_lam_re, m_ssm_lam_im, m_ssm_log_dt, m_ssm_b_re, m_ssm_b_im, m_ssm_c_re, m_ssm_c_im, m_ssm_d, m_ssm_w_glu, m_ssm_b_glu, m_o_w_in, m_o_w_out, m_mla_g_cq, m_mla_g_ckv, m_mla_w_uq, m_mla_w_ukv, m_mla_g_q, m_mla_g_k, m_na_g_q, m_na_g_k, m_na_rpb, v_c_ctx, v_w_mod, v_b_mod, v_g_norm1, v_g_norm2, v_w_ff1, v_w_ff2, v_e_w_in, v_e_w_out, v_e_g_q, v_e_g_k, v_ssm_lam_re, v_ssm_lam_im, v_ssm_log_dt, v_ssm_b_re, v_ssm_b_im, v_ssm_c_re, v_ssm_c_im, v_ssm_d, v_ssm_w_glu, v_ssm_b_glu, v_o_w_in, v_o_w_out, v_mla_g_cq, v_mla_g_ckv, v_mla_w_uq, v_mla_w_ukv, v_mla_g_q, v_mla_g_k, v_na_g_q, v_na_g_k, v_na_rpb):
    env = dict(locals())
    weights = {n: env[n] for n in _WEIGHTS}
    mom_m = {n: env["m_" + n] for n in _WEIGHTS}
    mom_v = {n: env["v_" + n] for n in _WEIGHTS}
    ax, ay, ac = _me()
    plane = 2 * ax + ay
    dev = 4 * ax + 2 * ay + ac
    b_loc, d = c.shape
    depth = w_mod.shape[0]
    n_all = N_DEV * b_loc
    mod_cols = w_mod.shape[2]

    big = _pack([weights[n] for n, _ in _SHARDED], BF16)
    small = _pack([weights[n] for n, _ in _SHARDED_SMALL], F32, cols=LANE, row_mult=8)
    g_big, g_small = plane_allgather(big, small)
    full = {n: weights[n] for n in _REPLICATED}
    parts = [_unpack(g_big[j], [weights[n].shape for n, _ in _SHARDED]) for j in range(N_PLANE)]
    for t, (n, axis) in enumerate(_SHARDED):
        full[n] = [jnp.concatenate([parts[j][t][l] for j in range(N_PLANE)], axis=axis - 1)
                   for l in range(weights[n].shape[0])]
    parts_s = [_unpack(g_small[j], [weights[n].shape for n, _ in _SHARDED_SMALL]) for j in range(N_PLANE)]
    for t, (n, axis) in enumerate(_SHARDED_SMALL):
        full[n] = jnp.concatenate([parts_s[j][t] for j in range(N_PLANE)], axis=axis)

    rows_pad = 8 * ((n_all + 1 + 7) // 8)
    c_all = allgather8(jnp.pad(c, ((0, 8 - b_loc), (0, 0)))).reshape(N_DEV, 8, d)[:, :b_loc].reshape(n_all, d)
    cond_raw = jnp.concatenate([c_all, c_ctx[None], jnp.zeros((rows_pad - n_all - 1, d), F32)], axis=0)
    b_cols = lax.dynamic_slice_in_dim(b_mod, plane * mod_cols, mod_cols, axis=1)
    mod_loc = jnp.stack([_mm(cond_raw, w_mod[i], a_act="silu") + b_cols[i][None] for i in range(depth)])
    mod_g = allgather8(mod_loc.reshape(depth * rows_pad, mod_cols)).reshape(N_PLANE, 2, depth, rows_pad, mod_cols)
    mod_all = jnp.concatenate([mod_g[j, 0] for j in range(N_PLANE)], axis=-1)
    m_lat = lax.dynamic_slice_in_dim(mod_all, dev * b_loc, b_loc, axis=1)
    m_ctx = jnp.broadcast_to(mod_all[:, n_all][:, None], m_lat.shape)
    mods = jnp.stack([m_ctx, m_lat], axis=2).reshape(depth, b_loc, 2, N_MOD, d)

    loss_part, grad_x, dmods, dw = local_step(x, ctx, mods, full, loss_target)

    dm = dmods.reshape(depth, b_loc, 2, N_MOD * d)
    dm_rows = jnp.concatenate([dm[:, :, 1], jnp.sum(dm[:, :, 0], axis=1, keepdims=True)], axis=1)
    rep_shapes = [weights[n].shape for n in _REPLICATED] + [(1,)]
    small_pack = _pack([dm_rows] + [dw[n] for n in _REPLICATED] + [loss_part.reshape(1)], F32, cols=1024, row_mult=8)
    sp_rows = small_pack.shape[0]
    gathered = allgather8(small_pack).reshape(N_DEV, sp_rows, 1024)
    n_dm = depth * (b_loc + 1) * N_MOD * d
    dm_all = gathered.reshape(N_DEV, -1)[:, :n_dm].reshape(N_DEV, depth, b_loc + 1, N_MOD * d)
    rep_sum = _sum_rows(gathered, N_DEV).reshape(-1)
    rep_parts = _unpack(rep_sum[n_dm:], rep_shapes)
    rep_grads = dict(zip(_REPLICATED, rep_parts[:-1]))
    loss = rep_parts[-1][0]
    d_ctx_row = rep_sum[:n_dm].reshape(depth, b_loc + 1, N_MOD * d)[:, b_loc]
    d_lat_rows = jnp.transpose(dm_all[:, :, :b_loc], (1, 0, 2, 3)).reshape(depth, n_all, N_MOD * d)
    d_mod_all = jnp.concatenate([d_lat_rows, d_ctx_row[:, None],
                                 jnp.zeros((depth, rows_pad - n_all - 1, N_MOD * d), F32)], axis=1)
    grads = dict(rep_grads)
    grads["b_mod"] = jnp.sum(d_mod_all, axis=1)
    d_cols = lax.dynamic_slice_in_dim(d_mod_all, plane * mod_cols, mod_cols, axis=2)
    grads["w_mod"] = jnp.stack([_mm(cond_raw, d_cols[i], ta=True, a_act="silu") for i in range(depth)])
    d_cond = _mm(d_cols[0], w_mod[0], tb=True)
    for i in range(1, depth):
        d_cond = _add2(d_cond, _mm(d_cols[i], w_mod[i], tb=True))
    d_cond_g = allgather8(d_cond[n_all:n_all + 8] if rows_pad - n_all >= 8 else
                          jnp.pad(d_cond[n_all:], ((0, 8 - (rows_pad - n_all)), (0, 0)))).reshape(N_PLANE, 2, 8, d)
    d_silu = _sum_rows(d_cond_g[:, 0], N_PLANE)[0]
    sg = jax.nn.sigmoid(c_ctx)
    grads["c_ctx"] = d_silu * (sg * (1.0 + c_ctx * (1.0 - sg)))

    def shards_of(g, axis, j):
        layers = g if isinstance(g, (list, tuple)) else [g]
        ax = axis - 1 if isinstance(g, (list, tuple)) else axis
        n = layers[0].shape[ax] // N_PLANE
        return [lax.slice_in_dim(t, j * n, (j + 1) * n, axis=ax) for t in layers]

    send = jnp.stack([_pack([t for n, axis in _SHARDED + _SHARDED_SMALL for t in shards_of(dw[n], axis, j)], BF16)
                      for j in range(N_PLANE)])
    rows_h = send.shape[1] // 2
    send = send.reshape(N_PLANE, 2, rows_h, 1024)
    mine = lax.dynamic_index_in_dim(send, ac, 1, keepdims=False).reshape(N_PLANE * rows_h, 1024)
    theirs = sibling_halves(send).reshape(N_PLANE * rows_h, 1024)
    chip_sum = _accumulate([mine, theirs], BF16).reshape(N_PLANE, rows_h, 1024)
    pick = lambda k: lax.dynamic_index_in_dim(chip_sum, k, 0, keepdims=False)
    own, for_x, for_y, for_diag = pick(plane), pick(plane ^ 2), pick(plane ^ 1), pick(plane ^ 3)
    rows_q = rows_h // 2
    from_x, from_y = neighbour_exchange(for_diag[:rows_q], for_diag[rows_q:])
    zeros_q = jnp.zeros((rows_q, 1024), BF16)
    relayed = jnp.concatenate([zeros_q, from_y, from_x, zeros_q])
    merged = _accumulate([jnp.concatenate([for_x, for_y]), relayed], BF16)
    got_x, got_y = neighbour_exchange(merged[:rows_h], merged[rows_h:])
    done = _accumulate([own, got_x, got_y], BF16)
    both = jnp.stack([done, sibling_swap(done)])
    flat = jnp.where(ac == 0, both, both[::-1]).astype(F32).reshape(-1, 1024)
    shard_shapes = [weights[n].shape for n, _ in _SHARDED] + [weights[n].shape for n, _ in _SHARDED_SMALL]
    for (n, _), g in zip(_SHARDED + _SHARDED_SMALL, _unpack(flat, shard_shapes)):
        grads[n] = g

    big_names = ("w_mod",) + tuple(n for n, _ in _SHARDED)
    small_names = tuple(n for n in _WEIGHTS if n not in big_names)
    delta, new_m, new_v = {}, {}, {}
    for n in big_names:
        delta[n], new_m[n], new_v[n] = _adamw(weights[n], grads[n], mom_m[n], mom_v[n])
    sm_shapes = [weights[n].shape for n in small_names]
    packed = [_pack([src[n] for n in small_names], F32, cols=1024, row_mult=8)
              for src in (weights, grads, mom_m, mom_v)]
    for dst, res in zip((delta, new_m, new_v), _adamw(*packed)):
        dst.update(dict(zip(small_names, _unpack(res, sm_shapes))))

    return (loss, grad_x, *[grads[n] for n in _WEIGHTS], *[delta[n] for n in _WEIGHTS],
            *[new_m[n] for n in _WEIGHTS], *[new_v[n] for n in _WEIGHTS])
```

```python
import functools

import numpy as np
import jax
import jax.numpy as jnp
from jax import lax
from jax.experimental import pallas as pl
from jax.experimental.pallas import tpu as pltpu

F32 = jnp.float32
BF16 = jnp.bfloat16
HI = lax.Precision.HIGHEST
MESH = pl.DeviceIdType.MESH
ANY = pl.BlockSpec(memory_space=pl.ANY)
VMEM_SPEC = pl.BlockSpec(memory_space=pltpu.VMEM)

GRID_W = 64
HEAD_DIM = 64
ROPE_BASE = 10000.0
EPS = 1e-6
N_MOD = 6
GQA_Q_HEADS, GQA_KV_HEADS = 12, 4
GQA_Q_W, GQA_KV_W = GQA_Q_HEADS * HEAD_DIM, GQA_KV_HEADS * HEAD_DIM
SSM_WIDTH, SSM_GROUP, SSM_STATE = 256, 16, 64
SSM_GROUPS = SSM_WIDTH // SSM_GROUP
SSM_LANES = SSM_GROUPS * SSM_STATE
MLA_HEADS, MLA_Q_RANK, MLA_KV_RANK, MLA_NOPE, MLA_ROPE, MLA_V = 8, 512, 256, 64, 32, 64
MLA_QK = MLA_NOPE + MLA_ROPE
NA_HEADS, NA_WIN_R, NA_WIN_C = 8, 8, 16
NA_W = NA_HEADS * HEAD_DIM
NA_BAND = NA_WIN_R * GRID_W
ODD_IN_W = MLA_Q_RANK + MLA_KV_RANK + MLA_ROPE + 3 * NA_W
ODD_IN_PAD = 2560
ADAM_LR, ADAM_B1, ADAM_B2, ADAM_EPS, ADAM_WD, ADAM_STEP = 0.001, 0.9, 0.999, 1e-08, 0.01, 10
NEG = -1e30
VMEM_LIMIT = 56 * 1024 * 1024
LANE = 128
MM_TILE_M = (1152, 1024, 768, 512, 256, 128)
MM_TILE_N = (1280, 1024, 768, 512, 256, 128)
MM_TILE_K = (1152, 1024, 768, 512, 256, 128)
ROW_TILES = (576, 512, 384, 256, 128, 64)
N_PLANE = 4
N_DEV = 8


def _pick(n, cands):
    for c in cands:
        if n % c == 0:
            return c
    return n


def _params(**kw):
    return pltpu.CompilerParams(vmem_limit_bytes=VMEM_LIMIT, **kw)


def _mm(a, b, *, ta=False, tb=False, a_act=None, epi=None, e=None, exact=False, out_dtype=F32):
    m, kd = (a.shape[1], a.shape[0]) if ta else a.shape
    n = b.shape[0] if tb else b.shape[1]
    tm = _pick(m, MM_TILE_M)
    tn = _pick(n, MM_TILE_N)
    tk = _pick(kd, MM_TILE_K)
    nk = kd // tk
    dn = (((0 if ta else 1,), (1 if tb else 0,)), ((), ()))
    narrow = jnp.dtype(out_dtype) != jnp.dtype(F32)

    def body(*refs):
        if narrow and epi is not None:
            a_ref, b_ref, e_ref, out_ref, o_ref = refs
        elif narrow:
            a_ref, b_ref, out_ref, o_ref = refs
        elif epi is None:
            a_ref, b_ref, o_ref = refs
        else:
            a_ref, b_ref, e_ref, o_ref = refs
        k = pl.program_id(2)
        av = a_ref[...]
        if a_act == "relu2":
            av = jnp.square(jnp.maximum(av.astype(F32), 0.0))
        elif a_act == "silu":
            av = av * jax.nn.sigmoid(av)
        bv = b_ref[...]
        if exact:
            p = lax.dot_general(av, bv, dn, precision=HI, preferred_element_type=F32)
        else:
            p = lax.dot_general(av.astype(BF16), bv.astype(BF16), dn, preferred_element_type=F32)

        @pl.when(k == 0)
        def _():
            o_ref[...] = p

        @pl.when(k > 0)
        def _():
            o_ref[...] += p

        if epi == "drelu2":
            @pl.when(k == nk - 1)
            def _():
                o_ref[...] = o_ref[...] * (2.0 * jnp.maximum(e_ref[...].astype(F32), 0.0))

        if narrow:
            @pl.when(k == nk - 1)
            def _():
                out_ref[...] = o_ref[...].astype(out_dtype)

    a_spec = pl.BlockSpec((tk, tm), lambda i, j, k: (k, i)) if ta else pl.BlockSpec((tm, tk), lambda i, j, k: (i, k))
    b_spec = pl.BlockSpec((tn, tk), lambda i, j, k: (j, k)) if tb else pl.BlockSpec((tk, tn), lambda i, j, k: (k, j))
    o_spec = pl.BlockSpec((tm, tn), lambda i, j, k: (i, j))
    ins, specs = [a, b], [a_spec, b_spec]
    if epi is not None:
        ins.append(e)
        specs.append(o_spec)
    name = f"mm_{m}x{kd}x{n}_{int(ta)}{int(tb)}_{a_act}_{epi}_{int(exact)}_{jnp.dtype(out_dtype).name}"
    return pl.pallas_call(
        body, out_shape=jax.ShapeDtypeStruct((m, n), out_dtype), grid=(m // tm, n // tn, nk),
        in_specs=specs, out_specs=o_spec, name=name, compiler_params=_params(),
        scratch_shapes=[pltpu.VMEM((tm, tn), F32)] if narrow else [],
    )(*ins)


@functools.partial(jax.custom_vjp, nondiff_argnums=(2,))
def _linear(a, w, exact):
    return _mm(a, w, exact=exact)


def _linear_fwd(a, w, exact):
    return _mm(a, w, exact=exact), (a, w)


def _linear_bwd(exact, res, g):
    a, w = res
    return _mm(g, w, tb=True, exact=exact), _mm(a, g, ta=True, exact=exact, out_dtype=w.dtype)


_linear.defvjp(_linear_fwd, _linear_bwd)


def linear(a, w, exact=False):
    return _linear(a, w, exact)


@jax.custom_vjp
def ffn(a, w1, w2):
    return _mm(_mm(a, w1, out_dtype=BF16), w2, a_act="relu2")


def _ffn_fwd(a, w1, w2):
    h1 = _mm(a, w1, out_dtype=BF16)
    return _mm(h1, w2, a_act="relu2"), (a, w1, w2, h1)


def _ffn_bwd(res, g):
    a, w1, w2, h1 = res
    dh1 = _mm(g, w2, tb=True, epi="drelu2", e=h1, out_dtype=BF16)
    dw2 = _mm(h1, g, ta=True, a_act="relu2", out_dtype=w2.dtype)
    return _mm(dh1, w1, tb=True), _mm(a, dh1, ta=True, out_dtype=w1.dtype), dw2


ffn.defvjp(_ffn_fwd, _ffn_bwd)


def make_rowwise(fn, name, kinds, out_dims, nctx_rows=0, whole_seq=False):
    n_in = len(kinds)
    n_out = len(out_dims)
    diff = [i for i, kd in enumerate(kinds) if kd in ("row", "glob", "seg")]
    seg_idx = [i for i, kd in enumerate(kinds) if kd == "seg"]

    def layout(args):
        row0 = args[kinds.index("row")]
        g, s = row0.shape[0], row0.shape[1]
        ts = s if whole_seq else _pick(s, ROW_TILES)
        return g, s, ts, 0

    def spec_of(kind, arr, ts, nctx):
        if kind == "row":
            return pl.BlockSpec((None, ts, arr.shape[2]), lambda g, i: (g, i, 0))
        if kind == "tab":
            return pl.BlockSpec((ts, arr.shape[1]), lambda g, i: (i, 0))
        if kind in ("const", "glob"):
            return pl.BlockSpec(arr.shape, lambda g, i: (0, 0))
        return pl.BlockSpec((None,) + arr.shape[1:], lambda g, i: (g, 0, 0, 0))

    def with_segments(ts):
        if not seg_idx:
            return fn

        def wrapped(*vals):
            rows = pl.program_id(1) * ts + lax.broadcasted_iota(jnp.int32, (ts, 1), 0)
            vals = list(vals)
            for idx in seg_idx:
                vals[idx] = jnp.where(rows < nctx_rows, vals[idx][0], vals[idx][1])
            return fn(*vals)

        return wrapped

    def fwd_call(*args):
        g, s, ts, nctx = layout(args)
        fn = with_segments(ts)

        def body(*refs):
            vals = [r[...] for r in refs[:n_in]]
            outs = fn(*vals)
            for o_ref, o in zip(refs[n_in:], outs):
                o_ref[...] = o

        return pl.pallas_call(
            body, out_shape=[jax.ShapeDtypeStruct((g, s, d), F32) for d in out_dims], grid=(g, s // ts),
            in_specs=[spec_of(kd, a, ts, nctx) for kd, a in zip(kinds, args)],
            out_specs=[pl.BlockSpec((None, ts, d), lambda g_, i: (g_, i, 0)) for d in out_dims],
            name=f"{name}_f_{g}x{s}", compiler_params=_params(),
        )(*args)

    def bwd_call(args, cts):
        g, s, ts, nctx = layout(args)
        fn = with_segments(ts)

        def body(*refs):
            in_refs, ct_refs, out_refs = refs[:n_in], refs[n_in:n_in + n_out], refs[n_in + n_out:]
            gi, i = pl.program_id(0), pl.program_id(1)
            vals = [r[...] for r in in_refs]

            def f(*dv):
                full = list(vals)
                for idx, v in zip(diff, dv):
                    full[idx] = v
                return tuple(fn(*full))

            _, vjp = jax.vjp(f, *[vals[idx] for idx in diff])
            grads = vjp(tuple(r[...] for r in ct_refs))
            for idx, o_ref, gr in zip(diff, out_refs, grads):
                if kinds[idx] == "row":
                    o_ref[...] = gr
                    continue
                if kinds[idx] == "glob":
                    first = jnp.logical_and(gi == 0, i == 0)
                else:
                    first = i == 0

                @pl.when(first)
                def _(o_ref=o_ref, gr=gr):
                    o_ref[...] = gr

                @pl.when(jnp.logical_not(first))
                def _(o_ref=o_ref, gr=gr):
                    o_ref[...] += gr

        in_specs = [spec_of(kd, a, ts, nctx) for kd, a in zip(kinds, args)]
        in_specs += [pl.BlockSpec((None, ts, d), lambda g_, i: (g_, i, 0)) for d in out_dims]
        return pl.pallas_call(
            body, out_shape=[jax.ShapeDtypeStruct(args[idx].shape, F32) for idx in diff], grid=(g, s // ts),
            in_specs=in_specs, out_specs=[spec_of(kinds[idx], args[idx], ts, nctx) for idx in diff],
            name=f"{name}_b_{g}x{s}", compiler_params=_params(),
        )(*args, *cts)

    @jax.custom_vjp
    def op(*args):
        return tuple(fwd_call(*args))

    def op_fwd(*args):
        return tuple(fwd_call(*args)), args

    def op_bwd(args, cts):
        grads = bwd_call(args, cts)
        full = [None] * n_in
        for idx, gr in zip(diff, grads):
            full[idx] = gr
        return tuple(jnp.zeros_like(a) if gfull is None else gfull for a, gfull in zip(args, full))

    op.defvjp(op_fwd, op_bwd)
    op.fwd_call, op.bwd_call = fwd_call, bwd_call
    return op


def make_modulate(d, n_ctx):
    one = make_rowwise(_fn_modulate, "modulate", ("row", "glob", "seg", "seg"), (d,), nctx_rows=n_ctx)
    two = make_rowwise(_fn_modulate_keep, "modulate_keep", ("row", "glob", "seg", "seg"), (d, d), nctx_rows=n_ctx)

    @jax.custom_vjp
    def op(x, g, shift, scale):
        return one.fwd_call(x, g, shift, scale)[0], x

    def fwd(x, g, shift, scale):
        return (one.fwd_call(x, g, shift, scale)[0], x), (x, g, shift, scale)

    def bwd(res, cts):
        return tuple(two.bwd_call(res, cts))

    op.defvjp(fwd, bwd)
    return op


def make_gated_add(d, n_ctx):
    add = make_rowwise(_fn_gated_add, "gated", ("row", "row", "seg"), (d,), nctx_rows=n_ctx)
    mul = make_rowwise(_fn_gate_mul, "gate_mul", ("row", "seg"), (d,), nctx_rows=n_ctx)

    @jax.custom_vjp
    def op(x, o, gate):
        return add.fwd_call(x, o, gate)[0]

    def fwd(x, o, gate):
        return add.fwd_call(x, o, gate)[0], (o, gate)

    def bwd(res, ct):
        do, dgate = mul.bwd_call(res, (ct,))
        return ct, do, dgate

    op.defvjp(fwd, bwd)
    return op


def _rms(x):
    return lax.rsqrt(jnp.mean(x * x, axis=-1, keepdims=True) + EPS)


def _fn_modulate(x, g, shift, scale):
    return ((x * _rms(x) * g) * (1.0 + scale) + shift,)


def _fn_modulate_keep(x, g, shift, scale):
    return _fn_modulate(x, g, shift, scale) + (x,)


def _fn_gated_add(x, o, gate):
    return (x + gate * o,)


def _fn_gate_mul(o, gate):
    return (gate * o,)


def _fn_norm(x, g):
    return (x * _rms(x) * g,)


def _fn_glu_pre(u, y0, y1, d):
    return (jax.nn.gelu(d * u + y0 + y1),)


def _fn_glu_post(z, t, bg):
    return (z * jax.nn.sigmoid(t + bg),)


def _rope_tables(n_ctx, n_lat, dh, start, rot_dim):
    t = jnp.arange(n_lat)
    rows = (t // GRID_W).astype(F32)
    cols = (t % GRID_W).astype(F32)
    axis_dim = rot_dim // 2
    freqs = ROPE_BASE ** (-jnp.arange(0, axis_dim, 2, dtype=F32) / axis_dim)
    ang_r = rows[:, None] * freqs
    ang_c = cols[:, None] * freqs
    ang = jnp.concatenate([ang_r, ang_r, ang_c, ang_c], axis=-1)
    cos = jnp.concatenate([jnp.ones((n_lat, start), F32), jnp.cos(ang)], axis=-1)
    sin = jnp.concatenate([jnp.zeros((n_lat, start), F32), jnp.sin(ang)], axis=-1)
    cos = jnp.concatenate([jnp.ones((n_ctx, dh), F32), cos], axis=0)
    sin = jnp.concatenate([jnp.zeros((n_ctx, dh), F32), sin], axis=0)
    return cos, sin


_NT = (((1,), (1,)), ((), ()))
_TN = (((0,), (0,)), ((), ()))


def _na_geometry(i, nc, rows):
    r = i - nc
    rs = jnp.clip(r - NA_WIN_R // 2, 0, rows - NA_WIN_R)
    is_ctx = i < nc
    cls = jnp.where(is_ctx, NA_WIN_R, r - rs)
    return jnp.where(is_ctx, 0, rs), cls


def _na_onehots():
    q = np.arange(GRID_W)[:, None]
    col = np.arange(GRID_W)[None, :]
    cs = np.clip(q - NA_WIN_C // 2, 0, GRID_W - NA_WIN_C)
    valid = (col >= cs) & (col < cs + NA_WIN_C)
    cidx = col - q + (NA_WIN_C - 1)
    n_b = 2 * NA_WIN_C - 1
    col_hot = np.zeros((LANE, GRID_W * GRID_W), np.float32)
    for qq in range(GRID_W):
        for cc in range(GRID_W):
            if valid[qq, cc]:
                col_hot[cidx[qq, cc], qq * GRID_W + cc] = 1.0
    row_hot = np.zeros((NA_WIN_R, NA_WIN_R, 2 * NA_WIN_R - 1), np.float32)
    for c in range(NA_WIN_R):
        for j in range(NA_WIN_R):
            row_hot[c, j, j - c + NA_WIN_R - 1] = 1.0
    mask = np.where(valid, 0.0, NEG).astype(np.float32)
    return col_hot, row_hot, mask, n_b


def na_bias_table(rpb):
    h = rpb.shape[0]
    col_hot, row_hot, mask, n_b = _na_onehots()
    t1 = jnp.einsum("cja,hab->hcjb", jnp.asarray(row_hot), rpb)
    t1 = jnp.pad(t1.reshape(h * NA_WIN_R * NA_WIN_R, n_b), ((0, 0), (0, LANE - n_b)))
    t2 = linear(t1, jnp.asarray(col_hot), True)
    t2 = t2.reshape(h, NA_WIN_R, NA_WIN_R, GRID_W, GRID_W) + jnp.asarray(mask)
    tab = jnp.transpose(t2, (0, 1, 3, 2, 4)).reshape(h, NA_WIN_R, GRID_W, NA_BAND)
    return jnp.concatenate([tab, jnp.full((h, 1, GRID_W, NA_BAND), NEG, F32)], axis=1)


def _first_step():
    return jnp.logical_and(pl.program_id(0) == 0, pl.program_id(1) == 0)


def _accum_out(ref, val, first):
    @pl.when(first)
    def _():
        ref[...] = val

    @pl.when(jnp.logical_not(first))
    def _():
        ref[...] += val


def _norm_head(xh, g):
    r = _rms(xh)
    yn = xh * r
    return yn * g, yn, r


def _norm_head_bwd(dy, yn, r, g):
    dg = jnp.sum(dy * yn, axis=0, keepdims=True)
    dyn = dy * g
    return r * (dyn - yn * jnp.mean(dyn * yn, axis=-1, keepdims=True)), dg


def _rope_signs(dh, start, rot_dim, n_heads):
    q = rot_dim // 4
    pos = np.arange(dh)
    quarter = (pos - start) // q
    inr = pos >= start
    sg = np.zeros((8, n_heads * dh), np.float32)
    sg[0] = np.tile(np.where(inr & (quarter % 2 == 0), -1.0, 0.0), n_heads)
    sg[1] = np.tile(np.where(inr & (quarter % 2 == 1), 1.0, 0.0), n_heads)
    return sg


def _rope_full(y, cos, sin, sg, q):
    w = y.shape[-1]
    rot = sg[0:1] * pltpu.roll(y, w - q, 1) + sg[1:2] * pltpu.roll(y, q, 1)
    return y * cos + rot * sin


def _rope_full_t(dy, cos, sin, sg, q):
    w = dy.shape[-1]
    z = dy * sin
    return dy * cos - sg[1:2] * pltpu.roll(z, q, 1) - sg[0:1] * pltpu.roll(z, w - q, 1)


def _hnr_call(x, g, cos, sin, sg, n_heads, q, dy=None):
    b, s, w = x.shape
    dh = w // n_heads
    ts = _pick(s, ROW_TILES)
    rope = cos is not None

    def body(*refs):
        refs = list(refs)
        x_ref, g_ref = refs[0], refs[1]
        k = 2
        if rope:
            cos_ref, sin_ref, sg_ref = refs[2], refs[3], refs[4]
            k = 5
        gv = g_ref[...]
        if dy is None:
            o_ref = refs[k]
            for h in range(n_heads):
                sl = slice(h * dh, (h + 1) * dh)
                o_ref[:, sl] = _norm_head(x_ref[:, sl], gv)[0]
            if rope:
                o_ref[...] = _rope_full(o_ref[...], cos_ref[...], sin_ref[...], sg_ref[...], q)
            return
        dy_ref, dx_ref, dg_ref = refs[k], refs[k + 1], refs[k + 2]
        src = dy_ref
        if rope:
            dx_ref[...] = _rope_full_t(dy_ref[...], cos_ref[...], sin_ref[...], sg_ref[...], q)
            src = dx_ref
        dg = jnp.zeros((1, dh), F32)
        for h in range(n_heads):
            sl = slice(h * dh, (h + 1) * dh)
            _, yn, r = _norm_head(x_ref[:, sl], gv)
            dxh, dgh = _norm_head_bwd(src[:, sl], yn, r, gv)
            dx_ref[:, sl] = dxh
            dg = dg + dgh
        _accum_out(dg_ref, dg, _first_step())

    row = pl.BlockSpec((None, ts, w), lambda bi, i: (bi, i, 0))
    whole = lambda a: pl.BlockSpec(a.shape, lambda bi, i: (0, 0))
    ins, specs = [x, g], [row, whole(g)]
    if rope:
        ins += [cos, sin, sg]
        specs += [pl.BlockSpec((ts, w), lambda bi, i: (i, 0)), pl.BlockSpec((ts, w), lambda bi, i: (i, 0)), whole(sg)]
    if dy is None:
        out_shape, out_specs = jax.ShapeDtypeStruct(x.shape, F32), row
    else:
        ins.append(dy)
        specs.append(row)
        out_shape = [jax.ShapeDtypeStruct(x.shape, F32), jax.ShapeDtypeStruct(g.shape, F32)]
        out_specs = [row, whole(g)]
    return pl.pallas_call(
        body, out_shape=out_shape, grid=(b, s // ts), in_specs=specs, out_specs=out_specs,
        name=f"hnr_{'b' if dy is not None else 'f'}_{n_heads}x{dh}_{int(rope)}", compiler_params=_params(),
    )(*ins)


@functools.partial(jax.custom_vjp, nondiff_argnums=(5, 6))
def head_norm_rope(x, g, cos, sin, sg, n_heads, q):
    return _hnr_call(x, g, cos, sin, sg, n_heads, q)


def _head_norm_rope_fwd(x, g, cos, sin, sg, n_heads, q):
    return _hnr_call(x, g, cos, sin, sg, n_heads, q), (x, g, cos, sin, sg)


def _head_norm_rope_bwd(n_heads, q, res, dy):
    x, g, cos, sin, sg = res
    dx, dg = _hnr_call(x, g, cos, sin, sg, n_heads, q, dy=dy)
    zero = lambda t: None if t is None else jnp.zeros_like(t)
    return dx, dg, zero(cos), zero(sin), zero(sg)


head_norm_rope.defvjp(_head_norm_rope_fwd, _head_norm_rope_bwd)


def _mla_k_call(kv, kr, g, cos, sin, sg, dkn=None):
    b, s, _ = kv.shape
    ts = _pick(s, ROW_TILES)
    hw = MLA_NOPE + MLA_V
    kn_w = MLA_HEADS * MLA_QK
    q = MLA_ROPE // 4

    def body(kv_ref, kr_ref, g_ref, cos_ref, sin_ref, sg_ref, *rest):
        gv = g_ref[...]
        krv = kr_ref[...]
        if dkn is None:
            (o_ref,) = rest
            for h in range(MLA_HEADS):
                kh = jnp.concatenate([kv_ref[:, h * hw:h * hw + MLA_NOPE], krv], axis=-1)
                o_ref[:, h * MLA_QK:(h + 1) * MLA_QK] = _norm_head(kh, gv)[0]
            o_ref[...] = _rope_full(o_ref[...], cos_ref[...], sin_ref[...], sg_ref[...], q)
            return
        dkn_ref, dkv_ref, dkr_ref, dg_ref, dy_ref = rest
        dy_ref[...] = _rope_full_t(dkn_ref[...], cos_ref[...], sin_ref[...], sg_ref[...], q)
        dg = jnp.zeros((1, MLA_QK), F32)
        dkr = jnp.zeros((ts, MLA_ROPE), F32)
        for h in range(MLA_HEADS):
            kh = jnp.concatenate([kv_ref[:, h * hw:h * hw + MLA_NOPE], krv], axis=-1)
            _, yn, r = _norm_head(kh, gv)
            dxh, dgh = _norm_head_bwd(dy_ref[:, h * MLA_QK:(h + 1) * MLA_QK], yn, r, gv)
            dkv_ref[:, h * hw:h * hw + MLA_NOPE] = dxh[:, :MLA_NOPE]
            dkv_ref[:, h * hw + MLA_NOPE:(h + 1) * hw] = jnp.zeros((ts, MLA_V), F32)
            dkr = dkr + dxh[:, MLA_NOPE:]
            dg = dg + dgh
        dkr_ref[...] = dkr
        _accum_out(dg_ref, dg, _first_step())

    row = lambda w: pl.BlockSpec((None, ts, w), lambda bi, i: (bi, i, 0))
    tab = pl.BlockSpec((ts, kn_w), lambda bi, i: (i, 0))
    whole = lambda a: pl.BlockSpec(a.shape, lambda bi, i: (0, 0))
    ins = [kv, kr, g, cos, sin, sg]
    specs = [row(kv.shape[2]), row(MLA_ROPE), whole(g), tab, tab, whole(sg)]
    scratch = []
    if dkn is None:
        out_shape, out_specs = jax.ShapeDtypeStruct((b, s, kn_w), F32), row(kn_w)
    else:
        ins.append(dkn)
        specs.append(row(kn_w))
        out_shape = [jax.ShapeDtypeStruct(kv.shape, F32), jax.ShapeDtypeStruct(kr.shape, F32),
                     jax.ShapeDtypeStruct(g.shape, F32)]
        out_specs = [row(kv.shape[2]), row(MLA_ROPE), whole(g)]
        scratch = [pltpu.VMEM((ts, kn_w), F32)]
    return pl.pallas_call(
        body, out_shape=out_shape, grid=(b, s // ts), in_specs=specs, out_specs=out_specs, scratch_shapes=scratch,
        name=f"mla_k_{'b' if dkn is not None else 'f'}", compiler_params=_params(),
    )(*ins)


@jax.custom_vjp
def mla_k_prep(kv, kr, g, cos, sin, sg):
    return _mla_k_call(kv, kr, g, cos, sin, sg)


def _mla_k_prep_fwd(kv, kr, g, cos, sin, sg):
    return _mla_k_call(kv, kr, g, cos, sin, sg), (kv, kr, g, cos, sin, sg)


def _mla_k_prep_bwd(res, dkn):
    kv, kr, g, cos, sin, sg = res
    dkv, dkr, dg = _mla_k_call(kv, kr, g, cos, sin, sg, dkn=dkn)
    return dkv, dkr, dg, jnp.zeros_like(cos), jnp.zeros_like(sin), jnp.zeros_like(sg)


mla_k_prep.defvjp(_mla_k_prep_fwd, _mla_k_prep_bwd)


class _HeadLayout:
    def __init__(self, groups, dq, dv, q_off, k_off, v_off, o_off, wq, wk, wv, wo, scale):
        self.groups, self.dq, self.dv, self.scale = groups, dq, dv, scale
        self.q_off, self.k_off, self.v_off, self.o_off = q_off, k_off, v_off, o_off
        self.wq, self.wk, self.wv, self.wo = wq, wk, wv, wo
        self.n_h = len(q_off)


def _gqa_layout():
    rep = GQA_Q_HEADS // GQA_KV_HEADS
    n_h = GQA_Q_HEADS // 2
    return _HeadLayout(2, HEAD_DIM, HEAD_DIM, [h * HEAD_DIM for h in range(n_h)], [(h // rep) * HEAD_DIM for h in range(n_h)],
                       [(h // rep) * HEAD_DIM for h in range(n_h)], [h * HEAD_DIM for h in range(n_h)],
                       n_h * HEAD_DIM, (n_h // rep) * HEAD_DIM, (n_h // rep) * HEAD_DIM, n_h * HEAD_DIM, HEAD_DIM ** -0.5)


def _mla_layout():
    n_h = MLA_HEADS // 2
    hw = MLA_NOPE + MLA_V
    return _HeadLayout(2, MLA_QK, MLA_V, [h * MLA_QK for h in range(n_h)], [h * MLA_QK for h in range(n_h)],
                       [h * hw + MLA_NOPE for h in range(n_h)], [h * MLA_V for h in range(n_h)],
                       n_h * MLA_QK, n_h * MLA_QK, n_h * hw, n_h * MLA_V, MLA_QK ** -0.5)


def _attn_tm_fwd(q, k, v, lay, n_ctx):
    b, s, _ = q.shape
    tq = min(256, n_ctx)
    nc = n_ctx // tq

    def body(q_ref, k_ref, v_ref, o_ref, lse_ref):
        def run(n_keys):
            for h in range(lay.n_h):
                qo, ko, vo, oo = lay.q_off[h], lay.k_off[h], lay.v_off[h], lay.o_off[h]
                qv = (q_ref[:, qo:qo + lay.dq] * lay.scale).astype(BF16)
                sc = lax.dot_general(qv, k_ref[0:n_keys, ko:ko + lay.dq].astype(BF16), _NT, preferred_element_type=F32)
                m = jnp.max(sc, axis=-1, keepdims=True)
                p = jnp.exp(sc - m)
                l = jnp.sum(p, axis=-1, keepdims=True)
                o = jnp.dot(p.astype(BF16), v_ref[0:n_keys, vo:vo + lay.dv].astype(BF16), preferred_element_type=F32)
                o_ref[:, oo:oo + lay.dv] = o / l
                lse_ref[:, h:h + 1] = m + jnp.log(l)

        pl.when(pl.program_id(2) < nc)(lambda: run(n_ctx))
        pl.when(pl.program_id(2) >= nc)(lambda: run(s))

    return pl.pallas_call(
        body, out_shape=[jax.ShapeDtypeStruct((b, s, lay.groups * lay.wo), F32),
                         jax.ShapeDtypeStruct((b, lay.groups, s, lay.n_h), F32)],
        grid=(b, lay.groups, s // tq),
        in_specs=[pl.BlockSpec((None, tq, lay.wq), lambda bi, g, i: (bi, i, g)),
                  pl.BlockSpec((None, s, lay.wk), lambda bi, g, i: (bi, 0, g)),
                  pl.BlockSpec((None, s, lay.wv), lambda bi, g, i: (bi, 0, g))],
        out_specs=[pl.BlockSpec((None, tq, lay.wo), lambda bi, g, i: (bi, i, g)),
                   pl.BlockSpec((None, None, tq, lay.n_h), lambda bi, g, i: (bi, g, i, 0))],
        name=f"attn_tm_f_{lay.dq}", compiler_params=_params(),
    )(q, k, v)


def _attn_tm_bwd(q, k, v, lse, o, do, lay, n_ctx):
    b, s, _ = q.shape
    tk = min(256, n_ctx)
    nc = n_ctx // tk

    def body(q_ref, k_ref, v_ref, lse_ref, o_ref, do_ref, dq_ref, dk_ref, dv_ref, delta_ref):
        @pl.when(pl.program_id(2) == 0)
        def _():
            dq_ref[...] = jnp.zeros_like(dq_ref)
            for h in range(lay.n_h):
                oo = lay.o_off[h]
                delta_ref[:, h:h + 1] = jnp.sum(o_ref[:, oo:oo + lay.dv] * do_ref[:, oo:oo + lay.dv], axis=-1,
                                                keepdims=True)

        def run(r0):
            dk_acc, dv_acc = {}, {}
            for h in range(lay.n_h):
                qo, ko, vo, oo = lay.q_off[h], lay.k_off[h], lay.v_off[h], lay.o_off[h]
                kh = k_ref[:, ko:ko + lay.dq].astype(BF16)
                vh = v_ref[:, vo:vo + lay.dv].astype(BF16)
                qv = (q_ref[r0:s, qo:qo + lay.dq] * lay.scale).astype(BF16)
                dob = do_ref[r0:s, oo:oo + lay.dv].astype(BF16)
                sc = lax.dot_general(qv, kh, _NT, preferred_element_type=F32)
                p = jnp.exp(sc - lse_ref[r0:s, h:h + 1])
                dvh = lax.dot_general(p.astype(BF16), dob, _TN, preferred_element_type=F32)
                dp = lax.dot_general(dob, vh, _NT, preferred_element_type=F32)
                dsb = (p * (dp - delta_ref[r0:s, h:h + 1])).astype(BF16)
                dkh = lax.dot_general(dsb, qv, _TN, preferred_element_type=F32)
                dq_ref[r0:s, qo:qo + lay.dq] += jnp.dot(dsb, kh, preferred_element_type=F32) * lay.scale
                dk_acc[ko] = dkh if ko not in dk_acc else dk_acc[ko] + dkh
                dv_acc[vo] = dvh if vo not in dv_acc else dv_acc[vo] + dvh
            if len(dv_acc) * lay.dv != lay.wv:
                dv_ref[...] = jnp.zeros_like(dv_ref)
            for ko, val in dk_acc.items():
                dk_ref[:, ko:ko + lay.dq] = val
            for vo, val in dv_acc.items():
                dv_ref[:, vo:vo + lay.dv] = val

        pl.when(pl.program_id(2) < nc)(lambda: run(0))
        pl.when(pl.program_id(2) >= nc)(lambda: run(n_ctx))

    full = lambda w: pl.BlockSpec((None, s, w), lambda bi, g, j: (bi, 0, g))
    blk = lambda w: pl.BlockSpec((None, tk, w), lambda bi, g, j: (bi, j, g))
    stat = pl.BlockSpec((None, None, s, lay.n_h), lambda bi, g, j: (bi, g, 0, 0))
    return pl.pallas_call(
        body, out_shape=[jax.ShapeDtypeStruct(q.shape, F32), jax.ShapeDtypeStruct(k.shape, F32),
                         jax.ShapeDtypeStruct(v.shape, F32)],
        grid=(b, lay.groups, s // tk),
        in_specs=[full(lay.wq), blk(lay.wk), blk(lay.wv), stat, full(lay.wo), full(lay.wo)],
        out_specs=[full(lay.wq), blk(lay.wk), blk(lay.wv)],
        scratch_shapes=[pltpu.VMEM((s, lay.n_h), F32)],
        name=f"attn_tm_b_{lay.dq}", compiler_params=_params(),
    )(q, k, v, lse, o, do)


def _make_attention_tm(lay):
    @functools.partial(jax.custom_vjp, nondiff_argnums=(3,))
    def op(q, k, v, n_ctx):
        return _attn_tm_fwd(q, k, v, lay, n_ctx)[0]

    def fwd(q, k, v, n_ctx):
        o, lse = _attn_tm_fwd(q, k, v, lay, n_ctx)
        return o, (q, k, v, o, lse)

    def bwd(n_ctx, res, do):
        q, k, v, o, lse = res
        return _attn_tm_bwd(q, k, v, lse, o, do, lay, n_ctx)

    op.defvjp(fwd, bwd)
    return op


gqa_attention = _make_attention_tm(_gqa_layout())
mla_attention = _make_attention_tm(_mla_layout())

NA_GROUPS_FWD = 1
NA_GROUPS_BWD = 2


def _na_tm_specs(s, nc, rows, groups):
    hg = NA_HEADS // groups
    w = hg * HEAD_DIM
    qs = pl.BlockSpec((None, GRID_W, w), lambda bi, g, i: (bi, i, g))
    ks = pl.BlockSpec((None, s, w), lambda bi, g, i: (bi, 0, g))
    bs = pl.BlockSpec((hg, None, GRID_W, NA_BAND), lambda bi, g, i: (g, _na_geometry(i, nc, rows)[1], 0, 0))
    ls = pl.BlockSpec((None, None, GRID_W, hg), lambda bi, g, i: (bi, g, i, 0))
    return hg, w, qs, ks, bs, ls


def _na_tm_scores(q_ref, k_ref, bias_ref, hd, n_ctx, start, scale):
    sl = slice(hd * HEAD_DIM, (hd + 1) * HEAD_DIM)
    qv = (q_ref[:, sl] * scale).astype(BF16)
    kc = k_ref[0:n_ctx, sl].astype(BF16)
    kb = k_ref[pl.ds(start, NA_BAND), sl].astype(BF16)
    s_c = lax.dot_general(qv, kc, _NT, preferred_element_type=F32)
    s_l = lax.dot_general(qv, kb, _NT, preferred_element_type=F32) + bias_ref[hd]
    return sl, qv, kc, kb, s_c, s_l


def _na_tm_fwd(q, k, v, bias, n_ctx):
    b, s, _ = q.shape
    nc = n_ctx // GRID_W
    rows = (s - n_ctx) // GRID_W
    scale = HEAD_DIM ** -0.5
    hg, w, qs, ks, bs, ls = _na_tm_specs(s, nc, rows, NA_GROUPS_FWD)

    def body(q_ref, k_ref, v_ref, bias_ref, o_ref, lse_ref):
        rs, _ = _na_geometry(pl.program_id(2), nc, rows)
        start = pl.multiple_of(n_ctx + rs * GRID_W, GRID_W)
        for hd in range(hg):
            sl, _, _, _, s_c, s_l = _na_tm_scores(q_ref, k_ref, bias_ref, hd, n_ctx, start, scale)
            m = jnp.maximum(jnp.max(s_c, axis=-1, keepdims=True), jnp.max(s_l, axis=-1, keepdims=True))
            p_c = jnp.exp(s_c - m)
            p_l = jnp.exp(s_l - m)
            l = jnp.sum(p_c, axis=-1, keepdims=True) + jnp.sum(p_l, axis=-1, keepdims=True)
            o = jnp.dot(p_c.astype(BF16), v_ref[0:n_ctx, sl].astype(BF16), preferred_element_type=F32)
            o = o + jnp.dot(p_l.astype(BF16), v_ref[pl.ds(start, NA_BAND), sl].astype(BF16), preferred_element_type=F32)
            o_ref[:, sl] = o / l
            lse_ref[:, hd:hd + 1] = m + jnp.log(l)

    return pl.pallas_call(
        body, out_shape=[jax.ShapeDtypeStruct(q.shape, F32), jax.ShapeDtypeStruct((b, NA_GROUPS_FWD, s, hg), F32)],
        grid=(b, NA_GROUPS_FWD, s // GRID_W), in_specs=[qs, ks, ks, bs], out_specs=[qs, ls],
        name=f"na_tm_f_{s}", compiler_params=_params(),
    )(q, k, v, bias)


def _na_tm_bwd(q, k, v, bias, o, lse, do, n_ctx):
    b, s, _ = q.shape
    nc = n_ctx // GRID_W
    rows = (s - n_ctx) // GRID_W
    scale = HEAD_DIM ** -0.5
    n_cls = NA_WIN_R + 1
    hg, w, qs, ks, bs, ls = _na_tm_specs(s, nc, rows, NA_GROUPS_BWD)
    lse = jnp.transpose(lse, (0, 2, 1, 3)).reshape(b, s, NA_GROUPS_BWD, hg)
    lse = jnp.transpose(lse, (0, 2, 1, 3))

    def body(q_ref, k_ref, v_ref, bias_ref, o_ref, lse_ref, do_ref, dq_ref, dk_ref, dv_ref, db_ref):
        i = pl.program_id(2)
        rs, cls = _na_geometry(i, nc, rows)
        _, cls_prev = _na_geometry(i - 1, nc, rows)
        start = pl.multiple_of(n_ctx + rs * GRID_W, GRID_W)
        first = jnp.logical_or(i == 0, cls != cls_prev)

        @pl.when(i == 0)
        def _():
            dk_ref[...] = jnp.zeros_like(dk_ref)
            dv_ref[...] = jnp.zeros_like(dv_ref)

        @pl.when(first)
        def _():
            db_ref[...] = jnp.zeros_like(db_ref)

        for hd in range(hg):
            sl, qv, kc, kb, s_c, s_l = _na_tm_scores(q_ref, k_ref, bias_ref, hd, n_ctx, start, scale)
            lse_v = lse_ref[:, hd:hd + 1]
            p_c = jnp.exp(s_c - lse_v)
            p_l = jnp.exp(s_l - lse_v)
            dov = do_ref[:, sl]
            dob = dov.astype(BF16)
            delta = jnp.sum(dov * o_ref[:, sl], axis=-1, keepdims=True)
            vc = v_ref[0:n_ctx, sl].astype(BF16)
            vb = v_ref[pl.ds(start, NA_BAND), sl].astype(BF16)
            ds_c = p_c * (lax.dot_general(dob, vc, _NT, preferred_element_type=F32) - delta)
            ds_l = p_l * (lax.dot_general(dob, vb, _NT, preferred_element_type=F32) - delta)
            dsc_b = ds_c.astype(BF16)
            dsl_b = ds_l.astype(BF16)
            dq_ref[:, sl] = (jnp.dot(dsc_b, kc, preferred_element_type=F32)
                             + jnp.dot(dsl_b, kb, preferred_element_type=F32)) * scale
            dk_ref[0:n_ctx, sl] += lax.dot_general(dsc_b, qv, _TN, preferred_element_type=F32)
            dk_ref[pl.ds(start, NA_BAND), sl] += lax.dot_general(dsl_b, qv, _TN, preferred_element_type=F32)
            dv_ref[0:n_ctx, sl] += lax.dot_general(p_c.astype(BF16), dob, _TN, preferred_element_type=F32)
            dv_ref[pl.ds(start, NA_BAND), sl] += lax.dot_general(p_l.astype(BF16), dob, _TN, preferred_element_type=F32)
            db_ref[hd] += ds_l

    dbs = pl.BlockSpec((None, hg, None, GRID_W, NA_BAND), lambda bi, g, i: (bi, g, _na_geometry(i, nc, rows)[1], 0, 0))
    return pl.pallas_call(
        body,
        out_shape=[jax.ShapeDtypeStruct(q.shape, F32), jax.ShapeDtypeStruct(q.shape, F32), jax.ShapeDtypeStruct(q.shape, F32),
                   jax.ShapeDtypeStruct((b, NA_HEADS, n_cls, GRID_W, NA_BAND), F32)],
        grid=(b, NA_GROUPS_BWD, s // GRID_W), in_specs=[qs, ks, ks, bs, qs, ls, qs], out_specs=[qs, ks, ks, dbs],
        name=f"na_tm_b_{s}", compiler_params=_params(),
    )(q, k, v, bias, o, lse, do)


@functools.partial(jax.custom_vjp, nondiff_argnums=(4,))
def na_attention_tm(q, k, v, bias, n_ctx):
    return _na_tm_fwd(q, k, v, bias, n_ctx)[0]


def _na_attention_tm_fwd(q, k, v, bias, n_ctx):
    o, lse = _na_tm_fwd(q, k, v, bias, n_ctx)
    return o, (q, k, v, bias, o, lse)


def _na_attention_tm_bwd(n_ctx, res, do):
    q, k, v, bias, o, lse = res
    dq, dk, dv, db = _na_tm_bwd(q, k, v, bias, o, lse, do, n_ctx)
    return dq, dk, dv, _sum_rows(db.reshape(db.shape[0], -1, NA_BAND), db.shape[0]).reshape(db.shape[1:])


na_attention_tm.defvjp(_na_attention_tm_fwd, _na_attention_tm_bwd)


def _cmul(ar, ai, br, bi):
    return ar * br - ai * bi, ar * bi + ai * br


def _s5_chunk(n_ctx):
    return min(256, n_ctx)


def _s5_powers(a_re, a_im, t_len):
    exps = np.concatenate([np.minimum(2 ** np.arange(8), t_len), np.arange(1, 9), [t_len] + [0] * 7,
                           np.arange(0, t_len, 8)]).astype(np.float32)
    a_re, a_im = lax.stop_gradient(a_re), lax.stop_gradient(a_im)
    mag = jnp.sqrt(a_re * a_re + a_im * a_im)
    th = jnp.arctan2(a_im, a_re)
    t = jnp.asarray(exps)[:, None]
    pm = jnp.where(t == 0, 1.0, jnp.exp(t * jnp.log(jnp.maximum(mag, 1e-37))) * (mag > 0))
    return jnp.stack([pm * jnp.cos(t * th), pm * jnp.sin(t * th)])


def _s5_tables(pw, t_len, rev, conj=False):
    if conj:
        pw = pw * jnp.asarray([1.0, -1.0], F32)[:, None, None]
    if rev:
        pw = jnp.concatenate([pw[:, :8], pw[:, 8:16][:, ::-1], pw[:, 16:24], pw[:, 24:][:, ::-1]], axis=1)
    return pw


def _scan_chunk(x_re, x_im, tab_ref, hin_re, hin_im, rev, t_len, xs_ref, es_ref):
    outs = [_scan_slab(x_re[:, k:k + LANE], x_im[:, k:k + LANE], tab_ref, hin_re[:, k:k + LANE], hin_im[:, k:k + LANE],
                       rev, t_len, xs_ref, es_ref, k) for k in range(0, x_re.shape[-1], LANE)]
    return tuple(jnp.concatenate([o[t] for o in outs], axis=-1) for t in range(4))


def _scan_slab(x_re, x_im, tab_ref, hin_re, hin_im, rev, t_len, xs_ref, es_ref, k0):
    lanes = LANE
    n2 = t_len // 8
    tab_ref = tab_ref.at[:, :, k0:k0 + LANE]
    rin = lax.broadcasted_iota(jnp.int32, (t_len, lanes), 0) & 7
    for li, sh in enumerate((1, 2, 4)):
        m_re, m_im = tab_ref[0, li:li + 1, :], tab_ref[1, li:li + 1, :]
        amt = sh if not rev else t_len - sh
        c_re, c_im = _cmul(m_re, m_im, pltpu.roll(x_re, amt, 0), pltpu.roll(x_im, amt, 0))
        ok = (rin >= sh) if not rev else (rin < 8 - sh)
        x_re = x_re + jnp.where(ok, c_re, 0.0)
        x_im = x_im + jnp.where(ok, c_im, 0.0)
    xr_ref, xi_ref = xs_ref
    xr_ref[...] = x_re
    xi_ref[...] = x_im
    off = 0 if rev else 7
    e_re = xr_ref[pl.ds(off, n2, stride=8), :]
    e_im = xi_ref[pl.ds(off, n2, stride=8), :]
    row2 = lax.broadcasted_iota(jnp.int32, (n2, lanes), 0)
    sh, li = 1, 3
    while sh < n2:
        m_re, m_im = tab_ref[0, li:li + 1, :], tab_ref[1, li:li + 1, :]
        amt = sh if not rev else n2 - sh
        c_re, c_im = _cmul(m_re, m_im, pltpu.roll(e_re, amt, 0), pltpu.roll(e_im, amt, 0))
        ok = (row2 >= sh) if not rev else (row2 < n2 - sh)
        e_re = e_re + jnp.where(ok, c_re, 0.0)
        e_im = e_im + jnp.where(ok, c_im, 0.0)
        sh, li = sh * 2, li + 1
    es_ref[0] = e_re
    es_ref[1] = e_im
    last = 0 if rev else n2 - 1
    t_re, t_im = _cmul(tab_ref[0, 16:17, :], tab_ref[1, 16:17, :], hin_re, hin_im)
    hout_re = es_ref[0, last:last + 1, :] + t_re
    hout_im = es_ref[1, last:last + 1, :] + t_im
    amt = 1 if not rev else n2 - 1
    ok = (row2 >= 1) if not rev else (row2 < n2 - 1)
    k_re, k_im = _cmul(tab_ref[0, 24:24 + n2, :], tab_ref[1, 24:24 + n2, :], hin_re, hin_im)
    c_re = jnp.where(ok, pltpu.roll(e_re, amt, 0), 0.0) + k_re
    c_im = jnp.where(ok, pltpu.roll(e_im, amt, 0), 0.0) + k_im
    tp_re, tp_im = tab_ref[0, 8:16, :][None], tab_ref[1, 8:16, :][None]
    add_re, add_im = _cmul(tp_re, tp_im, c_re[:, None, :], c_im[:, None, :])
    h_re = xr_ref[...] + add_re.reshape(t_len, lanes)
    h_im = xi_ref[...] + add_im.reshape(t_len, lanes)
    return h_re, h_im, hout_re, hout_im


def _s5_order(j, n_chunks, nc, rev):
    if not rev:
        return j
    return jnp.where(j < nc, nc - 1 - j, n_chunks - 1 - (j - nc))


def _s5_fwd(u, tab, b_bd, c_bd, n_ctx, rev):
    b, s, w = u.shape
    lanes = b_bd.shape[-1]
    t_len = _s5_chunk(n_ctx)
    n_chunks, nc = s // t_len, n_ctx // t_len

    def body(u_ref, tab_ref, b_ref, c_ref, y_ref, h_ref, hin_ref, carry_ref, xr_ref, xi_ref, es_ref):
        xs_ref = (xr_ref, xi_ref)

        @pl.when(pl.program_id(1) == 0)
        def _():
            carry_ref[...] = jnp.zeros_like(carry_ref)

        ub = u_ref[...].astype(BF16)
        x_re = jnp.dot(ub, b_ref[0].astype(BF16), preferred_element_type=F32)
        x_im = jnp.dot(ub, b_ref[1].astype(BF16), preferred_element_type=F32)
        hin_re, hin_im = carry_ref[0, 0:1, :], carry_ref[1, 0:1, :]
        hin_ref[...] = carry_ref[...]
        h_re, h_im, ho_re, ho_im = _scan_chunk(x_re, x_im, tab_ref, hin_re, hin_im, rev, t_len, xs_ref, es_ref)
        carry_ref[0] = jnp.broadcast_to(ho_re, (8, lanes))
        carry_ref[1] = jnp.broadcast_to(ho_im, (8, lanes))
        h_ref[0] = h_re
        h_ref[1] = h_im
        y_ref[...] = (jnp.dot(h_re.astype(BF16), c_ref[0].astype(BF16), preferred_element_type=F32)
                      - jnp.dot(h_im.astype(BF16), c_ref[1].astype(BF16), preferred_element_type=F32))

    order = lambda j: _s5_order(j, n_chunks, nc, rev)
    whole = lambda arr: pl.BlockSpec(arr.shape, lambda bi, j: (0,) * arr.ndim)
    return pl.pallas_call(
        body,
        out_shape=[jax.ShapeDtypeStruct((b, s, w), F32), jax.ShapeDtypeStruct((2, b, s, lanes), F32),
                   jax.ShapeDtypeStruct((2, b, n_chunks, 8, lanes), F32)],
        grid=(b, n_chunks),
        in_specs=[pl.BlockSpec((None, t_len, w), lambda bi, j: (bi, order(j), 0)), whole(tab), whole(b_bd), whole(c_bd)],
        out_specs=[pl.BlockSpec((None, t_len, w), lambda bi, j: (bi, order(j), 0)),
                   pl.BlockSpec((2, None, t_len, lanes), lambda bi, j: (0, bi, order(j), 0)),
                   pl.BlockSpec((2, None, None, 8, lanes), lambda bi, j: (0, bi, order(j), 0, 0))],
        scratch_shapes=[pltpu.VMEM((2, 8, lanes), F32), pltpu.VMEM((t_len, LANE), F32), pltpu.VMEM((t_len, LANE), F32),
                        pltpu.VMEM((2, t_len // 8, LANE), F32)],
        name=f"s5_f_{s}_{int(rev)}", compiler_params=_params(),
    )(u, tab, b_bd, c_bd)


def _s5_bwd(u, tab_adj, b_bd, c_bd, h, hin, dy, n_ctx, rev):
    b, s, w = u.shape
    lanes = b_bd.shape[-1]
    t_len = _s5_chunk(n_ctx)
    n_chunks, nc = s // t_len, n_ctx // t_len
    arev = not rev

    def body(u_ref, tab_ref, b_ref, c_ref, h_ref, hin_ref, dy_ref, du_ref, db_ref, dc_ref, da_ref,
             carry_ref, xr_ref, xi_ref, es_ref):
        xs_ref = (xr_ref, xi_ref)
        first = jnp.logical_and(pl.program_id(0) == 0, pl.program_id(1) == 0)

        @pl.when(pl.program_id(1) == 0)
        def _():
            carry_ref[...] = jnp.zeros_like(carry_ref)

        dyv = dy_ref[...]
        dyb = dyv.astype(BF16)
        dn = (((1,), (1,)), ((), ()))
        dt = (((0,), (0,)), ((), ()))
        x_re = lax.dot_general(dyb, c_ref[0].astype(BF16), dn, preferred_element_type=F32)
        x_im = -lax.dot_general(dyb, c_ref[1].astype(BF16), dn, preferred_element_type=F32)
        g_re, g_im, go_re, go_im = _scan_chunk(x_re, x_im, tab_ref, carry_ref[0, 0:1, :], carry_ref[1, 0:1, :],
                                               arev, t_len, xs_ref, es_ref)
        carry_ref[0] = jnp.broadcast_to(go_re, (8, lanes))
        carry_ref[1] = jnp.broadcast_to(go_im, (8, lanes))
        h_re, h_im = h_ref[0], h_ref[1]
        gb_re, gb_im = g_re.astype(BF16), g_im.astype(BF16)
        du_ref[...] = (lax.dot_general(gb_re, b_ref[0].astype(BF16), dn, preferred_element_type=F32)
                       + lax.dot_general(gb_im, b_ref[1].astype(BF16), dn, preferred_element_type=F32))
        ub = u_ref[...].astype(BF16)
        db_re = lax.dot_general(ub, gb_re, dt, preferred_element_type=F32)
        db_im = lax.dot_general(ub, gb_im, dt, preferred_element_type=F32)
        dc_re = lax.dot_general(h_re.astype(BF16), dyb, dt, preferred_element_type=F32)
        dc_im = -lax.dot_general(h_im.astype(BF16), dyb, dt, preferred_element_type=F32)
        row = lax.broadcasted_iota(jnp.int32, (t_len, lanes), 0)
        amt = 1 if not rev else t_len - 1
        edge = (row == 0) if not rev else (row == t_len - 1)
        hp_re = jnp.where(edge, hin_ref[0, 0:1, :], pltpu.roll(h_re, amt, 0))
        hp_im = jnp.where(edge, hin_ref[1, 0:1, :], pltpu.roll(h_im, amt, 0))
        da_re = jnp.sum(g_re * hp_re + g_im * hp_im, axis=0, keepdims=True)
        da_im = jnp.sum(g_im * hp_re - g_re * hp_im, axis=0, keepdims=True)

        @pl.when(first)
        def _():
            db_ref[0], db_ref[1] = db_re, db_im
            dc_ref[0], dc_ref[1] = dc_re, dc_im
            da_ref[0] = jnp.broadcast_to(da_re, (8, lanes))
            da_ref[1] = jnp.broadcast_to(da_im, (8, lanes))

        @pl.when(jnp.logical_not(first))
        def _():
            db_ref[0] += db_re
            db_ref[1] += db_im
            dc_ref[0] += dc_re
            dc_ref[1] += dc_im
            da_ref[0] += jnp.broadcast_to(da_re, (8, lanes))
            da_ref[1] += jnp.broadcast_to(da_im, (8, lanes))

    order = lambda j: _s5_order(n_chunks - 1 - j, n_chunks, nc, rev)
    whole = lambda arr: pl.BlockSpec(arr.shape, lambda bi, j: (0,) * arr.ndim)
    us = pl.BlockSpec((None, t_len, w), lambda bi, j: (bi, order(j), 0))
    return pl.pallas_call(
        body,
        out_shape=[jax.ShapeDtypeStruct((b, s, w), F32), jax.ShapeDtypeStruct(b_bd.shape, F32),
                   jax.ShapeDtypeStruct(c_bd.shape, F32), jax.ShapeDtypeStruct((2, 8, lanes), F32)],
        grid=(b, n_chunks),
        in_specs=[us, whole(tab_adj), whole(b_bd), whole(c_bd),
                  pl.BlockSpec((2, None, t_len, lanes), lambda bi, j: (0, bi, order(j), 0)),
                  pl.BlockSpec((2, None, None, 8, lanes), lambda bi, j: (0, bi, order(j), 0, 0)), us],
        out_specs=[us, whole(b_bd), whole(c_bd), pl.BlockSpec((2, 8, lanes), lambda bi, j: (0, 0, 0))],
        scratch_shapes=[pltpu.VMEM((2, 8, lanes), F32), pltpu.VMEM((t_len, LANE), F32), pltpu.VMEM((t_len, LANE), F32),
                        pltpu.VMEM((2, t_len // 8, LANE), F32)],
        name=f"s5_b_{s}_{int(rev)}", compiler_params=_params(),
    )(u, tab_adj, b_bd, c_bd, h, hin, dy)


@functools.partial(jax.custom_vjp, nondiff_argnums=(4, 5))
def s5_direction(u, a, b_bd, c_bd, n_ctx, rev):
    t_len = _s5_chunk(n_ctx)
    return _s5_fwd(u, _s5_tables(_s5_powers(a[0], a[1], t_len), t_len, rev), b_bd, c_bd, n_ctx, rev)[0]


def _s5_direction_fwd(u, a, b_bd, c_bd, n_ctx, rev):
    t_len = _s5_chunk(n_ctx)
    pw = _s5_powers(a[0], a[1], t_len)
    y, h, hin = _s5_fwd(u, _s5_tables(pw, t_len, rev), b_bd, c_bd, n_ctx, rev)
    return y, (u, pw, b_bd, c_bd, h, hin)


def _s5_direction_bwd(n_ctx, rev, res, dy):
    u, pw, b_bd, c_bd, h, hin = res
    tab_adj = _s5_tables(pw, _s5_chunk(n_ctx), not rev, conj=True)
    du, db, dc, da = _s5_bwd(u, tab_adj, b_bd, c_bd, h, hin, dy, n_ctx, rev)
    return du, da[:, 0, :], db, dc


s5_direction.defvjp(_s5_direction_fwd, _s5_direction_bwd)


def _s5_discretize(lam_re, lam_im, log_dt, b_re, b_im):
    dt = jnp.exp(log_dt)[:, None]
    mag = jnp.exp(lam_re * dt)
    a_re = mag * jnp.cos(lam_im * dt)
    a_im = mag * jnp.sin(lam_im * dt)
    den = jnp.square(lam_re) + jnp.square(lam_im)
    f_re = ((a_re - 1.0) * lam_re + a_im * lam_im) / den
    f_im = (a_im * lam_re - (a_re - 1.0) * lam_im) / den
    bb_re = f_re[..., None] * b_re - f_im[..., None] * b_im
    bb_im = f_re[..., None] * b_im + f_im[..., None] * b_re
    return a_re, a_im, bb_re, bb_im


def _block_diag(t):
    g, r, c = t.shape
    return (jnp.eye(g, dtype=F32)[:, None, :, None] * t[:, :, None, :]).reshape(g * r, g * c)


def _loss_head(y, target):
    b, n, d = y.shape
    ts = _pick(n, (256, 128, 64))

    def body(y_ref, t_ref, loss_ref, dy_ref):
        first = jnp.logical_and(pl.program_id(0) == 0, pl.program_id(1) == 0)
        err = y_ref[...] - t_ref[...]
        dy_ref[...] = err * (1.0 / d)
        part = 0.5 * jnp.sum(jnp.sum(err * err, axis=-1, keepdims=True) * (1.0 / d), axis=0, keepdims=True)
        part = jnp.broadcast_to(part, (8, LANE))

        @pl.when(first)
        def _():
            loss_ref[...] = part

        @pl.when(jnp.logical_not(first))
        def _():
            loss_ref[...] += part

    blk = pl.BlockSpec((None, ts, d), lambda bi, i: (bi, i, 0))
    return pl.pallas_call(
        body, out_shape=[jax.ShapeDtypeStruct((8, LANE), F32), jax.ShapeDtypeStruct((b, n, d), F32)],
        grid=(b, n // ts), in_specs=[blk, blk], out_specs=[pl.BlockSpec((8, LANE), lambda bi, i: (0, 0)), blk],
        name="loss_head", compiler_params=_params(),
    )(y, target)


def _adamw(w, g, m, v):
    shape = w.shape
    n = int(np.prod(shape))
    cols = shape[-1]
    r = n // cols
    tr = _pick(r, (512, 256, 128, 64, 32, 16, 8))
    c1 = 1.0 / (1.0 - ADAM_B1 ** ADAM_STEP)
    c2 = 1.0 / (1.0 - ADAM_B2 ** ADAM_STEP)

    def body(w_ref, g_ref, m_ref, v_ref, d_ref, mo_ref, vo_ref):
        gv = g_ref[...]
        m2 = ADAM_B1 * m_ref[...] + (1.0 - ADAM_B1) * gv
        v2 = ADAM_B2 * v_ref[...] + (1.0 - ADAM_B2) * (gv * gv)
        d_ref[...] = -ADAM_LR * ((m2 * c1) / (jnp.sqrt(v2 * c2) + ADAM_EPS) + ADAM_WD * w_ref[...])
        mo_ref[...] = m2
        vo_ref[...] = v2

    blk = pl.BlockSpec((tr, cols), lambda i: (i, 0))
    outs = pl.pallas_call(
        body, out_shape=[jax.ShapeDtypeStruct((r, cols), F32)] * 3, grid=(r // tr,),
        in_specs=[blk] * 4, out_specs=[blk] * 3, name=f"adamw_{r}x{cols}", compiler_params=_params(),
    )(*[t.reshape(r, cols) for t in (w, g, m, v)])
    return tuple(o.reshape(shape) for o in outs)


def _sum_rows(x, n):
    _, r, c = x.shape
    tr = _pick(r, (512, 256, 128, 64, 32, 16, 8))

    def body(x_ref, o_ref):
        acc = x_ref[0]
        for j in range(1, n):
            acc = acc + x_ref[j]
        o_ref[...] = acc

    return pl.pallas_call(
        body, out_shape=jax.ShapeDtypeStruct((r, c), F32), grid=(r // tr,),
        in_specs=[pl.BlockSpec((n, tr, c), lambda i: (0, i, 0))], out_specs=pl.BlockSpec((tr, c), lambda i: (i, 0)),
        name=f"sum{n}_{r}x{c}", compiler_params=_params(),
    )(x)


def _accumulate(parts, out_dtype):
    r, c = parts[0].shape[-2:]
    tr = _pick(r, (1152, 1024, 768, 576, 512, 256, 128, 64, 32, 16))

    def body(*refs):
        acc = None
        for ref in refs[:-1]:
            terms = [ref[j] for j in range(ref.shape[0])] if len(ref.shape) == 3 else [ref[...]]
            for t in terms:
                acc = t.astype(F32) if acc is None else acc + t.astype(F32)
        refs[-1][...] = acc.astype(out_dtype)

    specs = [pl.BlockSpec((p.shape[0], tr, c), lambda i: (0, i, 0)) if p.ndim == 3 else pl.BlockSpec((tr, c), lambda i: (i, 0))
             for p in parts]
    tag = "_".join(str(p.shape[0]) if p.ndim == 3 else "1" for p in parts)
    return pl.pallas_call(
        body, out_shape=jax.ShapeDtypeStruct((r, c), out_dtype), grid=(r // tr,), in_specs=specs,
        out_specs=pl.BlockSpec((tr, c), lambda i: (i, 0)), name=f"accumulate_{tag}_{r}x{c}_{jnp.dtype(out_dtype).name}",
        compiler_params=_params(),
    )(*parts)


def _add2(x, y):
    shape = x.shape
    c = shape[-1]
    r = int(np.prod(shape)) // c
    tr = _pick(r, (512, 256, 128, 64, 32, 16, 8))

    def body(x_ref, y_ref, o_ref):
        o_ref[...] = x_ref[...] + y_ref[...]

    blk = pl.BlockSpec((tr, c), lambda i: (i, 0))
    return pl.pallas_call(
        body, out_shape=jax.ShapeDtypeStruct((r, c), F32), grid=(r // tr,), in_specs=[blk, blk], out_specs=blk,
        name=f"add2_{r}x{c}", compiler_params=_params(),
    )(x.reshape(r, c), y.reshape(r, c)).reshape(shape)


_FLIPS = ((1, 0), (0, 1), (1, 1))


def _me():
    return lax.axis_index("x"), lax.axis_index("y"), lax.axis_index("c")


def allgather8(v):
    m_per, n = v.shape

    def body(x_ref, out_ref, send_sems, recv_sems, local_sem):
        x, y, c = _me()
        me, sibling = (x, y, c), (x, y, 1 - c)
        chips = [(1 - x, y), (x, 1 - y), (1 - x, 1 - y)]

        def rows(px, py, pc):
            return out_ref.at[pl.ds((4 * px + 2 * py + pc) * m_per, m_per), :]

        def copy(k, block, to, src=None):
            return pltpu.make_async_remote_copy(
                src_ref=rows(*block) if src is None else src, dst_ref=rows(*block),
                send_sem=send_sems.at[k], recv_sem=recv_sems.at[k], device_id=to, device_id_type=MESH)

        mine = pltpu.make_async_copy(x_ref, rows(*me), local_sem)
        mine.start()
        first = [copy(0, me, sibling, src=x_ref)]
        first += [copy(1 + j, me, (*chip, c), src=x_ref) for j, chip in enumerate(chips)]
        for cp in first:
            cp.start()
        passed = [copy(4 + j, (*chip, c), sibling) for j, chip in enumerate(chips)]
        for j, chip in enumerate(chips):
            copy(1 + j, (*chip, c), me).wait_recv()
            passed[j].start()
        copy(0, sibling, me).wait_recv()
        for j, chip in enumerate(chips):
            copy(4 + j, (*chip, 1 - c), me).wait_recv()
        for cp in first + passed:
            cp.wait_send()
        mine.wait()

    return pl.pallas_call(
        body, out_shape=jax.ShapeDtypeStruct((N_DEV * m_per, n), v.dtype), in_specs=[VMEM_SPEC], out_specs=VMEM_SPEC,
        scratch_shapes=[pltpu.SemaphoreType.DMA((7,)), pltpu.SemaphoreType.DMA((7,)), pltpu.SemaphoreType.DMA],
        name=f"allgather8_{m_per}x{n}", compiler_params=_params(),
    )(v)


def _row_chunks(rows, tile_rows, want):
    n = want
    while n > 1 and rows % (n * tile_rows):
        n //= 2
    return [(i * (rows // n), rows // n) for i in range(n)]


def _remote(src, dst, send_sem, recv_sem, to):
    return pltpu.make_async_remote_copy(src_ref=src, dst_ref=dst, send_sem=send_sem, recv_sem=recv_sem, device_id=to,
                                        device_id_type=MESH)


def plane_allgather(big, small):
    rows = big.shape[0]
    rh = rows // 2
    rq = rh // 2
    tile = 16 if big.dtype == BF16 else 8
    assert rq % tile == 0
    ch_full = _row_chunks(rows, tile, 8)
    ch_half = _row_chunks(rh, tile, 4)

    def body(big_ref, small_ref, obig_ref, osmall_ref, send_sems, recv_sems, relay_send, relay_recv, fwd_send, fwd_recv,
             own_send, own_recv):
        x, y, c = _me()
        me = 2 * x + y
        sibling = (x, y, 1 - c)
        nbr_x, nbr_y, diag = (1 - x, y, c), (x, 1 - y, c), (1 - x, 1 - y, c)
        xi, yi, di = 2 * (1 - x) + y, 2 * x + (1 - y), 2 * (1 - x) + (1 - y)
        base, obase = c * rh, (1 - c) * rh
        mine, other = pl.ds(base, rh), pl.ds(obase, rh)
        qa, qb = pl.ds(base, rq), pl.ds(base + rq, rq)
        for st, sz in ch_full:
            sl = pl.ds(st, sz)
            _remote(big_ref.at[sl], obig_ref.at[me, sl], own_send.at[0], own_recv.at[0], sibling).start()
        _remote(small_ref, osmall_ref.at[me], own_send.at[1], own_recv.at[1], sibling).start()
        for j, peer in enumerate((nbr_x, nbr_y)):
            for st, sz in ch_half:
                sl = pl.ds(base + st, sz)
                _remote(big_ref.at[sl], obig_ref.at[me, sl], send_sems.at[j], recv_sems.at[j], peer).start()
        for j, peer in enumerate((nbr_x, nbr_y, diag)):
            _remote(small_ref, osmall_ref.at[me], send_sems.at[3 + j], recv_sems.at[3 + j], peer).start()

        def pass_on(k, slot, sl):
            _remote(obig_ref.at[slot, sl], obig_ref.at[slot, sl], fwd_send.at[k], fwd_recv.at[k], sibling).start()

        _remote(big_ref.at[mine], obig_ref.at[xi, mine], send_sems.at[0], recv_sems.at[0], nbr_x).wait_recv()
        _remote(obig_ref.at[xi, qa], obig_ref.at[xi, qa], relay_send.at[0], relay_recv.at[0], nbr_y).start()
        pass_on(0, xi, mine)
        _remote(big_ref.at[mine], obig_ref.at[yi, mine], send_sems.at[1], recv_sems.at[1], nbr_y).wait_recv()
        _remote(obig_ref.at[yi, qb], obig_ref.at[yi, qb], relay_send.at[1], relay_recv.at[1], nbr_x).start()
        pass_on(1, yi, mine)
        _remote(obig_ref.at[di, qa], obig_ref.at[di, qa], relay_send.at[0], relay_recv.at[0], nbr_y).wait_recv()
        pass_on(2, di, qa)
        _remote(obig_ref.at[di, qb], obig_ref.at[di, qb], relay_send.at[1], relay_recv.at[1], nbr_x).wait_recv()
        pass_on(3, di, qb)
        for j, (peer, slot) in enumerate(((nbr_x, xi), (nbr_y, yi), (diag, di))):
            _remote(small_ref, osmall_ref.at[slot], send_sems.at[3 + j], recv_sems.at[3 + j], peer).wait_recv()
        oqa, oqb = pl.ds(obase, rq), pl.ds(obase + rq, rq)
        for k, (slot, sl) in enumerate(((xi, other), (yi, other), (di, oqa), (di, oqb))):
            _remote(obig_ref.at[slot, sl], obig_ref.at[slot, sl], fwd_send.at[k], fwd_recv.at[k], sibling).wait_recv()
        for k, (slot, sl) in enumerate(((xi, mine), (yi, mine), (di, qa), (di, qb))):
            _remote(obig_ref.at[slot, sl], obig_ref.at[slot, sl], fwd_send.at[k], fwd_recv.at[k], sibling).wait_send()
        for j, peer in enumerate((nbr_x, nbr_y)):
            _remote(big_ref.at[mine], obig_ref.at[me, mine], send_sems.at[j], recv_sems.at[j], peer).wait_send()
        for j, peer in enumerate((nbr_x, nbr_y, diag)):
            _remote(small_ref, osmall_ref.at[me], send_sems.at[3 + j], recv_sems.at[3 + j], peer).wait_send()
        _remote(obig_ref.at[xi, qa], obig_ref.at[xi, qa], relay_send.at[0], relay_recv.at[0], nbr_y).wait_send()
        _remote(obig_ref.at[yi, qb], obig_ref.at[yi, qb], relay_send.at[1], relay_recv.at[1], nbr_x).wait_send()
        _remote(big_ref, obig_ref.at[me], own_send.at[0], own_recv.at[0], sibling).wait()
        _remote(small_ref, osmall_ref.at[me], own_send.at[1], own_recv.at[1], sibling).wait()

    dma = pltpu.SemaphoreType.DMA
    return pl.pallas_call(
        body, out_shape=[jax.ShapeDtypeStruct((N_PLANE,) + big.shape, big.dtype),
                         jax.ShapeDtypeStruct((N_PLANE,) + small.shape, small.dtype)],
        in_specs=[ANY, ANY], out_specs=[ANY, ANY],
        scratch_shapes=[dma((6,)), dma((6,)), dma((2,)), dma((2,)), dma((4,)), dma((4,)), dma((2,)), dma((2,))],
        name="plane_allgather", compiler_params=_params(),
    )(big, small)


def neighbour_exchange(to_x, to_y):
    tile = 16 if to_x.dtype == BF16 else 8
    chunks = _row_chunks(to_x.shape[0], tile, 4)

    def body(ax_ref, ay_ref, fx_ref, fy_ref, send_sems, recv_sems):
        x, y, c = _me()
        for k, (src, dst, peer) in enumerate(((ax_ref, fx_ref, (1 - x, y, c)), (ay_ref, fy_ref, (x, 1 - y, c)))):
            for st, sz in chunks:
                sl = pl.ds(st, sz)
                _remote(src.at[sl], dst.at[sl], send_sems.at[k], recv_sems.at[k], peer).start()
        for k, (src, dst, peer) in enumerate(((ax_ref, fx_ref, (1 - x, y, c)), (ay_ref, fy_ref, (x, 1 - y, c)))):
            _remote(src, dst, send_sems.at[k], recv_sems.at[k], peer).wait()

    shape = jax.ShapeDtypeStruct(to_x.shape, to_x.dtype)
    return pl.pallas_call(
        body, out_shape=[shape, shape], in_specs=[ANY, ANY], out_specs=[ANY, ANY],
        scratch_shapes=[pltpu.SemaphoreType.DMA((2,)), pltpu.SemaphoreType.DMA((2,))],
        name=f"neighbour_exchange_{to_x.shape[0]}", compiler_params=_params(),
    )(to_x, to_y)


def sibling_halves(buf):
    n_blk, _, rows, cols = buf.shape
    tile = 16 if buf.dtype == BF16 else 8
    chunks = _row_chunks(rows, tile, 2)

    def body(buf_ref, got_ref, send_sem, recv_sem):
        x, y, c = _me()
        for j in range(n_blk):
            for st, sz in chunks:
                sl = pl.ds(st, sz)
                _remote(buf_ref.at[j, 1 - c, sl], got_ref.at[j, sl], send_sem, recv_sem, (x, y, 1 - c)).start()
        _remote(got_ref, got_ref, send_sem, recv_sem, (x, y, 1 - c)).wait()

    return pl.pallas_call(
        body, out_shape=jax.ShapeDtypeStruct((n_blk, rows, cols), buf.dtype), in_specs=[ANY], out_specs=ANY,
        scratch_shapes=[pltpu.SemaphoreType.DMA, pltpu.SemaphoreType.DMA],
        name="sibling_halves", compiler_params=_params(),
    )(buf)


def sibling_swap(s):
    tile = 16 if s.dtype == BF16 else 8
    chunks = _row_chunks(s.shape[0], tile, 8)

    def body(s_ref, got_ref, send_sem, recv_sem):
        x, y, c = _me()
        for st, sz in chunks:
            sl = pl.ds(st, sz)
            _remote(s_ref.at[sl], got_ref.at[sl], send_sem, recv_sem, (x, y, 1 - c)).start()
        _remote(s_ref, got_ref, send_sem, recv_sem, (x, y, 1 - c)).wait()

    return pl.pallas_call(
        body, out_shape=jax.ShapeDtypeStruct(s.shape, s.dtype), in_specs=[ANY], out_specs=ANY,
        scratch_shapes=[pltpu.SemaphoreType.DMA, pltpu.SemaphoreType.DMA],
        name="sibling_swap", compiler_params=_params(),
    )(s)


def _op(cache, fn, name, kinds, out_dims, **kw):
    key = (name, tuple(out_dims), tuple(sorted(kw.items())))
    if key not in cache:
        cache[key] = make_rowwise(fn, name, kinds, out_dims, **kw)
    return cache[key]


def _even_mixer(ops, a, w, n_ctx):
    b, s, d = a.shape
    proj = linear(a.reshape(b * s, d), w["e_w_in"]).reshape(b, s, -1)
    q, k, v, u = jnp.split(proj, [GQA_Q_W, GQA_Q_W + GQA_KV_W, GQA_Q_W + 2 * GQA_KV_W], axis=-1)
    cos, sin = _rope_tables(n_ctx, s - n_ctx, HEAD_DIM, 0, HEAD_DIM)
    shift = HEAD_DIM // 4
    qn = head_norm_rope(q, w["e_g_q"][None], jnp.tile(cos, (1, GQA_Q_HEADS)), jnp.tile(sin, (1, GQA_Q_HEADS)),
                        jnp.asarray(_rope_signs(HEAD_DIM, 0, HEAD_DIM, GQA_Q_HEADS)), GQA_Q_HEADS, shift)
    kn = head_norm_rope(k, w["e_g_k"][None], jnp.tile(cos, (1, GQA_KV_HEADS)), jnp.tile(sin, (1, GQA_KV_HEADS)),
                        jnp.asarray(_rope_signs(HEAD_DIM, 0, HEAD_DIM, GQA_KV_HEADS)), GQA_KV_HEADS, shift)
    att = gqa_attention(qn, kn, v, n_ctx)
    ys = []
    for dr in range(2):
        a_re, a_im, bb_re, bb_im = _s5_discretize(w["ssm_lam_re"][dr], w["ssm_lam_im"][dr], w["ssm_log_dt"][dr],
                                                  w["ssm_b_re"][dr], w["ssm_b_im"][dr])
        a_flat = jnp.stack([a_re.reshape(-1), a_im.reshape(-1)])
        b_bd = jnp.stack([_block_diag(jnp.swapaxes(bb_re, 1, 2)), _block_diag(jnp.swapaxes(bb_im, 1, 2))])
        c_bd = jnp.stack([_block_diag(jnp.swapaxes(w["ssm_c_re"][dr], 1, 2)),
                          _block_diag(jnp.swapaxes(w["ssm_c_im"][dr], 1, 2))])
        ys.append(s5_direction(u, a_flat, b_bd, c_bd, n_ctx, dr == 1))
    pre = _op(ops, _fn_glu_pre, "glu_pre", ("row", "row", "row", "glob"), (SSM_WIDTH,))
    post = _op(ops, _fn_glu_post, "glu_post", ("row", "row", "glob"), (SSM_WIDTH,))
    z = pre(u, ys[0], ys[1], w["ssm_d"][None])[0]
    t = linear(z.reshape(b * s, SSM_WIDTH), w["ssm_w_glu"]).reshape(b, s, SSM_WIDTH)
    ssm = post(z, t, w["ssm_b_glu"][None])[0]
    mix = jnp.concatenate([att, ssm], axis=-1)
    return linear(mix.reshape(b * s, -1), w["e_w_out"]).reshape(b, s, d)


def _odd_mixer(ops, a, w, n_ctx):
    b, s, d = a.shape
    w_in = jnp.pad(w["o_w_in"], ((0, 0), (0, ODD_IN_PAD - ODD_IN_W)))
    proj = linear(a.reshape(b * s, d), w_in).reshape(b, s, -1)
    c1 = MLA_Q_RANK
    c2 = c1 + MLA_KV_RANK
    c3 = c2 + MLA_ROPE
    cq, ckv, kr, nq, nk, nv, _ = jnp.split(proj, [c1, c2, c3, c3 + NA_W, c3 + 2 * NA_W, ODD_IN_W], axis=-1)
    nrm = lambda wd: _op(ops, _fn_norm, f"norm{wd}", ("row", "glob"), (wd,))
    cqn = nrm(MLA_Q_RANK)(cq, w["mla_g_cq"][None])[0]
    ckvn = nrm(MLA_KV_RANK)(ckv, w["mla_g_ckv"][None])[0]
    q = linear(cqn.reshape(b * s, -1), w["mla_w_uq"]).reshape(b, s, -1)
    kv = linear(ckvn.reshape(b * s, -1), w["mla_w_ukv"]).reshape(b, s, -1)
    cos, sin = _rope_tables(n_ctx, s - n_ctx, MLA_QK, MLA_NOPE, MLA_ROPE)
    cos, sin = jnp.tile(cos, (1, MLA_HEADS)), jnp.tile(sin, (1, MLA_HEADS))
    sg = jnp.asarray(_rope_signs(MLA_QK, MLA_NOPE, MLA_ROPE, MLA_HEADS))
    mq = head_norm_rope(q, w["mla_g_q"][None], cos, sin, sg, MLA_HEADS, MLA_ROPE // 4)
    mk = mla_k_prep(kv, kr, w["mla_g_k"][None], cos, sin, sg)
    mla = mla_attention(mq, mk, kv, n_ctx)
    nqn = head_norm_rope(nq, w["na_g_q"][None], None, None, None, NA_HEADS, 0)
    nkn = head_norm_rope(nk, w["na_g_k"][None], None, None, None, NA_HEADS, 0)
    na = na_attention_tm(nqn, nkn, nv, na_bias_table(w["na_rpb"]), n_ctx)
    mix = jnp.concatenate([mla, na], axis=-1)
    return linear(mix.reshape(b * s, -1), w["o_w_out"]).reshape(b, s, d)


_EVEN_KEYS = ("e_w_in", "e_w_out", "e_g_q", "e_g_k", "ssm_lam_re", "ssm_lam_im", "ssm_log_dt", "ssm_b_re", "ssm_b_im",
              "ssm_c_re", "ssm_c_im", "ssm_d", "ssm_w_glu", "ssm_b_glu")
_ODD_KEYS = ("o_w_in", "o_w_out", "mla_g_cq", "mla_g_ckv", "mla_w_uq", "mla_w_ukv", "mla_g_q", "mla_g_k", "na_g_q",
             "na_g_k", "na_rpb")


def _trunk(x_all, mods, w, n_ctx):
    ops = {}
    depth = mods.shape[0]
    b, s, d = x_all.shape
    modulate = make_modulate(d, n_ctx)
    gated = make_gated_add(d, n_ctx)
    x = x_all
    for i in range(depth):
        j = i // 2
        m = [mods[i][:, :, r:r + 1, :] for r in range(N_MOD)]
        a, x = modulate(x, w["g_norm1"][i][None], m[0], m[1])
        if i % 2 == 0:
            o = _even_mixer(ops, a, {k: w[k][j] for k in _EVEN_KEYS}, n_ctx)
        else:
            o = _odd_mixer(ops, a, {k: w[k][j] for k in _ODD_KEYS}, n_ctx)
        x = gated(x, o, m[2])
        a2, x = modulate(x, w["g_norm2"][i][None], m[3], m[4])
        f = ffn(a2.reshape(b * s, d), w["w_ff1"][i], w["w_ff2"][i]).reshape(b, s, d)
        x = gated(x, f, m[5])
    return x[:, n_ctx:]


def local_step(x, ctx, mods, w, loss_target):
    n_ctx = ctx.shape[1]
    x_all = jnp.concatenate([ctx, x], axis=1)
    y, vjp = jax.vjp(lambda xa, md, ww: _trunk(xa, md, ww, n_ctx), x_all, mods, w)
    loss_tile, dy = _loss_head(y, loss_target)
    dx_all, dmods, dw = vjp(dy)
    return loss_tile[0, 0], dx_all[:, n_ctx:], dmods, dw


_SHARDED = (("w_ff1", 2), ("w_ff2", 1), ("e_w_in", 2), ("e_w_out", 1), ("o_w_in", 2), ("o_w_out", 1),
            ("mla_w_uq", 2), ("mla_w_ukv", 2), ("ssm_w_glu", 1))
_SHARDED_SMALL = (("mla_g_cq", 1), ("mla_g_ckv", 1))
_REPLICATED = ("g_norm1", "g_norm2", "e_g_q", "e_g_k", "ssm_lam_re", "ssm_lam_im", "ssm_log_dt", "ssm_b_re", "ssm_b_im",
               "ssm_c_re", "ssm_c_im", "ssm_d", "ssm_b_glu", "mla_g_q", "mla_g_k", "na_g_q", "na_g_k", "na_rpb")
_WEIGHTS = ("c_ctx", "w_mod", "b_mod", "g_norm1", "g_norm2", "w_ff1", "w_ff2", "e_w_in", "e_w_out", "e_g_q", "e_g_k",
            "ssm_lam_re", "ssm_lam_im", "ssm_log_dt", "ssm_b_re", "ssm_b_im", "ssm_c_re", "ssm_c_im", "ssm_d",
            "ssm_w_glu", "ssm_b_glu", "o_w_in", "o_w_out", "mla_g_cq", "mla_g_ckv", "mla_w_uq", "mla_w_ukv", "mla_g_q",
            "mla_g_k", "na_g_q", "na_g_k", "na_rpb")
_PACK_ROWS = 64


def _pack(arrs, dtype, cols=1024, row_mult=_PACK_ROWS):
    blocks, tail, off = [], [], 0
    for a in arrs:
        n = int(np.prod(a.shape))
        if not tail and off % cols == 0 and n % cols == 0:
            blocks.append(a.astype(dtype).reshape(-1, cols))
        else:
            tail.append(a.astype(dtype).reshape(-1))
        off += n
    rows = -(-off // cols)
    pad = (-rows) % row_mult * cols + rows * cols - off
    if tail or pad:
        blocks.append(jnp.concatenate(tail + [jnp.zeros((pad,), dtype)]).reshape(-1, cols))
    return jnp.concatenate(blocks, axis=0)


def _unpack(packed, shapes):
    cols = packed.shape[-1]
    packed = packed.reshape(-1, cols)
    out, off = [], 0
    for sh in shapes:
        n = int(np.prod(sh))
        if off % cols == 0 and n % cols == 0:
            out.append(packed[off // cols:(off + n) // cols].reshape(sh))
        else:
            r0, r1 = off // cols, -(-(off + n) // cols)
            out.append(packed[r0:r1].reshape(-1)[off - r0 * cols:off - r0 * cols + n].reshape(sh))
        off += n
    return out


def kernel(x, c, ctx, c_ctx, w_mod, b_mod, g_norm1, g_norm2, w_ff1, w_ff2, e_w_in, e_w_out, e_g_q, e_g_k, ssm_lam_re, ssm_lam_im, ssm_log_dt, ssm_b_re, ssm_b_im, ssm_c_re, ssm_c_im, ssm_d, ssm_w_glu, ssm_b_glu, o_w_in, o_w_out, mla_g_cq, mla_g_ckv, mla_w_uq, mla_w_ukv, mla_g_q, mla_g_k, na_g_q, na_g_k, na_rpb, loss_target, m_c_ctx, m_w_mod, m_b_mod, m_g_norm1, m_g_norm2, m_w_ff1, m_w_ff2, m_e_w_in, m_e_w_out, m_e_g_q, m_e_g_k, m_ssm_lam_re, m_ssm_lam_im, m_ssm_log_dt, m_ssm_b_re, m_ssm_b_im, m_ssm_c_re, m_ssm_c_im, m_ssm_d, m_ssm_w_glu, m_ssm_b_glu, m_o_w_in, m_o_w_out, m_mla_g_cq, m_mla_g_ckv, m_mla_w_uq, m_mla_w_ukv, m_mla_g_q, m_mla_g_k, m_na_g_q, m_na_g_k, m_na_rpb, v_c_ctx, v_w_mod, v_b_mod, v_g_norm1, v_g_norm2, v_w_ff1, v_w_ff2, v_e_w_in, v_e_w_out, v_e_g_q, v_e_g_k, v_ssm_lam_re, v_ssm_lam_im, v_ssm_log_dt, v_ssm_b_re, v_ssm_b_im, v_ssm_c_re, v_ssm_c_im, v_ssm_d, v_ssm_w_glu, v_ssm_b_glu, v_o_w_in, v_o_w_out, v_mla_g_cq, v_mla_g_ckv, v_mla_w_uq, v_mla_w_ukv, v_mla_g_q, v_mla_g_k, v_na_g_q, v_na_g_k, v_na_rpb):
    env = dict(locals())
    weights = {n: env[n] for n in _WEIGHTS}
    mom_m = {n: env["m_" + n] for n in _WEIGHTS}
    mom_v = {n: env["v_" + n] for n in _WEIGHTS}
    ax, ay, ac = _me()
    plane = 2 * ax + ay
    dev = 4 * ax + 2 * ay + ac
    b_loc, d = c.shape
    depth = w_mod.shape[0]
    n_all = N_DEV * b_loc
    mod_cols = w_mod.shape[2]

    big = _pack([weights[n] for n, _ in _SHARDED], BF16)
    small = _pack([weights[n] for n, _ in _SHARDED_SMALL], F32, cols=LANE, row_mult=8)
    g_big, g_small = plane_allgather(big, small)
    full = {n: weights[n] for n in _REPLICATED}
    parts = [_unpack(g_big[j], [weights[n].shape for n, _ in _SHARDED]) for j in range(N_PLANE)]
    for t, (n, axis) in enumerate(_SHARDED):
        full[n] = [jnp.concatenate([parts[j][t][l] for j in range(N_PLANE)], axis=axis - 1)
                   for l in range(weights[n].shape[0])]
    parts_s = [_unpack(g_small[j], [weights[n].shape for n, _ in _SHARDED_SMALL]) for j in range(N_PLANE)]
    for t, (n, axis) in enumerate(_SHARDED_SMALL):
        full[n] = jnp.concatenate([parts_s[j][t] for j in range(N_PLANE)], axis=axis)

    rows_pad = 8 * ((n_all + 1 + 7) // 8)
    c_all = allgather8(jnp.pad(c, ((0, 8 - b_loc), (0, 0)))).reshape(N_DEV, 8, d)[:, :b_loc].reshape(n_all, d)
    cond_raw = jnp.concatenate([c_all, c_ctx[None], jnp.zeros((rows_pad - n_all - 1, d), F32)], axis=0)
    b_cols = lax.dynamic_slice_in_dim(b_mod, plane * mod_cols, mod_cols, axis=1)
    mod_loc = jnp.stack([_mm(cond_raw, w_mod[i], a_act="silu") + b_cols[i][None] for i in range(depth)])
    mod_g = allgather8(mod_loc.reshape(depth * rows_pad, mod_cols)).reshape(N_PLANE, 2, depth, rows_pad, mod_cols)
    mod_all = jnp.concatenate([mod_g[j, 0] for j in range(N_PLANE)], axis=-1)
    m_lat = lax.dynamic_slice_in_dim(mod_all, dev * b_loc, b_loc, axis=1)
    m_ctx = jnp.broadcast_to(mod_all[:, n_all][:, None], m_lat.shape)
    mods = jnp.stack([m_ctx, m_lat], axis=2).reshape(depth, b_loc, 2, N_MOD, d)

    loss_part, grad_x, dmods, dw = local_step(x, ctx, mods, full, loss_target)

    dm = dmods.reshape(depth, b_loc, 2, N_MOD * d)
    dm_rows = jnp.concatenate([dm[:, :, 1], jnp.sum(dm[:, :, 0], axis=1, keepdims=True)], axis=1)
    rep_shapes = [weights[n].shape for n in _REPLICATED] + [(1,)]
    small_pack = _pack([dm_rows] + [dw[n] for n in _REPLICATED] + [loss_part.reshape(1)], F32, cols=1024, row_mult=8)
    sp_rows = small_pack.shape[0]
    gathered = allgather8(small_pack).reshape(N_DEV, sp_rows, 1024)
    n_dm = depth * (b_loc + 1) * N_MOD * d
    dm_all = gathered.reshape(N_DEV, -1)[:, :n_dm].reshape(N_DEV, depth, b_loc + 1, N_MOD * d)
    rep_sum = _sum_rows(gathered, N_DEV).reshape(-1)
    rep_parts = _unpack(rep_sum[n_dm:], rep_shapes)
    rep_grads = dict(zip(_REPLICATED, rep_parts[:-1]))
    loss = rep_parts[-1][0]
    d_ctx_row = rep_sum[:n_dm].reshape(depth, b_loc + 1, N_MOD * d)[:, b_loc]
    d_lat_rows = jnp.transpose(dm_all[:, :, :b_loc], (1, 0, 2, 3)).reshape(depth, n_all, N_MOD * d)
    d_mod_all = jnp.concatenate([d_lat_rows, d_ctx_row[:, None],
                                 jnp.zeros((depth, rows_pad - n_all - 1, N_MOD * d), F32)], axis=1)
    grads = dict(rep_grads)
    grads["b_mod"] = jnp.sum(d_mod_all, axis=1)
    d_cols = lax.dynamic_slice_in_dim(d_mod_all, plane * mod_cols, mod_cols, axis=2)
    grads["w_mod"] = jnp.stack([_mm(cond_raw, d_cols[i], ta=True, a_act="silu") for i in range(depth)])
    d_cond = _mm(d_cols[0], w_mod[0], tb=True)
    for i in range(1, depth):
        d_cond = _add2(d_cond, _mm(d_cols[i], w_mod[i], tb=True))
    d_cond_g = allgather8(d_cond[n_all:n_all + 8] if rows_pad - n_all >= 8 else
                          jnp.pad(d_cond[n_all:], ((0, 8 - (rows_pad - n_all)), (0, 0)))).reshape(N_PLANE, 2, 8, d)
    d_silu = _sum_rows(d_cond_g[:, 0], N_PLANE)[0]
    sg = jax.nn.sigmoid(c_ctx)
    grads["c_ctx"] = d_silu * (sg * (1.0 + c_ctx * (1.0 - sg)))

    def shards_of(g, axis, j):
        layers = g if isinstance(g, (list, tuple)) else [g]
        ax = axis - 1 if isinstance(g, (list, tuple)) else axis
        n = layers[0].shape[ax] // N_PLANE
        return [lax.slice_in_dim(t, j * n, (j + 1) * n, axis=ax) for t in layers]

    send = jnp.stack([_pack([t for n, axis in _SHARDED + _SHARDED_SMALL for t in shards_of(dw[n], axis, j)], BF16)
                      for j in range(N_PLANE)])
    rows_h = send.shape[1] // 2
    send = send.reshape(N_PLANE, 2, rows_h, 1024)
    mine = lax.dynamic_index_in_dim(send, ac, 1, keepdims=False).reshape(N_PLANE * rows_h, 1024)
    theirs = sibling_halves(send).reshape(N_PLANE * rows_h, 1024)
    chip_sum = _accumulate([mine, theirs], BF16).reshape(N_PLANE, rows_h, 1024)
    pick = lambda k: lax.dynamic_index_in_dim(chip_sum, k, 0, keepdims=False)
    own, for_x, for_y, for_diag = pick(plane), pick(plane ^ 2), pick(plane ^ 1), pick(plane ^ 3)
    rows_q = rows_h // 2
    from_x, from_y = neighbour_exchange(for_diag[:rows_q], for_diag[rows_q:])
    zeros_q = jnp.zeros((rows_q, 1024), BF16)
    relayed = jnp.concatenate([zeros_q, from_y, from_x, zeros_q])
    merged = _accumulate([jnp.concatenate([for_x, for_y]), relayed], BF16)
    got_x, got_y = neighbour_exchange(merged[:rows_h], merged[rows_h:])
    done = _accumulate([own, got_x, got_y], BF16)
    both = jnp.stack([done, sibling_swap(done)])
    flat = jnp.where(ac == 0, both, both[::-1]).astype(F32).reshape(-1, 1024)
    shard_shapes = [weights[n].shape for n, _ in _SHARDED] + [weights[n].shape for n, _ in _SHARDED_SMALL]
    for (n, _), g in zip(_SHARDED + _SHARDED_SMALL, _unpack(flat, shard_shapes)):
        grads[n] = g

    big_names = ("w_mod",) + tuple(n for n, _ in _SHARDED)
    small_names = tuple(n for n in _WEIGHTS if n not in big_names)
    delta, new_m, new_v = {}, {}, {}
    for n in big_names:
        delta[n], new_m[n], new_v[n] = _adamw(weights[n], grads[n], mom_m[n], mom_v[n])
    sm_shapes = [weights[n].shape for n in small_names]
    packed = [_pack([src[n] for n in small_names], F32, cols=1024, row_mult=8)
              for src in (weights, grads, mom_m, mom_v)]
    for dst, res in zip((delta, new_m, new_v), _adamw(*packed)):
        dst.update(dict(zip(small_names, _unpack(res, sm_shapes))))

    return (loss, grad_x, *[grads[n] for n in _WEIGHTS], *[delta[n] for n in _WEIGHTS],
            *[new_m[n] for n in _WEIGHTS], *[new_v[n] for n in _WEIGHTS])
```
